```python
import jax, jax.numpy as jnp
from jax import lax
import numpy as np

D_MODEL = 1024
BATCH = 8
SEQ = 4096
DEPTH = 2

D_MIX = 512
GMLP_GROUPS = 8
GMLP_CHUNK = 128
GMLP_GROUP_DIM = D_MIX // GMLP_GROUPS
CONV_WIDTH = 31
FOX_HEADS = 8
FOX_HEAD_DIM = D_MIX // FOX_HEADS
QBLK = 128
N_BRANCH = 3
D_FF = 4 * D_MODEL
N_MOD = 6
NORM_EPS = 1e-6

COL_GMLP = 2 * D_MIX
COL_CONV = 2 * D_MIX
COL_FOX = 3 * D_MIX + FOX_HEADS
COL_GATE = N_BRANCH * D_MODEL
D_IN = COL_GMLP + COL_CONV + COL_FOX + COL_GATE
SPLIT_IDX = (COL_GMLP,
             COL_GMLP + COL_CONV,
             COL_GMLP + COL_CONV + D_MIX,
             COL_GMLP + COL_CONV + 2 * D_MIX,
             COL_GMLP + COL_CONV + 3 * D_MIX,
             COL_GMLP + COL_CONV + 3 * D_MIX + FOX_HEADS)

kernel_name = "hybrid_gmlp_conformer_fox_sandwich_adaln"


def rms_norm(x, g):
    xf = x.astype(jnp.float32)
    y = xf * lax.rsqrt(jnp.mean(xf * xf, axis=-1, keepdims=True) + NORM_EPS)
    return (y * g.astype(jnp.float32)).astype(x.dtype)


def layer_norm(x, g, b):
    xf = x.astype(jnp.float32)
    mu = jnp.mean(xf, axis=-1, keepdims=True)
    var = jnp.mean(jnp.square(xf - mu), axis=-1, keepdims=True)
    y = (xf - mu) * lax.rsqrt(var + NORM_EPS)
    return (y * g.astype(jnp.float32) + b.astype(jnp.float32)).astype(x.dtype)


def gmlp_spatial(v, ws, bs):
    b, s, _ = v.shape
    nc = s // GMLP_CHUNK
    vc = v.reshape(b, nc, GMLP_CHUNK, GMLP_GROUPS, GMLP_GROUP_DIM)
    w = ws * jnp.tril(jnp.ones((GMLP_CHUNK, GMLP_CHUNK), ws.dtype))
    sv = jnp.einsum('gts,bnsgc->bntgc', w, vc) + bs.T[None, None, :, :, None]
    return sv.reshape(b, s, D_MIX)


def causal_depthwise_conv(z, w, bias):
    out = lax.conv_general_dilated(
        z, w[:, None, :], window_strides=(1,), padding=[(CONV_WIDTH - 1, 0)],
        dimension_numbers=('NWC', 'WIO', 'NWC'), feature_group_count=D_MIX)
    return out + bias


def fox_attention(q, k, v, log_f):
    b, h, s, dh = q.shape
    nb = s // QBLK
    cum = jnp.cumsum(log_f.astype(jnp.float32), axis=-1)
    qb = jnp.moveaxis(q.reshape(b, h, nb, QBLK, dh), 2, 0)
    cq = jnp.moveaxis(cum.reshape(b, h, nb, QBLK), 2, 0)
    kpos = jnp.arange(s)
    scale = 1.0 / float(np.sqrt(dh))

    def one_block(args):
        qi, ci, i = args
        logits = jnp.einsum('bhqd,bhkd->bhqk', qi, k).astype(jnp.float32) * scale
        logits = logits + ci[..., None] - cum[:, :, None, :]
        qpos = i * QBLK + jnp.arange(QBLK)
        causal = kpos[None, :] <= qpos[:, None]
        logits = jnp.where(causal, logits, -jnp.inf)
        p = jax.nn.softmax(logits, axis=-1)
        return jnp.einsum('bhqk,bhkd->bhqd', p.astype(v.dtype), v)

    out = lax.map(one_block, (qb, cq, jnp.arange(nb)))
    return out.transpose(1, 0, 3, 2, 4).reshape(b, s, h * dh)


def hybrid_mixer(h, w_in, gmlp_ln_g, gmlp_ln_b, gmlp_ws, gmlp_bs, w_a_out,
                 conv_w, conv_b, conv_ln_g, conv_ln_b, w_b_out, fox_bf, w_c_out, w_out):
    b, s, _ = h.shape
    proj = h @ w_in
    uv_a, glu_b, q, k, v, f_raw, gate_raw = jnp.split(proj, SPLIT_IDX, axis=-1)

    u_a, v_a = jnp.split(jax.nn.gelu(uv_a), 2, axis=-1)
    v_a = layer_norm(v_a, gmlp_ln_g, gmlp_ln_b)
    y_a = (u_a * gmlp_spatial(v_a, gmlp_ws, gmlp_bs)) @ w_a_out

    val_b, gat_b = jnp.split(glu_b, 2, axis=-1)
    z = val_b * jax.nn.sigmoid(gat_b)
    z = causal_depthwise_conv(z, conv_w, conv_b)
    z = jax.nn.silu(layer_norm(z, conv_ln_g, conv_ln_b))
    y_b = z @ w_b_out

    def heads(t):
        return t.reshape(b, s, FOX_HEADS, FOX_HEAD_DIM).transpose(0, 2, 1, 3)
    log_f = jax.nn.log_sigmoid((f_raw + fox_bf).astype(jnp.float32)).transpose(0, 2, 1)
    y_c = fox_attention(heads(q), heads(k), heads(v), log_f) @ w_c_out

    g = jax.nn.sigmoid(gate_raw).reshape(b, s, N_BRANCH, D_MODEL)
    merged = g[:, :, 0] * y_a + g[:, :, 1] * y_b + g[:, :, 2] * y_c
    return merged @ w_out


def _fwd_setup_inputs(seed: int = 0) -> dict:
    key = jax.random.key(seed)
    ks = jax.random.split(key, 32)
    f32 = jnp.float32

    def nrm(k, shape, scale):
        return jax.random.normal(k, shape, f32) * scale

    L = DEPTH
    return {
        "x": nrm(ks[0], (BATCH, SEQ, D_MODEL), 1.0),
        "c": nrm(ks[1], (BATCH, D_MODEL), 1.0),
        "ada_w": nrm(ks[2], (L, D_MODEL, N_MOD * D_MODEL), 0.5 * D_MODEL ** -0.5),
        "ada_b": nrm(ks[3], (L, N_MOD * D_MODEL), 0.02),
        "mix_pre_g": 1.0 + nrm(ks[4], (L, D_MODEL), 0.02),
        "mix_post_g": 1.0 + nrm(ks[5], (L, D_MODEL), 0.02),
        "mlp_pre_g": 1.0 + nrm(ks[6], (L, D_MODEL), 0.02),
        "mlp_post_g": 1.0 + nrm(ks[7], (L, D_MODEL), 0.02),
        "w_in": nrm(ks[8], (L, D_MODEL, D_IN), D_MODEL ** -0.5),
        "gmlp_ln_g": 1.0 + nrm(ks[9], (L, D_MIX), 0.02),
        "gmlp_ln_b": nrm(ks[10], (L, D_MIX), 0.02),
        "gmlp_ws": nrm(ks[11], (L, GMLP_GROUPS, GMLP_CHUNK, GMLP_CHUNK), GMLP_CHUNK ** -0.5),
        "gmlp_bs": 1.0 + nrm(ks[12], (L, GMLP_GROUPS, GMLP_CHUNK), 0.02),
        "w_a_out": nrm(ks[13], (L, D_MIX, D_MODEL), D_MIX ** -0.5),
        "conv_w": nrm(ks[14], (L, CONV_WIDTH, D_MIX), CONV_WIDTH ** -0.5),
        "conv_b": nrm(ks[15], (L, D_MIX), 0.02),
        "conv_ln_g": 1.0 + nrm(ks[16], (L, D_MIX), 0.02),
        "conv_ln_b": nrm(ks[17], (L, D_MIX), 0.02),
        "w_b_out": nrm(ks[18], (L, D_MIX, D_MODEL), D_MIX ** -0.5),
        "fox_bf": jnp.linspace(1.0, 6.0, FOX_HEADS, dtype=f32)[None, :] + nrm(ks[19], (L, FOX_HEADS), 0.1),
        "w_c_out": nrm(ks[20], (L, D_MIX, D_MODEL), D_MIX ** -0.5),
        "w_out": nrm(ks[21], (L, D_MODEL, D_MODEL), D_MODEL ** -0.5),
        "mlp_w1": nrm(ks[22], (L, D_MODEL, D_FF), D_MODEL ** -0.5),
        "mlp_w2": nrm(ks[23], (L, D_FF, D_MODEL), D_FF ** -0.5),
    }


def _fwd_reference(x, c, ada_w, ada_b, mix_pre_g, mix_post_g, mlp_pre_g, mlp_post_g, w_in,
              gmlp_ln_g, gmlp_ln_b, gmlp_ws, gmlp_bs, w_a_out, conv_w, conv_b, conv_ln_g,
              conv_ln_b, w_b_out, fox_bf, w_c_out, w_out, mlp_w1, mlp_w2):
    c_act = jax.nn.silu(c)
    for l in range(DEPTH):
        mod = (c_act @ ada_w[l] + ada_b[l])[:, None, :]
        sh1, sc1, gt1, sh2, sc2, gt2 = jnp.split(mod, N_MOD, axis=-1)

        h = rms_norm(x, mix_pre_g[l]) * (1.0 + sc1) + sh1
        y = hybrid_mixer(h, w_in[l], gmlp_ln_g[l], gmlp_ln_b[l], gmlp_ws[l], gmlp_bs[l],
                         w_a_out[l], conv_w[l], conv_b[l], conv_ln_g[l], conv_ln_b[l],
                         w_b_out[l], fox_bf[l], w_c_out[l], w_out[l])
        x = x + gt1 * rms_norm(y, mix_post_g[l])

        h = rms_norm(x, mlp_pre_g[l]) * (1.0 + sc2) + sh2
        y = jnp.square(jax.nn.relu(h @ mlp_w1[l])) @ mlp_w2[l]
        x = x + gt2 * rms_norm(y, mlp_post_g[l])
    return x


import jax as _jax
import jax.numpy as _jnp

TWIN_FORMAT = 'train_step'
FWD_PARAMS = ['x', 'c', 'ada_w', 'ada_b', 'mix_pre_g', 'mix_post_g', 'mlp_pre_g', 'mlp_post_g', 'w_in', 'gmlp_ln_g', 'gmlp_ln_b', 'gmlp_ws', 'gmlp_bs', 'w_a_out', 'conv_w', 'conv_b', 'conv_ln_g', 'conv_ln_b', 'w_b_out', 'fox_bf', 'w_c_out', 'w_out', 'mlp_w1', 'mlp_w2']
TWIN_WEIGHTS = ['ada_w', 'ada_b', 'mix_pre_g', 'mix_post_g', 'mlp_pre_g', 'mlp_post_g', 'w_in', 'gmlp_ln_g', 'gmlp_ln_b', 'gmlp_ws', 'gmlp_bs', 'w_a_out', 'conv_w', 'conv_b', 'conv_ln_g', 'conv_ln_b', 'w_b_out', 'fox_bf', 'w_c_out', 'w_out', 'mlp_w1', 'mlp_w2']
TWIN_DIFF_INPUT = 'x'
TWIN_INPUTS = ['x', 'c', 'ada_w', 'ada_b', 'mix_pre_g', 'mix_post_g', 'mlp_pre_g', 'mlp_post_g', 'w_in', 'gmlp_ln_g', 'gmlp_ln_b', 'gmlp_ws', 'gmlp_bs', 'w_a_out', 'conv_w', 'conv_b', 'conv_ln_g', 'conv_ln_b', 'w_b_out', 'fox_bf', 'w_c_out', 'w_out', 'mlp_w1', 'mlp_w2', 'loss_target', 'm_ada_w', 'm_ada_b', 'm_mix_pre_g', 'm_mix_post_g', 'm_mlp_pre_g', 'm_mlp_post_g', 'm_w_in', 'm_gmlp_ln_g', 'm_gmlp_ln_b', 'm_gmlp_ws', 'm_gmlp_bs', 'm_w_a_out', 'm_conv_w', 'm_conv_b', 'm_conv_ln_g', 'm_conv_ln_b', 'm_w_b_out', 'm_fox_bf', 'm_w_c_out', 'm_w_out', 'm_mlp_w1', 'm_mlp_w2', 'v_ada_w', 'v_ada_b', 'v_mix_pre_g', 'v_mix_post_g', 'v_mlp_pre_g', 'v_mlp_post_g', 'v_w_in', 'v_gmlp_ln_g', 'v_gmlp_ln_b', 'v_gmlp_ws', 'v_gmlp_bs', 'v_w_a_out', 'v_conv_w', 'v_conv_b', 'v_conv_ln_g', 'v_conv_ln_b', 'v_w_b_out', 'v_fox_bf', 'v_w_c_out', 'v_w_out', 'v_mlp_w1', 'v_mlp_w2']
TWIN_OUTPUTS = ['loss', 'grad_x', 'grad_ada_w', 'grad_ada_b', 'grad_mix_pre_g', 'grad_mix_post_g', 'grad_mlp_pre_g', 'grad_mlp_post_g', 'grad_w_in', 'grad_gmlp_ln_g', 'grad_gmlp_ln_b', 'grad_gmlp_ws', 'grad_gmlp_bs', 'grad_w_a_out', 'grad_conv_w', 'grad_conv_b', 'grad_conv_ln_g', 'grad_conv_ln_b', 'grad_w_b_out', 'grad_fox_bf', 'grad_w_c_out', 'grad_w_out', 'grad_mlp_w1', 'grad_mlp_w2', 'delta_ada_w', 'delta_ada_b', 'delta_mix_pre_g', 'delta_mix_post_g', 'delta_mlp_pre_g', 'delta_mlp_post_g', 'delta_w_in', 'delta_gmlp_ln_g', 'delta_gmlp_ln_b', 'delta_gmlp_ws', 'delta_gmlp_bs', 'delta_w_a_out', 'delta_conv_w', 'delta_conv_b', 'delta_conv_ln_g', 'delta_conv_ln_b', 'delta_w_b_out', 'delta_fox_bf', 'delta_w_c_out', 'delta_w_out', 'delta_mlp_w1', 'delta_mlp_w2', 'new_m_ada_w', 'new_m_ada_b', 'new_m_mix_pre_g', 'new_m_mix_post_g', 'new_m_mlp_pre_g', 'new_m_mlp_post_g', 'new_m_w_in', 'new_m_gmlp_ln_g', 'new_m_gmlp_ln_b', 'new_m_gmlp_ws', 'new_m_gmlp_bs', 'new_m_w_a_out', 'new_m_conv_w', 'new_m_conv_b', 'new_m_conv_ln_g', 'new_m_conv_ln_b', 'new_m_w_b_out', 'new_m_fox_bf', 'new_m_w_c_out', 'new_m_w_out', 'new_m_mlp_w1', 'new_m_mlp_w2', 'new_v_ada_w', 'new_v_ada_b', 'new_v_mix_pre_g', 'new_v_mix_post_g', 'new_v_mlp_pre_g', 'new_v_mlp_post_g', 'new_v_w_in', 'new_v_gmlp_ln_g', 'new_v_gmlp_ln_b', 'new_v_gmlp_ws', 'new_v_gmlp_bs', 'new_v_w_a_out', 'new_v_conv_w', 'new_v_conv_b', 'new_v_conv_ln_g', 'new_v_conv_ln_b', 'new_v_w_b_out', 'new_v_fox_bf', 'new_v_w_c_out', 'new_v_w_out', 'new_v_mlp_w1', 'new_v_mlp_w2']
TWIN_LEAF_KINDS = {'loss': 'loss', 'grad_x': 'grad_x', 'grad_ada_w': 'grad_w', 'grad_ada_b': 'grad_w', 'grad_mix_pre_g': 'grad_w', 'grad_mix_post_g': 'grad_w', 'grad_mlp_pre_g': 'grad_w', 'grad_mlp_post_g': 'grad_w', 'grad_w_in': 'grad_w', 'grad_gmlp_ln_g': 'grad_w', 'grad_gmlp_ln_b': 'grad_w', 'grad_gmlp_ws': 'grad_w', 'grad_gmlp_bs': 'grad_w', 'grad_w_a_out': 'grad_w', 'grad_conv_w': 'grad_w', 'grad_conv_b': 'grad_w', 'grad_conv_ln_g': 'grad_w', 'grad_conv_ln_b': 'grad_w', 'grad_w_b_out': 'grad_w', 'grad_fox_bf': 'grad_w', 'grad_w_c_out': 'grad_w', 'grad_w_out': 'grad_w', 'grad_mlp_w1': 'grad_w', 'grad_mlp_w2': 'grad_w', 'delta_ada_w': 'delta_w', 'delta_ada_b': 'delta_w', 'delta_mix_pre_g': 'delta_w', 'delta_mix_post_g': 'delta_w', 'delta_mlp_pre_g': 'delta_w', 'delta_mlp_post_g': 'delta_w', 'delta_w_in': 'delta_w', 'delta_gmlp_ln_g': 'delta_w', 'delta_gmlp_ln_b': 'delta_w', 'delta_gmlp_ws': 'delta_w', 'delta_gmlp_bs': 'delta_w', 'delta_w_a_out': 'delta_w', 'delta_conv_w': 'delta_w', 'delta_conv_b': 'delta_w', 'delta_conv_ln_g': 'delta_w', 'delta_conv_ln_b': 'delta_w', 'delta_w_b_out': 'delta_w', 'delta_fox_bf': 'delta_w', 'delta_w_c_out': 'delta_w', 'delta_w_out': 'delta_w', 'delta_mlp_w1': 'delta_w', 'delta_mlp_w2': 'delta_w', 'new_m_ada_w': 'new_m', 'new_m_ada_b': 'new_m', 'new_m_mix_pre_g': 'new_m', 'new_m_mix_post_g': 'new_m', 'new_m_mlp_pre_g': 'new_m', 'new_m_mlp_post_g': 'new_m', 'new_m_w_in': 'new_m', 'new_m_gmlp_ln_g': 'new_m', 'new_m_gmlp_ln_b': 'new_m', 'new_m_gmlp_ws': 'new_m', 'new_m_gmlp_bs': 'new_m', 'new_m_w_a_out': 'new_m', 'new_m_conv_w': 'new_m', 'new_m_conv_b': 'new_m', 'new_m_conv_ln_g': 'new_m', 'new_m_conv_ln_b': 'new_m', 'new_m_w_b_out': 'new_m', 'new_m_fox_bf': 'new_m', 'new_m_w_c_out': 'new_m', 'new_m_w_out': 'new_m', 'new_m_mlp_w1': 'new_m', 'new_m_mlp_w2': 'new_m', 'new_v_ada_w': 'new_v', 'new_v_ada_b': 'new_v', 'new_v_mix_pre_g': 'new_v', 'new_v_mix_post_g': 'new_v', 'new_v_mlp_pre_g': 'new_v', 'new_v_mlp_post_g': 'new_v', 'new_v_w_in': 'new_v', 'new_v_gmlp_ln_g': 'new_v', 'new_v_gmlp_ln_b': 'new_v', 'new_v_gmlp_ws': 'new_v', 'new_v_gmlp_bs': 'new_v', 'new_v_w_a_out': 'new_v', 'new_v_conv_w': 'new_v', 'new_v_conv_b': 'new_v', 'new_v_conv_ln_g': 'new_v', 'new_v_conv_ln_b': 'new_v', 'new_v_w_b_out': 'new_v', 'new_v_fox_bf': 'new_v', 'new_v_w_c_out': 'new_v', 'new_v_w_out': 'new_v', 'new_v_mlp_w1': 'new_v', 'new_v_mlp_w2': 'new_v'}


def _forward(args):
    return _fwd_reference(*[args[k] for k in FWD_PARAMS])


def _output_shape():
    out = _jax.eval_shape(lambda: _forward(_fwd_setup_inputs(0)))
    return out.shape, out.dtype

N_MICROBATCH = 1
ADAM_LR = 0.001
ADAM_B1 = 0.9
ADAM_B2 = 0.999
ADAM_EPS = 1e-08
ADAM_WD = 0.01
ADAM_STEP = 10
PER_EXAMPLE_BATCH_AXIS = {'x': 0, 'c': 0, 'loss_target': 0}
SHARED_INPUTS = []
_WEIGHT_DTYPES = {'ada_w': _jnp.float32, 'ada_b': _jnp.float32, 'mix_pre_g': _jnp.float32, 'mix_post_g': _jnp.float32, 'mlp_pre_g': _jnp.float32, 'mlp_post_g': _jnp.float32, 'w_in': _jnp.float32, 'gmlp_ln_g': _jnp.float32, 'gmlp_ln_b': _jnp.float32, 'gmlp_ws': _jnp.float32, 'gmlp_bs': _jnp.float32, 'w_a_out': _jnp.float32, 'conv_w': _jnp.float32, 'conv_b': _jnp.float32, 'conv_ln_g': _jnp.float32, 'conv_ln_b': _jnp.float32, 'w_b_out': _jnp.float32, 'fox_bf': _jnp.float32, 'w_c_out': _jnp.float32, 'w_out': _jnp.float32, 'mlp_w1': _jnp.float32, 'mlp_w2': _jnp.float32}
MOMENT_SCALE = {'ada_w': 1.988166e+00, 'ada_b': 3.678917e+00, 'mix_pre_g': 1.713003e-01, 'mix_post_g': 3.889276e+00, 'mlp_pre_g': 1.747265e-01, 'mlp_post_g': 3.922181e+00, 'w_in': 1.650432e-01, 'gmlp_ln_g': 5.817045e-02, 'gmlp_ln_b': 6.738268e-02, 'gmlp_ws': 4.049982e-02, 'gmlp_bs': 6.097754e-02, 'w_a_out': 4.057699e-01, 'conv_w': 2.198960e-01, 'conv_b': 1.363456e+00, 'conv_ln_g': 6.735266e-01, 'conv_ln_b': 9.139035e-01, 'w_b_out': 2.921175e-01, 'fox_bf': 1.651540e-01, 'w_c_out': 3.978482e-01, 'w_out': 6.404344e-01, 'mlp_w1': 1.493703e-01, 'mlp_w2': 6.599383e-01}


def _to_microbatches(a, axis):
    t = _jnp.moveaxis(a, axis, 0)
    t = t.reshape((N_MICROBATCH, t.shape[0] // N_MICROBATCH) + t.shape[1:])
    return _jnp.moveaxis(t, 1, axis + 1)


def setup_inputs(seed: int = 0) -> dict:
    inp = _fwd_setup_inputs(seed)
    key = _jax.random.fold_in(_jax.random.key(seed), 7919)
    shape, _ = _output_shape()
    out = dict(inp)
    out["loss_target"] = _jax.random.normal(_jax.random.fold_in(key, 0), shape, _jnp.float32)
    for i, name in enumerate(TWIN_WEIGHTS):
        w = inp[name].astype(_jnp.float32)
        if MOMENT_SCALE is None:
            s = _jnp.sqrt(_jnp.mean(_jnp.square(w)) + 1e-30)
        else:
            s = MOMENT_SCALE[name]
        km, kv = _jax.random.split(_jax.random.fold_in(key, i + 1))
        out[name] = w
        out["m_" + name] = s * _jax.random.normal(km, w.shape, _jnp.float32)
        out["v_" + name] = (s * s) * _jax.random.uniform(kv, w.shape, _jnp.float32, 0.5, 1.5)
    if N_MICROBATCH > 1:
        for name, axis in PER_EXAMPLE_BATCH_AXIS.items():
            out[name] = _to_microbatches(out[name], axis)
    return {'x': out['x'], 'c': out['c'], 'ada_w': out['ada_w'], 'ada_b': out['ada_b'], 'mix_pre_g': out['mix_pre_g'], 'mix_post_g': out['mix_post_g'], 'mlp_pre_g': out['mlp_pre_g'], 'mlp_post_g': out['mlp_post_g'], 'w_in': out['w_in'], 'gmlp_ln_g': out['gmlp_ln_g'], 'gmlp_ln_b': out['gmlp_ln_b'], 'gmlp_ws': out['gmlp_ws'], 'gmlp_bs': out['gmlp_bs'], 'w_a_out': out['w_a_out'], 'conv_w': out['conv_w'], 'conv_b': out['conv_b'], 'conv_ln_g': out['conv_ln_g'], 'conv_ln_b': out['conv_ln_b'], 'w_b_out': out['w_b_out'], 'fox_bf': out['fox_bf'], 'w_c_out': out['w_c_out'], 'w_out': out['w_out'], 'mlp_w1': out['mlp_w1'], 'mlp_w2': out['mlp_w2'], 'loss_target': out['loss_target'], 'm_ada_w': out['m_ada_w'], 'm_ada_b': out['m_ada_b'], 'm_mix_pre_g': out['m_mix_pre_g'], 'm_mix_post_g': out['m_mix_post_g'], 'm_mlp_pre_g': out['m_mlp_pre_g'], 'm_mlp_post_g': out['m_mlp_post_g'], 'm_w_in': out['m_w_in'], 'm_gmlp_ln_g': out['m_gmlp_ln_g'], 'm_gmlp_ln_b': out['m_gmlp_ln_b'], 'm_gmlp_ws': out['m_gmlp_ws'], 'm_gmlp_bs': out['m_gmlp_bs'], 'm_w_a_out': out['m_w_a_out'], 'm_conv_w': out['m_conv_w'], 'm_conv_b': out['m_conv_b'], 'm_conv_ln_g': out['m_conv_ln_g'], 'm_conv_ln_b': out['m_conv_ln_b'], 'm_w_b_out': out['m_w_b_out'], 'm_fox_bf': out['m_fox_bf'], 'm_w_c_out': out['m_w_c_out'], 'm_w_out': out['m_w_out'], 'm_mlp_w1': out['m_mlp_w1'], 'm_mlp_w2': out['m_mlp_w2'], 'v_ada_w': out['v_ada_w'], 'v_ada_b': out['v_ada_b'], 'v_mix_pre_g': out['v_mix_pre_g'], 'v_mix_post_g': out['v_mix_post_g'], 'v_mlp_pre_g': out['v_mlp_pre_g'], 'v_mlp_post_g': out['v_mlp_post_g'], 'v_w_in': out['v_w_in'], 'v_gmlp_ln_g': out['v_gmlp_ln_g'], 'v_gmlp_ln_b': out['v_gmlp_ln_b'], 'v_gmlp_ws': out['v_gmlp_ws'], 'v_gmlp_bs': out['v_gmlp_bs'], 'v_w_a_out': out['v_w_a_out'], 'v_conv_w': out['v_conv_w'], 'v_conv_b': out['v_conv_b'], 'v_conv_ln_g': out['v_conv_ln_g'], 'v_conv_ln_b': out['v_conv_ln_b'], 'v_w_b_out': out['v_w_b_out'], 'v_fox_bf': out['v_fox_bf'], 'v_w_c_out': out['v_w_c_out'], 'v_w_out': out['v_w_out'], 'v_mlp_w1': out['v_mlp_w1'], 'v_mlp_w2': out['v_mlp_w2']}


def _loss(weights, diff, rest, loss_target):
    with _jax.named_scope("forward"):
        args = {**rest, TWIN_DIFF_INPUT: diff, **{k: w.astype(_WEIGHT_DTYPES[k]) for k, w in weights.items()}}
        y = _forward(args)
    with _jax.named_scope("loss_head"):
        err = _jnp.square(y.astype(_jnp.float32) - loss_target)
        return 0.5 * _jnp.sum(_jnp.mean(err, axis=-1)) if err.ndim else 0.5 * err


def _adamw(w, g, m, v):
    m = ADAM_B1 * m + (1.0 - ADAM_B1) * g
    v = ADAM_B2 * v + (1.0 - ADAM_B2) * _jnp.square(g)
    m_hat = m / (1.0 - ADAM_B1 ** ADAM_STEP)
    v_hat = v / (1.0 - ADAM_B2 ** ADAM_STEP)
    delta = -ADAM_LR * (m_hat / (_jnp.sqrt(v_hat) + ADAM_EPS) + ADAM_WD * w)
    return delta, m, v


def reference(x, c, ada_w, ada_b, mix_pre_g, mix_post_g, mlp_pre_g, mlp_post_g, w_in, gmlp_ln_g, gmlp_ln_b, gmlp_ws, gmlp_bs, w_a_out, conv_w, conv_b, conv_ln_g, conv_ln_b, w_b_out, fox_bf, w_c_out, w_out, mlp_w1, mlp_w2, loss_target, m_ada_w, m_ada_b, m_mix_pre_g, m_mix_post_g, m_mlp_pre_g, m_mlp_post_g, m_w_in, m_gmlp_ln_g, m_gmlp_ln_b, m_gmlp_ws, m_gmlp_bs, m_w_a_out, m_conv_w, m_conv_b, m_conv_ln_g, m_conv_ln_b, m_w_b_out, m_fox_bf, m_w_c_out, m_w_out, m_mlp_w1, m_mlp_w2, v_ada_w, v_ada_b, v_mix_pre_g, v_mix_post_g, v_mlp_pre_g, v_mlp_post_g, v_w_in, v_gmlp_ln_g, v_gmlp_ln_b, v_gmlp_ws, v_gmlp_bs, v_w_a_out, v_conv_w, v_conv_b, v_conv_ln_g, v_conv_ln_b, v_w_b_out, v_fox_bf, v_w_c_out, v_w_out, v_mlp_w1, v_mlp_w2):
    given = dict(x=x, c=c, ada_w=ada_w, ada_b=ada_b, mix_pre_g=mix_pre_g, mix_post_g=mix_post_g, mlp_pre_g=mlp_pre_g, mlp_post_g=mlp_post_g, w_in=w_in, gmlp_ln_g=gmlp_ln_g, gmlp_ln_b=gmlp_ln_b, gmlp_ws=gmlp_ws, gmlp_bs=gmlp_bs, w_a_out=w_a_out, conv_w=conv_w, conv_b=conv_b, conv_ln_g=conv_ln_g, conv_ln_b=conv_ln_b, w_b_out=w_b_out, fox_bf=fox_bf, w_c_out=w_c_out, w_out=w_out, mlp_w1=mlp_w1, mlp_w2=mlp_w2, loss_target=loss_target, m_ada_w=m_ada_w, m_ada_b=m_ada_b, m_mix_pre_g=m_mix_pre_g, m_mix_post_g=m_mix_post_g, m_mlp_pre_g=m_mlp_pre_g, m_mlp_post_g=m_mlp_post_g, m_w_in=m_w_in, m_gmlp_ln_g=m_gmlp_ln_g, m_gmlp_ln_b=m_gmlp_ln_b, m_gmlp_ws=m_gmlp_ws, m_gmlp_bs=m_gmlp_bs, m_w_a_out=m_w_a_out, m_conv_w=m_conv_w, m_conv_b=m_conv_b, m_conv_ln_g=m_conv_ln_g, m_conv_ln_b=m_conv_ln_b, m_w_b_out=m_w_b_out, m_fox_bf=m_fox_bf, m_w_c_out=m_w_c_out, m_w_out=m_w_out, m_mlp_w1=m_mlp_w1, m_mlp_w2=m_mlp_w2, v_ada_w=v_ada_w, v_ada_b=v_ada_b, v_mix_pre_g=v_mix_pre_g, v_mix_post_g=v_mix_post_g, v_mlp_pre_g=v_mlp_pre_g, v_mlp_post_g=v_mlp_post_g, v_w_in=v_w_in, v_gmlp_ln_g=v_gmlp_ln_g, v_gmlp_ln_b=v_gmlp_ln_b, v_gmlp_ws=v_gmlp_ws, v_gmlp_bs=v_gmlp_bs, v_w_a_out=v_w_a_out, v_conv_w=v_conv_w, v_conv_b=v_conv_b, v_conv_ln_g=v_conv_ln_g, v_conv_ln_b=v_conv_ln_b, v_w_b_out=v_w_b_out, v_fox_bf=v_fox_bf, v_w_c_out=v_w_c_out, v_w_out=v_w_out, v_mlp_w1=v_mlp_w1, v_mlp_w2=v_mlp_w2)
    weights = {n: given[n] for n in TWIN_WEIGHTS}
    shared = {n: given[n] for n in SHARED_INPUTS}
    per_example = {n: given[n] for n in ['x', 'c']}
    grad_fn = _jax.value_and_grad(_loss, argnums=(0, 1))

    def one_microbatch(ex, loss_target):
        ex = dict(ex)
        diff = ex.pop(TWIN_DIFF_INPUT)
        return grad_fn(weights, diff, {**shared, **ex}, loss_target)

    if N_MICROBATCH == 1:
        loss, (grad_w, grad_x) = one_microbatch(per_example, given["loss_target"])
    else:
        def body(carry, xs):
            loss_sum, grad_sum = carry
            l_k, (gw_k, gx_k) = one_microbatch(xs[0], xs[1])
            with _jax.named_scope("update"):
                return (loss_sum + l_k, _jax.tree.map(_jnp.add, grad_sum, gw_k)), gx_k

        init = (_jnp.zeros((), _jnp.float32), _jax.tree.map(_jnp.zeros_like, weights))
        (loss, grad_w), grad_x = _jax.lax.scan(body, init, (per_example, given["loss_target"]))
    with _jax.named_scope("update"):
        delta_w, new_m, new_v = {}, {}, {}
        for n in TWIN_WEIGHTS:
            delta_w[n], new_m[n], new_v[n] = _adamw(weights[n], grad_w[n], given["m_" + n], given["v_" + n])
    return (loss, grad_x, *[grad_w[n] for n in TWIN_WEIGHTS], *[delta_w[n] for n in TWIN_WEIGHTS],
            *[new_m[n] for n in TWIN_WEIGHTS], *[new_v[n] for n in TWIN_WEIGHTS])
```

```python
import functools
import math

import jax
import jax.numpy as jnp
from jax import lax
from jax.experimental import pallas as pl
from jax.experimental.pallas import tpu as pltpu

F32 = jnp.float32
BF16 = jnp.bfloat16
MESH = pl.DeviceIdType.MESH
N_DEV = 8
NORM_EPS = 1e-6
D_MIX = 512
N_HEADS = 8
HEAD_DIM = 64
GROUP_DIM = 64
CHUNK = 128
CONV_WIDTH = 31
CONV_HALO = 32
LANES = 128
ADAM_LR, ADAM_B1, ADAM_B2, ADAM_EPS, ADAM_WD, ADAM_STEP = 0.001, 0.9, 0.999, 1e-08, 0.01, 10
VMEM_LIMIT = 56 * 1024 * 1024
HIGHEST = lax.Precision.HIGHEST


def _tile(dim, pref, mult=LANES):
    t = min(pref, dim)
    t -= t % mult
    while t >= mult:
        if dim % t == 0:
            return t
        t -= mult
    return dim


def _params(sem):
    return pltpu.CompilerParams(dimension_semantics=sem, vmem_limit_bytes=VMEM_LIMIT)


def rowwise(name, fn, rows, consts, outs, accs=(), ts=256):
    s = rows[0][0].shape[0]
    ts = min(ts, s)
    nr, nc, no, na = len(rows), len(consts), len(outs), len(accs)

    def body(*refs):
        vals = [r[...] for r in refs[:nr + nc]]
        res = fn(*vals)
        if not isinstance(res, (tuple, list)):
            res = (res,)
        for r, v in zip(refs[nr + nc:nr + nc + no], res[:no]):
            r[...] = v.astype(r.dtype)
        if na:
            acc_refs = refs[nr + nc + no:]

            @pl.when(pl.program_id(0) == 0)
            def _():
                for r in acc_refs:
                    r[...] = jnp.zeros(r.shape, r.dtype)

            for r, v in zip(acc_refs, res[no:]):
                r[...] += v.astype(F32)

    in_specs = [pl.BlockSpec((ts, w), functools.partial(lambda i, cb: (i, cb), cb=cb)) for (_, cb, w) in rows]
    in_specs += [pl.BlockSpec(c.shape, lambda i: (0, 0)) for c in consts]
    out_specs = [pl.BlockSpec((ts, w), lambda i: (i, 0)) for (w, _) in outs]
    out_specs += [pl.BlockSpec(shp, lambda i: (0, 0)) for shp in accs]
    out_shape = [jax.ShapeDtypeStruct((s, w), dt) for (w, dt) in outs]
    out_shape += [jax.ShapeDtypeStruct(shp, F32) for shp in accs]
    res = pl.pallas_call(
        body, name=name, grid=(s // ts,), in_specs=in_specs, out_specs=out_specs, out_shape=out_shape,
        compiler_params=_params(("arbitrary",) if na else ("parallel",)),
    )(*[a for (a, _, _) in rows], *consts)
    return res


def rowwise_vjp(name, f, rows, consts, cts, grad_dtypes, ts=256):
    nr, nc, nt = len(rows), len(consts), len(cts)
    keep = [i for i, dt in enumerate(grad_dtypes) if dt is not None]

    def g(*vals):
        rv = [v.astype(F32) for v in vals[:nr]]
        ctv = tuple(v.astype(F32) for v in vals[nr:nr + nt])
        cv = list(vals[nr + nt:])
        _, vjp = jax.vjp(lambda *a: tuple(f(*a)), *rv, *cv)
        grads = vjp(ctv)
        return tuple(grads[i] for i in keep) + tuple(grads[nr:])

    outs = [(rows[i][2], grad_dtypes[i]) for i in keep]
    return rowwise(name, g, list(rows) + list(cts), consts, outs, accs=[c.shape for c in consts], ts=ts)


def matmul(name, a, b, *, ta=False, tb=False, out_dtypes=(F32,), epilogue=None, epi=(), tm=1024, tn=1024, tk=1024):
    m, k = (a.shape[1], a.shape[0]) if ta else a.shape
    n = b.shape[0] if tb else b.shape[1]
    assert (b.shape[1] if tb else b.shape[0]) == k
    tm, tn, tk = _tile(m, tm), _tile(n, tn), _tile(k, tk)
    nk = k // tk
    ne, no = len(epi), len(out_dtypes)
    dims = (((0 if ta else 1,), (1 if tb else 0,)), ((), ()))

    def body(*refs):
        a_ref, b_ref = refs[0], refs[1]
        epi_refs = refs[2:2 + ne]
        out_refs = refs[2 + ne:2 + ne + no]
        part = lax.dot_general(a_ref[...].astype(BF16), b_ref[...].astype(BF16), dims, preferred_element_type=F32)

        def finish(acc):
            res = (acc,) if epilogue is None else epilogue(acc, *[r[...] for r in epi_refs])
            for r, v in zip(out_refs, res):
                r[...] = v.astype(r.dtype)

        if nk == 1:
            finish(part)
        else:
            acc_ref = refs[-1]
            kk = pl.program_id(2)

            @pl.when(kk == 0)
            def _():
                acc_ref[...] = part

            @pl.when(kk > 0)
            def _():
                acc_ref[...] += part

            @pl.when(kk == nk - 1)
            def _():
                finish(acc_ref[...])

    a_spec = pl.BlockSpec((tk, tm), lambda i, j, kk: (kk, i)) if ta else pl.BlockSpec((tm, tk), lambda i, j, kk: (i, kk))
    b_spec = pl.BlockSpec((tn, tk), lambda i, j, kk: (j, kk)) if tb else pl.BlockSpec((tk, tn), lambda i, j, kk: (kk, j))
    epi_specs = []
    for (arr, col0) in epi:
        assert col0 % tn == 0
        epi_specs.append(pl.BlockSpec((tm, tn), functools.partial(lambda i, j, kk, c0: (i, j + c0), c0=col0 // tn)))
    res = pl.pallas_call(
        body, name=name, grid=(m // tm, n // tn, nk),
        in_specs=[a_spec, b_spec] + epi_specs,
        out_specs=[pl.BlockSpec((tm, tn), lambda i, j, kk: (i, j)) for _ in out_dtypes],
        out_shape=[jax.ShapeDtypeStruct((m, n), dt) for dt in out_dtypes],
        scratch_shapes=[pltpu.VMEM((tm, tn), F32)] if nk > 1 else [],
        compiler_params=_params(("parallel", "parallel", "arbitrary")),
    )(a, b, *[arr for (arr, _) in epi])
    return res[0] if no == 1 else res


def _rms(x, g):
    return x * lax.rsqrt(jnp.mean(x * x, axis=-1, keepdims=True) + NORM_EPS) * g


def _ln(x, g, b):
    mu = jnp.mean(x, axis=-1, keepdims=True)
    xc = x - mu
    var = jnp.mean(xc * xc, axis=-1, keepdims=True)
    return xc * lax.rsqrt(var + NORM_EPS) * g + b


def _gelu(x):
    return 0.5 * x * (1.0 + jnp.tanh(math.sqrt(2.0 / math.pi) * (x + 0.044715 * (x * x * x))))


def _sigmoid(x):
    return 1.0 / (1.0 + jnp.exp(-x))


def _silu(x):
    return x * _sigmoid(x)


def _log_sigmoid(x):
    return jnp.minimum(x, 0.0) - jnp.log(1.0 + jnp.exp(-jnp.abs(x)))


def _f_pre(x, g, sc, sh):
    return (_rms(x, g) * (1.0 + sc) + sh,)


def _f_post(y, g, gt):
    return (gt * _rms(y, g),)


def _f_a1(u_raw, v_raw, g, b):
    return _gelu(u_raw), _ln(_gelu(v_raw), g, b)


def _f_glu(val, gate):
    return (val * _sigmoid(gate),)


def _f_lnsilu(zc, g, b):
    return (_silu(_ln(zc, g, b)),)


def _f_merge(g0, g1, g2, ya, yb, yc):
    return (_sigmoid(g0) * ya + _sigmoid(g1) * yb + _sigmoid(g2) * yc,)


def _lane_lt64(shape):
    return lax.broadcasted_iota(jnp.int32, shape, 1) < HEAD_DIM


def spatial_fwd(vln, u, w_bf, b_exp, rows_per_step=512):
    s = vln.shape[0]
    tr = min(rows_per_step, s)

    def body(v_ref, u_ref, w_ref, b_ref, sv_ref, ya_ref):
        lo = _lane_lt64((CHUNK, LANES))
        for ch in range(tr // CHUNK):
            r0 = ch * CHUNK
            for p in range(D_MIX // LANES):
                vp = v_ref[r0:r0 + CHUNK, p * LANES:(p + 1) * LANES]
                o0 = jnp.dot(w_ref[2 * p], vp, preferred_element_type=F32)
                o1 = jnp.dot(w_ref[2 * p + 1], vp, preferred_element_type=F32)
                sv = jnp.where(lo, o0, o1) + b_ref[:, p * LANES:(p + 1) * LANES]
                sv_ref[r0:r0 + CHUNK, p * LANES:(p + 1) * LANES] = sv
                ya_ref[r0:r0 + CHUNK, p * LANES:(p + 1) * LANES] = (
                    u_ref[r0:r0 + CHUNK, p * LANES:(p + 1) * LANES] * sv).astype(BF16)

    row = pl.BlockSpec((tr, D_MIX), lambda i: (i, 0))
    return pl.pallas_call(
        body, name="spatial_fwd", grid=(s // tr,),
        in_specs=[row, row, pl.BlockSpec(w_bf.shape, lambda i: (0, 0, 0)), pl.BlockSpec(b_exp.shape, lambda i: (0, 0))],
        out_specs=[row, row],
        out_shape=[jax.ShapeDtypeStruct((s, D_MIX), F32), jax.ShapeDtypeStruct((s, D_MIX), BF16)],
        compiler_params=_params(("parallel",)),
    )(vln, u, w_bf, b_exp)


def spatial_bwd(dya, u, sv, vln, wt_bf, rows_per_step=512):
    s = vln.shape[0]
    tr = min(rows_per_step, s)
    ng = wt_bf.shape[0]

    def body(dya_ref, u_ref, sv_ref, v_ref, wt_ref, du_ref, dv_ref, dw_ref, db_ref):
        @pl.when(pl.program_id(0) == 0)
        def _():
            dw_ref[...] = jnp.zeros(dw_ref.shape, F32)
            db_ref[...] = jnp.zeros(db_ref.shape, F32)

        lo = _lane_lt64((CHUNK, LANES))
        for ch in range(tr // CHUNK):
            r0 = ch * CHUNK
            for p in range(D_MIX // LANES):
                cs = slice(p * LANES, (p + 1) * LANES)
                dya_p = dya_ref[r0:r0 + CHUNK, cs].astype(F32)
                du_ref[r0:r0 + CHUNK, cs] = dya_p * sv_ref[r0:r0 + CHUNK, cs]
                dsv = dya_p * u_ref[r0:r0 + CHUNK, cs]
                db_ref[:, cs] += dsv
                dsv0 = jnp.where(lo, dsv, 0.0).astype(BF16)
                dsv1 = jnp.where(lo, 0.0, dsv).astype(BF16)
                vp = v_ref[r0:r0 + CHUNK, cs]
                d0 = jnp.dot(wt_ref[2 * p], dsv0, preferred_element_type=F32)
                d1 = jnp.dot(wt_ref[2 * p + 1], dsv1, preferred_element_type=F32)
                dv_ref[r0:r0 + CHUNK, cs] = d0 + d1
                nt = (((1,), (1,)), ((), ()))
                dw_ref[2 * p] += lax.dot_general(dsv0, vp, nt, preferred_element_type=F32)
                dw_ref[2 * p + 1] += lax.dot_general(dsv1, vp, nt, preferred_element_type=F32)

    row = pl.BlockSpec((tr, D_MIX), lambda i: (i, 0))
    return pl.pallas_call(
        body, name="spatial_bwd", grid=(s // tr,),
        in_specs=[row, row, row, row, pl.BlockSpec(wt_bf.shape, lambda i: (0, 0, 0))],
        out_specs=[row, row, pl.BlockSpec((ng, CHUNK, CHUNK), lambda i: (0, 0, 0)), pl.BlockSpec((CHUNK, D_MIX), lambda i: (0, 0))],
        out_shape=[jax.ShapeDtypeStruct((s, D_MIX), F32), jax.ShapeDtypeStruct((s, D_MIX), F32),
                   jax.ShapeDtypeStruct((ng, CHUNK, CHUNK), F32), jax.ShapeDtypeStruct((CHUNK, D_MIX), F32)],
        compiler_params=_params(("arbitrary",)),
    )(dya, u, sv, vln, wt_bf)


def conv_fwd(proj, cb_val, cb_gate, w_pad, cb, ln_g, ln_b, ts=256):
    s = proj.shape[0]
    ts = min(ts, s)
    per = ts // CONV_HALO

    def body(val_ref, gate_ref, pval_ref, pgate_ref, w_ref, cb_ref, g_ref, b_ref, zc_ref, yb_ref, ext_ref):
        i = pl.program_id(0)
        zprev = pval_ref[...] * _sigmoid(pgate_ref[...])
        ext_ref[0:CONV_HALO, :] = jnp.where(i > 0, zprev, 0.0)
        ext_ref[CONV_HALO:, :] = val_ref[...] * _sigmoid(gate_ref[...])
        acc = jnp.zeros((ts, D_MIX), F32)
        for j in range(CONV_WIDTH):
            off = CONV_HALO - (CONV_WIDTH - 1) + j
            acc = acc + w_ref[j:j + 1, :] * ext_ref[off:off + ts, :]
        zc = acc + cb_ref[...]
        zc_ref[...] = zc
        yb_ref[...] = _f_lnsilu(zc, g_ref[...], b_ref[...])[0].astype(BF16)

    def cur(c):
        return pl.BlockSpec((ts, D_MIX), functools.partial(lambda i, c: (i, c), c=c))

    def prev(c):
        return pl.BlockSpec((CONV_HALO, D_MIX), functools.partial(lambda i, c: (jnp.maximum(i * per - 1, 0), c), c=c))

    const = lambda a: pl.BlockSpec(a.shape, lambda i: (0, 0))
    out = pl.BlockSpec((ts, D_MIX), lambda i: (i, 0))
    return pl.pallas_call(
        body, name="conv_fwd", grid=(s // ts,),
        in_specs=[cur(cb_val), cur(cb_gate), prev(cb_val), prev(cb_gate), const(w_pad), const(cb), const(ln_g), const(ln_b)],
        out_specs=[out, out],
        out_shape=[jax.ShapeDtypeStruct((s, D_MIX), F32), jax.ShapeDtypeStruct((s, D_MIX), BF16)],
        scratch_shapes=[pltpu.VMEM((CONV_HALO + ts, D_MIX), F32)],
        compiler_params=_params(("parallel",)),
    )(proj, proj, proj, proj, w_pad, cb, ln_g, ln_b)


def conv_bwd(proj, cb_val, cb_gate, zc, dyb, w_pad, ln_g, ln_b, ts=256):
    s = proj.shape[0]
    ts = min(ts, s)
    per = ts // CONV_HALO
    n_tiles = s // ts
    n_halo = s // CONV_HALO

    def body(val_ref, gate_ref, pval_ref, pgate_ref, zc_ref, dyb_ref, nzc_ref, ndyb_ref, w_ref, g_ref, b_ref,
             dval_ref, dgate_ref, dw_ref, dcb_ref, dg_ref, db_ref, zext_ref, dext_ref):
        i = pl.program_id(0)

        @pl.when(i == 0)
        def _():
            for r in (dw_ref, dcb_ref, dg_ref, db_ref):
                r[...] = jnp.zeros(r.shape, F32)

        g, b = g_ref[...], b_ref[...]
        _, vjp = jax.vjp(lambda z, gg, bb: _f_lnsilu(z, gg, bb)[0], zc_ref[...], g, b)
        dzc, dg, db = vjp(dyb_ref[...].astype(F32))
        dg_ref[...] += dg
        db_ref[...] += db
        dcb_ref[...] += jnp.sum(dzc, axis=0, keepdims=True)
        _, vjp_n = jax.vjp(lambda z: _f_lnsilu(z, g, b)[0], nzc_ref[...])
        (dzc_next,) = vjp_n(ndyb_ref[...].astype(F32))
        dext_ref[0:ts, :] = dzc
        dext_ref[ts:, :] = jnp.where(i < n_tiles - 1, dzc_next, 0.0)
        val, gate = val_ref[...], gate_ref[...]
        zprev = pval_ref[...] * _sigmoid(pgate_ref[...])
        zext_ref[0:CONV_HALO, :] = jnp.where(i > 0, zprev, 0.0)
        zext_ref[CONV_HALO:, :] = val * _sigmoid(gate)
        dz = jnp.zeros((ts, D_MIX), F32)
        for j in range(CONV_WIDTH):
            shift = CONV_WIDTH - 1 - j
            dz = dz + w_ref[j:j + 1, :] * dext_ref[shift:shift + ts, :]
            off = CONV_HALO - shift
            dw_ref[j:j + 1, :] += jnp.sum(dzc * zext_ref[off:off + ts, :], axis=0, keepdims=True)
        _, vjp_glu = jax.vjp(lambda a, c: _f_glu(a, c)[0], val, gate)
        dval, dgate = vjp_glu(dz)
        dval_ref[...] = dval.astype(BF16)
        dgate_ref[...] = dgate.astype(BF16)

    def cur(c):
        return pl.BlockSpec((ts, D_MIX), functools.partial(lambda i, c: (i, c), c=c))

    def prev(c):
        return pl.BlockSpec((CONV_HALO, D_MIX), functools.partial(lambda i, c: (jnp.maximum(i * per - 1, 0), c), c=c))

    nxt = pl.BlockSpec((CONV_HALO, D_MIX), lambda i: (jnp.minimum((i + 1) * per, n_halo - 1), 0))
    const = lambda a: pl.BlockSpec(a.shape, lambda i: (0, 0))
    out = pl.BlockSpec((ts, D_MIX), lambda i: (i, 0))
    vec = pl.BlockSpec((1, D_MIX), lambda i: (0, 0))
    return pl.pallas_call(
        body, name="conv_bwd", grid=(n_tiles,),
        in_specs=[cur(cb_val), cur(cb_gate), prev(cb_val), prev(cb_gate), out, out, nxt, nxt, const(w_pad), const(ln_g), const(ln_b)],
        out_specs=[out, out, pl.BlockSpec((CONV_HALO, D_MIX), lambda i: (0, 0)), vec, vec, vec],
        out_shape=[jax.ShapeDtypeStruct((s, D_MIX), BF16), jax.ShapeDtypeStruct((s, D_MIX), BF16),
                   jax.ShapeDtypeStruct((CONV_HALO, D_MIX), F32)] + [jax.ShapeDtypeStruct((1, D_MIX), F32)] * 3,
        scratch_shapes=[pltpu.VMEM((CONV_HALO + ts, D_MIX), F32), pltpu.VMEM((ts + CONV_HALO, D_MIX), F32)],
        compiler_params=_params(("arbitrary",)),
    )(proj, proj, proj, proj, zc, dyb, zc, dyb, w_pad, ln_g, ln_b)


def forget_cumsum(proj, cb_f, bf_exp, t=256):
    s = proj.shape[0]
    t = min(t, s)

    def body(f_ref, bf_ref, out_ref, carry_ref):
        @pl.when(pl.program_id(0) == 0)
        def _():
            carry_ref[...] = jnp.zeros(carry_ref.shape, F32)

        lf = _log_sigmoid(f_ref[...] + bf_ref[...])
        tri = (lax.broadcasted_iota(jnp.int32, (t, t), 1) <= lax.broadcasted_iota(jnp.int32, (t, t), 0)).astype(F32)
        c = jnp.dot(tri, lf, precision=HIGHEST, preferred_element_type=F32) + carry_ref[...]
        out_ref[...] = c
        carry_ref[...] = c[t - 1:t, :]

    return pl.pallas_call(
        body, name="forget_cumsum", grid=(s // t,),
        in_specs=[pl.BlockSpec((t, D_MIX), functools.partial(lambda i, c: (i, c), c=cb_f)), pl.BlockSpec((1, D_MIX), lambda i: (0, 0))],
        out_specs=pl.BlockSpec((t, D_MIX), lambda i: (i, 0)),
        out_shape=jax.ShapeDtypeStruct((s, D_MIX), F32),
        scratch_shapes=[pltpu.VMEM((1, D_MIX), F32)],
        compiler_params=_params(("arbitrary",)),
    )(proj, bf_exp)


def forget_bwd(proj, cb_f, bf_exp, dcum, t=256):
    s = proj.shape[0]
    t = min(t, s)
    n = s // t

    def body(f_ref, bf_ref, dc_ref, df_ref, dbf_ref, carry_ref):
        @pl.when(pl.program_id(0) == 0)
        def _():
            carry_ref[...] = jnp.zeros(carry_ref.shape, F32)
            dbf_ref[...] = jnp.zeros(dbf_ref.shape, F32)

        tri = (lax.broadcasted_iota(jnp.int32, (t, t), 1) >= lax.broadcasted_iota(jnp.int32, (t, t), 0)).astype(F32)
        r = jnp.dot(tri, dc_ref[...], precision=HIGHEST, preferred_element_type=F32) + carry_ref[...]
        carry_ref[...] = r[0:1, :]
        df = r * _sigmoid(-(f_ref[...] + bf_ref[...]))
        dbf_ref[...] += jnp.sum(df, axis=0, keepdims=True)
        live = lax.broadcasted_iota(jnp.int32, (t, D_MIX), 1) % HEAD_DIM == 0
        df_ref[...] = jnp.where(live, df, 0.0).astype(BF16)

    return pl.pallas_call(
        body, name="forget_bwd", grid=(n,),
        in_specs=[pl.BlockSpec((t, D_MIX), functools.partial(lambda i, c: (n - 1 - i, c), c=cb_f)), pl.BlockSpec((1, D_MIX), lambda i: (0, 0)),
                  pl.BlockSpec((t, D_MIX), lambda i: (n - 1 - i, 0))],
        out_specs=[pl.BlockSpec((t, D_MIX), lambda i: (n - 1 - i, 0)), pl.BlockSpec((1, D_MIX), lambda i: (0, 0))],
        out_shape=[jax.ShapeDtypeStruct((s, D_MIX), BF16), jax.ShapeDtypeStruct((1, D_MIX), F32)],
        scratch_shapes=[pltpu.VMEM((1, D_MIX), F32)],
        compiler_params=_params(("arbitrary",)),
    )(proj, bf_exp, dcum)


NT = (((1,), (1,)), ((), ()))


def _rows_layout(a_pl):
    return a_pl[:, ::HEAD_DIM].T.reshape(D_MIX // LANES, 2, a_pl.shape[0])


def attn_fwd(qkv, cum, cum_rows, tq=256):
    s = qkv.shape[0]
    tq = min(tq, s)
    npair = D_MIX // LANES

    def body(q_ref, k_ref, v_ref, cq_ref, ck_ref, o_ref, lse_ref):
        qi = pl.program_id(1)
        lo = _lane_lt64((tq, LANES))
        q = q_ref[...]
        qh = (jnp.where(lo, q, 0), jnp.where(lo, 0, q))
        cqh = (cq_ref[:, 0:1], cq_ref[:, HEAD_DIM:HEAD_DIM + 1])
        causal = lax.broadcasted_iota(jnp.int32, (tq, tq), 1) <= lax.broadcasted_iota(jnp.int32, (tq, tq), 0)

        def step(j, carry, diag):
            ks = pl.multiple_of(j * tq, tq)
            kb = k_ref[pl.ds(ks, tq), :]
            vb = v_ref[pl.ds(ks, tq), :]
            ck = ck_ref[:, pl.ds(ks, tq)]
            new = []
            for h in range(2):
                m, l, acc = carry[h]
                sc = lax.dot_general(qh[h], kb, NT, preferred_element_type=F32) + (cqh[h] - ck[h:h + 1, :])
                if diag:
                    sc = jnp.where(causal, sc, -jnp.inf)
                m_new = jnp.maximum(m, jnp.max(sc, axis=1, keepdims=True))
                alpha = jnp.exp(m - m_new)
                p = jnp.exp(sc - m_new)
                l = alpha * l + jnp.sum(p, axis=1, keepdims=True)
                acc = alpha * acc + jnp.dot(p.astype(BF16), vb, preferred_element_type=F32)
                new.append((m_new, l, acc))
            return tuple(new)

        init = tuple((jnp.full((tq, 1), -jnp.inf, F32), jnp.zeros((tq, 1), F32), jnp.zeros((tq, LANES), F32)) for _ in range(2))
        carry = lax.fori_loop(0, qi, lambda j, c: step(j, c, False), init)
        (m0, l0, a0), (m1, l1, a1) = step(qi, carry, True)
        o_ref[...] = jnp.where(lo, a0 / l0, a1 / l1).astype(o_ref.dtype)
        lse_ref[...] = jnp.where(lo, m0 + jnp.log(l0), m1 + jnp.log(l1))

    blk = lambda off: pl.BlockSpec((tq, LANES), functools.partial(lambda p, i, off: (i, off + p), off=off))
    full = lambda off: pl.BlockSpec((s, LANES), functools.partial(lambda p, i, off: (0, off + p), off=off))
    return pl.pallas_call(
        body, name="attn_fwd", grid=(npair, s // tq),
        in_specs=[blk(0), full(npair), full(2 * npair), blk(0), pl.BlockSpec((None, 2, s), lambda p, i: (p, 0, 0))],
        out_specs=[blk(0), blk(0)],
        out_shape=[jax.ShapeDtypeStruct((s, D_MIX), BF16), jax.ShapeDtypeStruct((s, D_MIX), F32)],
        compiler_params=_params(("parallel", "parallel")),
    )(qkv, qkv, qkv, cum, cum_rows)


def attn_dq(qkv, cum, cum_rows, lse, do, tq=256):
    s = qkv.shape[0]
    tq = min(tq, s)
    npair = D_MIX // LANES
    scale = 1.0 / math.sqrt(HEAD_DIM)

    def body(q_ref, k_ref, v_ref, cq_ref, ck_ref, lse_ref, do_ref, dq_ref, dsum_ref):
        qi = pl.program_id(1)
        lo = _lane_lt64((tq, LANES))
        q, do_ = q_ref[...], do_ref[...]
        qh = (jnp.where(lo, q, 0), jnp.where(lo, 0, q))
        doh = (jnp.where(lo, do_, 0), jnp.where(lo, 0, do_))
        cqh = (cq_ref[:, 0:1], cq_ref[:, HEAD_DIM:HEAD_DIM + 1])
        lseh = (lse_ref[:, 0:1], lse_ref[:, HEAD_DIM:HEAD_DIM + 1])
        causal = lax.broadcasted_iota(jnp.int32, (tq, tq), 1) <= lax.broadcasted_iota(jnp.int32, (tq, tq), 0)

        def step(j, carry, diag):
            ks = pl.multiple_of(j * tq, tq)
            kb = k_ref[pl.ds(ks, tq), :]
            vb = v_ref[pl.ds(ks, tq), :]
            ck = ck_ref[:, pl.ds(ks, tq)]
            new = []
            for h in range(2):
                pdpk, pk, dsum = carry[h]
                sc = lax.dot_general(qh[h], kb, NT, preferred_element_type=F32) + (cqh[h] - ck[h:h + 1, :])
                if diag:
                    sc = jnp.where(causal, sc, -jnp.inf)
                p = jnp.exp(sc - lseh[h])
                pdp = p * lax.dot_general(doh[h], vb, NT, preferred_element_type=F32)
                new.append((pdpk + jnp.dot(pdp.astype(BF16), kb, preferred_element_type=F32),
                            pk + jnp.dot(p.astype(BF16), kb, preferred_element_type=F32),
                            dsum + jnp.sum(pdp, axis=1, keepdims=True)))
            return tuple(new)

        init = tuple((jnp.zeros((tq, LANES), F32), jnp.zeros((tq, LANES), F32), jnp.zeros((tq, 1), F32)) for _ in range(2))
        carry = lax.fori_loop(0, qi, lambda j, c: step(j, c, False), init)
        (a0, b0, s0), (a1, b1, s1) = step(qi, carry, True)
        dq_ref[...] = (jnp.where(lo, a0 - s0 * b0, a1 - s1 * b1) * scale).astype(dq_ref.dtype)
        dsum_ref[...] = jnp.where(lo, s0, s1)

    blk = lambda off: pl.BlockSpec((tq, LANES), functools.partial(lambda p, i, off: (i, off + p), off=off))
    full = lambda off: pl.BlockSpec((s, LANES), functools.partial(lambda p, i, off: (0, off + p), off=off))
    return pl.pallas_call(
        body, name="attn_dq", grid=(npair, s // tq),
        in_specs=[blk(0), full(npair), full(2 * npair), blk(0), pl.BlockSpec((None, 2, s), lambda p, i: (p, 0, 0)), blk(0), blk(0)],
        out_specs=[blk(0), blk(0)],
        out_shape=[jax.ShapeDtypeStruct((s, D_MIX), BF16), jax.ShapeDtypeStruct((s, D_MIX), F32)],
        compiler_params=_params(("parallel", "parallel")),
    )(qkv, qkv, qkv, cum, cum_rows, lse, do)


def attn_dkv(qkv, cum, cum_rows, lse_rows, dsum_rows, do, tk=256):
    s = qkv.shape[0]
    tk = min(tk, s)
    npair = D_MIX // LANES
    nq = s // tk

    def body(k_ref, v_ref, ck_ref, q_ref, do_ref, cq_ref, lse_ref, dsum_ref, dk_ref, dv_ref, dck_ref):
        kj = pl.program_id(1)
        lo = _lane_lt64((tk, LANES))
        k, v = k_ref[...], v_ref[...]
        kh = (jnp.where(lo, k, 0), jnp.where(lo, 0, k))
        vh = (jnp.where(lo, v, 0), jnp.where(lo, 0, v))
        ckh = (ck_ref[:, 0:1], ck_ref[:, HEAD_DIM:HEAD_DIM + 1])
        causal = lax.broadcasted_iota(jnp.int32, (tk, tk), 0) <= lax.broadcasted_iota(jnp.int32, (tk, tk), 1)

        def step(i, carry, diag):
            qs = pl.multiple_of(i * tk, tk)
            qb = q_ref[pl.ds(qs, tk), :]
            dob = do_ref[pl.ds(qs, tk), :]
            cq = cq_ref[:, pl.ds(qs, tk)]
            lse = lse_ref[:, pl.ds(qs, tk)]
            dsum = dsum_ref[:, pl.ds(qs, tk)]
            new = []
            for h in range(2):
                dk, dv, dck = carry[h]
                st = lax.dot_general(kh[h], qb, NT, preferred_element_type=F32) + (cq[h:h + 1, :] - ckh[h])
                if diag:
                    st = jnp.where(causal, st, -jnp.inf)
                pt = jnp.exp(st - lse[h:h + 1, :])
                dv = dv + jnp.dot(pt.astype(BF16), dob, preferred_element_type=F32)
                dpt = lax.dot_general(vh[h], dob, NT, preferred_element_type=F32)
                dst = pt * (dpt - dsum[h:h + 1, :])
                dk = dk + jnp.dot(dst.astype(BF16), qb, preferred_element_type=F32)
                dck = dck - jnp.sum(dst, axis=1, keepdims=True)
                new.append((dk, dv, dck))
            return tuple(new)

        init = tuple((jnp.zeros((tk, LANES), F32), jnp.zeros((tk, LANES), F32), jnp.zeros((tk, 1), F32)) for _ in range(2))
        carry = step(kj, init, True)
        (dk0, dv0, dc0), (dk1, dv1, dc1) = lax.fori_loop(kj + 1, nq, lambda i, c: step(i, c, False), carry)
        dk_ref[...] = jnp.where(lo, dk0, dk1).astype(dk_ref.dtype)
        dv_ref[...] = jnp.where(lo, dv0, dv1).astype(dv_ref.dtype)
        dck_ref[...] = jnp.where(lo, dc0, dc1)

    blk = lambda off: pl.BlockSpec((tk, LANES), functools.partial(lambda p, i, off: (i, off + p), off=off))
    full = lambda off: pl.BlockSpec((s, LANES), functools.partial(lambda p, i, off: (0, off + p), off=off))
    rows = pl.BlockSpec((None, 2, s), lambda p, i: (p, 0, 0))
    return pl.pallas_call(
        body, name="attn_dkv", grid=(npair, nq),
        in_specs=[blk(npair), blk(2 * npair), blk(0), full(0), full(0), rows, rows, rows],
        out_specs=[blk(0), blk(0), blk(0)],
        out_shape=[jax.ShapeDtypeStruct((s, D_MIX), BF16), jax.ShapeDtypeStruct((s, D_MIX), BF16), jax.ShapeDtypeStruct((s, D_MIX), F32)],
        compiler_params=_params(("parallel", "parallel")),
    )(qkv, qkv, cum, qkv, do, cum_rows, lse_rows, dsum_rows)


def _pre_bwd(name, x, g, sc, sh, dh, dres):
    d = x.shape[1]

    def fn(xv, dhv, dresv, gv, scv, shv):
        _, vjp = jax.vjp(lambda *a: _f_pre(*a)[0], xv, gv, scv, shv)
        dx, dg, dsc, dsh = vjp(dhv.astype(F32))
        return dx + dresv, dg, dsc, dsh

    return rowwise(name, fn, [(x, 0, d), (dh, 0, d), (dres, 0, d)], [g, sc, sh], [(d, F32)], accs=[(1, d)] * 3)


def layer_fwd(x, mod, w):
    s, d = x.shape
    m = D_MIX
    sh1, sc1, gt1, sh2, sc2, gt2 = (mod[i:i + 1] for i in range(6))
    cb = 3 * d // m
    (h,) = rowwise("pre1", _f_pre, [(x, 0, d)], [w["mix_pre_g"], sc1, sh1], [(d, BF16)])
    proj = matmul("w_in", h, w["w_in"])
    u, vln = rowwise("gmlp_in", _f_a1, [(proj, cb, m), (proj, cb + 1, m)], [w["gmlp_ln_g"], w["gmlp_ln_b"]], [(m, F32), (m, BF16)])
    sv, ya = spatial_fwd(vln, u, w["ws"], w["bs_exp"])
    y_a = matmul("w_a", ya, w["w_a_out"])
    zc, yb = conv_fwd(proj, cb + 2, cb + 3, w["conv_w"], w["conv_b"], w["conv_ln_g"], w["conv_ln_b"])
    y_b = matmul("w_b", yb, w["w_b_out"])
    scale = 1.0 / math.sqrt(HEAD_DIM)
    (qkv,) = rowwise("qkv", lambda q, k, v: jnp.concatenate([q * scale, k, v], axis=1),
                     [(proj, cb + 4, m), (proj, cb + 5, m), (proj, cb + 6, m)], [], [(3 * m, BF16)])
    cum = forget_cumsum(proj, cb + 7, w["bf_exp"])
    cum_rows = _rows_layout(cum)
    o, lse = attn_fwd(qkv, cum, cum_rows)
    y_c = matmul("w_c", o, w["w_c_out"])
    (merged,) = rowwise("merge", _f_merge, [(proj, 0, d), (proj, 1, d), (proj, 2, d), (y_a, 0, d), (y_b, 0, d), (y_c, 0, d)], [], [(d, BF16)])
    y = matmul("w_out", merged, w["w_out"])

    def post_pre(xv, yv, gp, gt, g2, sc, sh):
        x1 = xv + _f_post(yv, gp, gt)[0]
        return x1, _f_pre(x1, g2, sc, sh)[0]

    x1, h2 = rowwise("post1", post_pre, [(x, 0, d), (y, 0, d)], [w["mix_post_g"], gt1, w["mlp_pre_g"], sc2, sh2], [(d, F32), (d, BF16)])
    a_bf, r = matmul("w1", h2, w["mlp_w1"], out_dtypes=(BF16, BF16), epilogue=lambda acc: (acc, jnp.square(jnp.maximum(acc, 0.0))))
    y2 = matmul("w2", r, w["mlp_w2"])
    (x2,) = rowwise("post2", lambda xv, yv, g, gt: xv + _f_post(yv, g, gt)[0], [(x1, 0, d), (y2, 0, d)], [w["mlp_post_g"], gt2], [(d, F32)])
    saved = dict(x=x, h=h, proj=proj, u=u, vln=vln, sv=sv, ya=ya, y_a=y_a, zc=zc, yb=yb, y_b=y_b, qkv=qkv, cum=cum, cum_rows=cum_rows,
                 o=o, lse=lse, y_c=y_c, merged=merged, y=y, x1=x1, h2=h2, a_bf=a_bf, r=r, y2=y2)
    return x2, saved


def layer_bwd(dx2, mod, w, sv):
    x, proj = sv["x"], sv["proj"]
    s, d = x.shape
    m = D_MIX
    sh1, sc1, gt1, sh2, sc2, gt2 = (mod[i:i + 1] for i in range(6))
    cb = 3 * d // m
    g = {}
    dy2, g["mlp_post_g"], dgt2 = rowwise_vjp("post2_b", _f_post, [(sv["y2"], 0, d)], [w["mlp_post_g"], gt2], [(dx2, 0, d)], [BF16])
    da = matmul("w2_dx", dy2, w["mlp_w2"], tb=True, out_dtypes=(BF16,),
                epilogue=lambda acc, a: (acc * (2.0 * jnp.maximum(a.astype(F32), 0.0)),), epi=[(sv["a_bf"], 0)])
    g["mlp_w2"] = matmul("w2_dw", sv["r"], dy2, ta=True)
    dh2 = matmul("w1_dx", da, w["mlp_w1"], tb=True)
    g["mlp_w1"] = matmul("w1_dw", sv["h2"], da, ta=True)
    dx1, g["mlp_pre_g"], dsc2, dsh2 = _pre_bwd("pre2_b", sv["x1"], w["mlp_pre_g"], sc2, sh2, dh2, dx2)
    dy, g["mix_post_g"], dgt1 = rowwise_vjp("post1_b", _f_post, [(sv["y"], 0, d)], [w["mix_post_g"], gt1], [(dx1, 0, d)], [BF16])
    dmerged = matmul("w_out_dx", dy, w["w_out"], tb=True)
    g["w_out"] = matmul("w_out_dw", sv["merged"], dy, ta=True)
    dg0, dg1, dg2, dya_, dyb_, dyc_ = rowwise_vjp(
        "merge_b", _f_merge, [(proj, 0, d), (proj, 1, d), (proj, 2, d), (sv["y_a"], 0, d), (sv["y_b"], 0, d), (sv["y_c"], 0, d)], [],
        [(dmerged, 0, d)], [BF16] * 6)
    dya_pre = matmul("w_a_dx", dya_, w["w_a_out"], tb=True)
    g["w_a_out"] = matmul("w_a_dw", sv["ya"], dya_, ta=True)
    dyb_pre = matmul("w_b_dx", dyb_, w["w_b_out"], tb=True)
    g["w_b_out"] = matmul("w_b_dw", sv["yb"], dyb_, ta=True)
    do = matmul("w_c_dx", dyc_, w["w_c_out"], tb=True, out_dtypes=(BF16,))
    g["w_c_out"] = matmul("w_c_dw", sv["o"], dyc_, ta=True)
    dq, dsum = attn_dq(sv["qkv"], sv["cum"], sv["cum_rows"], sv["lse"], do)
    dk, dv, dcum = attn_dkv(sv["qkv"], sv["cum"], sv["cum_rows"], _rows_layout(sv["lse"]), _rows_layout(dsum), do)
    df, dbf = forget_bwd(proj, cb + 7, w["bf_exp"], dcum)
    g["fox_bf"] = dbf[0, ::HEAD_DIM]
    dval, dgate, dwc, g["conv_b"], g["conv_ln_g"], g["conv_ln_b"] = conv_bwd(
        proj, cb + 2, cb + 3, sv["zc"], dyb_pre, w["conv_w"], w["conv_ln_g"], w["conv_ln_b"])
    g["conv_w"] = dwc[:CONV_WIDTH]
    du, dvln, dws, dbexp = spatial_bwd(dya_pre, sv["u"], sv["sv"], sv["vln"], w["ws_t"])
    g["gmlp_ws"] = dws * jnp.tril(jnp.ones((CHUNK, CHUNK), F32))
    g["gmlp_bs"] = dbexp.reshape(CHUNK, m // GROUP_DIM, GROUP_DIM).sum(-1).T
    du_raw, dv_raw, g["gmlp_ln_g"], g["gmlp_ln_b"] = rowwise_vjp(
        "gmlp_in_b", _f_a1, [(proj, cb, m), (proj, cb + 1, m)], [w["gmlp_ln_g"], w["gmlp_ln_b"]], [(du, 0, m), (dvln, 0, m)], [BF16, BF16])
    dproj = jnp.concatenate([dg0, dg1, dg2, du_raw, dv_raw, dval, dgate, dq, dk, dv, df], axis=1)
    dh = matmul("w_in_dx", dproj, w["w_in"], tb=True)
    g["w_in"] = matmul("w_in_dw", sv["h"], dproj, ta=True)
    dx, g["mix_pre_g"], dsc1, dsh1 = _pre_bwd("pre1_b", x, w["mix_pre_g"], sc1, sh1, dh, dx1)
    dmod = jnp.concatenate([dsh1, dsc1, dgt1, dsh2, dsc2, dgt2], axis=0)
    return dx, dmod, g


def local_step(x, target, mods, ws):
    d = x.shape[1]
    saved = []
    for l in range(len(ws)):
        x, sv = layer_fwd(x, mods[l], ws[l])
        saved.append(sv)

    def loss_fn(xv, tv):
        err = xv - tv
        return err * (1.0 / d), jnp.sum(err * err, axis=0, keepdims=True)

    dx, sq = rowwise("loss", loss_fn, [(x, 0, d), (target, 0, d)], [], [(d, F32)], accs=[(1, d)])
    loss = (0.5 / d) * jnp.sum(sq)
    dmods, grads = [None] * len(ws), [None] * len(ws)
    for l in reversed(range(len(ws))):
        dx, dmods[l], grads[l] = layer_bwd(dx, mods[l], ws[l], saved[l])
    return loss, dx, dmods, grads


def exchange(name, arrs, scatter):
    n = len(arrs)

    def body(*refs):
        in_refs, out_refs = refs[:n], refs[n:2 * n]
        send_sems, recv_sems, local_sems = refs[2 * n:]
        x, y, c = lax.axis_index("x"), lax.axis_index("y"), lax.axis_index("c")
        me = 4 * x + 2 * y + c
        local = []
        for a in range(n):
            src = in_refs[a].at[me] if scatter else in_refs[a]
            cp = pltpu.make_async_copy(src, out_refs[a].at[me], local_sems.at[a])
            cp.start()
            local.append(cp)
        remote = []
        for k in range(1, N_DEV):
            px, py, pc = x ^ ((k >> 2) & 1), y ^ ((k >> 1) & 1), c ^ (k & 1)
            peer = 4 * px + 2 * py + pc
            for a in range(n):
                src = in_refs[a].at[peer] if scatter else in_refs[a]
                cp = pltpu.make_async_remote_copy(
                    src_ref=src, dst_ref=out_refs[a].at[me], send_sem=send_sems.at[a * (N_DEV - 1) + k - 1],
                    recv_sem=recv_sems.at[a * (N_DEV - 1) + k - 1], device_id=(px, py, pc), device_id_type=MESH)
                cp.start()
                remote.append(cp)
        for cp in remote:
            cp.wait()
        for cp in local:
            cp.wait()

    hbm = pl.BlockSpec(memory_space=pltpu.HBM)
    out_shape = [jax.ShapeDtypeStruct(a.shape if scatter else (N_DEV,) + a.shape, a.dtype) for a in arrs]
    return pl.pallas_call(
        body, name=name, in_specs=[hbm] * n, out_specs=[hbm] * n, out_shape=out_shape,
        scratch_shapes=[pltpu.SemaphoreType.DMA((n * (N_DEV - 1),)), pltpu.SemaphoreType.DMA((n * (N_DEV - 1),)),
                        pltpu.SemaphoreType.DMA((n,))],
    )(*arrs)


def adamw_sum(name, parts, w, m, v, tr=256):
    k, r, c = parts.shape
    tr = _tile(r, tr, 16)
    c1 = 1.0 - ADAM_B1 ** ADAM_STEP
    c2 = 1.0 - ADAM_B2 ** ADAM_STEP

    def body(p_ref, w_ref, m_ref, v_ref, g_ref, d_ref, nm_ref, nv_ref):
        grad = p_ref[0].astype(F32)
        for j in range(1, k):
            grad = grad + p_ref[j].astype(F32)
        new_m = ADAM_B1 * m_ref[...] + (1.0 - ADAM_B1) * grad
        new_v = ADAM_B2 * v_ref[...] + (1.0 - ADAM_B2) * (grad * grad)
        m_hat = new_m / c1
        v_hat = new_v / c2
        g_ref[...] = grad
        d_ref[...] = -ADAM_LR * (m_hat / (jnp.sqrt(v_hat) + ADAM_EPS) + ADAM_WD * w_ref[...])
        nm_ref[...] = new_m
        nv_ref[...] = new_v

    blk = pl.BlockSpec((tr, c), lambda i: (i, 0))
    return pl.pallas_call(
        body, name=name, grid=(r // tr,),
        in_specs=[pl.BlockSpec((k, tr, c), lambda i: (0, i, 0)), blk, blk, blk],
        out_specs=[blk] * 4, out_shape=[jax.ShapeDtypeStruct((r, c), F32)] * 4,
        compiler_params=_params(("parallel",)),
    )(parts, w, m, v)


def ada_fwd(c_all, ada_w):
    nl, d, n = ada_w.shape

    def body(c_ref, w_ref, o_ref):
        o_ref[...] = jnp.dot(_silu(c_ref[...]), w_ref[...], precision=HIGHEST, preferred_element_type=F32)

    return pl.pallas_call(
        body, name="ada_fwd", grid=(nl,),
        in_specs=[pl.BlockSpec((N_DEV, d), lambda l: (0, 0)), pl.BlockSpec((None, d, n), lambda l: (l, 0, 0))],
        out_specs=pl.BlockSpec((None, N_DEV, n), lambda l: (l, 0, 0)),
        out_shape=jax.ShapeDtypeStruct((nl, N_DEV, n), F32),
        compiler_params=_params(("parallel",)),
    )(c_all, ada_w)


def ada_bwd(c_all_t, dmod, td=256):
    d = c_all_t.shape[0]
    nl, _, n = dmod.shape
    td = _tile(d, td, 8)

    def body(c_ref, dm_ref, o_ref):
        ca = _silu(c_ref[...])
        acc = ca[:, 0:1] * dm_ref[0:1, :]
        for b in range(1, N_DEV):
            acc = acc + ca[:, b:b + 1] * dm_ref[b:b + 1, :]
        o_ref[...] = acc

    return pl.pallas_call(
        body, name="ada_bwd", grid=(nl, d // td),
        in_specs=[pl.BlockSpec((td, N_DEV), lambda l, i: (i, 0)), pl.BlockSpec((None, N_DEV, n), lambda l, i: (l, 0, 0))],
        out_specs=pl.BlockSpec((None, td, n), lambda l, i: (l, i, 0)),
        out_shape=jax.ShapeDtypeStruct((nl, d, n), F32),
        compiler_params=_params(("parallel", "parallel")),
    )(c_all_t, dmod)


ARG_NAMES = ["x", "c", "ada_w", "ada_b", "mix_pre_g", "mix_post_g", "mlp_pre_g", "mlp_post_g", "w_in", "gmlp_ln_g", "gmlp_ln_b",
             "gmlp_ws", "gmlp_bs", "w_a_out", "conv_w", "conv_b", "conv_ln_g", "conv_ln_b", "w_b_out", "fox_bf", "w_c_out",
             "w_out", "mlp_w1", "mlp_w2"]
WEIGHTS = ARG_NAMES[2:]
COL_SHARDED = ["w_in", "w_a_out", "w_b_out", "w_c_out", "mlp_w1"]
ROW_SHARDED = ["w_out", "mlp_w2"]
BIG = COL_SHARDED + ROW_SHARDED
SMALL = ["ada_b", "mix_pre_g", "mix_post_g", "mlp_pre_g", "mlp_post_g", "gmlp_ln_g", "gmlp_ln_b", "gmlp_ws", "gmlp_bs",
         "conv_b", "conv_ln_g", "conv_ln_b", "fox_bf"]
PACK_COLS = 512


def _to_my_layout(w_in, d):
    m = D_MIX
    nf = 7 * m
    return jnp.concatenate([w_in[..., nf + N_HEADS:], w_in[..., :nf], jnp.repeat(w_in[..., nf:nf + N_HEADS], HEAD_DIM, axis=-1)], axis=-1)


def _from_my_layout(gw, d):
    m = D_MIX
    return jnp.concatenate([gw[..., 3 * d:3 * d + 7 * m], gw[..., 3 * d + 7 * m::HEAD_DIM], gw[..., :3 * d]], axis=-1)


def _pack(parts):
    flat = jnp.concatenate([p.reshape(-1).astype(F32) for p in parts])
    pad = (-flat.shape[0]) % (PACK_COLS * 8)
    return jnp.pad(flat, (0, pad)).reshape(-1, PACK_COLS)


def _unpack(packed, shapes):
    flat, out, off = packed.reshape(-1), [], 0
    for shp in shapes:
        n = math.prod(shp)
        out.append(flat[off:off + n].reshape(shp))
        off += n
    return out


def _layer_weights(full, p, conv_full, l):
    wl = {k: full[k][l] for k in BIG}
    for k in ["mix_pre_g", "mix_post_g", "mlp_pre_g", "mlp_post_g", "gmlp_ln_g", "gmlp_ln_b", "conv_b", "conv_ln_g", "conv_ln_b"]:
        wl[k] = p[k][l][None, :]
    wm = p["gmlp_ws"][l] * jnp.tril(jnp.ones((CHUNK, CHUNK), F32))
    wl["ws"] = wm.astype(BF16)
    wl["ws_t"] = jnp.transpose(wm, (0, 2, 1)).astype(BF16)
    wl["bs_exp"] = jnp.repeat(p["gmlp_bs"][l].T, GROUP_DIM, axis=1)
    wl["bf_exp"] = jnp.repeat(p["fox_bf"][l], HEAD_DIM)[None, :]
    wl["conv_w"] = jnp.pad(conv_full[l], ((0, CONV_HALO - CONV_WIDTH), (0, 0)))
    return wl


def kernel(x, c, ada_w, ada_b, mix_pre_g, mix_post_g, mlp_pre_g, mlp_post_g, w_in, gmlp_ln_g, gmlp_ln_b, gmlp_ws, gmlp_bs, w_a_out, conv_w, conv_b, conv_ln_g, conv_ln_b, w_b_out, fox_bf, w_c_out, w_out, mlp_w1, mlp_w2, loss_target, m_ada_w, m_ada_b, m_mix_pre_g, m_mix_post_g, m_mlp_pre_g, m_mlp_post_g, m_w_in, m_gmlp_ln_g, m_gmlp_ln_b, m_gmlp_ws, m_gmlp_bs, m_w_a_out, m_conv_w, m_conv_b, m_conv_ln_g, m_conv_ln_b, m_w_b_out, m_fox_bf, m_w_c_out, m_w_out, m_mlp_w1, m_mlp_w2, v_ada_w, v_ada_b, v_mix_pre_g, v_mix_post_g, v_mlp_pre_g, v_mlp_post_g, v_w_in, v_gmlp_ln_g, v_gmlp_ln_b, v_gmlp_ws, v_gmlp_bs, v_w_a_out, v_conv_w, v_conv_b, v_conv_ln_g, v_conv_ln_b, v_w_b_out, v_fox_bf, v_w_c_out, v_w_out, v_mlp_w1, v_mlp_w2):
    args = (x, c, ada_w, ada_b, mix_pre_g, mix_post_g, mlp_pre_g, mlp_post_g, w_in, gmlp_ln_g, gmlp_ln_b, gmlp_ws, gmlp_bs, w_a_out,
            conv_w, conv_b, conv_ln_g, conv_ln_b, w_b_out, fox_bf, w_c_out, w_out, mlp_w1, mlp_w2)
    ms = (m_ada_w, m_ada_b, m_mix_pre_g, m_mix_post_g, m_mlp_pre_g, m_mlp_post_g, m_w_in, m_gmlp_ln_g, m_gmlp_ln_b, m_gmlp_ws, m_gmlp_bs,
          m_w_a_out, m_conv_w, m_conv_b, m_conv_ln_g, m_conv_ln_b, m_w_b_out, m_fox_bf, m_w_c_out, m_w_out, m_mlp_w1, m_mlp_w2)
    vs = (v_ada_w, v_ada_b, v_mix_pre_g, v_mix_post_g, v_mlp_pre_g, v_mlp_post_g, v_w_in, v_gmlp_ln_g, v_gmlp_ln_b, v_gmlp_ws, v_gmlp_bs,
          v_w_a_out, v_conv_w, v_conv_b, v_conv_ln_g, v_conv_ln_b, v_w_b_out, v_fox_bf, v_w_c_out, v_w_out, v_mlp_w1, v_mlp_w2)
    p = dict(zip(ARG_NAMES, args))
    mom = dict(zip(WEIGHTS, ms))
    var = dict(zip(WEIGHTS, vs))
    nl = ada_w.shape[0]
    s, d = x.shape[1], x.shape[2]
    me = 4 * lax.axis_index("x") + 2 * lax.axis_index("y") + lax.axis_index("c")

    c_all, conv_all = exchange("gather_c", [c, conv_w], scatter=False)
    c_all = c_all.reshape(N_DEV, d)
    n_ada = ada_w.shape[2]
    mod_parts = ada_fwd(c_all, ada_w)
    (mod_recv,) = exchange("scatter_mod", [jnp.transpose(mod_parts, (1, 0, 2))], scatter=True)
    mod = jnp.transpose(mod_recv, (1, 0, 2)).reshape(nl, N_DEV * n_ada) + ada_b
    mods = [mod[l].reshape(6, d) for l in range(nl)]
    conv_full = jnp.transpose(conv_all, (1, 2, 0, 3)).reshape(nl, CONV_WIDTH, D_MIX)

    gathered = exchange("gather_w", [p[k].astype(BF16) for k in BIG], scatter=False)
    full = {}
    for k, gk in zip(BIG, gathered):
        _, _, r, cc = gk.shape
        if k in COL_SHARDED:
            full[k] = jnp.transpose(gk, (1, 2, 0, 3)).reshape(nl, r, N_DEV * cc)
        else:
            full[k] = jnp.transpose(gk, (1, 0, 2, 3)).reshape(nl, N_DEV * r, cc)
    full["w_in"] = _to_my_layout(full["w_in"], d)

    ws = [_layer_weights(full, p, conv_full, l) for l in range(nl)]

    loss_local, dx, dmods, grads = local_step(x[0], loss_target[0], mods, ws)
    loss = lax.psum(loss_local, ("x", "y", "c"))
    grad_x = dx[None]

    stack = lambda k: jnp.stack([grads[l][k] for l in range(nl)])
    dmod = jnp.stack([dmods[l].reshape(6 * d) for l in range(nl)])
    small_parts = [dmod] + [stack(k) for k in SMALL[1:]] + [stack("conv_w")]
    small_shapes = [p[k].shape for k in SMALL] + [(nl, CONV_WIDTH, D_MIX)]
    packed = _pack(small_parts)
    (small_all,) = exchange("gather_small", [packed], scatter=False)
    zeros_conv = jnp.zeros(small_shapes[-1], F32)
    packs = [_pack([src[k] for k in SMALL] + [zeros_conv]) for src in (p, mom, var)]
    small_out = [_unpack(o, small_shapes) for o in adamw_sum("adamw_small", small_all, *packs)]

    out = {k: [None] * 4 for k in WEIGHTS}
    for i, k in enumerate(SMALL):
        for j in range(4):
            out[k][j] = small_out[j][i]

    def shard_update(name, parts, k):
        shp = p[k].shape
        flat = lambda a: a.reshape(-1, shp[-1])
        res = adamw_sum(name, parts.reshape((parts.shape[0],) + flat(p[k]).shape), flat(p[k]), flat(mom[k]), flat(var[k]))
        out[k] = [a.reshape(shp) for a in res]

    n_conv = conv_w.shape[2]
    conv_grad = lax.dynamic_slice_in_dim(small_out[0][-1], me * n_conv, n_conv, axis=2)
    shard_update("adamw_conv_w", conv_grad[None], "conv_w")

    dmod_all = small_all.reshape(N_DEV, -1)[:, :nl * 6 * d].reshape(N_DEV, nl, 6 * d)
    dmod_mine = lax.dynamic_slice_in_dim(dmod_all, me * n_ada, n_ada, axis=2)
    g_ada = ada_bwd(c_all.T, jnp.transpose(dmod_mine, (1, 0, 2)))
    shard_update("adamw_ada_w", g_ada[None], "ada_w")

    send = []
    for k in BIG:
        gk = stack(k)
        if k == "w_in":
            gk = _from_my_layout(gk, d)
        if k in COL_SHARDED:
            r, cc = gk.shape[1], gk.shape[2] // N_DEV
            gk = jnp.transpose(gk.reshape(nl, r, N_DEV, cc), (2, 0, 1, 3))
        else:
            r, cc = gk.shape[1] // N_DEV, gk.shape[2]
            gk = jnp.transpose(gk.reshape(nl, N_DEV, r, cc), (1, 0, 2, 3))
        send.append(gk.astype(BF16))
    recv = exchange("scatter_grads", send, scatter=True)
    for k, rk in zip(BIG, recv):
        shard_update("adamw_" + k, rk, k)

    res = [loss, grad_x]
    for j in range(4):
        res += [out[k][j] for k in WEIGHTS]
    return tuple(res)
```

```python
import functools
import math

import jax
import jax.numpy as jnp
from jax import lax
from jax.experimental import pallas as pl
from jax.experimental.pallas import tpu as pltpu

F32 = jnp.float32
BF16 = jnp.bfloat16
MESH = pl.DeviceIdType.MESH
N_DEV = 8
NORM_EPS = 1e-6
D_MIX = 512
N_HEADS = 8
HEAD_DIM = 64
GROUP_DIM = 64
CHUNK = 128
CONV_WIDTH = 31
CONV_HALO = 32
LANES = 128
ADAM_LR, ADAM_B1, ADAM_B2, ADAM_EPS, ADAM_WD, ADAM_STEP = 0.001, 0.9, 0.999, 1e-08, 0.01, 10
VMEM_LIMIT = 56 * 1024 * 1024
HIGHEST = lax.Precision.HIGHEST


def _tile(dim, pref, mult=LANES):
    t = min(pref, dim)
    t -= t % mult
    while t >= mult:
        if dim % t == 0:
            return t
        t -= mult
    return dim


def _params(sem):
    return pltpu.CompilerParams(dimension_semantics=sem, vmem_limit_bytes=VMEM_LIMIT)


def rowwise(name, fn, rows, consts, outs, accs=(), ts=256):
    s = rows[0][0].shape[0]
    ts = min(ts, s)
    nr, nc, no, na = len(rows), len(consts), len(outs), len(accs)

    def body(*refs):
        vals = [r[...] for r in refs[:nr + nc]]
        res = fn(*vals)
        if not isinstance(res, (tuple, list)):
            res = (res,)
        for r, v in zip(refs[nr + nc:nr + nc + no], res[:no]):
            r[...] = v.astype(r.dtype)
        if na:
            acc_refs = refs[nr + nc + no:]

            @pl.when(pl.program_id(0) == 0)
            def _():
                for r in acc_refs:
                    r[...] = jnp.zeros(r.shape, r.dtype)

            for r, v in zip(acc_refs, res[no:]):
                r[...] += v.astype(F32)

    in_specs = [pl.BlockSpec((ts, w), functools.partial(lambda i, cb: (i, cb), cb=cb)) for (_, cb, w) in rows]
    in_specs += [pl.BlockSpec(c.shape, lambda i: (0, 0)) for c in consts]
    out_specs = [pl.BlockSpec((ts, w), lambda i: (i, 0)) for (w, _) in outs]
    out_specs += [pl.BlockSpec(shp, lambda i: (0, 0)) for shp in accs]
    out_shape = [jax.ShapeDtypeStruct((s, w), dt) for (w, dt) in outs]
    out_shape += [jax.ShapeDtypeStruct(shp, F32) for shp in accs]
    res = pl.pallas_call(
        body, name=name, grid=(s // ts,), in_specs=in_specs, out_specs=out_specs, out_shape=out_shape,
        compiler_params=_params(("arbitrary",) if na else ("parallel",)),
    )(*[a for (a, _, _) in rows], *consts)
    return res


def rowwise_vjp(name, f, rows, consts, cts, grad_dtypes, ts=256):
    nr, nc, nt = len(rows), len(consts), len(cts)
    keep = [i for i, dt in enumerate(grad_dtypes) if dt is not None]

    def g(*vals):
        rv = [v.astype(F32) for v in vals[:nr]]
        ctv = tuple(v.astype(F32) for v in vals[nr:nr + nt])
        cv = list(vals[nr + nt:])
        _, vjp = jax.vjp(lambda *a: tuple(f(*a)), *rv, *cv)
        grads = vjp(ctv)
        return tuple(grads[i] for i in keep) + tuple(grads[nr:])

    outs = [(rows[i][2], grad_dtypes[i]) for i in keep]
    return rowwise(name, g, list(rows) + list(cts), consts, outs, accs=[c.shape for c in consts], ts=ts)


def matmul(name, a, b, *, ta=False, tb=False, out_dtypes=(F32,), epilogue=None, epi=(), tm=1024, tn=1024, tk=1024):
    m, k = (a.shape[1], a.shape[0]) if ta else a.shape
    n = b.shape[0] if tb else b.shape[1]
    assert (b.shape[1] if tb else b.shape[0]) == k
    tm, tn, tk = _tile(m, tm), _tile(n, tn), _tile(k, tk)
    nk = k // tk
    ne, no = len(epi), len(out_dtypes)
    dims = (((0 if ta else 1,), (1 if tb else 0,)), ((), ()))

    def body(*refs):
        a_ref, b_ref = refs[0], refs[1]
        epi_refs = refs[2:2 + ne]
        out_refs = refs[2 + ne:2 + ne + no]
        part = lax.dot_general(a_ref[...].astype(BF16), b_ref[...].astype(BF16), dims, preferred_element_type=F32)

        def finish(acc):
            res = (acc,) if epilogue is None else epilogue(acc, *[r[...] for r in epi_refs])
            for r, v in zip(out_refs, res):
                r[...] = v.astype(r.dtype)

        if nk == 1:
            finish(part)
        else:
            acc_ref = refs[-1]
            kk = pl.program_id(2)

            @pl.when(kk == 0)
            def _():
                acc_ref[...] = part

            @pl.when(kk > 0)
            def _():
                acc_ref[...] += part

            @pl.when(kk == nk - 1)
            def _():
                finish(acc_ref[...])

    a_spec = pl.BlockSpec((tk, tm), lambda i, j, kk: (kk, i)) if ta else pl.BlockSpec((tm, tk), lambda i, j, kk: (i, kk))
    b_spec = pl.BlockSpec((tn, tk), lambda i, j, kk: (j, kk)) if tb else pl.BlockSpec((tk, tn), lambda i, j, kk: (kk, j))
    epi_specs = []
    for (arr, col0) in epi:
        assert col0 % tn == 0
        epi_specs.append(pl.BlockSpec((tm, tn), functools.partial(lambda i, j, kk, c0: (i, j + c0), c0=col0 // tn)))
    res = pl.pallas_call(
        body, name=name, grid=(m // tm, n // tn, nk),
        in_specs=[a_spec, b_spec] + epi_specs,
        out_specs=[pl.BlockSpec((tm, tn), lambda i, j, kk: (i, j)) for _ in out_dtypes],
        out_shape=[jax.ShapeDtypeStruct((m, n), dt) for dt in out_dtypes],
        scratch_shapes=[pltpu.VMEM((tm, tn), F32)] if nk > 1 else [],
        compiler_params=_params(("parallel", "parallel", "arbitrary")),
    )(a, b, *[arr for (arr, _) in epi])
    return res[0] if no == 1 else res


def _rms(x, g):
    return x * lax.rsqrt(jnp.mean(x * x, axis=-1, keepdims=True) + NORM_EPS) * g


def _ln(x, g, b):
    mu = jnp.mean(x, axis=-1, keepdims=True)
    xc = x - mu
    var = jnp.mean(xc * xc, axis=-1, keepdims=True)
    return xc * lax.rsqrt(var + NORM_EPS) * g + b


def _gelu(x):
    return 0.5 * x * (1.0 + jnp.tanh(math.sqrt(2.0 / math.pi) * (x + 0.044715 * (x * x * x))))


def _sigmoid(x):
    return 1.0 / (1.0 + jnp.exp(-x))


def _silu(x):
    return x * _sigmoid(x)


def _log_sigmoid(x):
    return jnp.minimum(x, 0.0) - jnp.log(1.0 + jnp.exp(-jnp.abs(x)))


def _f_pre(x, g, sc, sh):
    return (_rms(x, g) * (1.0 + sc) + sh,)


def _f_post(y, g, gt):
    return (gt * _rms(y, g),)


def _f_a1(u_raw, v_raw, g, b):
    return _gelu(u_raw), _ln(_gelu(v_raw), g, b)


def _f_glu(val, gate):
    return (val * _sigmoid(gate),)


def _f_lnsilu(zc, g, b):
    return (_silu(_ln(zc, g, b)),)


def _f_merge(g0, g1, g2, ya, yb, yc):
    return (_sigmoid(g0) * ya + _sigmoid(g1) * yb + _sigmoid(g2) * yc,)


def _lane_lt64(shape):
    return lax.broadcasted_iota(jnp.int32, shape, 1) < HEAD_DIM


def spatial_fwd(vln, u, w_bf, b_exp, rows_per_step=512):
    s = vln.shape[0]
    tr = min(rows_per_step, s)

    def body(v_ref, u_ref, w_ref, b_ref, sv_ref, ya_ref):
        lo = _lane_lt64((CHUNK, LANES))
        for ch in range(tr // CHUNK):
            r0 = ch * CHUNK
            for p in range(D_MIX // LANES):
                vp = v_ref[r0:r0 + CHUNK, p * LANES:(p + 1) * LANES]
                o0 = jnp.dot(w_ref[2 * p], vp, preferred_element_type=F32)
                o1 = jnp.dot(w_ref[2 * p + 1], vp, preferred_element_type=F32)
                sv = jnp.where(lo, o0, o1) + b_ref[:, p * LANES:(p + 1) * LANES]
                sv_ref[r0:r0 + CHUNK, p * LANES:(p + 1) * LANES] = sv
                ya_ref[r0:r0 + CHUNK, p * LANES:(p + 1) * LANES] = (
                    u_ref[r0:r0 + CHUNK, p * LANES:(p + 1) * LANES] * sv).astype(BF16)

    row = pl.BlockSpec((tr, D_MIX), lambda i: (i, 0))
    return pl.pallas_call(
        body, name="spatial_fwd", grid=(s // tr,),
        in_specs=[row, row, pl.BlockSpec(w_bf.shape, lambda i: (0, 0, 0)), pl.BlockSpec(b_exp.shape, lambda i: (0, 0))],
        out_specs=[row, row],
        out_shape=[jax.ShapeDtypeStruct((s, D_MIX), F32), jax.ShapeDtypeStruct((s, D_MIX), BF16)],
        compiler_params=_params(("parallel",)),
    )(vln, u, w_bf, b_exp)


def spatial_bwd(dya, u, sv, vln, wt_bf, rows_per_step=512):
    s = vln.shape[0]
    tr = min(rows_per_step, s)
    ng = wt_bf.shape[0]

    def body(dya_ref, u_ref, sv_ref, v_ref, wt_ref, du_ref, dv_ref, dw_ref, db_ref):
        @pl.when(pl.program_id(0) == 0)
        def _():
            dw_ref[...] = jnp.zeros(dw_ref.shape, F32)
            db_ref[...] = jnp.zeros(db_ref.shape, F32)

        lo = _lane_lt64((CHUNK, LANES))
        for ch in range(tr // CHUNK):
            r0 = ch * CHUNK
            for p in range(D_MIX // LANES):
                cs = slice(p * LANES, (p + 1) * LANES)
                dya_p = dya_ref[r0:r0 + CHUNK, cs].astype(F32)
                du_ref[r0:r0 + CHUNK, cs] = dya_p * sv_ref[r0:r0 + CHUNK, cs]
                dsv = dya_p * u_ref[r0:r0 + CHUNK, cs]
                db_ref[:, cs] += dsv
                dsv0 = jnp.where(lo, dsv, 0.0).astype(BF16)
                dsv1 = jnp.where(lo, 0.0, dsv).astype(BF16)
                vp = v_ref[r0:r0 + CHUNK, cs]
                d0 = jnp.dot(wt_ref[2 * p], dsv0, preferred_element_type=F32)
                d1 = jnp.dot(wt_ref[2 * p + 1], dsv1, preferred_element_type=F32)
                dv_ref[r0:r0 + CHUNK, cs] = d0 + d1
                nt = (((1,), (1,)), ((), ()))
                dw_ref[2 * p] += lax.dot_general(dsv0, vp, nt, preferred_element_type=F32)
                dw_ref[2 * p + 1] += lax.dot_general(dsv1, vp, nt, preferred_element_type=F32)

    row = pl.BlockSpec((tr, D_MIX), lambda i: (i, 0))
    return pl.pallas_call(
        body, name="spatial_bwd", grid=(s // tr,),
        in_specs=[row, row, row, row, pl.BlockSpec(wt_bf.shape, lambda i: (0, 0, 0))],
        out_specs=[row, row, pl.BlockSpec((ng, CHUNK, CHUNK), lambda i: (0, 0, 0)), pl.BlockSpec((CHUNK, D_MIX), lambda i: (0, 0))],
        out_shape=[jax.ShapeDtypeStruct((s, D_MIX), F32), jax.ShapeDtypeStruct((s, D_MIX), F32),
                   jax.ShapeDtypeStruct((ng, CHUNK, CHUNK), F32), jax.ShapeDtypeStruct((CHUNK, D_MIX), F32)],
        compiler_params=_params(("arbitrary",)),
    )(dya, u, sv, vln, wt_bf)


def conv_fwd(proj, cb_val, cb_gate, w_pad, cb, ln_g, ln_b, ts=256):
    s = proj.shape[0]
    ts = min(ts, s)
    per = ts // CONV_HALO

    def body(val_ref, gate_ref, pval_ref, pgate_ref, w_ref, cb_ref, g_ref, b_ref, zc_ref, yb_ref, ext_ref):
        i = pl.program_id(0)
        zprev = pval_ref[...] * _sigmoid(pgate_ref[...])
        ext_ref[0:CONV_HALO, :] = jnp.where(i > 0, zprev, 0.0)
        ext_ref[CONV_HALO:, :] = val_ref[...] * _sigmoid(gate_ref[...])
        acc = jnp.zeros((ts, D_MIX), F32)
        for j in range(CONV_WIDTH):
            off = CONV_HALO - (CONV_WIDTH - 1) + j
            acc = acc + w_ref[j:j + 1, :] * ext_ref[off:off + ts, :]
        zc = acc + cb_ref[...]
        zc_ref[...] = zc
        yb_ref[...] = _f_lnsilu(zc, g_ref[...], b_ref[...])[0].astype(BF16)

    def cur(c):
        return pl.BlockSpec((ts, D_MIX), functools.partial(lambda i, c: (i, c), c=c))

    def prev(c):
        return pl.BlockSpec((CONV_HALO, D_MIX), functools.partial(lambda i, c: (jnp.maximum(i * per - 1, 0), c), c=c))

    const = lambda a: pl.BlockSpec(a.shape, lambda i: (0, 0))
    out = pl.BlockSpec((ts, D_MIX), lambda i: (i, 0))
    return pl.pallas_call(
        body, name="conv_fwd", grid=(s // ts,),
        in_specs=[cur(cb_val), cur(cb_gate), prev(cb_val), prev(cb_gate), const(w_pad), const(cb), const(ln_g), const(ln_b)],
        out_specs=[out, out],
        out_shape=[jax.ShapeDtypeStruct((s, D_MIX), F32), jax.ShapeDtypeStruct((s, D_MIX), BF16)],
        scratch_shapes=[pltpu.VMEM((CONV_HALO + ts, D_MIX), F32)],
        compiler_params=_params(("parallel",)),
    )(proj, proj, proj, proj, w_pad, cb, ln_g, ln_b)


def conv_bwd(proj, cb_val, cb_gate, zc, dyb, w_pad, ln_g, ln_b, ts=256):
    s = proj.shape[0]
    ts = min(ts, s)
    per = ts // CONV_HALO
    n_tiles = s // ts
    n_halo = s // CONV_HALO

    def body(val_ref, gate_ref, pval_ref, pgate_ref, zc_ref, dyb_ref, nzc_ref, ndyb_ref, w_ref, g_ref, b_ref,
             dval_ref, dgate_ref, dw_ref, dcb_ref, dg_ref, db_ref, zext_ref, dext_ref):
        i = pl.program_id(0)

        @pl.when(i == 0)
        def _():
            for r in (dw_ref, dcb_ref, dg_ref, db_ref):
                r[...] = jnp.zeros(r.shape, F32)

        g, b = g_ref[...], b_ref[...]
        _, vjp = jax.vjp(lambda z, gg, bb: _f_lnsilu(z, gg, bb)[0], zc_ref[...], g, b)
        dzc, dg, db = vjp(dyb_ref[...].astype(F32))
        dg_ref[...] += dg
        db_ref[...] += db
        dcb_ref[...] += jnp.sum(dzc, axis=0, keepdims=True)
        _, vjp_n = jax.vjp(lambda z: _f_lnsilu(z, g, b)[0], nzc_ref[...])
        (dzc_next,) = vjp_n(ndyb_ref[...].astype(F32))
        dext_ref[0:ts, :] = dzc
        dext_ref[ts:, :] = jnp.where(i < n_tiles - 1, dzc_next, 0.0)
        val, gate = val_ref[...], gate_ref[...]
        zprev = pval_ref[...] * _sigmoid(pgate_ref[...])
        zext_ref[0:CONV_HALO, :] = jnp.where(i > 0, zprev, 0.0)
        zext_ref[CONV_HALO:, :] = val * _sigmoid(gate)
        dz = jnp.zeros((ts, D_MIX), F32)
        for j in range(CONV_WIDTH):
            shift = CONV_WIDTH - 1 - j
            dz = dz + w_ref[j:j + 1, :] * dext_ref[shift:shift + ts, :]
            off = CONV_HALO - shift
            dw_ref[j:j + 1, :] += jnp.sum(dzc * zext_ref[off:off + ts, :], axis=0, keepdims=True)
        _, vjp_glu = jax.vjp(lambda a, c: _f_glu(a, c)[0], val, gate)
        dval, dgate = vjp_glu(dz)
        dval_ref[...] = dval.astype(BF16)
        dgate_ref[...] = dgate.astype(BF16)

    def cur(c):
        return pl.BlockSpec((ts, D_MIX), functools.partial(lambda i, c: (i, c), c=c))

    def prev(c):
        return pl.BlockSpec((CONV_HALO, D_MIX), functools.partial(lambda i, c: (jnp.maximum(i * per - 1, 0), c), c=c))

    nxt = pl.BlockSpec((CONV_HALO, D_MIX), lambda i: (jnp.minimum((i + 1) * per, n_halo - 1), 0))
    const = lambda a: pl.BlockSpec(a.shape, lambda i: (0, 0))
    out = pl.BlockSpec((ts, D_MIX), lambda i: (i, 0))
    vec = pl.BlockSpec((1, D_MIX), lambda i: (0, 0))
    return pl.pallas_call(
        body, name="conv_bwd", grid=(n_tiles,),
        in_specs=[cur(cb_val), cur(cb_gate), prev(cb_val), prev(cb_gate), out, out, nxt, nxt, const(w_pad), const(ln_g), const(ln_b)],
        out_specs=[out, out, pl.BlockSpec((CONV_HALO, D_MIX), lambda i: (0, 0)), vec, vec, vec],
        out_shape=[jax.ShapeDtypeStruct((s, D_MIX), BF16), jax.ShapeDtypeStruct((s, D_MIX), BF16),
                   jax.ShapeDtypeStruct((CONV_HALO, D_MIX), F32)] + [jax.ShapeDtypeStruct((1, D_MIX), F32)] * 3,
        scratch_shapes=[pltpu.VMEM((CONV_HALO + ts, D_MIX), F32), pltpu.VMEM((ts + CONV_HALO, D_MIX), F32)],
        compiler_params=_params(("arbitrary",)),
    )(proj, proj, proj, proj, zc, dyb, zc, dyb, w_pad, ln_g, ln_b)


def forget_cumsum(proj, cb_f, bf_exp, t=256):
    s = proj.shape[0]
    t = min(t, s)

    def body(f_ref, bf_ref, out_ref, carry_ref):
        @pl.when(pl.program_id(0) == 0)
        def _():
            carry_ref[...] = jnp.zeros(carry_ref.shape, F32)

        lf = _log_sigmoid(f_ref[...] + bf_ref[...])
        tri = (lax.broadcasted_iota(jnp.int32, (t, t), 1) <= lax.broadcasted_iota(jnp.int32, (t, t), 0)).astype(F32)
        c = jnp.dot(tri, lf, precision=HIGHEST, preferred_element_type=F32) + carry_ref[...]
        out_ref[...] = c
        carry_ref[...] = c[t - 1:t, :]

    return pl.pallas_call(
        body, name="forget_cumsum", grid=(s // t,),
        in_specs=[pl.BlockSpec((t, D_MIX), functools.partial(lambda i, c: (i, c), c=cb_f)), pl.BlockSpec((1, D_MIX), lambda i: (0, 0))],
        out_specs=pl.BlockSpec((t, D_MIX), lambda i: (i, 0)),
        out_shape=jax.ShapeDtypeStruct((s, D_MIX), F32),
        scratch_shapes=[pltpu.VMEM((1, D_MIX), F32)],
        compiler_params=_params(("arbitrary",)),
    )(proj, bf_exp)


def forget_bwd(proj, cb_f, bf_exp, dcum, t=256):
    s = proj.shape[0]
    t = min(t, s)
    n = s // t

    def body(f_ref, bf_ref, dc_ref, df_ref, dbf_ref, carry_ref):
        @pl.when(pl.program_id(0) == 0)
        def _():
            carry_ref[...] = jnp.zeros(carry_ref.shape, F32)
            dbf_ref[...] = jnp.zeros(dbf_ref.shape, F32)

        tri = (lax.broadcasted_iota(jnp.int32, (t, t), 1) >= lax.broadcasted_iota(jnp.int32, (t, t), 0)).astype(F32)
        r = jnp.dot(tri, dc_ref[...], precision=HIGHEST, preferred_element_type=F32) + carry_ref[...]
        carry_ref[...] = r[0:1, :]
        df = r * _sigmoid(-(f_ref[...] + bf_ref[...]))
        dbf_ref[...] += jnp.sum(df, axis=0, keepdims=True)
        live = lax.broadcasted_iota(jnp.int32, (t, D_MIX), 1) % HEAD_DIM == 0
        df_ref[...] = jnp.where(live, df, 0.0).astype(BF16)

    return pl.pallas_call(
        body, name="forget_bwd", grid=(n,),
        in_specs=[pl.BlockSpec((t, D_MIX), functools.partial(lambda i, c: (n - 1 - i, c), c=cb_f)), pl.BlockSpec((1, D_MIX), lambda i: (0, 0)),
                  pl.BlockSpec((t, D_MIX), lambda i: (n - 1 - i, 0))],
        out_specs=[pl.BlockSpec((t, D_MIX), lambda i: (n - 1 - i, 0)), pl.BlockSpec((1, D_MIX), lambda i: (0, 0))],
        out_shape=[jax.ShapeDtypeStruct((s, D_MIX), BF16), jax.ShapeDtypeStruct((1, D_MIX), F32)],
        scratch_shapes=[pltpu.VMEM((1, D_MIX), F32)],
        compiler_params=_params(("arbitrary",)),
    )(proj, bf_exp, dcum)


NT = (((1,), (1,)), ((), ()))
LOG2E = math.log2(math.e)
N_PAIR = D_MIX // LANES


def _split3(x):
    hi = x.astype(BF16).astype(F32)
    mid = (x - hi).astype(BF16).astype(F32)
    return hi, mid, x - hi - mid


def _triple(li, first, vals):
    out = jnp.where(li == first, vals[0], 0.0)
    for i in (1, 2):
        out = jnp.where(li == first + i, vals[i], out)
    return out


def _lane_ids(shape):
    lane = lax.broadcasted_iota(jnp.int32, shape, 1)
    return lane, lane % HEAD_DIM, lane < HEAD_DIM


def attn_prep(proj, cb_q, cum, ts=256):
    s = proj.shape[0]
    ts = min(ts, s)
    scale = LOG2E / math.sqrt(HEAD_DIM)

    def body(q_ref, k_ref, v_ref, c_ref, qe_ref, qo_ref, ke_ref, ko_ref, ve_ref, vo_ref):
        _, li, lo = _lane_ids((ts, LANES))
        one3 = lambda first: ((li >= first) & (li < first + 3)).astype(F32)
        for p in range(N_PAIR):
            ps = slice(p * LANES, (p + 1) * LANES)
            c3 = _split3(pltpu.roll(c_ref[:, ps] * LOG2E, HEAD_DIM, axis=1))
            eq = _triple(li, 0, c3) + one3(3)
            ek = one3(0) - _triple(li, 3, c3) + one3(6)
            ev = one3(0)
            for src, even, odd, extra, mul in ((q_ref, qe_ref, qo_ref, eq, scale), (k_ref, ke_ref, ko_ref, ek, 1.0), (v_ref, ve_ref, vo_ref, ev, 1.0)):
                x = src[:, ps] * mul
                even[:, ps] = jnp.where(lo, x, extra).astype(BF16)
                odd[:, ps] = jnp.where(lo, extra, x).astype(BF16)

    col = lambda c: pl.BlockSpec((ts, D_MIX), functools.partial(lambda i, c: (i, c), c=c))
    out = pl.BlockSpec((ts, D_MIX), lambda i: (i, 0))
    return pl.pallas_call(
        body, name="attn_prep", grid=(s // ts,),
        in_specs=[col(cb_q), col(cb_q + 1), col(cb_q + 2), pl.BlockSpec((ts, D_MIX), lambda i: (i, 0))],
        out_specs=[out] * 6, out_shape=[jax.ShapeDtypeStruct((s, D_MIX), BF16)] * 6,
        compiler_params=_params(("parallel",)),
    )(proj, proj, proj, cum)


def _pair_specs(s, t):
    return pl.BlockSpec((t, LANES), lambda p, i: (i, p)), pl.BlockSpec((s, LANES), lambda p, i: (0, p))


def attn_fwd(qe, qo, ke, ko, ve, vo, tq=512):
    s = qe.shape[0]
    tq = min(tq, s)

    def body(qe_ref, qo_ref, ke_ref, ko_ref, ve_ref, vo_ref, o_ref, qbe_ref, qbo_ref):
        qi = pl.program_id(1)
        qs, k_refs, v_refs = (qe_ref[...], qo_ref[...]), (ke_ref, ko_ref), (ve_ref, vo_ref)
        causal = lax.broadcasted_iota(jnp.int32, (tq, tq), 1) <= lax.broadcasted_iota(jnp.int32, (tq, tq), 0)

        def step(j, carry, diag):
            ks = pl.multiple_of(j * tq, tq)
            new = []
            for h in range(2):
                m, l, acc = carry[h]
                sc = lax.dot_general(qs[h], k_refs[h][pl.ds(ks, tq), :], NT, preferred_element_type=F32)
                if diag:
                    sc = jnp.where(causal, sc, -jnp.inf)
                m_new = jnp.maximum(m, jnp.max(sc, axis=1, keepdims=True))
                p = jnp.exp2(sc - m_new)
                alpha = jnp.exp2(m - m_new)
                l = alpha * l + jnp.sum(p, axis=1, keepdims=True)
                acc = alpha * acc + jnp.dot(p.astype(BF16), v_refs[h][pl.ds(ks, tq), :], preferred_element_type=F32)
                new.append((m_new, l, acc))
            return tuple(new)

        init = tuple((jnp.full((tq, 1), -jnp.inf, F32), jnp.zeros((tq, 1), F32), jnp.zeros((tq, LANES), F32)) for _ in range(2))
        carry = lax.fori_loop(0, qi, lambda j, c: step(j, c, False), init)
        (m0, l0, a0), (m1, l1, a1) = step(qi, carry, True)
        _, li, lo = _lane_ids((tq, LANES))
        o_ref[...] = jnp.where(lo, a0 / l0, a1 / l1).astype(o_ref.dtype)
        lse_lanes = (li >= 6) & (li < 9)
        for q, m, l, spare, out_ref in ((qs[0], m0, l0, ~lo, qbe_ref), (qs[1], m1, l1, lo, qbo_ref)):
            neg_lse = _triple(li, 6, _split3(-(m + jnp.log(l) * LOG2E)))
            out_ref[...] = jnp.where(spare & lse_lanes, neg_lse.astype(BF16), q)

    blk, full = _pair_specs(s, tq)
    return pl.pallas_call(
        body, name="attn_fwd", grid=(N_PAIR, s // tq),
        in_specs=[blk, blk, full, full, full, full],
        out_specs=[blk] * 3, out_shape=[jax.ShapeDtypeStruct((s, D_MIX), BF16)] * 3,
        compiler_params=_params(("parallel", "parallel")),
    )(qe, qo, ke, ko, ve, vo)


def attn_dq(qbe, qbo, ke, ko, ve, vo, do, tq=512):
    s = qbe.shape[0]
    tq = min(tq, s)
    scale = 1.0 / math.sqrt(HEAD_DIM)

    def body(qe_ref, qo_ref, ke_ref, ko_ref, ve_ref, vo_ref, do_ref, dq_ref, dobe_ref, dobo_ref):
        qi = pl.program_id(1)
        _, li, lo = _lane_ids((tq, LANES))
        do_ = do_ref[...]
        qs, k_refs, v_refs = (qe_ref[...], qo_ref[...]), (ke_ref, ko_ref), (ve_ref, vo_ref)
        dos = (jnp.where(lo, do_, 0), jnp.where(lo, 0, do_))
        causal = lax.broadcasted_iota(jnp.int32, (tq, tq), 1) <= lax.broadcasted_iota(jnp.int32, (tq, tq), 0)

        def step(j, carry, diag):
            ks = pl.multiple_of(j * tq, tq)
            new = []
            for h in range(2):
                pdpk, pk, dsum = carry[h]
                kb = k_refs[h][pl.ds(ks, tq), :]
                sc = lax.dot_general(qs[h], kb, NT, preferred_element_type=F32)
                if diag:
                    sc = jnp.where(causal, sc, -jnp.inf)
                p = jnp.exp2(sc)
                pdp = p * lax.dot_general(dos[h], v_refs[h][pl.ds(ks, tq), :], NT, preferred_element_type=F32)
                new.append((pdpk + jnp.dot(pdp.astype(BF16), kb, preferred_element_type=F32),
                            pk + jnp.dot(p.astype(BF16), kb, preferred_element_type=F32),
                            dsum + jnp.sum(pdp, axis=1, keepdims=True)))
            return tuple(new)

        init = tuple((jnp.zeros((tq, LANES), F32), jnp.zeros((tq, LANES), F32), jnp.zeros((tq, 1), F32)) for _ in range(2))
        carry = lax.fori_loop(0, qi, lambda j, c: step(j, c, False), init)
        (a0, b0, s0), (a1, b1, s1) = step(qi, carry, True)
        dq_ref[...] = (jnp.where(lo, a0 - s0 * b0, a1 - s1 * b1) * scale).astype(dq_ref.dtype)
        dobe_ref[...] = jnp.where(lo, do_, _triple(li, 0, _split3(-s0)).astype(BF16))
        dobo_ref[...] = jnp.where(lo, _triple(li, 0, _split3(-s1)).astype(BF16), do_)

    blk, full = _pair_specs(s, tq)
    return pl.pallas_call(
        body, name="attn_dq", grid=(N_PAIR, s // tq),
        in_specs=[blk, blk, full, full, full, full, blk],
        out_specs=[blk] * 3, out_shape=[jax.ShapeDtypeStruct((s, D_MIX), BF16)] * 3,
        compiler_params=_params(("parallel", "parallel")),
    )(qbe, qbo, ke, ko, ve, vo, do)


def attn_dkv(ke, ko, ve, vo, qbe, qbo, dobe, dobo, tk=512):
    s = ke.shape[0]
    tk = min(tk, s)
    nq = s // tk

    def body(ke_ref, ko_ref, ve_ref, vo_ref, qe_ref, qo_ref, de_ref, do_ref, dk_ref, dv_ref, dck_ref):
        kj = pl.program_id(1)
        lo = _lane_lt64((tk, LANES))
        ks_, vs_, q_refs, d_refs = (ke_ref[...], ko_ref[...]), (ve_ref[...], vo_ref[...]), (qe_ref, qo_ref), (de_ref, do_ref)
        causal = lax.broadcasted_iota(jnp.int32, (tk, tk), 0) <= lax.broadcasted_iota(jnp.int32, (tk, tk), 1)

        def step(i, carry, diag):
            qs = pl.multiple_of(i * tk, tk)
            new = []
            for h in range(2):
                dk, dv, dck = carry[h]
                qblk = q_refs[h][pl.ds(qs, tk), :]
                dblk = d_refs[h][pl.ds(qs, tk), :]
                st = lax.dot_general(ks_[h], qblk, NT, preferred_element_type=F32)
                if diag:
                    st = jnp.where(causal, st, -jnp.inf)
                pt = jnp.exp2(st)
                dst = pt * lax.dot_general(vs_[h], dblk, NT, preferred_element_type=F32)
                new.append((dk + jnp.dot(dst.astype(BF16), qblk, preferred_element_type=F32),
                            dv + jnp.dot(pt.astype(BF16), dblk, preferred_element_type=F32),
                            dck - jnp.sum(dst, axis=1, keepdims=True)))
            return tuple(new)

        init = tuple((jnp.zeros((tk, LANES), F32), jnp.zeros((tk, LANES), F32), jnp.zeros((tk, 1), F32)) for _ in range(2))
        carry = step(kj, init, True)
        (dk0, dv0, dc0), (dk1, dv1, dc1) = lax.fori_loop(kj + 1, nq, lambda i, c: step(i, c, False), carry)
        dk_ref[...] = (jnp.where(lo, dk0, dk1) * (1.0 / LOG2E)).astype(dk_ref.dtype)
        dv_ref[...] = jnp.where(lo, dv0, dv1).astype(dv_ref.dtype)
        dck_ref[...] = jnp.where(lo, dc0, dc1)

    blk, full = _pair_specs(s, tk)
    return pl.pallas_call(
        body, name="attn_dkv", grid=(N_PAIR, nq),
        in_specs=[blk, blk, blk, blk, full, full, full, full],
        out_specs=[blk] * 3,
        out_shape=[jax.ShapeDtypeStruct((s, D_MIX), BF16), jax.ShapeDtypeStruct((s, D_MIX), BF16), jax.ShapeDtypeStruct((s, D_MIX), F32)],
        compiler_params=_params(("parallel", "parallel")),
    )(ke, ko, ve, vo, qbe, qbo, dobe, dobo)


def _pre_bwd(name, x, g, sc, sh, dh, dres):
    d = x.shape[1]

    def fn(xv, dhv, dresv, gv, scv, shv):
        _, vjp = jax.vjp(lambda *a: _f_pre(*a)[0], xv, gv, scv, shv)
        dx, dg, dsc, dsh = vjp(dhv.astype(F32))
        return dx + dresv, dg, dsc, dsh

    return rowwise(name, fn, [(x, 0, d), (dh, 0, d), (dres, 0, d)], [g, sc, sh], [(d, F32)], accs=[(1, d)] * 3)


def layer_fwd(x, mod, layer):
    s, d = x.shape
    m = D_MIX
    w = dict(layer["small"])
    sh1, sc1, gt1, sh2, sc2, gt2 = (mod[i:i + 1] for i in range(6))
    cb = 3 * d // m
    (h,) = rowwise("pre1", _f_pre, [(x, 0, d)], [w["mix_pre_g"], sc1, sh1], [(d, BF16)])
    w.update(layer["get_in"](h))
    proj = matmul("w_in", h, w["w_in"])
    w.update(layer["get_rest"](proj))
    u, vln = rowwise("gmlp_in", _f_a1, [(proj, cb, m), (proj, cb + 1, m)], [w["gmlp_ln_g"], w["gmlp_ln_b"]], [(m, F32), (m, BF16)])
    sv, ya = spatial_fwd(vln, u, w["ws"], w["bs_exp"])
    y_a = matmul("w_a", ya, w["w_a_out"])
    zc, yb = conv_fwd(proj, cb + 2, cb + 3, w["conv_w"], w["conv_b"], w["conv_ln_g"], w["conv_ln_b"])
    y_b = matmul("w_b", yb, w["w_b_out"])
    cum = forget_cumsum(proj, cb + 7, w["bf_exp"])
    kv_ops = attn_prep(proj, cb + 4, cum)
    o, qbe, qbo = attn_fwd(*kv_ops)
    att = (qbe, qbo) + tuple(kv_ops[2:])
    y_c = matmul("w_c", o, w["w_c_out"])
    (merged,) = rowwise("merge", _f_merge, [(proj, 0, d), (proj, 1, d), (proj, 2, d), (y_a, 0, d), (y_b, 0, d), (y_c, 0, d)], [], [(d, BF16)])
    y = matmul("w_out", merged, w["w_out"])

    def post_pre(xv, yv, gp, gt, g2, sc, sh):
        x1 = xv + _f_post(yv, gp, gt)[0]
        return x1, _f_pre(x1, g2, sc, sh)[0]

    x1, h2 = rowwise("post1", post_pre, [(x, 0, d), (y, 0, d)], [w["mix_post_g"], gt1, w["mlp_pre_g"], sc2, sh2], [(d, F32), (d, BF16)])
    a_bf, r = matmul("w1", h2, w["mlp_w1"], out_dtypes=(BF16, BF16), epilogue=lambda acc: (acc, jnp.square(jnp.maximum(acc, 0.0))))
    y2 = matmul("w2", r, w["mlp_w2"])
    (x2,) = rowwise("post2", lambda xv, yv, g, gt: xv + _f_post(yv, g, gt)[0], [(x1, 0, d), (y2, 0, d)], [w["mlp_post_g"], gt2], [(d, F32)])
    saved = dict(w=w, x=x, h=h, proj=proj, u=u, vln=vln, sv=sv, ya=ya, y_a=y_a, zc=zc, yb=yb, y_b=y_b, att=att,
                 o=o, y_c=y_c, merged=merged, y=y, x1=x1, h2=h2, a_bf=a_bf, r=r, y2=y2)
    return x2, saved


def layer_bwd(dx2, mod, sv, emit):
    x, proj, w = sv["x"], sv["proj"], sv["w"]
    s, d = x.shape
    m = D_MIX
    sh1, sc1, gt1, sh2, sc2, gt2 = (mod[i:i + 1] for i in range(6))
    cb = 3 * d // m
    g = {}
    dy2, g["mlp_post_g"], dgt2 = rowwise_vjp("post2_b", _f_post, [(sv["y2"], 0, d)], [w["mlp_post_g"], gt2], [(dx2, 0, d)], [BF16])
    da = matmul("w2_dx", dy2, w["mlp_w2"], tb=True, out_dtypes=(BF16,),
                epilogue=lambda acc, a: (acc * (2.0 * jnp.maximum(a.astype(F32), 0.0)),), epi=[(sv["a_bf"], 0)])
    big = {}
    big["mlp_w2"] = matmul("w2_dw", sv["r"], dy2, ta=True, out_dtypes=(BF16,))
    dh2 = matmul("w1_dx", da, w["mlp_w1"], tb=True)
    big["mlp_w1"] = matmul("w1_dw", sv["h2"], da, ta=True, out_dtypes=(BF16,))
    dx1, g["mlp_pre_g"], dsc2, dsh2 = _pre_bwd("pre2_b", sv["x1"], w["mlp_pre_g"], sc2, sh2, dh2, dx2)
    dy, g["mix_post_g"], dgt1 = rowwise_vjp("post1_b", _f_post, [(sv["y"], 0, d)], [w["mix_post_g"], gt1], [(dx1, 0, d)], [BF16])
    dmerged = matmul("w_out_dx", dy, w["w_out"], tb=True)
    big["w_out"] = matmul("w_out_dw", sv["merged"], dy, ta=True, out_dtypes=(BF16,))
    dg0, dg1, dg2, dya_, dyb_, dyc_ = rowwise_vjp(
        "merge_b", _f_merge, [(proj, 0, d), (proj, 1, d), (proj, 2, d), (sv["y_a"], 0, d), (sv["y_b"], 0, d), (sv["y_c"], 0, d)], [],
        [(dmerged, 0, d)], [BF16] * 6)
    dya_pre = matmul("w_a_dx", dya_, w["w_a_out"], tb=True)
    big["w_a_out"] = matmul("w_a_dw", sv["ya"], dya_, ta=True, out_dtypes=(BF16,))
    dyb_pre = matmul("w_b_dx", dyb_, w["w_b_out"], tb=True)
    big["w_b_out"] = matmul("w_b_dw", sv["yb"], dyb_, ta=True, out_dtypes=(BF16,))
    do = matmul("w_c_dx", dyc_, w["w_c_out"], tb=True, out_dtypes=(BF16,))
    big["w_c_out"] = matmul("w_c_dw", sv["o"], dyc_, ta=True, out_dtypes=(BF16,))
    emit("rest", big)
    qbe, qbo, ke, ko, ve, vo = sv["att"]
    dq, dobe, dobo = attn_dq(qbe, qbo, ke, ko, ve, vo, do)
    dk, dv, dcum = attn_dkv(ke, ko, ve, vo, qbe, qbo, dobe, dobo)
    df, dbf = forget_bwd(proj, cb + 7, w["bf_exp"], dcum)
    g["fox_bf"] = dbf[0, ::HEAD_DIM]
    dval, dgate, dwc, g["conv_b"], g["conv_ln_g"], g["conv_ln_b"] = conv_bwd(
        proj, cb + 2, cb + 3, sv["zc"], dyb_pre, w["conv_w"], w["conv_ln_g"], w["conv_ln_b"])
    g["conv_w"] = dwc[:CONV_WIDTH]
    du, dvln, dws, dbexp = spatial_bwd(dya_pre, sv["u"], sv["sv"], sv["vln"], w["ws_t"])
    g["gmlp_ws"] = dws * jnp.tril(jnp.ones((CHUNK, CHUNK), F32))
    g["gmlp_bs"] = dbexp.reshape(CHUNK, m // GROUP_DIM, GROUP_DIM).sum(-1).T
    du_raw, dv_raw, g["gmlp_ln_g"], g["gmlp_ln_b"] = rowwise_vjp(
        "gmlp_in_b", _f_a1, [(proj, cb, m), (proj, cb + 1, m)], [w["gmlp_ln_g"], w["gmlp_ln_b"]], [(du, 0, m), (dvln, 0, m)], [BF16, BF16])
    dproj = jnp.concatenate([dg0, dg1, dg2, du_raw, dv_raw, dval, dgate, dq, dk, dv, df], axis=1)
    emit("in", {"w_in": matmul("w_in_dw", sv["h"], dproj, ta=True, out_dtypes=(BF16,))})
    dh = matmul("w_in_dx", dproj, w["w_in"], tb=True)
    dx, g["mix_pre_g"], dsc1, dsh1 = _pre_bwd("pre1_b", x, w["mix_pre_g"], sc1, sh1, dh, dx1)
    dmod = jnp.concatenate([dsh1, dsc1, dgt1, dsh2, dsc2, dgt2], axis=0)
    return dx, dmod, g


def local_step(x, target, mods, layers):
    d = x.shape[1]
    saved = []
    for l in range(len(layers)):
        x, sv = layer_fwd(x, mods[l], layers[l])
        saved.append(sv)

    def loss_fn(xv, tv):
        err = xv - tv
        return err * (1.0 / d), jnp.sum(err * err, axis=0, keepdims=True)

    dx, sq = rowwise("loss", loss_fn, [(x, 0, d), (target, 0, d)], [], [(d, F32)], accs=[(1, d)])
    loss = (0.5 / d) * jnp.sum(sq)
    dmods, grads = [None] * len(layers), [None] * len(layers)
    for l in reversed(range(len(layers))):
        dx, dmods[l], grads[l] = layer_bwd(dx, mods[l], saved[l], layers[l]["emit"])
    return loss, dx, dmods, grads


def exchange(name, arrs, scatter):
    n = len(arrs)

    def body(*refs):
        in_refs, out_refs = refs[:n], refs[n:2 * n]
        send_sems, recv_sems, local_sems = refs[2 * n:]
        x, y, c = lax.axis_index("x"), lax.axis_index("y"), lax.axis_index("c")
        me = 4 * x + 2 * y + c
        local = []
        for a in range(n):
            src = in_refs[a].at[me] if scatter else in_refs[a]
            cp = pltpu.make_async_copy(src, out_refs[a].at[me], local_sems.at[a])
            cp.start()
            local.append(cp)
        remote = []
        for k in range(1, N_DEV):
            px, py, pc = x ^ ((k >> 2) & 1), y ^ ((k >> 1) & 1), c ^ (k & 1)
            peer = 4 * px + 2 * py + pc
            for a in range(n):
                src = in_refs[a].at[peer] if scatter else in_refs[a]
                cp = pltpu.make_async_remote_copy(
                    src_ref=src, dst_ref=out_refs[a].at[me], send_sem=send_sems.at[a * (N_DEV - 1) + k - 1],
                    recv_sem=recv_sems.at[a * (N_DEV - 1) + k - 1], device_id=(px, py, pc), device_id_type=MESH)
                cp.start()
                remote.append(cp)
        for cp in remote:
            cp.wait()
        for cp in local:
            cp.wait()

    hbm = pl.BlockSpec(memory_space=pltpu.HBM)
    out_shape = [jax.ShapeDtypeStruct(a.shape if scatter else (N_DEV,) + a.shape, a.dtype) for a in arrs]
    return pl.pallas_call(
        body, name=name, in_specs=[hbm] * n, out_specs=[hbm] * n, out_shape=out_shape,
        scratch_shapes=[pltpu.SemaphoreType.DMA((n * (N_DEV - 1),)), pltpu.SemaphoreType.DMA((n * (N_DEV - 1),)),
                        pltpu.SemaphoreType.DMA((n,))],
    )(*arrs)


def _peers(x, y, c):
    out = []
    for k in range(1, N_DEV):
        px, py, pc = x ^ ((k >> 2) & 1), y ^ ((k >> 1) & 1), c ^ (k & 1)
        out.append((k - 1, (px, py, pc), 4 * px + 2 * py + pc))
    return out


def _exchange_copies(srcs, lands, send_sems, recv_sems, scatter):
    x, y, c = lax.axis_index("x"), lax.axis_index("y"), lax.axis_index("c")
    me = 4 * x + 2 * y + c
    copies = []
    for slot, pos, peer in _peers(x, y, c):
        for a, (src, land) in enumerate(zip(srcs, lands)):
            copies.append(pltpu.make_async_remote_copy(
                src_ref=src.at[peer] if scatter else src, dst_ref=land.at[me],
                send_sem=send_sems.at[a * (N_DEV - 1) + slot], recv_sem=recv_sems.at[a * (N_DEV - 1) + slot],
                device_id=pos, device_id_type=MESH))
    return me, copies


def exchange_start(name, arrs, scatter):
    n = len(arrs)
    lands = [lax.empty(a.shape if scatter else (N_DEV,) + a.shape, a.dtype) for a in arrs]

    def body(*refs):
        srcs, lands_ = refs[:n], refs[n:2 * n]
        send_sems, recv_sems = refs[2 * n], refs[2 * n + 1]
        token = refs[4 * n + 2]
        _, copies = _exchange_copies(srcs, lands_, send_sems, recv_sems, scatter)
        for cp in copies:
            cp.start()
        token[...] = jnp.zeros(token.shape, token.dtype)

    hbm = pl.BlockSpec(memory_space=pltpu.HBM)
    sem = pl.BlockSpec(memory_space=pltpu.SEMAPHORE)
    n_sem = n * (N_DEV - 1)
    res = pl.pallas_call(
        body, name=name,
        out_shape=(pltpu.SemaphoreType.DMA((n_sem,)), pltpu.SemaphoreType.DMA((n_sem,)),
                   *[pltpu.HBM(a.shape, a.dtype) for a in arrs], *[pltpu.HBM(l.shape, l.dtype) for l in lands],
                   jax.ShapeDtypeStruct((8, LANES), F32)),
        in_specs=[hbm] * (2 * n), out_specs=(sem, sem, *([hbm] * (2 * n)), pl.BlockSpec(memory_space=pltpu.VMEM)),
        input_output_aliases={i: 2 + i for i in range(2 * n)},
        compiler_params=pltpu.CompilerParams(has_side_effects=pltpu.SideEffectType.DATAFLOW_SIDE_EFFECTING),
    )(*[pltpu.with_memory_space_constraint(a, pltpu.HBM) for a in arrs],
      *[pltpu.with_memory_space_constraint(l, pltpu.HBM) for l in lands])
    return dict(n=n, scatter=scatter, send=res[0], recv=res[1], srcs=res[2:2 + n], lands=res[2 + n:2 + 2 * n], token=res[2 + 2 * n])


def exchange_wait(name, st, after):
    n, scatter = st["n"], st["scatter"]

    def body(*refs):
        srcs, lands_ = refs[:n], refs[n:2 * n]
        send_sems, recv_sems = refs[2 * n], refs[2 * n + 1]
        _, copies = _exchange_copies(srcs, lands_, send_sems, recv_sems, scatter)
        for cp in copies:
            cp.wait_send()
            cp.wait_recv()

    hbm = pl.BlockSpec(memory_space=pltpu.HBM)
    sem = pl.BlockSpec(memory_space=pltpu.SEMAPHORE)
    res = pl.pallas_call(
        body, name=name,
        out_shape=tuple(pltpu.HBM(a.shape, a.dtype) for a in (*st["srcs"], *st["lands"])),
        in_specs=[hbm] * (2 * n) + [sem, sem, pl.BlockSpec(memory_space=pl.ANY)], out_specs=tuple([hbm] * (2 * n)),
        input_output_aliases={i: i for i in range(2 * n)},
        compiler_params=pltpu.CompilerParams(has_side_effects=pltpu.SideEffectType.DATAFLOW_SIDE_EFFECTING),
    )(*st["srcs"], *st["lands"], st["send"], st["recv"], after)
    return list(res[n:])


def adamw_sum(name, parts, w, m, v, tr=256):
    nl = len(parts)
    k, r, c = parts[0].shape
    tr = _tile(r, tr, 16)
    c1 = 1.0 - ADAM_B1 ** ADAM_STEP
    c2 = 1.0 - ADAM_B2 ** ADAM_STEP

    def body(*refs):
        p_refs = refs[:nl]
        w_ref, m_ref, v_ref, g_ref, d_ref, nm_ref, nv_ref = refs[nl:]
        for l in range(nl):
            @pl.when(pl.program_id(0) == l)
            def _(p_ref=p_refs[l]):
                grad = p_ref[0].astype(F32)
                for j in range(1, k):
                    grad = grad + p_ref[j].astype(F32)
                new_m = ADAM_B1 * m_ref[...] + (1.0 - ADAM_B1) * grad
                new_v = ADAM_B2 * v_ref[...] + (1.0 - ADAM_B2) * (grad * grad)
                m_hat = new_m / c1
                v_hat = new_v / c2
                g_ref[...] = grad
                d_ref[...] = -ADAM_LR * (m_hat / (jnp.sqrt(v_hat) + ADAM_EPS) + ADAM_WD * w_ref[...])
                nm_ref[...] = new_m
                nv_ref[...] = new_v

    part = lambda l: pl.BlockSpec((k, tr, c), functools.partial(lambda ll, i, l: (0, jnp.where(ll == l, i, 0), 0), l=l))
    blk = pl.BlockSpec((None, tr, c), lambda ll, i: (ll, i, 0))
    return pl.pallas_call(
        body, name=name, grid=(nl, r // tr),
        in_specs=[part(l) for l in range(nl)] + [blk, blk, blk],
        out_specs=[blk] * 4, out_shape=[jax.ShapeDtypeStruct((nl, r, c), F32)] * 4,
        compiler_params=_params(("parallel", "parallel")),
    )(*parts, w, m, v)


def ada_fwd(c_all, ada_w):
    nl, d, n = ada_w.shape

    def body(c_ref, w_ref, o_ref):
        o_ref[...] = jnp.dot(_silu(c_ref[...]), w_ref[...], precision=HIGHEST, preferred_element_type=F32)

    return pl.pallas_call(
        body, name="ada_fwd", grid=(nl,),
        in_specs=[pl.BlockSpec((N_DEV, d), lambda l: (0, 0)), pl.BlockSpec((None, d, n), lambda l: (l, 0, 0))],
        out_specs=pl.BlockSpec((None, N_DEV, n), lambda l: (l, 0, 0)),
        out_shape=jax.ShapeDtypeStruct((nl, N_DEV, n), F32),
        compiler_params=_params(("parallel",)),
    )(c_all, ada_w)


def ada_bwd(c_all_t, dmod, td=256):
    d = c_all_t.shape[0]
    nl, _, n = dmod.shape
    td = _tile(d, td, 8)

    def body(c_ref, dm_ref, o_ref):
        ca = _silu(c_ref[...])
        acc = ca[:, 0:1] * dm_ref[0:1, :]
        for b in range(1, N_DEV):
            acc = acc + ca[:, b:b + 1] * dm_ref[b:b + 1, :]
        o_ref[...] = acc

    return pl.pallas_call(
        body, name="ada_bwd", grid=(nl, d // td),
        in_specs=[pl.BlockSpec((td, N_DEV), lambda l, i: (i, 0)), pl.BlockSpec((None, N_DEV, n), lambda l, i: (l, 0, 0))],
        out_specs=pl.BlockSpec((None, td, n), lambda l, i: (l, i, 0)),
        out_shape=jax.ShapeDtypeStruct((nl, d, n), F32),
        compiler_params=_params(("parallel", "parallel")),
    )(c_all_t, dmod)


ARG_NAMES = ["x", "c", "ada_w", "ada_b", "mix_pre_g", "mix_post_g", "mlp_pre_g", "mlp_post_g", "w_in", "gmlp_ln_g", "gmlp_ln_b",
             "gmlp_ws", "gmlp_bs", "w_a_out", "conv_w", "conv_b", "conv_ln_g", "conv_ln_b", "w_b_out", "fox_bf", "w_c_out",
             "w_out", "mlp_w1", "mlp_w2"]
WEIGHTS = ARG_NAMES[2:]
COL_SHARDED = ["w_in", "w_a_out", "w_b_out", "w_c_out", "mlp_w1"]
ROW_SHARDED = ["w_out", "mlp_w2"]
BIG = COL_SHARDED + ROW_SHARDED
REST = ["w_a_out", "w_b_out", "w_c_out", "w_out", "mlp_w1", "mlp_w2"]
SMALL = ["ada_b", "mix_pre_g", "mix_post_g", "mlp_pre_g", "mlp_post_g", "gmlp_ln_g", "gmlp_ln_b", "gmlp_ws", "gmlp_bs",
         "conv_b", "conv_ln_g", "conv_ln_b", "fox_bf"]
PACK_COLS = 512


def _to_my_layout(w_in, d):
    m = D_MIX
    nf = 7 * m
    return jnp.concatenate([w_in[..., nf + N_HEADS:], w_in[..., :nf], jnp.repeat(w_in[..., nf:nf + N_HEADS], HEAD_DIM, axis=-1)], axis=-1)


def _from_my_layout(gw, d):
    m = D_MIX
    return jnp.concatenate([gw[..., 3 * d:3 * d + 7 * m], gw[..., 3 * d + 7 * m::HEAD_DIM], gw[..., :3 * d]], axis=-1)


def _pack(parts):
    flat = jnp.concatenate([p.reshape(-1).astype(F32) for p in parts])
    pad = (-flat.shape[0]) % (PACK_COLS * 8)
    return jnp.pad(flat, (0, pad)).reshape(-1, PACK_COLS)


def _unpack(packed, shapes):
    flat, out, off = packed.reshape(-1), [], 0
    for shp in shapes:
        n = math.prod(shp)
        out.append(flat[off:off + n].reshape(shp))
        off += n
    return out


def _layer_small(p, conv_full, l):
    wl = {}
    for k in ["mix_pre_g", "mix_post_g", "mlp_pre_g", "mlp_post_g", "gmlp_ln_g", "gmlp_ln_b", "conv_b", "conv_ln_g", "conv_ln_b"]:
        wl[k] = p[k][l][None, :]
    wm = p["gmlp_ws"][l] * jnp.tril(jnp.ones((CHUNK, CHUNK), F32))
    wl["ws"] = wm.astype(BF16)
    wl["ws_t"] = jnp.transpose(wm, (0, 2, 1)).astype(BF16)
    wl["bs_exp"] = jnp.repeat(p["gmlp_bs"][l].T, GROUP_DIM, axis=1)
    wl["bf_exp"] = jnp.repeat(p["fox_bf"][l], HEAD_DIM)[None, :]
    wl["conv_w"] = jnp.pad(conv_full[l], ((0, CONV_HALO - CONV_WIDTH), (0, 0)))
    return wl


def kernel(x, c, ada_w, ada_b, mix_pre_g, mix_post_g, mlp_pre_g, mlp_post_g, w_in, gmlp_ln_g, gmlp_ln_b, gmlp_ws, gmlp_bs, w_a_out, conv_w, conv_b, conv_ln_g, conv_ln_b, w_b_out, fox_bf, w_c_out, w_out, mlp_w1, mlp_w2, loss_target, m_ada_w, m_ada_b, m_mix_pre_g, m_mix_post_g, m_mlp_pre_g, m_mlp_post_g, m_w_in, m_gmlp_ln_g, m_gmlp_ln_b, m_gmlp_ws, m_gmlp_bs, m_w_a_out, m_conv_w, m_conv_b, m_conv_ln_g, m_conv_ln_b, m_w_b_out, m_fox_bf, m_w_c_out, m_w_out, m_mlp_w1, m_mlp_w2, v_ada_w, v_ada_b, v_mix_pre_g, v_mix_post_g, v_mlp_pre_g, v_mlp_post_g, v_w_in, v_gmlp_ln_g, v_gmlp_ln_b, v_gmlp_ws, v_gmlp_bs, v_w_a_out, v_conv_w, v_conv_b, v_conv_ln_g, v_conv_ln_b, v_w_b_out, v_fox_bf, v_w_c_out, v_w_out, v_mlp_w1, v_mlp_w2):
    args = (x, c, ada_w, ada_b, mix_pre_g, mix_post_g, mlp_pre_g, mlp_post_g, w_in, gmlp_ln_g, gmlp_ln_b, gmlp_ws, gmlp_bs, w_a_out,
            conv_w, conv_b, conv_ln_g, conv_ln_b, w_b_out, fox_bf, w_c_out, w_out, mlp_w1, mlp_w2)
    ms = (m_ada_w, m_ada_b, m_mix_pre_g, m_mix_post_g, m_mlp_pre_g, m_mlp_post_g, m_w_in, m_gmlp_ln_g, m_gmlp_ln_b, m_gmlp_ws, m_gmlp_bs,
          m_w_a_out, m_conv_w, m_conv_b, m_conv_ln_g, m_conv_ln_b, m_w_b_out, m_fox_bf, m_w_c_out, m_w_out, m_mlp_w1, m_mlp_w2)
    vs = (v_ada_w, v_ada_b, v_mix_pre_g, v_mix_post_g, v_mlp_pre_g, v_mlp_post_g, v_w_in, v_gmlp_ln_g, v_gmlp_ln_b, v_gmlp_ws, v_gmlp_bs,
          v_w_a_out, v_conv_w, v_conv_b, v_conv_ln_g, v_conv_ln_b, v_w_b_out, v_fox_bf, v_w_c_out, v_w_out, v_mlp_w1, v_mlp_w2)
    p = dict(zip(ARG_NAMES, args))
    mom = dict(zip(WEIGHTS, ms))
    var = dict(zip(WEIGHTS, vs))
    nl = ada_w.shape[0]
    s, d = x.shape[1], x.shape[2]
    me = 4 * lax.axis_index("x") + 2 * lax.axis_index("y") + lax.axis_index("c")

    def full_matrix(k, land, own):
        g = lax.dynamic_update_index_in_dim(land, own, me, 0)
        r, cc = own.shape
        full = jnp.transpose(g, (1, 0, 2)).reshape(r, N_DEV * cc) if k in COL_SHARDED else g.reshape(N_DEV * r, cc)
        return _to_my_layout(full, d) if k == "w_in" else full

    def fetch(l, tag, keys):
        own = [p[k][l].astype(BF16) for k in keys]
        st = exchange_start(f"gather_{tag}{l}_start", own, scatter=False)

        def get(after):
            lands = exchange_wait(f"gather_{tag}{l}_wait", st, after)
            return {k: full_matrix(k, land, o) for k, land, o in zip(keys, lands, own)}

        return get

    getters = [(fetch(l, "in", ["w_in"]), fetch(l, "rest", REST)) for l in range(nl)]

    c_all, conv_all = exchange("gather_c", [c, conv_w], scatter=False)
    c_all = c_all.reshape(N_DEV, d)
    n_ada = ada_w.shape[2]
    mod_parts = ada_fwd(c_all, ada_w)
    (mod_recv,) = exchange("scatter_mod", [jnp.transpose(mod_parts, (1, 0, 2))], scatter=True)
    mod = jnp.transpose(mod_recv, (1, 0, 2)).reshape(nl, N_DEV * n_ada) + ada_b
    mods = [mod[l].reshape(6, d) for l in range(nl)]
    conv_full = jnp.transpose(conv_all, (1, 2, 0, 3)).reshape(nl, CONV_WIDTH, D_MIX)

    sent = {}

    def emitter(l):
        def emit(tag, grads_big):
            keys = list(grads_big)
            send = []
            for k in keys:
                gk = grads_big[k]
                if k == "w_in":
                    gk = _from_my_layout(gk, d)
                if k in COL_SHARDED:
                    r, cc = gk.shape[0], gk.shape[1] // N_DEV
                    gk = jnp.transpose(gk.reshape(r, N_DEV, cc), (1, 0, 2))
                else:
                    gk = gk.reshape(N_DEV, gk.shape[0] // N_DEV, gk.shape[1])
                send.append(gk.astype(BF16))
            sent[(l, tag)] = (keys, send, exchange_start(f"scatter_{tag}{l}_start", send, scatter=True))

        return emit

    layers = [dict(small=_layer_small(p, conv_full, l), get_in=getters[l][0], get_rest=getters[l][1], emit=emitter(l))
              for l in range(nl)]
    loss_local, dx, dmods, grads = local_step(x[0], loss_target[0], mods, layers)
    loss = lax.psum(loss_local, ("x", "y", "c"))
    grad_x = dx[None]

    stack = lambda k: jnp.stack([grads[l][k] for l in range(nl)])
    dmod = jnp.stack([dmods[l].reshape(6 * d) for l in range(nl)])
    small_parts = [dmod] + [stack(k) for k in SMALL[1:]] + [stack("conv_w")]
    small_shapes = [p[k].shape for k in SMALL] + [(nl, CONV_WIDTH, D_MIX)]
    packed = _pack(small_parts)
    (small_all,) = exchange("gather_small", [packed], scatter=False)
    zeros_conv = jnp.zeros(small_shapes[-1], F32)
    packs = [_pack([src[k] for k in SMALL] + [zeros_conv])[None] for src in (p, mom, var)]
    small_out = [_unpack(o, small_shapes) for o in adamw_sum("adamw_small", [small_all], *packs)]

    out = {k: [None] * 4 for k in WEIGHTS}
    for i, k in enumerate(SMALL):
        for j in range(4):
            out[k][j] = small_out[j][i]

    def shard_update(name, parts, k):
        shp = p[k].shape
        flat = lambda a: a.reshape(nl, -1, shp[-1])
        res = adamw_sum(name, [pt.reshape((pt.shape[0],) + flat(p[k]).shape[1:]) for pt in parts], flat(p[k]), flat(mom[k]), flat(var[k]))
        out[k] = [a.reshape(shp) for a in res]

    n_conv = conv_w.shape[2]
    conv_grad = lax.dynamic_slice_in_dim(small_out[0][-1], me * n_conv, n_conv, axis=2)
    shard_update("adamw_conv_w", [conv_grad[l][None] for l in range(nl)], "conv_w")

    dmod_all = small_all.reshape(N_DEV, -1)[:, :nl * 6 * d].reshape(N_DEV, nl, 6 * d)
    dmod_mine = lax.dynamic_slice_in_dim(dmod_all, me * n_ada, n_ada, axis=2)
    g_ada = ada_bwd(c_all.T, jnp.transpose(dmod_mine, (1, 0, 2)))
    shard_update("adamw_ada_w", [g_ada[l][None] for l in range(nl)], "ada_w")

    parts = {}
    for l in range(nl):
        for tag in ("rest", "in"):
            keys, send, st = sent[(l, tag)]
            lands = exchange_wait(f"scatter_{tag}{l}_wait", st, dx)
            for k, land, sd in zip(keys, lands, send):
                own = lax.dynamic_index_in_dim(sd, me, 0, keepdims=False)
                parts[(k, l)] = lax.dynamic_update_index_in_dim(land, own, me, 0)
    for k in BIG:
        shard_update("adamw_" + k, [parts[(k, l)] for l in range(nl)], k)

    res = [loss, grad_x]
    for j in range(4):
        res += [out[k][j] for k in WEIGHTS]
    return tuple(res)
```

```python
import functools
import math

import jax
import jax.numpy as jnp
from jax import lax
from jax.experimental import pallas as pl
from jax.experimental.pallas import tpu as pltpu

F32 = jnp.float32
BF16 = jnp.bfloat16
MESH = pl.DeviceIdType.MESH
N_DEV = 8
NORM_EPS = 1e-6
D_MIX = 512
N_HEADS = 8
HEAD_DIM = 64
GROUP_DIM = 64
CHUNK = 128
CONV_WIDTH = 31
CONV_HALO = 32
LANES = 128
ADAM_LR, ADAM_B1, ADAM_B2, ADAM_EPS, ADAM_WD, ADAM_STEP = 0.001, 0.9, 0.999, 1e-08, 0.01, 10
VMEM_LIMIT = 56 * 1024 * 1024
HIGHEST = lax.Precision.HIGHEST


def _tile(dim, pref, mult=LANES):
    t = min(pref, dim)
    t -= t % mult
    while t >= mult:
        if dim % t == 0:
            return t
        t -= mult
    return dim


def _params(sem):
    return pltpu.CompilerParams(dimension_semantics=sem, vmem_limit_bytes=VMEM_LIMIT)


def rowwise(name, fn, rows, consts, outs, accs=(), ts=256):
    s = rows[0][0].shape[0]
    ts = min(ts, s)
    nr, nc, no, na = len(rows), len(consts), len(outs), len(accs)

    def body(*refs):
        vals = [r[...] for r in refs[:nr + nc]]
        res = fn(*vals)
        if not isinstance(res, (tuple, list)):
            res = (res,)
        for r, v in zip(refs[nr + nc:nr + nc + no], res[:no]):
            r[...] = v.astype(r.dtype)
        if na:
            acc_refs = refs[nr + nc + no:]

            @pl.when(pl.program_id(0) == 0)
            def _():
                for r in acc_refs:
                    r[...] = jnp.zeros(r.shape, r.dtype)

            for r, v in zip(acc_refs, res[no:]):
                r[...] += v.astype(F32)

    in_specs = [pl.BlockSpec((ts, w), functools.partial(lambda i, cb: (i, cb), cb=cb)) for (_, cb, w) in rows]
    in_specs += [pl.BlockSpec(c.shape, lambda i: (0, 0)) for c in consts]
    out_specs = [pl.BlockSpec((ts, w), lambda i: (i, 0)) for (w, _) in outs]
    out_specs += [pl.BlockSpec(shp, lambda i: (0, 0)) for shp in accs]
    out_shape = [jax.ShapeDtypeStruct((s, w), dt) for (w, dt) in outs]
    out_shape += [jax.ShapeDtypeStruct(shp, F32) for shp in accs]
    res = pl.pallas_call(
        body, name=name, grid=(s // ts,), in_specs=in_specs, out_specs=out_specs, out_shape=out_shape,
        compiler_params=_params(("arbitrary",) if na else ("parallel",)),
    )(*[a for (a, _, _) in rows], *consts)
    return res


def rowwise_vjp(name, f, rows, consts, cts, grad_dtypes, ts=256):
    nr, nc, nt = len(rows), len(consts), len(cts)
    keep = [i for i, dt in enumerate(grad_dtypes) if dt is not None]

    def g(*vals):
        rv = [v.astype(F32) for v in vals[:nr]]
        ctv = tuple(v.astype(F32) for v in vals[nr:nr + nt])
        cv = list(vals[nr + nt:])
        _, vjp = jax.vjp(lambda *a: tuple(f(*a)), *rv, *cv)
        grads = vjp(ctv)
        return tuple(grads[i] for i in keep) + tuple(grads[nr:])

    outs = [(rows[i][2], grad_dtypes[i]) for i in keep]
    return rowwise(name, g, list(rows) + list(cts), consts, outs, accs=[c.shape for c in consts], ts=ts)


def matmul(name, a, b, *, ta=False, tb=False, out_dtypes=(F32,), epilogue=None, epi=(), tm=1024, tn=1024, tk=1024, dep=None):
    m, k = (a.shape[1], a.shape[0]) if ta else a.shape
    n = b.shape[0] if tb else b.shape[1]
    assert (b.shape[1] if tb else b.shape[0]) == k
    tm, tn, tk = _tile(m, tm), _tile(n, tn), _tile(k, tk)
    nk = k // tk
    ne, no = len(epi), len(out_dtypes)
    dims = (((0 if ta else 1,), (1 if tb else 0,)), ((), ()))

    def body(*refs):
        a_ref, b_ref = refs[0], refs[1]
        epi_refs = refs[2:2 + ne]
        n_in = 2 + ne + (dep is not None)
        out_refs = refs[n_in:n_in + no]
        part = lax.dot_general(a_ref[...].astype(BF16), b_ref[...].astype(BF16), dims, preferred_element_type=F32)

        def finish(acc):
            res = (acc,) if epilogue is None else epilogue(acc, *[r[...] for r in epi_refs])
            for r, v in zip(out_refs, res):
                r[...] = v.astype(r.dtype)

        if nk == 1:
            finish(part)
        else:
            acc_ref = refs[-1]
            kk = pl.program_id(2)

            @pl.when(kk == 0)
            def _():
                acc_ref[...] = part

            @pl.when(kk > 0)
            def _():
                acc_ref[...] += part

            @pl.when(kk == nk - 1)
            def _():
                finish(acc_ref[...])

    a_spec = pl.BlockSpec((tk, tm), lambda i, j, kk: (kk, i)) if ta else pl.BlockSpec((tm, tk), lambda i, j, kk: (i, kk))
    b_spec = pl.BlockSpec((tn, tk), lambda i, j, kk: (j, kk)) if tb else pl.BlockSpec((tk, tn), lambda i, j, kk: (kk, j))
    epi_specs = []
    for (arr, col0) in epi:
        assert col0 % tn == 0
        epi_specs.append(pl.BlockSpec((tm, tn), functools.partial(lambda i, j, kk, c0: (i, j + c0), c0=col0 // tn)))
    res = pl.pallas_call(
        body, name=name, grid=(m // tm, n // tn, nk),
        in_specs=[a_spec, b_spec] + epi_specs + ([] if dep is None else [pl.BlockSpec(dep.shape, lambda i, j, kk: (0, 0))]),
        out_specs=[pl.BlockSpec((tm, tn), lambda i, j, kk: (i, j)) for _ in out_dtypes],
        out_shape=[jax.ShapeDtypeStruct((m, n), dt) for dt in out_dtypes],
        scratch_shapes=[pltpu.VMEM((tm, tn), F32)] if nk > 1 else [],
        compiler_params=_params(("parallel", "parallel", "arbitrary")),
    )(a, b, *[arr for (arr, _) in epi], *([] if dep is None else [dep]))
    return res[0] if no == 1 else res


def _rms(x, g):
    return x * lax.rsqrt(jnp.mean(x * x, axis=-1, keepdims=True) + NORM_EPS) * g


def _ln(x, g, b):
    mu = jnp.mean(x, axis=-1, keepdims=True)
    xc = x - mu
    var = jnp.mean(xc * xc, axis=-1, keepdims=True)
    return xc * lax.rsqrt(var + NORM_EPS) * g + b


def _gelu(x):
    return 0.5 * x * (1.0 + jnp.tanh(math.sqrt(2.0 / math.pi) * (x + 0.044715 * (x * x * x))))


def _sigmoid(x):
    return 1.0 / (1.0 + jnp.exp(-x))


def _silu(x):
    return x * _sigmoid(x)


def _log_sigmoid(x):
    return jnp.minimum(x, 0.0) - jnp.log(1.0 + jnp.exp(-jnp.abs(x)))


def _f_pre(x, g, sc, sh):
    return (_rms(x, g) * (1.0 + sc) + sh,)


def _f_post(y, g, gt):
    return (gt * _rms(y, g),)


def _f_a1(u_raw, v_raw, g, b):
    return _gelu(u_raw), _ln(_gelu(v_raw), g, b)


def _f_glu(val, gate):
    return (val * _sigmoid(gate),)


def _f_lnsilu(zc, g, b):
    return (_silu(_ln(zc, g, b)),)


def _f_merge(g0, g1, g2, ya, yb, yc):
    return (_sigmoid(g0) * ya + _sigmoid(g1) * yb + _sigmoid(g2) * yc,)


def _lane_lt64(shape):
    return lax.broadcasted_iota(jnp.int32, shape, 1) < HEAD_DIM


def spatial_fwd(vln, u, w_bf, b_exp, rows_per_step=512):
    s = vln.shape[0]
    tr = min(rows_per_step, s)

    def body(v_ref, u_ref, w_ref, b_ref, sv_ref, ya_ref):
        lo = _lane_lt64((CHUNK, LANES))
        for ch in range(tr // CHUNK):
            r0 = ch * CHUNK
            for p in range(D_MIX // LANES):
                vp = v_ref[r0:r0 + CHUNK, p * LANES:(p + 1) * LANES]
                o0 = jnp.dot(w_ref[2 * p], vp, preferred_element_type=F32)
                o1 = jnp.dot(w_ref[2 * p + 1], vp, preferred_element_type=F32)
                sv = jnp.where(lo, o0, o1) + b_ref[:, p * LANES:(p + 1) * LANES]
                sv_ref[r0:r0 + CHUNK, p * LANES:(p + 1) * LANES] = sv
                ya_ref[r0:r0 + CHUNK, p * LANES:(p + 1) * LANES] = (
                    u_ref[r0:r0 + CHUNK, p * LANES:(p + 1) * LANES] * sv).astype(BF16)

    row = pl.BlockSpec((tr, D_MIX), lambda i: (i, 0))
    return pl.pallas_call(
        body, name="spatial_fwd", grid=(s // tr,),
        in_specs=[row, row, pl.BlockSpec(w_bf.shape, lambda i: (0, 0, 0)), pl.BlockSpec(b_exp.shape, lambda i: (0, 0))],
        out_specs=[row, row],
        out_shape=[jax.ShapeDtypeStruct((s, D_MIX), F32), jax.ShapeDtypeStruct((s, D_MIX), BF16)],
        compiler_params=_params(("parallel",)),
    )(vln, u, w_bf, b_exp)


def spatial_bwd(dya, u, sv, vln, wt_bf, rows_per_step=512):
    s = vln.shape[0]
    tr = min(rows_per_step, s)
    ng = wt_bf.shape[0]

    def body(dya_ref, u_ref, sv_ref, v_ref, wt_ref, du_ref, dv_ref, dw_ref, db_ref):
        @pl.when(pl.program_id(0) == 0)
        def _():
            dw_ref[...] = jnp.zeros(dw_ref.shape, F32)
            db_ref[...] = jnp.zeros(db_ref.shape, F32)

        lo = _lane_lt64((CHUNK, LANES))
        for ch in range(tr // CHUNK):
            r0 = ch * CHUNK
            for p in range(D_MIX // LANES):
                cs = slice(p * LANES, (p + 1) * LANES)
                dya_p = dya_ref[r0:r0 + CHUNK, cs].astype(F32)
                du_ref[r0:r0 + CHUNK, cs] = dya_p * sv_ref[r0:r0 + CHUNK, cs]
                dsv = dya_p * u_ref[r0:r0 + CHUNK, cs]
                db_ref[:, cs] += dsv
                dsv0 = jnp.where(lo, dsv, 0.0).astype(BF16)
                dsv1 = jnp.where(lo, 0.0, dsv).astype(BF16)
                vp = v_ref[r0:r0 + CHUNK, cs]
                d0 = jnp.dot(wt_ref[2 * p], dsv0, preferred_element_type=F32)
                d1 = jnp.dot(wt_ref[2 * p + 1], dsv1, preferred_element_type=F32)
                dv_ref[r0:r0 + CHUNK, cs] = d0 + d1
                nt = (((1,), (1,)), ((), ()))
                dw_ref[2 * p] += lax.dot_general(dsv0, vp, nt, preferred_element_type=F32)
                dw_ref[2 * p + 1] += lax.dot_general(dsv1, vp, nt, preferred_element_type=F32)

    row = pl.BlockSpec((tr, D_MIX), lambda i: (i, 0))
    return pl.pallas_call(
        body, name="spatial_bwd", grid=(s // tr,),
        in_specs=[row, row, row, row, pl.BlockSpec(wt_bf.shape, lambda i: (0, 0, 0))],
        out_specs=[row, row, pl.BlockSpec((ng, CHUNK, CHUNK), lambda i: (0, 0, 0)), pl.BlockSpec((CHUNK, D_MIX), lambda i: (0, 0))],
        out_shape=[jax.ShapeDtypeStruct((s, D_MIX), F32), jax.ShapeDtypeStruct((s, D_MIX), F32),
                   jax.ShapeDtypeStruct((ng, CHUNK, CHUNK), F32), jax.ShapeDtypeStruct((CHUNK, D_MIX), F32)],
        compiler_params=_params(("arbitrary",)),
    )(dya, u, sv, vln, wt_bf)


def conv_fwd(proj, cb_val, cb_gate, w_pad, cb, ln_g, ln_b, ts=256):
    s = proj.shape[0]
    ts = min(ts, s)
    per = ts // CONV_HALO

    def body(val_ref, gate_ref, pval_ref, pgate_ref, w_ref, cb_ref, g_ref, b_ref, zc_ref, yb_ref, ext_ref):
        i = pl.program_id(0)
        zprev = pval_ref[...] * _sigmoid(pgate_ref[...])
        ext_ref[0:CONV_HALO, :] = jnp.where(i > 0, zprev, 0.0)
        ext_ref[CONV_HALO:, :] = val_ref[...] * _sigmoid(gate_ref[...])
        acc = jnp.zeros((ts, D_MIX), F32)
        for j in range(CONV_WIDTH):
            off = CONV_HALO - (CONV_WIDTH - 1) + j
            acc = acc + w_ref[j:j + 1, :] * ext_ref[off:off + ts, :]
        zc = acc + cb_ref[...]
        zc_ref[...] = zc
        yb_ref[...] = _f_lnsilu(zc, g_ref[...], b_ref[...])[0].astype(BF16)

    def cur(c):
        return pl.BlockSpec((ts, D_MIX), functools.partial(lambda i, c: (i, c), c=c))

    def prev(c):
        return pl.BlockSpec((CONV_HALO, D_MIX), functools.partial(lambda i, c: (jnp.maximum(i * per - 1, 0), c), c=c))

    const = lambda a: pl.BlockSpec(a.shape, lambda i: (0, 0))
    out = pl.BlockSpec((ts, D_MIX), lambda i: (i, 0))
    return pl.pallas_call(
        body, name="conv_fwd", grid=(s // ts,),
        in_specs=[cur(cb_val), cur(cb_gate), prev(cb_val), prev(cb_gate), const(w_pad), const(cb), const(ln_g), const(ln_b)],
        out_specs=[out, out],
        out_shape=[jax.ShapeDtypeStruct((s, D_MIX), F32), jax.ShapeDtypeStruct((s, D_MIX), BF16)],
        scratch_shapes=[pltpu.VMEM((CONV_HALO + ts, D_MIX), F32)],
        compiler_params=_params(("parallel",)),
    )(proj, proj, proj, proj, w_pad, cb, ln_g, ln_b)


def conv_bwd(proj, cb_val, cb_gate, zc, dyb, w_pad, ln_g, ln_b, ts=256):
    s = proj.shape[0]
    ts = min(ts, s)
    per = ts // CONV_HALO
    n_tiles = s // ts
    n_halo = s // CONV_HALO

    def body(val_ref, gate_ref, pval_ref, pgate_ref, zc_ref, dyb_ref, nzc_ref, ndyb_ref, w_ref, g_ref, b_ref,
             dval_ref, dgate_ref, dw_ref, dcb_ref, dg_ref, db_ref, zext_ref, dext_ref):
        i = pl.program_id(0)

        @pl.when(i == 0)
        def _():
            for r in (dw_ref, dcb_ref, dg_ref, db_ref):
                r[...] = jnp.zeros(r.shape, F32)

        g, b = g_ref[...], b_ref[...]
        _, vjp = jax.vjp(lambda z, gg, bb: _f_lnsilu(z, gg, bb)[0], zc_ref[...], g, b)
        dzc, dg, db = vjp(dyb_ref[...].astype(F32))
        dg_ref[...] += dg
        db_ref[...] += db
        dcb_ref[...] += jnp.sum(dzc, axis=0, keepdims=True)
        _, vjp_n = jax.vjp(lambda z: _f_lnsilu(z, g, b)[0], nzc_ref[...])
        (dzc_next,) = vjp_n(ndyb_ref[...].astype(F32))
        dext_ref[0:ts, :] = dzc
        dext_ref[ts:, :] = jnp.where(i < n_tiles - 1, dzc_next, 0.0)
        val, gate = val_ref[...], gate_ref[...]
        zprev = pval_ref[...] * _sigmoid(pgate_ref[...])
        zext_ref[0:CONV_HALO, :] = jnp.where(i > 0, zprev, 0.0)
        zext_ref[CONV_HALO:, :] = val * _sigmoid(gate)
        dz = jnp.zeros((ts, D_MIX), F32)
        for j in range(CONV_WIDTH):
            shift = CONV_WIDTH - 1 - j
            dz = dz + w_ref[j:j + 1, :] * dext_ref[shift:shift + ts, :]
            off = CONV_HALO - shift
            dw_ref[j:j + 1, :] += jnp.sum(dzc * zext_ref[off:off + ts, :], axis=0, keepdims=True)
        _, vjp_glu = jax.vjp(lambda a, c: _f_glu(a, c)[0], val, gate)
        dval, dgate = vjp_glu(dz)
        dval_ref[...] = dval.astype(BF16)
        dgate_ref[...] = dgate.astype(BF16)

    def cur(c):
        return pl.BlockSpec((ts, D_MIX), functools.partial(lambda i, c: (i, c), c=c))

    def prev(c):
        return pl.BlockSpec((CONV_HALO, D_MIX), functools.partial(lambda i, c: (jnp.maximum(i * per - 1, 0), c), c=c))

    nxt = pl.BlockSpec((CONV_HALO, D_MIX), lambda i: (jnp.minimum((i + 1) * per, n_halo - 1), 0))
    const = lambda a: pl.BlockSpec(a.shape, lambda i: (0, 0))
    out = pl.BlockSpec((ts, D_MIX), lambda i: (i, 0))
    vec = pl.BlockSpec((1, D_MIX), lambda i: (0, 0))
    return pl.pallas_call(
        body, name="conv_bwd", grid=(n_tiles,),
        in_specs=[cur(cb_val), cur(cb_gate), prev(cb_val), prev(cb_gate), out, out, nxt, nxt, const(w_pad), const(ln_g), const(ln_b)],
        out_specs=[out, out, pl.BlockSpec((CONV_HALO, D_MIX), lambda i: (0, 0)), vec, vec, vec],
        out_shape=[jax.ShapeDtypeStruct((s, D_MIX), BF16), jax.ShapeDtypeStruct((s, D_MIX), BF16),
                   jax.ShapeDtypeStruct((CONV_HALO, D_MIX), F32)] + [jax.ShapeDtypeStruct((1, D_MIX), F32)] * 3,
        scratch_shapes=[pltpu.VMEM((CONV_HALO + ts, D_MIX), F32), pltpu.VMEM((ts + CONV_HALO, D_MIX), F32)],
        compiler_params=_params(("arbitrary",)),
    )(proj, proj, proj, proj, zc, dyb, zc, dyb, w_pad, ln_g, ln_b)


def forget_cumsum(proj, cb_f, bf_exp, t=256):
    s = proj.shape[0]
    t = min(t, s)

    def body(f_ref, bf_ref, out_ref, carry_ref):
        @pl.when(pl.program_id(0) == 0)
        def _():
            carry_ref[...] = jnp.zeros(carry_ref.shape, F32)

        lf = _log_sigmoid(f_ref[...] + bf_ref[...])
        tri = (lax.broadcasted_iota(jnp.int32, (t, t), 1) <= lax.broadcasted_iota(jnp.int32, (t, t), 0)).astype(F32)
        c = jnp.dot(tri, lf, precision=HIGHEST, preferred_element_type=F32) + carry_ref[...]
        out_ref[...] = c
        carry_ref[...] = c[t - 1:t, :]

    return pl.pallas_call(
        body, name="forget_cumsum", grid=(s // t,),
        in_specs=[pl.BlockSpec((t, D_MIX), functools.partial(lambda i, c: (i, c), c=cb_f)), pl.BlockSpec((1, D_MIX), lambda i: (0, 0))],
        out_specs=pl.BlockSpec((t, D_MIX), lambda i: (i, 0)),
        out_shape=jax.ShapeDtypeStruct((s, D_MIX), F32),
        scratch_shapes=[pltpu.VMEM((1, D_MIX), F32)],
        compiler_params=_params(("arbitrary",)),
    )(proj, bf_exp)


def forget_bwd(proj, cb_f, bf_exp, dcum, t=256):
    s = proj.shape[0]
    t = min(t, s)
    n = s // t

    def body(f_ref, bf_ref, dc_ref, df_ref, dbf_ref, carry_ref):
        @pl.when(pl.program_id(0) == 0)
        def _():
            carry_ref[...] = jnp.zeros(carry_ref.shape, F32)
            dbf_ref[...] = jnp.zeros(dbf_ref.shape, F32)

        tri = (lax.broadcasted_iota(jnp.int32, (t, t), 1) >= lax.broadcasted_iota(jnp.int32, (t, t), 0)).astype(F32)
        r = jnp.dot(tri, dc_ref[...], precision=HIGHEST, preferred_element_type=F32) + carry_ref[...]
        carry_ref[...] = r[0:1, :]
        df = r * _sigmoid(-(f_ref[...] + bf_ref[...]))
        dbf_ref[...] += jnp.sum(df, axis=0, keepdims=True)
        live = lax.broadcasted_iota(jnp.int32, (t, D_MIX), 1) % HEAD_DIM == 0
        df_ref[...] = jnp.where(live, df, 0.0).astype(BF16)

    return pl.pallas_call(
        body, name="forget_bwd", grid=(n,),
        in_specs=[pl.BlockSpec((t, D_MIX), functools.partial(lambda i, c: (n - 1 - i, c), c=cb_f)), pl.BlockSpec((1, D_MIX), lambda i: (0, 0)),
                  pl.BlockSpec((t, D_MIX), lambda i: (n - 1 - i, 0))],
        out_specs=[pl.BlockSpec((t, D_MIX), lambda i: (n - 1 - i, 0)), pl.BlockSpec((1, D_MIX), lambda i: (0, 0))],
        out_shape=[jax.ShapeDtypeStruct((s, D_MIX), BF16), jax.ShapeDtypeStruct((1, D_MIX), F32)],
        scratch_shapes=[pltpu.VMEM((1, D_MIX), F32)],
        compiler_params=_params(("arbitrary",)),
    )(proj, bf_exp, dcum)


NT = (((1,), (1,)), ((), ()))
LOG2E = math.log2(math.e)
N_PAIR = D_MIX // LANES


def _split3(x):
    hi = x.astype(BF16).astype(F32)
    mid = (x - hi).astype(BF16).astype(F32)
    return hi, mid, x - hi - mid


def _triple(li, first, vals):
    out = jnp.where(li == first, vals[0], 0.0)
    for i in (1, 2):
        out = jnp.where(li == first + i, vals[i], out)
    return out


def _lane_ids(shape):
    lane = lax.broadcasted_iota(jnp.int32, shape, 1)
    return lane, lane % HEAD_DIM, lane < HEAD_DIM


def attn_prep(proj, cb_q, cum, ts=256):
    s = proj.shape[0]
    ts = min(ts, s)
    scale = LOG2E / math.sqrt(HEAD_DIM)

    def body(q_ref, k_ref, v_ref, c_ref, qe_ref, qo_ref, ke_ref, ko_ref, ve_ref, vo_ref):
        _, li, lo = _lane_ids((ts, LANES))
        one3 = lambda first: ((li >= first) & (li < first + 3)).astype(F32)
        for p in range(N_PAIR):
            ps = slice(p * LANES, (p + 1) * LANES)
            c3 = _split3(pltpu.roll(c_ref[:, ps] * LOG2E, HEAD_DIM, axis=1))
            eq = _triple(li, 0, c3) + one3(3)
            ek = one3(0) - _triple(li, 3, c3) + one3(6)
            ev = one3(0)
            for src, even, odd, extra, mul in ((q_ref, qe_ref, qo_ref, eq, scale), (k_ref, ke_ref, ko_ref, ek, 1.0), (v_ref, ve_ref, vo_ref, ev, 1.0)):
                x = src[:, ps] * mul
                even[:, ps] = jnp.where(lo, x, extra).astype(BF16)
                odd[:, ps] = jnp.where(lo, extra, x).astype(BF16)

    col = lambda c: pl.BlockSpec((ts, D_MIX), functools.partial(lambda i, c: (i, c), c=c))
    out = pl.BlockSpec((ts, D_MIX), lambda i: (i, 0))
    return pl.pallas_call(
        body, name="attn_prep", grid=(s // ts,),
        in_specs=[col(cb_q), col(cb_q + 1), col(cb_q + 2), pl.BlockSpec((ts, D_MIX), lambda i: (i, 0))],
        out_specs=[out] * 6, out_shape=[jax.ShapeDtypeStruct((s, D_MIX), BF16)] * 6,
        compiler_params=_params(("parallel",)),
    )(proj, proj, proj, cum)


def _pair_specs(s, t):
    return pl.BlockSpec((t, LANES), lambda p, i: (i, p)), pl.BlockSpec((s, LANES), lambda p, i: (0, p))


def attn_fwd(qe, qo, ke, ko, ve, vo, tq=512):
    s = qe.shape[0]
    tq = min(tq, s)

    def body(qe_ref, qo_ref, ke_ref, ko_ref, ve_ref, vo_ref, o_ref, qbe_ref, qbo_ref):
        qi = pl.program_id(1)
        qs, k_refs, v_refs = (qe_ref[...], qo_ref[...]), (ke_ref, ko_ref), (ve_ref, vo_ref)
        causal = lax.broadcasted_iota(jnp.int32, (tq, tq), 1) <= lax.broadcasted_iota(jnp.int32, (tq, tq), 0)

        def step(j, carry, diag):
            ks = pl.multiple_of(j * tq, tq)
            new = []
            for h in range(2):
                m, l, acc = carry[h]
                sc = lax.dot_general(qs[h], k_refs[h][pl.ds(ks, tq), :], NT, preferred_element_type=F32)
                if diag:
                    sc = jnp.where(causal, sc, -jnp.inf)
                m_new = jnp.maximum(m, jnp.max(sc, axis=1, keepdims=True))
                p = jnp.exp2(sc - m_new)
                alpha = jnp.exp2(m - m_new)
                l = alpha * l + jnp.sum(p, axis=1, keepdims=True)
                acc = alpha * acc + jnp.dot(p.astype(BF16), v_refs[h][pl.ds(ks, tq), :], preferred_element_type=F32)
                new.append((m_new, l, acc))
            return tuple(new)

        init = tuple((jnp.full((tq, 1), -jnp.inf, F32), jnp.zeros((tq, 1), F32), jnp.zeros((tq, LANES), F32)) for _ in range(2))
        carry = lax.fori_loop(0, qi, lambda j, c: step(j, c, False), init)
        (m0, l0, a0), (m1, l1, a1) = step(qi, carry, True)
        _, li, lo = _lane_ids((tq, LANES))
        o_ref[...] = jnp.where(lo, a0 / l0, a1 / l1).astype(o_ref.dtype)
        lse_lanes = (li >= 6) & (li < 9)
        for q, m, l, spare, out_ref in ((qs[0], m0, l0, ~lo, qbe_ref), (qs[1], m1, l1, lo, qbo_ref)):
            neg_lse = _triple(li, 6, _split3(-(m + jnp.log(l) * LOG2E)))
            out_ref[...] = jnp.where(spare & lse_lanes, neg_lse.astype(BF16), q)

    blk, full = _pair_specs(s, tq)
    return pl.pallas_call(
        body, name="attn_fwd", grid=(N_PAIR, s // tq),
        in_specs=[blk, blk, full, full, full, full],
        out_specs=[blk] * 3, out_shape=[jax.ShapeDtypeStruct((s, D_MIX), BF16)] * 3,
        compiler_params=_params(("parallel", "parallel")),
    )(qe, qo, ke, ko, ve, vo)


def attn_dq(qbe, qbo, ke, ko, ve, vo, do, dep, tq=512):
    s = qbe.shape[0]
    tq = min(tq, s)
    scale = 1.0 / math.sqrt(HEAD_DIM)

    def body(qe_ref, qo_ref, ke_ref, ko_ref, ve_ref, vo_ref, do_ref, dep_ref, dq_ref, dobe_ref, dobo_ref):
        qi = pl.program_id(1)
        _, li, lo = _lane_ids((tq, LANES))
        do_ = do_ref[...]
        qs, k_refs, v_refs = (qe_ref[...], qo_ref[...]), (ke_ref, ko_ref), (ve_ref, vo_ref)
        dos = (jnp.where(lo, do_, 0), jnp.where(lo, 0, do_))
        causal = lax.broadcasted_iota(jnp.int32, (tq, tq), 1) <= lax.broadcasted_iota(jnp.int32, (tq, tq), 0)

        def step(j, carry, diag):
            ks = pl.multiple_of(j * tq, tq)
            new = []
            for h in range(2):
                pdpk, pk, dsum = carry[h]
                kb = k_refs[h][pl.ds(ks, tq), :]
                sc = lax.dot_general(qs[h], kb, NT, preferred_element_type=F32)
                if diag:
                    sc = jnp.where(causal, sc, -jnp.inf)
                p = jnp.exp2(sc)
                pdp = p * lax.dot_general(dos[h], v_refs[h][pl.ds(ks, tq), :], NT, preferred_element_type=F32)
                new.append((pdpk + jnp.dot(pdp.astype(BF16), kb, preferred_element_type=F32),
                            pk + jnp.dot(p.astype(BF16), kb, preferred_element_type=F32),
                            dsum + jnp.sum(pdp, axis=1, keepdims=True)))
            return tuple(new)

        init = tuple((jnp.zeros((tq, LANES), F32), jnp.zeros((tq, LANES), F32), jnp.zeros((tq, 1), F32)) for _ in range(2))
        carry = lax.fori_loop(0, qi, lambda j, c: step(j, c, False), init)
        (a0, b0, s0), (a1, b1, s1) = step(qi, carry, True)
        dq_ref[...] = (jnp.where(lo, a0 - s0 * b0, a1 - s1 * b1) * scale).astype(dq_ref.dtype)
        dobe_ref[...] = jnp.where(lo, do_, _triple(li, 0, _split3(-s0)).astype(BF16))
        dobo_ref[...] = jnp.where(lo, _triple(li, 0, _split3(-s1)).astype(BF16), do_)

    blk, full = _pair_specs(s, tq)
    return pl.pallas_call(
        body, name="attn_dq", grid=(N_PAIR, s // tq),
        in_specs=[blk, blk, full, full, full, full, blk, pl.BlockSpec(dep.shape, lambda p, i: (0, 0))],
        out_specs=[blk] * 3, out_shape=[jax.ShapeDtypeStruct((s, D_MIX), BF16)] * 3,
        compiler_params=_params(("parallel", "parallel")),
    )(qbe, qbo, ke, ko, ve, vo, do, dep)


def attn_dkv(ke, ko, ve, vo, qbe, qbo, dobe, dobo, tk=512):
    s = ke.shape[0]
    tk = min(tk, s)
    nq = s // tk

    def body(ke_ref, ko_ref, ve_ref, vo_ref, qe_ref, qo_ref, de_ref, do_ref, dk_ref, dv_ref, dck_ref):
        kj = pl.program_id(1)
        lo = _lane_lt64((tk, LANES))
        ks_, vs_, q_refs, d_refs = (ke_ref[...], ko_ref[...]), (ve_ref[...], vo_ref[...]), (qe_ref, qo_ref), (de_ref, do_ref)
        causal = lax.broadcasted_iota(jnp.int32, (tk, tk), 0) <= lax.broadcasted_iota(jnp.int32, (tk, tk), 1)

        def step(i, carry, diag):
            qs = pl.multiple_of(i * tk, tk)
            new = []
            for h in range(2):
                dk, dv, dck = carry[h]
                qblk = q_refs[h][pl.ds(qs, tk), :]
                dblk = d_refs[h][pl.ds(qs, tk), :]
                st = lax.dot_general(ks_[h], qblk, NT, preferred_element_type=F32)
                if diag:
                    st = jnp.where(causal, st, -jnp.inf)
                pt = jnp.exp2(st)
                dst = pt * lax.dot_general(vs_[h], dblk, NT, preferred_element_type=F32)
                new.append((dk + jnp.dot(dst.astype(BF16), qblk, preferred_element_type=F32),
                            dv + jnp.dot(pt.astype(BF16), dblk, preferred_element_type=F32),
                            dck - jnp.sum(dst, axis=1, keepdims=True)))
            return tuple(new)

        init = tuple((jnp.zeros((tk, LANES), F32), jnp.zeros((tk, LANES), F32), jnp.zeros((tk, 1), F32)) for _ in range(2))
        carry = step(kj, init, True)
        (dk0, dv0, dc0), (dk1, dv1, dc1) = lax.fori_loop(kj + 1, nq, lambda i, c: step(i, c, False), carry)
        dk_ref[...] = (jnp.where(lo, dk0, dk1) * (1.0 / LOG2E)).astype(dk_ref.dtype)
        dv_ref[...] = jnp.where(lo, dv0, dv1).astype(dv_ref.dtype)
        dck_ref[...] = jnp.where(lo, dc0, dc1)

    blk, full = _pair_specs(s, tk)
    return pl.pallas_call(
        body, name="attn_dkv", grid=(N_PAIR, nq),
        in_specs=[blk, blk, blk, blk, full, full, full, full],
        out_specs=[blk] * 3,
        out_shape=[jax.ShapeDtypeStruct((s, D_MIX), BF16), jax.ShapeDtypeStruct((s, D_MIX), BF16), jax.ShapeDtypeStruct((s, D_MIX), F32)],
        compiler_params=_params(("parallel", "parallel")),
    )(ke, ko, ve, vo, qbe, qbo, dobe, dobo)


def _pre_bwd(name, x, g, sc, sh, dh, dres):
    d = x.shape[1]

    def fn(xv, dhv, dresv, gv, scv, shv):
        _, vjp = jax.vjp(lambda *a: _f_pre(*a)[0], xv, gv, scv, shv)
        dx, dg, dsc, dsh = vjp(dhv.astype(F32))
        return dx + dresv, dg, dsc, dsh

    return rowwise(name, fn, [(x, 0, d), (dh, 0, d), (dres, 0, d)], [g, sc, sh], [(d, F32)], accs=[(1, d)] * 3)


def layer_fwd(x, mod, layer):
    s, d = x.shape
    m = D_MIX
    w = dict(layer["small"])
    sh1, sc1, gt1, sh2, sc2, gt2 = (mod[i:i + 1] for i in range(6))
    cb = 3 * d // m
    (h,) = rowwise("pre1", _f_pre, [(x, 0, d)], [w["mix_pre_g"], sc1, sh1], [(d, BF16)])
    w.update(layer["get_in"](h))
    proj = matmul("w_in", h, w["w_in"])
    w.update(layer["get_abco"](proj))
    u, vln = rowwise("gmlp_in", _f_a1, [(proj, cb, m), (proj, cb + 1, m)], [w["gmlp_ln_g"], w["gmlp_ln_b"]], [(m, F32), (m, BF16)])
    sv, ya = spatial_fwd(vln, u, w["ws"], w["bs_exp"])
    y_a = matmul("w_a", ya, w["w_a_out"])
    zc, yb = conv_fwd(proj, cb + 2, cb + 3, w["conv_w"], w["conv_b"], w["conv_ln_g"], w["conv_ln_b"])
    y_b = matmul("w_b", yb, w["w_b_out"])
    cum = forget_cumsum(proj, cb + 7, w["bf_exp"])
    kv_ops = attn_prep(proj, cb + 4, cum)
    o, qbe, qbo = attn_fwd(*kv_ops)
    att = (qbe, qbo) + tuple(kv_ops[2:])
    y_c = matmul("w_c", o, w["w_c_out"])
    (merged,) = rowwise("merge", _f_merge, [(proj, 0, d), (proj, 1, d), (proj, 2, d), (y_a, 0, d), (y_b, 0, d), (y_c, 0, d)], [], [(d, BF16)])
    y = matmul("w_out", merged, w["w_out"])
    w.update(layer["get_mlp"](y))

    def post_pre(xv, yv, gp, gt, g2, sc, sh):
        x1 = xv + _f_post(yv, gp, gt)[0]
        return x1, _f_pre(x1, g2, sc, sh)[0]

    x1, h2 = rowwise("post1", post_pre, [(x, 0, d), (y, 0, d)], [w["mix_post_g"], gt1, w["mlp_pre_g"], sc2, sh2], [(d, F32), (d, BF16)])
    a_bf, r = matmul("w1", h2, w["mlp_w1"], out_dtypes=(BF16, BF16), epilogue=lambda acc: (acc, jnp.square(jnp.maximum(acc, 0.0))))
    y2 = matmul("w2", r, w["mlp_w2"])
    (x2,) = rowwise("post2", lambda xv, yv, g, gt: xv + _f_post(yv, g, gt)[0], [(x1, 0, d), (y2, 0, d)], [w["mlp_post_g"], gt2], [(d, F32)])
    saved = dict(w=w, x=x, h=h, proj=proj, u=u, vln=vln, sv=sv, ya=ya, y_a=y_a, zc=zc, yb=yb, y_b=y_b, att=att,
                 o=o, y_c=y_c, merged=merged, y=y, x1=x1, h2=h2, a_bf=a_bf, r=r, y2=y2)
    return x2, saved


def layer_bwd(dx2, mod, sv, emit, tok_in=None):
    x, proj, w = sv["x"], sv["proj"], sv["w"]
    s, d = x.shape
    m = D_MIX
    sh1, sc1, gt1, sh2, sc2, gt2 = (mod[i:i + 1] for i in range(6))
    cb = 3 * d // m
    g = {}
    if tok_in is not None:
        gt2 = gt2 + tok_in[0:1, 0:1]
    dy2, g["mlp_post_g"], dgt2 = rowwise_vjp("post2_b", _f_post, [(sv["y2"], 0, d)], [w["mlp_post_g"], gt2], [(dx2, 0, d)], [BF16])
    da = matmul("w2_dx", dy2, w["mlp_w2"], tb=True, out_dtypes=(BF16,),
                epilogue=lambda acc, a: (acc * (2.0 * jnp.maximum(a.astype(F32), 0.0)),), epi=[(sv["a_bf"], 0)])
    big = {}
    big["mlp_w2"] = matmul("w2_dw", sv["r"], dy2, ta=True, out_dtypes=(BF16,))
    dh2 = matmul("w1_dx", da, w["mlp_w1"], tb=True)
    big["mlp_w1"] = matmul("w1_dw", sv["h2"], da, ta=True, out_dtypes=(BF16,))
    tok = emit("mlp", big)
    dx1, g["mlp_pre_g"], dsc2, dsh2 = _pre_bwd("pre2_b", sv["x1"], w["mlp_pre_g"], sc2 + tok[0:1, 0:1], sh2, dh2, dx2)
    dy, g["mix_post_g"], dgt1 = rowwise_vjp("post1_b", _f_post, [(sv["y"], 0, d)], [w["mix_post_g"], gt1], [(dx1, 0, d)], [BF16])
    dmerged = matmul("w_out_dx", dy, w["w_out"], tb=True)
    big = {}
    big["w_out"] = matmul("w_out_dw", sv["merged"], dy, ta=True, out_dtypes=(BF16,))
    dg0, dg1, dg2, dya_, dyb_, dyc_ = rowwise_vjp(
        "merge_b", _f_merge, [(proj, 0, d), (proj, 1, d), (proj, 2, d), (sv["y_a"], 0, d), (sv["y_b"], 0, d), (sv["y_c"], 0, d)], [],
        [(dmerged, 0, d)], [BF16] * 6)
    dya_pre = matmul("w_a_dx", dya_, w["w_a_out"], tb=True)
    big["w_a_out"] = matmul("w_a_dw", sv["ya"], dya_, ta=True, out_dtypes=(BF16,))
    dyb_pre = matmul("w_b_dx", dyb_, w["w_b_out"], tb=True)
    big["w_b_out"] = matmul("w_b_dw", sv["yb"], dyb_, ta=True, out_dtypes=(BF16,))
    do = matmul("w_c_dx", dyc_, w["w_c_out"], tb=True, out_dtypes=(BF16,))
    big["w_c_out"] = matmul("w_c_dw", sv["o"], dyc_, ta=True, out_dtypes=(BF16,))
    tok = emit("abco", big)
    qbe, qbo, ke, ko, ve, vo = sv["att"]
    dq, dobe, dobo = attn_dq(qbe, qbo, ke, ko, ve, vo, do, tok)
    dk, dv, dcum = attn_dkv(ke, ko, ve, vo, qbe, qbo, dobe, dobo)
    df, dbf = forget_bwd(proj, cb + 7, w["bf_exp"], dcum)
    g["fox_bf"] = dbf[0, ::HEAD_DIM]
    dval, dgate, dwc, g["conv_b"], g["conv_ln_g"], g["conv_ln_b"] = conv_bwd(
        proj, cb + 2, cb + 3, sv["zc"], dyb_pre, w["conv_w"], w["conv_ln_g"], w["conv_ln_b"])
    g["conv_w"] = dwc[:CONV_WIDTH]
    du, dvln, dws, dbexp = spatial_bwd(dya_pre, sv["u"], sv["sv"], sv["vln"], w["ws_t"])
    g["gmlp_ws"] = dws * jnp.tril(jnp.ones((CHUNK, CHUNK), F32))
    g["gmlp_bs"] = dbexp.reshape(CHUNK, m // GROUP_DIM, GROUP_DIM).sum(-1).T
    du_raw, dv_raw, g["gmlp_ln_g"], g["gmlp_ln_b"] = rowwise_vjp(
        "gmlp_in_b", _f_a1, [(proj, cb, m), (proj, cb + 1, m)], [w["gmlp_ln_g"], w["gmlp_ln_b"]], [(du, 0, m), (dvln, 0, m)], [BF16, BF16])
    dproj = jnp.concatenate([dg0, dg1, dg2, du_raw, dv_raw, dval, dgate, dq, dk, dv, df], axis=1)
    tok = emit("in", {"w_in": matmul("w_in_dw", sv["h"], dproj, ta=True, out_dtypes=(BF16,))})
    dh = matmul("w_in_dx", dproj, w["w_in"], tb=True, dep=tok)
    dx, g["mix_pre_g"], dsc1, dsh1 = _pre_bwd("pre1_b", x, w["mix_pre_g"], sc1, sh1, dh, dx1)
    dmod = jnp.concatenate([dsh1, dsc1, dgt1, dsh2, dsc2, dgt2], axis=0)
    return dx, dmod, g


def local_step(x, target, mods, layers):
    d = x.shape[1]
    saved = []
    for l in range(len(layers)):
        x, sv = layer_fwd(x, mods[l], layers[l])
        saved.append(sv)

    def loss_fn(xv, tv):
        err = xv - tv
        return err * (1.0 / d), jnp.sum(err * err, axis=0, keepdims=True)

    dx, sq = rowwise("loss", loss_fn, [(x, 0, d), (target, 0, d)], [], [(d, F32)], accs=[(1, d)])
    loss = (0.5 / d) * jnp.sum(sq)
    dmods, grads = [None] * len(layers), [None] * len(layers)
    tok = None
    for l in reversed(range(len(layers))):
        dx, dmods[l], grads[l] = layer_bwd(dx, mods[l], saved[l], layers[l]["emit"], tok)
        tok = layers[l]["emit_small"](dmods[l], grads[l])
    return loss, dx, dmods, grads


def exchange(name, arrs, scatter):
    n = len(arrs)

    def body(*refs):
        in_refs, out_refs = refs[:n], refs[n:2 * n]
        send_sems, recv_sems, local_sems = refs[2 * n:]
        x, y, c = lax.axis_index("x"), lax.axis_index("y"), lax.axis_index("c")
        me = 4 * x + 2 * y + c
        local = []
        for a in range(n):
            src = in_refs[a].at[me] if scatter else in_refs[a]
            cp = pltpu.make_async_copy(src, out_refs[a].at[me], local_sems.at[a])
            cp.start()
            local.append(cp)
        remote = []
        for k in range(1, N_DEV):
            px, py, pc = x ^ ((k >> 2) & 1), y ^ ((k >> 1) & 1), c ^ (k & 1)
            peer = 4 * px + 2 * py + pc
            for a in range(n):
                src = in_refs[a].at[peer] if scatter else in_refs[a]
                cp = pltpu.make_async_remote_copy(
                    src_ref=src, dst_ref=out_refs[a].at[me], send_sem=send_sems.at[a * (N_DEV - 1) + k - 1],
                    recv_sem=recv_sems.at[a * (N_DEV - 1) + k - 1], device_id=(px, py, pc), device_id_type=MESH)
                cp.start()
                remote.append(cp)
        for cp in remote:
            cp.wait()
        for cp in local:
            cp.wait()

    hbm = pl.BlockSpec(memory_space=pltpu.HBM)
    out_shape = [jax.ShapeDtypeStruct(a.shape if scatter else (N_DEV,) + a.shape, a.dtype) for a in arrs]
    return pl.pallas_call(
        body, name=name, in_specs=[hbm] * n, out_specs=[hbm] * n, out_shape=out_shape,
        scratch_shapes=[pltpu.SemaphoreType.DMA((n * (N_DEV - 1),)), pltpu.SemaphoreType.DMA((n * (N_DEV - 1),)),
                        pltpu.SemaphoreType.DMA((n,))],
    )(*arrs)


def _peers(x, y, c):
    out = []
    for k in range(1, N_DEV):
        px, py, pc = x ^ ((k >> 2) & 1), y ^ ((k >> 1) & 1), c ^ (k & 1)
        out.append((k - 1, (px, py, pc), 4 * px + 2 * py + pc))
    return out


def _exchange_copies(srcs, lands, send_sems, recv_sems, scatter):
    x, y, c = lax.axis_index("x"), lax.axis_index("y"), lax.axis_index("c")
    me = 4 * x + 2 * y + c
    copies = []
    for slot, pos, peer in _peers(x, y, c):
        for a, (src, land) in enumerate(zip(srcs, lands)):
            copies.append(pltpu.make_async_remote_copy(
                src_ref=src.at[peer] if scatter else src, dst_ref=land.at[me],
                send_sem=send_sems.at[a * (N_DEV - 1) + slot], recv_sem=recv_sems.at[a * (N_DEV - 1) + slot],
                device_id=pos, device_id_type=MESH))
    return me, copies


def exchange_start(name, arrs, scatter):
    n = len(arrs)
    lands = [lax.empty(a.shape if scatter else (N_DEV,) + a.shape, a.dtype) for a in arrs]

    def body(*refs):
        srcs, lands_ = refs[:n], refs[n:2 * n]
        send_sems, recv_sems = refs[2 * n], refs[2 * n + 1]
        token = refs[4 * n + 2]
        _, copies = _exchange_copies(srcs, lands_, send_sems, recv_sems, scatter)
        for cp in copies:
            cp.start()
        token[...] = jnp.zeros(token.shape, token.dtype)

    hbm = pl.BlockSpec(memory_space=pltpu.HBM)
    sem = pl.BlockSpec(memory_space=pltpu.SEMAPHORE)
    n_sem = n * (N_DEV - 1)
    res = pl.pallas_call(
        body, name=name,
        out_shape=(pltpu.SemaphoreType.DMA((n_sem,)), pltpu.SemaphoreType.DMA((n_sem,)),
                   *[pltpu.HBM(a.shape, a.dtype) for a in arrs], *[pltpu.HBM(l.shape, l.dtype) for l in lands],
                   jax.ShapeDtypeStruct((8, LANES), F32)),
        in_specs=[hbm] * (2 * n), out_specs=(sem, sem, *([hbm] * (2 * n)), pl.BlockSpec(memory_space=pltpu.VMEM)),
        input_output_aliases={i: 2 + i for i in range(2 * n)},
        compiler_params=pltpu.CompilerParams(has_side_effects=pltpu.SideEffectType.DATAFLOW_SIDE_EFFECTING),
    )(*[pltpu.with_memory_space_constraint(a, pltpu.HBM) for a in arrs],
      *[pltpu.with_memory_space_constraint(l, pltpu.HBM) for l in lands])
    return dict(n=n, scatter=scatter, send=res[0], recv=res[1], srcs=res[2:2 + n], lands=res[2 + n:2 + 2 * n], token=res[2 + 2 * n])


def exchange_wait(name, st, after):
    n, scatter = st["n"], st["scatter"]

    def body(*refs):
        srcs, lands_ = refs[:n], refs[n:2 * n]
        send_sems, recv_sems = refs[2 * n], refs[2 * n + 1]
        _, copies = _exchange_copies(srcs, lands_, send_sems, recv_sems, scatter)
        for cp in copies:
            cp.wait_send()
            cp.wait_recv()

    hbm = pl.BlockSpec(memory_space=pltpu.HBM)
    sem = pl.BlockSpec(memory_space=pltpu.SEMAPHORE)
    res = pl.pallas_call(
        body, name=name,
        out_shape=tuple(pltpu.HBM(a.shape, a.dtype) for a in (*st["srcs"], *st["lands"])),
        in_specs=[hbm] * (2 * n) + [sem, sem, pl.BlockSpec(memory_space=pl.ANY)], out_specs=tuple([hbm] * (2 * n)),
        input_output_aliases={i: i for i in range(2 * n)},
        compiler_params=pltpu.CompilerParams(has_side_effects=pltpu.SideEffectType.DATAFLOW_SIDE_EFFECTING),
    )(*st["srcs"], *st["lands"], st["send"], st["recv"], after)
    return list(res[:n]), list(res[n:])


def adamw_sum(name, parts, w, m, v, tr=256):
    nl = len(parts)
    k, r, c = parts[0].shape
    tr = _tile(r, tr, 16)
    c1 = 1.0 - ADAM_B1 ** ADAM_STEP
    c2 = 1.0 - ADAM_B2 ** ADAM_STEP

    def body(*refs):
        p_refs = refs[:nl]
        w_ref, m_ref, v_ref, g_ref, d_ref, nm_ref, nv_ref = refs[nl:]
        for l in range(nl):
            @pl.when(pl.program_id(0) == l)
            def _(p_ref=p_refs[l]):
                grad = p_ref[0].astype(F32)
                for j in range(1, k):
                    grad = grad + p_ref[j].astype(F32)
                new_m = ADAM_B1 * m_ref[...] + (1.0 - ADAM_B1) * grad
                new_v = ADAM_B2 * v_ref[...] + (1.0 - ADAM_B2) * (grad * grad)
                m_hat = new_m / c1
                v_hat = new_v / c2
                g_ref[...] = grad
                d_ref[...] = -ADAM_LR * (m_hat / (jnp.sqrt(v_hat) + ADAM_EPS) + ADAM_WD * w_ref[...])
                nm_ref[...] = new_m
                nv_ref[...] = new_v

    part = lambda l: pl.BlockSpec((k, tr, c), functools.partial(lambda ll, i, l: (0, jnp.where(ll == l, i, 0), 0), l=l))
    blk = pl.BlockSpec((None, tr, c), lambda ll, i: (ll, i, 0))
    return pl.pallas_call(
        body, name=name, grid=(nl, r // tr),
        in_specs=[part(l) for l in range(nl)] + [blk, blk, blk],
        out_specs=[blk] * 4, out_shape=[jax.ShapeDtypeStruct((nl, r, c), F32)] * 4,
        compiler_params=_params(("parallel", "parallel")),
    )(*parts, w, m, v)


def ada_fwd(c_all, ada_w):
    nl, d, n = ada_w.shape

    def body(c_ref, w_ref, o_ref):
        o_ref[...] = jnp.dot(_silu(c_ref[...]), w_ref[...], precision=HIGHEST, preferred_element_type=F32)

    return pl.pallas_call(
        body, name="ada_fwd", grid=(nl,),
        in_specs=[pl.BlockSpec((N_DEV, d), lambda l: (0, 0)), pl.BlockSpec((None, d, n), lambda l: (l, 0, 0))],
        out_specs=pl.BlockSpec((None, N_DEV, n), lambda l: (l, 0, 0)),
        out_shape=jax.ShapeDtypeStruct((nl, N_DEV, n), F32),
        compiler_params=_params(("parallel",)),
    )(c_all, ada_w)


def ada_bwd(c_all_t, dmod, td=256):
    d = c_all_t.shape[0]
    nl, _, n = dmod.shape
    td = _tile(d, td, 8)

    def body(c_ref, dm_ref, o_ref):
        ca = _silu(c_ref[...])
        acc = ca[:, 0:1] * dm_ref[0:1, :]
        for b in range(1, N_DEV):
            acc = acc + ca[:, b:b + 1] * dm_ref[b:b + 1, :]
        o_ref[...] = acc

    return pl.pallas_call(
        body, name="ada_bwd", grid=(nl, d // td),
        in_specs=[pl.BlockSpec((td, N_DEV), lambda l, i: (i, 0)), pl.BlockSpec((None, N_DEV, n), lambda l, i: (l, 0, 0))],
        out_specs=pl.BlockSpec((None, td, n), lambda l, i: (l, i, 0)),
        out_shape=jax.ShapeDtypeStruct((nl, d, n), F32),
        compiler_params=_params(("parallel", "parallel")),
    )(c_all_t, dmod)


ARG_NAMES = ["x", "c", "ada_w", "ada_b", "mix_pre_g", "mix_post_g", "mlp_pre_g", "mlp_post_g", "w_in", "gmlp_ln_g", "gmlp_ln_b",
             "gmlp_ws", "gmlp_bs", "w_a_out", "conv_w", "conv_b", "conv_ln_g", "conv_ln_b", "w_b_out", "fox_bf", "w_c_out",
             "w_out", "mlp_w1", "mlp_w2"]
WEIGHTS = ARG_NAMES[2:]
COL_SHARDED = ["w_in", "w_a_out", "w_b_out", "w_c_out", "mlp_w1"]
ROW_SHARDED = ["w_out", "mlp_w2"]
BIG = COL_SHARDED + ROW_SHARDED
GROUPS = {"in": ["w_in"], "abco": ["w_a_out", "w_b_out", "w_c_out", "w_out"], "mlp": ["mlp_w1", "mlp_w2"]}
SMALL = ["ada_b", "mix_pre_g", "mix_post_g", "mlp_pre_g", "mlp_post_g", "gmlp_ln_g", "gmlp_ln_b", "gmlp_ws", "gmlp_bs",
         "conv_b", "conv_ln_g", "conv_ln_b", "fox_bf"]
PACK_COLS = 512


def _to_my_layout(w_in, d):
    m = D_MIX
    nf = 7 * m
    return jnp.concatenate([w_in[..., nf + N_HEADS:], w_in[..., :nf], jnp.repeat(w_in[..., nf:nf + N_HEADS], HEAD_DIM, axis=-1)], axis=-1)


def _from_my_layout(gw, d):
    m = D_MIX
    return jnp.concatenate([gw[..., 3 * d:3 * d + 7 * m], gw[..., 3 * d + 7 * m::HEAD_DIM], gw[..., :3 * d]], axis=-1)


def _ref_ranges(lo, hi, shard):
    out = []
    while lo < hi:
        j = lo // shard
        end = min(hi, (j + 1) * shard)
        out.append((j, lo - j * shard, end - j * shard))
        lo = end
    return out


def _w_in_to_kernel_layout(g, d):
    m = D_MIX
    nf = 7 * m
    shard = g.shape[2]
    cols = lambda lo, hi: [g[j, :, a:b] for j, a, b in _ref_ranges(lo, hi, shard)]
    forget = jnp.concatenate(cols(nf, nf + N_HEADS), axis=1)
    return jnp.concatenate(cols(nf + N_HEADS, nf + N_HEADS + 3 * d) + cols(0, nf) + [jnp.repeat(forget, HEAD_DIM, axis=1)], axis=1)


def _w_in_grad_blocks(gw, d):
    m = D_MIX
    nf = 7 * m
    n_ref = nf + N_HEADS + 3 * d
    shard = n_ref // N_DEV
    segs = [(0, nf, 3 * d, 1), (nf, nf + N_HEADS, 3 * d + nf, HEAD_DIM), (nf + N_HEADS, n_ref, 0, 1)]
    blocks = []
    for j in range(N_DEV):
        lo, hi = j * shard, (j + 1) * shard
        pieces = []
        for r0, r1, k0, stride in segs:
            a, b = max(lo, r0), min(hi, r1)
            if a < b:
                pieces.append(gw[:, k0 + (a - r0) * stride:k0 + (b - r0) * stride:stride])
        blocks.append(jnp.concatenate(pieces, axis=1) if len(pieces) > 1 else pieces[0])
    return jnp.stack(blocks)


def _pack(parts):
    flat = jnp.concatenate([p.reshape(-1).astype(F32) for p in parts])
    pad = (-flat.shape[0]) % (PACK_COLS * 8)
    return jnp.pad(flat, (0, pad)).reshape(-1, PACK_COLS)


def _unpack(packed, shapes):
    flat, out, off = packed.reshape(-1), [], 0
    for shp in shapes:
        n = math.prod(shp)
        out.append(flat[off:off + n].reshape(shp))
        off += n
    return out


def _layer_small(p, conv_full, l):
    wl = {}
    for k in ["mix_pre_g", "mix_post_g", "mlp_pre_g", "mlp_post_g", "gmlp_ln_g", "gmlp_ln_b", "conv_b", "conv_ln_g", "conv_ln_b"]:
        wl[k] = p[k][l][None, :]
    wm = p["gmlp_ws"][l] * jnp.tril(jnp.ones((CHUNK, CHUNK), F32))
    wl["ws"] = wm.astype(BF16)
    wl["ws_t"] = jnp.transpose(wm, (0, 2, 1)).astype(BF16)
    wl["bs_exp"] = jnp.repeat(p["gmlp_bs"][l].T, GROUP_DIM, axis=1)
    wl["bf_exp"] = jnp.repeat(p["fox_bf"][l], HEAD_DIM)[None, :]
    wl["conv_w"] = jnp.pad(conv_full[l], ((0, CONV_HALO - CONV_WIDTH), (0, 0)))
    return wl


def kernel(x, c, ada_w, ada_b, mix_pre_g, mix_post_g, mlp_pre_g, mlp_post_g, w_in, gmlp_ln_g, gmlp_ln_b, gmlp_ws, gmlp_bs, w_a_out, conv_w, conv_b, conv_ln_g, conv_ln_b, w_b_out, fox_bf, w_c_out, w_out, mlp_w1, mlp_w2, loss_target, m_ada_w, m_ada_b, m_mix_pre_g, m_mix_post_g, m_mlp_pre_g, m_mlp_post_g, m_w_in, m_gmlp_ln_g, m_gmlp_ln_b, m_gmlp_ws, m_gmlp_bs, m_w_a_out, m_conv_w, m_conv_b, m_conv_ln_g, m_conv_ln_b, m_w_b_out, m_fox_bf, m_w_c_out, m_w_out, m_mlp_w1, m_mlp_w2, v_ada_w, v_ada_b, v_mix_pre_g, v_mix_post_g, v_mlp_pre_g, v_mlp_post_g, v_w_in, v_gmlp_ln_g, v_gmlp_ln_b, v_gmlp_ws, v_gmlp_bs, v_w_a_out, v_conv_w, v_conv_b, v_conv_ln_g, v_conv_ln_b, v_w_b_out, v_fox_bf, v_w_c_out, v_w_out, v_mlp_w1, v_mlp_w2):
    args = (x, c, ada_w, ada_b, mix_pre_g, mix_post_g, mlp_pre_g, mlp_post_g, w_in, gmlp_ln_g, gmlp_ln_b, gmlp_ws, gmlp_bs, w_a_out,
            conv_w, conv_b, conv_ln_g, conv_ln_b, w_b_out, fox_bf, w_c_out, w_out, mlp_w1, mlp_w2)
    ms = (m_ada_w, m_ada_b, m_mix_pre_g, m_mix_post_g, m_mlp_pre_g, m_mlp_post_g, m_w_in, m_gmlp_ln_g, m_gmlp_ln_b, m_gmlp_ws, m_gmlp_bs,
          m_w_a_out, m_conv_w, m_conv_b, m_conv_ln_g, m_conv_ln_b, m_w_b_out, m_fox_bf, m_w_c_out, m_w_out, m_mlp_w1, m_mlp_w2)
    vs = (v_ada_w, v_ada_b, v_mix_pre_g, v_mix_post_g, v_mlp_pre_g, v_mlp_post_g, v_w_in, v_gmlp_ln_g, v_gmlp_ln_b, v_gmlp_ws, v_gmlp_bs,
          v_w_a_out, v_conv_w, v_conv_b, v_conv_ln_g, v_conv_ln_b, v_w_b_out, v_fox_bf, v_w_c_out, v_w_out, v_mlp_w1, v_mlp_w2)
    p = dict(zip(ARG_NAMES, args))
    mom = dict(zip(WEIGHTS, ms))
    var = dict(zip(WEIGHTS, vs))
    nl = ada_w.shape[0]
    s, d = x.shape[1], x.shape[2]
    me = 4 * lax.axis_index("x") + 2 * lax.axis_index("y") + lax.axis_index("c")

    def full_matrix(k, land, own):
        g = lax.dynamic_update_index_in_dim(land, own, me, 0)
        r, cc = own.shape
        if k == "w_in":
            return _w_in_to_kernel_layout(g, d)
        return jnp.transpose(g, (1, 0, 2)).reshape(r, N_DEV * cc) if k in COL_SHARDED else g.reshape(N_DEV * r, cc)

    tokens = []

    def fetch(l, tag):
        keys = GROUPS[tag]
        st = exchange_start(f"gather_{tag}{l}_start", [p[k][l].astype(BF16) for k in keys], scatter=False)
        tokens.append(st["token"])

        def get(after):
            owns, lands = exchange_wait(f"gather_{tag}{l}_wait", st, after)
            return {k: full_matrix(k, land, o) for k, land, o in zip(keys, lands, owns)}

        return get

    getters = [{tag: fetch(l, tag) for tag in GROUPS} for l in range(nl)]

    c_all, conv_all = exchange("gather_c", [c + sum(t[0:1, 0:1] for t in tokens), conv_w], scatter=False)
    c_all = c_all.reshape(N_DEV, d)
    n_ada = ada_w.shape[2]
    mod_parts = ada_fwd(c_all, ada_w)
    (mod_recv,) = exchange("scatter_mod", [jnp.transpose(mod_parts, (1, 0, 2))], scatter=True)
    mod = jnp.transpose(mod_recv, (1, 0, 2)).reshape(nl, N_DEV * n_ada) + ada_b
    mods = [mod[l].reshape(6, d) for l in range(nl)]
    conv_full = jnp.transpose(conv_all, (1, 2, 0, 3)).reshape(nl, CONV_WIDTH, D_MIX)

    sent = {}

    def emitter(l):
        def emit(tag, grads_big):
            keys = GROUPS[tag]
            send = []
            for k in keys:
                gk = grads_big[k]
                if k == "w_in":
                    gk = _w_in_grad_blocks(gk, d)
                elif k in COL_SHARDED:
                    r, cc = gk.shape[0], gk.shape[1] // N_DEV
                    gk = jnp.transpose(gk.reshape(r, N_DEV, cc), (1, 0, 2))
                else:
                    gk = gk.reshape(N_DEV, gk.shape[0] // N_DEV, gk.shape[1])
                send.append(gk.astype(BF16))
            sent[(l, tag)] = (keys, exchange_start(f"scatter_{tag}{l}_start", send, scatter=True))
            return sent[(l, tag)][1]["token"]

        return emit

    small_sent = {}

    def small_emitter(l):
        def emit_small(dmod_l, g):
            packed = _pack([dmod_l] + [g[k] for k in SMALL[1:]] + [g["conv_w"]])
            small_sent[l] = exchange_start(f"gather_small{l}_start", [packed], scatter=False)
            return small_sent[l]["token"]

        return emit_small

    layers = [dict(small=_layer_small(p, conv_full, l), emit_small=small_emitter(l), get_in=getters[l]["in"], get_abco=getters[l]["abco"], get_mlp=getters[l]["mlp"],
                   emit=emitter(l)) for l in range(nl)]
    loss_local, dx, dmods, grads = local_step(x[0], loss_target[0], mods, layers)
    loss = lax.psum(loss_local, ("x", "y", "c"))
    grad_x = dx[None]

    small_shapes = [p[k].shape[1:] for k in SMALL] + [(CONV_WIDTH, D_MIX)]
    small_all = []
    for l in range(nl):
        srcs, lands = exchange_wait(f"gather_small{l}_wait", small_sent[l], dx)
        small_all.append(lax.dynamic_update_index_in_dim(lands[0], srcs[0], me, 0))
    zeros_conv = jnp.zeros(small_shapes[-1], F32)
    packs = [jnp.stack([_pack([src[k][l] for k in SMALL] + [zeros_conv]) for l in range(nl)]) for src in (p, mom, var)]
    small_out = [[_unpack(o[l], small_shapes) for l in range(nl)] for o in adamw_sum("adamw_small", small_all, *packs)]

    out = {k: [None] * 4 for k in WEIGHTS}
    for i, k in enumerate(SMALL):
        for j in range(4):
            out[k][j] = jnp.stack([small_out[j][l][i] for l in range(nl)])

    def shard_update(name, parts, k):
        shp = p[k].shape
        flat = lambda a: a.reshape(nl, -1, shp[-1])
        res = adamw_sum(name, [pt.reshape((pt.shape[0],) + flat(p[k]).shape[1:]) for pt in parts], flat(p[k]), flat(mom[k]), flat(var[k]))
        out[k] = [a.reshape(shp) for a in res]

    n_conv = conv_w.shape[2]
    shard_update("adamw_conv_w", [lax.dynamic_slice_in_dim(small_out[0][l][-1], me * n_conv, n_conv, axis=1)[None] for l in range(nl)], "conv_w")

    dmod_all = jnp.stack([small_all[l].reshape(N_DEV, -1)[:, :6 * d] for l in range(nl)])
    dmod_mine = lax.dynamic_slice_in_dim(dmod_all, me * n_ada, n_ada, axis=2)
    g_ada = ada_bwd(c_all.T, dmod_mine)
    shard_update("adamw_ada_w", [g_ada[l][None] for l in range(nl)], "ada_w")

    parts = {}
    for l in reversed(range(nl)):
        for tag in ("mlp", "abco", "in"):
            keys, st = sent[(l, tag)]
            sends, lands = exchange_wait(f"scatter_{tag}{l}_wait", st, dx)
            for k, land, sd in zip(keys, lands, sends):
                own = lax.dynamic_index_in_dim(sd, me, 0, keepdims=False)
                parts[(k, l)] = lax.dynamic_update_index_in_dim(land, own, me, 0)
    for k in BIG:
        shard_update("adamw_" + k, [parts[(k, l)] for l in range(nl)], k)

    res = [loss, grad_x]
    for j in range(4):
        res += [out[k][j] for k in WEIGHTS]
    return tuple(res)
```

```python
import functools
import math

import jax
import jax.numpy as jnp
from jax import lax
from jax.experimental import pallas as pl
from jax.experimental.pallas import tpu as pltpu

F32 = jnp.float32
BF16 = jnp.bfloat16
MESH = pl.DeviceIdType.MESH
N_DEV = 8
NORM_EPS = 1e-6
D_MIX = 512
N_HEADS = 8
HEAD_DIM = 64
GROUP_DIM = 64
CHUNK = 128
CONV_WIDTH = 31
CONV_HALO = 32
LANES = 128
ADAM_LR, ADAM_B1, ADAM_B2, ADAM_EPS, ADAM_WD, ADAM_STEP = 0.001, 0.9, 0.999, 1e-08, 0.01, 10
VMEM_LIMIT = 56 * 1024 * 1024
HIGHEST = lax.Precision.HIGHEST


def _tile(dim, pref, mult=LANES):
    t = min(pref, dim)
    t -= t % mult
    while t >= mult:
        if dim % t == 0:
            return t
        t -= mult
    return dim


def _params(sem):
    return pltpu.CompilerParams(dimension_semantics=sem, vmem_limit_bytes=VMEM_LIMIT)


def rowwise(name, fn, rows, consts, outs, accs=(), ts=256):
    s = rows[0][0].shape[0]
    ts = min(ts, s)
    nr, nc, no, na = len(rows), len(consts), len(outs), len(accs)

    def body(*refs):
        vals = [r[...] for r in refs[:nr + nc]]
        res = fn(*vals)
        if not isinstance(res, (tuple, list)):
            res = (res,)
        for r, v in zip(refs[nr + nc:nr + nc + no], res[:no]):
            r[...] = v.astype(r.dtype)
        if na:
            acc_refs = refs[nr + nc + no:]

            @pl.when(pl.program_id(0) == 0)
            def _():
                for r in acc_refs:
                    r[...] = jnp.zeros(r.shape, r.dtype)

            for r, v in zip(acc_refs, res[no:]):
                r[...] += v.astype(F32)

    in_specs = [pl.BlockSpec((ts, w), functools.partial(lambda i, cb: (i, cb), cb=cb)) for (_, cb, w) in rows]
    in_specs += [pl.BlockSpec(c.shape, lambda i: (0, 0)) for c in consts]
    out_specs = [pl.BlockSpec((ts, w), lambda i: (i, 0)) for (w, _) in outs]
    out_specs += [pl.BlockSpec(shp, lambda i: (0, 0)) for shp in accs]
    out_shape = [jax.ShapeDtypeStruct((s, w), dt) for (w, dt) in outs]
    out_shape += [jax.ShapeDtypeStruct(shp, F32) for shp in accs]
    res = pl.pallas_call(
        body, name=name, grid=(s // ts,), in_specs=in_specs, out_specs=out_specs, out_shape=out_shape,
        compiler_params=_params(("arbitrary",) if na else ("parallel",)),
    )(*[a for (a, _, _) in rows], *consts)
    return res


def rowwise_vjp(name, f, rows, consts, cts, grad_dtypes, ts=256):
    nr, nc, nt = len(rows), len(consts), len(cts)
    keep = [i for i, dt in enumerate(grad_dtypes) if dt is not None]

    def g(*vals):
        rv = [v.astype(F32) for v in vals[:nr]]
        ctv = tuple(v.astype(F32) for v in vals[nr:nr + nt])
        cv = list(vals[nr + nt:])
        _, vjp = jax.vjp(lambda *a: tuple(f(*a)), *rv, *cv)
        grads = vjp(ctv)
        return tuple(grads[i] for i in keep) + tuple(grads[nr:])

    outs = [(rows[i][2], grad_dtypes[i]) for i in keep]
    return rowwise(name, g, list(rows) + list(cts), consts, outs, accs=[c.shape for c in consts], ts=ts)


def matmul(name, a, b, *, ta=False, tb=False, out_dtypes=(F32,), epilogue=None, epi=(), tm=1024, tn=1024, tk=1024, dep=None):
    m, k = (a.shape[1], a.shape[0]) if ta else a.shape
    n = b.shape[0] if tb else b.shape[1]
    assert (b.shape[1] if tb else b.shape[0]) == k
    tm, tn, tk = _tile(m, tm), _tile(n, tn), _tile(k, tk)
    nk = k // tk
    ne, no = len(epi), len(out_dtypes)
    dims = (((0 if ta else 1,), (1 if tb else 0,)), ((), ()))

    def body(*refs):
        a_ref, b_ref = refs[0], refs[1]
        epi_refs = refs[2:2 + ne]
        n_in = 2 + ne + (dep is not None)
        out_refs = refs[n_in:n_in + no]
        part = lax.dot_general(a_ref[...].astype(BF16), b_ref[...].astype(BF16), dims, preferred_element_type=F32)

        def finish(acc):
            res = (acc,) if epilogue is None else epilogue(acc, *[r[...] for r in epi_refs])
            for r, v in zip(out_refs, res):
                r[...] = v.astype(r.dtype)

        if nk == 1:
            finish(part)
        else:
            acc_ref = refs[-1]
            kk = pl.program_id(2)

            @pl.when(kk == 0)
            def _():
                acc_ref[...] = part

            @pl.when(kk > 0)
            def _():
                acc_ref[...] += part

            @pl.when(kk == nk - 1)
            def _():
                finish(acc_ref[...])

    a_spec = pl.BlockSpec((tk, tm), lambda i, j, kk: (kk, i)) if ta else pl.BlockSpec((tm, tk), lambda i, j, kk: (i, kk))
    b_spec = pl.BlockSpec((tn, tk), lambda i, j, kk: (j, kk)) if tb else pl.BlockSpec((tk, tn), lambda i, j, kk: (kk, j))
    epi_specs = []
    for (arr, col0) in epi:
        assert col0 % tn == 0
        epi_specs.append(pl.BlockSpec((tm, tn), functools.partial(lambda i, j, kk, c0: (i, j + c0), c0=col0 // tn)))
    res = pl.pallas_call(
        body, name=name, grid=(m // tm, n // tn, nk),
        in_specs=[a_spec, b_spec] + epi_specs + ([] if dep is None else [pl.BlockSpec(dep.shape, lambda i, j, kk: (0, 0))]),
        out_specs=[pl.BlockSpec((tm, tn), lambda i, j, kk: (i, j)) for _ in out_dtypes],
        out_shape=[jax.ShapeDtypeStruct((m, n), dt) for dt in out_dtypes],
        scratch_shapes=[pltpu.VMEM((tm, tn), F32)] if nk > 1 else [],
        compiler_params=_params(("parallel", "parallel", "arbitrary")),
    )(a, b, *[arr for (arr, _) in epi], *([] if dep is None else [dep]))
    return res[0] if no == 1 else res


def _rms(x, g):
    return x * lax.rsqrt(jnp.mean(x * x, axis=-1, keepdims=True) + NORM_EPS) * g


def _ln(x, g, b):
    mu = jnp.mean(x, axis=-1, keepdims=True)
    xc = x - mu
    var = jnp.mean(xc * xc, axis=-1, keepdims=True)
    return xc * lax.rsqrt(var + NORM_EPS) * g + b


def _gelu(x):
    return 0.5 * x * (1.0 + jnp.tanh(math.sqrt(2.0 / math.pi) * (x + 0.044715 * (x * x * x))))


def _sigmoid(x):
    return 1.0 / (1.0 + jnp.exp(-x))


def _silu(x):
    return x * _sigmoid(x)


def _log_sigmoid(x):
    return jnp.minimum(x, 0.0) - jnp.log(1.0 + jnp.exp(-jnp.abs(x)))


def _f_pre(x, g, sc, sh):
    return (_rms(x, g) * (1.0 + sc) + sh,)


def _f_post(y, g, gt):
    return (gt * _rms(y, g),)


def _f_a1(u_raw, v_raw, g, b):
    return _gelu(u_raw), _ln(_gelu(v_raw), g, b)


def _f_glu(val, gate):
    return (val * _sigmoid(gate),)


def _f_lnsilu(zc, g, b):
    return (_silu(_ln(zc, g, b)),)


def _f_merge(g0, g1, g2, ya, yb, yc):
    return (_sigmoid(g0) * ya + _sigmoid(g1) * yb + _sigmoid(g2) * yc,)


def _lane_lt64(shape):
    return lax.broadcasted_iota(jnp.int32, shape, 1) < HEAD_DIM


def spatial_fwd(vln, u, w_bf, b_exp, rows_per_step=512):
    s = vln.shape[0]
    tr = min(rows_per_step, s)

    def body(v_ref, u_ref, w_ref, b_ref, sv_ref, ya_ref):
        lo = _lane_lt64((CHUNK, LANES))
        for ch in range(tr // CHUNK):
            r0 = ch * CHUNK
            for p in range(D_MIX // LANES):
                vp = v_ref[r0:r0 + CHUNK, p * LANES:(p + 1) * LANES]
                o0 = jnp.dot(w_ref[2 * p], vp, preferred_element_type=F32)
                o1 = jnp.dot(w_ref[2 * p + 1], vp, preferred_element_type=F32)
                sv = jnp.where(lo, o0, o1) + b_ref[:, p * LANES:(p + 1) * LANES]
                sv_ref[r0:r0 + CHUNK, p * LANES:(p + 1) * LANES] = sv
                ya_ref[r0:r0 + CHUNK, p * LANES:(p + 1) * LANES] = (
                    u_ref[r0:r0 + CHUNK, p * LANES:(p + 1) * LANES] * sv).astype(BF16)

    row = pl.BlockSpec((tr, D_MIX), lambda i: (i, 0))
    return pl.pallas_call(
        body, name="spatial_fwd", grid=(s // tr,),
        in_specs=[row, row, pl.BlockSpec(w_bf.shape, lambda i: (0, 0, 0)), pl.BlockSpec(b_exp.shape, lambda i: (0, 0))],
        out_specs=[row, row],
        out_shape=[jax.ShapeDtypeStruct((s, D_MIX), F32), jax.ShapeDtypeStruct((s, D_MIX), BF16)],
        compiler_params=_params(("parallel",)),
    )(vln, u, w_bf, b_exp)


def spatial_bwd(dya, u, sv, vln, wt_bf, rows_per_step=512):
    s = vln.shape[0]
    tr = min(rows_per_step, s)
    ng = wt_bf.shape[0]

    def body(dya_ref, u_ref, sv_ref, v_ref, wt_ref, du_ref, dv_ref, dw_ref, db_ref):
        @pl.when(pl.program_id(0) == 0)
        def _():
            dw_ref[...] = jnp.zeros(dw_ref.shape, F32)
            db_ref[...] = jnp.zeros(db_ref.shape, F32)

        lo = _lane_lt64((CHUNK, LANES))
        for ch in range(tr // CHUNK):
            r0 = ch * CHUNK
            for p in range(D_MIX // LANES):
                cs = slice(p * LANES, (p + 1) * LANES)
                dya_p = dya_ref[r0:r0 + CHUNK, cs].astype(F32)
                du_ref[r0:r0 + CHUNK, cs] = dya_p * sv_ref[r0:r0 + CHUNK, cs]
                dsv = dya_p * u_ref[r0:r0 + CHUNK, cs]
                db_ref[:, cs] += dsv
                dsv0 = jnp.where(lo, dsv, 0.0).astype(BF16)
                dsv1 = jnp.where(lo, 0.0, dsv).astype(BF16)
                vp = v_ref[r0:r0 + CHUNK, cs]
                d0 = jnp.dot(wt_ref[2 * p], dsv0, preferred_element_type=F32)
                d1 = jnp.dot(wt_ref[2 * p + 1], dsv1, preferred_element_type=F32)
                dv_ref[r0:r0 + CHUNK, cs] = d0 + d1
                nt = (((1,), (1,)), ((), ()))
                dw_ref[2 * p] += lax.dot_general(dsv0, vp, nt, preferred_element_type=F32)
                dw_ref[2 * p + 1] += lax.dot_general(dsv1, vp, nt, preferred_element_type=F32)

    row = pl.BlockSpec((tr, D_MIX), lambda i: (i, 0))
    return pl.pallas_call(
        body, name="spatial_bwd", grid=(s // tr,),
        in_specs=[row, row, row, row, pl.BlockSpec(wt_bf.shape, lambda i: (0, 0, 0))],
        out_specs=[row, row, pl.BlockSpec((ng, CHUNK, CHUNK), lambda i: (0, 0, 0)), pl.BlockSpec((CHUNK, D_MIX), lambda i: (0, 0))],
        out_shape=[jax.ShapeDtypeStruct((s, D_MIX), F32), jax.ShapeDtypeStruct((s, D_MIX), F32),
                   jax.ShapeDtypeStruct((ng, CHUNK, CHUNK), F32), jax.ShapeDtypeStruct((CHUNK, D_MIX), F32)],
        compiler_params=_params(("arbitrary",)),
    )(dya, u, sv, vln, wt_bf)


def conv_fwd(proj, cb_val, cb_gate, w_pad, cb, ln_g, ln_b, ts=256):
    s = proj.shape[0]
    ts = min(ts, s)
    per = ts // CONV_HALO

    def body(val_ref, gate_ref, pval_ref, pgate_ref, w_ref, cb_ref, g_ref, b_ref, zc_ref, yb_ref, ext_ref):
        i = pl.program_id(0)
        zprev = pval_ref[...] * _sigmoid(pgate_ref[...])
        ext_ref[0:CONV_HALO, :] = jnp.where(i > 0, zprev, 0.0)
        ext_ref[CONV_HALO:, :] = val_ref[...] * _sigmoid(gate_ref[...])
        acc = jnp.zeros((ts, D_MIX), F32)
        for j in range(CONV_WIDTH):
            off = CONV_HALO - (CONV_WIDTH - 1) + j
            acc = acc + w_ref[j:j + 1, :] * ext_ref[off:off + ts, :]
        zc = acc + cb_ref[...]
        zc_ref[...] = zc
        yb_ref[...] = _f_lnsilu(zc, g_ref[...], b_ref[...])[0].astype(BF16)

    def cur(c):
        return pl.BlockSpec((ts, D_MIX), functools.partial(lambda i, c: (i, c), c=c))

    def prev(c):
        return pl.BlockSpec((CONV_HALO, D_MIX), functools.partial(lambda i, c: (jnp.maximum(i * per - 1, 0), c), c=c))

    const = lambda a: pl.BlockSpec(a.shape, lambda i: (0, 0))
    out = pl.BlockSpec((ts, D_MIX), lambda i: (i, 0))
    return pl.pallas_call(
        body, name="conv_fwd", grid=(s // ts,),
        in_specs=[cur(cb_val), cur(cb_gate), prev(cb_val), prev(cb_gate), const(w_pad), const(cb), const(ln_g), const(ln_b)],
        out_specs=[out, out],
        out_shape=[jax.ShapeDtypeStruct((s, D_MIX), F32), jax.ShapeDtypeStruct((s, D_MIX), BF16)],
        scratch_shapes=[pltpu.VMEM((CONV_HALO + ts, D_MIX), F32)],
        compiler_params=_params(("parallel",)),
    )(proj, proj, proj, proj, w_pad, cb, ln_g, ln_b)


def conv_bwd(proj, cb_val, cb_gate, zc, dyb, w_pad, ln_g, ln_b, ts=256):
    s = proj.shape[0]
    ts = min(ts, s)
    per = ts // CONV_HALO
    n_tiles = s // ts
    n_halo = s // CONV_HALO

    def body(val_ref, gate_ref, pval_ref, pgate_ref, zc_ref, dyb_ref, nzc_ref, ndyb_ref, w_ref, g_ref, b_ref,
             dval_ref, dgate_ref, dw_ref, dcb_ref, dg_ref, db_ref, zext_ref, dext_ref):
        i = pl.program_id(0)

        @pl.when(i == 0)
        def _():
            for r in (dw_ref, dcb_ref, dg_ref, db_ref):
                r[...] = jnp.zeros(r.shape, F32)

        g, b = g_ref[...], b_ref[...]
        _, vjp = jax.vjp(lambda z, gg, bb: _f_lnsilu(z, gg, bb)[0], zc_ref[...], g, b)
        dzc, dg, db = vjp(dyb_ref[...].astype(F32))
        dg_ref[...] += dg
        db_ref[...] += db
        dcb_ref[...] += jnp.sum(dzc, axis=0, keepdims=True)
        _, vjp_n = jax.vjp(lambda z: _f_lnsilu(z, g, b)[0], nzc_ref[...])
        (dzc_next,) = vjp_n(ndyb_ref[...].astype(F32))
        dext_ref[0:ts, :] = dzc
        dext_ref[ts:, :] = jnp.where(i < n_tiles - 1, dzc_next, 0.0)
        val, gate = val_ref[...], gate_ref[...]
        zprev = pval_ref[...] * _sigmoid(pgate_ref[...])
        zext_ref[0:CONV_HALO, :] = jnp.where(i > 0, zprev, 0.0)
        zext_ref[CONV_HALO:, :] = val * _sigmoid(gate)
        dz = jnp.zeros((ts, D_MIX), F32)
        for j in range(CONV_WIDTH):
            shift = CONV_WIDTH - 1 - j
            dz = dz + w_ref[j:j + 1, :] * dext_ref[shift:shift + ts, :]
            off = CONV_HALO - shift
            dw_ref[j:j + 1, :] += jnp.sum(dzc * zext_ref[off:off + ts, :], axis=0, keepdims=True)
        _, vjp_glu = jax.vjp(lambda a, c: _f_glu(a, c)[0], val, gate)
        dval, dgate = vjp_glu(dz)
        dval_ref[...] = dval.astype(BF16)
        dgate_ref[...] = dgate.astype(BF16)

    def cur(c):
        return pl.BlockSpec((ts, D_MIX), functools.partial(lambda i, c: (i, c), c=c))

    def prev(c):
        return pl.BlockSpec((CONV_HALO, D_MIX), functools.partial(lambda i, c: (jnp.maximum(i * per - 1, 0), c), c=c))

    nxt = pl.BlockSpec((CONV_HALO, D_MIX), lambda i: (jnp.minimum((i + 1) * per, n_halo - 1), 0))
    const = lambda a: pl.BlockSpec(a.shape, lambda i: (0, 0))
    out = pl.BlockSpec((ts, D_MIX), lambda i: (i, 0))
    vec = pl.BlockSpec((1, D_MIX), lambda i: (0, 0))
    return pl.pallas_call(
        body, name="conv_bwd", grid=(n_tiles,),
        in_specs=[cur(cb_val), cur(cb_gate), prev(cb_val), prev(cb_gate), out, out, nxt, nxt, const(w_pad), const(ln_g), const(ln_b)],
        out_specs=[out, out, pl.BlockSpec((CONV_HALO, D_MIX), lambda i: (0, 0)), vec, vec, vec],
        out_shape=[jax.ShapeDtypeStruct((s, D_MIX), BF16), jax.ShapeDtypeStruct((s, D_MIX), BF16),
                   jax.ShapeDtypeStruct((CONV_HALO, D_MIX), F32)] + [jax.ShapeDtypeStruct((1, D_MIX), F32)] * 3,
        scratch_shapes=[pltpu.VMEM((CONV_HALO + ts, D_MIX), F32), pltpu.VMEM((ts + CONV_HALO, D_MIX), F32)],
        compiler_params=_params(("arbitrary",)),
    )(proj, proj, proj, proj, zc, dyb, zc, dyb, w_pad, ln_g, ln_b)


def forget_cumsum(proj, cb_f, bf_exp, t=256):
    s = proj.shape[0]
    t = min(t, s)

    def body(f_ref, bf_ref, out_ref, carry_ref):
        @pl.when(pl.program_id(0) == 0)
        def _():
            carry_ref[...] = jnp.zeros(carry_ref.shape, F32)

        lf = _log_sigmoid(f_ref[...] + bf_ref[...])
        tri = (lax.broadcasted_iota(jnp.int32, (t, t), 1) <= lax.broadcasted_iota(jnp.int32, (t, t), 0)).astype(F32)
        c = jnp.dot(tri, lf, precision=HIGHEST, preferred_element_type=F32) + carry_ref[...]
        out_ref[...] = c
        carry_ref[...] = c[t - 1:t, :]

    return pl.pallas_call(
        body, name="forget_cumsum", grid=(s // t,),
        in_specs=[pl.BlockSpec((t, D_MIX), functools.partial(lambda i, c: (i, c), c=cb_f)), pl.BlockSpec((1, D_MIX), lambda i: (0, 0))],
        out_specs=pl.BlockSpec((t, D_MIX), lambda i: (i, 0)),
        out_shape=jax.ShapeDtypeStruct((s, D_MIX), F32),
        scratch_shapes=[pltpu.VMEM((1, D_MIX), F32)],
        compiler_params=_params(("arbitrary",)),
    )(proj, bf_exp)


def forget_bwd(proj, cb_f, bf_exp, dcum, t=256):
    s = proj.shape[0]
    t = min(t, s)
    n = s // t

    def body(f_ref, bf_ref, dc_ref, df_ref, dbf_ref, carry_ref):
        @pl.when(pl.program_id(0) == 0)
        def _():
            carry_ref[...] = jnp.zeros(carry_ref.shape, F32)
            dbf_ref[...] = jnp.zeros(dbf_ref.shape, F32)

        tri = (lax.broadcasted_iota(jnp.int32, (t, t), 1) >= lax.broadcasted_iota(jnp.int32, (t, t), 0)).astype(F32)
        r = jnp.dot(tri, dc_ref[...], precision=HIGHEST, preferred_element_type=F32) + carry_ref[...]
        carry_ref[...] = r[0:1, :]
        df = r * _sigmoid(-(f_ref[...] + bf_ref[...]))
        dbf_ref[...] += jnp.sum(df, axis=0, keepdims=True)
        live = lax.broadcasted_iota(jnp.int32, (t, D_MIX), 1) % HEAD_DIM == 0
        df_ref[...] = jnp.where(live, df, 0.0).astype(BF16)

    return pl.pallas_call(
        body, name="forget_bwd", grid=(n,),
        in_specs=[pl.BlockSpec((t, D_MIX), functools.partial(lambda i, c: (n - 1 - i, c), c=cb_f)), pl.BlockSpec((1, D_MIX), lambda i: (0, 0)),
                  pl.BlockSpec((t, D_MIX), lambda i: (n - 1 - i, 0))],
        out_specs=[pl.BlockSpec((t, D_MIX), lambda i: (n - 1 - i, 0)), pl.BlockSpec((1, D_MIX), lambda i: (0, 0))],
        out_shape=[jax.ShapeDtypeStruct((s, D_MIX), BF16), jax.ShapeDtypeStruct((1, D_MIX), F32)],
        scratch_shapes=[pltpu.VMEM((1, D_MIX), F32)],
        compiler_params=_params(("arbitrary",)),
    )(proj, bf_exp, dcum)


NT = (((1,), (1,)), ((), ()))
LOG2E = math.log2(math.e)
N_PAIR = D_MIX // LANES


def _split3(x):
    hi = x.astype(BF16).astype(F32)
    mid = (x - hi).astype(BF16).astype(F32)
    return hi, mid, x - hi - mid


def _triple(li, first, vals):
    out = jnp.where(li == first, vals[0], 0.0)
    for i in (1, 2):
        out = jnp.where(li == first + i, vals[i], out)
    return out


def _lane_ids(shape):
    lane = lax.broadcasted_iota(jnp.int32, shape, 1)
    return lane, lane % HEAD_DIM, lane < HEAD_DIM


def attn_prep(proj, cb_q, cum, ts=256):
    s = proj.shape[0]
    ts = min(ts, s)
    scale = LOG2E / math.sqrt(HEAD_DIM)

    def body(q_ref, k_ref, v_ref, c_ref, qe_ref, qo_ref, ke_ref, ko_ref, ve_ref, vo_ref):
        _, li, lo = _lane_ids((ts, LANES))
        one3 = lambda first: ((li >= first) & (li < first + 3)).astype(F32)
        for p in range(N_PAIR):
            ps = slice(p * LANES, (p + 1) * LANES)
            c3 = _split3(pltpu.roll(c_ref[:, ps] * LOG2E, HEAD_DIM, axis=1))
            eq = _triple(li, 0, c3) + one3(3)
            ek = one3(0) - _triple(li, 3, c3) + one3(6)
            ev = one3(0)
            for src, even, odd, extra, mul in ((q_ref, qe_ref, qo_ref, eq, scale), (k_ref, ke_ref, ko_ref, ek, 1.0), (v_ref, ve_ref, vo_ref, ev, 1.0)):
                x = src[:, ps] * mul
                even[:, ps] = jnp.where(lo, x, extra).astype(BF16)
                odd[:, ps] = jnp.where(lo, extra, x).astype(BF16)

    col = lambda c: pl.BlockSpec((ts, D_MIX), functools.partial(lambda i, c: (i, c), c=c))
    out = pl.BlockSpec((ts, D_MIX), lambda i: (i, 0))
    return pl.pallas_call(
        body, name="attn_prep", grid=(s // ts,),
        in_specs=[col(cb_q), col(cb_q + 1), col(cb_q + 2), pl.BlockSpec((ts, D_MIX), lambda i: (i, 0))],
        out_specs=[out] * 6, out_shape=[jax.ShapeDtypeStruct((s, D_MIX), BF16)] * 6,
        compiler_params=_params(("parallel",)),
    )(proj, proj, proj, cum)


def _pair_specs(s, t):
    return pl.BlockSpec((t, LANES), lambda p, i: (i, p)), pl.BlockSpec((s, LANES), lambda p, i: (0, p))


def attn_fwd(qe, qo, ke, ko, ve, vo, tq=512):
    s = qe.shape[0]
    tq = min(tq, s)

    def body(qe_ref, qo_ref, ke_ref, ko_ref, ve_ref, vo_ref, o_ref, qbe_ref, qbo_ref):
        qi = pl.program_id(1)
        qs, k_refs, v_refs = (qe_ref[...], qo_ref[...]), (ke_ref, ko_ref), (ve_ref, vo_ref)
        causal = lax.broadcasted_iota(jnp.int32, (tq, tq), 1) <= lax.broadcasted_iota(jnp.int32, (tq, tq), 0)

        def step(j, carry, diag):
            ks = pl.multiple_of(j * tq, tq)
            new = []
            for h in range(2):
                m, l, acc = carry[h]
                sc = lax.dot_general(qs[h], k_refs[h][pl.ds(ks, tq), :], NT, preferred_element_type=F32)
                if diag:
                    sc = jnp.where(causal, sc, -jnp.inf)
                m_new = jnp.maximum(m, jnp.max(sc, axis=1, keepdims=True))
                p = jnp.exp2(sc - m_new)
                alpha = jnp.exp2(m - m_new)
                l = alpha * l + jnp.sum(p, axis=1, keepdims=True)
                acc = alpha * acc + jnp.dot(p.astype(BF16), v_refs[h][pl.ds(ks, tq), :], preferred_element_type=F32)
                new.append((m_new, l, acc))
            return tuple(new)

        init = tuple((jnp.full((tq, 1), -jnp.inf, F32), jnp.zeros((tq, 1), F32), jnp.zeros((tq, LANES), F32)) for _ in range(2))
        carry = lax.fori_loop(0, qi, lambda j, c: step(j, c, False), init)
        (m0, l0, a0), (m1, l1, a1) = step(qi, carry, True)
        _, li, lo = _lane_ids((tq, LANES))
        o_ref[...] = jnp.where(lo, a0 / l0, a1 / l1).astype(o_ref.dtype)
        lse_lanes = (li >= 6) & (li < 9)
        for q, m, l, spare, out_ref in ((qs[0], m0, l0, ~lo, qbe_ref), (qs[1], m1, l1, lo, qbo_ref)):
            neg_lse = _triple(li, 6, _split3(-(m + jnp.log(l) * LOG2E)))
            out_ref[...] = jnp.where(spare & lse_lanes, neg_lse.astype(BF16), q)

    blk, full = _pair_specs(s, tq)
    return pl.pallas_call(
        body, name="attn_fwd", grid=(N_PAIR, s // tq),
        in_specs=[blk, blk, full, full, full, full],
        out_specs=[blk] * 3, out_shape=[jax.ShapeDtypeStruct((s, D_MIX), BF16)] * 3,
        compiler_params=_params(("parallel", "parallel")),
    )(qe, qo, ke, ko, ve, vo)


def attn_dq(qbe, qbo, ke, ko, ve, vo, do, dep, tq=512):
    s = qbe.shape[0]
    tq = min(tq, s)
    scale = 1.0 / math.sqrt(HEAD_DIM)

    def body(qe_ref, qo_ref, ke_ref, ko_ref, ve_ref, vo_ref, do_ref, dep_ref, dq_ref, dobe_ref, dobo_ref):
        qi = pl.program_id(1)
        _, li, lo = _lane_ids((tq, LANES))
        do_ = do_ref[...]
        qs, k_refs, v_refs = (qe_ref[...], qo_ref[...]), (ke_ref, ko_ref), (ve_ref, vo_ref)
        dos = (jnp.where(lo, do_, 0), jnp.where(lo, 0, do_))
        causal = lax.broadcasted_iota(jnp.int32, (tq, tq), 1) <= lax.broadcasted_iota(jnp.int32, (tq, tq), 0)

        def step(j, carry, diag):
            ks = pl.multiple_of(j * tq, tq)
            new = []
            for h in range(2):
                pdpk, pk, dsum = carry[h]
                kb = k_refs[h][pl.ds(ks, tq), :]
                sc = lax.dot_general(qs[h], kb, NT, preferred_element_type=F32)
                if diag:
                    sc = jnp.where(causal, sc, -jnp.inf)
                p = jnp.exp2(sc)
                pdp = p * lax.dot_general(dos[h], v_refs[h][pl.ds(ks, tq), :], NT, preferred_element_type=F32)
                new.append((pdpk + jnp.dot(pdp.astype(BF16), kb, preferred_element_type=F32),
                            pk + jnp.dot(p.astype(BF16), kb, preferred_element_type=F32),
                            dsum + jnp.sum(pdp, axis=1, keepdims=True)))
            return tuple(new)

        init = tuple((jnp.zeros((tq, LANES), F32), jnp.zeros((tq, LANES), F32), jnp.zeros((tq, 1), F32)) for _ in range(2))
        carry = lax.fori_loop(0, qi, lambda j, c: step(j, c, False), init)
        (a0, b0, s0), (a1, b1, s1) = step(qi, carry, True)
        dq_ref[...] = (jnp.where(lo, a0 - s0 * b0, a1 - s1 * b1) * scale).astype(dq_ref.dtype)
        dobe_ref[...] = jnp.where(lo, do_, _triple(li, 0, _split3(-s0)).astype(BF16))
        dobo_ref[...] = jnp.where(lo, _triple(li, 0, _split3(-s1)).astype(BF16), do_)

    blk, full = _pair_specs(s, tq)
    return pl.pallas_call(
        body, name="attn_dq", grid=(N_PAIR, s // tq),
        in_specs=[blk, blk, full, full, full, full, blk, pl.BlockSpec(dep.shape, lambda p, i: (0, 0))],
        out_specs=[blk] * 3, out_shape=[jax.ShapeDtypeStruct((s, D_MIX), BF16)] * 3,
        compiler_params=_params(("parallel", "parallel")),
    )(qbe, qbo, ke, ko, ve, vo, do, dep)


def attn_dkv(ke, ko, ve, vo, qbe, qbo, dobe, dobo, tk=512):
    s = ke.shape[0]
    tk = min(tk, s)
    nq = s // tk

    def body(ke_ref, ko_ref, ve_ref, vo_ref, qe_ref, qo_ref, de_ref, do_ref, dk_ref, dv_ref, dck_ref):
        kj = pl.program_id(1)
        lo = _lane_lt64((tk, LANES))
        ks_, vs_, q_refs, d_refs = (ke_ref[...], ko_ref[...]), (ve_ref[...], vo_ref[...]), (qe_ref, qo_ref), (de_ref, do_ref)
        causal = lax.broadcasted_iota(jnp.int32, (tk, tk), 0) <= lax.broadcasted_iota(jnp.int32, (tk, tk), 1)

        def step(i, carry, diag):
            qs = pl.multiple_of(i * tk, tk)
            new = []
            for h in range(2):
                dk, dv, dck = carry[h]
                qblk = q_refs[h][pl.ds(qs, tk), :]
                dblk = d_refs[h][pl.ds(qs, tk), :]
                st = lax.dot_general(ks_[h], qblk, NT, preferred_element_type=F32)
                if diag:
                    st = jnp.where(causal, st, -jnp.inf)
                pt = jnp.exp2(st)
                dst = pt * lax.dot_general(vs_[h], dblk, NT, preferred_element_type=F32)
                new.append((dk + jnp.dot(dst.astype(BF16), qblk, preferred_element_type=F32),
                            dv + jnp.dot(pt.astype(BF16), dblk, preferred_element_type=F32),
                            dck - jnp.sum(dst, axis=1, keepdims=True)))
            return tuple(new)

        init = tuple((jnp.zeros((tk, LANES), F32), jnp.zeros((tk, LANES), F32), jnp.zeros((tk, 1), F32)) for _ in range(2))
        carry = step(kj, init, True)
        (dk0, dv0, dc0), (dk1, dv1, dc1) = lax.fori_loop(kj + 1, nq, lambda i, c: step(i, c, False), carry)
        dk_ref[...] = (jnp.where(lo, dk0, dk1) * (1.0 / LOG2E)).astype(dk_ref.dtype)
        dv_ref[...] = jnp.where(lo, dv0, dv1).astype(dv_ref.dtype)
        dck_ref[...] = jnp.where(lo, dc0, dc1)

    blk, full = _pair_specs(s, tk)
    return pl.pallas_call(
        body, name="attn_dkv", grid=(N_PAIR, nq),
        in_specs=[blk, blk, blk, blk, full, full, full, full],
        out_specs=[blk] * 3,
        out_shape=[jax.ShapeDtypeStruct((s, D_MIX), BF16), jax.ShapeDtypeStruct((s, D_MIX), BF16), jax.ShapeDtypeStruct((s, D_MIX), F32)],
        compiler_params=_params(("parallel", "parallel")),
    )(ke, ko, ve, vo, qbe, qbo, dobe, dobo)


def _pre_bwd(name, x, g, sc, sh, dh, dres):
    d = x.shape[1]

    def fn(xv, dhv, dresv, gv, scv, shv):
        _, vjp = jax.vjp(lambda *a: _f_pre(*a)[0], xv, gv, scv, shv)
        dx, dg, dsc, dsh = vjp(dhv.astype(F32))
        return dx + dresv, dg, dsc, dsh

    return rowwise(name, fn, [(x, 0, d), (dh, 0, d), (dres, 0, d)], [g, sc, sh], [(d, F32)], accs=[(1, d)] * 3)


def layer_fwd(x, mod, layer):
    s, d = x.shape
    m = D_MIX
    w = dict(layer["small"])
    sh1, sc1, gt1, sh2, sc2, gt2 = (mod[i:i + 1] for i in range(6))
    cb = 3 * d // m
    (h,) = rowwise("pre1", _f_pre, [(x, 0, d)], [w["mix_pre_g"], sc1, sh1], [(d, BF16)])
    w.update(layer["get_in"](h))
    proj = matmul("w_in", h, w["w_in"])
    w.update(layer["get_abco"](proj))
    u, vln = rowwise("gmlp_in", _f_a1, [(proj, cb, m), (proj, cb + 1, m)], [w["gmlp_ln_g"], w["gmlp_ln_b"]], [(m, F32), (m, BF16)])
    sv, ya = spatial_fwd(vln, u, w["ws"], w["bs_exp"])
    y_a = matmul("w_a", ya, w["w_a_out"])
    zc, yb = conv_fwd(proj, cb + 2, cb + 3, w["conv_w"], w["conv_b"], w["conv_ln_g"], w["conv_ln_b"])
    y_b = matmul("w_b", yb, w["w_b_out"])
    cum = forget_cumsum(proj, cb + 7, w["bf_exp"])
    kv_ops = attn_prep(proj, cb + 4, cum)
    o, qbe, qbo = attn_fwd(*kv_ops)
    att = (qbe, qbo) + tuple(kv_ops[2:])
    y_c = matmul("w_c", o, w["w_c_out"])
    (merged,) = rowwise("merge", _f_merge, [(proj, 0, d), (proj, 1, d), (proj, 2, d), (y_a, 0, d), (y_b, 0, d), (y_c, 0, d)], [], [(d, BF16)])
    y = matmul("w_out", merged, w["w_out"])
    w.update(layer["get_mlp"](y))

    def post_pre(xv, yv, gp, gt, g2, sc, sh):
        x1 = xv + _f_post(yv, gp, gt)[0]
        return x1, _f_pre(x1, g2, sc, sh)[0]

    x1, h2 = rowwise("post1", post_pre, [(x, 0, d), (y, 0, d)], [w["mix_post_g"], gt1, w["mlp_pre_g"], sc2, sh2], [(d, F32), (d, BF16)])
    a_bf, r = matmul("w1", h2, w["mlp_w1"], out_dtypes=(BF16, BF16), epilogue=lambda acc: (acc, jnp.square(jnp.maximum(acc, 0.0))))
    y2 = matmul("w2", r, w["mlp_w2"])
    (x2,) = rowwise("post2", lambda xv, yv, g, gt: xv + _f_post(yv, g, gt)[0], [(x1, 0, d), (y2, 0, d)], [w["mlp_post_g"], gt2], [(d, F32)])
    saved = dict(w=w, x=x, h=h, proj=proj, u=u, vln=vln, sv=sv, ya=ya, y_a=y_a, zc=zc, yb=yb, y_b=y_b, att=att,
                 o=o, y_c=y_c, merged=merged, y=y, x1=x1, h2=h2, a_bf=a_bf, r=r, y2=y2)
    return x2, saved


def layer_bwd(dx2, mod, sv, emit, tok_in=None):
    x, proj, w = sv["x"], sv["proj"], sv["w"]
    s, d = x.shape
    m = D_MIX
    sh1, sc1, gt1, sh2, sc2, gt2 = (mod[i:i + 1] for i in range(6))
    cb = 3 * d // m
    g = {}
    if tok_in is not None:
        gt2 = gt2 + tok_in[0:1, 0:1]
    dy2, g["mlp_post_g"], dgt2 = rowwise_vjp("post2_b", _f_post, [(sv["y2"], 0, d)], [w["mlp_post_g"], gt2], [(dx2, 0, d)], [BF16])
    da = matmul("w2_dx", dy2, w["mlp_w2"], tb=True, out_dtypes=(BF16,),
                epilogue=lambda acc, a: (acc * (2.0 * jnp.maximum(a.astype(F32), 0.0)),), epi=[(sv["a_bf"], 0)])
    big = {}
    big["mlp_w2"] = matmul("w2_dw", sv["r"], dy2, ta=True, out_dtypes=(BF16,))
    dh2 = matmul("w1_dx", da, w["mlp_w1"], tb=True)
    big["mlp_w1"] = matmul("w1_dw", sv["h2"], da, ta=True, out_dtypes=(BF16,))
    tok = emit("mlp", big)
    dx1, g["mlp_pre_g"], dsc2, dsh2 = _pre_bwd("pre2_b", sv["x1"], w["mlp_pre_g"], sc2 + tok[0:1, 0:1], sh2, dh2, dx2)
    dy, g["mix_post_g"], dgt1 = rowwise_vjp("post1_b", _f_post, [(sv["y"], 0, d)], [w["mix_post_g"], gt1], [(dx1, 0, d)], [BF16])
    dmerged = matmul("w_out_dx", dy, w["w_out"], tb=True)
    big = {}
    big["w_out"] = matmul("w_out_dw", sv["merged"], dy, ta=True, out_dtypes=(BF16,))
    dg0, dg1, dg2, dya_, dyb_, dyc_ = rowwise_vjp(
        "merge_b", _f_merge, [(proj, 0, d), (proj, 1, d), (proj, 2, d), (sv["y_a"], 0, d), (sv["y_b"], 0, d), (sv["y_c"], 0, d)], [],
        [(dmerged, 0, d)], [BF16] * 6)
    dya_pre = matmul("w_a_dx", dya_, w["w_a_out"], tb=True)
    big["w_a_out"] = matmul("w_a_dw", sv["ya"], dya_, ta=True, out_dtypes=(BF16,))
    dyb_pre = matmul("w_b_dx", dyb_, w["w_b_out"], tb=True)
    big["w_b_out"] = matmul("w_b_dw", sv["yb"], dyb_, ta=True, out_dtypes=(BF16,))
    do = matmul("w_c_dx", dyc_, w["w_c_out"], tb=True, out_dtypes=(BF16,))
    big["w_c_out"] = matmul("w_c_dw", sv["o"], dyc_, ta=True, out_dtypes=(BF16,))
    tok = emit("abco", big)
    qbe, qbo, ke, ko, ve, vo = sv["att"]
    dq, dobe, dobo = attn_dq(qbe, qbo, ke, ko, ve, vo, do, tok)
    dk, dv, dcum = attn_dkv(ke, ko, ve, vo, qbe, qbo, dobe, dobo)
    df, dbf = forget_bwd(proj, cb + 7, w["bf_exp"], dcum)
    g["fox_bf"] = dbf[0, ::HEAD_DIM]
    dval, dgate, dwc, g["conv_b"], g["conv_ln_g"], g["conv_ln_b"] = conv_bwd(
        proj, cb + 2, cb + 3, sv["zc"], dyb_pre, w["conv_w"], w["conv_ln_g"], w["conv_ln_b"])
    g["conv_w"] = dwc[:CONV_WIDTH]
    du, dvln, dws, dbexp = spatial_bwd(dya_pre, sv["u"], sv["sv"], sv["vln"], w["ws_t"])
    g["gmlp_ws"] = dws * jnp.tril(jnp.ones((CHUNK, CHUNK), F32))
    g["gmlp_bs"] = dbexp.reshape(CHUNK, m // GROUP_DIM, GROUP_DIM).sum(-1).T
    du_raw, dv_raw, g["gmlp_ln_g"], g["gmlp_ln_b"] = rowwise_vjp(
        "gmlp_in_b", _f_a1, [(proj, cb, m), (proj, cb + 1, m)], [w["gmlp_ln_g"], w["gmlp_ln_b"]], [(du, 0, m), (dvln, 0, m)], [BF16, BF16])
    dproj = jnp.concatenate([dg0, dg1, dg2, du_raw, dv_raw, dval, dgate, dq, dk, dv, df], axis=1)
    tok = emit("in", {"w_in": matmul("w_in_dw", sv["h"], dproj, ta=True, out_dtypes=(BF16,))})
    dh = matmul("w_in_dx", dproj, w["w_in"], tb=True, dep=tok)
    dx, g["mix_pre_g"], dsc1, dsh1 = _pre_bwd("pre1_b", x, w["mix_pre_g"], sc1, sh1, dh, dx1)
    dmod = jnp.concatenate([dsh1, dsc1, dgt1, dsh2, dsc2, dgt2], axis=0)
    return dx, dmod, g


def local_step(x, target, mods, layers):
    d = x.shape[1]
    saved = []
    for l in range(len(layers)):
        x, sv = layer_fwd(x, mods[l], layers[l])
        saved.append(sv)

    def loss_fn(xv, tv):
        err = xv - tv
        return err * (1.0 / d), jnp.sum(err * err, axis=0, keepdims=True)

    dx, sq = rowwise("loss", loss_fn, [(x, 0, d), (target, 0, d)], [], [(d, F32)], accs=[(1, d)])
    loss = (0.5 / d) * jnp.sum(sq)
    dmods, grads = [None] * len(layers), [None] * len(layers)
    tok = None
    for l in reversed(range(len(layers))):
        dx, dmods[l], grads[l] = layer_bwd(dx, mods[l], saved[l], layers[l]["emit"], tok)
        tok = layers[l]["emit_small"](dmods[l], grads[l])
    return loss, dx, dmods, grads


def exchange(name, arrs, scatter):
    n = len(arrs)

    def body(*refs):
        in_refs, out_refs = refs[:n], refs[n:2 * n]
        send_sems, recv_sems, local_sems = refs[2 * n:]
        x, y, c = lax.axis_index("x"), lax.axis_index("y"), lax.axis_index("c")
        me = 4 * x + 2 * y + c
        local = []
        for a in range(n):
            src = in_refs[a].at[me] if scatter else in_refs[a]
            cp = pltpu.make_async_copy(src, out_refs[a].at[me], local_sems.at[a])
            cp.start()
            local.append(cp)
        remote = []
        for k in range(1, N_DEV):
            px, py, pc = x ^ ((k >> 2) & 1), y ^ ((k >> 1) & 1), c ^ (k & 1)
            peer = 4 * px + 2 * py + pc
            for a in range(n):
                src = in_refs[a].at[peer] if scatter else in_refs[a]
                cp = pltpu.make_async_remote_copy(
                    src_ref=src, dst_ref=out_refs[a].at[me], send_sem=send_sems.at[a * (N_DEV - 1) + k - 1],
                    recv_sem=recv_sems.at[a * (N_DEV - 1) + k - 1], device_id=(px, py, pc), device_id_type=MESH)
                cp.start()
                remote.append(cp)
        for cp in remote:
            cp.wait()
        for cp in local:
            cp.wait()

    hbm = pl.BlockSpec(memory_space=pltpu.HBM)
    out_shape = [jax.ShapeDtypeStruct(a.shape if scatter else (N_DEV,) + a.shape, a.dtype) for a in arrs]
    return pl.pallas_call(
        body, name=name, in_specs=[hbm] * n, out_specs=[hbm] * n, out_shape=out_shape,
        scratch_shapes=[pltpu.SemaphoreType.DMA((n * (N_DEV - 1),)), pltpu.SemaphoreType.DMA((n * (N_DEV - 1),)),
                        pltpu.SemaphoreType.DMA((n,))],
    )(*arrs)


def _peers(x, y, c):
    out = []
    for k in range(1, N_DEV):
        px, py, pc = x ^ ((k >> 2) & 1), y ^ ((k >> 1) & 1), c ^ (k & 1)
        out.append((k - 1, (px, py, pc), 4 * px + 2 * py + pc))
    return out


def _exchange_copies(srcs, lands, send_sems, recv_sems, scatter):
    x, y, c = lax.axis_index("x"), lax.axis_index("y"), lax.axis_index("c")
    me = 4 * x + 2 * y + c
    copies = []
    for slot, pos, peer in _peers(x, y, c):
        for a, (src, land) in enumerate(zip(srcs, lands)):
            copies.append(pltpu.make_async_remote_copy(
                src_ref=src.at[peer] if scatter else src, dst_ref=land.at[me],
                send_sem=send_sems.at[a * (N_DEV - 1) + slot], recv_sem=recv_sems.at[a * (N_DEV - 1) + slot],
                device_id=pos, device_id_type=MESH))
    return me, copies


def exchange_start(name, arrs, scatter, after=None):
    n = len(arrs)
    lands = [lax.empty(a.shape if scatter else (N_DEV,) + a.shape, a.dtype) for a in arrs]
    n_in = 2 * n + (after is not None)

    def body(*refs):
        srcs, lands_ = refs[:n], refs[n:2 * n]
        send_sems, recv_sems = refs[n_in], refs[n_in + 1]
        token = refs[n_in + 2 + 2 * n]
        _, copies = _exchange_copies(srcs, lands_, send_sems, recv_sems, scatter)
        for cp in copies:
            cp.start()
        token[...] = jnp.zeros(token.shape, token.dtype)

    hbm = pl.BlockSpec(memory_space=pltpu.HBM)
    sem = pl.BlockSpec(memory_space=pltpu.SEMAPHORE)
    n_sem = n * (N_DEV - 1)
    res = pl.pallas_call(
        body, name=name,
        out_shape=(pltpu.SemaphoreType.DMA((n_sem,)), pltpu.SemaphoreType.DMA((n_sem,)),
                   *[pltpu.HBM(a.shape, a.dtype) for a in arrs], *[pltpu.HBM(l.shape, l.dtype) for l in lands],
                   jax.ShapeDtypeStruct((8, LANES), F32)),
        in_specs=[hbm] * (2 * n) + ([] if after is None else [pl.BlockSpec(memory_space=pl.ANY)]),
        out_specs=(sem, sem, *([hbm] * (2 * n)), pl.BlockSpec(memory_space=pltpu.VMEM)),
        input_output_aliases={i: 2 + i for i in range(2 * n)},
        compiler_params=pltpu.CompilerParams(has_side_effects=pltpu.SideEffectType.DATAFLOW_SIDE_EFFECTING),
    )(*[pltpu.with_memory_space_constraint(a, pltpu.HBM) for a in arrs],
      *[pltpu.with_memory_space_constraint(l, pltpu.HBM) for l in lands], *([] if after is None else [after]))
    return dict(n=n, scatter=scatter, send=res[0], recv=res[1], srcs=res[2:2 + n], lands=res[2 + n:2 + 2 * n], token=res[2 + 2 * n])


def exchange_wait(name, st, after):
    n, scatter = st["n"], st["scatter"]

    def body(*refs):
        srcs, lands_ = refs[:n], refs[n:2 * n]
        send_sems, recv_sems = refs[2 * n], refs[2 * n + 1]
        _, copies = _exchange_copies(srcs, lands_, send_sems, recv_sems, scatter)
        for cp in copies:
            cp.wait_send()
            cp.wait_recv()

    hbm = pl.BlockSpec(memory_space=pltpu.HBM)
    sem = pl.BlockSpec(memory_space=pltpu.SEMAPHORE)
    res = pl.pallas_call(
        body, name=name,
        out_shape=tuple(pltpu.HBM(a.shape, a.dtype) for a in (*st["srcs"], *st["lands"])),
        in_specs=[hbm] * (2 * n) + [sem, sem, pl.BlockSpec(memory_space=pl.ANY)], out_specs=tuple([hbm] * (2 * n)),
        input_output_aliases={i: i for i in range(2 * n)},
        compiler_params=pltpu.CompilerParams(has_side_effects=pltpu.SideEffectType.DATAFLOW_SIDE_EFFECTING),
    )(*st["srcs"], *st["lands"], st["send"], st["recv"], after)
    return list(res[:n]), list(res[n:])


def adamw_sum(name, parts, w, m, v, tr=256):
    nl = len(parts)
    k, r, c = parts[0].shape
    tr = _tile(r, tr, 16)
    c1 = 1.0 - ADAM_B1 ** ADAM_STEP
    c2 = 1.0 - ADAM_B2 ** ADAM_STEP

    def body(*refs):
        p_refs = refs[:nl]
        w_ref, m_ref, v_ref, g_ref, d_ref, nm_ref, nv_ref = refs[nl:]
        for l in range(nl):
            @pl.when(pl.program_id(0) == l)
            def _(p_ref=p_refs[l]):
                grad = p_ref[0].astype(F32)
                for j in range(1, k):
                    grad = grad + p_ref[j].astype(F32)
                new_m = ADAM_B1 * m_ref[...] + (1.0 - ADAM_B1) * grad
                new_v = ADAM_B2 * v_ref[...] + (1.0 - ADAM_B2) * (grad * grad)
                m_hat = new_m / c1
                v_hat = new_v / c2
                g_ref[...] = grad
                d_ref[...] = -ADAM_LR * (m_hat / (jnp.sqrt(v_hat) + ADAM_EPS) + ADAM_WD * w_ref[...])
                nm_ref[...] = new_m
                nv_ref[...] = new_v

    part = lambda l: pl.BlockSpec((k, tr, c), functools.partial(lambda ll, i, l: (0, jnp.where(ll == l, i, 0), 0), l=l))
    blk = pl.BlockSpec((None, tr, c), lambda ll, i: (ll, i, 0))
    return pl.pallas_call(
        body, name=name, grid=(nl, r // tr),
        in_specs=[part(l) for l in range(nl)] + [blk, blk, blk],
        out_specs=[blk] * 4, out_shape=[jax.ShapeDtypeStruct((nl, r, c), F32)] * 4,
        compiler_params=_params(("parallel", "parallel")),
    )(*parts, w, m, v)


def ada_fwd(c_all, ada_w):
    nl, d, n = ada_w.shape

    def body(c_ref, w_ref, o_ref):
        o_ref[...] = jnp.dot(_silu(c_ref[...]), w_ref[...], precision=HIGHEST, preferred_element_type=F32)

    return pl.pallas_call(
        body, name="ada_fwd", grid=(nl,),
        in_specs=[pl.BlockSpec((N_DEV, d), lambda l: (0, 0)), pl.BlockSpec((None, d, n), lambda l: (l, 0, 0))],
        out_specs=pl.BlockSpec((None, N_DEV, n), lambda l: (l, 0, 0)),
        out_shape=jax.ShapeDtypeStruct((nl, N_DEV, n), F32),
        compiler_params=_params(("parallel",)),
    )(c_all, ada_w)


def ada_bwd(c_all_t, dmod, td=256):
    d = c_all_t.shape[0]
    nl, _, n = dmod.shape
    td = _tile(d, td, 8)

    def body(c_ref, dm_ref, o_ref):
        ca = _silu(c_ref[...])
        acc = ca[:, 0:1] * dm_ref[0:1, :]
        for b in range(1, N_DEV):
            acc = acc + ca[:, b:b + 1] * dm_ref[b:b + 1, :]
        o_ref[...] = acc

    return pl.pallas_call(
        body, name="ada_bwd", grid=(nl, d // td),
        in_specs=[pl.BlockSpec((td, N_DEV), lambda l, i: (i, 0)), pl.BlockSpec((None, N_DEV, n), lambda l, i: (l, 0, 0))],
        out_specs=pl.BlockSpec((None, td, n), lambda l, i: (l, i, 0)),
        out_shape=jax.ShapeDtypeStruct((nl, d, n), F32),
        compiler_params=_params(("parallel", "parallel")),
    )(c_all_t, dmod)


ARG_NAMES = ["x", "c", "ada_w", "ada_b", "mix_pre_g", "mix_post_g", "mlp_pre_g", "mlp_post_g", "w_in", "gmlp_ln_g", "gmlp_ln_b",
             "gmlp_ws", "gmlp_bs", "w_a_out", "conv_w", "conv_b", "conv_ln_g", "conv_ln_b", "w_b_out", "fox_bf", "w_c_out",
             "w_out", "mlp_w1", "mlp_w2"]
WEIGHTS = ARG_NAMES[2:]
COL_SHARDED = ["w_in", "w_a_out", "w_b_out", "w_c_out", "mlp_w1"]
ROW_SHARDED = ["w_out", "mlp_w2"]
BIG = COL_SHARDED + ROW_SHARDED
GROUPS = {"in": ["w_in"], "abco": ["w_a_out", "w_b_out", "w_c_out", "w_out"], "mlp": ["mlp_w1", "mlp_w2"]}
SMALL = ["ada_b", "mix_pre_g", "mix_post_g", "mlp_pre_g", "mlp_post_g", "gmlp_ln_g", "gmlp_ln_b", "gmlp_ws", "gmlp_bs",
         "conv_b", "conv_ln_g", "conv_ln_b", "fox_bf"]
PACK_COLS = 512


def _to_my_layout(w_in, d):
    m = D_MIX
    nf = 7 * m
    return jnp.concatenate([w_in[..., nf + N_HEADS:], w_in[..., :nf], jnp.repeat(w_in[..., nf:nf + N_HEADS], HEAD_DIM, axis=-1)], axis=-1)


def _from_my_layout(gw, d):
    m = D_MIX
    return jnp.concatenate([gw[..., 3 * d:3 * d + 7 * m], gw[..., 3 * d + 7 * m::HEAD_DIM], gw[..., :3 * d]], axis=-1)


def _ref_ranges(lo, hi, shard):
    out = []
    while lo < hi:
        j = lo // shard
        end = min(hi, (j + 1) * shard)
        out.append((j, lo - j * shard, end - j * shard))
        lo = end
    return out


def _w_in_to_kernel_layout(g, d):
    m = D_MIX
    nf = 7 * m
    shard = g.shape[2]
    cols = lambda lo, hi: [g[j, :, a:b] for j, a, b in _ref_ranges(lo, hi, shard)]
    forget = jnp.concatenate(cols(nf, nf + N_HEADS), axis=1)
    return jnp.concatenate(cols(nf + N_HEADS, nf + N_HEADS + 3 * d) + cols(0, nf) + [jnp.repeat(forget, HEAD_DIM, axis=1)], axis=1)


def _w_in_grad_blocks(gw, d):
    m = D_MIX
    nf = 7 * m
    n_ref = nf + N_HEADS + 3 * d
    shard = n_ref // N_DEV
    segs = [(0, nf, 3 * d, 1), (nf, nf + N_HEADS, 3 * d + nf, HEAD_DIM), (nf + N_HEADS, n_ref, 0, 1)]
    blocks = []
    for j in range(N_DEV):
        lo, hi = j * shard, (j + 1) * shard
        pieces = []
        for r0, r1, k0, stride in segs:
            a, b = max(lo, r0), min(hi, r1)
            if a < b:
                pieces.append(gw[:, k0 + (a - r0) * stride:k0 + (b - r0) * stride:stride])
        blocks.append(jnp.concatenate(pieces, axis=1) if len(pieces) > 1 else pieces[0])
    return jnp.stack(blocks)


def _pack(parts):
    flat = jnp.concatenate([p.reshape(-1).astype(F32) for p in parts])
    pad = (-flat.shape[0]) % (PACK_COLS * 8)
    return jnp.pad(flat, (0, pad)).reshape(-1, PACK_COLS)


def _unpack(packed, shapes):
    flat, out, off = packed.reshape(-1), [], 0
    for shp in shapes:
        n = math.prod(shp)
        out.append(flat[off:off + n].reshape(shp))
        off += n
    return out


def _layer_small(p, conv_full, l):
    wl = {}
    for k in ["mix_pre_g", "mix_post_g", "mlp_pre_g", "mlp_post_g", "gmlp_ln_g", "gmlp_ln_b", "conv_b", "conv_ln_g", "conv_ln_b"]:
        wl[k] = p[k][l][None, :]
    wm = p["gmlp_ws"][l] * jnp.tril(jnp.ones((CHUNK, CHUNK), F32))
    wl["ws"] = wm.astype(BF16)
    wl["ws_t"] = jnp.transpose(wm, (0, 2, 1)).astype(BF16)
    wl["bs_exp"] = jnp.repeat(p["gmlp_bs"][l].T, GROUP_DIM, axis=1)
    wl["bf_exp"] = jnp.repeat(p["fox_bf"][l], HEAD_DIM)[None, :]
    wl["conv_w"] = jnp.pad(conv_full[l], ((0, CONV_HALO - CONV_WIDTH), (0, 0)))
    return wl


def kernel(x, c, ada_w, ada_b, mix_pre_g, mix_post_g, mlp_pre_g, mlp_post_g, w_in, gmlp_ln_g, gmlp_ln_b, gmlp_ws, gmlp_bs, w_a_out, conv_w, conv_b, conv_ln_g, conv_ln_b, w_b_out, fox_bf, w_c_out, w_out, mlp_w1, mlp_w2, loss_target, m_ada_w, m_ada_b, m_mix_pre_g, m_mix_post_g, m_mlp_pre_g, m_mlp_post_g, m_w_in, m_gmlp_ln_g, m_gmlp_ln_b, m_gmlp_ws, m_gmlp_bs, m_w_a_out, m_conv_w, m_conv_b, m_conv_ln_g, m_conv_ln_b, m_w_b_out, m_fox_bf, m_w_c_out, m_w_out, m_mlp_w1, m_mlp_w2, v_ada_w, v_ada_b, v_mix_pre_g, v_mix_post_g, v_mlp_pre_g, v_mlp_post_g, v_w_in, v_gmlp_ln_g, v_gmlp_ln_b, v_gmlp_ws, v_gmlp_bs, v_w_a_out, v_conv_w, v_conv_b, v_conv_ln_g, v_conv_ln_b, v_w_b_out, v_fox_bf, v_w_c_out, v_w_out, v_mlp_w1, v_mlp_w2):
    args = (x, c, ada_w, ada_b, mix_pre_g, mix_post_g, mlp_pre_g, mlp_post_g, w_in, gmlp_ln_g, gmlp_ln_b, gmlp_ws, gmlp_bs, w_a_out,
            conv_w, conv_b, conv_ln_g, conv_ln_b, w_b_out, fox_bf, w_c_out, w_out, mlp_w1, mlp_w2)
    ms = (m_ada_w, m_ada_b, m_mix_pre_g, m_mix_post_g, m_mlp_pre_g, m_mlp_post_g, m_w_in, m_gmlp_ln_g, m_gmlp_ln_b, m_gmlp_ws, m_gmlp_bs,
          m_w_a_out, m_conv_w, m_conv_b, m_conv_ln_g, m_conv_ln_b, m_w_b_out, m_fox_bf, m_w_c_out, m_w_out, m_mlp_w1, m_mlp_w2)
    vs = (v_ada_w, v_ada_b, v_mix_pre_g, v_mix_post_g, v_mlp_pre_g, v_mlp_post_g, v_w_in, v_gmlp_ln_g, v_gmlp_ln_b, v_gmlp_ws, v_gmlp_bs,
          v_w_a_out, v_conv_w, v_conv_b, v_conv_ln_g, v_conv_ln_b, v_w_b_out, v_fox_bf, v_w_c_out, v_w_out, v_mlp_w1, v_mlp_w2)
    p = dict(zip(ARG_NAMES, args))
    mom = dict(zip(WEIGHTS, ms))
    var = dict(zip(WEIGHTS, vs))
    nl = ada_w.shape[0]
    s, d = x.shape[1], x.shape[2]
    me = 4 * lax.axis_index("x") + 2 * lax.axis_index("y") + lax.axis_index("c")

    c_all, conv_all = exchange("gather_c", [c, conv_w], scatter=False)
    c_all = c_all.reshape(N_DEV, d)
    n_ada = ada_w.shape[2]
    mod_parts = ada_fwd(c_all, ada_w)
    (mod_recv,) = exchange("scatter_mod", [jnp.transpose(mod_parts, (1, 0, 2))], scatter=True)
    conv_full = jnp.transpose(conv_all, (1, 2, 0, 3)).reshape(nl, CONV_WIDTH, D_MIX)

    def full_matrix(k, land, own):
        g = lax.dynamic_update_index_in_dim(land, own, me, 0)
        r, cc = own.shape
        if k == "w_in":
            return _w_in_to_kernel_layout(g, d)
        return jnp.transpose(g, (1, 0, 2)).reshape(r, N_DEV * cc) if k in COL_SHARDED else g.reshape(N_DEV * r, cc)

    started = [mod_recv]

    def fetch(l, tag):
        keys = GROUPS[tag]
        st = exchange_start(f"gather_{tag}{l}_start", [p[k][l].astype(BF16) for k in keys], scatter=False, after=started[-1])
        started.append(st["token"])

        def get(after):
            owns, lands = exchange_wait(f"gather_{tag}{l}_wait", st, after)
            return {k: full_matrix(k, land, o) for k, land, o in zip(keys, lands, owns)}

        return get

    getters = [{tag: fetch(l, tag) for tag in GROUPS} for l in range(nl)]
    mod = jnp.transpose(mod_recv, (1, 0, 2)).reshape(nl, N_DEV * n_ada) + ada_b + started[-1][0:1, 0:1]
    mods = [mod[l].reshape(6, d) for l in range(nl)]

    sent = {}

    def emitter(l):
        def emit(tag, grads_big):
            keys = GROUPS[tag]
            send = []
            for k in keys:
                gk = grads_big[k]
                if k == "w_in":
                    gk = _w_in_grad_blocks(gk, d)
                elif k in COL_SHARDED:
                    r, cc = gk.shape[0], gk.shape[1] // N_DEV
                    gk = jnp.transpose(gk.reshape(r, N_DEV, cc), (1, 0, 2))
                else:
                    gk = gk.reshape(N_DEV, gk.shape[0] // N_DEV, gk.shape[1])
                send.append(gk.astype(BF16))
            sent[(l, tag)] = (keys, exchange_start(f"scatter_{tag}{l}_start", send, scatter=True))
            return sent[(l, tag)][1]["token"]

        return emit

    small_sent = {}

    def small_emitter(l):
        def emit_small(dmod_l, g):
            packed = _pack([dmod_l] + [g[k] for k in SMALL[1:]] + [g["conv_w"]])
            small_sent[l] = exchange_start(f"gather_small{l}_start", [packed], scatter=False)
            return small_sent[l]["token"]

        return emit_small

    layers = [dict(small=_layer_small(p, conv_full, l), emit_small=small_emitter(l), get_in=getters[l]["in"], get_abco=getters[l]["abco"], get_mlp=getters[l]["mlp"],
                   emit=emitter(l)) for l in range(nl)]
    loss_local, dx, dmods, grads = local_step(x[0], loss_target[0], mods, layers)
    loss = lax.psum(loss_local, ("x", "y", "c"))
    grad_x = dx[None]

    out = {k: [None] * 4 for k in WEIGHTS}

    def shard_update(name, parts, k):
        shp = p[k].shape
        flat = lambda a: a.reshape(nl, -1, shp[-1])
        res = adamw_sum(name, [pt.reshape((pt.shape[0],) + flat(p[k]).shape[1:]) for pt in parts], flat(p[k]), flat(mom[k]), flat(var[k]))
        out[k] = [a.reshape(shp) for a in res]

    parts = {}
    for l in reversed(range(nl)):
        for tag in ("mlp", "abco", "in"):
            keys, st = sent[(l, tag)]
            sends, lands = exchange_wait(f"scatter_{tag}{l}_wait", st, dx)
            for k, land, sd in zip(keys, lands, sends):
                own = lax.dynamic_index_in_dim(sd, me, 0, keepdims=False)
                parts[(k, l)] = lax.dynamic_update_index_in_dim(land, own, me, 0)
    for k in BIG:
        shard_update("adamw_" + k, [parts[(k, l)] for l in range(nl)], k)

    small_shapes = [p[k].shape[1:] for k in SMALL] + [(CONV_WIDTH, D_MIX)]
    small_all = []
    for l in range(nl):
        srcs, lands = exchange_wait(f"gather_small{l}_wait", small_sent[l], out[BIG[-1]][0])
        small_all.append(lax.dynamic_update_index_in_dim(lands[0], srcs[0], me, 0))
    zeros_conv = jnp.zeros(small_shapes[-1], F32)
    packs = [jnp.stack([_pack([src[k][l] for k in SMALL] + [zeros_conv]) for l in range(nl)]) for src in (p, mom, var)]
    small_out = [[_unpack(o[l], small_shapes) for l in range(nl)] for o in adamw_sum("adamw_small", small_all, *packs)]
    for i, k in enumerate(SMALL):
        for j in range(4):
            out[k][j] = jnp.stack([small_out[j][l][i] for l in range(nl)])

    n_conv = conv_w.shape[2]
    shard_update("adamw_conv_w", [lax.dynamic_slice_in_dim(small_out[0][l][-1], me * n_conv, n_conv, axis=1)[None] for l in range(nl)], "conv_w")

    dmod_all = jnp.stack([small_all[l].reshape(N_DEV, -1)[:, :6 * d] for l in range(nl)])
    dmod_mine = lax.dynamic_slice_in_dim(dmod_all, me * n_ada, n_ada, axis=2)
    g_ada = ada_bwd(c_all.T, dmod_mine)
    shard_update("adamw_ada_w", [g_ada[l][None] for l in range(nl)], "ada_w")

    res = [loss, grad_x]
    for j in range(4):
        res += [out[k][j] for k in WEIGHTS]
    return tuple(res)
```

```python
import functools
import math

import jax
import jax.numpy as jnp
from jax import lax
from jax.experimental import pallas as pl
from jax.experimental.pallas import tpu as pltpu

F32 = jnp.float32
BF16 = jnp.bfloat16
MESH = pl.DeviceIdType.MESH
N_DEV = 8
NORM_EPS = 1e-6
D_MIX = 512
N_HEADS = 8
HEAD_DIM = 64
GROUP_DIM = 64
CHUNK = 128
CONV_WIDTH = 31
CONV_HALO = 32
LANES = 128
ADAM_LR, ADAM_B1, ADAM_B2, ADAM_EPS, ADAM_WD, ADAM_STEP = 0.001, 0.9, 0.999, 1e-08, 0.01, 10
VMEM_LIMIT = 56 * 1024 * 1024
HIGHEST = lax.Precision.HIGHEST


def _tile(dim, pref, mult=LANES):
    t = min(pref, dim)
    t -= t % mult
    while t >= mult:
        if dim % t == 0:
            return t
        t -= mult
    return dim


def _params(sem):
    return pltpu.CompilerParams(dimension_semantics=sem, vmem_limit_bytes=VMEM_LIMIT)


def rowwise(name, fn, rows, consts, outs, accs=(), ts=256):
    s = rows[0][0].shape[0]
    ts = min(ts, s)
    nr, nc, no, na = len(rows), len(consts), len(outs), len(accs)

    def body(*refs):
        vals = [r[...] for r in refs[:nr + nc]]
        res = fn(*vals)
        if not isinstance(res, (tuple, list)):
            res = (res,)
        for r, v in zip(refs[nr + nc:nr + nc + no], res[:no]):
            r[...] = v.astype(r.dtype)
        if na:
            acc_refs = refs[nr + nc + no:]

            @pl.when(pl.program_id(0) == 0)
            def _():
                for r in acc_refs:
                    r[...] = jnp.zeros(r.shape, r.dtype)

            for r, v in zip(acc_refs, res[no:]):
                r[...] += v.astype(F32)

    in_specs = [pl.BlockSpec((ts, w), functools.partial(lambda i, cb: (i, cb), cb=cb)) for (_, cb, w) in rows]
    in_specs += [pl.BlockSpec(c.shape, lambda i: (0, 0)) for c in consts]
    out_specs = [pl.BlockSpec((ts, w), lambda i: (i, 0)) for (w, _) in outs]
    out_specs += [pl.BlockSpec(shp, lambda i: (0, 0)) for shp in accs]
    out_shape = [jax.ShapeDtypeStruct((s, w), dt) for (w, dt) in outs]
    out_shape += [jax.ShapeDtypeStruct(shp, F32) for shp in accs]
    res = pl.pallas_call(
        body, name=name, grid=(s // ts,), in_specs=in_specs, out_specs=out_specs, out_shape=out_shape,
        compiler_params=_params(("arbitrary",) if na else ("parallel",)),
    )(*[a for (a, _, _) in rows], *consts)
    return res


def rowwise_vjp(name, f, rows, consts, cts, grad_dtypes, ts=256):
    nr, nc, nt = len(rows), len(consts), len(cts)
    keep = [i for i, dt in enumerate(grad_dtypes) if dt is not None]

    def g(*vals):
        rv = [v.astype(F32) for v in vals[:nr]]
        ctv = tuple(v.astype(F32) for v in vals[nr:nr + nt])
        cv = list(vals[nr + nt:])
        _, vjp = jax.vjp(lambda *a: tuple(f(*a)), *rv, *cv)
        grads = vjp(ctv)
        return tuple(grads[i] for i in keep) + tuple(grads[nr:])

    outs = [(rows[i][2], grad_dtypes[i]) for i in keep]
    return rowwise(name, g, list(rows) + list(cts), consts, outs, accs=[c.shape for c in consts], ts=ts)


def matmul(name, a, b, *, ta=False, tb=False, out_dtypes=(F32,), epilogue=None, epi=(), tm=1024, tn=1024, tk=1024, dep=None):
    m, k = (a.shape[1], a.shape[0]) if ta else a.shape
    n = b.shape[0] if tb else b.shape[1]
    assert (b.shape[1] if tb else b.shape[0]) == k
    tm, tn, tk = _tile(m, tm), _tile(n, tn), _tile(k, tk)
    nk = k // tk
    ne, no = len(epi), len(out_dtypes)
    dims = (((0 if ta else 1,), (1 if tb else 0,)), ((), ()))

    def body(*refs):
        a_ref, b_ref = refs[0], refs[1]
        epi_refs = refs[2:2 + ne]
        n_in = 2 + ne + (dep is not None)
        out_refs = refs[n_in:n_in + no]
        part = lax.dot_general(a_ref[...].astype(BF16), b_ref[...].astype(BF16), dims, preferred_element_type=F32)

        def finish(acc):
            res = (acc,) if epilogue is None else epilogue(acc, *[r[...] for r in epi_refs])
            for r, v in zip(out_refs, res):
                r[...] = v.astype(r.dtype)

        if nk == 1:
            finish(part)
        else:
            acc_ref = refs[-1]
            kk = pl.program_id(2)

            @pl.when(kk == 0)
            def _():
                acc_ref[...] = part

            @pl.when(kk > 0)
            def _():
                acc_ref[...] += part

            @pl.when(kk == nk - 1)
            def _():
                finish(acc_ref[...])

    a_spec = pl.BlockSpec((tk, tm), lambda i, j, kk: (kk, i)) if ta else pl.BlockSpec((tm, tk), lambda i, j, kk: (i, kk))
    b_spec = pl.BlockSpec((tn, tk), lambda i, j, kk: (j, kk)) if tb else pl.BlockSpec((tk, tn), lambda i, j, kk: (kk, j))
    epi_specs = []
    for (arr, col0) in epi:
        assert col0 % tn == 0
        epi_specs.append(pl.BlockSpec((tm, tn), functools.partial(lambda i, j, kk, c0: (i, j + c0), c0=col0 // tn)))
    res = pl.pallas_call(
        body, name=name, grid=(m // tm, n // tn, nk),
        in_specs=[a_spec, b_spec] + epi_specs + ([] if dep is None else [pl.BlockSpec(dep.shape, lambda i, j, kk: (0, 0))]),
        out_specs=[pl.BlockSpec((tm, tn), lambda i, j, kk: (i, j)) for _ in out_dtypes],
        out_shape=[jax.ShapeDtypeStruct((m, n), dt) for dt in out_dtypes],
        scratch_shapes=[pltpu.VMEM((tm, tn), F32)] if nk > 1 else [],
        compiler_params=_params(("parallel", "parallel", "arbitrary")),
    )(a, b, *[arr for (arr, _) in epi], *([] if dep is None else [dep]))
    return res[0] if no == 1 else res


def _rms(x, g):
    return x * lax.rsqrt(jnp.mean(x * x, axis=-1, keepdims=True) + NORM_EPS) * g


def _ln(x, g, b):
    mu = jnp.mean(x, axis=-1, keepdims=True)
    xc = x - mu
    var = jnp.mean(xc * xc, axis=-1, keepdims=True)
    return xc * lax.rsqrt(var + NORM_EPS) * g + b


def _gelu(x):
    return 0.5 * x * (1.0 + jnp.tanh(math.sqrt(2.0 / math.pi) * (x + 0.044715 * (x * x * x))))


def _sigmoid(x):
    return 1.0 / (1.0 + jnp.exp(-x))


def _silu(x):
    return x * _sigmoid(x)


def _log_sigmoid(x):
    return jnp.minimum(x, 0.0) - jnp.log(1.0 + jnp.exp(-jnp.abs(x)))


def _f_pre(x, g, sc, sh):
    return (_rms(x, g) * (1.0 + sc) + sh,)


def _f_post(y, g, gt):
    return (gt * _rms(y, g),)


def _f_a1(u_raw, v_raw, g, b):
    return _gelu(u_raw), _ln(_gelu(v_raw), g, b)


def _f_glu(val, gate):
    return (val * _sigmoid(gate),)


def _f_lnsilu(zc, g, b):
    return (_silu(_ln(zc, g, b)),)


def _f_merge(g0, g1, g2, ya, yb, yc):
    return (_sigmoid(g0) * ya + _sigmoid(g1) * yb + _sigmoid(g2) * yc,)


def _lane_lt64(shape):
    return lax.broadcasted_iota(jnp.int32, shape, 1) < HEAD_DIM


def spatial_fwd(vln, u, w_bf, b_exp, rows_per_step=512):
    s = vln.shape[0]
    tr = min(rows_per_step, s)

    def body(v_ref, u_ref, w_ref, b_ref, sv_ref, ya_ref):
        lo = _lane_lt64((CHUNK, LANES))
        for ch in range(tr // CHUNK):
            r0 = ch * CHUNK
            for p in range(D_MIX // LANES):
                vp = v_ref[r0:r0 + CHUNK, p * LANES:(p + 1) * LANES]
                o0 = jnp.dot(w_ref[2 * p], vp, preferred_element_type=F32)
                o1 = jnp.dot(w_ref[2 * p + 1], vp, preferred_element_type=F32)
                sv = jnp.where(lo, o0, o1) + b_ref[:, p * LANES:(p + 1) * LANES]
                sv_ref[r0:r0 + CHUNK, p * LANES:(p + 1) * LANES] = sv
                ya_ref[r0:r0 + CHUNK, p * LANES:(p + 1) * LANES] = (
                    u_ref[r0:r0 + CHUNK, p * LANES:(p + 1) * LANES] * sv).astype(BF16)

    row = pl.BlockSpec((tr, D_MIX), lambda i: (i, 0))
    return pl.pallas_call(
        body, name="spatial_fwd", grid=(s // tr,),
        in_specs=[row, row, pl.BlockSpec(w_bf.shape, lambda i: (0, 0, 0)), pl.BlockSpec(b_exp.shape, lambda i: (0, 0))],
        out_specs=[row, row],
        out_shape=[jax.ShapeDtypeStruct((s, D_MIX), F32), jax.ShapeDtypeStruct((s, D_MIX), BF16)],
        compiler_params=_params(("parallel",)),
    )(vln, u, w_bf, b_exp)


def spatial_bwd(dya, u, sv, vln, wt_bf, rows_per_step=512):
    s = vln.shape[0]
    tr = min(rows_per_step, s)
    ng = wt_bf.shape[0]

    def body(dya_ref, u_ref, sv_ref, v_ref, wt_ref, du_ref, dv_ref, dw_ref, db_ref):
        @pl.when(pl.program_id(0) == 0)
        def _():
            dw_ref[...] = jnp.zeros(dw_ref.shape, F32)
            db_ref[...] = jnp.zeros(db_ref.shape, F32)

        lo = _lane_lt64((CHUNK, LANES))
        for ch in range(tr // CHUNK):
            r0 = ch * CHUNK
            for p in range(D_MIX // LANES):
                cs = slice(p * LANES, (p + 1) * LANES)
                dya_p = dya_ref[r0:r0 + CHUNK, cs].astype(F32)
                du_ref[r0:r0 + CHUNK, cs] = dya_p * sv_ref[r0:r0 + CHUNK, cs]
                dsv = dya_p * u_ref[r0:r0 + CHUNK, cs]
                db_ref[:, cs] += dsv
                dsv0 = jnp.where(lo, dsv, 0.0).astype(BF16)
                dsv1 = jnp.where(lo, 0.0, dsv).astype(BF16)
                vp = v_ref[r0:r0 + CHUNK, cs]
                d0 = jnp.dot(wt_ref[2 * p], dsv0, preferred_element_type=F32)
                d1 = jnp.dot(wt_ref[2 * p + 1], dsv1, preferred_element_type=F32)
                dv_ref[r0:r0 + CHUNK, cs] = d0 + d1
                nt = (((1,), (1,)), ((), ()))
                dw_ref[2 * p] += lax.dot_general(dsv0, vp, nt, preferred_element_type=F32)
                dw_ref[2 * p + 1] += lax.dot_general(dsv1, vp, nt, preferred_element_type=F32)

    row = pl.BlockSpec((tr, D_MIX), lambda i: (i, 0))
    return pl.pallas_call(
        body, name="spatial_bwd", grid=(s // tr,),
        in_specs=[row, row, row, row, pl.BlockSpec(wt_bf.shape, lambda i: (0, 0, 0))],
        out_specs=[row, row, pl.BlockSpec((ng, CHUNK, CHUNK), lambda i: (0, 0, 0)), pl.BlockSpec((CHUNK, D_MIX), lambda i: (0, 0))],
        out_shape=[jax.ShapeDtypeStruct((s, D_MIX), F32), jax.ShapeDtypeStruct((s, D_MIX), F32),
                   jax.ShapeDtypeStruct((ng, CHUNK, CHUNK), F32), jax.ShapeDtypeStruct((CHUNK, D_MIX), F32)],
        compiler_params=_params(("arbitrary",)),
    )(dya, u, sv, vln, wt_bf)


def _windows(ref, first, count, ts):
    for r in range(8):
        ks = [k for k in range(count) if (first + k) % 8 == r]
        if ks:
            base = first + ks[0]
            blk = ref[base:base + ks[-1] - ks[0] + ts, :]
            for k in ks:
                yield k, blk[k - ks[0]:k - ks[0] + ts, :]


def conv_fwd(proj, cb_val, cb_gate, w_pad, cb, ln_g, ln_b, ts=256):
    s = proj.shape[0]
    ts = min(ts, s)
    per = ts // CONV_HALO

    def body(val_ref, gate_ref, pval_ref, pgate_ref, w_ref, cb_ref, g_ref, b_ref, zc_ref, yb_ref, ext_ref):
        i = pl.program_id(0)
        zprev = pval_ref[...] * _sigmoid(pgate_ref[...])
        ext_ref[0:CONV_HALO, :] = jnp.where(i > 0, zprev, 0.0)
        ext_ref[CONV_HALO:, :] = val_ref[...] * _sigmoid(gate_ref[...])
        acc = jnp.zeros((ts, D_MIX), F32)
        for j, win in _windows(ext_ref, CONV_HALO - (CONV_WIDTH - 1), CONV_WIDTH, ts):
            acc = acc + w_ref[j:j + 1, :] * win
        zc = acc + cb_ref[...]
        zc_ref[...] = zc
        yb_ref[...] = _f_lnsilu(zc, g_ref[...], b_ref[...])[0].astype(BF16)

    def cur(c):
        return pl.BlockSpec((ts, D_MIX), functools.partial(lambda i, c: (i, c), c=c))

    def prev(c):
        return pl.BlockSpec((CONV_HALO, D_MIX), functools.partial(lambda i, c: (jnp.maximum(i * per - 1, 0), c), c=c))

    const = lambda a: pl.BlockSpec(a.shape, lambda i: (0, 0))
    out = pl.BlockSpec((ts, D_MIX), lambda i: (i, 0))
    return pl.pallas_call(
        body, name="conv_fwd", grid=(s // ts,),
        in_specs=[cur(cb_val), cur(cb_gate), prev(cb_val), prev(cb_gate), const(w_pad), const(cb), const(ln_g), const(ln_b)],
        out_specs=[out, out],
        out_shape=[jax.ShapeDtypeStruct((s, D_MIX), F32), jax.ShapeDtypeStruct((s, D_MIX), BF16)],
        scratch_shapes=[pltpu.VMEM((CONV_HALO + ts, D_MIX), F32)],
        compiler_params=_params(("parallel",)),
    )(proj, proj, proj, proj, w_pad, cb, ln_g, ln_b)


def conv_bwd(proj, cb_val, cb_gate, zc, dyb, w_pad, ln_g, ln_b, ts=256):
    s = proj.shape[0]
    ts = min(ts, s)
    per = ts // CONV_HALO
    n_tiles = s // ts
    n_halo = s // CONV_HALO

    def body(val_ref, gate_ref, pval_ref, pgate_ref, zc_ref, dyb_ref, nzc_ref, ndyb_ref, w_ref, g_ref, b_ref,
             dval_ref, dgate_ref, dw_ref, dcb_ref, dg_ref, db_ref, zext_ref, dext_ref):
        i = pl.program_id(0)

        @pl.when(i == 0)
        def _():
            for r in (dw_ref, dcb_ref, dg_ref, db_ref):
                r[...] = jnp.zeros(r.shape, F32)

        g, b = g_ref[...], b_ref[...]
        _, vjp = jax.vjp(lambda z, gg, bb: _f_lnsilu(z, gg, bb)[0], zc_ref[...], g, b)
        dzc, dg, db = vjp(dyb_ref[...].astype(F32))
        dg_ref[...] += dg
        db_ref[...] += db
        dcb_ref[...] += jnp.sum(dzc, axis=0, keepdims=True)
        _, vjp_n = jax.vjp(lambda z: _f_lnsilu(z, g, b)[0], nzc_ref[...])
        (dzc_next,) = vjp_n(ndyb_ref[...].astype(F32))
        dext_ref[0:ts, :] = dzc
        dext_ref[ts:, :] = jnp.where(i < n_tiles - 1, dzc_next, 0.0)
        val, gate = val_ref[...], gate_ref[...]
        zprev = pval_ref[...] * _sigmoid(pgate_ref[...])
        zext_ref[0:CONV_HALO, :] = jnp.where(i > 0, zprev, 0.0)
        zext_ref[CONV_HALO:, :] = val * _sigmoid(gate)
        dz = jnp.zeros((ts, D_MIX), F32)
        for shift, win in _windows(dext_ref, 0, CONV_WIDTH, ts):
            j = CONV_WIDTH - 1 - shift
            dz = dz + w_ref[j:j + 1, :] * win
        for j, win in _windows(zext_ref, CONV_HALO - (CONV_WIDTH - 1), CONV_WIDTH, ts):
            dw_ref[j:j + 1, :] += jnp.sum(dzc * win, axis=0, keepdims=True)
        _, vjp_glu = jax.vjp(lambda a, c: _f_glu(a, c)[0], val, gate)
        dval, dgate = vjp_glu(dz)
        dval_ref[...] = dval.astype(BF16)
        dgate_ref[...] = dgate.astype(BF16)

    def cur(c):
        return pl.BlockSpec((ts, D_MIX), functools.partial(lambda i, c: (i, c), c=c))

    def prev(c):
        return pl.BlockSpec((CONV_HALO, D_MIX), functools.partial(lambda i, c: (jnp.maximum(i * per - 1, 0), c), c=c))

    nxt = pl.BlockSpec((CONV_HALO, D_MIX), lambda i: (jnp.minimum((i + 1) * per, n_halo - 1), 0))
    const = lambda a: pl.BlockSpec(a.shape, lambda i: (0, 0))
    out = pl.BlockSpec((ts, D_MIX), lambda i: (i, 0))
    vec = pl.BlockSpec((1, D_MIX), lambda i: (0, 0))
    return pl.pallas_call(
        body, name="conv_bwd", grid=(n_tiles,),
        in_specs=[cur(cb_val), cur(cb_gate), prev(cb_val), prev(cb_gate), out, out, nxt, nxt, const(w_pad), const(ln_g), const(ln_b)],
        out_specs=[out, out, pl.BlockSpec((CONV_HALO, D_MIX), lambda i: (0, 0)), vec, vec, vec],
        out_shape=[jax.ShapeDtypeStruct((s, D_MIX), BF16), jax.ShapeDtypeStruct((s, D_MIX), BF16),
                   jax.ShapeDtypeStruct((CONV_HALO, D_MIX), F32)] + [jax.ShapeDtypeStruct((1, D_MIX), F32)] * 3,
        scratch_shapes=[pltpu.VMEM((CONV_HALO + ts, D_MIX), F32), pltpu.VMEM((ts + CONV_HALO, D_MIX), F32)],
        compiler_params=_params(("arbitrary",)),
    )(proj, proj, proj, proj, zc, dyb, zc, dyb, w_pad, ln_g, ln_b)


def forget_cumsum(proj, cb_f, bf_exp, t=256):
    s = proj.shape[0]
    t = min(t, s)

    def body(f_ref, bf_ref, out_ref, carry_ref):
        @pl.when(pl.program_id(0) == 0)
        def _():
            carry_ref[...] = jnp.zeros(carry_ref.shape, F32)

        lf = _log_sigmoid(f_ref[...] + bf_ref[...])
        tri = (lax.broadcasted_iota(jnp.int32, (t, t), 1) <= lax.broadcasted_iota(jnp.int32, (t, t), 0)).astype(F32)
        c = jnp.dot(tri, lf, precision=HIGHEST, preferred_element_type=F32) + carry_ref[...]
        out_ref[...] = c
        carry_ref[...] = c[t - 1:t, :]

    return pl.pallas_call(
        body, name="forget_cumsum", grid=(s // t,),
        in_specs=[pl.BlockSpec((t, D_MIX), functools.partial(lambda i, c: (i, c), c=cb_f)), pl.BlockSpec((1, D_MIX), lambda i: (0, 0))],
        out_specs=pl.BlockSpec((t, D_MIX), lambda i: (i, 0)),
        out_shape=jax.ShapeDtypeStruct((s, D_MIX), F32),
        scratch_shapes=[pltpu.VMEM((1, D_MIX), F32)],
        compiler_params=_params(("arbitrary",)),
    )(proj, bf_exp)


def forget_bwd(proj, cb_f, bf_exp, dcum, t=256):
    s = proj.shape[0]
    t = min(t, s)
    n = s // t

    def body(f_ref, bf_ref, dc_ref, df_ref, dbf_ref, carry_ref):
        @pl.when(pl.program_id(0) == 0)
        def _():
            carry_ref[...] = jnp.zeros(carry_ref.shape, F32)
            dbf_ref[...] = jnp.zeros(dbf_ref.shape, F32)

        tri = (lax.broadcasted_iota(jnp.int32, (t, t), 1) >= lax.broadcasted_iota(jnp.int32, (t, t), 0)).astype(F32)
        r = jnp.dot(tri, dc_ref[...], precision=HIGHEST, preferred_element_type=F32) + carry_ref[...]
        carry_ref[...] = r[0:1, :]
        df = r * _sigmoid(-(f_ref[...] + bf_ref[...]))
        dbf_ref[...] += jnp.sum(df, axis=0, keepdims=True)
        live = lax.broadcasted_iota(jnp.int32, (t, D_MIX), 1) % HEAD_DIM == 0
        df_ref[...] = jnp.where(live, df, 0.0).astype(BF16)

    return pl.pallas_call(
        body, name="forget_bwd", grid=(n,),
        in_specs=[pl.BlockSpec((t, D_MIX), functools.partial(lambda i, c: (n - 1 - i, c), c=cb_f)), pl.BlockSpec((1, D_MIX), lambda i: (0, 0)),
                  pl.BlockSpec((t, D_MIX), lambda i: (n - 1 - i, 0))],
        out_specs=[pl.BlockSpec((t, D_MIX), lambda i: (n - 1 - i, 0)), pl.BlockSpec((1, D_MIX), lambda i: (0, 0))],
        out_shape=[jax.ShapeDtypeStruct((s, D_MIX), BF16), jax.ShapeDtypeStruct((1, D_MIX), F32)],
        scratch_shapes=[pltpu.VMEM((1, D_MIX), F32)],
        compiler_params=_params(("arbitrary",)),
    )(proj, bf_exp, dcum)


NT = (((1,), (1,)), ((), ()))
LOG2E = math.log2(math.e)
N_PAIR = D_MIX // LANES


def _split3(x):
    hi = x.astype(BF16).astype(F32)
    mid = (x - hi).astype(BF16).astype(F32)
    return hi, mid, x - hi - mid


def _triple(li, first, vals):
    out = jnp.where(li == first, vals[0], 0.0)
    for i in (1, 2):
        out = jnp.where(li == first + i, vals[i], out)
    return out


def _lane_ids(shape):
    lane = lax.broadcasted_iota(jnp.int32, shape, 1)
    return lane, lane % HEAD_DIM, lane < HEAD_DIM


def attn_prep(proj, cb_q, cum, ts=256):
    s = proj.shape[0]
    ts = min(ts, s)
    scale = LOG2E / math.sqrt(HEAD_DIM)

    def body(q_ref, k_ref, v_ref, c_ref, qe_ref, qo_ref, ke_ref, ko_ref, ve_ref, vo_ref):
        _, li, lo = _lane_ids((ts, LANES))
        one3 = lambda first: ((li >= first) & (li < first + 3)).astype(F32)
        for p in range(N_PAIR):
            ps = slice(p * LANES, (p + 1) * LANES)
            c3 = _split3(pltpu.roll(c_ref[:, ps] * LOG2E, HEAD_DIM, axis=1))
            eq = _triple(li, 0, c3) + one3(3)
            ek = one3(0) - _triple(li, 3, c3) + one3(6)
            ev = one3(0)
            for src, even, odd, extra, mul in ((q_ref, qe_ref, qo_ref, eq, scale), (k_ref, ke_ref, ko_ref, ek, 1.0), (v_ref, ve_ref, vo_ref, ev, 1.0)):
                x = src[:, ps] * mul
                even[:, ps] = jnp.where(lo, x, extra).astype(BF16)
                odd[:, ps] = jnp.where(lo, extra, x).astype(BF16)

    col = lambda c: pl.BlockSpec((ts, D_MIX), functools.partial(lambda i, c: (i, c), c=c))
    out = pl.BlockSpec((ts, D_MIX), lambda i: (i, 0))
    return pl.pallas_call(
        body, name="attn_prep", grid=(s // ts,),
        in_specs=[col(cb_q), col(cb_q + 1), col(cb_q + 2), pl.BlockSpec((ts, D_MIX), lambda i: (i, 0))],
        out_specs=[out] * 6, out_shape=[jax.ShapeDtypeStruct((s, D_MIX), BF16)] * 6,
        compiler_params=_params(("parallel",)),
    )(proj, proj, proj, cum)


def _pair_specs(s, t):
    return pl.BlockSpec((t, LANES), lambda p, i: (i, p)), pl.BlockSpec((s, LANES), lambda p, i: (0, p))


def attn_fwd(qe, qo, ke, ko, ve, vo, tq=512):
    s = qe.shape[0]
    tq = min(tq, s)

    def body(qe_ref, qo_ref, ke_ref, ko_ref, ve_ref, vo_ref, o_ref, qbe_ref, qbo_ref):
        qi = pl.program_id(1)
        qs, k_refs, v_refs = (qe_ref[...], qo_ref[...]), (ke_ref, ko_ref), (ve_ref, vo_ref)
        causal = lax.broadcasted_iota(jnp.int32, (tq, tq), 1) <= lax.broadcasted_iota(jnp.int32, (tq, tq), 0)

        def step(j, carry, diag):
            ks = pl.multiple_of(j * tq, tq)
            new = []
            for h in range(2):
                m, l, acc = carry[h]
                sc = lax.dot_general(qs[h], k_refs[h][pl.ds(ks, tq), :], NT, preferred_element_type=F32)
                if diag:
                    sc = jnp.where(causal, sc, -jnp.inf)
                m_new = jnp.maximum(m, jnp.max(sc, axis=1, keepdims=True))
                p = jnp.exp2(sc - m_new)
                alpha = jnp.exp2(m - m_new)
                l = alpha * l + jnp.sum(p, axis=1, keepdims=True)
                acc = alpha * acc + jnp.dot(p.astype(BF16), v_refs[h][pl.ds(ks, tq), :], preferred_element_type=F32)
                new.append((m_new, l, acc))
            return tuple(new)

        init = tuple((jnp.full((tq, 1), -jnp.inf, F32), jnp.zeros((tq, 1), F32), jnp.zeros((tq, LANES), F32)) for _ in range(2))
        carry = lax.fori_loop(0, qi, lambda j, c: step(j, c, False), init)
        (m0, l0, a0), (m1, l1, a1) = step(qi, carry, True)
        _, li, lo = _lane_ids((tq, LANES))
        o_ref[...] = jnp.where(lo, a0 / l0, a1 / l1).astype(o_ref.dtype)
        lse_lanes = (li >= 6) & (li < 9)
        for q, m, l, spare, out_ref in ((qs[0], m0, l0, ~lo, qbe_ref), (qs[1], m1, l1, lo, qbo_ref)):
            neg_lse = _triple(li, 6, _split3(-(m + jnp.log(l) * LOG2E)))
            out_ref[...] = jnp.where(spare & lse_lanes, neg_lse.astype(BF16), q)

    blk, full = _pair_specs(s, tq)
    return pl.pallas_call(
        body, name="attn_fwd", grid=(N_PAIR, s // tq),
        in_specs=[blk, blk, full, full, full, full],
        out_specs=[blk] * 3, out_shape=[jax.ShapeDtypeStruct((s, D_MIX), BF16)] * 3,
        compiler_params=_params(("parallel", "parallel")),
    )(qe, qo, ke, ko, ve, vo)


def attn_dq(qbe, qbo, ke, ko, ve, vo, do, dep, tq=512):
    s = qbe.shape[0]
    tq = min(tq, s)
    scale = 1.0 / math.sqrt(HEAD_DIM)

    def body(qe_ref, qo_ref, ke_ref, ko_ref, ve_ref, vo_ref, do_ref, dep_ref, dq_ref, dobe_ref, dobo_ref):
        qi = pl.program_id(1)
        _, li, lo = _lane_ids((tq, LANES))
        do_ = do_ref[...]
        qs, k_refs, v_refs = (qe_ref[...], qo_ref[...]), (ke_ref, ko_ref), (ve_ref, vo_ref)
        dos = (jnp.where(lo, do_, 0), jnp.where(lo, 0, do_))
        causal = lax.broadcasted_iota(jnp.int32, (tq, tq), 1) <= lax.broadcasted_iota(jnp.int32, (tq, tq), 0)

        def step(j, carry, diag):
            ks = pl.multiple_of(j * tq, tq)
            new = []
            for h in range(2):
                pdpk, pk, dsum = carry[h]
                kb = k_refs[h][pl.ds(ks, tq), :]
                sc = lax.dot_general(qs[h], kb, NT, preferred_element_type=F32)
                if diag:
                    sc = jnp.where(causal, sc, -jnp.inf)
                p = jnp.exp2(sc)
                pdp = p * lax.dot_general(dos[h], v_refs[h][pl.ds(ks, tq), :], NT, preferred_element_type=F32)
                new.append((pdpk + jnp.dot(pdp.astype(BF16), kb, preferred_element_type=F32),
                            pk + jnp.dot(p.astype(BF16), kb, preferred_element_type=F32),
                            dsum + jnp.sum(pdp, axis=1, keepdims=True)))
            return tuple(new)

        init = tuple((jnp.zeros((tq, LANES), F32), jnp.zeros((tq, LANES), F32), jnp.zeros((tq, 1), F32)) for _ in range(2))
        carry = lax.fori_loop(0, qi, lambda j, c: step(j, c, False), init)
        (a0, b0, s0), (a1, b1, s1) = step(qi, carry, True)
        dq_ref[...] = (jnp.where(lo, a0 - s0 * b0, a1 - s1 * b1) * scale).astype(dq_ref.dtype)
        dobe_ref[...] = jnp.where(lo, do_, _triple(li, 0, _split3(-s0)).astype(BF16))
        dobo_ref[...] = jnp.where(lo, _triple(li, 0, _split3(-s1)).astype(BF16), do_)

    blk, full = _pair_specs(s, tq)
    return pl.pallas_call(
        body, name="attn_dq", grid=(N_PAIR, s // tq),
        in_specs=[blk, blk, full, full, full, full, blk, pl.BlockSpec(dep.shape, lambda p, i: (0, 0))],
        out_specs=[blk] * 3, out_shape=[jax.ShapeDtypeStruct((s, D_MIX), BF16)] * 3,
        compiler_params=_params(("parallel", "parallel")),
    )(qbe, qbo, ke, ko, ve, vo, do, dep)


def attn_dkv(ke, ko, ve, vo, qbe, qbo, dobe, dobo, tk=512):
    s = ke.shape[0]
    tk = min(tk, s)
    nq = s // tk

    def body(ke_ref, ko_ref, ve_ref, vo_ref, qe_ref, qo_ref, de_ref, do_ref, dk_ref, dv_ref, dck_ref):
        kj = pl.program_id(1)
        lo = _lane_lt64((tk, LANES))
        ks_, vs_, q_refs, d_refs = (ke_ref[...], ko_ref[...]), (ve_ref[...], vo_ref[...]), (qe_ref, qo_ref), (de_ref, do_ref)
        causal = lax.broadcasted_iota(jnp.int32, (tk, tk), 0) <= lax.broadcasted_iota(jnp.int32, (tk, tk), 1)

        def step(i, carry, diag):
            qs = pl.multiple_of(i * tk, tk)
            new = []
            for h in range(2):
                dk, dv, dck = carry[h]
                qblk = q_refs[h][pl.ds(qs, tk), :]
                dblk = d_refs[h][pl.ds(qs, tk), :]
                st = lax.dot_general(ks_[h], qblk, NT, preferred_element_type=F32)
                if diag:
                    st = jnp.where(causal, st, -jnp.inf)
                pt = jnp.exp2(st)
                dst = pt * lax.dot_general(vs_[h], dblk, NT, preferred_element_type=F32)
                new.append((dk + jnp.dot(dst.astype(BF16), qblk, preferred_element_type=F32),
                            dv + jnp.dot(pt.astype(BF16), dblk, preferred_element_type=F32),
                            dck - jnp.sum(dst, axis=1, keepdims=True)))
            return tuple(new)

        init = tuple((jnp.zeros((tk, LANES), F32), jnp.zeros((tk, LANES), F32), jnp.zeros((tk, 1), F32)) for _ in range(2))
        carry = step(kj, init, True)
        (dk0, dv0, dc0), (dk1, dv1, dc1) = lax.fori_loop(kj + 1, nq, lambda i, c: step(i, c, False), carry)
        dk_ref[...] = (jnp.where(lo, dk0, dk1) * (1.0 / LOG2E)).astype(dk_ref.dtype)
        dv_ref[...] = jnp.where(lo, dv0, dv1).astype(dv_ref.dtype)
        dck_ref[...] = jnp.where(lo, dc0, dc1)

    blk, full = _pair_specs(s, tk)
    return pl.pallas_call(
        body, name="attn_dkv", grid=(N_PAIR, nq),
        in_specs=[blk, blk, blk, blk, full, full, full, full],
        out_specs=[blk] * 3,
        out_shape=[jax.ShapeDtypeStruct((s, D_MIX), BF16), jax.ShapeDtypeStruct((s, D_MIX), BF16), jax.ShapeDtypeStruct((s, D_MIX), F32)],
        compiler_params=_params(("parallel", "parallel")),
    )(ke, ko, ve, vo, qbe, qbo, dobe, dobo)


def _pre_bwd(name, x, g, sc, sh, dh, dres):
    d = x.shape[1]

    def fn(xv, dhv, dresv, gv, scv, shv):
        _, vjp = jax.vjp(lambda *a: _f_pre(*a)[0], xv, gv, scv, shv)
        dx, dg, dsc, dsh = vjp(dhv.astype(F32))
        return dx + dresv, dg, dsc, dsh

    return rowwise(name, fn, [(x, 0, d), (dh, 0, d), (dres, 0, d)], [g, sc, sh], [(d, F32)], accs=[(1, d)] * 3)


def layer_fwd(x, mod, layer):
    s, d = x.shape
    m = D_MIX
    w = dict(layer["small"])
    sh1, sc1, gt1, sh2, sc2, gt2 = (mod[i:i + 1] for i in range(6))
    cb = 3 * d // m
    (h,) = rowwise("pre1", _f_pre, [(x, 0, d)], [w["mix_pre_g"], sc1, sh1], [(d, BF16)])
    w.update(layer["get_in"](h))
    proj = matmul("w_in", h, w["w_in"])
    w.update(layer["get_abco"](proj))
    u, vln = rowwise("gmlp_in", _f_a1, [(proj, cb, m), (proj, cb + 1, m)], [w["gmlp_ln_g"], w["gmlp_ln_b"]], [(m, F32), (m, BF16)])
    sv, ya = spatial_fwd(vln, u, w["ws"], w["bs_exp"])
    y_a = matmul("w_a", ya, w["w_a_out"])
    zc, yb = conv_fwd(proj, cb + 2, cb + 3, w["conv_w"], w["conv_b"], w["conv_ln_g"], w["conv_ln_b"])
    y_b = matmul("w_b", yb, w["w_b_out"])
    cum = forget_cumsum(proj, cb + 7, w["bf_exp"])
    kv_ops = attn_prep(proj, cb + 4, cum)
    o, qbe, qbo = attn_fwd(*kv_ops)
    att = (qbe, qbo) + tuple(kv_ops[2:])
    y_c = matmul("w_c", o, w["w_c_out"])
    (merged,) = rowwise("merge", _f_merge, [(proj, 0, d), (proj, 1, d), (proj, 2, d), (y_a, 0, d), (y_b, 0, d), (y_c, 0, d)], [], [(d, BF16)])
    y = matmul("w_out", merged, w["w_out"])
    w.update(layer["get_mlp"](y))

    def post_pre(xv, yv, gp, gt, g2, sc, sh):
        x1 = xv + _f_post(yv, gp, gt)[0]
        return x1, _f_pre(x1, g2, sc, sh)[0]

    x1, h2 = rowwise("post1", post_pre, [(x, 0, d), (y, 0, d)], [w["mix_post_g"], gt1, w["mlp_pre_g"], sc2, sh2], [(d, F32), (d, BF16)])
    a_bf, r = matmul("w1", h2, w["mlp_w1"], out_dtypes=(BF16, BF16), epilogue=lambda acc: (acc, jnp.square(jnp.maximum(acc, 0.0))))
    y2 = matmul("w2", r, w["mlp_w2"])
    (x2,) = rowwise("post2", lambda xv, yv, g, gt: xv + _f_post(yv, g, gt)[0], [(x1, 0, d), (y2, 0, d)], [w["mlp_post_g"], gt2], [(d, F32)])
    saved = dict(w=w, x=x, h=h, proj=proj, u=u, vln=vln, sv=sv, ya=ya, y_a=y_a, zc=zc, yb=yb, y_b=y_b, att=att,
                 o=o, y_c=y_c, merged=merged, y=y, x1=x1, h2=h2, a_bf=a_bf, r=r, y2=y2)
    return x2, saved


def layer_bwd(dx2, mod, sv, emit, tok_in=None):
    x, proj, w = sv["x"], sv["proj"], sv["w"]
    s, d = x.shape
    m = D_MIX
    sh1, sc1, gt1, sh2, sc2, gt2 = (mod[i:i + 1] for i in range(6))
    cb = 3 * d // m
    g = {}
    if tok_in is not None:
        gt2 = gt2 + tok_in[0:1, 0:1]
    dy2, g["mlp_post_g"], dgt2 = rowwise_vjp("post2_b", _f_post, [(sv["y2"], 0, d)], [w["mlp_post_g"], gt2], [(dx2, 0, d)], [BF16])
    da = matmul("w2_dx", dy2, w["mlp_w2"], tb=True, out_dtypes=(BF16,),
                epilogue=lambda acc, a: (acc * (2.0 * jnp.maximum(a.astype(F32), 0.0)),), epi=[(sv["a_bf"], 0)])
    big = {}
    big["mlp_w2"] = matmul("w2_dw", sv["r"], dy2, ta=True, out_dtypes=(BF16,))
    dh2 = matmul("w1_dx", da, w["mlp_w1"], tb=True)
    big["mlp_w1"] = matmul("w1_dw", sv["h2"], da, ta=True, out_dtypes=(BF16,))
    tok = emit("mlp", big)
    dx1, g["mlp_pre_g"], dsc2, dsh2 = _pre_bwd("pre2_b", sv["x1"], w["mlp_pre_g"], sc2 + tok[0:1, 0:1], sh2, dh2, dx2)
    dy, g["mix_post_g"], dgt1 = rowwise_vjp("post1_b", _f_post, [(sv["y"], 0, d)], [w["mix_post_g"], gt1], [(dx1, 0, d)], [BF16])
    dmerged = matmul("w_out_dx", dy, w["w_out"], tb=True)
    big = {}
    big["w_out"] = matmul("w_out_dw", sv["merged"], dy, ta=True, out_dtypes=(BF16,))
    dg0, dg1, dg2, dya_, dyb_, dyc_ = rowwise_vjp(
        "merge_b", _f_merge, [(proj, 0, d), (proj, 1, d), (proj, 2, d), (sv["y_a"], 0, d), (sv["y_b"], 0, d), (sv["y_c"], 0, d)], [],
        [(dmerged, 0, d)], [BF16] * 6)
    dya_pre = matmul("w_a_dx", dya_, w["w_a_out"], tb=True)
    big["w_a_out"] = matmul("w_a_dw", sv["ya"], dya_, ta=True, out_dtypes=(BF16,))
    dyb_pre = matmul("w_b_dx", dyb_, w["w_b_out"], tb=True)
    big["w_b_out"] = matmul("w_b_dw", sv["yb"], dyb_, ta=True, out_dtypes=(BF16,))
    do = matmul("w_c_dx", dyc_, w["w_c_out"], tb=True, out_dtypes=(BF16,))
    big["w_c_out"] = matmul("w_c_dw", sv["o"], dyc_, ta=True, out_dtypes=(BF16,))
    tok = emit("abco", big)
    qbe, qbo, ke, ko, ve, vo = sv["att"]
    dq, dobe, dobo = attn_dq(qbe, qbo, ke, ko, ve, vo, do, tok)
    dk, dv, dcum = attn_dkv(ke, ko, ve, vo, qbe, qbo, dobe, dobo)
    df, dbf = forget_bwd(proj, cb + 7, w["bf_exp"], dcum)
    g["fox_bf"] = dbf[0, ::HEAD_DIM]
    dval, dgate, dwc, g["conv_b"], g["conv_ln_g"], g["conv_ln_b"] = conv_bwd(
        proj, cb + 2, cb + 3, sv["zc"], dyb_pre, w["conv_w"], w["conv_ln_g"], w["conv_ln_b"])
    g["conv_w"] = dwc[:CONV_WIDTH]
    du, dvln, dws, dbexp = spatial_bwd(dya_pre, sv["u"], sv["sv"], sv["vln"], w["ws_t"])
    g["gmlp_ws"] = dws * jnp.tril(jnp.ones((CHUNK, CHUNK), F32))
    g["gmlp_bs"] = dbexp.reshape(CHUNK, m // GROUP_DIM, GROUP_DIM).sum(-1).T
    du_raw, dv_raw, g["gmlp_ln_g"], g["gmlp_ln_b"] = rowwise_vjp(
        "gmlp_in_b", _f_a1, [(proj, cb, m), (proj, cb + 1, m)], [w["gmlp_ln_g"], w["gmlp_ln_b"]], [(du, 0, m), (dvln, 0, m)], [BF16, BF16])
    dproj = jnp.concatenate([dg0, dg1, dg2, du_raw, dv_raw, dval, dgate, dq, dk, dv, df], axis=1)
    tok = emit("in", {"w_in": matmul("w_in_dw", sv["h"], dproj, ta=True, out_dtypes=(BF16,))})
    dh = matmul("w_in_dx", dproj, w["w_in"], tb=True, dep=tok)
    dx, g["mix_pre_g"], dsc1, dsh1 = _pre_bwd("pre1_b", x, w["mix_pre_g"], sc1, sh1, dh, dx1)
    dmod = jnp.concatenate([dsh1, dsc1, dgt1, dsh2, dsc2, dgt2], axis=0)
    return dx, dmod, g


def local_step(x, target, mods, layers):
    d = x.shape[1]
    saved = []
    for l in range(len(layers)):
        x, sv = layer_fwd(x, mods[l], layers[l])
        saved.append(sv)

    def loss_fn(xv, tv):
        err = xv - tv
        return err * (1.0 / d), jnp.sum(err * err, axis=0, keepdims=True)

    dx, sq = rowwise("loss", loss_fn, [(x, 0, d), (target, 0, d)], [], [(d, F32)], accs=[(1, d)])
    loss = (0.5 / d) * jnp.sum(sq)
    dmods, grads = [None] * len(layers), [None] * len(layers)
    tok = None
    for l in reversed(range(len(layers))):
        dx, dmods[l], grads[l] = layer_bwd(dx, mods[l], saved[l], layers[l]["emit"], tok)
        tok = layers[l]["emit_small"](dmods[l], grads[l])
    return loss, dx, dmods, grads


def exchange(name, arrs, scatter):
    n = len(arrs)

    def body(*refs):
        in_refs, out_refs = refs[:n], refs[n:2 * n]
        send_sems, recv_sems, local_sems = refs[2 * n:]
        x, y, c = lax.axis_index("x"), lax.axis_index("y"), lax.axis_index("c")
        me = 4 * x + 2 * y + c
        local = []
        for a in range(n):
            src = in_refs[a].at[me] if scatter else in_refs[a]
            cp = pltpu.make_async_copy(src, out_refs[a].at[me], local_sems.at[a])
            cp.start()
            local.append(cp)
        remote = []
        for k in range(1, N_DEV):
            px, py, pc = x ^ ((k >> 2) & 1), y ^ ((k >> 1) & 1), c ^ (k & 1)
            peer = 4 * px + 2 * py + pc
            for a in range(n):
                src = in_refs[a].at[peer] if scatter else in_refs[a]
                cp = pltpu.make_async_remote_copy(
                    src_ref=src, dst_ref=out_refs[a].at[me], send_sem=send_sems.at[a * (N_DEV - 1) + k - 1],
                    recv_sem=recv_sems.at[a * (N_DEV - 1) + k - 1], device_id=(px, py, pc), device_id_type=MESH)
                cp.start()
                remote.append(cp)
        for cp in remote:
            cp.wait()
        for cp in local:
            cp.wait()

    hbm = pl.BlockSpec(memory_space=pltpu.HBM)
    out_shape = [jax.ShapeDtypeStruct(a.shape if scatter else (N_DEV,) + a.shape, a.dtype) for a in arrs]
    return pl.pallas_call(
        body, name=name, in_specs=[hbm] * n, out_specs=[hbm] * n, out_shape=out_shape,
        scratch_shapes=[pltpu.SemaphoreType.DMA((n * (N_DEV - 1),)), pltpu.SemaphoreType.DMA((n * (N_DEV - 1),)),
                        pltpu.SemaphoreType.DMA((n,))],
    )(*arrs)


def _peers(x, y, c):
    out = []
    for k in range(1, N_DEV):
        px, py, pc = x ^ ((k >> 2) & 1), y ^ ((k >> 1) & 1), c ^ (k & 1)
        out.append((k - 1, (px, py, pc), 4 * px + 2 * py + pc))
    return out


def _exchange_copies(srcs, lands, send_sems, recv_sems, scatter):
    x, y, c = lax.axis_index("x"), lax.axis_index("y"), lax.axis_index("c")
    me = 4 * x + 2 * y + c
    copies = []
    for slot, pos, peer in _peers(x, y, c):
        for a, (src, land) in enumerate(zip(srcs, lands)):
            copies.append(pltpu.make_async_remote_copy(
                src_ref=src.at[peer] if scatter else src, dst_ref=land.at[me],
                send_sem=send_sems.at[a * (N_DEV - 1) + slot], recv_sem=recv_sems.at[a * (N_DEV - 1) + slot],
                device_id=pos, device_id_type=MESH))
    return me, copies


def exchange_start(name, arrs, scatter, after=None):
    n = len(arrs)
    lands = [lax.empty(a.shape if scatter else (N_DEV,) + a.shape, a.dtype) for a in arrs]
    n_in = 2 * n + (after is not None)

    def body(*refs):
        srcs, lands_ = refs[:n], refs[n:2 * n]
        send_sems, recv_sems = refs[n_in], refs[n_in + 1]
        token = refs[n_in + 2 + 2 * n]
        _, copies = _exchange_copies(srcs, lands_, send_sems, recv_sems, scatter)
        for cp in copies:
            cp.start()
        token[...] = jnp.zeros(token.shape, token.dtype)

    hbm = pl.BlockSpec(memory_space=pltpu.HBM)
    sem = pl.BlockSpec(memory_space=pltpu.SEMAPHORE)
    n_sem = n * (N_DEV - 1)
    res = pl.pallas_call(
        body, name=name,
        out_shape=(pltpu.SemaphoreType.DMA((n_sem,)), pltpu.SemaphoreType.DMA((n_sem,)),
                   *[pltpu.HBM(a.shape, a.dtype) for a in arrs], *[pltpu.HBM(l.shape, l.dtype) for l in lands],
                   jax.ShapeDtypeStruct((8, LANES), F32)),
        in_specs=[hbm] * (2 * n) + ([] if after is None else [pl.BlockSpec(memory_space=pl.ANY)]),
        out_specs=(sem, sem, *([hbm] * (2 * n)), pl.BlockSpec(memory_space=pltpu.VMEM)),
        input_output_aliases={i: 2 + i for i in range(2 * n)},
        compiler_params=pltpu.CompilerParams(has_side_effects=pltpu.SideEffectType.DATAFLOW_SIDE_EFFECTING),
    )(*[pltpu.with_memory_space_constraint(a, pltpu.HBM) for a in arrs],
      *[pltpu.with_memory_space_constraint(l, pltpu.HBM) for l in lands], *([] if after is None else [after]))
    return dict(n=n, scatter=scatter, send=res[0], recv=res[1], srcs=res[2:2 + n], lands=res[2 + n:2 + 2 * n], token=res[2 + 2 * n])


def exchange_wait(name, st, after):
    n, scatter = st["n"], st["scatter"]
    after = list(after) if isinstance(after, (list, tuple)) else [after]

    def body(*refs):
        srcs, lands_ = refs[:n], refs[n:2 * n]
        send_sems, recv_sems = refs[2 * n], refs[2 * n + 1]
        _, copies = _exchange_copies(srcs, lands_, send_sems, recv_sems, scatter)
        for cp in copies:
            cp.wait_send()
            cp.wait_recv()

    hbm = pl.BlockSpec(memory_space=pltpu.HBM)
    sem = pl.BlockSpec(memory_space=pltpu.SEMAPHORE)
    res = pl.pallas_call(
        body, name=name,
        out_shape=tuple(pltpu.HBM(a.shape, a.dtype) for a in (*st["srcs"], *st["lands"])),
        in_specs=[hbm] * (2 * n) + [sem, sem] + [pl.BlockSpec(memory_space=pl.ANY)] * len(after), out_specs=tuple([hbm] * (2 * n)),
        input_output_aliases={i: i for i in range(2 * n)},
        compiler_params=pltpu.CompilerParams(has_side_effects=pltpu.SideEffectType.DATAFLOW_SIDE_EFFECTING),
    )(*st["srcs"], *st["lands"], st["send"], st["recv"], *after)
    return list(res[:n]), list(res[n:])


def adamw_sum(name, parts, w, m, v, tr=256):
    nl = len(parts)
    k, r, c = parts[0].shape
    tr = _tile(r, tr, 16)
    c1 = 1.0 - ADAM_B1 ** ADAM_STEP
    c2 = 1.0 - ADAM_B2 ** ADAM_STEP

    def body(*refs):
        p_refs = refs[:nl]
        w_ref, m_ref, v_ref, g_ref, d_ref, nm_ref, nv_ref = refs[nl:]
        for l in range(nl):
            @pl.when(pl.program_id(0) == l)
            def _(p_ref=p_refs[l]):
                grad = p_ref[0].astype(F32)
                for j in range(1, k):
                    grad = grad + p_ref[j].astype(F32)
                new_m = ADAM_B1 * m_ref[...] + (1.0 - ADAM_B1) * grad
                new_v = ADAM_B2 * v_ref[...] + (1.0 - ADAM_B2) * (grad * grad)
                m_hat = new_m / c1
                v_hat = new_v / c2
                g_ref[...] = grad
                d_ref[...] = -ADAM_LR * (m_hat / (jnp.sqrt(v_hat) + ADAM_EPS) + ADAM_WD * w_ref[...])
                nm_ref[...] = new_m
                nv_ref[...] = new_v

    part = lambda l: pl.BlockSpec((k, tr, c), functools.partial(lambda ll, i, l: (0, jnp.where(ll == l, i, 0), 0), l=l))
    blk = pl.BlockSpec((None, tr, c), lambda ll, i: (ll, i, 0))
    return pl.pallas_call(
        body, name=name, grid=(nl, r // tr),
        in_specs=[part(l) for l in range(nl)] + [blk, blk, blk],
        out_specs=[blk] * 4, out_shape=[jax.ShapeDtypeStruct((nl, r, c), F32)] * 4,
        compiler_params=_params(("parallel", "parallel")),
    )(*parts, w, m, v)


def adamw_sum_cols(name, parts, w, m, v, tc=256):
    nl = len(parts)
    k, r, c = parts[0].shape
    tc = _tile(c, tc)
    c1 = 1.0 - ADAM_B1 ** ADAM_STEP
    c2 = 1.0 - ADAM_B2 ** ADAM_STEP

    def body(*refs):
        p_refs = refs[:nl]
        w_ref, m_ref, v_ref, g_ref, d_ref, nm_ref, nv_ref = refs[nl:]
        for l in range(nl):
            @pl.when(pl.program_id(0) == l)
            def _(p_ref=p_refs[l]):
                grad = p_ref[0].astype(F32)
                for j in range(1, k):
                    grad = grad + p_ref[j].astype(F32)
                new_m = ADAM_B1 * m_ref[...] + (1.0 - ADAM_B1) * grad
                new_v = ADAM_B2 * v_ref[...] + (1.0 - ADAM_B2) * (grad * grad)
                m_hat = new_m / c1
                v_hat = new_v / c2
                g_ref[...] = grad
                d_ref[...] = -ADAM_LR * (m_hat / (jnp.sqrt(v_hat) + ADAM_EPS) + ADAM_WD * w_ref[...])
                nm_ref[...] = new_m
                nv_ref[...] = new_v

    part = lambda l: pl.BlockSpec((k, r, tc), functools.partial(lambda ll, j, l: (0, 0, jnp.where(ll == l, j, 0)), l=l))
    blk = pl.BlockSpec((r, tc), lambda ll, j: (0, ll * (c // tc) + j))
    return pl.pallas_call(
        body, name=name, grid=(nl, c // tc),
        in_specs=[part(l) for l in range(nl)] + [blk, blk, blk],
        out_specs=[blk] * 4, out_shape=[jax.ShapeDtypeStruct((r, nl * c), F32)] * 4,
        compiler_params=_params(("parallel", "parallel")),
    )(*parts, w, m, v)


def ada_fwd(c_all, ada_w):
    nl, d, n = ada_w.shape

    def body(c_ref, w_ref, o_ref):
        o_ref[...] = jnp.dot(_silu(c_ref[...]), w_ref[...], precision=HIGHEST, preferred_element_type=F32)

    return pl.pallas_call(
        body, name="ada_fwd", grid=(nl,),
        in_specs=[pl.BlockSpec((N_DEV, d), lambda l: (0, 0)), pl.BlockSpec((None, d, n), lambda l: (l, 0, 0))],
        out_specs=pl.BlockSpec((None, N_DEV, n), lambda l: (l, 0, 0)),
        out_shape=jax.ShapeDtypeStruct((nl, N_DEV, n), F32),
        compiler_params=_params(("parallel",)),
    )(c_all, ada_w)


def ada_bwd(c_all_t, dmod, td=256):
    d = c_all_t.shape[0]
    nl, _, n = dmod.shape
    td = _tile(d, td, 8)

    def body(c_ref, dm_ref, o_ref):
        ca = _silu(c_ref[...])
        acc = ca[:, 0:1] * dm_ref[0:1, :]
        for b in range(1, N_DEV):
            acc = acc + ca[:, b:b + 1] * dm_ref[b:b + 1, :]
        o_ref[...] = acc

    return pl.pallas_call(
        body, name="ada_bwd", grid=(nl, d // td),
        in_specs=[pl.BlockSpec((td, N_DEV), lambda l, i: (i, 0)), pl.BlockSpec((None, N_DEV, n), lambda l, i: (l, 0, 0))],
        out_specs=pl.BlockSpec((None, td, n), lambda l, i: (l, i, 0)),
        out_shape=jax.ShapeDtypeStruct((nl, d, n), F32),
        compiler_params=_params(("parallel", "parallel")),
    )(c_all_t, dmod)


ARG_NAMES = ["x", "c", "ada_w", "ada_b", "mix_pre_g", "mix_post_g", "mlp_pre_g", "mlp_post_g", "w_in", "gmlp_ln_g", "gmlp_ln_b",
             "gmlp_ws", "gmlp_bs", "w_a_out", "conv_w", "conv_b", "conv_ln_g", "conv_ln_b", "w_b_out", "fox_bf", "w_c_out",
             "w_out", "mlp_w1", "mlp_w2"]
WEIGHTS = ARG_NAMES[2:]
COL_SHARDED = ["w_in", "w_a_out", "w_b_out", "w_c_out", "mlp_w1"]
ROW_SHARDED = ["w_out", "mlp_w2"]
BIG = COL_SHARDED + ROW_SHARDED
GROUPS = {"in": ["w_in"], "abco": ["w_a_out", "w_b_out", "w_c_out", "w_out"], "mlp": ["mlp_w1", "mlp_w2"]}
SMALL = ["ada_b", "mix_pre_g", "mix_post_g", "mlp_pre_g", "mlp_post_g", "gmlp_ln_g", "gmlp_ln_b", "gmlp_ws", "gmlp_bs",
         "conv_b", "conv_ln_g", "conv_ln_b", "fox_bf"]
PACK_COLS = 512


def _to_my_layout(w_in, d):
    m = D_MIX
    nf = 7 * m
    return jnp.concatenate([w_in[..., nf + N_HEADS:], w_in[..., :nf], jnp.repeat(w_in[..., nf:nf + N_HEADS], HEAD_DIM, axis=-1)], axis=-1)


def _from_my_layout(gw, d):
    m = D_MIX
    return jnp.concatenate([gw[..., 3 * d:3 * d + 7 * m], gw[..., 3 * d + 7 * m::HEAD_DIM], gw[..., :3 * d]], axis=-1)


def _ref_ranges(lo, hi, shard):
    out = []
    while lo < hi:
        j = lo // shard
        end = min(hi, (j + 1) * shard)
        out.append((j, lo - j * shard, end - j * shard))
        lo = end
    return out


def _w_in_to_kernel_layout(g, d):
    m = D_MIX
    nf = 7 * m
    shard = g.shape[2]
    cols = lambda lo, hi: [g[j, :, a:b] for j, a, b in _ref_ranges(lo, hi, shard)]
    forget = jnp.concatenate(cols(nf, nf + N_HEADS), axis=1)
    return jnp.concatenate(cols(nf + N_HEADS, nf + N_HEADS + 3 * d) + cols(0, nf) + [jnp.repeat(forget, HEAD_DIM, axis=1)], axis=1)


def _w_in_grad_blocks(gw, d):
    m = D_MIX
    nf = 7 * m
    n_ref = nf + N_HEADS + 3 * d
    shard = n_ref // N_DEV
    segs = [(0, nf, 3 * d, 1), (nf, nf + N_HEADS, 3 * d + nf, HEAD_DIM), (nf + N_HEADS, n_ref, 0, 1)]
    blocks = []
    for j in range(N_DEV):
        lo, hi = j * shard, (j + 1) * shard
        pieces = []
        for r0, r1, k0, stride in segs:
            a, b = max(lo, r0), min(hi, r1)
            if a < b:
                pieces.append(gw[:, k0 + (a - r0) * stride:k0 + (b - r0) * stride:stride])
        blocks.append(jnp.concatenate(pieces, axis=1) if len(pieces) > 1 else pieces[0])
    return jnp.stack(blocks)


def _pack(parts):
    flat = jnp.concatenate([p.reshape(-1).astype(F32) for p in parts])
    pad = (-flat.shape[0]) % (PACK_COLS * 8)
    return jnp.pad(flat, (0, pad)).reshape(-1, PACK_COLS)


def _unpack(packed, shapes):
    nl = packed.shape[0]
    flat, out, off = packed.reshape(nl, -1), [], 0
    for shp in shapes:
        n = math.prod(shp)
        out.append(flat[:, off:off + n].reshape((nl,) + tuple(shp)))
        off += n
    return out


def _layer_small(p, conv_full, l):
    wl = {}
    for k in ["mix_pre_g", "mix_post_g", "mlp_pre_g", "mlp_post_g", "gmlp_ln_g", "gmlp_ln_b", "conv_b", "conv_ln_g", "conv_ln_b"]:
        wl[k] = p[k][l][None, :]
    wm = p["gmlp_ws"][l] * jnp.tril(jnp.ones((CHUNK, CHUNK), F32))
    wl["ws"] = wm.astype(BF16)
    wl["ws_t"] = jnp.transpose(wm, (0, 2, 1)).astype(BF16)
    wl["bs_exp"] = jnp.repeat(p["gmlp_bs"][l].T, GROUP_DIM, axis=1)
    wl["bf_exp"] = jnp.repeat(p["fox_bf"][l], HEAD_DIM)[None, :]
    wl["conv_w"] = jnp.pad(conv_full[l], ((0, CONV_HALO - CONV_WIDTH), (0, 0)))
    return wl


def kernel(x, c, ada_w, ada_b, mix_pre_g, mix_post_g, mlp_pre_g, mlp_post_g, w_in, gmlp_ln_g, gmlp_ln_b, gmlp_ws, gmlp_bs, w_a_out, conv_w, conv_b, conv_ln_g, conv_ln_b, w_b_out, fox_bf, w_c_out, w_out, mlp_w1, mlp_w2, loss_target, m_ada_w, m_ada_b, m_mix_pre_g, m_mix_post_g, m_mlp_pre_g, m_mlp_post_g, m_w_in, m_gmlp_ln_g, m_gmlp_ln_b, m_gmlp_ws, m_gmlp_bs, m_w_a_out, m_conv_w, m_conv_b, m_conv_ln_g, m_conv_ln_b, m_w_b_out, m_fox_bf, m_w_c_out, m_w_out, m_mlp_w1, m_mlp_w2, v_ada_w, v_ada_b, v_mix_pre_g, v_mix_post_g, v_mlp_pre_g, v_mlp_post_g, v_w_in, v_gmlp_ln_g, v_gmlp_ln_b, v_gmlp_ws, v_gmlp_bs, v_w_a_out, v_conv_w, v_conv_b, v_conv_ln_g, v_conv_ln_b, v_w_b_out, v_fox_bf, v_w_c_out, v_w_out, v_mlp_w1, v_mlp_w2):
    args = (x, c, ada_w, ada_b, mix_pre_g, mix_post_g, mlp_pre_g, mlp_post_g, w_in, gmlp_ln_g, gmlp_ln_b, gmlp_ws, gmlp_bs, w_a_out,
            conv_w, conv_b, conv_ln_g, conv_ln_b, w_b_out, fox_bf, w_c_out, w_out, mlp_w1, mlp_w2)
    ms = (m_ada_w, m_ada_b, m_mix_pre_g, m_mix_post_g, m_mlp_pre_g, m_mlp_post_g, m_w_in, m_gmlp_ln_g, m_gmlp_ln_b, m_gmlp_ws, m_gmlp_bs,
          m_w_a_out, m_conv_w, m_conv_b, m_conv_ln_g, m_conv_ln_b, m_w_b_out, m_fox_bf, m_w_c_out, m_w_out, m_mlp_w1, m_mlp_w2)
    vs = (v_ada_w, v_ada_b, v_mix_pre_g, v_mix_post_g, v_mlp_pre_g, v_mlp_post_g, v_w_in, v_gmlp_ln_g, v_gmlp_ln_b, v_gmlp_ws, v_gmlp_bs,
          v_w_a_out, v_conv_w, v_conv_b, v_conv_ln_g, v_conv_ln_b, v_w_b_out, v_fox_bf, v_w_c_out, v_w_out, v_mlp_w1, v_mlp_w2)
    p = dict(zip(ARG_NAMES, args))
    mom = dict(zip(WEIGHTS, ms))
    var = dict(zip(WEIGHTS, vs))
    nl = ada_w.shape[0]
    s, d = x.shape[1], x.shape[2]
    me = 4 * lax.axis_index("x") + 2 * lax.axis_index("y") + lax.axis_index("c")

    c_all, conv_all = exchange("gather_c", [c, conv_w], scatter=False)
    c_all = c_all.reshape(N_DEV, d)
    n_ada = ada_w.shape[2]
    mod_parts = ada_fwd(c_all, ada_w)
    (mod_recv,) = exchange("scatter_mod", [jnp.transpose(mod_parts, (1, 0, 2))], scatter=True)
    conv_full = jnp.transpose(conv_all, (1, 2, 0, 3)).reshape(nl, CONV_WIDTH, D_MIX)

    def full_matrix(k, land, own):
        g = lax.dynamic_update_index_in_dim(land, own, me, 0)
        r, cc = own.shape
        if k == "w_in":
            return _w_in_to_kernel_layout(g, d)
        return jnp.transpose(g, (1, 0, 2)).reshape(r, N_DEV * cc) if k in COL_SHARDED else g.reshape(N_DEV * r, cc)

    started = [mod_recv]

    def fetch(l, tag):
        keys = GROUPS[tag]
        st = exchange_start(f"gather_{tag}{l}_start", [p[k][l].astype(BF16) for k in keys], scatter=False, after=started[-1])
        started.append(st["token"])

        def get(after):
            owns, lands = exchange_wait(f"gather_{tag}{l}_wait", st, after)
            return {k: full_matrix(k, land, o) for k, land, o in zip(keys, lands, owns)}

        return get

    getters = [{tag: fetch(l, tag) for tag in GROUPS} for l in range(nl)]
    mod = jnp.transpose(mod_recv, (1, 0, 2)).reshape(nl, N_DEV * n_ada) + ada_b + started[-1][0:1, 0:1]
    mods = [mod[l].reshape(6, d) for l in range(nl)]

    sent = {}

    def emitter(l):
        def emit(tag, grads_big):
            keys = GROUPS[tag]
            send = []
            for k in keys:
                gk = grads_big[k]
                if k == "w_in":
                    gk = jnp.transpose(_w_in_grad_blocks(gk, d), (0, 2, 1))
                elif k in COL_SHARDED:
                    r, cc = gk.shape[0], gk.shape[1] // N_DEV
                    gk = jnp.transpose(gk.reshape(r, N_DEV, cc), (1, 0, 2))
                else:
                    gk = gk.reshape(N_DEV, gk.shape[0] // N_DEV, gk.shape[1])
                send.append(gk.astype(BF16))
            sent[(l, tag)] = (keys, exchange_start(f"scatter_{tag}{l}_start", send, scatter=True))
            return sent[(l, tag)][1]["token"]

        return emit

    small_sent = {}

    def small_emitter(l):
        def emit_small(dmod_l, g):
            packed = _pack([dmod_l] + [g[k] for k in SMALL[1:]] + [g["conv_w"]])
            small_sent[l] = exchange_start(f"gather_small{l}_start", [packed], scatter=False)
            return small_sent[l]["token"]

        return emit_small

    layers = [dict(small=_layer_small(p, conv_full, l), emit_small=small_emitter(l), get_in=getters[l]["in"], get_abco=getters[l]["abco"], get_mlp=getters[l]["mlp"],
                   emit=emitter(l)) for l in range(nl)]
    loss_local, dx, dmods, grads = local_step(x[0], loss_target[0], mods, layers)
    loss = lax.psum(loss_local, ("x", "y", "c"))
    grad_x = dx[None]

    out = {k: [None] * 4 for k in WEIGHTS}

    def shard_update(name, parts, k):
        shp = p[k].shape
        flat = lambda a: a.reshape(nl, -1, shp[-1])
        res = adamw_sum(name, [pt.reshape((pt.shape[0],) + flat(p[k]).shape[1:]) for pt in parts], flat(p[k]), flat(mom[k]), flat(var[k]))
        out[k] = [a.reshape(shp) for a in res]

    parts = {}
    for l in reversed(range(nl)):
        for tag in ("mlp", "abco", "in"):
            keys, st = sent[(l, tag)]
            sends, lands = exchange_wait(f"scatter_{tag}{l}_wait", st, dx)
            for k, land, sd in zip(keys, lands, sends):
                own = lax.dynamic_index_in_dim(sd, me, 0, keepdims=False)
                parts[(k, l)] = lax.dynamic_update_index_in_dim(land, own, me, 0)
    for k in BIG[1:]:
        shard_update("adamw_" + k, [parts[(k, l)] for l in range(nl)], k)
    n_in = w_in.shape[2]
    res_in = adamw_sum_cols("adamw_w_in", [parts[("w_in", l)] for l in range(nl)],
                            *[jnp.transpose(src["w_in"], (2, 0, 1)).reshape(n_in, nl * d) for src in (p, mom, var)])
    out["w_in"] = [jnp.transpose(a.reshape(n_in, nl, d), (1, 2, 0)) for a in res_in]

    small_shapes = [p[k].shape[1:] for k in SMALL] + [(CONV_WIDTH, D_MIX)]
    small_all = []
    for l in range(nl):
        srcs, lands = exchange_wait(f"gather_small{l}_wait", small_sent[l], [out[k][0] for k in BIG])
        small_all.append(lax.dynamic_update_index_in_dim(lands[0], srcs[0], me, 0))
    zeros_conv = jnp.zeros(small_shapes[-1], F32)
    packs = [jnp.stack([_pack([src[k][l] for k in SMALL] + [zeros_conv]) for l in range(nl)]) for src in (p, mom, var)]
    small_out = [_unpack(o, small_shapes) for o in adamw_sum("adamw_small", small_all, *packs)]
    for i, k in enumerate(SMALL):
        for j in range(4):
            out[k][j] = small_out[j][i]

    n_conv = conv_w.shape[2]
    conv_grad = lax.dynamic_slice_in_dim(small_out[0][-1], me * n_conv, n_conv, axis=2)
    shard_update("adamw_conv_w", [conv_grad[l][None] for l in range(nl)], "conv_w")

    dmod_all = jnp.stack([small_all[l].reshape(N_DEV, -1)[:, :6 * d] for l in range(nl)])
    dmod_mine = lax.dynamic_slice_in_dim(dmod_all, me * n_ada, n_ada, axis=2)
    g_ada = ada_bwd(c_all.T, dmod_mine)
    shard_update("adamw_ada_w", [g_ada[l][None] for l in range(nl)], "ada_w")

    res = [loss, grad_x]
    for j in range(4):
        res += [out[k][j] for k in WEIGHTS]
    return tuple(res)
```

```python
import functools
import math

import jax
import jax.numpy as jnp
from jax import lax
from jax.experimental import pallas as pl
from jax.experimental.pallas import tpu as pltpu

F32 = jnp.float32
BF16 = jnp.bfloat16
MESH = pl.DeviceIdType.MESH
N_DEV = 8
NORM_EPS = 1e-6
D_MIX = 512
N_HEADS = 8
HEAD_DIM = 64
GROUP_DIM = 64
CHUNK = 128
CONV_WIDTH = 31
CONV_HALO = 32
LANES = 128
ADAM_LR, ADAM_B1, ADAM_B2, ADAM_EPS, ADAM_WD, ADAM_STEP = 0.001, 0.9, 0.999, 1e-08, 0.01, 10
VMEM_LIMIT = 56 * 1024 * 1024
HIGHEST = lax.Precision.HIGHEST


def _tile(dim, pref, mult=LANES):
    t = min(pref, dim)
    t -= t % mult
    while t >= mult:
        if dim % t == 0:
            return t
        t -= mult
    return dim


def _params(sem):
    return pltpu.CompilerParams(dimension_semantics=sem, vmem_limit_bytes=VMEM_LIMIT)


def rowwise(name, fn, rows, consts, outs, accs=(), ts=256):
    s = rows[0][0].shape[0]
    ts = min(ts, s)
    nr, nc, no, na = len(rows), len(consts), len(outs), len(accs)

    def body(*refs):
        vals = [r[...].astype(F32) for r in refs[:nr]] + [r[...] for r in refs[nr:nr + nc]]
        res = fn(*vals)
        if not isinstance(res, (tuple, list)):
            res = (res,)
        for r, v in zip(refs[nr + nc:nr + nc + no], res[:no]):
            r[...] = v.astype(r.dtype)
        if na:
            acc_refs = refs[nr + nc + no:]

            @pl.when(pl.program_id(0) == 0)
            def _():
                for r in acc_refs:
                    r[...] = jnp.zeros(r.shape, r.dtype)

            for r, v in zip(acc_refs, res[no:]):
                r[...] += v.astype(F32)

    in_specs = [pl.BlockSpec((ts, w), functools.partial(lambda i, cb: (i, cb), cb=cb)) for (_, cb, w) in rows]
    in_specs += [pl.BlockSpec(c.shape, lambda i: (0, 0)) for c in consts]
    out_specs = [pl.BlockSpec((ts, w), lambda i: (i, 0)) for (w, _) in outs]
    out_specs += [pl.BlockSpec(shp, lambda i: (0, 0)) for shp in accs]
    out_shape = [jax.ShapeDtypeStruct((s, w), dt) for (w, dt) in outs]
    out_shape += [jax.ShapeDtypeStruct(shp, F32) for shp in accs]
    res = pl.pallas_call(
        body, name=name, grid=(s // ts,), in_specs=in_specs, out_specs=out_specs, out_shape=out_shape,
        compiler_params=_params(("arbitrary",) if na else ("parallel",)),
    )(*[a for (a, _, _) in rows], *consts)
    return res


def rowwise_vjp(name, f, rows, consts, cts, grad_dtypes, ts=256):
    nr, nc, nt = len(rows), len(consts), len(cts)
    keep = [i for i, dt in enumerate(grad_dtypes) if dt is not None]

    def g(*vals):
        rv = [v.astype(F32) for v in vals[:nr]]
        ctv = tuple(v.astype(F32) for v in vals[nr:nr + nt])
        cv = list(vals[nr + nt:])
        _, vjp = jax.vjp(lambda *a: tuple(f(*a)), *rv, *cv)
        grads = vjp(ctv)
        return tuple(grads[i] for i in keep) + tuple(grads[nr:])

    outs = [(rows[i][2], grad_dtypes[i]) for i in keep]
    return rowwise(name, g, list(rows) + list(cts), consts, outs, accs=[c.shape for c in consts], ts=ts)


def matmul(name, a, b, *, ta=False, tb=False, out_dtypes=(F32,), epilogue=None, epi=(), tm=1024, tn=1024, tk=1024, dep=None, b_cols=None):
    m, k = (a.shape[1], a.shape[0]) if ta else a.shape
    n = b.shape[0] if tb else b.shape[1]
    b_col0 = 0
    if b_cols is not None:
        assert not tb
        b_col0, n = b_cols
    assert (b.shape[1] if tb else b.shape[0]) == k
    tm, tn, tk = _tile(m, tm), _tile(n, tn), _tile(k, tk)
    assert b_col0 % tn == 0
    jb = b_col0 // tn
    nk = k // tk
    ne, no = len(epi), len(out_dtypes)
    dims = (((0 if ta else 1,), (1 if tb else 0,)), ((), ()))

    def body(*refs):
        a_ref, b_ref = refs[0], refs[1]
        epi_refs = refs[2:2 + ne]
        n_in = 2 + ne + (dep is not None)
        out_refs = refs[n_in:n_in + no]
        part = lax.dot_general(a_ref[...].astype(BF16), b_ref[...].astype(BF16), dims, preferred_element_type=F32)

        def finish(acc):
            res = (acc,) if epilogue is None else epilogue(acc, *[r[...] for r in epi_refs])
            for r, v in zip(out_refs, res):
                r[...] = v.astype(r.dtype)

        if nk == 1:
            finish(part)
        else:
            acc_ref = refs[-1]
            kk = pl.program_id(2)

            @pl.when(kk == 0)
            def _():
                acc_ref[...] = part

            @pl.when(kk > 0)
            def _():
                acc_ref[...] += part

            @pl.when(kk == nk - 1)
            def _():
                finish(acc_ref[...])

    a_spec = pl.BlockSpec((tk, tm), lambda i, j, kk: (kk, i)) if ta else pl.BlockSpec((tm, tk), lambda i, j, kk: (i, kk))
    b_spec = pl.BlockSpec((tn, tk), lambda i, j, kk: (j, kk)) if tb else pl.BlockSpec((tk, tn), lambda i, j, kk: (kk, j + jb))
    epi_specs = []
    for (arr, col0) in epi:
        assert col0 % tn == 0
        epi_specs.append(pl.BlockSpec((tm, tn), functools.partial(lambda i, j, kk, c0: (i, j + c0), c0=col0 // tn)))
    res = pl.pallas_call(
        body, name=name, grid=(m // tm, n // tn, nk),
        in_specs=[a_spec, b_spec] + epi_specs + ([] if dep is None else [pl.BlockSpec(dep.shape, lambda i, j, kk: (0, 0))]),
        out_specs=[pl.BlockSpec((tm, tn), lambda i, j, kk: (i, j)) for _ in out_dtypes],
        out_shape=[jax.ShapeDtypeStruct((m, n), dt) for dt in out_dtypes],
        scratch_shapes=[pltpu.VMEM((tm, tn), F32)] if nk > 1 else [],
        compiler_params=_params(("parallel", "parallel", "arbitrary")),
    )(a, b, *[arr for (arr, _) in epi], *([] if dep is None else [dep]))
    return res[0] if no == 1 else res


def _rms(x, g):
    return x * lax.rsqrt(jnp.mean(x * x, axis=-1, keepdims=True) + NORM_EPS) * g


def _ln(x, g, b):
    mu = jnp.mean(x, axis=-1, keepdims=True)
    xc = x - mu
    var = jnp.mean(xc * xc, axis=-1, keepdims=True)
    return xc * lax.rsqrt(var + NORM_EPS) * g + b


def _gelu(x):
    return 0.5 * x * (1.0 + jnp.tanh(math.sqrt(2.0 / math.pi) * (x + 0.044715 * (x * x * x))))


def _sigmoid(x):
    return 1.0 / (1.0 + jnp.exp(-x))


def _silu(x):
    return x * _sigmoid(x)


def _log_sigmoid(x):
    return jnp.minimum(x, 0.0) - jnp.log(1.0 + jnp.exp(-jnp.abs(x)))


def _f_pre(x, g, sc, sh):
    return (_rms(x, g) * (1.0 + sc) + sh,)


def _f_post(y, g, gt):
    return (gt * _rms(y, g),)


def _f_a1(u_raw, v_raw, g, b):
    return _gelu(u_raw), _ln(_gelu(v_raw), g, b)


def _f_glu(val, gate):
    return (val * _sigmoid(gate),)


def _f_lnsilu(zc, g, b):
    return (_silu(_ln(zc, g, b)),)


def _f_merge(g0, g1, g2, ya, yb, yc):
    return (_sigmoid(g0) * ya + _sigmoid(g1) * yb + _sigmoid(g2) * yc,)


def _lane_lt64(shape):
    return lax.broadcasted_iota(jnp.int32, shape, 1) < HEAD_DIM


def spatial_fwd(vln, u, w_bf, b_exp, rows_per_step=512):
    s = vln.shape[0]
    tr = min(rows_per_step, s)

    def body(v_ref, u_ref, w_ref, b_ref, sv_ref, ya_ref):
        lo = _lane_lt64((CHUNK, LANES))
        for ch in range(tr // CHUNK):
            r0 = ch * CHUNK
            for p in range(D_MIX // LANES):
                vp = v_ref[r0:r0 + CHUNK, p * LANES:(p + 1) * LANES]
                o0 = jnp.dot(w_ref[2 * p], vp, preferred_element_type=F32)
                o1 = jnp.dot(w_ref[2 * p + 1], vp, preferred_element_type=F32)
                sv = jnp.where(lo, o0, o1) + b_ref[:, p * LANES:(p + 1) * LANES]
                sv_ref[r0:r0 + CHUNK, p * LANES:(p + 1) * LANES] = sv
                ya_ref[r0:r0 + CHUNK, p * LANES:(p + 1) * LANES] = (
                    u_ref[r0:r0 + CHUNK, p * LANES:(p + 1) * LANES] * sv).astype(BF16)

    row = pl.BlockSpec((tr, D_MIX), lambda i: (i, 0))
    return pl.pallas_call(
        body, name="spatial_fwd", grid=(s // tr,),
        in_specs=[row, row, pl.BlockSpec(w_bf.shape, lambda i: (0, 0, 0)), pl.BlockSpec(b_exp.shape, lambda i: (0, 0))],
        out_specs=[row, row],
        out_shape=[jax.ShapeDtypeStruct((s, D_MIX), F32), jax.ShapeDtypeStruct((s, D_MIX), BF16)],
        compiler_params=_params(("parallel",)),
    )(vln, u, w_bf, b_exp)


def spatial_bwd(dya, u, sv, vln, wt_bf, rows_per_step=512):
    s = vln.shape[0]
    tr = min(rows_per_step, s)
    ng = wt_bf.shape[0]

    def body(dya_ref, u_ref, sv_ref, v_ref, wt_ref, du_ref, dv_ref, dw_ref, db_ref):
        @pl.when(pl.program_id(0) == 0)
        def _():
            dw_ref[...] = jnp.zeros(dw_ref.shape, F32)
            db_ref[...] = jnp.zeros(db_ref.shape, F32)

        lo = _lane_lt64((CHUNK, LANES))
        for ch in range(tr // CHUNK):
            r0 = ch * CHUNK
            for p in range(D_MIX // LANES):
                cs = slice(p * LANES, (p + 1) * LANES)
                dya_p = dya_ref[r0:r0 + CHUNK, cs].astype(F32)
                du_ref[r0:r0 + CHUNK, cs] = dya_p * sv_ref[r0:r0 + CHUNK, cs]
                dsv = dya_p * u_ref[r0:r0 + CHUNK, cs]
                db_ref[:, cs] += dsv
                dsv0 = jnp.where(lo, dsv, 0.0).astype(BF16)
                dsv1 = jnp.where(lo, 0.0, dsv).astype(BF16)
                vp = v_ref[r0:r0 + CHUNK, cs]
                d0 = jnp.dot(wt_ref[2 * p], dsv0, preferred_element_type=F32)
                d1 = jnp.dot(wt_ref[2 * p + 1], dsv1, preferred_element_type=F32)
                dv_ref[r0:r0 + CHUNK, cs] = d0 + d1
                nt = (((1,), (1,)), ((), ()))
                dw_ref[2 * p] += lax.dot_general(dsv0, vp, nt, preferred_element_type=F32)
                dw_ref[2 * p + 1] += lax.dot_general(dsv1, vp, nt, preferred_element_type=F32)

    row = pl.BlockSpec((tr, D_MIX), lambda i: (i, 0))
    return pl.pallas_call(
        body, name="spatial_bwd", grid=(s // tr,),
        in_specs=[row, row, row, row, pl.BlockSpec(wt_bf.shape, lambda i: (0, 0, 0))],
        out_specs=[row, row, pl.BlockSpec((ng, CHUNK, CHUNK), lambda i: (0, 0, 0)), pl.BlockSpec((CHUNK, D_MIX), lambda i: (0, 0))],
        out_shape=[jax.ShapeDtypeStruct((s, D_MIX), F32), jax.ShapeDtypeStruct((s, D_MIX), F32),
                   jax.ShapeDtypeStruct((ng, CHUNK, CHUNK), F32), jax.ShapeDtypeStruct((CHUNK, D_MIX), F32)],
        compiler_params=_params(("arbitrary",)),
    )(dya, u, sv, vln, wt_bf)


def _windows(ref, first, count, ts):
    for r in range(8):
        ks = [k for k in range(count) if (first + k) % 8 == r]
        if ks:
            base = first + ks[0]
            blk = ref[base:base + ks[-1] - ks[0] + ts, :]
            for k in ks:
                yield k, blk[k - ks[0]:k - ks[0] + ts, :]


def conv_fwd(proj, cb_val, cb_gate, w_pad, cb, ln_g, ln_b, ts=256):
    s = proj.shape[0]
    ts = min(ts, s)
    per = ts // CONV_HALO

    def body(val_ref, gate_ref, pval_ref, pgate_ref, w_ref, cb_ref, g_ref, b_ref, zc_ref, yb_ref, ext_ref):
        i = pl.program_id(0)
        zprev = pval_ref[...].astype(F32) * _sigmoid(pgate_ref[...].astype(F32))
        ext_ref[0:CONV_HALO, :] = jnp.where(i > 0, zprev, 0.0)
        ext_ref[CONV_HALO:, :] = val_ref[...].astype(F32) * _sigmoid(gate_ref[...].astype(F32))
        acc = jnp.zeros((ts, D_MIX), F32)
        for j, win in _windows(ext_ref, CONV_HALO - (CONV_WIDTH - 1), CONV_WIDTH, ts):
            acc = acc + w_ref[j:j + 1, :] * win
        zc = acc + cb_ref[...]
        zc_ref[...] = zc
        yb_ref[...] = _f_lnsilu(zc, g_ref[...], b_ref[...])[0].astype(BF16)

    def cur(c):
        return pl.BlockSpec((ts, D_MIX), functools.partial(lambda i, c: (i, c), c=c))

    def prev(c):
        return pl.BlockSpec((CONV_HALO, D_MIX), functools.partial(lambda i, c: (jnp.maximum(i * per - 1, 0), c), c=c))

    const = lambda a: pl.BlockSpec(a.shape, lambda i: (0, 0))
    out = pl.BlockSpec((ts, D_MIX), lambda i: (i, 0))
    return pl.pallas_call(
        body, name="conv_fwd", grid=(s // ts,),
        in_specs=[cur(cb_val), cur(cb_gate), prev(cb_val), prev(cb_gate), const(w_pad), const(cb), const(ln_g), const(ln_b)],
        out_specs=[out, out],
        out_shape=[jax.ShapeDtypeStruct((s, D_MIX), F32), jax.ShapeDtypeStruct((s, D_MIX), BF16)],
        scratch_shapes=[pltpu.VMEM((CONV_HALO + ts, D_MIX), F32)],
        compiler_params=_params(("parallel",)),
    )(proj, proj, proj, proj, w_pad, cb, ln_g, ln_b)


def conv_bwd(proj, cb_val, cb_gate, zc, dyb, w_pad, ln_g, ln_b, ts=256):
    s = proj.shape[0]
    ts = min(ts, s)
    per = ts // CONV_HALO
    n_tiles = s // ts
    n_halo = s // CONV_HALO

    def body(val_ref, gate_ref, pval_ref, pgate_ref, zc_ref, dyb_ref, nzc_ref, ndyb_ref, w_ref, g_ref, b_ref,
             dval_ref, dgate_ref, dw_ref, dcb_ref, dg_ref, db_ref, zext_ref, dext_ref):
        i = pl.program_id(0)

        @pl.when(i == 0)
        def _():
            for r in (dw_ref, dcb_ref, dg_ref, db_ref):
                r[...] = jnp.zeros(r.shape, F32)

        g, b = g_ref[...], b_ref[...]
        _, vjp = jax.vjp(lambda z, gg, bb: _f_lnsilu(z, gg, bb)[0], zc_ref[...], g, b)
        dzc, dg, db = vjp(dyb_ref[...].astype(F32))
        dg_ref[...] += dg
        db_ref[...] += db
        dcb_ref[...] += jnp.sum(dzc, axis=0, keepdims=True)
        _, vjp_n = jax.vjp(lambda z: _f_lnsilu(z, g, b)[0], nzc_ref[...])
        (dzc_next,) = vjp_n(ndyb_ref[...].astype(F32))
        dext_ref[0:ts, :] = dzc
        dext_ref[ts:, :] = jnp.where(i < n_tiles - 1, dzc_next, 0.0)
        val, gate = val_ref[...].astype(F32), gate_ref[...].astype(F32)
        zprev = pval_ref[...].astype(F32) * _sigmoid(pgate_ref[...].astype(F32))
        zext_ref[0:CONV_HALO, :] = jnp.where(i > 0, zprev, 0.0)
        zext_ref[CONV_HALO:, :] = val * _sigmoid(gate)
        dz = jnp.zeros((ts, D_MIX), F32)
        for shift, win in _windows(dext_ref, 0, CONV_WIDTH, ts):
            j = CONV_WIDTH - 1 - shift
            dz = dz + w_ref[j:j + 1, :] * win
        for j, win in _windows(zext_ref, CONV_HALO - (CONV_WIDTH - 1), CONV_WIDTH, ts):
            dw_ref[j:j + 1, :] += jnp.sum(dzc * win, axis=0, keepdims=True)
        _, vjp_glu = jax.vjp(lambda a, c: _f_glu(a, c)[0], val, gate)
        dval, dgate = vjp_glu(dz)
        dval_ref[...] = dval.astype(BF16)
        dgate_ref[...] = dgate.astype(BF16)

    def cur(c):
        return pl.BlockSpec((ts, D_MIX), functools.partial(lambda i, c: (i, c), c=c))

    def prev(c):
        return pl.BlockSpec((CONV_HALO, D_MIX), functools.partial(lambda i, c: (jnp.maximum(i * per - 1, 0), c), c=c))

    nxt = pl.BlockSpec((CONV_HALO, D_MIX), lambda i: (jnp.minimum((i + 1) * per, n_halo - 1), 0))
    const = lambda a: pl.BlockSpec(a.shape, lambda i: (0, 0))
    out = pl.BlockSpec((ts, D_MIX), lambda i: (i, 0))
    vec = pl.BlockSpec((1, D_MIX), lambda i: (0, 0))
    return pl.pallas_call(
        body, name="conv_bwd", grid=(n_tiles,),
        in_specs=[cur(cb_val), cur(cb_gate), prev(cb_val), prev(cb_gate), out, out, nxt, nxt, const(w_pad), const(ln_g), const(ln_b)],
        out_specs=[out, out, pl.BlockSpec((CONV_HALO, D_MIX), lambda i: (0, 0)), vec, vec, vec],
        out_shape=[jax.ShapeDtypeStruct((s, D_MIX), BF16), jax.ShapeDtypeStruct((s, D_MIX), BF16),
                   jax.ShapeDtypeStruct((CONV_HALO, D_MIX), F32)] + [jax.ShapeDtypeStruct((1, D_MIX), F32)] * 3,
        scratch_shapes=[pltpu.VMEM((CONV_HALO + ts, D_MIX), F32), pltpu.VMEM((ts + CONV_HALO, D_MIX), F32)],
        compiler_params=_params(("arbitrary",)),
    )(proj, proj, proj, proj, zc, dyb, zc, dyb, w_pad, ln_g, ln_b)


def forget_cumsum(proj, cb_f, bf_exp, t=256):
    s = proj.shape[0]
    t = min(t, s)

    def body(f_ref, bf_ref, out_ref, carry_ref):
        @pl.when(pl.program_id(0) == 0)
        def _():
            carry_ref[...] = jnp.zeros(carry_ref.shape, F32)

        lf = _log_sigmoid(f_ref[...] + bf_ref[...])
        tri = (lax.broadcasted_iota(jnp.int32, (t, t), 1) <= lax.broadcasted_iota(jnp.int32, (t, t), 0)).astype(F32)
        c = jnp.dot(tri, lf, precision=HIGHEST, preferred_element_type=F32) + carry_ref[...]
        out_ref[...] = c
        carry_ref[...] = c[t - 1:t, :]

    return pl.pallas_call(
        body, name="forget_cumsum", grid=(s // t,),
        in_specs=[pl.BlockSpec((t, D_MIX), functools.partial(lambda i, c: (i, c), c=cb_f)), pl.BlockSpec((1, D_MIX), lambda i: (0, 0))],
        out_specs=pl.BlockSpec((t, D_MIX), lambda i: (i, 0)),
        out_shape=jax.ShapeDtypeStruct((s, D_MIX), F32),
        scratch_shapes=[pltpu.VMEM((1, D_MIX), F32)],
        compiler_params=_params(("arbitrary",)),
    )(proj, bf_exp)


def forget_bwd(proj, cb_f, bf_exp, dcum, t=256):
    s = proj.shape[0]
    t = min(t, s)
    n = s // t

    def body(f_ref, bf_ref, dc_ref, df_ref, dbf_ref, carry_ref):
        @pl.when(pl.program_id(0) == 0)
        def _():
            carry_ref[...] = jnp.zeros(carry_ref.shape, F32)
            dbf_ref[...] = jnp.zeros(dbf_ref.shape, F32)

        tri = (lax.broadcasted_iota(jnp.int32, (t, t), 1) >= lax.broadcasted_iota(jnp.int32, (t, t), 0)).astype(F32)
        r = jnp.dot(tri, dc_ref[...], precision=HIGHEST, preferred_element_type=F32) + carry_ref[...]
        carry_ref[...] = r[0:1, :]
        df = r * _sigmoid(-(f_ref[...] + bf_ref[...]))
        dbf_ref[...] += jnp.sum(df, axis=0, keepdims=True)
        live = lax.broadcasted_iota(jnp.int32, (t, D_MIX), 1) % HEAD_DIM == 0
        df_ref[...] = jnp.where(live, df, 0.0).astype(BF16)

    return pl.pallas_call(
        body, name="forget_bwd", grid=(n,),
        in_specs=[pl.BlockSpec((t, D_MIX), functools.partial(lambda i, c: (n - 1 - i, c), c=cb_f)), pl.BlockSpec((1, D_MIX), lambda i: (0, 0)),
                  pl.BlockSpec((t, D_MIX), lambda i: (n - 1 - i, 0))],
        out_specs=[pl.BlockSpec((t, D_MIX), lambda i: (n - 1 - i, 0)), pl.BlockSpec((1, D_MIX), lambda i: (0, 0))],
        out_shape=[jax.ShapeDtypeStruct((s, D_MIX), BF16), jax.ShapeDtypeStruct((1, D_MIX), F32)],
        scratch_shapes=[pltpu.VMEM((1, D_MIX), F32)],
        compiler_params=_params(("arbitrary",)),
    )(proj, bf_exp, dcum)


NT = (((1,), (1,)), ((), ()))
LOG2E = math.log2(math.e)
N_PAIR = D_MIX // LANES


def _split3(x):
    hi = x.astype(BF16).astype(F32)
    mid = (x - hi).astype(BF16).astype(F32)
    return hi, mid, x - hi - mid


def _triple(li, first, vals):
    out = jnp.where(li == first, vals[0], 0.0)
    for i in (1, 2):
        out = jnp.where(li == first + i, vals[i], out)
    return out


def _lane_ids(shape):
    lane = lax.broadcasted_iota(jnp.int32, shape, 1)
    return lane, lane % HEAD_DIM, lane < HEAD_DIM


def attn_prep(proj, cb_q, cum, ts=256):
    s = proj.shape[0]
    ts = min(ts, s)
    scale = LOG2E / math.sqrt(HEAD_DIM)

    def body(q_ref, k_ref, v_ref, c_ref, qe_ref, qo_ref, ke_ref, ko_ref, ve_ref, vo_ref):
        _, li, lo = _lane_ids((ts, LANES))
        one3 = lambda first: ((li >= first) & (li < first + 3)).astype(F32)
        for p in range(N_PAIR):
            ps = slice(p * LANES, (p + 1) * LANES)
            c3 = _split3(pltpu.roll(c_ref[:, ps] * LOG2E, HEAD_DIM, axis=1))
            eq = _triple(li, 0, c3) + one3(3)
            ek = one3(0) - _triple(li, 3, c3) + one3(6)
            ev = one3(0)
            for src, even, odd, extra, mul in ((q_ref, qe_ref, qo_ref, eq, scale), (k_ref, ke_ref, ko_ref, ek, 1.0), (v_ref, ve_ref, vo_ref, ev, 1.0)):
                x = src[:, ps].astype(F32) * mul
                even[:, ps] = jnp.where(lo, x, extra).astype(BF16)
                odd[:, ps] = jnp.where(lo, extra, x).astype(BF16)

    col = lambda c: pl.BlockSpec((ts, D_MIX), functools.partial(lambda i, c: (i, c), c=c))
    out = pl.BlockSpec((ts, D_MIX), lambda i: (i, 0))
    return pl.pallas_call(
        body, name="attn_prep", grid=(s // ts,),
        in_specs=[col(cb_q), col(cb_q + 1), col(cb_q + 2), pl.BlockSpec((ts, D_MIX), lambda i: (i, 0))],
        out_specs=[out] * 6, out_shape=[jax.ShapeDtypeStruct((s, D_MIX), BF16)] * 6,
        compiler_params=_params(("parallel",)),
    )(proj, proj, proj, cum)


def _pair_specs(s, t):
    return pl.BlockSpec((t, LANES), lambda p, i: (i, p)), pl.BlockSpec((s, LANES), lambda p, i: (0, p))


def attn_fwd(qe, qo, ke, ko, ve, vo, tq=512):
    s = qe.shape[0]
    tq = min(tq, s)

    def body(qe_ref, qo_ref, ke_ref, ko_ref, ve_ref, vo_ref, o_ref, qbe_ref, qbo_ref):
        qi = pl.program_id(1)
        qs, k_refs, v_refs = (qe_ref[...], qo_ref[...]), (ke_ref, ko_ref), (ve_ref, vo_ref)
        causal = lax.broadcasted_iota(jnp.int32, (tq, tq), 1) <= lax.broadcasted_iota(jnp.int32, (tq, tq), 0)

        def step(j, carry, diag):
            ks = pl.multiple_of(j * tq, tq)
            new = []
            for h in range(2):
                m, l, acc = carry[h]
                sc = lax.dot_general(qs[h], k_refs[h][pl.ds(ks, tq), :], NT, preferred_element_type=F32)
                if diag:
                    sc = jnp.where(causal, sc, -jnp.inf)
                m_new = jnp.maximum(m, jnp.max(sc, axis=1, keepdims=True))
                p = jnp.exp2(sc - m_new)
                alpha = jnp.exp2(m - m_new)
                l = alpha * l + jnp.sum(p, axis=1, keepdims=True)
                acc = alpha * acc + jnp.dot(p.astype(BF16), v_refs[h][pl.ds(ks, tq), :], preferred_element_type=F32)
                new.append((m_new, l, acc))
            return tuple(new)

        init = tuple((jnp.full((tq, 1), -jnp.inf, F32), jnp.zeros((tq, 1), F32), jnp.zeros((tq, LANES), F32)) for _ in range(2))
        carry = lax.fori_loop(0, qi, lambda j, c: step(j, c, False), init)
        (m0, l0, a0), (m1, l1, a1) = step(qi, carry, True)
        _, li, lo = _lane_ids((tq, LANES))
        o_ref[...] = jnp.where(lo, a0 / l0, a1 / l1).astype(o_ref.dtype)
        lse_lanes = (li >= 6) & (li < 9)
        for q, m, l, spare, out_ref in ((qs[0], m0, l0, ~lo, qbe_ref), (qs[1], m1, l1, lo, qbo_ref)):
            neg_lse = _triple(li, 6, _split3(-(m + jnp.log(l) * LOG2E)))
            out_ref[...] = jnp.where(spare & lse_lanes, neg_lse.astype(BF16), q)

    blk, full = _pair_specs(s, tq)
    return pl.pallas_call(
        body, name="attn_fwd", grid=(N_PAIR, s // tq),
        in_specs=[blk, blk, full, full, full, full],
        out_specs=[blk] * 3, out_shape=[jax.ShapeDtypeStruct((s, D_MIX), BF16)] * 3,
        compiler_params=_params(("parallel", "parallel")),
    )(qe, qo, ke, ko, ve, vo)


def attn_dq(qbe, qbo, ke, ko, ve, vo, do, dep, tq=512):
    s = qbe.shape[0]
    tq = min(tq, s)
    scale = 1.0 / math.sqrt(HEAD_DIM)

    def body(qe_ref, qo_ref, ke_ref, ko_ref, ve_ref, vo_ref, do_ref, dep_ref, dq_ref, dobe_ref, dobo_ref):
        qi = pl.program_id(1)
        _, li, lo = _lane_ids((tq, LANES))
        do_ = do_ref[...]
        qs, k_refs, v_refs = (qe_ref[...], qo_ref[...]), (ke_ref, ko_ref), (ve_ref, vo_ref)
        dos = (jnp.where(lo, do_, 0), jnp.where(lo, 0, do_))
        causal = lax.broadcasted_iota(jnp.int32, (tq, tq), 1) <= lax.broadcasted_iota(jnp.int32, (tq, tq), 0)

        def step(j, carry, diag):
            ks = pl.multiple_of(j * tq, tq)
            new = []
            for h in range(2):
                pdpk, pk, dsum = carry[h]
                kb = k_refs[h][pl.ds(ks, tq), :]
                sc = lax.dot_general(qs[h], kb, NT, preferred_element_type=F32)
                if diag:
                    sc = jnp.where(causal, sc, -jnp.inf)
                p = jnp.exp2(sc)
                pdp = p * lax.dot_general(dos[h], v_refs[h][pl.ds(ks, tq), :], NT, preferred_element_type=F32)
                new.append((pdpk + jnp.dot(pdp.astype(BF16), kb, preferred_element_type=F32),
                            pk + jnp.dot(p.astype(BF16), kb, preferred_element_type=F32),
                            dsum + jnp.sum(pdp, axis=1, keepdims=True)))
            return tuple(new)

        init = tuple((jnp.zeros((tq, LANES), F32), jnp.zeros((tq, LANES), F32), jnp.zeros((tq, 1), F32)) for _ in range(2))
        carry = lax.fori_loop(0, qi, lambda j, c: step(j, c, False), init)
        (a0, b0, s0), (a1, b1, s1) = step(qi, carry, True)
        dq_ref[...] = (jnp.where(lo, a0 - s0 * b0, a1 - s1 * b1) * scale).astype(dq_ref.dtype)
        dobe_ref[...] = jnp.where(lo, do_, _triple(li, 0, _split3(-s0)).astype(BF16))
        dobo_ref[...] = jnp.where(lo, _triple(li, 0, _split3(-s1)).astype(BF16), do_)

    blk, full = _pair_specs(s, tq)
    return pl.pallas_call(
        body, name="attn_dq", grid=(N_PAIR, s // tq),
        in_specs=[blk, blk, full, full, full, full, blk, pl.BlockSpec(dep.shape, lambda p, i: (0, 0))],
        out_specs=[blk] * 3, out_shape=[jax.ShapeDtypeStruct((s, D_MIX), BF16)] * 3,
        compiler_params=_params(("parallel", "parallel")),
    )(qbe, qbo, ke, ko, ve, vo, do, dep)


def attn_dkv(ke, ko, ve, vo, qbe, qbo, dobe, dobo, tk=512):
    s = ke.shape[0]
    tk = min(tk, s)
    nq = s // tk

    def body(ke_ref, ko_ref, ve_ref, vo_ref, qe_ref, qo_ref, de_ref, do_ref, dk_ref, dv_ref, dck_ref):
        kj = pl.program_id(1)
        lo = _lane_lt64((tk, LANES))
        ks_, vs_, q_refs, d_refs = (ke_ref[...], ko_ref[...]), (ve_ref[...], vo_ref[...]), (qe_ref, qo_ref), (de_ref, do_ref)
        causal = lax.broadcasted_iota(jnp.int32, (tk, tk), 0) <= lax.broadcasted_iota(jnp.int32, (tk, tk), 1)

        def step(i, carry, diag):
            qs = pl.multiple_of(i * tk, tk)
            new = []
            for h in range(2):
                dk, dv, dck = carry[h]
                qblk = q_refs[h][pl.ds(qs, tk), :]
                dblk = d_refs[h][pl.ds(qs, tk), :]
                st = lax.dot_general(ks_[h], qblk, NT, preferred_element_type=F32)
                if diag:
                    st = jnp.where(causal, st, -jnp.inf)
                pt = jnp.exp2(st)
                dst = pt * lax.dot_general(vs_[h], dblk, NT, preferred_element_type=F32)
                new.append((dk + jnp.dot(dst.astype(BF16), qblk, preferred_element_type=F32),
                            dv + jnp.dot(pt.astype(BF16), dblk, preferred_element_type=F32),
                            dck - jnp.sum(dst, axis=1, keepdims=True)))
            return tuple(new)

        init = tuple((jnp.zeros((tk, LANES), F32), jnp.zeros((tk, LANES), F32), jnp.zeros((tk, 1), F32)) for _ in range(2))
        carry = step(kj, init, True)
        (dk0, dv0, dc0), (dk1, dv1, dc1) = lax.fori_loop(kj + 1, nq, lambda i, c: step(i, c, False), carry)
        dk_ref[...] = (jnp.where(lo, dk0, dk1) * (1.0 / LOG2E)).astype(dk_ref.dtype)
        dv_ref[...] = jnp.where(lo, dv0, dv1).astype(dv_ref.dtype)
        dck_ref[...] = jnp.where(lo, dc0, dc1)

    blk, full = _pair_specs(s, tk)
    return pl.pallas_call(
        body, name="attn_dkv", grid=(N_PAIR, nq),
        in_specs=[blk, blk, blk, blk, full, full, full, full],
        out_specs=[blk] * 3,
        out_shape=[jax.ShapeDtypeStruct((s, D_MIX), BF16), jax.ShapeDtypeStruct((s, D_MIX), BF16), jax.ShapeDtypeStruct((s, D_MIX), F32)],
        compiler_params=_params(("parallel", "parallel")),
    )(ke, ko, ve, vo, qbe, qbo, dobe, dobo)


def _pre_bwd(name, x, g, sc, sh, dh, dres):
    d = x.shape[1]

    def fn(xv, dhv, dresv, gv, scv, shv):
        _, vjp = jax.vjp(lambda *a: _f_pre(*a)[0], xv, gv, scv, shv)
        dx, dg, dsc, dsh = vjp(dhv.astype(F32))
        return dx + dresv, dg, dsc, dsh

    return rowwise(name, fn, [(x, 0, d), (dh, 0, d), (dres, 0, d)], [g, sc, sh], [(d, F32)], accs=[(1, d)] * 3)


def layer_fwd(x, mod, layer):
    s, d = x.shape
    m = D_MIX
    w = dict(layer["small"])
    sh1, sc1, gt1, sh2, sc2, gt2 = (mod[i:i + 1] for i in range(6))
    cb = 3 * d // m
    (h,) = rowwise("pre1", _f_pre, [(x, 0, d)], [w["mix_pre_g"], sc1, sh1], [(d, BF16)])
    w.update(layer["get_in"](h))
    proj = matmul("w_in", h, w["w_in"], out_dtypes=(BF16,))
    fproj = matmul("w_in_forget", h, w["w_in"], b_cols=((cb + 7) * m, m))
    w.update(layer["get_abco"](proj))
    u, vln = rowwise("gmlp_in", _f_a1, [(proj, cb, m), (proj, cb + 1, m)], [w["gmlp_ln_g"], w["gmlp_ln_b"]], [(m, F32), (m, BF16)])
    sv, ya = spatial_fwd(vln, u, w["ws"], w["bs_exp"])
    y_a = matmul("w_a", ya, w["w_a_out"], out_dtypes=(BF16,))
    zc, yb = conv_fwd(proj, cb + 2, cb + 3, w["conv_w"], w["conv_b"], w["conv_ln_g"], w["conv_ln_b"])
    y_b = matmul("w_b", yb, w["w_b_out"], out_dtypes=(BF16,))
    cum = forget_cumsum(fproj, 0, w["bf_exp"])
    kv_ops = attn_prep(proj, cb + 4, cum)
    o, qbe, qbo = attn_fwd(*kv_ops)
    att = (qbe, qbo) + tuple(kv_ops[2:])
    y_c = matmul("w_c", o, w["w_c_out"], out_dtypes=(BF16,))
    (merged,) = rowwise("merge", _f_merge, [(proj, 0, d), (proj, 1, d), (proj, 2, d), (y_a, 0, d), (y_b, 0, d), (y_c, 0, d)], [], [(d, BF16)])
    y = matmul("w_out", merged, w["w_out"])
    w.update(layer["get_mlp"](y))

    def post_pre(xv, yv, gp, gt, g2, sc, sh):
        x1 = xv + _f_post(yv, gp, gt)[0]
        return x1, _f_pre(x1, g2, sc, sh)[0]

    x1, h2 = rowwise("post1", post_pre, [(x, 0, d), (y, 0, d)], [w["mix_post_g"], gt1, w["mlp_pre_g"], sc2, sh2], [(d, F32), (d, BF16)])
    r = matmul("w1", h2, w["mlp_w1"], out_dtypes=(BF16,), epilogue=lambda acc: (jnp.square(jnp.maximum(acc, 0.0)),))
    y2 = matmul("w2", r, w["mlp_w2"])
    (x2,) = rowwise("post2", lambda xv, yv, g, gt: xv + _f_post(yv, g, gt)[0], [(x1, 0, d), (y2, 0, d)], [w["mlp_post_g"], gt2], [(d, F32)])
    saved = dict(w=w, x=x, h=h, proj=proj, fproj=fproj, u=u, vln=vln, sv=sv, ya=ya, y_a=y_a, zc=zc, yb=yb, y_b=y_b, att=att,
                 o=o, y_c=y_c, merged=merged, y=y, x1=x1, h2=h2, r=r, y2=y2)
    return x2, saved


def layer_bwd(dx2, mod, sv, emit, tok_in=None):
    x, proj, w = sv["x"], sv["proj"], sv["w"]
    s, d = x.shape
    m = D_MIX
    sh1, sc1, gt1, sh2, sc2, gt2 = (mod[i:i + 1] for i in range(6))
    cb = 3 * d // m
    g = {}
    if tok_in is not None:
        gt2 = gt2 + tok_in[0:1, 0:1]
    dy2, g["mlp_post_g"], dgt2 = rowwise_vjp("post2_b", _f_post, [(sv["y2"], 0, d)], [w["mlp_post_g"], gt2], [(dx2, 0, d)], [BF16])
    da = matmul("w2_dx", dy2, w["mlp_w2"], tb=True, out_dtypes=(BF16,),
                epilogue=lambda acc, r: (acc * (2.0 * jnp.sqrt(r.astype(F32))),), epi=[(sv["r"], 0)])
    big = {}
    big["mlp_w2"] = matmul("w2_dw", sv["r"], dy2, ta=True, out_dtypes=(BF16,))
    dh2 = matmul("w1_dx", da, w["mlp_w1"], tb=True, out_dtypes=(BF16,))
    big["mlp_w1"] = matmul("w1_dw", sv["h2"], da, ta=True, out_dtypes=(BF16,))
    tok = emit("mlp", big)
    dx1, g["mlp_pre_g"], dsc2, dsh2 = _pre_bwd("pre2_b", sv["x1"], w["mlp_pre_g"], sc2 + tok[0:1, 0:1], sh2, dh2, dx2)
    dy, g["mix_post_g"], dgt1 = rowwise_vjp("post1_b", _f_post, [(sv["y"], 0, d)], [w["mix_post_g"], gt1], [(dx1, 0, d)], [BF16])
    dmerged = matmul("w_out_dx", dy, w["w_out"], tb=True, out_dtypes=(BF16,))
    big = {}
    big["w_out"] = matmul("w_out_dw", sv["merged"], dy, ta=True, out_dtypes=(BF16,))
    dg0, dg1, dg2, dya_, dyb_, dyc_ = rowwise_vjp(
        "merge_b", _f_merge, [(proj, 0, d), (proj, 1, d), (proj, 2, d), (sv["y_a"], 0, d), (sv["y_b"], 0, d), (sv["y_c"], 0, d)], [],
        [(dmerged, 0, d)], [BF16] * 6)
    dya_pre = matmul("w_a_dx", dya_, w["w_a_out"], tb=True, out_dtypes=(BF16,))
    big["w_a_out"] = matmul("w_a_dw", sv["ya"], dya_, ta=True, out_dtypes=(BF16,))
    dyb_pre = matmul("w_b_dx", dyb_, w["w_b_out"], tb=True, out_dtypes=(BF16,))
    big["w_b_out"] = matmul("w_b_dw", sv["yb"], dyb_, ta=True, out_dtypes=(BF16,))
    do = matmul("w_c_dx", dyc_, w["w_c_out"], tb=True, out_dtypes=(BF16,))
    big["w_c_out"] = matmul("w_c_dw", sv["o"], dyc_, ta=True, out_dtypes=(BF16,))
    tok = emit("abco", big)
    qbe, qbo, ke, ko, ve, vo = sv["att"]
    dq, dobe, dobo = attn_dq(qbe, qbo, ke, ko, ve, vo, do, tok)
    dk, dv, dcum = attn_dkv(ke, ko, ve, vo, qbe, qbo, dobe, dobo)
    df, dbf = forget_bwd(sv["fproj"], 0, w["bf_exp"], dcum)
    g["fox_bf"] = dbf[0, ::HEAD_DIM]
    dval, dgate, dwc, g["conv_b"], g["conv_ln_g"], g["conv_ln_b"] = conv_bwd(
        proj, cb + 2, cb + 3, sv["zc"], dyb_pre, w["conv_w"], w["conv_ln_g"], w["conv_ln_b"])
    g["conv_w"] = dwc[:CONV_WIDTH]
    du, dvln, dws, dbexp = spatial_bwd(dya_pre, sv["u"], sv["sv"], sv["vln"], w["ws_t"])
    g["gmlp_ws"] = dws * jnp.tril(jnp.ones((CHUNK, CHUNK), F32))
    g["gmlp_bs"] = dbexp.reshape(CHUNK, m // GROUP_DIM, GROUP_DIM).sum(-1).T
    du_raw, dv_raw, g["gmlp_ln_g"], g["gmlp_ln_b"] = rowwise_vjp(
        "gmlp_in_b", _f_a1, [(proj, cb, m), (proj, cb + 1, m)], [w["gmlp_ln_g"], w["gmlp_ln_b"]], [(du, 0, m), (dvln, 0, m)], [BF16, BF16])
    dproj = jnp.concatenate([dg0, dg1, dg2, du_raw, dv_raw, dval, dgate, dq, dk, dv, df], axis=1)
    tok = emit("in", {"w_in": matmul("w_in_dw", sv["h"], dproj, ta=True, out_dtypes=(BF16,))})
    dh = matmul("w_in_dx", dproj, w["w_in"], tb=True, dep=tok, out_dtypes=(BF16,))
    dx, g["mix_pre_g"], dsc1, dsh1 = _pre_bwd("pre1_b", x, w["mix_pre_g"], sc1, sh1, dh, dx1)
    dmod = jnp.concatenate([dsh1, dsc1, dgt1, dsh2, dsc2, dgt2], axis=0)
    return dx, dmod, g


def local_step(x, target, mods, layers):
    d = x.shape[1]
    saved = []
    for l in range(len(layers)):
        x, sv = layer_fwd(x, mods[l], layers[l])
        saved.append(sv)

    def loss_fn(xv, tv):
        err = xv - tv
        return err * (1.0 / d), jnp.sum(err * err, axis=0, keepdims=True)

    dx, sq = rowwise("loss", loss_fn, [(x, 0, d), (target, 0, d)], [], [(d, F32)], accs=[(1, d)])
    loss = (0.5 / d) * jnp.sum(sq)
    dmods, grads = [None] * len(layers), [None] * len(layers)
    tok = None
    for l in reversed(range(len(layers))):
        dx, dmods[l], grads[l] = layer_bwd(dx, mods[l], saved[l], layers[l]["emit"], tok)
        tok = layers[l]["emit_small"](dmods[l], grads[l])
    return loss, dx, dmods, grads


def exchange(name, arrs, scatter):
    n = len(arrs)

    def body(*refs):
        in_refs, out_refs = refs[:n], refs[n:2 * n]
        send_sems, recv_sems, local_sems = refs[2 * n:]
        x, y, c = lax.axis_index("x"), lax.axis_index("y"), lax.axis_index("c")
        me = 4 * x + 2 * y + c
        local = []
        for a in range(n):
            src = in_refs[a].at[me] if scatter else in_refs[a]
            cp = pltpu.make_async_copy(src, out_refs[a].at[me], local_sems.at[a])
            cp.start()
            local.append(cp)
        remote = []
        for k in range(1, N_DEV):
            px, py, pc = x ^ ((k >> 2) & 1), y ^ ((k >> 1) & 1), c ^ (k & 1)
            peer = 4 * px + 2 * py + pc
            for a in range(n):
                src = in_refs[a].at[peer] if scatter else in_refs[a]
                cp = pltpu.make_async_remote_copy(
                    src_ref=src, dst_ref=out_refs[a].at[me], send_sem=send_sems.at[a * (N_DEV - 1) + k - 1],
                    recv_sem=recv_sems.at[a * (N_DEV - 1) + k - 1], device_id=(px, py, pc), device_id_type=MESH)
                cp.start()
                remote.append(cp)
        for cp in remote:
            cp.wait()
        for cp in local:
            cp.wait()

    hbm = pl.BlockSpec(memory_space=pltpu.HBM)
    out_shape = [jax.ShapeDtypeStruct(a.shape if scatter else (N_DEV,) + a.shape, a.dtype) for a in arrs]
    return pl.pallas_call(
        body, name=name, in_specs=[hbm] * n, out_specs=[hbm] * n, out_shape=out_shape,
        scratch_shapes=[pltpu.SemaphoreType.DMA((n * (N_DEV - 1),)), pltpu.SemaphoreType.DMA((n * (N_DEV - 1),)),
                        pltpu.SemaphoreType.DMA((n,))],
    )(*arrs)


def _peers(x, y, c):
    out = []
    for k in range(1, N_DEV):
        px, py, pc = x ^ ((k >> 2) & 1), y ^ ((k >> 1) & 1), c ^ (k & 1)
        out.append((k - 1, (px, py, pc), 4 * px + 2 * py + pc))
    return out


def _exchange_copies(srcs, lands, send_sems, recv_sems, scatter):
    x, y, c = lax.axis_index("x"), lax.axis_index("y"), lax.axis_index("c")
    me = 4 * x + 2 * y + c
    copies = []
    for slot, pos, peer in _peers(x, y, c):
        for a, (src, land) in enumerate(zip(srcs, lands)):
            copies.append(pltpu.make_async_remote_copy(
                src_ref=src.at[peer] if scatter else src, dst_ref=land.at[me],
                send_sem=send_sems.at[a * (N_DEV - 1) + slot], recv_sem=recv_sems.at[a * (N_DEV - 1) + slot],
                device_id=pos, device_id_type=MESH))
    return me, copies


def exchange_start(name, arrs, scatter, after=None):
    n = len(arrs)
    lands = [lax.empty(a.shape if scatter else (N_DEV,) + a.shape, a.dtype) for a in arrs]
    n_in = 2 * n + (after is not None)

    def body(*refs):
        srcs, lands_ = refs[:n], refs[n:2 * n]
        send_sems, recv_sems = refs[n_in], refs[n_in + 1]
        token = refs[n_in + 2 + 2 * n]
        _, copies = _exchange_copies(srcs, lands_, send_sems, recv_sems, scatter)
        for cp in copies:
            cp.start()
        token[...] = jnp.zeros(token.shape, token.dtype)

    hbm = pl.BlockSpec(memory_space=pltpu.HBM)
    sem = pl.BlockSpec(memory_space=pltpu.SEMAPHORE)
    n_sem = n * (N_DEV - 1)
    res = pl.pallas_call(
        body, name=name,
        out_shape=(pltpu.SemaphoreType.DMA((n_sem,)), pltpu.SemaphoreType.DMA((n_sem,)),
                   *[pltpu.HBM(a.shape, a.dtype) for a in arrs], *[pltpu.HBM(l.shape, l.dtype) for l in lands],
                   jax.ShapeDtypeStruct((8, LANES), F32)),
        in_specs=[hbm] * (2 * n) + ([] if after is None else [pl.BlockSpec(memory_space=pl.ANY)]),
        out_specs=(sem, sem, *([hbm] * (2 * n)), pl.BlockSpec(memory_space=pltpu.VMEM)),
        input_output_aliases={i: 2 + i for i in range(2 * n)},
        compiler_params=pltpu.CompilerParams(has_side_effects=pltpu.SideEffectType.DATAFLOW_SIDE_EFFECTING),
    )(*[pltpu.with_memory_space_constraint(a, pltpu.HBM) for a in arrs],
      *[pltpu.with_memory_space_constraint(l, pltpu.HBM) for l in lands], *([] if after is None else [after]))
    return dict(n=n, scatter=scatter, send=res[0], recv=res[1], srcs=res[2:2 + n], lands=res[2 + n:2 + 2 * n], token=res[2 + 2 * n])


def exchange_wait(name, st, after):
    n, scatter = st["n"], st["scatter"]
    after = list(after) if isinstance(after, (list, tuple)) else [after]

    def body(*refs):
        srcs, lands_ = refs[:n], refs[n:2 * n]
        send_sems, recv_sems = refs[2 * n], refs[2 * n + 1]
        _, copies = _exchange_copies(srcs, lands_, send_sems, recv_sems, scatter)
        for cp in copies:
            cp.wait_send()
            cp.wait_recv()

    hbm = pl.BlockSpec(memory_space=pltpu.HBM)
    sem = pl.BlockSpec(memory_space=pltpu.SEMAPHORE)
    res = pl.pallas_call(
        body, name=name,
        out_shape=tuple(pltpu.HBM(a.shape, a.dtype) for a in (*st["srcs"], *st["lands"])),
        in_specs=[hbm] * (2 * n) + [sem, sem] + [pl.BlockSpec(memory_space=pl.ANY)] * len(after), out_specs=tuple([hbm] * (2 * n)),
        input_output_aliases={i: i for i in range(2 * n)},
        compiler_params=pltpu.CompilerParams(has_side_effects=pltpu.SideEffectType.DATAFLOW_SIDE_EFFECTING),
    )(*st["srcs"], *st["lands"], st["send"], st["recv"], *after)
    return list(res[:n]), list(res[n:])


def adamw_sum(name, parts, w, m, v, tr=256):
    nl = len(parts)
    k, r, c = parts[0].shape
    tr = _tile(r, tr, 16)
    c1 = 1.0 - ADAM_B1 ** ADAM_STEP
    c2 = 1.0 - ADAM_B2 ** ADAM_STEP

    def body(*refs):
        p_refs = refs[:nl]
        w_ref, m_ref, v_ref, g_ref, d_ref, nm_ref, nv_ref = refs[nl:]
        for l in range(nl):
            @pl.when(pl.program_id(0) == l)
            def _(p_ref=p_refs[l]):
                grad = p_ref[0].astype(F32)
                for j in range(1, k):
                    grad = grad + p_ref[j].astype(F32)
                new_m = ADAM_B1 * m_ref[...] + (1.0 - ADAM_B1) * grad
                new_v = ADAM_B2 * v_ref[...] + (1.0 - ADAM_B2) * (grad * grad)
                m_hat = new_m / c1
                v_hat = new_v / c2
                g_ref[...] = grad
                d_ref[...] = -ADAM_LR * (m_hat / (jnp.sqrt(v_hat) + ADAM_EPS) + ADAM_WD * w_ref[...])
                nm_ref[...] = new_m
                nv_ref[...] = new_v

    part = lambda l: pl.BlockSpec((k, tr, c), functools.partial(lambda ll, i, l: (0, jnp.where(ll == l, i, 0), 0), l=l))
    blk = pl.BlockSpec((None, tr, c), lambda ll, i: (ll, i, 0))
    return pl.pallas_call(
        body, name=name, grid=(nl, r // tr),
        in_specs=[part(l) for l in range(nl)] + [blk, blk, blk],
        out_specs=[blk] * 4, out_shape=[jax.ShapeDtypeStruct((nl, r, c), F32)] * 4,
        compiler_params=_params(("parallel", "parallel")),
    )(*parts, w, m, v)


def adamw_sum_cols(name, parts, w, m, v, tc=256):
    nl = len(parts)
    k, r, c = parts[0].shape
    tc = _tile(c, tc)
    c1 = 1.0 - ADAM_B1 ** ADAM_STEP
    c2 = 1.0 - ADAM_B2 ** ADAM_STEP

    def body(*refs):
        p_refs = refs[:nl]
        w_ref, m_ref, v_ref, g_ref, d_ref, nm_ref, nv_ref = refs[nl:]
        for l in range(nl):
            @pl.when(pl.program_id(0) == l)
            def _(p_ref=p_refs[l]):
                grad = p_ref[0].astype(F32)
                for j in range(1, k):
                    grad = grad + p_ref[j].astype(F32)
                new_m = ADAM_B1 * m_ref[...] + (1.0 - ADAM_B1) * grad
                new_v = ADAM_B2 * v_ref[...] + (1.0 - ADAM_B2) * (grad * grad)
                m_hat = new_m / c1
                v_hat = new_v / c2
                g_ref[...] = grad
                d_ref[...] = -ADAM_LR * (m_hat / (jnp.sqrt(v_hat) + ADAM_EPS) + ADAM_WD * w_ref[...])
                nm_ref[...] = new_m
                nv_ref[...] = new_v

    part = lambda l: pl.BlockSpec((k, r, tc), functools.partial(lambda ll, j, l: (0, 0, jnp.where(ll == l, j, 0)), l=l))
    blk = pl.BlockSpec((r, tc), lambda ll, j: (0, ll * (c // tc) + j))
    return pl.pallas_call(
        body, name=name, grid=(nl, c // tc),
        in_specs=[part(l) for l in range(nl)] + [blk, blk, blk],
        out_specs=[blk] * 4, out_shape=[jax.ShapeDtypeStruct((r, nl * c), F32)] * 4,
        compiler_params=_params(("parallel", "parallel")),
    )(*parts, w, m, v)


def ada_fwd(c_all, ada_w):
    nl, d, n = ada_w.shape

    def body(c_ref, w_ref, o_ref):
        o_ref[...] = jnp.dot(_silu(c_ref[...]), w_ref[...], precision=HIGHEST, preferred_element_type=F32)

    return pl.pallas_call(
        body, name="ada_fwd", grid=(nl,),
        in_specs=[pl.BlockSpec((N_DEV, d), lambda l: (0, 0)), pl.BlockSpec((None, d, n), lambda l: (l, 0, 0))],
        out_specs=pl.BlockSpec((None, N_DEV, n), lambda l: (l, 0, 0)),
        out_shape=jax.ShapeDtypeStruct((nl, N_DEV, n), F32),
        compiler_params=_params(("parallel",)),
    )(c_all, ada_w)


def ada_bwd(c_all_t, dmod, td=256):
    d = c_all_t.shape[0]
    nl, _, n = dmod.shape
    td = _tile(d, td, 8)

    def body(c_ref, dm_ref, o_ref):
        ca = _silu(c_ref[...])
        acc = ca[:, 0:1] * dm_ref[0:1, :]
        for b in range(1, N_DEV):
            acc = acc + ca[:, b:b + 1] * dm_ref[b:b + 1, :]
        o_ref[...] = acc

    return pl.pallas_call(
        body, name="ada_bwd", grid=(nl, d // td),
        in_specs=[pl.BlockSpec((td, N_DEV), lambda l, i: (i, 0)), pl.BlockSpec((None, N_DEV, n), lambda l, i: (l, 0, 0))],
        out_specs=pl.BlockSpec((None, td, n), lambda l, i: (l, i, 0)),
        out_shape=jax.ShapeDtypeStruct((nl, d, n), F32),
        compiler_params=_params(("parallel", "parallel")),
    )(c_all_t, dmod)


ARG_NAMES = ["x", "c", "ada_w", "ada_b", "mix_pre_g", "mix_post_g", "mlp_pre_g", "mlp_post_g", "w_in", "gmlp_ln_g", "gmlp_ln_b",
             "gmlp_ws", "gmlp_bs", "w_a_out", "conv_w", "conv_b", "conv_ln_g", "conv_ln_b", "w_b_out", "fox_bf", "w_c_out",
             "w_out", "mlp_w1", "mlp_w2"]
WEIGHTS = ARG_NAMES[2:]
COL_SHARDED = ["w_in", "w_a_out", "w_b_out", "w_c_out", "mlp_w1"]
ROW_SHARDED = ["w_out", "mlp_w2"]
BIG = COL_SHARDED + ROW_SHARDED
GROUPS = {"in": ["w_in"], "abco": ["w_a_out", "w_b_out", "w_c_out", "w_out"], "mlp": ["mlp_w1", "mlp_w2"]}
SMALL = ["ada_b", "mix_pre_g", "mix_post_g", "mlp_pre_g", "mlp_post_g", "gmlp_ln_g", "gmlp_ln_b", "gmlp_ws", "gmlp_bs",
         "conv_b", "conv_ln_g", "conv_ln_b", "fox_bf"]
PACK_COLS = 512


def _to_my_layout(w_in, d):
    m = D_MIX
    nf = 7 * m
    return jnp.concatenate([w_in[..., nf + N_HEADS:], w_in[..., :nf], jnp.repeat(w_in[..., nf:nf + N_HEADS], HEAD_DIM, axis=-1)], axis=-1)


def _from_my_layout(gw, d):
    m = D_MIX
    return jnp.concatenate([gw[..., 3 * d:3 * d + 7 * m], gw[..., 3 * d + 7 * m::HEAD_DIM], gw[..., :3 * d]], axis=-1)


def _ref_ranges(lo, hi, shard):
    out = []
    while lo < hi:
        j = lo // shard
        end = min(hi, (j + 1) * shard)
        out.append((j, lo - j * shard, end - j * shard))
        lo = end
    return out


def _w_in_to_kernel_layout(g, d):
    m = D_MIX
    nf = 7 * m
    shard = g.shape[2]
    cols = lambda lo, hi: [g[j, :, a:b] for j, a, b in _ref_ranges(lo, hi, shard)]
    forget = jnp.concatenate(cols(nf, nf + N_HEADS), axis=1)
    return jnp.concatenate(cols(nf + N_HEADS, nf + N_HEADS + 3 * d) + cols(0, nf) + [jnp.repeat(forget, HEAD_DIM, axis=1)], axis=1)


def _w_in_grad_blocks(gw, d):
    m = D_MIX
    nf = 7 * m
    n_ref = nf + N_HEADS + 3 * d
    shard = n_ref // N_DEV
    segs = [(0, nf, 3 * d, 1), (nf, nf + N_HEADS, 3 * d + nf, HEAD_DIM), (nf + N_HEADS, n_ref, 0, 1)]
    blocks = []
    for j in range(N_DEV):
        lo, hi = j * shard, (j + 1) * shard
        pieces = []
        for r0, r1, k0, stride in segs:
            a, b = max(lo, r0), min(hi, r1)
            if a < b:
                pieces.append(gw[:, k0 + (a - r0) * stride:k0 + (b - r0) * stride:stride])
        blocks.append(jnp.concatenate(pieces, axis=1) if len(pieces) > 1 else pieces[0])
    return jnp.stack(blocks)


def _pack(parts):
    flat = jnp.concatenate([p.reshape(-1).astype(F32) for p in parts])
    pad = (-flat.shape[0]) % (PACK_COLS * 8)
    return jnp.pad(flat, (0, pad)).reshape(-1, PACK_COLS)


def _unpack(packed, shapes):
    nl = packed.shape[0]
    flat, out, off = packed.reshape(nl, -1), [], 0
    for shp in shapes:
        n = math.prod(shp)
        out.append(flat[:, off:off + n].reshape((nl,) + tuple(shp)))
        off += n
    return out


def _layer_small(p, conv_full, l):
    wl = {}
    for k in ["mix_pre_g", "mix_post_g", "mlp_pre_g", "mlp_post_g", "gmlp_ln_g", "gmlp_ln_b", "conv_b", "conv_ln_g", "conv_ln_b"]:
        wl[k] = p[k][l][None, :]
    wm = p["gmlp_ws"][l] * jnp.tril(jnp.ones((CHUNK, CHUNK), F32))
    wl["ws"] = wm.astype(BF16)
    wl["ws_t"] = jnp.transpose(wm, (0, 2, 1)).astype(BF16)
    wl["bs_exp"] = jnp.repeat(p["gmlp_bs"][l].T, GROUP_DIM, axis=1)
    wl["bf_exp"] = jnp.repeat(p["fox_bf"][l], HEAD_DIM)[None, :]
    wl["conv_w"] = jnp.pad(conv_full[l], ((0, CONV_HALO - CONV_WIDTH), (0, 0)))
    return wl


def kernel(x, c, ada_w, ada_b, mix_pre_g, mix_post_g, mlp_pre_g, mlp_post_g, w_in, gmlp_ln_g, gmlp_ln_b, gmlp_ws, gmlp_bs, w_a_out, conv_w, conv_b, conv_ln_g, conv_ln_b, w_b_out, fox_bf, w_c_out, w_out, mlp_w1, mlp_w2, loss_target, m_ada_w, m_ada_b, m_mix_pre_g, m_mix_post_g, m_mlp_pre_g, m_mlp_post_g, m_w_in, m_gmlp_ln_g, m_gmlp_ln_b, m_gmlp_ws, m_gmlp_bs, m_w_a_out, m_conv_w, m_conv_b, m_conv_ln_g, m_conv_ln_b, m_w_b_out, m_fox_bf, m_w_c_out, m_w_out, m_mlp_w1, m_mlp_w2, v_ada_w, v_ada_b, v_mix_pre_g, v_mix_post_g, v_mlp_pre_g, v_mlp_post_g, v_w_in, v_gmlp_ln_g, v_gmlp_ln_b, v_gmlp_ws, v_gmlp_bs, v_w_a_out, v_conv_w, v_conv_b, v_conv_ln_g, v_conv_ln_b, v_w_b_out, v_fox_bf, v_w_c_out, v_w_out, v_mlp_w1, v_mlp_w2):
    args = (x, c, ada_w, ada_b, mix_pre_g, mix_post_g, mlp_pre_g, mlp_post_g, w_in, gmlp_ln_g, gmlp_ln_b, gmlp_ws, gmlp_bs, w_a_out,
            conv_w, conv_b, conv_ln_g, conv_ln_b, w_b_out, fox_bf, w_c_out, w_out, mlp_w1, mlp_w2)
    ms = (m_ada_w, m_ada_b, m_mix_pre_g, m_mix_post_g, m_mlp_pre_g, m_mlp_post_g, m_w_in, m_gmlp_ln_g, m_gmlp_ln_b, m_gmlp_ws, m_gmlp_bs,
          m_w_a_out, m_conv_w, m_conv_b, m_conv_ln_g, m_conv_ln_b, m_w_b_out, m_fox_bf, m_w_c_out, m_w_out, m_mlp_w1, m_mlp_w2)
    vs = (v_ada_w, v_ada_b, v_mix_pre_g, v_mix_post_g, v_mlp_pre_g, v_mlp_post_g, v_w_in, v_gmlp_ln_g, v_gmlp_ln_b, v_gmlp_ws, v_gmlp_bs,
          v_w_a_out, v_conv_w, v_conv_b, v_conv_ln_g, v_conv_ln_b, v_w_b_out, v_fox_bf, v_w_c_out, v_w_out, v_mlp_w1, v_mlp_w2)
    p = dict(zip(ARG_NAMES, args))
    mom = dict(zip(WEIGHTS, ms))
    var = dict(zip(WEIGHTS, vs))
    nl = ada_w.shape[0]
    s, d = x.shape[1], x.shape[2]
    me = 4 * lax.axis_index("x") + 2 * lax.axis_index("y") + lax.axis_index("c")

    c_all, conv_all = exchange("gather_c", [c, conv_w], scatter=False)
    c_all = c_all.reshape(N_DEV, d)
    n_ada = ada_w.shape[2]
    mod_parts = ada_fwd(c_all, ada_w)
    (mod_recv,) = exchange("scatter_mod", [jnp.transpose(mod_parts, (1, 0, 2))], scatter=True)
    conv_full = jnp.transpose(conv_all, (1, 2, 0, 3)).reshape(nl, CONV_WIDTH, D_MIX)

    def full_matrix(k, land, own):
        g = lax.dynamic_update_index_in_dim(land, own, me, 0)
        r, cc = own.shape
        if k == "w_in":
            return _w_in_to_kernel_layout(g, d)
        return jnp.transpose(g, (1, 0, 2)).reshape(r, N_DEV * cc) if k in COL_SHARDED else g.reshape(N_DEV * r, cc)

    started = [mod_recv]

    def fetch(l, tag):
        keys = GROUPS[tag]
        st = exchange_start(f"gather_{tag}{l}_start", [p[k][l].astype(BF16) for k in keys], scatter=False, after=started[-1])
        started.append(st["token"])

        def get(after):
            owns, lands = exchange_wait(f"gather_{tag}{l}_wait", st, after)
            return {k: full_matrix(k, land, o) for k, land, o in zip(keys, lands, owns)}

        return get

    getters = [{tag: fetch(l, tag) for tag in GROUPS} for l in range(nl)]
    mod = jnp.transpose(mod_recv, (1, 0, 2)).reshape(nl, N_DEV * n_ada) + ada_b + started[-1][0:1, 0:1]
    mods = [mod[l].reshape(6, d) for l in range(nl)]

    sent = {}

    def emitter(l):
        def emit(tag, grads_big):
            keys = GROUPS[tag]
            send = []
            for k in keys:
                gk = grads_big[k]
                if k == "w_in":
                    gk = jnp.transpose(_w_in_grad_blocks(gk, d), (0, 2, 1))
                elif k in COL_SHARDED:
                    r, cc = gk.shape[0], gk.shape[1] // N_DEV
                    gk = jnp.transpose(gk.reshape(r, N_DEV, cc), (1, 0, 2))
                else:
                    gk = gk.reshape(N_DEV, gk.shape[0] // N_DEV, gk.shape[1])
                send.append(gk.astype(BF16))
            sent[(l, tag)] = (keys, exchange_start(f"scatter_{tag}{l}_start", send, scatter=True))
            return sent[(l, tag)][1]["token"]

        return emit

    small_sent = {}

    def small_emitter(l):
        def emit_small(dmod_l, g):
            packed = _pack([dmod_l] + [g[k] for k in SMALL[1:]] + [g["conv_w"]])
            small_sent[l] = exchange_start(f"gather_small{l}_start", [packed], scatter=False)
            return small_sent[l]["token"]

        return emit_small

    layers = [dict(small=_layer_small(p, conv_full, l), emit_small=small_emitter(l), get_in=getters[l]["in"], get_abco=getters[l]["abco"], get_mlp=getters[l]["mlp"],
                   emit=emitter(l)) for l in range(nl)]
    loss_local, dx, dmods, grads = local_step(x[0], loss_target[0], mods, layers)
    loss = lax.psum(loss_local, ("x", "y", "c"))
    grad_x = dx[None]

    out = {k: [None] * 4 for k in WEIGHTS}

    def shard_update(name, parts, k):
        shp = p[k].shape
        flat = lambda a: a.reshape(nl, -1, shp[-1])
        res = adamw_sum(name, [pt.reshape((pt.shape[0],) + flat(p[k]).shape[1:]) for pt in parts], flat(p[k]), flat(mom[k]), flat(var[k]))
        out[k] = [a.reshape(shp) for a in res]

    parts = {}
    for l in reversed(range(nl)):
        for tag in ("mlp", "abco", "in"):
            keys, st = sent[(l, tag)]
            sends, lands = exchange_wait(f"scatter_{tag}{l}_wait", st, dx)
            for k, land, sd in zip(keys, lands, sends):
                own = lax.dynamic_index_in_dim(sd, me, 0, keepdims=False)
                parts[(k, l)] = lax.dynamic_update_index_in_dim(land, own, me, 0)
    for k in BIG[1:]:
        shard_update("adamw_" + k, [parts[(k, l)] for l in range(nl)], k)
    n_in = w_in.shape[2]
    res_in = adamw_sum_cols("adamw_w_in", [parts[("w_in", l)] for l in range(nl)],
                            *[jnp.transpose(src["w_in"], (2, 0, 1)).reshape(n_in, nl * d) for src in (p, mom, var)])
    out["w_in"] = [jnp.transpose(a.reshape(n_in, nl, d), (1, 2, 0)) for a in res_in]

    small_shapes = [p[k].shape[1:] for k in SMALL] + [(CONV_WIDTH, D_MIX)]
    small_all = []
    for l in range(nl):
        srcs, lands = exchange_wait(f"gather_small{l}_wait", small_sent[l], [out[k][0] for k in BIG])
        small_all.append(lax.dynamic_update_index_in_dim(lands[0], srcs[0], me, 0))
    zeros_conv = jnp.zeros(small_shapes[-1], F32)
    packs = [jnp.stack([_pack([src[k][l] for k in SMALL] + [zeros_conv]) for l in range(nl)]) for src in (p, mom, var)]
    small_out = [_unpack(o, small_shapes) for o in adamw_sum("adamw_small", small_all, *packs)]
    for i, k in enumerate(SMALL):
        for j in range(4):
            out[k][j] = small_out[j][i]

    n_conv = conv_w.shape[2]
    conv_grad = lax.dynamic_slice_in_dim(small_out[0][-1], me * n_conv, n_conv, axis=2)
    shard_update("adamw_conv_w", [conv_grad[l][None] for l in range(nl)], "conv_w")

    dmod_all = jnp.stack([small_all[l].reshape(N_DEV, -1)[:, :6 * d] for l in range(nl)])
    dmod_mine = lax.dynamic_slice_in_dim(dmod_all, me * n_ada, n_ada, axis=2)
    g_ada = ada_bwd(c_all.T, dmod_mine)
    shard_update("adamw_ada_w", [g_ada[l][None] for l in range(nl)], "ada_w")

    res = [loss, grad_x]
    for j in range(4):
        res += [out[k][j] for k in WEIGHTS]
    return tuple(res)
```

```python
import functools
import math

import jax
import jax.numpy as jnp
from jax import lax
from jax.experimental import pallas as pl
from jax.experimental.pallas import tpu as pltpu

F32 = jnp.float32
BF16 = jnp.bfloat16
MESH = pl.DeviceIdType.MESH
N_DEV = 8
NORM_EPS = 1e-6
D_MIX = 512
N_HEADS = 8
HEAD_DIM = 64
GROUP_DIM = 64
CHUNK = 128
CONV_WIDTH = 31
CONV_HALO = 32
LANES = 128
ADAM_LR, ADAM_B1, ADAM_B2, ADAM_EPS, ADAM_WD, ADAM_STEP = 0.001, 0.9, 0.999, 1e-08, 0.01, 10
VMEM_LIMIT = 56 * 1024 * 1024
HIGHEST = lax.Precision.HIGHEST


def _tile(dim, pref, mult=LANES):
    t = min(pref, dim)
    t -= t % mult
    while t >= mult:
        if dim % t == 0:
            return t
        t -= mult
    return dim


def _params(sem):
    return pltpu.CompilerParams(dimension_semantics=sem, vmem_limit_bytes=VMEM_LIMIT)


def rowwise(name, fn, rows, consts, outs, accs=(), ts=512):
    s = rows[0][0].shape[0]
    ts = min(ts, s)
    nr, nc, no, na = len(rows), len(consts), len(outs), len(accs)

    def body(*refs):
        vals = [r[...].astype(F32) for r in refs[:nr]] + [r[...] for r in refs[nr:nr + nc]]
        res = fn(*vals)
        if not isinstance(res, (tuple, list)):
            res = (res,)
        for r, v in zip(refs[nr + nc:nr + nc + no], res[:no]):
            r[...] = v.astype(r.dtype)
        if na:
            acc_refs = refs[nr + nc + no:]

            @pl.when(pl.program_id(0) == 0)
            def _():
                for r in acc_refs:
                    r[...] = jnp.zeros(r.shape, r.dtype)

            for r, v in zip(acc_refs, res[no:]):
                r[...] += v.astype(F32)

    in_specs = [pl.BlockSpec((ts, w), functools.partial(lambda i, cb: (i, cb), cb=cb)) for (_, cb, w) in rows]
    in_specs += [pl.BlockSpec(c.shape, lambda i: (0, 0)) for c in consts]
    out_specs = [pl.BlockSpec((ts, w), lambda i: (i, 0)) for (w, _) in outs]
    out_specs += [pl.BlockSpec(shp, lambda i: (0, 0)) for shp in accs]
    out_shape = [jax.ShapeDtypeStruct((s, w), dt) for (w, dt) in outs]
    out_shape += [jax.ShapeDtypeStruct(shp, F32) for shp in accs]
    res = pl.pallas_call(
        body, name=name, grid=(s // ts,), in_specs=in_specs, out_specs=out_specs, out_shape=out_shape,
        compiler_params=_params(("arbitrary",) if na else ("parallel",)),
    )(*[a for (a, _, _) in rows], *consts)
    return res


def rowwise_vjp(name, f, rows, consts, cts, grad_dtypes, ts=512):
    nr, nc, nt = len(rows), len(consts), len(cts)
    keep = [i for i, dt in enumerate(grad_dtypes) if dt is not None]

    def g(*vals):
        rv = [v.astype(F32) for v in vals[:nr]]
        ctv = tuple(v.astype(F32) for v in vals[nr:nr + nt])
        cv = list(vals[nr + nt:])
        _, vjp = jax.vjp(lambda *a: tuple(f(*a)), *rv, *cv)
        grads = vjp(ctv)
        return tuple(grads[i] for i in keep) + tuple(grads[nr:])

    outs = [(rows[i][2], grad_dtypes[i]) for i in keep]
    return rowwise(name, g, list(rows) + list(cts), consts, outs, accs=[c.shape for c in consts], ts=ts)


def matmul(name, a, b, *, ta=False, tb=False, out_dtypes=(F32,), epilogue=None, epi=(), tm=None, tn=None, tk=4096, dep=None, b_cols=None):
    m, k = (a.shape[1], a.shape[0]) if ta else a.shape
    n = b.shape[0] if tb else b.shape[1]
    b_col0 = 0
    if b_cols is not None:
        assert not tb
        b_col0, n = b_cols
    assert (b.shape[1] if tb else b.shape[0]) == k
    tk = _tile(k, tk)
    if tk > 1024:
        tm, tn = _tile(m, tm or 1024), _tile(n, tn or 1024)
    else:
        tm, tn = _tile(m, tm or 2048), _tile(n, tn or (1024 if m >= 2048 else 2048))
    assert b_col0 % tn == 0
    jb = b_col0 // tn
    nk = k // tk
    ne, no = len(epi), len(out_dtypes)
    dims = (((0 if ta else 1,), (1 if tb else 0,)), ((), ()))

    def body(*refs):
        a_ref, b_ref = refs[0], refs[1]
        epi_refs = refs[2:2 + ne]
        n_in = 2 + ne + (dep is not None)
        out_refs = refs[n_in:n_in + no]
        part = lax.dot_general(a_ref[...].astype(BF16), b_ref[...].astype(BF16), dims, preferred_element_type=F32)

        def finish(acc):
            res = (acc,) if epilogue is None else epilogue(acc, *[r[...] for r in epi_refs])
            for r, v in zip(out_refs, res):
                r[...] = v.astype(r.dtype)

        if nk == 1:
            finish(part)
        else:
            acc_ref = refs[-1]
            kk = pl.program_id(2)

            @pl.when(kk == 0)
            def _():
                acc_ref[...] = part

            @pl.when(kk > 0)
            def _():
                acc_ref[...] += part

            @pl.when(kk == nk - 1)
            def _():
                finish(acc_ref[...])

    a_spec = pl.BlockSpec((tk, tm), lambda i, j, kk: (kk, i)) if ta else pl.BlockSpec((tm, tk), lambda i, j, kk: (i, kk))
    b_spec = pl.BlockSpec((tn, tk), lambda i, j, kk: (j, kk)) if tb else pl.BlockSpec((tk, tn), lambda i, j, kk: (kk, j + jb))
    epi_specs = []
    for (arr, col0) in epi:
        assert col0 % tn == 0
        epi_specs.append(pl.BlockSpec((tm, tn), functools.partial(lambda i, j, kk, c0: (i, j + c0), c0=col0 // tn)))
    res = pl.pallas_call(
        body, name=name, grid=(m // tm, n // tn, nk),
        in_specs=[a_spec, b_spec] + epi_specs + ([] if dep is None else [pl.BlockSpec(dep.shape, lambda i, j, kk: (0, 0))]),
        out_specs=[pl.BlockSpec((tm, tn), lambda i, j, kk: (i, j)) for _ in out_dtypes],
        out_shape=[jax.ShapeDtypeStruct((m, n), dt) for dt in out_dtypes],
        scratch_shapes=[pltpu.VMEM((tm, tn), F32)] if nk > 1 else [],
        compiler_params=_params(("parallel", "parallel", "arbitrary")),
    )(a, b, *[arr for (arr, _) in epi], *([] if dep is None else [dep]))
    return res[0] if no == 1 else res


def _rms(x, g):
    return x * lax.rsqrt(jnp.mean(x * x, axis=-1, keepdims=True) + NORM_EPS) * g


def _ln(x, g, b):
    mu = jnp.mean(x, axis=-1, keepdims=True)
    xc = x - mu
    var = jnp.mean(xc * xc, axis=-1, keepdims=True)
    return xc * lax.rsqrt(var + NORM_EPS) * g + b


def _gelu(x):
    return 0.5 * x * (1.0 + jnp.tanh(math.sqrt(2.0 / math.pi) * (x + 0.044715 * (x * x * x))))


def _sigmoid(x):
    return 1.0 / (1.0 + jnp.exp(-x))


def _silu(x):
    return x * _sigmoid(x)


def _log_sigmoid(x):
    return jnp.minimum(x, 0.0) - jnp.log(1.0 + jnp.exp(-jnp.abs(x)))


def _f_pre(x, g, sc, sh):
    return (_rms(x, g) * (1.0 + sc) + sh,)


def _f_post(y, g, gt):
    return (gt * _rms(y, g),)


def _f_a1(u_raw, v_raw, g, b):
    return _gelu(u_raw), _ln(_gelu(v_raw), g, b)


def _f_glu(val, gate):
    return (val * _sigmoid(gate),)


def _f_lnsilu(zc, g, b):
    return (_silu(_ln(zc, g, b)),)


def _f_merge(g0, g1, g2, ya, yb, yc):
    return (_sigmoid(g0) * ya + _sigmoid(g1) * yb + _sigmoid(g2) * yc,)


def _lane_lt64(shape):
    return lax.broadcasted_iota(jnp.int32, shape, 1) < HEAD_DIM


def spatial_fwd(vln, u, w_bf, b_exp, rows_per_step=512):
    s = vln.shape[0]
    tr = min(rows_per_step, s)

    def body(v_ref, u_ref, w_ref, b_ref, sv_ref, ya_ref):
        lo = _lane_lt64((CHUNK, LANES))
        for ch in range(tr // CHUNK):
            r0 = ch * CHUNK
            for p in range(D_MIX // LANES):
                vp = v_ref[r0:r0 + CHUNK, p * LANES:(p + 1) * LANES]
                o0 = jnp.dot(w_ref[2 * p], vp, preferred_element_type=F32)
                o1 = jnp.dot(w_ref[2 * p + 1], vp, preferred_element_type=F32)
                sv = jnp.where(lo, o0, o1) + b_ref[:, p * LANES:(p + 1) * LANES]
                sv_ref[r0:r0 + CHUNK, p * LANES:(p + 1) * LANES] = sv
                ya_ref[r0:r0 + CHUNK, p * LANES:(p + 1) * LANES] = (
                    u_ref[r0:r0 + CHUNK, p * LANES:(p + 1) * LANES] * sv).astype(BF16)

    row = pl.BlockSpec((tr, D_MIX), lambda i: (i, 0))
    return pl.pallas_call(
        body, name="spatial_fwd", grid=(s // tr,),
        in_specs=[row, row, pl.BlockSpec(w_bf.shape, lambda i: (0, 0, 0)), pl.BlockSpec(b_exp.shape, lambda i: (0, 0))],
        out_specs=[row, row],
        out_shape=[jax.ShapeDtypeStruct((s, D_MIX), F32), jax.ShapeDtypeStruct((s, D_MIX), BF16)],
        compiler_params=_params(("parallel",)),
    )(vln, u, w_bf, b_exp)


def spatial_bwd(dya, u, sv, vln, wt_bf, rows_per_step=512):
    s = vln.shape[0]
    tr = min(rows_per_step, s)
    ng = wt_bf.shape[0]

    def body(dya_ref, u_ref, sv_ref, v_ref, wt_ref, du_ref, dv_ref, dw_ref, db_ref):
        @pl.when(pl.program_id(0) == 0)
        def _():
            dw_ref[...] = jnp.zeros(dw_ref.shape, F32)
            db_ref[...] = jnp.zeros(db_ref.shape, F32)

        lo = _lane_lt64((CHUNK, LANES))
        for ch in range(tr // CHUNK):
            r0 = ch * CHUNK
            for p in range(D_MIX // LANES):
                cs = slice(p * LANES, (p + 1) * LANES)
                dya_p = dya_ref[r0:r0 + CHUNK, cs].astype(F32)
                du_ref[r0:r0 + CHUNK, cs] = dya_p * sv_ref[r0:r0 + CHUNK, cs]
                dsv = dya_p * u_ref[r0:r0 + CHUNK, cs]
                db_ref[:, cs] += dsv
                dsv0 = jnp.where(lo, dsv, 0.0).astype(BF16)
                dsv1 = jnp.where(lo, 0.0, dsv).astype(BF16)
                vp = v_ref[r0:r0 + CHUNK, cs]
                d0 = jnp.dot(wt_ref[2 * p], dsv0, preferred_element_type=F32)
                d1 = jnp.dot(wt_ref[2 * p + 1], dsv1, preferred_element_type=F32)
                dv_ref[r0:r0 + CHUNK, cs] = d0 + d1
                nt = (((1,), (1,)), ((), ()))
                dw_ref[2 * p] += lax.dot_general(dsv0, vp, nt, preferred_element_type=F32)
                dw_ref[2 * p + 1] += lax.dot_general(dsv1, vp, nt, preferred_element_type=F32)

    row = pl.BlockSpec((tr, D_MIX), lambda i: (i, 0))
    return pl.pallas_call(
        body, name="spatial_bwd", grid=(s // tr,),
        in_specs=[row, row, row, row, pl.BlockSpec(wt_bf.shape, lambda i: (0, 0, 0))],
        out_specs=[row, row, pl.BlockSpec((ng, CHUNK, CHUNK), lambda i: (0, 0, 0)), pl.BlockSpec((CHUNK, D_MIX), lambda i: (0, 0))],
        out_shape=[jax.ShapeDtypeStruct((s, D_MIX), F32), jax.ShapeDtypeStruct((s, D_MIX), F32),
                   jax.ShapeDtypeStruct((ng, CHUNK, CHUNK), F32), jax.ShapeDtypeStruct((CHUNK, D_MIX), F32)],
        compiler_params=_params(("arbitrary",)),
    )(dya, u, sv, vln, wt_bf)


def _windows(ref, first, count, ts):
    for r in range(8):
        ks = [k for k in range(count) if (first + k) % 8 == r]
        if ks:
            base = first + ks[0]
            blk = ref[base:base + ks[-1] - ks[0] + ts, :]
            for k in ks:
                yield k, blk[k - ks[0]:k - ks[0] + ts, :]


def conv_fwd(proj, cb_val, cb_gate, w_pad, cb, ln_g, ln_b, ts=256):
    s = proj.shape[0]
    ts = min(ts, s)
    per = ts // CONV_HALO

    def body(val_ref, gate_ref, pval_ref, pgate_ref, w_ref, cb_ref, g_ref, b_ref, zc_ref, yb_ref, ext_ref):
        i = pl.program_id(0)
        zprev = pval_ref[...].astype(F32) * _sigmoid(pgate_ref[...].astype(F32))
        ext_ref[0:CONV_HALO, :] = jnp.where(i > 0, zprev, 0.0)
        ext_ref[CONV_HALO:, :] = val_ref[...].astype(F32) * _sigmoid(gate_ref[...].astype(F32))
        acc = jnp.zeros((ts, D_MIX), F32)
        for j, win in _windows(ext_ref, CONV_HALO - (CONV_WIDTH - 1), CONV_WIDTH, ts):
            acc = acc + w_ref[j:j + 1, :] * win
        zc = acc + cb_ref[...]
        zc_ref[...] = zc
        yb_ref[...] = _f_lnsilu(zc, g_ref[...], b_ref[...])[0].astype(BF16)

    def cur(c):
        return pl.BlockSpec((ts, D_MIX), functools.partial(lambda i, c: (i, c), c=c))

    def prev(c):
        return pl.BlockSpec((CONV_HALO, D_MIX), functools.partial(lambda i, c: (jnp.maximum(i * per - 1, 0), c), c=c))

    const = lambda a: pl.BlockSpec(a.shape, lambda i: (0, 0))
    out = pl.BlockSpec((ts, D_MIX), lambda i: (i, 0))
    return pl.pallas_call(
        body, name="conv_fwd", grid=(s // ts,),
        in_specs=[cur(cb_val), cur(cb_gate), prev(cb_val), prev(cb_gate), const(w_pad), const(cb), const(ln_g), const(ln_b)],
        out_specs=[out, out],
        out_shape=[jax.ShapeDtypeStruct((s, D_MIX), F32), jax.ShapeDtypeStruct((s, D_MIX), BF16)],
        scratch_shapes=[pltpu.VMEM((CONV_HALO + ts, D_MIX), F32)],
        compiler_params=_params(("parallel",)),
    )(proj, proj, proj, proj, w_pad, cb, ln_g, ln_b)


def conv_bwd(proj, cb_val, cb_gate, zc, dyb, w_pad, ln_g, ln_b, ts=256):
    s = proj.shape[0]
    ts = min(ts, s)
    per = ts // CONV_HALO
    n_tiles = s // ts
    n_halo = s // CONV_HALO

    def body(val_ref, gate_ref, pval_ref, pgate_ref, zc_ref, dyb_ref, nzc_ref, ndyb_ref, w_ref, g_ref, b_ref,
             dval_ref, dgate_ref, dw_ref, dcb_ref, dg_ref, db_ref, zext_ref, dext_ref):
        i = pl.program_id(0)

        @pl.when(i == 0)
        def _():
            for r in (dw_ref, dcb_ref, dg_ref, db_ref):
                r[...] = jnp.zeros(r.shape, F32)

        g, b = g_ref[...], b_ref[...]
        _, vjp = jax.vjp(lambda z, gg, bb: _f_lnsilu(z, gg, bb)[0], zc_ref[...], g, b)
        dzc, dg, db = vjp(dyb_ref[...].astype(F32))
        dg_ref[...] += dg
        db_ref[...] += db
        dcb_ref[...] += jnp.sum(dzc, axis=0, keepdims=True)
        _, vjp_n = jax.vjp(lambda z: _f_lnsilu(z, g, b)[0], nzc_ref[...])
        (dzc_next,) = vjp_n(ndyb_ref[...].astype(F32))
        dext_ref[0:ts, :] = dzc
        dext_ref[ts:, :] = jnp.where(i < n_tiles - 1, dzc_next, 0.0)
        val, gate = val_ref[...].astype(F32), gate_ref[...].astype(F32)
        zprev = pval_ref[...].astype(F32) * _sigmoid(pgate_ref[...].astype(F32))
        zext_ref[0:CONV_HALO, :] = jnp.where(i > 0, zprev, 0.0)
        zext_ref[CONV_HALO:, :] = val * _sigmoid(gate)
        dz = jnp.zeros((ts, D_MIX), F32)
        for shift, win in _windows(dext_ref, 0, CONV_WIDTH, ts):
            j = CONV_WIDTH - 1 - shift
            dz = dz + w_ref[j:j + 1, :] * win
        for j, win in _windows(zext_ref, CONV_HALO - (CONV_WIDTH - 1), CONV_WIDTH, ts):
            dw_ref[j:j + 1, :] += jnp.sum(dzc * win, axis=0, keepdims=True)
        _, vjp_glu = jax.vjp(lambda a, c: _f_glu(a, c)[0], val, gate)
        dval, dgate = vjp_glu(dz)
        dval_ref[...] = dval.astype(BF16)
        dgate_ref[...] = dgate.astype(BF16)

    def cur(c):
        return pl.BlockSpec((ts, D_MIX), functools.partial(lambda i, c: (i, c), c=c))

    def prev(c):
        return pl.BlockSpec((CONV_HALO, D_MIX), functools.partial(lambda i, c: (jnp.maximum(i * per - 1, 0), c), c=c))

    nxt = pl.BlockSpec((CONV_HALO, D_MIX), lambda i: (jnp.minimum((i + 1) * per, n_halo - 1), 0))
    const = lambda a: pl.BlockSpec(a.shape, lambda i: (0, 0))
    out = pl.BlockSpec((ts, D_MIX), lambda i: (i, 0))
    vec = pl.BlockSpec((1, D_MIX), lambda i: (0, 0))
    return pl.pallas_call(
        body, name="conv_bwd", grid=(n_tiles,),
        in_specs=[cur(cb_val), cur(cb_gate), prev(cb_val), prev(cb_gate), out, out, nxt, nxt, const(w_pad), const(ln_g), const(ln_b)],
        out_specs=[out, out, pl.BlockSpec((CONV_HALO, D_MIX), lambda i: (0, 0)), vec, vec, vec],
        out_shape=[jax.ShapeDtypeStruct((s, D_MIX), BF16), jax.ShapeDtypeStruct((s, D_MIX), BF16),
                   jax.ShapeDtypeStruct((CONV_HALO, D_MIX), F32)] + [jax.ShapeDtypeStruct((1, D_MIX), F32)] * 3,
        scratch_shapes=[pltpu.VMEM((CONV_HALO + ts, D_MIX), F32), pltpu.VMEM((ts + CONV_HALO, D_MIX), F32)],
        compiler_params=_params(("arbitrary",)),
    )(proj, proj, proj, proj, zc, dyb, zc, dyb, w_pad, ln_g, ln_b)


def forget_cumsum(proj, cb_f, bf_exp, t=256):
    s = proj.shape[0]
    t = min(t, s)

    def body(f_ref, bf_ref, out_ref, carry_ref):
        @pl.when(pl.program_id(0) == 0)
        def _():
            carry_ref[...] = jnp.zeros(carry_ref.shape, F32)

        lf = _log_sigmoid(f_ref[...] + bf_ref[...])
        tri = (lax.broadcasted_iota(jnp.int32, (t, t), 1) <= lax.broadcasted_iota(jnp.int32, (t, t), 0)).astype(F32)
        c = jnp.dot(tri, lf, precision=HIGHEST, preferred_element_type=F32) + carry_ref[...]
        out_ref[...] = c
        carry_ref[...] = c[t - 1:t, :]

    return pl.pallas_call(
        body, name="forget_cumsum", grid=(s // t,),
        in_specs=[pl.BlockSpec((t, D_MIX), functools.partial(lambda i, c: (i, c), c=cb_f)), pl.BlockSpec((1, D_MIX), lambda i: (0, 0))],
        out_specs=pl.BlockSpec((t, D_MIX), lambda i: (i, 0)),
        out_shape=jax.ShapeDtypeStruct((s, D_MIX), F32),
        scratch_shapes=[pltpu.VMEM((1, D_MIX), F32)],
        compiler_params=_params(("arbitrary",)),
    )(proj, bf_exp)


def forget_bwd(proj, cb_f, bf_exp, dcum, t=256):
    s = proj.shape[0]
    t = min(t, s)
    n = s // t

    def body(f_ref, bf_ref, dc_ref, df_ref, dbf_ref, carry_ref):
        @pl.when(pl.program_id(0) == 0)
        def _():
            carry_ref[...] = jnp.zeros(carry_ref.shape, F32)
            dbf_ref[...] = jnp.zeros(dbf_ref.shape, F32)

        tri = (lax.broadcasted_iota(jnp.int32, (t, t), 1) >= lax.broadcasted_iota(jnp.int32, (t, t), 0)).astype(F32)
        r = jnp.dot(tri, dc_ref[...], precision=HIGHEST, preferred_element_type=F32) + carry_ref[...]
        carry_ref[...] = r[0:1, :]
        df = r * _sigmoid(-(f_ref[...] + bf_ref[...]))
        dbf_ref[...] += jnp.sum(df, axis=0, keepdims=True)
        live = lax.broadcasted_iota(jnp.int32, (t, D_MIX), 1) % HEAD_DIM == 0
        df_ref[...] = jnp.where(live, df, 0.0).astype(BF16)

    return pl.pallas_call(
        body, name="forget_bwd", grid=(n,),
        in_specs=[pl.BlockSpec((t, D_MIX), functools.partial(lambda i, c: (n - 1 - i, c), c=cb_f)), pl.BlockSpec((1, D_MIX), lambda i: (0, 0)),
                  pl.BlockSpec((t, D_MIX), lambda i: (n - 1 - i, 0))],
        out_specs=[pl.BlockSpec((t, D_MIX), lambda i: (n - 1 - i, 0)), pl.BlockSpec((1, D_MIX), lambda i: (0, 0))],
        out_shape=[jax.ShapeDtypeStruct((s, D_MIX), BF16), jax.ShapeDtypeStruct((1, D_MIX), F32)],
        scratch_shapes=[pltpu.VMEM((1, D_MIX), F32)],
        compiler_params=_params(("arbitrary",)),
    )(proj, bf_exp, dcum)


NT = (((1,), (1,)), ((), ()))
LOG2E = math.log2(math.e)
N_PAIR = D_MIX // LANES


def _split3(x):
    hi = x.astype(BF16).astype(F32)
    mid = (x - hi).astype(BF16).astype(F32)
    return hi, mid, x - hi - mid


def _triple(li, first, vals):
    out = jnp.where(li == first, vals[0], 0.0)
    for i in (1, 2):
        out = jnp.where(li == first + i, vals[i], out)
    return out


def _lane_ids(shape):
    lane = lax.broadcasted_iota(jnp.int32, shape, 1)
    return lane, lane % HEAD_DIM, lane < HEAD_DIM


def attn_prep(proj, cb_q, cum, ts=512):
    s = proj.shape[0]
    ts = min(ts, s)
    scale = LOG2E / math.sqrt(HEAD_DIM)

    def body(q_ref, k_ref, v_ref, c_ref, qe_ref, qo_ref, ke_ref, ko_ref, ve_ref, vo_ref):
        _, li, lo = _lane_ids((ts, LANES))
        one3 = lambda first: ((li >= first) & (li < first + 3)).astype(F32)
        for p in range(N_PAIR):
            ps = slice(p * LANES, (p + 1) * LANES)
            c3 = _split3(pltpu.roll(c_ref[:, ps] * LOG2E, HEAD_DIM, axis=1))
            eq = _triple(li, 0, c3) + one3(3)
            ek = one3(0) - _triple(li, 3, c3) + one3(6)
            ev = one3(0)
            for src, even, odd, extra, mul in ((q_ref, qe_ref, qo_ref, eq, scale), (k_ref, ke_ref, ko_ref, ek, 1.0), (v_ref, ve_ref, vo_ref, ev, 1.0)):
                x = src[:, ps].astype(F32) * mul
                even[:, ps] = jnp.where(lo, x, extra).astype(BF16)
                odd[:, ps] = jnp.where(lo, extra, x).astype(BF16)

    col = lambda c: pl.BlockSpec((ts, D_MIX), functools.partial(lambda i, c: (i, c), c=c))
    out = pl.BlockSpec((ts, D_MIX), lambda i: (i, 0))
    return pl.pallas_call(
        body, name="attn_prep", grid=(s // ts,),
        in_specs=[col(cb_q), col(cb_q + 1), col(cb_q + 2), pl.BlockSpec((ts, D_MIX), lambda i: (i, 0))],
        out_specs=[out] * 6, out_shape=[jax.ShapeDtypeStruct((s, D_MIX), BF16)] * 6,
        compiler_params=_params(("parallel",)),
    )(proj, proj, proj, cum)


def _pair_specs(s, t):
    return pl.BlockSpec((t, LANES), lambda p, i: (i, p)), pl.BlockSpec((s, LANES), lambda p, i: (0, p))


def attn_fwd(qe, qo, ke, ko, ve, vo, tq=512):
    s = qe.shape[0]
    tq = min(tq, s)

    def body(qe_ref, qo_ref, ke_ref, ko_ref, ve_ref, vo_ref, o_ref, qbe_ref, qbo_ref):
        qi = pl.program_id(1)
        qs, k_refs, v_refs = (qe_ref[...], qo_ref[...]), (ke_ref, ko_ref), (ve_ref, vo_ref)
        causal = lax.broadcasted_iota(jnp.int32, (tq, tq), 1) <= lax.broadcasted_iota(jnp.int32, (tq, tq), 0)

        def step(j, carry, diag):
            ks = pl.multiple_of(j * tq, tq)
            new = []
            for h in range(2):
                m, l, acc = carry[h]
                sc = lax.dot_general(qs[h], k_refs[h][pl.ds(ks, tq), :], NT, preferred_element_type=F32)
                if diag:
                    sc = jnp.where(causal, sc, -jnp.inf)
                m_new = jnp.maximum(m, jnp.max(sc, axis=1, keepdims=True))
                p = jnp.exp2(sc - m_new)
                alpha = jnp.exp2(m - m_new)
                l = alpha * l + jnp.sum(p, axis=1, keepdims=True)
                acc = alpha * acc + jnp.dot(p.astype(BF16), v_refs[h][pl.ds(ks, tq), :], preferred_element_type=F32)
                new.append((m_new, l, acc))
            return tuple(new)

        init = tuple((jnp.full((tq, 1), -jnp.inf, F32), jnp.zeros((tq, 1), F32), jnp.zeros((tq, LANES), F32)) for _ in range(2))
        carry = lax.fori_loop(0, qi, lambda j, c: step(j, c, False), init)
        (m0, l0, a0), (m1, l1, a1) = step(qi, carry, True)
        _, li, lo = _lane_ids((tq, LANES))
        o_ref[...] = jnp.where(lo, a0 / l0, a1 / l1).astype(o_ref.dtype)
        lse_lanes = (li >= 6) & (li < 9)
        for q, m, l, spare, out_ref in ((qs[0], m0, l0, ~lo, qbe_ref), (qs[1], m1, l1, lo, qbo_ref)):
            neg_lse = _triple(li, 6, _split3(-(m + jnp.log(l) * LOG2E)))
            out_ref[...] = jnp.where(spare & lse_lanes, neg_lse.astype(BF16), q)

    blk, full = _pair_specs(s, tq)
    return pl.pallas_call(
        body, name="attn_fwd", grid=(N_PAIR, s // tq),
        in_specs=[blk, blk, full, full, full, full],
        out_specs=[blk] * 3, out_shape=[jax.ShapeDtypeStruct((s, D_MIX), BF16)] * 3,
        compiler_params=_params(("parallel", "parallel")),
    )(qe, qo, ke, ko, ve, vo)


def attn_dq(qbe, qbo, ke, ko, ve, vo, do, dep, tq=512):
    s = qbe.shape[0]
    tq = min(tq, s)
    scale = 1.0 / math.sqrt(HEAD_DIM)

    def body(qe_ref, qo_ref, ke_ref, ko_ref, ve_ref, vo_ref, do_ref, dep_ref, dq_ref, dobe_ref, dobo_ref):
        qi = pl.program_id(1)
        _, li, lo = _lane_ids((tq, LANES))
        do_ = do_ref[...]
        qs, k_refs, v_refs = (qe_ref[...], qo_ref[...]), (ke_ref, ko_ref), (ve_ref, vo_ref)
        dos = (jnp.where(lo, do_, 0), jnp.where(lo, 0, do_))
        causal = lax.broadcasted_iota(jnp.int32, (tq, tq), 1) <= lax.broadcasted_iota(jnp.int32, (tq, tq), 0)

        def step(j, carry, diag):
            ks = pl.multiple_of(j * tq, tq)
            new = []
            for h in range(2):
                pdpk, pk, dsum = carry[h]
                kb = k_refs[h][pl.ds(ks, tq), :]
                sc = lax.dot_general(qs[h], kb, NT, preferred_element_type=F32)
                if diag:
                    sc = jnp.where(causal, sc, -jnp.inf)
                p = jnp.exp2(sc)
                pdp = p * lax.dot_general(dos[h], v_refs[h][pl.ds(ks, tq), :], NT, preferred_element_type=F32)
                new.append((pdpk + jnp.dot(pdp.astype(BF16), kb, preferred_element_type=F32),
                            pk + jnp.dot(p.astype(BF16), kb, preferred_element_type=F32),
                            dsum + jnp.sum(pdp, axis=1, keepdims=True)))
            return tuple(new)

        init = tuple((jnp.zeros((tq, LANES), F32), jnp.zeros((tq, LANES), F32), jnp.zeros((tq, 1), F32)) for _ in range(2))
        carry = lax.fori_loop(0, qi, lambda j, c: step(j, c, False), init)
        (a0, b0, s0), (a1, b1, s1) = step(qi, carry, True)
        dq_ref[...] = (jnp.where(lo, a0 - s0 * b0, a1 - s1 * b1) * scale).astype(dq_ref.dtype)
        dobe_ref[...] = jnp.where(lo, do_, _triple(li, 0, _split3(-s0)).astype(BF16))
        dobo_ref[...] = jnp.where(lo, _triple(li, 0, _split3(-s1)).astype(BF16), do_)

    blk, full = _pair_specs(s, tq)
    return pl.pallas_call(
        body, name="attn_dq", grid=(N_PAIR, s // tq),
        in_specs=[blk, blk, full, full, full, full, blk, pl.BlockSpec(dep.shape, lambda p, i: (0, 0))],
        out_specs=[blk] * 3, out_shape=[jax.ShapeDtypeStruct((s, D_MIX), BF16)] * 3,
        compiler_params=_params(("parallel", "parallel")),
    )(qbe, qbo, ke, ko, ve, vo, do, dep)


def attn_dkv(ke, ko, ve, vo, qbe, qbo, dobe, dobo, tk=512):
    s = ke.shape[0]
    tk = min(tk, s)
    nq = s // tk

    def body(ke_ref, ko_ref, ve_ref, vo_ref, qe_ref, qo_ref, de_ref, do_ref, dk_ref, dv_ref, dck_ref):
        kj = pl.program_id(1)
        lo = _lane_lt64((tk, LANES))
        ks_, vs_, q_refs, d_refs = (ke_ref[...], ko_ref[...]), (ve_ref[...], vo_ref[...]), (qe_ref, qo_ref), (de_ref, do_ref)
        causal = lax.broadcasted_iota(jnp.int32, (tk, tk), 0) <= lax.broadcasted_iota(jnp.int32, (tk, tk), 1)

        def step(i, carry, diag):
            qs = pl.multiple_of(i * tk, tk)
            new = []
            for h in range(2):
                dk, dv, dck = carry[h]
                qblk = q_refs[h][pl.ds(qs, tk), :]
                dblk = d_refs[h][pl.ds(qs, tk), :]
                st = lax.dot_general(ks_[h], qblk, NT, preferred_element_type=F32)
                if diag:
                    st = jnp.where(causal, st, -jnp.inf)
                pt = jnp.exp2(st)
                dst = pt * lax.dot_general(vs_[h], dblk, NT, preferred_element_type=F32)
                new.append((dk + jnp.dot(dst.astype(BF16), qblk, preferred_element_type=F32),
                            dv + jnp.dot(pt.astype(BF16), dblk, preferred_element_type=F32),
                            dck - jnp.sum(dst, axis=1, keepdims=True)))
            return tuple(new)

        init = tuple((jnp.zeros((tk, LANES), F32), jnp.zeros((tk, LANES), F32), jnp.zeros((tk, 1), F32)) for _ in range(2))
        carry = step(kj, init, True)
        (dk0, dv0, dc0), (dk1, dv1, dc1) = lax.fori_loop(kj + 1, nq, lambda i, c: step(i, c, False), carry)
        dk_ref[...] = (jnp.where(lo, dk0, dk1) * (1.0 / LOG2E)).astype(dk_ref.dtype)
        dv_ref[...] = jnp.where(lo, dv0, dv1).astype(dv_ref.dtype)
        dck_ref[...] = jnp.where(lo, dc0, dc1)

    blk, full = _pair_specs(s, tk)
    return pl.pallas_call(
        body, name="attn_dkv", grid=(N_PAIR, nq),
        in_specs=[blk, blk, blk, blk, full, full, full, full],
        out_specs=[blk] * 3,
        out_shape=[jax.ShapeDtypeStruct((s, D_MIX), BF16), jax.ShapeDtypeStruct((s, D_MIX), BF16), jax.ShapeDtypeStruct((s, D_MIX), F32)],
        compiler_params=_params(("parallel", "parallel")),
    )(ke, ko, ve, vo, qbe, qbo, dobe, dobo)


def _pre_bwd(name, x, g, sc, sh, dh, dres):
    d = x.shape[1]

    def fn(xv, dhv, dresv, gv, scv, shv):
        _, vjp = jax.vjp(lambda *a: _f_pre(*a)[0], xv, gv, scv, shv)
        dx, dg, dsc, dsh = vjp(dhv.astype(F32))
        return dx + dresv, dg, dsc, dsh

    return rowwise(name, fn, [(x, 0, d), (dh, 0, d), (dres, 0, d)], [g, sc, sh], [(d, F32)], accs=[(1, d)] * 3)


def layer_fwd(x, mod, layer):
    s, d = x.shape
    m = D_MIX
    w = dict(layer["small"])
    sh1, sc1, gt1, sh2, sc2, gt2 = (mod[i:i + 1] for i in range(6))
    cb = 3 * d // m
    (h,) = rowwise("pre1", _f_pre, [(x, 0, d)], [w["mix_pre_g"], sc1, sh1], [(d, BF16)])
    w.update(layer["get_in"](h))
    proj = matmul("w_in", h, w["w_in"], out_dtypes=(BF16,))
    fproj = matmul("w_in_forget", h, w["w_in"], b_cols=((cb + 7) * m, m))
    w.update(layer["get_abco"](proj))
    u, vln = rowwise("gmlp_in", _f_a1, [(proj, cb, m), (proj, cb + 1, m)], [w["gmlp_ln_g"], w["gmlp_ln_b"]], [(m, F32), (m, BF16)])
    sv, ya = spatial_fwd(vln, u, w["ws"], w["bs_exp"])
    y_a = matmul("w_a", ya, w["w_a_out"], out_dtypes=(BF16,))
    zc, yb = conv_fwd(proj, cb + 2, cb + 3, w["conv_w"], w["conv_b"], w["conv_ln_g"], w["conv_ln_b"])
    y_b = matmul("w_b", yb, w["w_b_out"], out_dtypes=(BF16,))
    cum = forget_cumsum(fproj, 0, w["bf_exp"])
    kv_ops = attn_prep(proj, cb + 4, cum)
    o, qbe, qbo = attn_fwd(*kv_ops)
    att = (qbe, qbo) + tuple(kv_ops[2:])
    y_c = matmul("w_c", o, w["w_c_out"], out_dtypes=(BF16,))
    (merged,) = rowwise("merge", _f_merge, [(proj, 0, d), (proj, 1, d), (proj, 2, d), (y_a, 0, d), (y_b, 0, d), (y_c, 0, d)], [], [(d, BF16)])
    y = matmul("w_out", merged, w["w_out"])
    w.update(layer["get_mlp"](y))

    def post_pre(xv, yv, gp, gt, g2, sc, sh):
        x1 = xv + _f_post(yv, gp, gt)[0]
        return x1, _f_pre(x1, g2, sc, sh)[0]

    x1, h2 = rowwise("post1", post_pre, [(x, 0, d), (y, 0, d)], [w["mix_post_g"], gt1, w["mlp_pre_g"], sc2, sh2], [(d, F32), (d, BF16)])
    r = matmul("w1", h2, w["mlp_w1"], out_dtypes=(BF16,), epilogue=lambda acc: (jnp.square(jnp.maximum(acc, 0.0)),))
    y2 = matmul("w2", r, w["mlp_w2"])
    (x2,) = rowwise("post2", lambda xv, yv, g, gt: xv + _f_post(yv, g, gt)[0], [(x1, 0, d), (y2, 0, d)], [w["mlp_post_g"], gt2], [(d, F32)])
    saved = dict(w=w, x=x, h=h, proj=proj, fproj=fproj, u=u, vln=vln, sv=sv, ya=ya, y_a=y_a, zc=zc, yb=yb, y_b=y_b, att=att,
                 o=o, y_c=y_c, merged=merged, y=y, x1=x1, h2=h2, r=r, y2=y2)
    return x2, saved


def layer_bwd(dx2, mod, sv, emit, tok_in=None):
    x, proj, w = sv["x"], sv["proj"], sv["w"]
    s, d = x.shape
    m = D_MIX
    sh1, sc1, gt1, sh2, sc2, gt2 = (mod[i:i + 1] for i in range(6))
    cb = 3 * d // m
    g = {}
    if tok_in is not None:
        gt2 = gt2 + tok_in[0:1, 0:1]
    dy2, g["mlp_post_g"], dgt2 = rowwise_vjp("post2_b", _f_post, [(sv["y2"], 0, d)], [w["mlp_post_g"], gt2], [(dx2, 0, d)], [BF16])
    da = matmul("w2_dx", dy2, w["mlp_w2"], tb=True, out_dtypes=(BF16,),
                epilogue=lambda acc, r: (acc * (2.0 * jnp.sqrt(r.astype(F32))),), epi=[(sv["r"], 0)])
    big = {}
    big["mlp_w2"] = matmul("w2_dw", sv["r"], dy2, ta=True, out_dtypes=(BF16,))
    dh2 = matmul("w1_dx", da, w["mlp_w1"], tb=True, out_dtypes=(BF16,))
    big["mlp_w1"] = matmul("w1_dw", sv["h2"], da, ta=True, out_dtypes=(BF16,))
    tok = emit("mlp", big)
    dx1, g["mlp_pre_g"], dsc2, dsh2 = _pre_bwd("pre2_b", sv["x1"], w["mlp_pre_g"], sc2 + tok[0:1, 0:1], sh2, dh2, dx2)
    dy, g["mix_post_g"], dgt1 = rowwise_vjp("post1_b", _f_post, [(sv["y"], 0, d)], [w["mix_post_g"], gt1], [(dx1, 0, d)], [BF16])
    dmerged = matmul("w_out_dx", dy, w["w_out"], tb=True, out_dtypes=(BF16,))
    big = {}
    big["w_out"] = matmul("w_out_dw", sv["merged"], dy, ta=True, out_dtypes=(BF16,))
    dg0, dg1, dg2, dya_, dyb_, dyc_ = rowwise_vjp(
        "merge_b", _f_merge, [(proj, 0, d), (proj, 1, d), (proj, 2, d), (sv["y_a"], 0, d), (sv["y_b"], 0, d), (sv["y_c"], 0, d)], [],
        [(dmerged, 0, d)], [BF16] * 6)
    dya_pre = matmul("w_a_dx", dya_, w["w_a_out"], tb=True, out_dtypes=(BF16,))
    big["w_a_out"] = matmul("w_a_dw", sv["ya"], dya_, ta=True, out_dtypes=(BF16,))
    dyb_pre = matmul("w_b_dx", dyb_, w["w_b_out"], tb=True, out_dtypes=(BF16,))
    big["w_b_out"] = matmul("w_b_dw", sv["yb"], dyb_, ta=True, out_dtypes=(BF16,))
    do = matmul("w_c_dx", dyc_, w["w_c_out"], tb=True, out_dtypes=(BF16,))
    big["w_c_out"] = matmul("w_c_dw", sv["o"], dyc_, ta=True, out_dtypes=(BF16,))
    tok = emit("abco", big)
    qbe, qbo, ke, ko, ve, vo = sv["att"]
    dq, dobe, dobo = attn_dq(qbe, qbo, ke, ko, ve, vo, do, tok)
    dk, dv, dcum = attn_dkv(ke, ko, ve, vo, qbe, qbo, dobe, dobo)
    df, dbf = forget_bwd(sv["fproj"], 0, w["bf_exp"], dcum)
    g["fox_bf"] = dbf[0, ::HEAD_DIM]
    dval, dgate, dwc, g["conv_b"], g["conv_ln_g"], g["conv_ln_b"] = conv_bwd(
        proj, cb + 2, cb + 3, sv["zc"], dyb_pre, w["conv_w"], w["conv_ln_g"], w["conv_ln_b"])
    g["conv_w"] = dwc[:CONV_WIDTH]
    du, dvln, dws, dbexp = spatial_bwd(dya_pre, sv["u"], sv["sv"], sv["vln"], w["ws_t"])
    g["gmlp_ws"] = dws * jnp.tril(jnp.ones((CHUNK, CHUNK), F32))
    g["gmlp_bs"] = dbexp.reshape(CHUNK, m // GROUP_DIM, GROUP_DIM).sum(-1).T
    du_raw, dv_raw, g["gmlp_ln_g"], g["gmlp_ln_b"] = rowwise_vjp(
        "gmlp_in_b", _f_a1, [(proj, cb, m), (proj, cb + 1, m)], [w["gmlp_ln_g"], w["gmlp_ln_b"]], [(du, 0, m), (dvln, 0, m)], [BF16, BF16])
    dproj = jnp.concatenate([dg0, dg1, dg2, du_raw, dv_raw, dval, dgate, dq, dk, dv, df], axis=1)
    tok = emit("in", {"w_in": matmul("w_in_dw", sv["h"], dproj, ta=True, out_dtypes=(BF16,))})
    dh = matmul("w_in_dx", dproj, w["w_in"], tb=True, dep=tok, out_dtypes=(BF16,))
    dx, g["mix_pre_g"], dsc1, dsh1 = _pre_bwd("pre1_b", x, w["mix_pre_g"], sc1, sh1, dh, dx1)
    dmod = jnp.concatenate([dsh1, dsc1, dgt1, dsh2, dsc2, dgt2], axis=0)
    return dx, dmod, g


def local_step(x, target, mods, layers):
    d = x.shape[1]
    saved = []
    for l in range(len(layers)):
        x, sv = layer_fwd(x, mods[l], layers[l])
        saved.append(sv)

    def loss_fn(xv, tv):
        err = xv - tv
        return err * (1.0 / d), jnp.sum(err * err, axis=0, keepdims=True)

    dx, sq = rowwise("loss", loss_fn, [(x, 0, d), (target, 0, d)], [], [(d, F32)], accs=[(1, d)])
    loss = (0.5 / d) * jnp.sum(sq)
    dmods, grads = [None] * len(layers), [None] * len(layers)
    tok = None
    for l in reversed(range(len(layers))):
        dx, dmods[l], grads[l] = layer_bwd(dx, mods[l], saved[l], layers[l]["emit"], tok)
        tok = layers[l]["emit_small"](dmods[l], grads[l])
    return loss, dx, dmods, grads


def exchange(name, arrs, scatter):
    n = len(arrs)

    def body(*refs):
        in_refs, out_refs = refs[:n], refs[n:2 * n]
        send_sems, recv_sems, local_sems = refs[2 * n:]
        x, y, c = lax.axis_index("x"), lax.axis_index("y"), lax.axis_index("c")
        me = 4 * x + 2 * y + c
        local = []
        for a in range(n):
            src = in_refs[a].at[me] if scatter else in_refs[a]
            cp = pltpu.make_async_copy(src, out_refs[a].at[me], local_sems.at[a])
            cp.start()
            local.append(cp)
        remote = []
        for k in range(1, N_DEV):
            px, py, pc = x ^ ((k >> 2) & 1), y ^ ((k >> 1) & 1), c ^ (k & 1)
            peer = 4 * px + 2 * py + pc
            for a in range(n):
                src = in_refs[a].at[peer] if scatter else in_refs[a]
                cp = pltpu.make_async_remote_copy(
                    src_ref=src, dst_ref=out_refs[a].at[me], send_sem=send_sems.at[a * (N_DEV - 1) + k - 1],
                    recv_sem=recv_sems.at[a * (N_DEV - 1) + k - 1], device_id=(px, py, pc), device_id_type=MESH)
                cp.start()
                remote.append(cp)
        for cp in remote:
            cp.wait()
        for cp in local:
            cp.wait()

    hbm = pl.BlockSpec(memory_space=pltpu.HBM)
    out_shape = [jax.ShapeDtypeStruct(a.shape if scatter else (N_DEV,) + a.shape, a.dtype) for a in arrs]
    return pl.pallas_call(
        body, name=name, in_specs=[hbm] * n, out_specs=[hbm] * n, out_shape=out_shape,
        scratch_shapes=[pltpu.SemaphoreType.DMA((n * (N_DEV - 1),)), pltpu.SemaphoreType.DMA((n * (N_DEV - 1),)),
                        pltpu.SemaphoreType.DMA((n,))],
    )(*arrs)


def _peers(x, y, c):
    out = []
    for k in range(1, N_DEV):
        px, py, pc = x ^ ((k >> 2) & 1), y ^ ((k >> 1) & 1), c ^ (k & 1)
        out.append((k - 1, (px, py, pc), 4 * px + 2 * py + pc))
    return out


def _exchange_copies(srcs, lands, send_sems, recv_sems, scatter):
    x, y, c = lax.axis_index("x"), lax.axis_index("y"), lax.axis_index("c")
    me = 4 * x + 2 * y + c
    copies = []
    for slot, pos, peer in _peers(x, y, c):
        for a, (src, land) in enumerate(zip(srcs, lands)):
            copies.append(pltpu.make_async_remote_copy(
                src_ref=src.at[peer] if scatter else src, dst_ref=land.at[me],
                send_sem=send_sems.at[a * (N_DEV - 1) + slot], recv_sem=recv_sems.at[a * (N_DEV - 1) + slot],
                device_id=pos, device_id_type=MESH))
    return me, copies


def exchange_start(name, arrs, scatter, after=None):
    n = len(arrs)
    lands = [lax.empty(a.shape if scatter else (N_DEV,) + a.shape, a.dtype) for a in arrs]
    n_in = 2 * n + (after is not None)

    def body(*refs):
        srcs, lands_ = refs[:n], refs[n:2 * n]
        send_sems, recv_sems = refs[n_in], refs[n_in + 1]
        token = refs[n_in + 2 + 2 * n]
        _, copies = _exchange_copies(srcs, lands_, send_sems, recv_sems, scatter)
        for cp in copies:
            cp.start()
        token[...] = jnp.zeros(token.shape, token.dtype)

    hbm = pl.BlockSpec(memory_space=pltpu.HBM)
    sem = pl.BlockSpec(memory_space=pltpu.SEMAPHORE)
    n_sem = n * (N_DEV - 1)
    res = pl.pallas_call(
        body, name=name,
        out_shape=(pltpu.SemaphoreType.DMA((n_sem,)), pltpu.SemaphoreType.DMA((n_sem,)),
                   *[pltpu.HBM(a.shape, a.dtype) for a in arrs], *[pltpu.HBM(l.shape, l.dtype) for l in lands],
                   jax.ShapeDtypeStruct((8, LANES), F32)),
        in_specs=[hbm] * (2 * n) + ([] if after is None else [pl.BlockSpec(memory_space=pl.ANY)]),
        out_specs=(sem, sem, *([hbm] * (2 * n)), pl.BlockSpec(memory_space=pltpu.VMEM)),
        input_output_aliases={i: 2 + i for i in range(2 * n)},
        compiler_params=pltpu.CompilerParams(has_side_effects=pltpu.SideEffectType.DATAFLOW_SIDE_EFFECTING),
    )(*[pltpu.with_memory_space_constraint(a, pltpu.HBM) for a in arrs],
      *[pltpu.with_memory_space_constraint(l, pltpu.HBM) for l in lands], *([] if after is None else [after]))
    return dict(n=n, scatter=scatter, send=res[0], recv=res[1], srcs=res[2:2 + n], lands=res[2 + n:2 + 2 * n], token=res[2 + 2 * n])


def exchange_wait(name, st, after):
    n, scatter = st["n"], st["scatter"]
    after = list(after) if isinstance(after, (list, tuple)) else [after]

    def body(*refs):
        srcs, lands_ = refs[:n], refs[n:2 * n]
        send_sems, recv_sems = refs[2 * n], refs[2 * n + 1]
        _, copies = _exchange_copies(srcs, lands_, send_sems, recv_sems, scatter)
        for cp in copies:
            cp.wait_send()
            cp.wait_recv()

    hbm = pl.BlockSpec(memory_space=pltpu.HBM)
    sem = pl.BlockSpec(memory_space=pltpu.SEMAPHORE)
    res = pl.pallas_call(
        body, name=name,
        out_shape=tuple(pltpu.HBM(a.shape, a.dtype) for a in (*st["srcs"], *st["lands"])),
        in_specs=[hbm] * (2 * n) + [sem, sem] + [pl.BlockSpec(memory_space=pl.ANY)] * len(after), out_specs=tuple([hbm] * (2 * n)),
        input_output_aliases={i: i for i in range(2 * n)},
        compiler_params=pltpu.CompilerParams(has_side_effects=pltpu.SideEffectType.DATAFLOW_SIDE_EFFECTING),
    )(*st["srcs"], *st["lands"], st["send"], st["recv"], *after)
    return list(res[:n]), list(res[n:])


def adamw_sum(name, parts, w, m, v, tr=256):
    nl = len(parts)
    k, r, c = parts[0].shape
    tr = _tile(r, tr, 16)
    c1 = 1.0 - ADAM_B1 ** ADAM_STEP
    c2 = 1.0 - ADAM_B2 ** ADAM_STEP

    def body(*refs):
        p_refs = refs[:nl]
        w_ref, m_ref, v_ref, g_ref, d_ref, nm_ref, nv_ref = refs[nl:]
        for l in range(nl):
            @pl.when(pl.program_id(0) == l)
            def _(p_ref=p_refs[l]):
                grad = p_ref[0].astype(F32)
                for j in range(1, k):
                    grad = grad + p_ref[j].astype(F32)
                new_m = ADAM_B1 * m_ref[...] + (1.0 - ADAM_B1) * grad
                new_v = ADAM_B2 * v_ref[...] + (1.0 - ADAM_B2) * (grad * grad)
                m_hat = new_m / c1
                v_hat = new_v / c2
                g_ref[...] = grad
                d_ref[...] = -ADAM_LR * (m_hat / (jnp.sqrt(v_hat) + ADAM_EPS) + ADAM_WD * w_ref[...])
                nm_ref[...] = new_m
                nv_ref[...] = new_v

    part = lambda l: pl.BlockSpec((k, tr, c), functools.partial(lambda ll, i, l: (0, jnp.where(ll == l, i, 0), 0), l=l))
    blk = pl.BlockSpec((None, tr, c), lambda ll, i: (ll, i, 0))
    return pl.pallas_call(
        body, name=name, grid=(nl, r // tr),
        in_specs=[part(l) for l in range(nl)] + [blk, blk, blk],
        out_specs=[blk] * 4, out_shape=[jax.ShapeDtypeStruct((nl, r, c), F32)] * 4,
        compiler_params=_params(("parallel", "parallel")),
    )(*parts, w, m, v)


def adamw_sum_cols(name, parts, w, m, v, tc=256):
    nl = len(parts)
    k, r, c = parts[0].shape
    tc = _tile(c, tc)
    c1 = 1.0 - ADAM_B1 ** ADAM_STEP
    c2 = 1.0 - ADAM_B2 ** ADAM_STEP

    def body(*refs):
        p_refs = refs[:nl]
        w_ref, m_ref, v_ref, g_ref, d_ref, nm_ref, nv_ref = refs[nl:]
        for l in range(nl):
            @pl.when(pl.program_id(0) == l)
            def _(p_ref=p_refs[l]):
                grad = p_ref[0].astype(F32)
                for j in range(1, k):
                    grad = grad + p_ref[j].astype(F32)
                new_m = ADAM_B1 * m_ref[...] + (1.0 - ADAM_B1) * grad
                new_v = ADAM_B2 * v_ref[...] + (1.0 - ADAM_B2) * (grad * grad)
                m_hat = new_m / c1
                v_hat = new_v / c2
                g_ref[...] = grad
                d_ref[...] = -ADAM_LR * (m_hat / (jnp.sqrt(v_hat) + ADAM_EPS) + ADAM_WD * w_ref[...])
                nm_ref[...] = new_m
                nv_ref[...] = new_v

    part = lambda l: pl.BlockSpec((k, r, tc), functools.partial(lambda ll, j, l: (0, 0, jnp.where(ll == l, j, 0)), l=l))
    blk = pl.BlockSpec((r, tc), lambda ll, j: (0, ll * (c // tc) + j))
    return pl.pallas_call(
        body, name=name, grid=(nl, c // tc),
        in_specs=[part(l) for l in range(nl)] + [blk, blk, blk],
        out_specs=[blk] * 4, out_shape=[jax.ShapeDtypeStruct((r, nl * c), F32)] * 4,
        compiler_params=_params(("parallel", "parallel")),
    )(*parts, w, m, v)


def ada_fwd(c_all, ada_w):
    nl, d, n = ada_w.shape

    def body(c_ref, w_ref, o_ref):
        o_ref[...] = jnp.dot(_silu(c_ref[...]), w_ref[...], precision=HIGHEST, preferred_element_type=F32)

    return pl.pallas_call(
        body, name="ada_fwd", grid=(nl,),
        in_specs=[pl.BlockSpec((N_DEV, d), lambda l: (0, 0)), pl.BlockSpec((None, d, n), lambda l: (l, 0, 0))],
        out_specs=pl.BlockSpec((None, N_DEV, n), lambda l: (l, 0, 0)),
        out_shape=jax.ShapeDtypeStruct((nl, N_DEV, n), F32),
        compiler_params=_params(("parallel",)),
    )(c_all, ada_w)


def ada_bwd(c_all_t, dmod, td=256):
    d = c_all_t.shape[0]
    nl, _, n = dmod.shape
    td = _tile(d, td, 8)

    def body(c_ref, dm_ref, o_ref):
        ca = _silu(c_ref[...])
        acc = ca[:, 0:1] * dm_ref[0:1, :]
        for b in range(1, N_DEV):
            acc = acc + ca[:, b:b + 1] * dm_ref[b:b + 1, :]
        o_ref[...] = acc

    return pl.pallas_call(
        body, name="ada_bwd", grid=(nl, d // td),
        in_specs=[pl.BlockSpec((td, N_DEV), lambda l, i: (i, 0)), pl.BlockSpec((None, N_DEV, n), lambda l, i: (l, 0, 0))],
        out_specs=pl.BlockSpec((None, td, n), lambda l, i: (l, i, 0)),
        out_shape=jax.ShapeDtypeStruct((nl, d, n), F32),
        compiler_params=_params(("parallel", "parallel")),
    )(c_all_t, dmod)


ARG_NAMES = ["x", "c", "ada_w", "ada_b", "mix_pre_g", "mix_post_g", "mlp_pre_g", "mlp_post_g", "w_in", "gmlp_ln_g", "gmlp_ln_b",
             "gmlp_ws", "gmlp_bs", "w_a_out", "conv_w", "conv_b", "conv_ln_g", "conv_ln_b", "w_b_out", "fox_bf", "w_c_out",
             "w_out", "mlp_w1", "mlp_w2"]
WEIGHTS = ARG_NAMES[2:]
COL_SHARDED = ["w_in", "w_a_out", "w_b_out", "w_c_out", "mlp_w1"]
ROW_SHARDED = ["w_out", "mlp_w2"]
BIG = COL_SHARDED + ROW_SHARDED
GROUPS = {"in": ["w_in"], "abco": ["w_a_out", "w_b_out", "w_c_out", "w_out"], "mlp": ["mlp_w1", "mlp_w2"]}
SMALL = ["ada_b", "mix_pre_g", "mix_post_g", "mlp_pre_g", "mlp_post_g", "gmlp_ln_g", "gmlp_ln_b", "gmlp_ws", "gmlp_bs",
         "conv_b", "conv_ln_g", "conv_ln_b", "fox_bf"]
PACK_COLS = 512


def _to_my_layout(w_in, d):
    m = D_MIX
    nf = 7 * m
    return jnp.concatenate([w_in[..., nf + N_HEADS:], w_in[..., :nf], jnp.repeat(w_in[..., nf:nf + N_HEADS], HEAD_DIM, axis=-1)], axis=-1)


def _from_my_layout(gw, d):
    m = D_MIX
    return jnp.concatenate([gw[..., 3 * d:3 * d + 7 * m], gw[..., 3 * d + 7 * m::HEAD_DIM], gw[..., :3 * d]], axis=-1)


def _ref_ranges(lo, hi, shard):
    out = []
    while lo < hi:
        j = lo // shard
        end = min(hi, (j + 1) * shard)
        out.append((j, lo - j * shard, end - j * shard))
        lo = end
    return out


def _w_in_to_kernel_layout(g, d):
    m = D_MIX
    nf = 7 * m
    shard = g.shape[2]
    cols = lambda lo, hi: [g[j, :, a:b] for j, a, b in _ref_ranges(lo, hi, shard)]
    forget = jnp.concatenate(cols(nf, nf + N_HEADS), axis=1)
    return jnp.concatenate(cols(nf + N_HEADS, nf + N_HEADS + 3 * d) + cols(0, nf) + [jnp.repeat(forget, HEAD_DIM, axis=1)], axis=1)


def _w_in_grad_blocks(gw, d):
    m = D_MIX
    nf = 7 * m
    n_ref = nf + N_HEADS + 3 * d
    shard = n_ref // N_DEV
    segs = [(0, nf, 3 * d, 1), (nf, nf + N_HEADS, 3 * d + nf, HEAD_DIM), (nf + N_HEADS, n_ref, 0, 1)]
    blocks = []
    for j in range(N_DEV):
        lo, hi = j * shard, (j + 1) * shard
        pieces = []
        for r0, r1, k0, stride in segs:
            a, b = max(lo, r0), min(hi, r1)
            if a < b:
                pieces.append(gw[:, k0 + (a - r0) * stride:k0 + (b - r0) * stride:stride])
        blocks.append(jnp.concatenate(pieces, axis=1) if len(pieces) > 1 else pieces[0])
    return jnp.stack(blocks)


def _pack(parts):
    flat = jnp.concatenate([p.reshape(-1).astype(F32) for p in parts])
    pad = (-flat.shape[0]) % (PACK_COLS * 8)
    return jnp.pad(flat, (0, pad)).reshape(-1, PACK_COLS)


def _unpack(packed, shapes):
    nl = packed.shape[0]
    flat, out, off = packed.reshape(nl, -1), [], 0
    for shp in shapes:
        n = math.prod(shp)
        out.append(flat[:, off:off + n].reshape((nl,) + tuple(shp)))
        off += n
    return out


def _layer_small(p, conv_full, l):
    wl = {}
    for k in ["mix_pre_g", "mix_post_g", "mlp_pre_g", "mlp_post_g", "gmlp_ln_g", "gmlp_ln_b", "conv_b", "conv_ln_g", "conv_ln_b"]:
        wl[k] = p[k][l][None, :]
    wm = p["gmlp_ws"][l] * jnp.tril(jnp.ones((CHUNK, CHUNK), F32))
    wl["ws"] = wm.astype(BF16)
    wl["ws_t"] = jnp.transpose(wm, (0, 2, 1)).astype(BF16)
    wl["bs_exp"] = jnp.repeat(p["gmlp_bs"][l].T, GROUP_DIM, axis=1)
    wl["bf_exp"] = jnp.repeat(p["fox_bf"][l], HEAD_DIM)[None, :]
    wl["conv_w"] = jnp.pad(conv_full[l], ((0, CONV_HALO - CONV_WIDTH), (0, 0)))
    return wl


def kernel(x, c, ada_w, ada_b, mix_pre_g, mix_post_g, mlp_pre_g, mlp_post_g, w_in, gmlp_ln_g, gmlp_ln_b, gmlp_ws, gmlp_bs, w_a_out, conv_w, conv_b, conv_ln_g, conv_ln_b, w_b_out, fox_bf, w_c_out, w_out, mlp_w1, mlp_w2, loss_target, m_ada_w, m_ada_b, m_mix_pre_g, m_mix_post_g, m_mlp_pre_g, m_mlp_post_g, m_w_in, m_gmlp_ln_g, m_gmlp_ln_b, m_gmlp_ws, m_gmlp_bs, m_w_a_out, m_conv_w, m_conv_b, m_conv_ln_g, m_conv_ln_b, m_w_b_out, m_fox_bf, m_w_c_out, m_w_out, m_mlp_w1, m_mlp_w2, v_ada_w, v_ada_b, v_mix_pre_g, v_mix_post_g, v_mlp_pre_g, v_mlp_post_g, v_w_in, v_gmlp_ln_g, v_gmlp_ln_b, v_gmlp_ws, v_gmlp_bs, v_w_a_out, v_conv_w, v_conv_b, v_conv_ln_g, v_conv_ln_b, v_w_b_out, v_fox_bf, v_w_c_out, v_w_out, v_mlp_w1, v_mlp_w2):
    args = (x, c, ada_w, ada_b, mix_pre_g, mix_post_g, mlp_pre_g, mlp_post_g, w_in, gmlp_ln_g, gmlp_ln_b, gmlp_ws, gmlp_bs, w_a_out,
            conv_w, conv_b, conv_ln_g, conv_ln_b, w_b_out, fox_bf, w_c_out, w_out, mlp_w1, mlp_w2)
    ms = (m_ada_w, m_ada_b, m_mix_pre_g, m_mix_post_g, m_mlp_pre_g, m_mlp_post_g, m_w_in, m_gmlp_ln_g, m_gmlp_ln_b, m_gmlp_ws, m_gmlp_bs,
          m_w_a_out, m_conv_w, m_conv_b, m_conv_ln_g, m_conv_ln_b, m_w_b_out, m_fox_bf, m_w_c_out, m_w_out, m_mlp_w1, m_mlp_w2)
    vs = (v_ada_w, v_ada_b, v_mix_pre_g, v_mix_post_g, v_mlp_pre_g, v_mlp_post_g, v_w_in, v_gmlp_ln_g, v_gmlp_ln_b, v_gmlp_ws, v_gmlp_bs,
          v_w_a_out, v_conv_w, v_conv_b, v_conv_ln_g, v_conv_ln_b, v_w_b_out, v_fox_bf, v_w_c_out, v_w_out, v_mlp_w1, v_mlp_w2)
    p = dict(zip(ARG_NAMES, args))
    mom = dict(zip(WEIGHTS, ms))
    var = dict(zip(WEIGHTS, vs))
    nl = ada_w.shape[0]
    s, d = x.shape[1], x.shape[2]
    me = 4 * lax.axis_index("x") + 2 * lax.axis_index("y") + lax.axis_index("c")

    c_all, conv_all = exchange("gather_c", [c, conv_w], scatter=False)
    c_all = c_all.reshape(N_DEV, d)
    n_ada = ada_w.shape[2]
    mod_parts = ada_fwd(c_all, ada_w)
    (mod_recv,) = exchange("scatter_mod", [jnp.transpose(mod_parts, (1, 0, 2))], scatter=True)
    conv_full = jnp.transpose(conv_all, (1, 2, 0, 3)).reshape(nl, CONV_WIDTH, D_MIX)

    def full_matrix(k, land, own):
        g = lax.dynamic_update_index_in_dim(land, own, me, 0)
        r, cc = own.shape
        if k == "w_in":
            return _w_in_to_kernel_layout(g, d)
        return jnp.transpose(g, (1, 0, 2)).reshape(r, N_DEV * cc) if k in COL_SHARDED else g.reshape(N_DEV * r, cc)

    started = [mod_recv]

    def fetch(l, tag):
        keys = GROUPS[tag]
        st = exchange_start(f"gather_{tag}{l}_start", [p[k][l].astype(BF16) for k in keys], scatter=False, after=started[-1])
        started.append(st["token"])

        def get(after):
            owns, lands = exchange_wait(f"gather_{tag}{l}_wait", st, after)
            return {k: full_matrix(k, land, o) for k, land, o in zip(keys, lands, owns)}

        return get

    getters = [{tag: fetch(l, tag) for tag in GROUPS} for l in range(nl)]
    mod = jnp.transpose(mod_recv, (1, 0, 2)).reshape(nl, N_DEV * n_ada) + ada_b + started[-1][0:1, 0:1]
    mods = [mod[l].reshape(6, d) for l in range(nl)]

    sent = {}

    def emitter(l):
        def emit(tag, grads_big):
            keys = GROUPS[tag]
            send = []
            for k in keys:
                gk = grads_big[k]
                if k == "w_in":
                    gk = jnp.transpose(_w_in_grad_blocks(gk, d), (0, 2, 1))
                elif k in COL_SHARDED:
                    r, cc = gk.shape[0], gk.shape[1] // N_DEV
                    gk = jnp.transpose(gk.reshape(r, N_DEV, cc), (1, 0, 2))
                else:
                    gk = gk.reshape(N_DEV, gk.shape[0] // N_DEV, gk.shape[1])
                send.append(gk.astype(BF16))
            sent[(l, tag)] = (keys, exchange_start(f"scatter_{tag}{l}_start", send, scatter=True))
            return sent[(l, tag)][1]["token"]

        return emit

    small_sent = {}

    def small_emitter(l):
        def emit_small(dmod_l, g):
            packed = _pack([dmod_l] + [g[k] for k in SMALL[1:]] + [g["conv_w"]])
            small_sent[l] = exchange_start(f"gather_small{l}_start", [packed], scatter=False)
            return small_sent[l]["token"]

        return emit_small

    layers = [dict(small=_layer_small(p, conv_full, l), emit_small=small_emitter(l), get_in=getters[l]["in"], get_abco=getters[l]["abco"], get_mlp=getters[l]["mlp"],
                   emit=emitter(l)) for l in range(nl)]
    loss_local, dx, dmods, grads = local_step(x[0], loss_target[0], mods, layers)
    loss = lax.psum(loss_local, ("x", "y", "c"))
    grad_x = dx[None]

    out = {k: [None] * 4 for k in WEIGHTS}

    def shard_update(name, parts, k):
        shp = p[k].shape
        flat = lambda a: a.reshape(nl, -1, shp[-1])
        res = adamw_sum(name, [pt.reshape((pt.shape[0],) + flat(p[k]).shape[1:]) for pt in parts], flat(p[k]), flat(mom[k]), flat(var[k]))
        out[k] = [a.reshape(shp) for a in res]

    parts = {}

    def collect(tag, after):
        for l in reversed(range(nl)):
            keys, st = sent[(l, tag)]
            sends, lands = exchange_wait(f"scatter_{tag}{l}_wait", st, after)
            for k, land, sd in zip(keys, lands, sends):
                own = lax.dynamic_index_in_dim(sd, me, 0, keepdims=False)
                parts[(k, l)] = lax.dynamic_update_index_in_dim(land, own, me, 0)

    behind_bwd = [dx] + [small_sent[l]["token"] for l in range(nl)]
    collect("mlp", behind_bwd)
    collect("abco", behind_bwd)
    for k in BIG[1:]:
        shard_update("adamw_" + k, [parts[(k, l)] for l in range(nl)], k)
    collect("in", [out[k][0] for k in BIG[1:]])
    n_in = w_in.shape[2]
    res_in = adamw_sum_cols("adamw_w_in", [parts[("w_in", l)] for l in range(nl)],
                            *[jnp.transpose(src["w_in"], (2, 0, 1)).reshape(n_in, nl * d) for src in (p, mom, var)])
    out["w_in"] = [jnp.transpose(a.reshape(n_in, nl, d), (1, 2, 0)) for a in res_in]

    small_shapes = [p[k].shape[1:] for k in SMALL] + [(CONV_WIDTH, D_MIX)]
    small_all = []
    for l in range(nl):
        srcs, lands = exchange_wait(f"gather_small{l}_wait", small_sent[l], [out[k][0] for k in BIG])
        small_all.append(lax.dynamic_update_index_in_dim(lands[0], srcs[0], me, 0))
    zeros_conv = jnp.zeros(small_shapes[-1], F32)
    packs = [jnp.stack([_pack([src[k][l] for k in SMALL] + [zeros_conv]) for l in range(nl)]) for src in (p, mom, var)]
    small_out = [_unpack(o, small_shapes) for o in adamw_sum("adamw_small", small_all, *packs)]
    for i, k in enumerate(SMALL):
        for j in range(4):
            out[k][j] = small_out[j][i]

    n_conv = conv_w.shape[2]
    conv_grad = lax.dynamic_slice_in_dim(small_out[0][-1], me * n_conv, n_conv, axis=2)
    shard_update("adamw_conv_w", [conv_grad[l][None] for l in range(nl)], "conv_w")

    dmod_all = jnp.stack([small_all[l].reshape(N_DEV, -1)[:, :6 * d] for l in range(nl)])
    dmod_mine = lax.dynamic_slice_in_dim(dmod_all, me * n_ada, n_ada, axis=2)
    g_ada = ada_bwd(c_all.T, dmod_mine)
    shard_update("adamw_ada_w", [g_ada[l][None] for l in range(nl)], "ada_w")

    res = [loss, grad_x]
    for j in range(4):
        res += [out[k][j] for k in WEIGHTS]
    return tuple(res)
```

```python
import functools
import math

import jax
import jax.numpy as jnp
from jax import lax
from jax.experimental import pallas as pl
from jax.experimental.pallas import tpu as pltpu

F32 = jnp.float32
BF16 = jnp.bfloat16
MESH = pl.DeviceIdType.MESH
N_DEV = 8
NORM_EPS = 1e-6
D_MIX = 512
N_HEADS = 8
HEAD_DIM = 64
GROUP_DIM = 64
CHUNK = 128
CONV_WIDTH = 31
CONV_HALO = 32
LANES = 128
ADAM_LR, ADAM_B1, ADAM_B2, ADAM_EPS, ADAM_WD, ADAM_STEP = 0.001, 0.9, 0.999, 1e-08, 0.01, 10
VMEM_LIMIT = 56 * 1024 * 1024
HIGHEST = lax.Precision.HIGHEST


def _tile(dim, pref, mult=LANES):
    t = min(pref, dim)
    t -= t % mult
    while t >= mult:
        if dim % t == 0:
            return t
        t -= mult
    return dim


def _params(sem):
    return pltpu.CompilerParams(dimension_semantics=sem, vmem_limit_bytes=VMEM_LIMIT)


def rowwise(name, fn, rows, consts, outs, accs=(), ts=512, into=None):
    s = rows[0][0].shape[0]
    ts = min(ts, s)
    nr, nc, no, na = len(rows), len(consts), len(outs), len(accs)
    n_in = nr + nc + (into is not None)

    def body(*refs):
        vals = [r[...].astype(F32) for r in refs[:nr]] + [r[...] for r in refs[nr:nr + nc]]
        res = fn(*vals)
        if not isinstance(res, (tuple, list)):
            res = (res,)
        for r, v in zip(refs[n_in:n_in + no], res[:no]):
            r[...] = v.astype(r.dtype)
        if na:
            acc_refs = refs[n_in + no:]

            @pl.when(pl.program_id(0) == 0)
            def _():
                for r in acc_refs:
                    r[...] = jnp.zeros(r.shape, r.dtype)

            for r, v in zip(acc_refs, res[no:]):
                r[...] += v.astype(F32)

    in_specs = [pl.BlockSpec((ts, w), functools.partial(lambda i, cb: (i, cb), cb=cb)) for (_, cb, w) in rows]
    in_specs += [pl.BlockSpec(c.shape, lambda i: (0, 0)) for c in consts]
    out_specs = [pl.BlockSpec((ts, w), lambda i: (i, 0)) for (w, _) in outs]
    out_specs += [pl.BlockSpec(shp, lambda i: (0, 0)) for shp in accs]
    out_shape = [jax.ShapeDtypeStruct((s, w), dt) for (w, dt) in outs]
    out_shape += [jax.ShapeDtypeStruct(shp, F32) for shp in accs]
    extra, aliases = [], {}
    if into is not None:
        buf, cb_into = into
        assert buf.dtype == outs[0][1] and buf.shape[0] == s
        in_specs.append(pl.BlockSpec(memory_space=pl.ANY))
        out_specs[0] = pl.BlockSpec((ts, outs[0][0]), lambda i: (i, cb_into))
        out_shape[0] = jax.ShapeDtypeStruct(buf.shape, buf.dtype)
        extra, aliases = [buf], {nr + nc: 0}
    res = pl.pallas_call(
        body, name=name, grid=(s // ts,), in_specs=in_specs, out_specs=out_specs, out_shape=out_shape,
        input_output_aliases=aliases, compiler_params=_params(("arbitrary",) if na else ("parallel",)),
    )(*[a for (a, _, _) in rows], *consts, *extra)
    return res


def rowwise_vjp(name, f, rows, consts, cts, grad_dtypes, ts=512, into=None):
    nr, nc, nt = len(rows), len(consts), len(cts)
    keep = [i for i, dt in enumerate(grad_dtypes) if dt is not None]
    k_into = 0 if into is None else into[2]

    def g(*vals):
        rv = [v.astype(F32) for v in vals[:nr]]
        ctv = tuple(v.astype(F32) for v in vals[nr:nr + nt])
        cv = list(vals[nr + nt:])
        _, vjp = jax.vjp(lambda *a: tuple(f(*a)), *rv, *cv)
        grads = vjp(ctv)
        row_grads = [grads[i] for i in keep]
        if k_into:
            row_grads = [jnp.concatenate(row_grads[:k_into], axis=1)] + row_grads[k_into:]
        return tuple(row_grads) + tuple(grads[nr:])

    outs = [(rows[i][2], grad_dtypes[i]) for i in keep]
    if k_into:
        assert len({dt for _, dt in outs[:k_into]}) == 1
        outs = [(sum(w for w, _ in outs[:k_into]), outs[0][1])] + outs[k_into:]
    return rowwise(name, g, list(rows) + list(cts), consts, outs, accs=[c.shape for c in consts], ts=ts,
                   into=None if into is None else into[:2])


def matmul(name, a, b, *, ta=False, tb=False, out_dtypes=(F32,), epilogue=None, epi=(), tm=None, tn=None, tk=4096, dep=None, b_cols=None):
    m, k = (a.shape[1], a.shape[0]) if ta else a.shape
    n = b.shape[0] if tb else b.shape[1]
    b_col0 = 0
    if b_cols is not None:
        assert not tb
        b_col0, n = b_cols
    assert (b.shape[1] if tb else b.shape[0]) == k
    tk = _tile(k, tk)
    if tk > 1024:
        tm, tn = _tile(m, tm or 1024), _tile(n, tn or 1024)
    else:
        tm, tn = _tile(m, tm or 2048), _tile(n, tn or (1024 if m >= 2048 else 2048))
    assert b_col0 % tn == 0
    jb = b_col0 // tn
    nk = k // tk
    ne, no = len(epi), len(out_dtypes)
    dims = (((0 if ta else 1,), (1 if tb else 0,)), ((), ()))

    def body(*refs):
        a_ref, b_ref = refs[0], refs[1]
        epi_refs = refs[2:2 + ne]
        n_in = 2 + ne + (dep is not None)
        out_refs = refs[n_in:n_in + no]
        part = lax.dot_general(a_ref[...].astype(BF16), b_ref[...].astype(BF16), dims, preferred_element_type=F32)

        def finish(acc):
            res = (acc,) if epilogue is None else epilogue(acc, *[r[...] for r in epi_refs])
            for r, v in zip(out_refs, res):
                r[...] = v.astype(r.dtype)

        if nk == 1:
            finish(part)
        else:
            acc_ref = refs[-1]
            kk = pl.program_id(2)

            @pl.when(kk == 0)
            def _():
                acc_ref[...] = part

            @pl.when(kk > 0)
            def _():
                acc_ref[...] += part

            @pl.when(kk == nk - 1)
            def _():
                finish(acc_ref[...])

    a_spec = pl.BlockSpec((tk, tm), lambda i, j, kk: (kk, i)) if ta else pl.BlockSpec((tm, tk), lambda i, j, kk: (i, kk))
    b_spec = pl.BlockSpec((tn, tk), lambda i, j, kk: (j, kk)) if tb else pl.BlockSpec((tk, tn), lambda i, j, kk: (kk, j + jb))
    epi_specs = []
    for (arr, col0) in epi:
        assert col0 % tn == 0
        epi_specs.append(pl.BlockSpec((tm, tn), functools.partial(lambda i, j, kk, c0: (i, j + c0), c0=col0 // tn)))
    res = pl.pallas_call(
        body, name=name, grid=(m // tm, n // tn, nk),
        in_specs=[a_spec, b_spec] + epi_specs + ([] if dep is None else [pl.BlockSpec(dep.shape, lambda i, j, kk: (0, 0))]),
        out_specs=[pl.BlockSpec((tm, tn), lambda i, j, kk: (i, j)) for _ in out_dtypes],
        out_shape=[jax.ShapeDtypeStruct((m, n), dt) for dt in out_dtypes],
        scratch_shapes=[pltpu.VMEM((tm, tn), F32)] if nk > 1 else [],
        compiler_params=_params(("parallel", "parallel", "arbitrary")),
    )(a, b, *[arr for (arr, _) in epi], *([] if dep is None else [dep]))
    return res[0] if no == 1 else res


def _rms(x, g):
    return x * lax.rsqrt(jnp.mean(x * x, axis=-1, keepdims=True) + NORM_EPS) * g


def _ln(x, g, b):
    mu = jnp.mean(x, axis=-1, keepdims=True)
    xc = x - mu
    var = jnp.mean(xc * xc, axis=-1, keepdims=True)
    return xc * lax.rsqrt(var + NORM_EPS) * g + b


def _gelu(x):
    return 0.5 * x * (1.0 + jnp.tanh(math.sqrt(2.0 / math.pi) * (x + 0.044715 * (x * x * x))))


def _sigmoid(x):
    return 1.0 / (1.0 + jnp.exp(-x))


def _silu(x):
    return x * _sigmoid(x)


def _log_sigmoid(x):
    return jnp.minimum(x, 0.0) - jnp.log(1.0 + jnp.exp(-jnp.abs(x)))


def _f_pre(x, g, sc, sh):
    return (_rms(x, g) * (1.0 + sc) + sh,)


def _f_post(y, g, gt):
    return (gt * _rms(y, g),)


def _f_a1(u_raw, v_raw, g, b):
    return _gelu(u_raw), _ln(_gelu(v_raw), g, b)


def _f_glu(val, gate):
    return (val * _sigmoid(gate),)


def _f_lnsilu(zc, g, b):
    return (_silu(_ln(zc, g, b)),)


def _f_merge(g0, g1, g2, ya, yb, yc):
    return (_sigmoid(g0) * ya + _sigmoid(g1) * yb + _sigmoid(g2) * yc,)


def _lane_lt64(shape):
    return lax.broadcasted_iota(jnp.int32, shape, 1) < HEAD_DIM


def spatial_fwd(vln, u, w_bf, b_exp, rows_per_step=512):
    s = vln.shape[0]
    tr = min(rows_per_step, s)

    def body(v_ref, u_ref, w_ref, b_ref, sv_ref, ya_ref):
        lo = _lane_lt64((CHUNK, LANES))
        for ch in range(tr // CHUNK):
            r0 = ch * CHUNK
            for p in range(D_MIX // LANES):
                vp = v_ref[r0:r0 + CHUNK, p * LANES:(p + 1) * LANES]
                o0 = jnp.dot(w_ref[2 * p], vp, preferred_element_type=F32)
                o1 = jnp.dot(w_ref[2 * p + 1], vp, preferred_element_type=F32)
                sv = jnp.where(lo, o0, o1) + b_ref[:, p * LANES:(p + 1) * LANES]
                sv_ref[r0:r0 + CHUNK, p * LANES:(p + 1) * LANES] = sv
                ya_ref[r0:r0 + CHUNK, p * LANES:(p + 1) * LANES] = (
                    u_ref[r0:r0 + CHUNK, p * LANES:(p + 1) * LANES] * sv).astype(BF16)

    row = pl.BlockSpec((tr, D_MIX), lambda i: (i, 0))
    return pl.pallas_call(
        body, name="spatial_fwd", grid=(s // tr,),
        in_specs=[row, row, pl.BlockSpec(w_bf.shape, lambda i: (0, 0, 0)), pl.BlockSpec(b_exp.shape, lambda i: (0, 0))],
        out_specs=[row, row],
        out_shape=[jax.ShapeDtypeStruct((s, D_MIX), F32), jax.ShapeDtypeStruct((s, D_MIX), BF16)],
        compiler_params=_params(("parallel",)),
    )(vln, u, w_bf, b_exp)


def spatial_bwd(dya, u, sv, vln, wt_bf, rows_per_step=512):
    s = vln.shape[0]
    tr = min(rows_per_step, s)
    ng = wt_bf.shape[0]

    def body(dya_ref, u_ref, sv_ref, v_ref, wt_ref, du_ref, dv_ref, dw_ref, db_ref):
        @pl.when(pl.program_id(0) == 0)
        def _():
            dw_ref[...] = jnp.zeros(dw_ref.shape, F32)
            db_ref[...] = jnp.zeros(db_ref.shape, F32)

        lo = _lane_lt64((CHUNK, LANES))
        for ch in range(tr // CHUNK):
            r0 = ch * CHUNK
            for p in range(D_MIX // LANES):
                cs = slice(p * LANES, (p + 1) * LANES)
                dya_p = dya_ref[r0:r0 + CHUNK, cs].astype(F32)
                du_ref[r0:r0 + CHUNK, cs] = dya_p * sv_ref[r0:r0 + CHUNK, cs]
                dsv = dya_p * u_ref[r0:r0 + CHUNK, cs]
                db_ref[:, cs] += dsv
                dsv0 = jnp.where(lo, dsv, 0.0).astype(BF16)
                dsv1 = jnp.where(lo, 0.0, dsv).astype(BF16)
                vp = v_ref[r0:r0 + CHUNK, cs]
                d0 = jnp.dot(wt_ref[2 * p], dsv0, preferred_element_type=F32)
                d1 = jnp.dot(wt_ref[2 * p + 1], dsv1, preferred_element_type=F32)
                dv_ref[r0:r0 + CHUNK, cs] = d0 + d1
                nt = (((1,), (1,)), ((), ()))
                dw_ref[2 * p] += lax.dot_general(dsv0, vp, nt, preferred_element_type=F32)
                dw_ref[2 * p + 1] += lax.dot_general(dsv1, vp, nt, preferred_element_type=F32)

    row = pl.BlockSpec((tr, D_MIX), lambda i: (i, 0))
    return pl.pallas_call(
        body, name="spatial_bwd", grid=(s // tr,),
        in_specs=[row, row, row, row, pl.BlockSpec(wt_bf.shape, lambda i: (0, 0, 0))],
        out_specs=[row, row, pl.BlockSpec((ng, CHUNK, CHUNK), lambda i: (0, 0, 0)), pl.BlockSpec((CHUNK, D_MIX), lambda i: (0, 0))],
        out_shape=[jax.ShapeDtypeStruct((s, D_MIX), F32), jax.ShapeDtypeStruct((s, D_MIX), F32),
                   jax.ShapeDtypeStruct((ng, CHUNK, CHUNK), F32), jax.ShapeDtypeStruct((CHUNK, D_MIX), F32)],
        compiler_params=_params(("arbitrary",)),
    )(dya, u, sv, vln, wt_bf)


def _windows(ref, first, count, ts):
    for r in range(8):
        ks = [k for k in range(count) if (first + k) % 8 == r]
        if ks:
            base = first + ks[0]
            blk = ref[base:base + ks[-1] - ks[0] + ts, :]
            for k in ks:
                yield k, blk[k - ks[0]:k - ks[0] + ts, :]


def conv_fwd(proj, cb_val, cb_gate, w_pad, cb, ln_g, ln_b, ts=256):
    s = proj.shape[0]
    ts = min(ts, s)
    per = ts // CONV_HALO

    def body(val_ref, gate_ref, pval_ref, pgate_ref, w_ref, cb_ref, g_ref, b_ref, zc_ref, yb_ref, ext_ref):
        i = pl.program_id(0)
        zprev = pval_ref[...].astype(F32) * _sigmoid(pgate_ref[...].astype(F32))
        ext_ref[0:CONV_HALO, :] = jnp.where(i > 0, zprev, 0.0)
        ext_ref[CONV_HALO:, :] = val_ref[...].astype(F32) * _sigmoid(gate_ref[...].astype(F32))
        acc = jnp.zeros((ts, D_MIX), F32)
        for j, win in _windows(ext_ref, CONV_HALO - (CONV_WIDTH - 1), CONV_WIDTH, ts):
            acc = acc + w_ref[j:j + 1, :] * win
        zc = acc + cb_ref[...]
        zc_ref[...] = zc
        yb_ref[...] = _f_lnsilu(zc, g_ref[...], b_ref[...])[0].astype(BF16)

    def cur(c):
        return pl.BlockSpec((ts, D_MIX), functools.partial(lambda i, c: (i, c), c=c))

    def prev(c):
        return pl.BlockSpec((CONV_HALO, D_MIX), functools.partial(lambda i, c: (jnp.maximum(i * per - 1, 0), c), c=c))

    const = lambda a: pl.BlockSpec(a.shape, lambda i: (0, 0))
    out = pl.BlockSpec((ts, D_MIX), lambda i: (i, 0))
    return pl.pallas_call(
        body, name="conv_fwd", grid=(s // ts,),
        in_specs=[cur(cb_val), cur(cb_gate), prev(cb_val), prev(cb_gate), const(w_pad), const(cb), const(ln_g), const(ln_b)],
        out_specs=[out, out],
        out_shape=[jax.ShapeDtypeStruct((s, D_MIX), F32), jax.ShapeDtypeStruct((s, D_MIX), BF16)],
        scratch_shapes=[pltpu.VMEM((CONV_HALO + ts, D_MIX), F32)],
        compiler_params=_params(("parallel",)),
    )(proj, proj, proj, proj, w_pad, cb, ln_g, ln_b)


def conv_bwd(proj, cb_val, cb_gate, zc, dyb, w_pad, ln_g, ln_b, dproj, ts=256):
    assert cb_gate == cb_val + 1 and cb_val % 2 == 0
    s = proj.shape[0]
    ts = min(ts, s)
    per = ts // CONV_HALO
    n_tiles = s // ts
    n_halo = s // CONV_HALO

    def body(val_ref, gate_ref, pval_ref, pgate_ref, zc_ref, dyb_ref, nzc_ref, ndyb_ref, w_ref, g_ref, b_ref, dproj_in,
             dvg_ref, dw_ref, dcb_ref, dg_ref, db_ref, zext_ref, dext_ref):
        i = pl.program_id(0)

        @pl.when(i == 0)
        def _():
            for r in (dw_ref, dcb_ref, dg_ref, db_ref):
                r[...] = jnp.zeros(r.shape, F32)

        g, b = g_ref[...], b_ref[...]
        _, vjp = jax.vjp(lambda z, gg, bb: _f_lnsilu(z, gg, bb)[0], zc_ref[...], g, b)
        dzc, dg, db = vjp(dyb_ref[...].astype(F32))
        dg_ref[...] += dg
        db_ref[...] += db
        dcb_ref[...] += jnp.sum(dzc, axis=0, keepdims=True)
        _, vjp_n = jax.vjp(lambda z: _f_lnsilu(z, g, b)[0], nzc_ref[...])
        (dzc_next,) = vjp_n(ndyb_ref[...].astype(F32))
        dext_ref[0:ts, :] = dzc
        dext_ref[ts:, :] = jnp.where(i < n_tiles - 1, dzc_next, 0.0)
        val, gate = val_ref[...].astype(F32), gate_ref[...].astype(F32)
        zprev = pval_ref[...].astype(F32) * _sigmoid(pgate_ref[...].astype(F32))
        zext_ref[0:CONV_HALO, :] = jnp.where(i > 0, zprev, 0.0)
        zext_ref[CONV_HALO:, :] = val * _sigmoid(gate)
        dz = jnp.zeros((ts, D_MIX), F32)
        for shift, win in _windows(dext_ref, 0, CONV_WIDTH, ts):
            j = CONV_WIDTH - 1 - shift
            dz = dz + w_ref[j:j + 1, :] * win
        for j, win in _windows(zext_ref, CONV_HALO - (CONV_WIDTH - 1), CONV_WIDTH, ts):
            dw_ref[j:j + 1, :] += jnp.sum(dzc * win, axis=0, keepdims=True)
        _, vjp_glu = jax.vjp(lambda a, c: _f_glu(a, c)[0], val, gate)
        dval, dgate = vjp_glu(dz)
        dvg_ref[:, :D_MIX] = dval.astype(BF16)
        dvg_ref[:, D_MIX:] = dgate.astype(BF16)

    def cur(c):
        return pl.BlockSpec((ts, D_MIX), functools.partial(lambda i, c: (i, c), c=c))

    def prev(c):
        return pl.BlockSpec((CONV_HALO, D_MIX), functools.partial(lambda i, c: (jnp.maximum(i * per - 1, 0), c), c=c))

    nxt = pl.BlockSpec((CONV_HALO, D_MIX), lambda i: (jnp.minimum((i + 1) * per, n_halo - 1), 0))
    const = lambda a: pl.BlockSpec(a.shape, lambda i: (0, 0))
    out = pl.BlockSpec((ts, D_MIX), lambda i: (i, 0))
    vec = pl.BlockSpec((1, D_MIX), lambda i: (0, 0))
    return pl.pallas_call(
        body, name="conv_bwd", grid=(n_tiles,),
        in_specs=[cur(cb_val), cur(cb_gate), prev(cb_val), prev(cb_gate), out, out, nxt, nxt, const(w_pad), const(ln_g), const(ln_b),
                  pl.BlockSpec(memory_space=pl.ANY)],
        out_specs=[pl.BlockSpec((ts, 2 * D_MIX), lambda i: (i, cb_val // 2)), pl.BlockSpec((CONV_HALO, D_MIX), lambda i: (0, 0)), vec, vec, vec],
        out_shape=[jax.ShapeDtypeStruct(dproj.shape, dproj.dtype),
                   jax.ShapeDtypeStruct((CONV_HALO, D_MIX), F32)] + [jax.ShapeDtypeStruct((1, D_MIX), F32)] * 3,
        scratch_shapes=[pltpu.VMEM((CONV_HALO + ts, D_MIX), F32), pltpu.VMEM((ts + CONV_HALO, D_MIX), F32)],
        input_output_aliases={11: 0}, compiler_params=_params(("arbitrary",)),
    )(proj, proj, proj, proj, zc, dyb, zc, dyb, w_pad, ln_g, ln_b, dproj)


def forget_cumsum(proj, cb_f, bf_exp, t=256):
    s = proj.shape[0]
    t = min(t, s)

    def body(f_ref, bf_ref, out_ref, carry_ref):
        @pl.when(pl.program_id(0) == 0)
        def _():
            carry_ref[...] = jnp.zeros(carry_ref.shape, F32)

        lf = _log_sigmoid(f_ref[...] + bf_ref[...])
        tri = (lax.broadcasted_iota(jnp.int32, (t, t), 1) <= lax.broadcasted_iota(jnp.int32, (t, t), 0)).astype(F32)
        c = jnp.dot(tri, lf, precision=HIGHEST, preferred_element_type=F32) + carry_ref[...]
        out_ref[...] = c
        carry_ref[...] = c[t - 1:t, :]

    return pl.pallas_call(
        body, name="forget_cumsum", grid=(s // t,),
        in_specs=[pl.BlockSpec((t, D_MIX), functools.partial(lambda i, c: (i, c), c=cb_f)), pl.BlockSpec((1, D_MIX), lambda i: (0, 0))],
        out_specs=pl.BlockSpec((t, D_MIX), lambda i: (i, 0)),
        out_shape=jax.ShapeDtypeStruct((s, D_MIX), F32),
        scratch_shapes=[pltpu.VMEM((1, D_MIX), F32)],
        compiler_params=_params(("arbitrary",)),
    )(proj, bf_exp)


def forget_bwd(proj, cb_f, bf_exp, dcum, dproj, cb_out, t=256):
    s = proj.shape[0]
    t = min(t, s)
    n = s // t

    def body(f_ref, bf_ref, dc_ref, dproj_in, df_ref, dbf_ref, carry_ref):
        @pl.when(pl.program_id(0) == 0)
        def _():
            carry_ref[...] = jnp.zeros(carry_ref.shape, F32)
            dbf_ref[...] = jnp.zeros(dbf_ref.shape, F32)

        tri = (lax.broadcasted_iota(jnp.int32, (t, t), 1) >= lax.broadcasted_iota(jnp.int32, (t, t), 0)).astype(F32)
        r = jnp.dot(tri, dc_ref[...], precision=HIGHEST, preferred_element_type=F32) + carry_ref[...]
        carry_ref[...] = r[0:1, :]
        df = r * _sigmoid(-(f_ref[...] + bf_ref[...]))
        dbf_ref[...] += jnp.sum(df, axis=0, keepdims=True)
        live = lax.broadcasted_iota(jnp.int32, (t, D_MIX), 1) % HEAD_DIM == 0
        df_ref[...] = jnp.where(live, df, 0.0).astype(BF16)

    return pl.pallas_call(
        body, name="forget_bwd", grid=(n,),
        in_specs=[pl.BlockSpec((t, D_MIX), functools.partial(lambda i, c: (n - 1 - i, c), c=cb_f)), pl.BlockSpec((1, D_MIX), lambda i: (0, 0)),
                  pl.BlockSpec((t, D_MIX), lambda i: (n - 1 - i, 0)), pl.BlockSpec(memory_space=pl.ANY)],
        out_specs=[pl.BlockSpec((t, D_MIX), lambda i: (n - 1 - i, cb_out)), pl.BlockSpec((1, D_MIX), lambda i: (0, 0))],
        out_shape=[jax.ShapeDtypeStruct(dproj.shape, dproj.dtype), jax.ShapeDtypeStruct((1, D_MIX), F32)],
        scratch_shapes=[pltpu.VMEM((1, D_MIX), F32)],
        input_output_aliases={3: 0}, compiler_params=_params(("arbitrary",)),
    )(proj, bf_exp, dcum, dproj)


NT = (((1,), (1,)), ((), ()))
LOG2E = math.log2(math.e)
N_PAIR = D_MIX // LANES


def _split3(x):
    hi = x.astype(BF16).astype(F32)
    mid = (x - hi).astype(BF16).astype(F32)
    return hi, mid, x - hi - mid


def _triple(li, first, vals):
    out = jnp.where(li == first, vals[0], 0.0)
    for i in (1, 2):
        out = jnp.where(li == first + i, vals[i], out)
    return out


def _lane_ids(shape):
    lane = lax.broadcasted_iota(jnp.int32, shape, 1)
    return lane, lane % HEAD_DIM, lane < HEAD_DIM


def attn_prep(proj, cb_q, cum, ts=512):
    s = proj.shape[0]
    ts = min(ts, s)
    scale = LOG2E / math.sqrt(HEAD_DIM)

    def body(q_ref, k_ref, v_ref, c_ref, qe_ref, qo_ref, ke_ref, ko_ref, ve_ref, vo_ref):
        _, li, lo = _lane_ids((ts, LANES))
        one3 = lambda first: ((li >= first) & (li < first + 3)).astype(F32)
        for p in range(N_PAIR):
            ps = slice(p * LANES, (p + 1) * LANES)
            c3 = _split3(pltpu.roll(c_ref[:, ps] * LOG2E, HEAD_DIM, axis=1))
            eq = _triple(li, 0, c3) + one3(3)
            ek = one3(0) - _triple(li, 3, c3) + one3(6)
            ev = one3(0)
            for src, even, odd, extra, mul in ((q_ref, qe_ref, qo_ref, eq, scale), (k_ref, ke_ref, ko_ref, ek, 1.0), (v_ref, ve_ref, vo_ref, ev, 1.0)):
                x = src[:, ps].astype(F32) * mul
                even[:, ps] = jnp.where(lo, x, extra).astype(BF16)
                odd[:, ps] = jnp.where(lo, extra, x).astype(BF16)

    col = lambda c: pl.BlockSpec((ts, D_MIX), functools.partial(lambda i, c: (i, c), c=c))
    out = pl.BlockSpec((ts, D_MIX), lambda i: (i, 0))
    return pl.pallas_call(
        body, name="attn_prep", grid=(s // ts,),
        in_specs=[col(cb_q), col(cb_q + 1), col(cb_q + 2), pl.BlockSpec((ts, D_MIX), lambda i: (i, 0))],
        out_specs=[out] * 6, out_shape=[jax.ShapeDtypeStruct((s, D_MIX), BF16)] * 6,
        compiler_params=_params(("parallel",)),
    )(proj, proj, proj, cum)


def _pair_specs(s, t):
    return pl.BlockSpec((t, LANES), lambda p, i: (i, p)), pl.BlockSpec((s, LANES), lambda p, i: (0, p))


def attn_fwd(qe, qo, ke, ko, ve, vo, tq=512):
    s = qe.shape[0]
    tq = min(tq, s)

    def body(qe_ref, qo_ref, ke_ref, ko_ref, ve_ref, vo_ref, o_ref, qbe_ref, qbo_ref):
        qi = pl.program_id(1)
        qs, k_refs, v_refs = (qe_ref[...], qo_ref[...]), (ke_ref, ko_ref), (ve_ref, vo_ref)
        causal = lax.broadcasted_iota(jnp.int32, (tq, tq), 1) <= lax.broadcasted_iota(jnp.int32, (tq, tq), 0)

        def step(j, carry, diag):
            ks = pl.multiple_of(j * tq, tq)
            new = []
            for h in range(2):
                m, l, acc = carry[h]
                sc = lax.dot_general(qs[h], k_refs[h][pl.ds(ks, tq), :], NT, preferred_element_type=F32)
                if diag:
                    sc = jnp.where(causal, sc, -jnp.inf)
                m_new = jnp.maximum(m, jnp.max(sc, axis=1, keepdims=True))
                p = jnp.exp2(sc - m_new)
                alpha = jnp.exp2(m - m_new)
                l = alpha * l + jnp.sum(p, axis=1, keepdims=True)
                acc = alpha * acc + jnp.dot(p.astype(BF16), v_refs[h][pl.ds(ks, tq), :], preferred_element_type=F32)
                new.append((m_new, l, acc))
            return tuple(new)

        init = tuple((jnp.full((tq, 1), -jnp.inf, F32), jnp.zeros((tq, 1), F32), jnp.zeros((tq, LANES), F32)) for _ in range(2))
        carry = lax.fori_loop(0, qi, lambda j, c: step(j, c, False), init)
        (m0, l0, a0), (m1, l1, a1) = step(qi, carry, True)
        _, li, lo = _lane_ids((tq, LANES))
        o_ref[...] = jnp.where(lo, a0 / l0, a1 / l1).astype(o_ref.dtype)
        lse_lanes = (li >= 6) & (li < 9)
        for q, m, l, spare, out_ref in ((qs[0], m0, l0, ~lo, qbe_ref), (qs[1], m1, l1, lo, qbo_ref)):
            neg_lse = _triple(li, 6, _split3(-(m + jnp.log(l) * LOG2E)))
            out_ref[...] = jnp.where(spare & lse_lanes, neg_lse.astype(BF16), q)

    blk, full = _pair_specs(s, tq)
    return pl.pallas_call(
        body, name="attn_fwd", grid=(N_PAIR, s // tq),
        in_specs=[blk, blk, full, full, full, full],
        out_specs=[blk] * 3, out_shape=[jax.ShapeDtypeStruct((s, D_MIX), BF16)] * 3,
        compiler_params=_params(("parallel", "parallel")),
    )(qe, qo, ke, ko, ve, vo)


def attn_dq(qbe, qbo, ke, ko, ve, vo, do, dep, dproj, lane_block, tq=512):
    s = qbe.shape[0]
    tq = min(tq, s)
    scale = 1.0 / math.sqrt(HEAD_DIM)

    def body(qe_ref, qo_ref, ke_ref, ko_ref, ve_ref, vo_ref, do_ref, dep_ref, dproj_in, dq_ref, dobe_ref, dobo_ref):
        qi = pl.program_id(1)
        _, li, lo = _lane_ids((tq, LANES))
        do_ = do_ref[...]
        qs, k_refs, v_refs = (qe_ref[...], qo_ref[...]), (ke_ref, ko_ref), (ve_ref, vo_ref)
        dos = (jnp.where(lo, do_, 0), jnp.where(lo, 0, do_))
        causal = lax.broadcasted_iota(jnp.int32, (tq, tq), 1) <= lax.broadcasted_iota(jnp.int32, (tq, tq), 0)

        def step(j, carry, diag):
            ks = pl.multiple_of(j * tq, tq)
            new = []
            for h in range(2):
                pdpk, pk, dsum = carry[h]
                kb = k_refs[h][pl.ds(ks, tq), :]
                sc = lax.dot_general(qs[h], kb, NT, preferred_element_type=F32)
                if diag:
                    sc = jnp.where(causal, sc, -jnp.inf)
                p = jnp.exp2(sc)
                pdp = p * lax.dot_general(dos[h], v_refs[h][pl.ds(ks, tq), :], NT, preferred_element_type=F32)
                new.append((pdpk + jnp.dot(pdp.astype(BF16), kb, preferred_element_type=F32),
                            pk + jnp.dot(p.astype(BF16), kb, preferred_element_type=F32),
                            dsum + jnp.sum(pdp, axis=1, keepdims=True)))
            return tuple(new)

        init = tuple((jnp.zeros((tq, LANES), F32), jnp.zeros((tq, LANES), F32), jnp.zeros((tq, 1), F32)) for _ in range(2))
        carry = lax.fori_loop(0, qi, lambda j, c: step(j, c, False), init)
        (a0, b0, s0), (a1, b1, s1) = step(qi, carry, True)
        dq_ref[...] = (jnp.where(lo, a0 - s0 * b0, a1 - s1 * b1) * scale).astype(dq_ref.dtype)
        dobe_ref[...] = jnp.where(lo, do_, _triple(li, 0, _split3(-s0)).astype(BF16))
        dobo_ref[...] = jnp.where(lo, _triple(li, 0, _split3(-s1)).astype(BF16), do_)

    blk, full = _pair_specs(s, tq)
    return pl.pallas_call(
        body, name="attn_dq", grid=(N_PAIR, s // tq),
        in_specs=[blk, blk, full, full, full, full, blk, pl.BlockSpec(dep.shape, lambda p, i: (0, 0)), pl.BlockSpec(memory_space=pl.ANY)],
        out_specs=[pl.BlockSpec((tq, LANES), lambda p, i: (i, lane_block + p)), blk, blk],
        out_shape=[jax.ShapeDtypeStruct(dproj.shape, dproj.dtype)] + [jax.ShapeDtypeStruct((s, D_MIX), BF16)] * 2,
        input_output_aliases={8: 0}, compiler_params=_params(("parallel", "parallel")),
    )(qbe, qbo, ke, ko, ve, vo, do, dep, dproj)


def attn_dkv(ke, ko, ve, vo, qbe, qbo, dobe, dobo, dproj, lane_block, tk=512):
    s = ke.shape[0]
    tk = min(tk, s)
    nq = s // tk

    def body(ke_ref, ko_ref, ve_ref, vo_ref, qe_ref, qo_ref, de_ref, do_ref, dproj_in, dk_ref, dv_ref, dck_ref):
        kj = pl.program_id(1)
        lo = _lane_lt64((tk, LANES))
        ks_, vs_, q_refs, d_refs = (ke_ref[...], ko_ref[...]), (ve_ref[...], vo_ref[...]), (qe_ref, qo_ref), (de_ref, do_ref)
        causal = lax.broadcasted_iota(jnp.int32, (tk, tk), 0) <= lax.broadcasted_iota(jnp.int32, (tk, tk), 1)

        def step(i, carry, diag):
            qs = pl.multiple_of(i * tk, tk)
            new = []
            for h in range(2):
                dk, dv, dck = carry[h]
                qblk = q_refs[h][pl.ds(qs, tk), :]
                dblk = d_refs[h][pl.ds(qs, tk), :]
                st = lax.dot_general(ks_[h], qblk, NT, preferred_element_type=F32)
                if diag:
                    st = jnp.where(causal, st, -jnp.inf)
                pt = jnp.exp2(st)
                dst = pt * lax.dot_general(vs_[h], dblk, NT, preferred_element_type=F32)
                new.append((dk + jnp.dot(dst.astype(BF16), qblk, preferred_element_type=F32),
                            dv + jnp.dot(pt.astype(BF16), dblk, preferred_element_type=F32),
                            dck - jnp.sum(dst, axis=1, keepdims=True)))
            return tuple(new)

        init = tuple((jnp.zeros((tk, LANES), F32), jnp.zeros((tk, LANES), F32), jnp.zeros((tk, 1), F32)) for _ in range(2))
        carry = step(kj, init, True)
        (dk0, dv0, dc0), (dk1, dv1, dc1) = lax.fori_loop(kj + 1, nq, lambda i, c: step(i, c, False), carry)
        dk_ref[...] = (jnp.where(lo, dk0, dk1) * (1.0 / LOG2E)).astype(dk_ref.dtype)
        dv_ref[...] = jnp.where(lo, dv0, dv1).astype(dv_ref.dtype)
        dck_ref[...] = jnp.where(lo, dc0, dc1)

    blk, full = _pair_specs(s, tk)
    return pl.pallas_call(
        body, name="attn_dkv", grid=(N_PAIR, nq),
        in_specs=[blk, blk, blk, blk, full, full, full, full, pl.BlockSpec(memory_space=pl.ANY)],
        out_specs=[pl.BlockSpec((tk, LANES), lambda p, i: (i, lane_block + p)), blk, blk],
        out_shape=[jax.ShapeDtypeStruct(dproj.shape, dproj.dtype), jax.ShapeDtypeStruct((s, D_MIX), BF16), jax.ShapeDtypeStruct((s, D_MIX), F32)],
        input_output_aliases={8: 0}, compiler_params=_params(("parallel", "parallel")),
    )(ke, ko, ve, vo, qbe, qbo, dobe, dobo, dproj)


def _pre_bwd(name, x, g, sc, sh, dh, dres):
    d = x.shape[1]

    def fn(xv, dhv, dresv, gv, scv, shv):
        _, vjp = jax.vjp(lambda *a: _f_pre(*a)[0], xv, gv, scv, shv)
        dx, dg, dsc, dsh = vjp(dhv.astype(F32))
        return dx + dresv, dg, dsc, dsh

    return rowwise(name, fn, [(x, 0, d), (dh, 0, d), (dres, 0, d)], [g, sc, sh], [(d, F32)], accs=[(1, d)] * 3)


def layer_fwd(x, mod, layer):
    s, d = x.shape
    m = D_MIX
    w = dict(layer["small"])
    sh1, sc1, gt1, sh2, sc2, gt2 = (mod[i:i + 1] for i in range(6))
    cb = 3 * d // m
    (h,) = rowwise("pre1", _f_pre, [(x, 0, d)], [w["mix_pre_g"], sc1, sh1], [(d, BF16)])
    w.update(layer["get_in"](h))
    proj = matmul("w_in", h, w["w_in"], out_dtypes=(BF16,))
    fproj = matmul("w_in_forget", h, w["w_in"], b_cols=((cb + 7) * m, m))
    w.update(layer["get_abco"](proj))
    u, vln = rowwise("gmlp_in", _f_a1, [(proj, cb, m), (proj, cb + 1, m)], [w["gmlp_ln_g"], w["gmlp_ln_b"]], [(m, F32), (m, BF16)])
    sv, ya = spatial_fwd(vln, u, w["ws"], w["bs_exp"])
    y_a = matmul("w_a", ya, w["w_a_out"], out_dtypes=(BF16,))
    zc, yb = conv_fwd(proj, cb + 2, cb + 3, w["conv_w"], w["conv_b"], w["conv_ln_g"], w["conv_ln_b"])
    y_b = matmul("w_b", yb, w["w_b_out"], out_dtypes=(BF16,))
    cum = forget_cumsum(fproj, 0, w["bf_exp"])
    kv_ops = attn_prep(proj, cb + 4, cum)
    o, qbe, qbo = attn_fwd(*kv_ops)
    att = (qbe, qbo) + tuple(kv_ops[2:])
    y_c = matmul("w_c", o, w["w_c_out"], out_dtypes=(BF16,))
    (merged,) = rowwise("merge", _f_merge, [(proj, 0, d), (proj, 1, d), (proj, 2, d), (y_a, 0, d), (y_b, 0, d), (y_c, 0, d)], [], [(d, BF16)])
    y = matmul("w_out", merged, w["w_out"])
    w.update(layer["get_mlp"](y))

    def post_pre(xv, yv, gp, gt, g2, sc, sh):
        x1 = xv + _f_post(yv, gp, gt)[0]
        return x1, _f_pre(x1, g2, sc, sh)[0]

    x1, h2 = rowwise("post1", post_pre, [(x, 0, d), (y, 0, d)], [w["mix_post_g"], gt1, w["mlp_pre_g"], sc2, sh2], [(d, F32), (d, BF16)])
    r = matmul("w1", h2, w["mlp_w1"], out_dtypes=(BF16,), epilogue=lambda acc: (jnp.square(jnp.maximum(acc, 0.0)),))
    y2 = matmul("w2", r, w["mlp_w2"])
    (x2,) = rowwise("post2", lambda xv, yv, g, gt: xv + _f_post(yv, g, gt)[0], [(x1, 0, d), (y2, 0, d)], [w["mlp_post_g"], gt2], [(d, F32)])
    saved = dict(w=w, x=x, h=h, proj=proj, fproj=fproj, u=u, vln=vln, sv=sv, ya=ya, y_a=y_a, zc=zc, yb=yb, y_b=y_b, att=att,
                 o=o, y_c=y_c, merged=merged, y=y, x1=x1, h2=h2, r=r, y2=y2)
    return x2, saved


def layer_bwd(dx2, mod, sv, emit, tok_in=None):
    x, proj, w = sv["x"], sv["proj"], sv["w"]
    s, d = x.shape
    m = D_MIX
    sh1, sc1, gt1, sh2, sc2, gt2 = (mod[i:i + 1] for i in range(6))
    cb = 3 * d // m
    g = {}
    if tok_in is not None:
        gt2 = gt2 + tok_in[0:1, 0:1]
    dy2, g["mlp_post_g"], dgt2 = rowwise_vjp("post2_b", _f_post, [(sv["y2"], 0, d)], [w["mlp_post_g"], gt2], [(dx2, 0, d)], [BF16])
    da = matmul("w2_dx", dy2, w["mlp_w2"], tb=True, out_dtypes=(BF16,),
                epilogue=lambda acc, r: (acc * (2.0 * jnp.sqrt(r.astype(F32))),), epi=[(sv["r"], 0)])
    big = {}
    big["mlp_w2"] = matmul("w2_dw", sv["r"], dy2, ta=True, out_dtypes=(BF16,))
    dh2 = matmul("w1_dx", da, w["mlp_w1"], tb=True, out_dtypes=(BF16,))
    big["mlp_w1"] = matmul("w1_dw", sv["h2"], da, ta=True, out_dtypes=(BF16,))
    tok = emit("mlp", big)
    dx1, g["mlp_pre_g"], dsc2, dsh2 = _pre_bwd("pre2_b", sv["x1"], w["mlp_pre_g"], sc2 + tok[0:1, 0:1], sh2, dh2, dx2)
    dy, g["mix_post_g"], dgt1 = rowwise_vjp("post1_b", _f_post, [(sv["y"], 0, d)], [w["mix_post_g"], gt1], [(dx1, 0, d)], [BF16])
    dmerged = matmul("w_out_dx", dy, w["w_out"], tb=True, out_dtypes=(BF16,))
    big = {}
    big["w_out"] = matmul("w_out_dw", sv["merged"], dy, ta=True, out_dtypes=(BF16,))
    assert (3 * d) % (2 * m) == 0
    dproj = lax.empty((s, 3 * d + 8 * m), BF16)
    dproj, dya_, dyb_, dyc_ = rowwise_vjp(
        "merge_b", _f_merge, [(proj, 0, d), (proj, 1, d), (proj, 2, d), (sv["y_a"], 0, d), (sv["y_b"], 0, d), (sv["y_c"], 0, d)], [],
        [(dmerged, 0, d)], [BF16] * 6, into=(dproj, 0, 3))
    dya_pre = matmul("w_a_dx", dya_, w["w_a_out"], tb=True, out_dtypes=(BF16,))
    big["w_a_out"] = matmul("w_a_dw", sv["ya"], dya_, ta=True, out_dtypes=(BF16,))
    dyb_pre = matmul("w_b_dx", dyb_, w["w_b_out"], tb=True, out_dtypes=(BF16,))
    big["w_b_out"] = matmul("w_b_dw", sv["yb"], dyb_, ta=True, out_dtypes=(BF16,))
    do = matmul("w_c_dx", dyc_, w["w_c_out"], tb=True, out_dtypes=(BF16,))
    big["w_c_out"] = matmul("w_c_dw", sv["o"], dyc_, ta=True, out_dtypes=(BF16,))
    tok = emit("abco", big)
    qbe, qbo, ke, ko, ve, vo = sv["att"]
    lane0 = (cb + 4) * (m // LANES)
    dproj, dobe, dobo = attn_dq(qbe, qbo, ke, ko, ve, vo, do, tok, dproj, lane0)
    dproj, dv, dcum = attn_dkv(ke, ko, ve, vo, qbe, qbo, dobe, dobo, dproj, lane0 + N_PAIR)
    dproj = lax.dynamic_update_slice(dproj, dv, (0, (cb + 6) * m))
    dproj, dbf = forget_bwd(sv["fproj"], 0, w["bf_exp"], dcum, dproj, cb + 7)
    g["fox_bf"] = dbf[0, ::HEAD_DIM]
    dproj, dwc, g["conv_b"], g["conv_ln_g"], g["conv_ln_b"] = conv_bwd(
        proj, cb + 2, cb + 3, sv["zc"], dyb_pre, w["conv_w"], w["conv_ln_g"], w["conv_ln_b"], dproj)
    g["conv_w"] = dwc[:CONV_WIDTH]
    du, dvln, dws, dbexp = spatial_bwd(dya_pre, sv["u"], sv["sv"], sv["vln"], w["ws_t"])
    g["gmlp_ws"] = dws * jnp.tril(jnp.ones((CHUNK, CHUNK), F32))
    g["gmlp_bs"] = dbexp.reshape(CHUNK, m // GROUP_DIM, GROUP_DIM).sum(-1).T
    dproj, g["gmlp_ln_g"], g["gmlp_ln_b"] = rowwise_vjp(
        "gmlp_in_b", _f_a1, [(proj, cb, m), (proj, cb + 1, m)], [w["gmlp_ln_g"], w["gmlp_ln_b"]], [(du, 0, m), (dvln, 0, m)], [BF16, BF16],
        into=(dproj, cb // 2, 2))
    tok = emit("in", {"w_in": matmul("w_in_dw", sv["h"], dproj, ta=True, out_dtypes=(BF16,))})
    dh = matmul("w_in_dx", dproj, w["w_in"], tb=True, dep=tok, out_dtypes=(BF16,))
    dx, g["mix_pre_g"], dsc1, dsh1 = _pre_bwd("pre1_b", x, w["mix_pre_g"], sc1, sh1, dh, dx1)
    dmod = jnp.concatenate([dsh1, dsc1, dgt1, dsh2, dsc2, dgt2], axis=0)
    return dx, dmod, g


def local_step(x, target, mods, layers):
    d = x.shape[1]
    saved = []
    for l in range(len(layers)):
        x, sv = layer_fwd(x, mods[l], layers[l])
        saved.append(sv)

    def loss_fn(xv, tv):
        err = xv - tv
        return err * (1.0 / d), jnp.sum(err * err, axis=0, keepdims=True)

    dx, sq = rowwise("loss", loss_fn, [(x, 0, d), (target, 0, d)], [], [(d, F32)], accs=[(1, d)])
    loss = (0.5 / d) * jnp.sum(sq)
    dmods, grads = [None] * len(layers), [None] * len(layers)
    tok = None
    for l in reversed(range(len(layers))):
        dx, dmods[l], grads[l] = layer_bwd(dx, mods[l], saved[l], layers[l]["emit"], tok)
        tok = layers[l]["emit_small"](dmods[l], grads[l])
    return loss, dx, dmods, grads


def exchange(name, arrs, scatter):
    n = len(arrs)

    def body(*refs):
        in_refs, out_refs = refs[:n], refs[n:2 * n]
        send_sems, recv_sems, local_sems = refs[2 * n:]
        x, y, c = lax.axis_index("x"), lax.axis_index("y"), lax.axis_index("c")
        me = 4 * x + 2 * y + c
        local = []
        for a in range(n):
            src = in_refs[a].at[me] if scatter else in_refs[a]
            cp = pltpu.make_async_copy(src, out_refs[a].at[me], local_sems.at[a])
            cp.start()
            local.append(cp)
        remote = []
        for k in range(1, N_DEV):
            px, py, pc = x ^ ((k >> 2) & 1), y ^ ((k >> 1) & 1), c ^ (k & 1)
            peer = 4 * px + 2 * py + pc
            for a in range(n):
                src = in_refs[a].at[peer] if scatter else in_refs[a]
                cp = pltpu.make_async_remote_copy(
                    src_ref=src, dst_ref=out_refs[a].at[me], send_sem=send_sems.at[a * (N_DEV - 1) + k - 1],
                    recv_sem=recv_sems.at[a * (N_DEV - 1) + k - 1], device_id=(px, py, pc), device_id_type=MESH)
                cp.start()
                remote.append(cp)
        for cp in remote:
            cp.wait()
        for cp in local:
            cp.wait()

    hbm = pl.BlockSpec(memory_space=pltpu.HBM)
    out_shape = [jax.ShapeDtypeStruct(a.shape if scatter else (N_DEV,) + a.shape, a.dtype) for a in arrs]
    return pl.pallas_call(
        body, name=name, in_specs=[hbm] * n, out_specs=[hbm] * n, out_shape=out_shape,
        scratch_shapes=[pltpu.SemaphoreType.DMA((n * (N_DEV - 1),)), pltpu.SemaphoreType.DMA((n * (N_DEV - 1),)),
                        pltpu.SemaphoreType.DMA((n,))],
    )(*arrs)


def _peers(x, y, c):
    out = []
    for k in range(1, N_DEV):
        px, py, pc = x ^ ((k >> 2) & 1), y ^ ((k >> 1) & 1), c ^ (k & 1)
        out.append((k - 1, (px, py, pc), 4 * px + 2 * py + pc))
    return out


def _exchange_copies(srcs, lands, send_sems, recv_sems, scatter):
    x, y, c = lax.axis_index("x"), lax.axis_index("y"), lax.axis_index("c")
    me = 4 * x + 2 * y + c
    copies = []
    for slot, pos, peer in _peers(x, y, c):
        for a, (src, land) in enumerate(zip(srcs, lands)):
            copies.append(pltpu.make_async_remote_copy(
                src_ref=src.at[peer] if scatter else src, dst_ref=land.at[me],
                send_sem=send_sems.at[a * (N_DEV - 1) + slot], recv_sem=recv_sems.at[a * (N_DEV - 1) + slot],
                device_id=pos, device_id_type=MESH))
    return me, copies


def exchange_start(name, arrs, scatter, after=None):
    n = len(arrs)
    lands = [lax.empty(a.shape if scatter else (N_DEV,) + a.shape, a.dtype) for a in arrs]
    n_in = 2 * n + (after is not None)

    def body(*refs):
        srcs, lands_ = refs[:n], refs[n:2 * n]
        send_sems, recv_sems = refs[n_in], refs[n_in + 1]
        token = refs[n_in + 2 + 2 * n]
        _, copies = _exchange_copies(srcs, lands_, send_sems, recv_sems, scatter)
        for cp in copies:
            cp.start()
        token[...] = jnp.zeros(token.shape, token.dtype)

    hbm = pl.BlockSpec(memory_space=pltpu.HBM)
    sem = pl.BlockSpec(memory_space=pltpu.SEMAPHORE)
    n_sem = n * (N_DEV - 1)
    res = pl.pallas_call(
        body, name=name,
        out_shape=(pltpu.SemaphoreType.DMA((n_sem,)), pltpu.SemaphoreType.DMA((n_sem,)),
                   *[pltpu.HBM(a.shape, a.dtype) for a in arrs], *[pltpu.HBM(l.shape, l.dtype) for l in lands],
                   jax.ShapeDtypeStruct((8, LANES), F32)),
        in_specs=[hbm] * (2 * n) + ([] if after is None else [pl.BlockSpec(memory_space=pl.ANY)]),
        out_specs=(sem, sem, *([hbm] * (2 * n)), pl.BlockSpec(memory_space=pltpu.VMEM)),
        input_output_aliases={i: 2 + i for i in range(2 * n)},
        compiler_params=pltpu.CompilerParams(has_side_effects=pltpu.SideEffectType.DATAFLOW_SIDE_EFFECTING),
    )(*[pltpu.with_memory_space_constraint(a, pltpu.HBM) for a in arrs],
      *[pltpu.with_memory_space_constraint(l, pltpu.HBM) for l in lands], *([] if after is None else [after]))
    return dict(n=n, scatter=scatter, send=res[0], recv=res[1], srcs=res[2:2 + n], lands=res[2 + n:2 + 2 * n], token=res[2 + 2 * n])


def exchange_wait(name, st, after):
    n, scatter = st["n"], st["scatter"]
    after = list(after) if isinstance(after, (list, tuple)) else [after]

    def body(*refs):
        srcs, lands_ = refs[:n], refs[n:2 * n]
        send_sems, recv_sems = refs[2 * n], refs[2 * n + 1]
        _, copies = _exchange_copies(srcs, lands_, send_sems, recv_sems, scatter)
        for cp in copies:
            cp.wait_send()
            cp.wait_recv()

    hbm = pl.BlockSpec(memory_space=pltpu.HBM)
    sem = pl.BlockSpec(memory_space=pltpu.SEMAPHORE)
    res = pl.pallas_call(
        body, name=name,
        out_shape=tuple(pltpu.HBM(a.shape, a.dtype) for a in (*st["srcs"], *st["lands"])),
        in_specs=[hbm] * (2 * n) + [sem, sem] + [pl.BlockSpec(memory_space=pl.ANY)] * len(after), out_specs=tuple([hbm] * (2 * n)),
        input_output_aliases={i: i for i in range(2 * n)},
        compiler_params=pltpu.CompilerParams(has_side_effects=pltpu.SideEffectType.DATAFLOW_SIDE_EFFECTING),
    )(*st["srcs"], *st["lands"], st["send"], st["recv"], *after)
    return list(res[:n]), list(res[n:])


def adamw_sum(name, parts, w, m, v, tr=256):
    nl = len(parts)
    k, r, c = parts[0].shape
    tr = _tile(r, tr, 16)
    c1 = 1.0 - ADAM_B1 ** ADAM_STEP
    c2 = 1.0 - ADAM_B2 ** ADAM_STEP

    def body(*refs):
        p_refs = refs[:nl]
        w_ref, m_ref, v_ref, g_ref, d_ref, nm_ref, nv_ref = refs[nl:]
        for l in range(nl):
            @pl.when(pl.program_id(0) == l)
            def _(p_ref=p_refs[l]):
                grad = p_ref[0].astype(F32)
                for j in range(1, k):
                    grad = grad + p_ref[j].astype(F32)
                new_m = ADAM_B1 * m_ref[...] + (1.0 - ADAM_B1) * grad
                new_v = ADAM_B2 * v_ref[...] + (1.0 - ADAM_B2) * (grad * grad)
                m_hat = new_m / c1
                v_hat = new_v / c2
                g_ref[...] = grad
                d_ref[...] = -ADAM_LR * (m_hat / (jnp.sqrt(v_hat) + ADAM_EPS) + ADAM_WD * w_ref[...])
                nm_ref[...] = new_m
                nv_ref[...] = new_v

    part = lambda l: pl.BlockSpec((k, tr, c), functools.partial(lambda ll, i, l: (0, jnp.where(ll == l, i, 0), 0), l=l))
    blk = pl.BlockSpec((None, tr, c), lambda ll, i: (ll, i, 0))
    return pl.pallas_call(
        body, name=name, grid=(nl, r // tr),
        in_specs=[part(l) for l in range(nl)] + [blk, blk, blk],
        out_specs=[blk] * 4, out_shape=[jax.ShapeDtypeStruct((nl, r, c), F32)] * 4,
        compiler_params=_params(("parallel", "parallel")),
    )(*parts, w, m, v)


def ada_fwd(c_all, ada_w):
    nl, d, n = ada_w.shape

    def body(c_ref, w_ref, o_ref):
        o_ref[...] = jnp.dot(_silu(c_ref[...]), w_ref[...], precision=HIGHEST, preferred_element_type=F32)

    return pl.pallas_call(
        body, name="ada_fwd", grid=(nl,),
        in_specs=[pl.BlockSpec((N_DEV, d), lambda l: (0, 0)), pl.BlockSpec((None, d, n), lambda l: (l, 0, 0))],
        out_specs=pl.BlockSpec((None, N_DEV, n), lambda l: (l, 0, 0)),
        out_shape=jax.ShapeDtypeStruct((nl, N_DEV, n), F32),
        compiler_params=_params(("parallel",)),
    )(c_all, ada_w)


def ada_bwd(c_all_t, dmod, td=256):
    d = c_all_t.shape[0]
    nl, _, n = dmod.shape
    td = _tile(d, td, 8)

    def body(c_ref, dm_ref, o_ref):
        ca = _silu(c_ref[...])
        acc = ca[:, 0:1] * dm_ref[0:1, :]
        for b in range(1, N_DEV):
            acc = acc + ca[:, b:b + 1] * dm_ref[b:b + 1, :]
        o_ref[...] = acc

    return pl.pallas_call(
        body, name="ada_bwd", grid=(nl, d // td),
        in_specs=[pl.BlockSpec((td, N_DEV), lambda l, i: (i, 0)), pl.BlockSpec((None, N_DEV, n), lambda l, i: (l, 0, 0))],
        out_specs=pl.BlockSpec((None, td, n), lambda l, i: (l, i, 0)),
        out_shape=jax.ShapeDtypeStruct((nl, d, n), F32),
        compiler_params=_params(("parallel", "parallel")),
    )(c_all_t, dmod)


ARG_NAMES = ["x", "c", "ada_w", "ada_b", "mix_pre_g", "mix_post_g", "mlp_pre_g", "mlp_post_g", "w_in", "gmlp_ln_g", "gmlp_ln_b",
             "gmlp_ws", "gmlp_bs", "w_a_out", "conv_w", "conv_b", "conv_ln_g", "conv_ln_b", "w_b_out", "fox_bf", "w_c_out",
             "w_out", "mlp_w1", "mlp_w2"]
WEIGHTS = ARG_NAMES[2:]
COL_SHARDED = ["w_in", "w_a_out", "w_b_out", "w_c_out", "mlp_w1"]
ROW_SHARDED = ["w_out", "mlp_w2"]
BIG = COL_SHARDED + ROW_SHARDED
GROUPS = {"in": ["w_in"], "abco": ["w_a_out", "w_b_out", "w_c_out", "w_out"], "mlp": ["mlp_w1", "mlp_w2"]}
SMALL = ["ada_b", "mix_pre_g", "mix_post_g", "mlp_pre_g", "mlp_post_g", "gmlp_ln_g", "gmlp_ln_b", "gmlp_ws", "gmlp_bs",
         "conv_b", "conv_ln_g", "conv_ln_b", "fox_bf"]
PACK_COLS = 512


def _to_my_layout(w_in, d):
    m = D_MIX
    nf = 7 * m
    return jnp.concatenate([w_in[..., nf + N_HEADS:], w_in[..., :nf], jnp.repeat(w_in[..., nf:nf + N_HEADS], HEAD_DIM, axis=-1)], axis=-1)


def _from_my_layout(gw, d):
    m = D_MIX
    return jnp.concatenate([gw[..., 3 * d:3 * d + 7 * m], gw[..., 3 * d + 7 * m::HEAD_DIM], gw[..., :3 * d]], axis=-1)


def _ref_ranges(lo, hi, shard):
    out = []
    while lo < hi:
        j = lo // shard
        end = min(hi, (j + 1) * shard)
        out.append((j, lo - j * shard, end - j * shard))
        lo = end
    return out


def _w_in_to_kernel_layout(g, d):
    m = D_MIX
    nf = 7 * m
    shard = g.shape[2]
    cols = lambda lo, hi: [g[j, :, a:b] for j, a, b in _ref_ranges(lo, hi, shard)]
    forget = jnp.concatenate(cols(nf, nf + N_HEADS), axis=1)
    return jnp.concatenate(cols(nf + N_HEADS, nf + N_HEADS + 3 * d) + cols(0, nf) + [jnp.repeat(forget, HEAD_DIM, axis=1)], axis=1)


def _w_in_grad_blocks(gw, d):
    m = D_MIX
    nf = 7 * m
    n_ref = nf + N_HEADS + 3 * d
    shard = n_ref // N_DEV
    segs = [(0, nf, 3 * d, 1), (nf, nf + N_HEADS, 3 * d + nf, HEAD_DIM), (nf + N_HEADS, n_ref, 0, 1)]
    blocks = []
    for j in range(N_DEV):
        lo, hi = j * shard, (j + 1) * shard
        pieces = []
        for r0, r1, k0, stride in segs:
            a, b = max(lo, r0), min(hi, r1)
            if a < b:
                pieces.append(gw[:, k0 + (a - r0) * stride:k0 + (b - r0) * stride:stride])
        blocks.append(jnp.concatenate(pieces, axis=1) if len(pieces) > 1 else pieces[0])
    return jnp.stack(blocks)


def _pack(parts):
    flat = jnp.concatenate([p.reshape(-1).astype(F32) for p in parts])
    pad = (-flat.shape[0]) % (PACK_COLS * 8)
    return jnp.pad(flat, (0, pad)).reshape(-1, PACK_COLS)


def _unpack(packed, shapes):
    nl = packed.shape[0]
    flat, out, off = packed.reshape(nl, -1), [], 0
    for shp in shapes:
        n = math.prod(shp)
        out.append(flat[:, off:off + n].reshape((nl,) + tuple(shp)))
        off += n
    return out


def _layer_small(p, conv_full, l):
    wl = {}
    for k in ["mix_pre_g", "mix_post_g", "mlp_pre_g", "mlp_post_g", "gmlp_ln_g", "gmlp_ln_b", "conv_b", "conv_ln_g", "conv_ln_b"]:
        wl[k] = p[k][l][None, :]
    wm = p["gmlp_ws"][l] * jnp.tril(jnp.ones((CHUNK, CHUNK), F32))
    wl["ws"] = wm.astype(BF16)
    wl["ws_t"] = jnp.transpose(wm, (0, 2, 1)).astype(BF16)
    wl["bs_exp"] = jnp.repeat(p["gmlp_bs"][l].T, GROUP_DIM, axis=1)
    wl["bf_exp"] = jnp.repeat(p["fox_bf"][l], HEAD_DIM)[None, :]
    wl["conv_w"] = jnp.pad(conv_full[l], ((0, CONV_HALO - CONV_WIDTH), (0, 0)))
    return wl


def kernel(x, c, ada_w, ada_b, mix_pre_g, mix_post_g, mlp_pre_g, mlp_post_g, w_in, gmlp_ln_g, gmlp_ln_b, gmlp_ws, gmlp_bs, w_a_out, conv_w, conv_b, conv_ln_g, conv_ln_b, w_b_out, fox_bf, w_c_out, w_out, mlp_w1, mlp_w2, loss_target, m_ada_w, m_ada_b, m_mix_pre_g, m_mix_post_g, m_mlp_pre_g, m_mlp_post_g, m_w_in, m_gmlp_ln_g, m_gmlp_ln_b, m_gmlp_ws, m_gmlp_bs, m_w_a_out, m_conv_w, m_conv_b, m_conv_ln_g, m_conv_ln_b, m_w_b_out, m_fox_bf, m_w_c_out, m_w_out, m_mlp_w1, m_mlp_w2, v_ada_w, v_ada_b, v_mix_pre_g, v_mix_post_g, v_mlp_pre_g, v_mlp_post_g, v_w_in, v_gmlp_ln_g, v_gmlp_ln_b, v_gmlp_ws, v_gmlp_bs, v_w_a_out, v_conv_w, v_conv_b, v_conv_ln_g, v_conv_ln_b, v_w_b_out, v_fox_bf, v_w_c_out, v_w_out, v_mlp_w1, v_mlp_w2):
    args = (x, c, ada_w, ada_b, mix_pre_g, mix_post_g, mlp_pre_g, mlp_post_g, w_in, gmlp_ln_g, gmlp_ln_b, gmlp_ws, gmlp_bs, w_a_out,
            conv_w, conv_b, conv_ln_g, conv_ln_b, w_b_out, fox_bf, w_c_out, w_out, mlp_w1, mlp_w2)
    ms = (m_ada_w, m_ada_b, m_mix_pre_g, m_mix_post_g, m_mlp_pre_g, m_mlp_post_g, m_w_in, m_gmlp_ln_g, m_gmlp_ln_b, m_gmlp_ws, m_gmlp_bs,
          m_w_a_out, m_conv_w, m_conv_b, m_conv_ln_g, m_conv_ln_b, m_w_b_out, m_fox_bf, m_w_c_out, m_w_out, m_mlp_w1, m_mlp_w2)
    vs = (v_ada_w, v_ada_b, v_mix_pre_g, v_mix_post_g, v_mlp_pre_g, v_mlp_post_g, v_w_in, v_gmlp_ln_g, v_gmlp_ln_b, v_gmlp_ws, v_gmlp_bs,
          v_w_a_out, v_conv_w, v_conv_b, v_conv_ln_g, v_conv_ln_b, v_w_b_out, v_fox_bf, v_w_c_out, v_w_out, v_mlp_w1, v_mlp_w2)
    p = dict(zip(ARG_NAMES, args))
    mom = dict(zip(WEIGHTS, ms))
    var = dict(zip(WEIGHTS, vs))
    nl = ada_w.shape[0]
    s, d = x.shape[1], x.shape[2]
    me = 4 * lax.axis_index("x") + 2 * lax.axis_index("y") + lax.axis_index("c")

    c_all, conv_all = exchange("gather_c", [c, conv_w], scatter=False)
    c_all = c_all.reshape(N_DEV, d)
    n_ada = ada_w.shape[2]
    mod_parts = ada_fwd(c_all, ada_w)
    (mod_recv,) = exchange("scatter_mod", [jnp.transpose(mod_parts, (1, 0, 2))], scatter=True)
    conv_full = jnp.transpose(conv_all, (1, 2, 0, 3)).reshape(nl, CONV_WIDTH, D_MIX)

    def full_matrix(k, land, own):
        g = lax.dynamic_update_index_in_dim(land, own, me, 0)
        r, cc = own.shape
        if k == "w_in":
            return _w_in_to_kernel_layout(g, d)
        return jnp.transpose(g, (1, 0, 2)).reshape(r, N_DEV * cc) if k in COL_SHARDED else g.reshape(N_DEV * r, cc)

    started = [mod_recv]

    def fetch(l, tag):
        keys = GROUPS[tag]
        st = exchange_start(f"gather_{tag}{l}_start", [p[k][l].astype(BF16) for k in keys], scatter=False, after=started[-1])
        started.append(st["token"])

        def get(after):
            owns, lands = exchange_wait(f"gather_{tag}{l}_wait", st, after)
            return {k: full_matrix(k, land, o) for k, land, o in zip(keys, lands, owns)}

        return get

    getters = [{tag: fetch(l, tag) for tag in GROUPS} for l in range(nl)]
    mod = jnp.transpose(mod_recv, (1, 0, 2)).reshape(nl, N_DEV * n_ada) + ada_b + started[-1][0:1, 0:1]
    mods = [mod[l].reshape(6, d) for l in range(nl)]

    sent = {}

    def emitter(l):
        def emit(tag, grads_big):
            keys = GROUPS[tag]
            send = []
            for k in keys:
                gk = grads_big[k]
                if k == "w_in":
                    gk = _w_in_grad_blocks(gk, d)
                elif k in COL_SHARDED:
                    r, cc = gk.shape[0], gk.shape[1] // N_DEV
                    gk = jnp.transpose(gk.reshape(r, N_DEV, cc), (1, 0, 2))
                else:
                    gk = gk.reshape(N_DEV, gk.shape[0] // N_DEV, gk.shape[1])
                send.append(gk.astype(BF16))
            sent[(l, tag)] = (keys, exchange_start(f"scatter_{tag}{l}_start", send, scatter=True))
            return sent[(l, tag)][1]["token"]

        return emit

    small_sent = {}

    def small_emitter(l):
        def emit_small(dmod_l, g):
            packed = _pack([dmod_l] + [g[k] for k in SMALL[1:]] + [g["conv_w"]])
            small_sent[l] = exchange_start(f"gather_small{l}_start", [packed], scatter=False)
            return small_sent[l]["token"]

        return emit_small

    layers = [dict(small=_layer_small(p, conv_full, l), emit_small=small_emitter(l), get_in=getters[l]["in"], get_abco=getters[l]["abco"], get_mlp=getters[l]["mlp"],
                   emit=emitter(l)) for l in range(nl)]
    loss_local, dx, dmods, grads = local_step(x[0], loss_target[0], mods, layers)
    loss = lax.psum(loss_local, ("x", "y", "c"))
    grad_x = dx[None]

    out = {k: [None] * 4 for k in WEIGHTS}

    def shard_update(name, parts, k):
        shp = p[k].shape
        flat = lambda a: a.reshape(nl, -1, shp[-1])
        res = adamw_sum(name, [pt.reshape((pt.shape[0],) + flat(p[k]).shape[1:]) for pt in parts], flat(p[k]), flat(mom[k]), flat(var[k]))
        out[k] = [a.reshape(shp) for a in res]

    parts = {}

    def collect(tag, after):
        for l in reversed(range(nl)):
            keys, st = sent[(l, tag)]
            sends, lands = exchange_wait(f"scatter_{tag}{l}_wait", st, after)
            for k, land, sd in zip(keys, lands, sends):
                own = lax.dynamic_index_in_dim(sd, me, 0, keepdims=False)
                parts[(k, l)] = lax.dynamic_update_index_in_dim(land, own, me, 0)

    behind_bwd = [dx] + [small_sent[l]["token"] for l in range(nl)]
    collect("mlp", behind_bwd)
    collect("abco", behind_bwd)
    for k in BIG[1:]:
        shard_update("adamw_" + k, [parts[(k, l)] for l in range(nl)], k)
    collect("in", [out[k][0] for k in BIG[1:]])
    shard_update("adamw_w_in", [parts[("w_in", l)] for l in range(nl)], "w_in")

    small_shapes = [p[k].shape[1:] for k in SMALL] + [(CONV_WIDTH, D_MIX)]
    small_all = []
    for l in range(nl):
        srcs, lands = exchange_wait(f"gather_small{l}_wait", small_sent[l], [out[k][0] for k in BIG])
        small_all.append(lax.dynamic_update_index_in_dim(lands[0], srcs[0], me, 0))
    zeros_conv = jnp.zeros(small_shapes[-1], F32)
    packs = [jnp.stack([_pack([src[k][l] for k in SMALL] + [zeros_conv]) for l in range(nl)]) for src in (p, mom, var)]
    small_out = [_unpack(o, small_shapes) for o in adamw_sum("adamw_small", small_all, *packs)]
    for i, k in enumerate(SMALL):
        for j in range(4):
            out[k][j] = small_out[j][i]

    n_conv = conv_w.shape[2]
    conv_grad = lax.dynamic_slice_in_dim(small_out[0][-1], me * n_conv, n_conv, axis=2)
    shard_update("adamw_conv_w", [conv_grad[l][None] for l in range(nl)], "conv_w")

    dmod_all = jnp.stack([small_all[l].reshape(N_DEV, -1)[:, :6 * d] for l in range(nl)])
    dmod_mine = lax.dynamic_slice_in_dim(dmod_all, me * n_ada, n_ada, axis=2)
    g_ada = ada_bwd(c_all.T, dmod_mine)
    shard_update("adamw_ada_w", [g_ada[l][None] for l in range(nl)], "ada_w")

    res = [loss, grad_x]
    for j in range(4):
        res += [out[k][j] for k in WEIGHTS]
    return tuple(res)
```

```python
import functools
import math

import jax
import jax.numpy as jnp
from jax import lax
from jax.experimental import pallas as pl
from jax.experimental.pallas import tpu as pltpu

F32 = jnp.float32
BF16 = jnp.bfloat16
MESH = pl.DeviceIdType.MESH
N_DEV = 8
NORM_EPS = 1e-6
D_MIX = 512
N_HEADS = 8
HEAD_DIM = 64
GROUP_DIM = 64
CHUNK = 128
CONV_WIDTH = 31
CONV_HALO = 32
LANES = 128
ADAM_LR, ADAM_B1, ADAM_B2, ADAM_EPS, ADAM_WD, ADAM_STEP = 0.001, 0.9, 0.999, 1e-08, 0.01, 10
VMEM_LIMIT = 56 * 1024 * 1024
HIGHEST = lax.Precision.HIGHEST


def _tile(dim, pref, mult=LANES):
    t = min(pref, dim)
    t -= t % mult
    while t >= mult:
        if dim % t == 0:
            return t
        t -= mult
    return dim


def _params(sem):
    return pltpu.CompilerParams(dimension_semantics=sem, vmem_limit_bytes=VMEM_LIMIT)


def rowwise(name, fn, rows, consts, outs, accs=(), ts=512, into=None):
    s = rows[0][0].shape[0]
    ts = min(ts, s)
    nr, nc, no, na = len(rows), len(consts), len(outs), len(accs)
    n_in = nr + nc + (into is not None)

    def body(*refs):
        vals = [r[...].astype(F32) for r in refs[:nr]] + [r[...] for r in refs[nr:nr + nc]]
        res = fn(*vals)
        if not isinstance(res, (tuple, list)):
            res = (res,)
        for r, v in zip(refs[n_in:n_in + no], res[:no]):
            r[...] = v.astype(r.dtype)
        if na:
            acc_refs = refs[n_in + no:]

            @pl.when(pl.program_id(0) == 0)
            def _():
                for r in acc_refs:
                    r[...] = jnp.zeros(r.shape, r.dtype)

            for r, v in zip(acc_refs, res[no:]):
                r[...] += v.astype(F32)

    in_specs = [pl.BlockSpec((ts, w), functools.partial(lambda i, cb: (i, cb), cb=cb)) for (_, cb, w) in rows]
    in_specs += [pl.BlockSpec(c.shape, lambda i: (0, 0)) for c in consts]
    out_specs = [pl.BlockSpec((ts, w), lambda i: (i, 0)) for (w, _) in outs]
    out_specs += [pl.BlockSpec(shp, lambda i: (0, 0)) for shp in accs]
    out_shape = [jax.ShapeDtypeStruct((s, w), dt) for (w, dt) in outs]
    out_shape += [jax.ShapeDtypeStruct(shp, F32) for shp in accs]
    extra, aliases = [], {}
    if into is not None:
        buf, cb_into = into
        assert buf.dtype == outs[0][1] and buf.shape[0] == s
        in_specs.append(pl.BlockSpec(memory_space=pl.ANY))
        out_specs[0] = pl.BlockSpec((ts, outs[0][0]), lambda i: (i, cb_into))
        out_shape[0] = jax.ShapeDtypeStruct(buf.shape, buf.dtype)
        extra, aliases = [buf], {nr + nc: 0}
    res = pl.pallas_call(
        body, name=name, grid=(s // ts,), in_specs=in_specs, out_specs=out_specs, out_shape=out_shape,
        input_output_aliases=aliases, compiler_params=_params(("arbitrary",) if na else ("parallel",)),
    )(*[a for (a, _, _) in rows], *consts, *extra)
    return res


def rowwise_vjp(name, f, rows, consts, cts, grad_dtypes, ts=512, into=None):
    nr, nc, nt = len(rows), len(consts), len(cts)
    keep = [i for i, dt in enumerate(grad_dtypes) if dt is not None]
    k_into = 0 if into is None else into[2]

    def g(*vals):
        rv = [v.astype(F32) for v in vals[:nr]]
        ctv = tuple(v.astype(F32) for v in vals[nr:nr + nt])
        cv = list(vals[nr + nt:])
        _, vjp = jax.vjp(lambda *a: tuple(f(*a)), *rv, *cv)
        grads = vjp(ctv)
        row_grads = [grads[i] for i in keep]
        if k_into:
            row_grads = [jnp.concatenate(row_grads[:k_into], axis=1)] + row_grads[k_into:]
        return tuple(row_grads) + tuple(grads[nr:])

    outs = [(rows[i][2], grad_dtypes[i]) for i in keep]
    if k_into:
        assert len({dt for _, dt in outs[:k_into]}) == 1
        outs = [(sum(w for w, _ in outs[:k_into]), outs[0][1])] + outs[k_into:]
    return rowwise(name, g, list(rows) + list(cts), consts, outs, accs=[c.shape for c in consts], ts=ts,
                   into=None if into is None else into[:2])


def matmul(name, a, b, *, ta=False, tb=False, out_dtypes=(F32,), epilogue=None, epi=(), tm=None, tn=None, tk=4096, dep=None, b_cols=None, out_blocks=None,
           f32_tail=False):
    m, k = (a.shape[1], a.shape[0]) if ta else a.shape
    n = b.shape[0] if tb else b.shape[1]
    b_col0 = 0
    if b_cols is not None:
        assert not tb
        b_col0, n = b_cols
    assert (b.shape[1] if tb else b.shape[0]) == k
    tk = _tile(k, tk)
    if tk > 1024:
        tm, tn = _tile(m, tm or 1024), _tile(n, tn or 1024)
    else:
        tm, tn = _tile(m, tm or 2048), _tile(n, tn or (1024 if m >= 2048 else 2048))
    if out_blocks is not None:
        tn = n // out_blocks
    assert b_col0 % tn == 0
    jb = b_col0 // tn
    nk = k // tk
    nj = n // tn
    assert not f32_tail or (nk == 1 and epilogue is None)
    ne, no = len(epi), len(out_dtypes)
    dims = (((0 if ta else 1,), (1 if tb else 0,)), ((), ()))

    def body(*refs):
        a_ref, b_ref = refs[0], refs[1]
        epi_refs = refs[2:2 + ne]
        n_in = 2 + ne + (dep is not None)
        out_refs = refs[n_in:n_in + no]
        part = lax.dot_general(a_ref[...].astype(BF16), b_ref[...].astype(BF16), dims, preferred_element_type=F32)

        def finish(acc):
            res = (acc,) if epilogue is None else epilogue(acc, *[r[...] for r in epi_refs])
            for r, v in zip(out_refs, res):
                r[...] = v.astype(r.dtype)

        if nk == 1:
            finish(part)
            if f32_tail:
                @pl.when(pl.program_id(1) == nj - 1)
                def _():
                    refs[n_in + no][...] = part
        else:
            acc_ref = refs[-1]
            kk = pl.program_id(2)

            @pl.when(kk == 0)
            def _():
                acc_ref[...] = part

            @pl.when(kk > 0)
            def _():
                acc_ref[...] += part

            @pl.when(kk == nk - 1)
            def _():
                finish(acc_ref[...])

    a_spec = pl.BlockSpec((tk, tm), lambda i, j, kk: (kk, i)) if ta else pl.BlockSpec((tm, tk), lambda i, j, kk: (i, kk))
    b_spec = pl.BlockSpec((tn, tk), lambda i, j, kk: (j, kk)) if tb else pl.BlockSpec((tk, tn), lambda i, j, kk: (kk, j + jb))
    epi_specs = []
    for (arr, col0) in epi:
        assert col0 % tn == 0
        epi_specs.append(pl.BlockSpec((tm, tn), functools.partial(lambda i, j, kk, c0: (i, j + c0), c0=col0 // tn)))
    res = pl.pallas_call(
        body, name=name, grid=(m // tm, n // tn, nk),
        in_specs=[a_spec, b_spec] + epi_specs + ([] if dep is None else [pl.BlockSpec(dep.shape, lambda i, j, kk: (0, 0))]),
        out_specs=[pl.BlockSpec((tm, tn), lambda i, j, kk: (i, j)) if out_blocks is None else
                   pl.BlockSpec((None, tm, tn), lambda i, j, kk: (j, i, 0)) for _ in out_dtypes]
        + ([pl.BlockSpec((tm, tn), lambda i, j, kk: (i, 0))] if f32_tail else []),
        out_shape=[jax.ShapeDtypeStruct((m, n) if out_blocks is None else (out_blocks, m, tn), dt) for dt in out_dtypes]
        + ([jax.ShapeDtypeStruct((m, tn), F32)] if f32_tail else []),
        scratch_shapes=[pltpu.VMEM((tm, tn), F32)] if nk > 1 else [],
        compiler_params=_params(("parallel", "arbitrary" if f32_tail else "parallel", "arbitrary")),
    )(a, b, *[arr for (arr, _) in epi], *([] if dep is None else [dep]))
    return res[0] if len(res) == 1 else res


def _rms(x, g):
    return x * lax.rsqrt(jnp.mean(x * x, axis=-1, keepdims=True) + NORM_EPS) * g


def _ln(x, g, b):
    mu = jnp.mean(x, axis=-1, keepdims=True)
    xc = x - mu
    var = jnp.mean(xc * xc, axis=-1, keepdims=True)
    return xc * lax.rsqrt(var + NORM_EPS) * g + b


def _gelu(x):
    return 0.5 * x * (1.0 + jnp.tanh(math.sqrt(2.0 / math.pi) * (x + 0.044715 * (x * x * x))))


def _sigmoid(x):
    return 1.0 / (1.0 + jnp.exp(-x))


def _silu(x):
    return x * _sigmoid(x)


def _log_sigmoid(x):
    return jnp.minimum(x, 0.0) - jnp.log(1.0 + jnp.exp(-jnp.abs(x)))


def _f_pre(x, g, sc, sh):
    return (_rms(x, g) * (1.0 + sc) + sh,)


def _f_post(y, g, gt):
    return (gt * _rms(y, g),)


def _f_a1(u_raw, v_raw, g, b):
    return _gelu(u_raw), _ln(_gelu(v_raw), g, b)


def _f_glu(val, gate):
    return (val * _sigmoid(gate),)


def _f_lnsilu(zc, g, b):
    return (_silu(_ln(zc, g, b)),)


def _f_merge(g0, g1, g2, ya, yb, yc):
    return (_sigmoid(g0) * ya + _sigmoid(g1) * yb + _sigmoid(g2) * yc,)


def _lane_lt64(shape):
    return lax.broadcasted_iota(jnp.int32, shape, 1) < HEAD_DIM


def spatial_fwd(vln, u, w_bf, b_exp, rows_per_step=512):
    s = vln.shape[0]
    tr = min(rows_per_step, s)

    def body(v_ref, u_ref, w_ref, b_ref, sv_ref, ya_ref):
        lo = _lane_lt64((CHUNK, LANES))
        for ch in range(tr // CHUNK):
            r0 = ch * CHUNK
            for p in range(D_MIX // LANES):
                vp = v_ref[r0:r0 + CHUNK, p * LANES:(p + 1) * LANES]
                o0 = jnp.dot(w_ref[2 * p], vp, preferred_element_type=F32)
                o1 = jnp.dot(w_ref[2 * p + 1], vp, preferred_element_type=F32)
                sv = jnp.where(lo, o0, o1) + b_ref[:, p * LANES:(p + 1) * LANES]
                sv_ref[r0:r0 + CHUNK, p * LANES:(p + 1) * LANES] = sv
                ya_ref[r0:r0 + CHUNK, p * LANES:(p + 1) * LANES] = (
                    u_ref[r0:r0 + CHUNK, p * LANES:(p + 1) * LANES] * sv).astype(BF16)

    row = pl.BlockSpec((tr, D_MIX), lambda i: (i, 0))
    return pl.pallas_call(
        body, name="spatial_fwd", grid=(s // tr,),
        in_specs=[row, row, pl.BlockSpec(w_bf.shape, lambda i: (0, 0, 0)), pl.BlockSpec(b_exp.shape, lambda i: (0, 0))],
        out_specs=[row, row],
        out_shape=[jax.ShapeDtypeStruct((s, D_MIX), F32), jax.ShapeDtypeStruct((s, D_MIX), BF16)],
        compiler_params=_params(("parallel",)),
    )(vln, u, w_bf, b_exp)


def spatial_bwd(dya, u, sv, vln, wt_bf, rows_per_step=512):
    s = vln.shape[0]
    tr = min(rows_per_step, s)
    ng = wt_bf.shape[0]

    def body(dya_ref, u_ref, sv_ref, v_ref, wt_ref, du_ref, dv_ref, dw_ref, db_ref):
        @pl.when(pl.program_id(0) == 0)
        def _():
            dw_ref[...] = jnp.zeros(dw_ref.shape, F32)
            db_ref[...] = jnp.zeros(db_ref.shape, F32)

        lo = _lane_lt64((CHUNK, LANES))
        for ch in range(tr // CHUNK):
            r0 = ch * CHUNK
            for p in range(D_MIX // LANES):
                cs = slice(p * LANES, (p + 1) * LANES)
                dya_p = dya_ref[r0:r0 + CHUNK, cs].astype(F32)
                du_ref[r0:r0 + CHUNK, cs] = dya_p * sv_ref[r0:r0 + CHUNK, cs]
                dsv = dya_p * u_ref[r0:r0 + CHUNK, cs]
                db_ref[:, cs] += dsv
                dsv0 = jnp.where(lo, dsv, 0.0).astype(BF16)
                dsv1 = jnp.where(lo, 0.0, dsv).astype(BF16)
                vp = v_ref[r0:r0 + CHUNK, cs]
                d0 = jnp.dot(wt_ref[2 * p], dsv0, preferred_element_type=F32)
                d1 = jnp.dot(wt_ref[2 * p + 1], dsv1, preferred_element_type=F32)
                dv_ref[r0:r0 + CHUNK, cs] = d0 + d1
                nt = (((1,), (1,)), ((), ()))
                dw_ref[2 * p] += lax.dot_general(dsv0, vp, nt, preferred_element_type=F32)
                dw_ref[2 * p + 1] += lax.dot_general(dsv1, vp, nt, preferred_element_type=F32)

    row = pl.BlockSpec((tr, D_MIX), lambda i: (i, 0))
    return pl.pallas_call(
        body, name="spatial_bwd", grid=(s // tr,),
        in_specs=[row, row, row, row, pl.BlockSpec(wt_bf.shape, lambda i: (0, 0, 0))],
        out_specs=[row, row, pl.BlockSpec((ng, CHUNK, CHUNK), lambda i: (0, 0, 0)), pl.BlockSpec((CHUNK, D_MIX), lambda i: (0, 0))],
        out_shape=[jax.ShapeDtypeStruct((s, D_MIX), F32), jax.ShapeDtypeStruct((s, D_MIX), F32),
                   jax.ShapeDtypeStruct((ng, CHUNK, CHUNK), F32), jax.ShapeDtypeStruct((CHUNK, D_MIX), F32)],
        compiler_params=_params(("arbitrary",)),
    )(dya, u, sv, vln, wt_bf)


def _windows(ref, first, count, ts):
    for r in range(8):
        ks = [k for k in range(count) if (first + k) % 8 == r]
        if ks:
            base = first + ks[0]
            blk = ref[base:base + ks[-1] - ks[0] + ts, :]
            for k in ks:
                yield k, blk[k - ks[0]:k - ks[0] + ts, :]


def conv_fwd(proj, cb_val, cb_gate, w_pad, cb, ln_g, ln_b, ts=256):
    s = proj.shape[0]
    ts = min(ts, s)
    per = ts // CONV_HALO

    def body(val_ref, gate_ref, pval_ref, pgate_ref, w_ref, cb_ref, g_ref, b_ref, zc_ref, yb_ref, ext_ref):
        i = pl.program_id(0)
        zprev = pval_ref[...].astype(F32) * _sigmoid(pgate_ref[...].astype(F32))
        ext_ref[0:CONV_HALO, :] = jnp.where(i > 0, zprev, 0.0)
        ext_ref[CONV_HALO:, :] = val_ref[...].astype(F32) * _sigmoid(gate_ref[...].astype(F32))
        acc = jnp.zeros((ts, D_MIX), F32)
        for j, win in _windows(ext_ref, CONV_HALO - (CONV_WIDTH - 1), CONV_WIDTH, ts):
            acc = acc + w_ref[j:j + 1, :] * win
        zc = acc + cb_ref[...]
        zc_ref[...] = zc
        yb_ref[...] = _f_lnsilu(zc, g_ref[...], b_ref[...])[0].astype(BF16)

    def cur(c):
        return pl.BlockSpec((ts, D_MIX), functools.partial(lambda i, c: (i, c), c=c))

    def prev(c):
        return pl.BlockSpec((CONV_HALO, D_MIX), functools.partial(lambda i, c: (jnp.maximum(i * per - 1, 0), c), c=c))

    const = lambda a: pl.BlockSpec(a.shape, lambda i: (0, 0))
    out = pl.BlockSpec((ts, D_MIX), lambda i: (i, 0))
    return pl.pallas_call(
        body, name="conv_fwd", grid=(s // ts,),
        in_specs=[cur(cb_val), cur(cb_gate), prev(cb_val), prev(cb_gate), const(w_pad), const(cb), const(ln_g), const(ln_b)],
        out_specs=[out, out],
        out_shape=[jax.ShapeDtypeStruct((s, D_MIX), F32), jax.ShapeDtypeStruct((s, D_MIX), BF16)],
        scratch_shapes=[pltpu.VMEM((CONV_HALO + ts, D_MIX), F32)],
        compiler_params=_params(("parallel",)),
    )(proj, proj, proj, proj, w_pad, cb, ln_g, ln_b)


def conv_bwd(proj, cb_val, cb_gate, zc, dyb, w_pad, ln_g, ln_b, dproj, ts=256):
    assert cb_gate == cb_val + 1 and cb_val % 2 == 0
    s = proj.shape[0]
    ts = min(ts, s)
    per = ts // CONV_HALO
    n_tiles = s // ts
    n_halo = s // CONV_HALO

    def body(val_ref, gate_ref, pval_ref, pgate_ref, zc_ref, dyb_ref, nzc_ref, ndyb_ref, w_ref, g_ref, b_ref, dproj_in,
             dvg_ref, dw_ref, dcb_ref, dg_ref, db_ref, zext_ref, dext_ref):
        i = pl.program_id(0)

        @pl.when(i == 0)
        def _():
            for r in (dw_ref, dcb_ref, dg_ref, db_ref):
                r[...] = jnp.zeros(r.shape, F32)

        g, b = g_ref[...], b_ref[...]
        _, vjp = jax.vjp(lambda z, gg, bb: _f_lnsilu(z, gg, bb)[0], zc_ref[...], g, b)
        dzc, dg, db = vjp(dyb_ref[...].astype(F32))
        dg_ref[...] += dg
        db_ref[...] += db
        dcb_ref[...] += jnp.sum(dzc, axis=0, keepdims=True)
        _, vjp_n = jax.vjp(lambda z: _f_lnsilu(z, g, b)[0], nzc_ref[...])
        (dzc_next,) = vjp_n(ndyb_ref[...].astype(F32))
        dext_ref[0:ts, :] = dzc
        dext_ref[ts:, :] = jnp.where(i < n_tiles - 1, dzc_next, 0.0)
        val, gate = val_ref[...].astype(F32), gate_ref[...].astype(F32)
        zprev = pval_ref[...].astype(F32) * _sigmoid(pgate_ref[...].astype(F32))
        zext_ref[0:CONV_HALO, :] = jnp.where(i > 0, zprev, 0.0)
        zext_ref[CONV_HALO:, :] = val * _sigmoid(gate)
        dz = jnp.zeros((ts, D_MIX), F32)
        for shift, win in _windows(dext_ref, 0, CONV_WIDTH, ts):
            j = CONV_WIDTH - 1 - shift
            dz = dz + w_ref[j:j + 1, :] * win
        for j, win in _windows(zext_ref, CONV_HALO - (CONV_WIDTH - 1), CONV_WIDTH, ts):
            dw_ref[j:j + 1, :] += jnp.sum(dzc * win, axis=0, keepdims=True)
        _, vjp_glu = jax.vjp(lambda a, c: _f_glu(a, c)[0], val, gate)
        dval, dgate = vjp_glu(dz)
        dvg_ref[:, :D_MIX] = dval.astype(BF16)
        dvg_ref[:, D_MIX:] = dgate.astype(BF16)

    def cur(c):
        return pl.BlockSpec((ts, D_MIX), functools.partial(lambda i, c: (i, c), c=c))

    def prev(c):
        return pl.BlockSpec((CONV_HALO, D_MIX), functools.partial(lambda i, c: (jnp.maximum(i * per - 1, 0), c), c=c))

    nxt = pl.BlockSpec((CONV_HALO, D_MIX), lambda i: (jnp.minimum((i + 1) * per, n_halo - 1), 0))
    const = lambda a: pl.BlockSpec(a.shape, lambda i: (0, 0))
    out = pl.BlockSpec((ts, D_MIX), lambda i: (i, 0))
    vec = pl.BlockSpec((1, D_MIX), lambda i: (0, 0))
    return pl.pallas_call(
        body, name="conv_bwd", grid=(n_tiles,),
        in_specs=[cur(cb_val), cur(cb_gate), prev(cb_val), prev(cb_gate), out, out, nxt, nxt, const(w_pad), const(ln_g), const(ln_b),
                  pl.BlockSpec(memory_space=pl.ANY)],
        out_specs=[pl.BlockSpec((ts, 2 * D_MIX), lambda i: (i, cb_val // 2)), pl.BlockSpec((CONV_HALO, D_MIX), lambda i: (0, 0)), vec, vec, vec],
        out_shape=[jax.ShapeDtypeStruct(dproj.shape, dproj.dtype),
                   jax.ShapeDtypeStruct((CONV_HALO, D_MIX), F32)] + [jax.ShapeDtypeStruct((1, D_MIX), F32)] * 3,
        scratch_shapes=[pltpu.VMEM((CONV_HALO + ts, D_MIX), F32), pltpu.VMEM((ts + CONV_HALO, D_MIX), F32)],
        input_output_aliases={11: 0}, compiler_params=_params(("arbitrary",)),
    )(proj, proj, proj, proj, zc, dyb, zc, dyb, w_pad, ln_g, ln_b, dproj)


def forget_cumsum(proj, cb_f, bf_exp, t=256):
    s = proj.shape[0]
    t = min(t, s)

    def body(f_ref, bf_ref, out_ref, carry_ref):
        @pl.when(pl.program_id(0) == 0)
        def _():
            carry_ref[...] = jnp.zeros(carry_ref.shape, F32)

        lf = _log_sigmoid(f_ref[...] + bf_ref[...])
        tri = (lax.broadcasted_iota(jnp.int32, (t, t), 1) <= lax.broadcasted_iota(jnp.int32, (t, t), 0)).astype(F32)
        c = jnp.dot(tri, lf, precision=HIGHEST, preferred_element_type=F32) + carry_ref[...]
        out_ref[...] = c
        carry_ref[...] = c[t - 1:t, :]

    return pl.pallas_call(
        body, name="forget_cumsum", grid=(s // t,),
        in_specs=[pl.BlockSpec((t, D_MIX), functools.partial(lambda i, c: (i, c), c=cb_f)), pl.BlockSpec((1, D_MIX), lambda i: (0, 0))],
        out_specs=pl.BlockSpec((t, D_MIX), lambda i: (i, 0)),
        out_shape=jax.ShapeDtypeStruct((s, D_MIX), F32),
        scratch_shapes=[pltpu.VMEM((1, D_MIX), F32)],
        compiler_params=_params(("arbitrary",)),
    )(proj, bf_exp)


def forget_bwd(proj, cb_f, bf_exp, dcum, dproj, cb_out, t=256):
    s = proj.shape[0]
    t = min(t, s)
    n = s // t

    def body(f_ref, bf_ref, dc_ref, dproj_in, df_ref, dbf_ref, carry_ref):
        @pl.when(pl.program_id(0) == 0)
        def _():
            carry_ref[...] = jnp.zeros(carry_ref.shape, F32)
            dbf_ref[...] = jnp.zeros(dbf_ref.shape, F32)

        tri = (lax.broadcasted_iota(jnp.int32, (t, t), 1) >= lax.broadcasted_iota(jnp.int32, (t, t), 0)).astype(F32)
        r = jnp.dot(tri, dc_ref[...], precision=HIGHEST, preferred_element_type=F32) + carry_ref[...]
        carry_ref[...] = r[0:1, :]
        df = r * _sigmoid(-(f_ref[...] + bf_ref[...]))
        dbf_ref[...] += jnp.sum(df, axis=0, keepdims=True)
        live = lax.broadcasted_iota(jnp.int32, (t, D_MIX), 1) % HEAD_DIM == 0
        df_ref[...] = jnp.where(live, df, 0.0).astype(BF16)

    return pl.pallas_call(
        body, name="forget_bwd", grid=(n,),
        in_specs=[pl.BlockSpec((t, D_MIX), functools.partial(lambda i, c: (n - 1 - i, c), c=cb_f)), pl.BlockSpec((1, D_MIX), lambda i: (0, 0)),
                  pl.BlockSpec((t, D_MIX), lambda i: (n - 1 - i, 0)), pl.BlockSpec(memory_space=pl.ANY)],
        out_specs=[pl.BlockSpec((t, D_MIX), lambda i: (n - 1 - i, cb_out)), pl.BlockSpec((1, D_MIX), lambda i: (0, 0))],
        out_shape=[jax.ShapeDtypeStruct(dproj.shape, dproj.dtype), jax.ShapeDtypeStruct((1, D_MIX), F32)],
        scratch_shapes=[pltpu.VMEM((1, D_MIX), F32)],
        input_output_aliases={3: 0}, compiler_params=_params(("arbitrary",)),
    )(proj, bf_exp, dcum, dproj)


NT = (((1,), (1,)), ((), ()))
LOG2E = math.log2(math.e)
N_PAIR = D_MIX // LANES


def _split3(x):
    hi = x.astype(BF16).astype(F32)
    mid = (x - hi).astype(BF16).astype(F32)
    return hi, mid, x - hi - mid


def _triple(li, first, vals):
    out = jnp.where(li == first, vals[0], 0.0)
    for i in (1, 2):
        out = jnp.where(li == first + i, vals[i], out)
    return out


def _lane_ids(shape):
    lane = lax.broadcasted_iota(jnp.int32, shape, 1)
    return lane, lane % HEAD_DIM, lane < HEAD_DIM


def attn_prep(proj, cb_q, cum, ts=512):
    s = proj.shape[0]
    ts = min(ts, s)
    scale = LOG2E / math.sqrt(HEAD_DIM)

    def body(q_ref, k_ref, v_ref, c_ref, qe_ref, qo_ref, ke_ref, ko_ref, ve_ref, vo_ref):
        _, li, lo = _lane_ids((ts, LANES))
        one3 = lambda first: ((li >= first) & (li < first + 3)).astype(F32)
        for p in range(N_PAIR):
            ps = slice(p * LANES, (p + 1) * LANES)
            c3 = _split3(pltpu.roll(c_ref[:, ps] * LOG2E, HEAD_DIM, axis=1))
            eq = _triple(li, 0, c3) + one3(3)
            ek = one3(0) - _triple(li, 3, c3) + one3(6)
            ev = one3(0)
            for src, even, odd, extra, mul in ((q_ref, qe_ref, qo_ref, eq, scale), (k_ref, ke_ref, ko_ref, ek, 1.0), (v_ref, ve_ref, vo_ref, ev, 1.0)):
                x = src[:, ps].astype(F32) * mul
                even[:, ps] = jnp.where(lo, x, extra).astype(BF16)
                odd[:, ps] = jnp.where(lo, extra, x).astype(BF16)

    col = lambda c: pl.BlockSpec((ts, D_MIX), functools.partial(lambda i, c: (i, c), c=c))
    out = pl.BlockSpec((ts, D_MIX), lambda i: (i, 0))
    return pl.pallas_call(
        body, name="attn_prep", grid=(s // ts,),
        in_specs=[col(cb_q), col(cb_q + 1), col(cb_q + 2), pl.BlockSpec((ts, D_MIX), lambda i: (i, 0))],
        out_specs=[out] * 6, out_shape=[jax.ShapeDtypeStruct((s, D_MIX), BF16)] * 6,
        compiler_params=_params(("parallel",)),
    )(proj, proj, proj, cum)


def _pair_specs(s, t):
    return pl.BlockSpec((t, LANES), lambda p, i: (i, p)), pl.BlockSpec((s, LANES), lambda p, i: (0, p))


def attn_fwd(qe, qo, ke, ko, ve, vo, tq=512):
    s = qe.shape[0]
    tq = min(tq, s)

    def body(qe_ref, qo_ref, ke_ref, ko_ref, ve_ref, vo_ref, o_ref, qbe_ref, qbo_ref):
        qi = pl.program_id(1)
        qs, k_refs, v_refs = (qe_ref[...], qo_ref[...]), (ke_ref, ko_ref), (ve_ref, vo_ref)
        causal = lax.broadcasted_iota(jnp.int32, (tq, tq), 1) <= lax.broadcasted_iota(jnp.int32, (tq, tq), 0)

        def step(j, carry, diag):
            ks = pl.multiple_of(j * tq, tq)
            new = []
            for h in range(2):
                m, l, acc = carry[h]
                sc = lax.dot_general(qs[h], k_refs[h][pl.ds(ks, tq), :], NT, preferred_element_type=F32)
                if diag:
                    sc = jnp.where(causal, sc, -jnp.inf)
                m_new = jnp.maximum(m, jnp.max(sc, axis=1, keepdims=True))
                p = jnp.exp2(sc - m_new)
                alpha = jnp.exp2(m - m_new)
                l = alpha * l + jnp.sum(p, axis=1, keepdims=True)
                acc = alpha * acc + jnp.dot(p.astype(BF16), v_refs[h][pl.ds(ks, tq), :], preferred_element_type=F32)
                new.append((m_new, l, acc))
            return tuple(new)

        init = tuple((jnp.full((tq, 1), -jnp.inf, F32), jnp.zeros((tq, 1), F32), jnp.zeros((tq, LANES), F32)) for _ in range(2))
        carry = lax.fori_loop(0, qi, lambda j, c: step(j, c, False), init)
        (m0, l0, a0), (m1, l1, a1) = step(qi, carry, True)
        _, li, lo = _lane_ids((tq, LANES))
        o_ref[...] = jnp.where(lo, a0 / l0, a1 / l1).astype(o_ref.dtype)
        lse_lanes = (li >= 6) & (li < 9)
        for q, m, l, spare, out_ref in ((qs[0], m0, l0, ~lo, qbe_ref), (qs[1], m1, l1, lo, qbo_ref)):
            neg_lse = _triple(li, 6, _split3(-(m + jnp.log(l) * LOG2E)))
            out_ref[...] = jnp.where(spare & lse_lanes, neg_lse.astype(BF16), q)

    blk, full = _pair_specs(s, tq)
    return pl.pallas_call(
        body, name="attn_fwd", grid=(N_PAIR, s // tq),
        in_specs=[blk, blk, full, full, full, full],
        out_specs=[blk] * 3, out_shape=[jax.ShapeDtypeStruct((s, D_MIX), BF16)] * 3,
        compiler_params=_params(("parallel", "parallel")),
    )(qe, qo, ke, ko, ve, vo)


def attn_dq(qbe, qbo, ke, ko, ve, vo, do, dep, dproj, lane_block, tq=512):
    s = qbe.shape[0]
    tq = min(tq, s)
    scale = 1.0 / math.sqrt(HEAD_DIM)

    def body(qe_ref, qo_ref, ke_ref, ko_ref, ve_ref, vo_ref, do_ref, dep_ref, dproj_in, dq_ref, dobe_ref, dobo_ref):
        qi = pl.program_id(1)
        _, li, lo = _lane_ids((tq, LANES))
        do_ = do_ref[...]
        qs, k_refs, v_refs = (qe_ref[...], qo_ref[...]), (ke_ref, ko_ref), (ve_ref, vo_ref)
        dos = (jnp.where(lo, do_, 0), jnp.where(lo, 0, do_))
        causal = lax.broadcasted_iota(jnp.int32, (tq, tq), 1) <= lax.broadcasted_iota(jnp.int32, (tq, tq), 0)

        def step(j, carry, diag):
            ks = pl.multiple_of(j * tq, tq)
            new = []
            for h in range(2):
                pdpk, pk, dsum = carry[h]
                kb = k_refs[h][pl.ds(ks, tq), :]
                sc = lax.dot_general(qs[h], kb, NT, preferred_element_type=F32)
                if diag:
                    sc = jnp.where(causal, sc, -jnp.inf)
                p = jnp.exp2(sc)
                pdp = p * lax.dot_general(dos[h], v_refs[h][pl.ds(ks, tq), :], NT, preferred_element_type=F32)
                new.append((pdpk + jnp.dot(pdp.astype(BF16), kb, preferred_element_type=F32),
                            pk + jnp.dot(p.astype(BF16), kb, preferred_element_type=F32),
                            dsum + jnp.sum(pdp, axis=1, keepdims=True)))
            return tuple(new)

        init = tuple((jnp.zeros((tq, LANES), F32), jnp.zeros((tq, LANES), F32), jnp.zeros((tq, 1), F32)) for _ in range(2))
        carry = lax.fori_loop(0, qi, lambda j, c: step(j, c, False), init)
        (a0, b0, s0), (a1, b1, s1) = step(qi, carry, True)
        dq_ref[...] = (jnp.where(lo, a0 - s0 * b0, a1 - s1 * b1) * scale).astype(dq_ref.dtype)
        dobe_ref[...] = jnp.where(lo, do_, _triple(li, 0, _split3(-s0)).astype(BF16))
        dobo_ref[...] = jnp.where(lo, _triple(li, 0, _split3(-s1)).astype(BF16), do_)

    blk, full = _pair_specs(s, tq)
    return pl.pallas_call(
        body, name="attn_dq", grid=(N_PAIR, s // tq),
        in_specs=[blk, blk, full, full, full, full, blk, pl.BlockSpec(dep.shape, lambda p, i: (0, 0)), pl.BlockSpec(memory_space=pl.ANY)],
        out_specs=[pl.BlockSpec((tq, LANES), lambda p, i: (i, lane_block + p)), blk, blk],
        out_shape=[jax.ShapeDtypeStruct(dproj.shape, dproj.dtype)] + [jax.ShapeDtypeStruct((s, D_MIX), BF16)] * 2,
        input_output_aliases={8: 0}, compiler_params=_params(("parallel", "parallel")),
    )(qbe, qbo, ke, ko, ve, vo, do, dep, dproj)


def attn_dkv(ke, ko, ve, vo, qbe, qbo, dobe, dobo, dproj, lane_block, tk=512):
    s = ke.shape[0]
    tk = min(tk, s)
    nq = s // tk

    def body(ke_ref, ko_ref, ve_ref, vo_ref, qe_ref, qo_ref, de_ref, do_ref, dproj_in, dk_ref, dv_ref, dck_ref):
        kj = pl.program_id(1)
        lo = _lane_lt64((tk, LANES))
        ks_, vs_, q_refs, d_refs = (ke_ref[...], ko_ref[...]), (ve_ref[...], vo_ref[...]), (qe_ref, qo_ref), (de_ref, do_ref)
        causal = lax.broadcasted_iota(jnp.int32, (tk, tk), 0) <= lax.broadcasted_iota(jnp.int32, (tk, tk), 1)

        def step(i, carry, diag):
            qs = pl.multiple_of(i * tk, tk)
            new = []
            for h in range(2):
                dk, dv, dck = carry[h]
                qblk = q_refs[h][pl.ds(qs, tk), :]
                dblk = d_refs[h][pl.ds(qs, tk), :]
                st = lax.dot_general(ks_[h], qblk, NT, preferred_element_type=F32)
                if diag:
                    st = jnp.where(causal, st, -jnp.inf)
                pt = jnp.exp2(st)
                dst = pt * lax.dot_general(vs_[h], dblk, NT, preferred_element_type=F32)
                new.append((dk + jnp.dot(dst.astype(BF16), qblk, preferred_element_type=F32),
                            dv + jnp.dot(pt.astype(BF16), dblk, preferred_element_type=F32),
                            dck - jnp.sum(dst, axis=1, keepdims=True)))
            return tuple(new)

        init = tuple((jnp.zeros((tk, LANES), F32), jnp.zeros((tk, LANES), F32), jnp.zeros((tk, 1), F32)) for _ in range(2))
        carry = step(kj, init, True)
        (dk0, dv0, dc0), (dk1, dv1, dc1) = lax.fori_loop(kj + 1, nq, lambda i, c: step(i, c, False), carry)
        dk_ref[...] = (jnp.where(lo, dk0, dk1) * (1.0 / LOG2E)).astype(dk_ref.dtype)
        dv_ref[...] = jnp.where(lo, dv0, dv1).astype(dv_ref.dtype)
        dck_ref[...] = jnp.where(lo, dc0, dc1)

    blk, full = _pair_specs(s, tk)
    return pl.pallas_call(
        body, name="attn_dkv", grid=(N_PAIR, nq),
        in_specs=[blk, blk, blk, blk, full, full, full, full, pl.BlockSpec(memory_space=pl.ANY)],
        out_specs=[pl.BlockSpec((tk, LANES), lambda p, i: (i, lane_block + p)), blk, blk],
        out_shape=[jax.ShapeDtypeStruct(dproj.shape, dproj.dtype), jax.ShapeDtypeStruct((s, D_MIX), BF16), jax.ShapeDtypeStruct((s, D_MIX), F32)],
        input_output_aliases={8: 0}, compiler_params=_params(("parallel", "parallel")),
    )(ke, ko, ve, vo, qbe, qbo, dobe, dobo, dproj)


def _pre_bwd(name, x, g, sc, sh, dh, dres):
    d = x.shape[1]

    def fn(xv, dhv, dresv, gv, scv, shv):
        _, vjp = jax.vjp(lambda *a: _f_pre(*a)[0], xv, gv, scv, shv)
        dx, dg, dsc, dsh = vjp(dhv.astype(F32))
        return dx + dresv, dg, dsc, dsh

    return rowwise(name, fn, [(x, 0, d), (dh, 0, d), (dres, 0, d)], [g, sc, sh], [(d, F32)], accs=[(1, d)] * 3)


def layer_fwd(x, mod, layer):
    s, d = x.shape
    m = D_MIX
    w = dict(layer["small"])
    sh1, sc1, gt1, sh2, sc2, gt2 = (mod[i:i + 1] for i in range(6))
    cb = 3 * d // m
    (h,) = rowwise("pre1", _f_pre, [(x, 0, d)], [w["mix_pre_g"], sc1, sh1], [(d, BF16)])
    w.update(layer["get_in"](h))
    proj, fproj = matmul("w_in", h, w["w_in"], out_dtypes=(BF16,), f32_tail=True, tn=2 * m)
    fcb = 1
    w.update(layer["get_abco"](proj))
    u, vln = rowwise("gmlp_in", _f_a1, [(proj, cb, m), (proj, cb + 1, m)], [w["gmlp_ln_g"], w["gmlp_ln_b"]], [(m, F32), (m, BF16)])
    sv, ya = spatial_fwd(vln, u, w["ws"], w["bs_exp"])
    y_a = matmul("w_a", ya, w["w_a_out"], out_dtypes=(BF16,))
    zc, yb = conv_fwd(proj, cb + 2, cb + 3, w["conv_w"], w["conv_b"], w["conv_ln_g"], w["conv_ln_b"])
    y_b = matmul("w_b", yb, w["w_b_out"], out_dtypes=(BF16,))
    cum = forget_cumsum(fproj, fcb, w["bf_exp"])
    kv_ops = attn_prep(proj, cb + 4, cum)
    o, qbe, qbo = attn_fwd(*kv_ops)
    att = (qbe, qbo) + tuple(kv_ops[2:])
    y_c = matmul("w_c", o, w["w_c_out"], out_dtypes=(BF16,))
    (merged,) = rowwise("merge", _f_merge, [(proj, 0, d), (proj, 1, d), (proj, 2, d), (y_a, 0, d), (y_b, 0, d), (y_c, 0, d)], [], [(d, BF16)])
    y = matmul("w_out", merged, w["w_out"], out_dtypes=(BF16,))
    w.update(layer["get_mlp"](y))

    def post_pre(xv, yv, gp, gt, g2, sc, sh):
        x1 = xv + _f_post(yv, gp, gt)[0]
        return x1, _f_pre(x1, g2, sc, sh)[0]

    x1, h2 = rowwise("post1", post_pre, [(x, 0, d), (y, 0, d)], [w["mix_post_g"], gt1, w["mlp_pre_g"], sc2, sh2], [(d, F32), (d, BF16)])
    r = matmul("w1", h2, w["mlp_w1"], out_dtypes=(BF16,), epilogue=lambda acc: (jnp.square(jnp.maximum(acc, 0.0)),))
    y2 = matmul("w2", r, w["mlp_w2"], out_dtypes=(BF16,))
    (x2,) = rowwise("post2", lambda xv, yv, g, gt: xv + _f_post(yv, g, gt)[0], [(x1, 0, d), (y2, 0, d)], [w["mlp_post_g"], gt2], [(d, F32)])
    saved = dict(w=w, x=x, h=h, proj=proj, fproj=fproj, fcb=fcb, u=u, vln=vln, sv=sv, ya=ya, y_a=y_a, zc=zc, yb=yb, y_b=y_b, att=att,
                 o=o, y_c=y_c, merged=merged, y=y, x1=x1, h2=h2, r=r, y2=y2)
    return x2, saved


def layer_bwd(dx2, mod, sv, emit, tok_in=None):
    x, proj, w = sv["x"], sv["proj"], sv["w"]
    s, d = x.shape
    m = D_MIX
    sh1, sc1, gt1, sh2, sc2, gt2 = (mod[i:i + 1] for i in range(6))
    cb = 3 * d // m
    g = {}
    if tok_in is not None:
        gt2 = gt2 + tok_in[0:1, 0:1]
    dy2, g["mlp_post_g"], dgt2 = rowwise_vjp("post2_b", _f_post, [(sv["y2"], 0, d)], [w["mlp_post_g"], gt2], [(dx2, 0, d)], [BF16])
    da = matmul("w2_dx", dy2, w["mlp_w2"], tb=True, out_dtypes=(BF16,),
                epilogue=lambda acc, r: (acc * (2.0 * jnp.sqrt(r.astype(F32))),), epi=[(sv["r"], 0)])
    big = {}
    big["mlp_w2"] = matmul("w2_dw", sv["r"], dy2, ta=True, out_dtypes=(BF16,))
    dh2 = matmul("w1_dx", da, w["mlp_w1"], tb=True, out_dtypes=(BF16,))
    big["mlp_w1"] = matmul("w1_dw", sv["h2"], da, ta=True, out_dtypes=(BF16,), out_blocks=N_DEV)
    tok = emit("mlp", big)
    dx1, g["mlp_pre_g"], dsc2, dsh2 = _pre_bwd("pre2_b", sv["x1"], w["mlp_pre_g"], sc2 + tok[0:1, 0:1], sh2, dh2, dx2)
    dy, g["mix_post_g"], dgt1 = rowwise_vjp("post1_b", _f_post, [(sv["y"], 0, d)], [w["mix_post_g"], gt1], [(dx1, 0, d)], [BF16])
    dmerged = matmul("w_out_dx", dy, w["w_out"], tb=True, out_dtypes=(BF16,))
    big = {}
    big["w_out"] = matmul("w_out_dw", sv["merged"], dy, ta=True, out_dtypes=(BF16,))
    assert (3 * d) % (2 * m) == 0
    dproj = lax.empty((s, 3 * d + 8 * m), BF16)
    dproj, dya_, dyb_, dyc_ = rowwise_vjp(
        "merge_b", _f_merge, [(proj, 0, d), (proj, 1, d), (proj, 2, d), (sv["y_a"], 0, d), (sv["y_b"], 0, d), (sv["y_c"], 0, d)], [],
        [(dmerged, 0, d)], [BF16] * 6, into=(dproj, 0, 3))
    dya_pre = matmul("w_a_dx", dya_, w["w_a_out"], tb=True, out_dtypes=(BF16,))
    big["w_a_out"] = matmul("w_a_dw", sv["ya"], dya_, ta=True, out_dtypes=(BF16,))
    dyb_pre = matmul("w_b_dx", dyb_, w["w_b_out"], tb=True, out_dtypes=(BF16,))
    big["w_b_out"] = matmul("w_b_dw", sv["yb"], dyb_, ta=True, out_dtypes=(BF16,))
    do = matmul("w_c_dx", dyc_, w["w_c_out"], tb=True, out_dtypes=(BF16,))
    big["w_c_out"] = matmul("w_c_dw", sv["o"], dyc_, ta=True, out_dtypes=(BF16,))
    tok = emit("abco", big)
    qbe, qbo, ke, ko, ve, vo = sv["att"]
    lane0 = (cb + 4) * (m // LANES)
    dproj, dobe, dobo = attn_dq(qbe, qbo, ke, ko, ve, vo, do, tok, dproj, lane0)
    dproj, dv, dcum = attn_dkv(ke, ko, ve, vo, qbe, qbo, dobe, dobo, dproj, lane0 + N_PAIR)
    dproj = lax.dynamic_update_slice(dproj, dv, (0, (cb + 6) * m))
    dproj, dbf = forget_bwd(sv["fproj"], sv["fcb"], w["bf_exp"], dcum, dproj, cb + 7)
    g["fox_bf"] = dbf[0, ::HEAD_DIM]
    dproj, dwc, g["conv_b"], g["conv_ln_g"], g["conv_ln_b"] = conv_bwd(
        proj, cb + 2, cb + 3, sv["zc"], dyb_pre, w["conv_w"], w["conv_ln_g"], w["conv_ln_b"], dproj)
    g["conv_w"] = dwc[:CONV_WIDTH]
    du, dvln, dws, dbexp = spatial_bwd(dya_pre, sv["u"], sv["sv"], sv["vln"], w["ws_t"])
    g["gmlp_ws"] = dws * jnp.tril(jnp.ones((CHUNK, CHUNK), F32))
    g["gmlp_bs"] = dbexp.reshape(CHUNK, m // GROUP_DIM, GROUP_DIM).sum(-1).T
    dproj, g["gmlp_ln_g"], g["gmlp_ln_b"] = rowwise_vjp(
        "gmlp_in_b", _f_a1, [(proj, cb, m), (proj, cb + 1, m)], [w["gmlp_ln_g"], w["gmlp_ln_b"]], [(du, 0, m), (dvln, 0, m)], [BF16, BF16],
        into=(dproj, cb // 2, 2))
    tok = emit("in", {"w_in": matmul("w_in_dw", sv["h"], dproj, ta=True, out_dtypes=(BF16,))})
    dh = matmul("w_in_dx", dproj, w["w_in"], tb=True, dep=tok, out_dtypes=(BF16,))
    dx, g["mix_pre_g"], dsc1, dsh1 = _pre_bwd("pre1_b", x, w["mix_pre_g"], sc1, sh1, dh, dx1)
    dmod = jnp.concatenate([dsh1, dsc1, dgt1, dsh2, dsc2, dgt2], axis=0)
    return dx, dmod, g


def local_step(x, target, mods, layers):
    d = x.shape[1]
    saved = []
    for l in range(len(layers)):
        x, sv = layer_fwd(x, mods[l], layers[l])
        saved.append(sv)

    def loss_fn(xv, tv):
        err = xv - tv
        return err * (1.0 / d), jnp.sum(err * err, axis=0, keepdims=True)

    dx, sq = rowwise("loss", loss_fn, [(x, 0, d), (target, 0, d)], [], [(d, F32)], accs=[(1, d)])
    loss = (0.5 / d) * jnp.sum(sq)
    dmods, grads = [None] * len(layers), [None] * len(layers)
    tok = None
    for l in reversed(range(len(layers))):
        dx, dmods[l], grads[l] = layer_bwd(dx, mods[l], saved[l], layers[l]["emit"], tok)
        tok = layers[l]["emit_small"](dmods[l], grads[l])
    return loss, dx, dmods, grads


def exchange(name, arrs, scatter):
    n = len(arrs)

    def body(*refs):
        in_refs, out_refs = refs[:n], refs[n:2 * n]
        send_sems, recv_sems, local_sems = refs[2 * n:]
        x, y, c = lax.axis_index("x"), lax.axis_index("y"), lax.axis_index("c")
        me = 4 * x + 2 * y + c
        local = []
        for a in range(n):
            src = in_refs[a].at[me] if scatter else in_refs[a]
            cp = pltpu.make_async_copy(src, out_refs[a].at[me], local_sems.at[a])
            cp.start()
            local.append(cp)
        remote = []
        for k in range(1, N_DEV):
            px, py, pc = x ^ ((k >> 2) & 1), y ^ ((k >> 1) & 1), c ^ (k & 1)
            peer = 4 * px + 2 * py + pc
            for a in range(n):
                src = in_refs[a].at[peer] if scatter else in_refs[a]
                cp = pltpu.make_async_remote_copy(
                    src_ref=src, dst_ref=out_refs[a].at[me], send_sem=send_sems.at[a * (N_DEV - 1) + k - 1],
                    recv_sem=recv_sems.at[a * (N_DEV - 1) + k - 1], device_id=(px, py, pc), device_id_type=MESH)
                cp.start()
                remote.append(cp)
        for cp in remote:
            cp.wait()
        for cp in local:
            cp.wait()

    hbm = pl.BlockSpec(memory_space=pltpu.HBM)
    out_shape = [jax.ShapeDtypeStruct(a.shape if scatter else (N_DEV,) + a.shape, a.dtype) for a in arrs]
    return pl.pallas_call(
        body, name=name, in_specs=[hbm] * n, out_specs=[hbm] * n, out_shape=out_shape,
        scratch_shapes=[pltpu.SemaphoreType.DMA((n * (N_DEV - 1),)), pltpu.SemaphoreType.DMA((n * (N_DEV - 1),)),
                        pltpu.SemaphoreType.DMA((n,))],
    )(*arrs)


def _peers(x, y, c):
    out = []
    for k in range(1, N_DEV):
        px, py, pc = x ^ ((k >> 2) & 1), y ^ ((k >> 1) & 1), c ^ (k & 1)
        out.append((k - 1, (px, py, pc), 4 * px + 2 * py + pc))
    return out


def _exchange_copies(srcs, lands, send_sems, recv_sems, scatter):
    x, y, c = lax.axis_index("x"), lax.axis_index("y"), lax.axis_index("c")
    me = 4 * x + 2 * y + c
    copies = []
    for slot, pos, peer in _peers(x, y, c):
        for a, (src, land) in enumerate(zip(srcs, lands)):
            copies.append(pltpu.make_async_remote_copy(
                src_ref=src.at[peer] if scatter else src, dst_ref=land.at[me],
                send_sem=send_sems.at[a * (N_DEV - 1) + slot], recv_sem=recv_sems.at[a * (N_DEV - 1) + slot],
                device_id=pos, device_id_type=MESH))
    return me, copies


def exchange_start(name, arrs, scatter, after=None):
    n = len(arrs)
    lands = [lax.empty(a.shape if scatter else (N_DEV,) + a.shape, a.dtype) for a in arrs]
    n_in = 2 * n + (after is not None)

    def body(*refs):
        srcs, lands_ = refs[:n], refs[n:2 * n]
        send_sems, recv_sems = refs[n_in], refs[n_in + 1]
        token = refs[n_in + 2 + 2 * n]
        _, copies = _exchange_copies(srcs, lands_, send_sems, recv_sems, scatter)
        for cp in copies:
            cp.start()
        token[...] = jnp.zeros(token.shape, token.dtype)

    hbm = pl.BlockSpec(memory_space=pltpu.HBM)
    sem = pl.BlockSpec(memory_space=pltpu.SEMAPHORE)
    n_sem = n * (N_DEV - 1)
    res = pl.pallas_call(
        body, name=name,
        out_shape=(pltpu.SemaphoreType.DMA((n_sem,)), pltpu.SemaphoreType.DMA((n_sem,)),
                   *[pltpu.HBM(a.shape, a.dtype) for a in arrs], *[pltpu.HBM(l.shape, l.dtype) for l in lands],
                   jax.ShapeDtypeStruct((8, LANES), F32)),
        in_specs=[hbm] * (2 * n) + ([] if after is None else [pl.BlockSpec(memory_space=pl.ANY)]),
        out_specs=(sem, sem, *([hbm] * (2 * n)), pl.BlockSpec(memory_space=pltpu.VMEM)),
        input_output_aliases={i: 2 + i for i in range(2 * n)},
        compiler_params=pltpu.CompilerParams(has_side_effects=pltpu.SideEffectType.DATAFLOW_SIDE_EFFECTING),
    )(*[pltpu.with_memory_space_constraint(a, pltpu.HBM) for a in arrs],
      *[pltpu.with_memory_space_constraint(l, pltpu.HBM) for l in lands], *([] if after is None else [after]))
    return dict(n=n, scatter=scatter, send=res[0], recv=res[1], srcs=res[2:2 + n], lands=res[2 + n:2 + 2 * n], token=res[2 + 2 * n])


def exchange_wait(name, st, after):
    n, scatter = st["n"], st["scatter"]
    after = list(after) if isinstance(after, (list, tuple)) else [after]

    def body(*refs):
        srcs, lands_ = refs[:n], refs[n:2 * n]
        send_sems, recv_sems = refs[2 * n], refs[2 * n + 1]
        _, copies = _exchange_copies(srcs, lands_, send_sems, recv_sems, scatter)
        for cp in copies:
            cp.wait_send()
            cp.wait_recv()

    hbm = pl.BlockSpec(memory_space=pltpu.HBM)
    sem = pl.BlockSpec(memory_space=pltpu.SEMAPHORE)
    res = pl.pallas_call(
        body, name=name,
        out_shape=tuple(pltpu.HBM(a.shape, a.dtype) for a in (*st["srcs"], *st["lands"])),
        in_specs=[hbm] * (2 * n) + [sem, sem] + [pl.BlockSpec(memory_space=pl.ANY)] * len(after), out_specs=tuple([hbm] * (2 * n)),
        input_output_aliases={i: i for i in range(2 * n)},
        compiler_params=pltpu.CompilerParams(has_side_effects=pltpu.SideEffectType.DATAFLOW_SIDE_EFFECTING),
    )(*st["srcs"], *st["lands"], st["send"], st["recv"], *after)
    return list(res[:n]), list(res[n:])


def adamw_sum(name, parts, w, m, v, tr=256):
    nl = len(parts)
    k, r, c = parts[0].shape
    tr = _tile(r, tr, 16)
    c1 = 1.0 - ADAM_B1 ** ADAM_STEP
    c2 = 1.0 - ADAM_B2 ** ADAM_STEP

    def body(*refs):
        p_refs = refs[:nl]
        w_ref, m_ref, v_ref, g_ref, d_ref, nm_ref, nv_ref = refs[nl:]
        for l in range(nl):
            @pl.when(pl.program_id(0) == l)
            def _(p_ref=p_refs[l]):
                grad = p_ref[0].astype(F32)
                for j in range(1, k):
                    grad = grad + p_ref[j].astype(F32)
                new_m = ADAM_B1 * m_ref[...] + (1.0 - ADAM_B1) * grad
                new_v = ADAM_B2 * v_ref[...] + (1.0 - ADAM_B2) * (grad * grad)
                m_hat = new_m / c1
                v_hat = new_v / c2
                g_ref[...] = grad
                d_ref[...] = -ADAM_LR * (m_hat / (jnp.sqrt(v_hat) + ADAM_EPS) + ADAM_WD * w_ref[...])
                nm_ref[...] = new_m
                nv_ref[...] = new_v

    part = lambda l: pl.BlockSpec((k, tr, c), functools.partial(lambda ll, i, l: (0, jnp.where(ll == l, i, 0), 0), l=l))
    blk = pl.BlockSpec((None, tr, c), lambda ll, i: (ll, i, 0))
    return pl.pallas_call(
        body, name=name, grid=(nl, r // tr),
        in_specs=[part(l) for l in range(nl)] + [blk, blk, blk],
        out_specs=[blk] * 4, out_shape=[jax.ShapeDtypeStruct((nl, r, c), F32)] * 4,
        compiler_params=_params(("parallel", "parallel")),
    )(*parts, w, m, v)


def ada_fwd(c_all, ada_w):
    nl, d, n = ada_w.shape

    def body(c_ref, w_ref, o_ref):
        o_ref[...] = jnp.dot(_silu(c_ref[...]), w_ref[...], precision=HIGHEST, preferred_element_type=F32)

    return pl.pallas_call(
        body, name="ada_fwd", grid=(nl,),
        in_specs=[pl.BlockSpec((N_DEV, d), lambda l: (0, 0)), pl.BlockSpec((None, d, n), lambda l: (l, 0, 0))],
        out_specs=pl.BlockSpec((None, N_DEV, n), lambda l: (l, 0, 0)),
        out_shape=jax.ShapeDtypeStruct((nl, N_DEV, n), F32),
        compiler_params=_params(("parallel",)),
    )(c_all, ada_w)


def ada_bwd(c_all_t, dmod, td=256):
    d = c_all_t.shape[0]
    nl, _, n = dmod.shape
    td = _tile(d, td, 8)

    def body(c_ref, dm_ref, o_ref):
        ca = _silu(c_ref[...])
        acc = ca[:, 0:1] * dm_ref[0:1, :]
        for b in range(1, N_DEV):
            acc = acc + ca[:, b:b + 1] * dm_ref[b:b + 1, :]
        o_ref[...] = acc

    return pl.pallas_call(
        body, name="ada_bwd", grid=(nl, d // td),
        in_specs=[pl.BlockSpec((td, N_DEV), lambda l, i: (i, 0)), pl.BlockSpec((None, N_DEV, n), lambda l, i: (l, 0, 0))],
        out_specs=pl.BlockSpec((None, td, n), lambda l, i: (l, i, 0)),
        out_shape=jax.ShapeDtypeStruct((nl, d, n), F32),
        compiler_params=_params(("parallel", "parallel")),
    )(c_all_t, dmod)


ARG_NAMES = ["x", "c", "ada_w", "ada_b", "mix_pre_g", "mix_post_g", "mlp_pre_g", "mlp_post_g", "w_in", "gmlp_ln_g", "gmlp_ln_b",
             "gmlp_ws", "gmlp_bs", "w_a_out", "conv_w", "conv_b", "conv_ln_g", "conv_ln_b", "w_b_out", "fox_bf", "w_c_out",
             "w_out", "mlp_w1", "mlp_w2"]
WEIGHTS = ARG_NAMES[2:]
COL_SHARDED = ["w_in", "w_a_out", "w_b_out", "w_c_out", "mlp_w1"]
ROW_SHARDED = ["w_out", "mlp_w2"]
BIG = COL_SHARDED + ROW_SHARDED
GROUPS = {"in": ["w_in"], "abco": ["w_a_out", "w_b_out", "w_c_out", "w_out"], "mlp": ["mlp_w1", "mlp_w2"]}
SMALL = ["ada_b", "mix_pre_g", "mix_post_g", "mlp_pre_g", "mlp_post_g", "gmlp_ln_g", "gmlp_ln_b", "gmlp_ws", "gmlp_bs",
         "conv_b", "conv_ln_g", "conv_ln_b", "fox_bf"]
PACK_COLS = 512


def _ref_ranges(lo, hi, shard):
    out = []
    while lo < hi:
        j = lo // shard
        end = min(hi, (j + 1) * shard)
        out.append((j, lo - j * shard, end - j * shard))
        lo = end
    return out


def _w_in_to_kernel_layout(g, d):
    m = D_MIX
    nf = 7 * m
    shard = g.shape[2]
    cols = lambda lo, hi: [g[j, :, a:b] for j, a, b in _ref_ranges(lo, hi, shard)]
    forget = jnp.concatenate(cols(nf, nf + N_HEADS), axis=1)
    return jnp.concatenate(cols(nf + N_HEADS, nf + N_HEADS + 3 * d) + cols(0, nf) + [jnp.repeat(forget, HEAD_DIM, axis=1)], axis=1)


def _w_in_grad_blocks(gw, d):
    m = D_MIX
    nf = 7 * m
    n_ref = nf + N_HEADS + 3 * d
    shard = n_ref // N_DEV
    segs = [(0, nf, 3 * d, 1), (nf, nf + N_HEADS, 3 * d + nf, HEAD_DIM), (nf + N_HEADS, n_ref, 0, 1)]
    blocks = []
    for j in range(N_DEV):
        lo, hi = j * shard, (j + 1) * shard
        pieces = []
        for r0, r1, k0, stride in segs:
            a, b = max(lo, r0), min(hi, r1)
            if a < b:
                pieces.append(gw[:, k0 + (a - r0) * stride:k0 + (b - r0) * stride:stride])
        blocks.append(jnp.concatenate(pieces, axis=1) if len(pieces) > 1 else pieces[0])
    return jnp.stack(blocks)


def _pack(parts):
    flat = jnp.concatenate([p.reshape(-1).astype(F32) for p in parts])
    pad = (-flat.shape[0]) % (PACK_COLS * 8)
    return jnp.pad(flat, (0, pad)).reshape(-1, PACK_COLS)


def _unpack(packed, shapes):
    nl = packed.shape[0]
    flat, out, off = packed.reshape(nl, -1), [], 0
    for shp in shapes:
        n = math.prod(shp)
        out.append(flat[:, off:off + n].reshape((nl,) + tuple(shp)))
        off += n
    return out


def _layer_small(p, conv_full, l):
    wl = {}
    for k in ["mix_pre_g", "mix_post_g", "mlp_pre_g", "mlp_post_g", "gmlp_ln_g", "gmlp_ln_b", "conv_b", "conv_ln_g", "conv_ln_b"]:
        wl[k] = p[k][l][None, :]
    wm = p["gmlp_ws"][l] * jnp.tril(jnp.ones((CHUNK, CHUNK), F32))
    wl["ws"] = wm.astype(BF16)
    wl["ws_t"] = jnp.transpose(wm, (0, 2, 1)).astype(BF16)
    wl["bs_exp"] = jnp.repeat(p["gmlp_bs"][l].T, GROUP_DIM, axis=1)
    wl["bf_exp"] = jnp.repeat(p["fox_bf"][l], HEAD_DIM)[None, :]
    wl["conv_w"] = jnp.pad(conv_full[l], ((0, CONV_HALO - CONV_WIDTH), (0, 0)))
    return wl


def kernel(x, c, ada_w, ada_b, mix_pre_g, mix_post_g, mlp_pre_g, mlp_post_g, w_in, gmlp_ln_g, gmlp_ln_b, gmlp_ws, gmlp_bs, w_a_out, conv_w, conv_b, conv_ln_g, conv_ln_b, w_b_out, fox_bf, w_c_out, w_out, mlp_w1, mlp_w2, loss_target, m_ada_w, m_ada_b, m_mix_pre_g, m_mix_post_g, m_mlp_pre_g, m_mlp_post_g, m_w_in, m_gmlp_ln_g, m_gmlp_ln_b, m_gmlp_ws, m_gmlp_bs, m_w_a_out, m_conv_w, m_conv_b, m_conv_ln_g, m_conv_ln_b, m_w_b_out, m_fox_bf, m_w_c_out, m_w_out, m_mlp_w1, m_mlp_w2, v_ada_w, v_ada_b, v_mix_pre_g, v_mix_post_g, v_mlp_pre_g, v_mlp_post_g, v_w_in, v_gmlp_ln_g, v_gmlp_ln_b, v_gmlp_ws, v_gmlp_bs, v_w_a_out, v_conv_w, v_conv_b, v_conv_ln_g, v_conv_ln_b, v_w_b_out, v_fox_bf, v_w_c_out, v_w_out, v_mlp_w1, v_mlp_w2):
    args = (x, c, ada_w, ada_b, mix_pre_g, mix_post_g, mlp_pre_g, mlp_post_g, w_in, gmlp_ln_g, gmlp_ln_b, gmlp_ws, gmlp_bs, w_a_out,
            conv_w, conv_b, conv_ln_g, conv_ln_b, w_b_out, fox_bf, w_c_out, w_out, mlp_w1, mlp_w2)
    ms = (m_ada_w, m_ada_b, m_mix_pre_g, m_mix_post_g, m_mlp_pre_g, m_mlp_post_g, m_w_in, m_gmlp_ln_g, m_gmlp_ln_b, m_gmlp_ws, m_gmlp_bs,
          m_w_a_out, m_conv_w, m_conv_b, m_conv_ln_g, m_conv_ln_b, m_w_b_out, m_fox_bf, m_w_c_out, m_w_out, m_mlp_w1, m_mlp_w2)
    vs = (v_ada_w, v_ada_b, v_mix_pre_g, v_mix_post_g, v_mlp_pre_g, v_mlp_post_g, v_w_in, v_gmlp_ln_g, v_gmlp_ln_b, v_gmlp_ws, v_gmlp_bs,
          v_w_a_out, v_conv_w, v_conv_b, v_conv_ln_g, v_conv_ln_b, v_w_b_out, v_fox_bf, v_w_c_out, v_w_out, v_mlp_w1, v_mlp_w2)
    p = dict(zip(ARG_NAMES, args))
    mom = dict(zip(WEIGHTS, ms))
    var = dict(zip(WEIGHTS, vs))
    nl = ada_w.shape[0]
    s, d = x.shape[1], x.shape[2]
    me = 4 * lax.axis_index("x") + 2 * lax.axis_index("y") + lax.axis_index("c")

    c_all, conv_all = exchange("gather_c", [c, conv_w], scatter=False)
    c_all = c_all.reshape(N_DEV, d)
    n_ada = ada_w.shape[2]
    mod_parts = ada_fwd(c_all, ada_w)
    (mod_recv,) = exchange("scatter_mod", [jnp.transpose(mod_parts, (1, 0, 2))], scatter=True)
    conv_full = jnp.transpose(conv_all, (1, 2, 0, 3)).reshape(nl, CONV_WIDTH, D_MIX)

    def full_matrix(k, land, own):
        g = lax.dynamic_update_index_in_dim(land, own, me, 0)
        r, cc = own.shape
        if k == "w_in":
            return _w_in_to_kernel_layout(g, d)
        return jnp.transpose(g, (1, 0, 2)).reshape(r, N_DEV * cc) if k in COL_SHARDED else g.reshape(N_DEV * r, cc)

    started = [mod_recv]

    def fetch(l, tag):
        keys = GROUPS[tag]
        st = exchange_start(f"gather_{tag}{l}_start", [p[k][l].astype(BF16) for k in keys], scatter=False, after=started[-1])
        started.append(st["token"])

        def get(after):
            owns, lands = exchange_wait(f"gather_{tag}{l}_wait", st, after)
            return {k: full_matrix(k, land, o) for k, land, o in zip(keys, lands, owns)}

        return get

    getters = [{tag: fetch(l, tag) for tag in GROUPS} for l in range(nl)]
    mod = jnp.transpose(mod_recv, (1, 0, 2)).reshape(nl, N_DEV * n_ada) + ada_b + started[-1][0:1, 0:1]
    mods = [mod[l].reshape(6, d) for l in range(nl)]

    sent = {}

    def emitter(l):
        def emit(tag, grads_big):
            keys = GROUPS[tag]
            send = []
            for k in keys:
                gk = grads_big[k]
                if k == "w_in":
                    gk = _w_in_grad_blocks(gk, d)
                elif gk.ndim == 3:
                    pass
                elif k in COL_SHARDED:
                    r, cc = gk.shape[0], gk.shape[1] // N_DEV
                    gk = jnp.transpose(gk.reshape(r, N_DEV, cc), (1, 0, 2))
                else:
                    gk = gk.reshape(N_DEV, gk.shape[0] // N_DEV, gk.shape[1])
                send.append(gk.astype(BF16))
            sent[(l, tag)] = (keys, exchange_start(f"scatter_{tag}{l}_start", send, scatter=True))
            return sent[(l, tag)][1]["token"]

        return emit

    small_sent = {}

    def small_emitter(l):
        def emit_small(dmod_l, g):
            packed = _pack([dmod_l] + [g[k] for k in SMALL[1:]] + [g["conv_w"]])
            small_sent[l] = exchange_start(f"gather_small{l}_start", [packed], scatter=False)
            return small_sent[l]["token"]

        return emit_small

    layers = [dict(small=_layer_small(p, conv_full, l), emit_small=small_emitter(l), get_in=getters[l]["in"], get_abco=getters[l]["abco"], get_mlp=getters[l]["mlp"],
                   emit=emitter(l)) for l in range(nl)]
    loss_local, dx, dmods, grads = local_step(x[0], loss_target[0], mods, layers)
    loss = lax.psum(loss_local, ("x", "y", "c"))
    grad_x = dx[None]

    out = {k: [None] * 4 for k in WEIGHTS}

    def shard_update(name, parts, k):
        shp = p[k].shape
        flat = lambda a: a.reshape(nl, -1, shp[-1])
        res = adamw_sum(name, [pt.reshape((pt.shape[0],) + flat(p[k]).shape[1:]) for pt in parts], flat(p[k]), flat(mom[k]), flat(var[k]))
        out[k] = [a.reshape(shp) for a in res]

    parts = {}

    def collect(tag, after):
        for l in reversed(range(nl)):
            keys, st = sent[(l, tag)]
            sends, lands = exchange_wait(f"scatter_{tag}{l}_wait", st, after)
            for k, land, sd in zip(keys, lands, sends):
                own = lax.dynamic_index_in_dim(sd, me, 0, keepdims=False)
                parts[(k, l)] = lax.dynamic_update_index_in_dim(land, own, me, 0)

    behind_bwd = [dx] + [small_sent[l]["token"] for l in range(nl)]
    collect("mlp", behind_bwd)
    collect("abco", behind_bwd)
    for k in BIG[1:]:
        shard_update("adamw_" + k, [parts[(k, l)] for l in range(nl)], k)
    collect("in", [out[k][0] for k in BIG[1:]])
    shard_update("adamw_w_in", [parts[("w_in", l)] for l in range(nl)], "w_in")

    small_shapes = [p[k].shape[1:] for k in SMALL] + [(CONV_WIDTH, D_MIX)]
    small_all = []
    for l in range(nl):
        srcs, lands = exchange_wait(f"gather_small{l}_wait", small_sent[l], [out[k][0] for k in BIG])
        small_all.append(lax.dynamic_update_index_in_dim(lands[0], srcs[0], me, 0))
    zeros_conv = jnp.zeros(small_shapes[-1], F32)
    packs = [jnp.stack([_pack([src[k][l] for k in SMALL] + [zeros_conv]) for l in range(nl)]) for src in (p, mom, var)]
    small_out = [_unpack(o, small_shapes) for o in adamw_sum("adamw_small", small_all, *packs)]
    for i, k in enumerate(SMALL):
        for j in range(4):
            out[k][j] = small_out[j][i]

    n_conv = conv_w.shape[2]
    conv_grad = lax.dynamic_slice_in_dim(small_out[0][-1], me * n_conv, n_conv, axis=2)
    shard_update("adamw_conv_w", [conv_grad[l][None] for l in range(nl)], "conv_w")

    dmod_all = jnp.stack([small_all[l].reshape(N_DEV, -1)[:, :6 * d] for l in range(nl)])
    dmod_mine = lax.dynamic_slice_in_dim(dmod_all, me * n_ada, n_ada, axis=2)
    g_ada = ada_bwd(c_all.T, dmod_mine)
    shard_update("adamw_ada_w", [g_ada[l][None] for l in range(nl)], "ada_w")

    res = [loss, grad_x]
    for j in range(4):
        res += [out[k][j] for k in WEIGHTS]
    return tuple(res)
```

```python
import functools
import math

import jax
import jax.numpy as jnp
from jax import lax
from jax.experimental import pallas as pl
from jax.experimental.pallas import tpu as pltpu

F32 = jnp.float32
BF16 = jnp.bfloat16
MESH = pl.DeviceIdType.MESH
N_DEV = 8
NORM_EPS = 1e-6
D_MIX = 512
N_HEADS = 8
HEAD_DIM = 64
GROUP_DIM = 64
CHUNK = 128
CONV_WIDTH = 31
CONV_HALO = 32
LANES = 128
ADAM_LR, ADAM_B1, ADAM_B2, ADAM_EPS, ADAM_WD, ADAM_STEP = 0.001, 0.9, 0.999, 1e-08, 0.01, 10
VMEM_LIMIT = 56 * 1024 * 1024
HIGHEST = lax.Precision.HIGHEST


def _tile(dim, pref, mult=LANES):
    t = min(pref, dim)
    t -= t % mult
    while t >= mult:
        if dim % t == 0:
            return t
        t -= mult
    return dim


def _params(sem):
    return pltpu.CompilerParams(dimension_semantics=sem, vmem_limit_bytes=VMEM_LIMIT)


def rowwise(name, fn, rows, consts, outs, accs=(), ts=512, into=None):
    s = rows[0][0].shape[0]
    ts = min(ts, s)
    nr, nc, no, na = len(rows), len(consts), len(outs), len(accs)
    n_in = nr + nc + (into is not None)

    def body(*refs):
        vals = [r[...].astype(F32) for r in refs[:nr]] + [r[...] for r in refs[nr:nr + nc]]
        res = fn(*vals)
        if not isinstance(res, (tuple, list)):
            res = (res,)
        for r, v in zip(refs[n_in:n_in + no], res[:no]):
            r[...] = v.astype(r.dtype)
        if na:
            acc_refs = refs[n_in + no:]

            @pl.when(pl.program_id(0) == 0)
            def _():
                for r in acc_refs:
                    r[...] = jnp.zeros(r.shape, r.dtype)

            for r, v in zip(acc_refs, res[no:]):
                r[...] += v.astype(F32)

    in_specs = [pl.BlockSpec((ts, w), functools.partial(lambda i, cb: (i, cb), cb=cb)) for (_, cb, w) in rows]
    in_specs += [pl.BlockSpec(c.shape, lambda i: (0, 0)) for c in consts]
    out_specs = [pl.BlockSpec((ts, w), lambda i: (i, 0)) for (w, _) in outs]
    out_specs += [pl.BlockSpec(shp, lambda i: (0, 0)) for shp in accs]
    out_shape = [jax.ShapeDtypeStruct((s, w), dt) for (w, dt) in outs]
    out_shape += [jax.ShapeDtypeStruct(shp, F32) for shp in accs]
    extra, aliases = [], {}
    if into is not None:
        buf, cb_into = into
        assert buf.dtype == outs[0][1] and buf.shape[0] == s
        in_specs.append(pl.BlockSpec(memory_space=pl.ANY))
        out_specs[0] = pl.BlockSpec((ts, outs[0][0]), lambda i: (i, cb_into))
        out_shape[0] = jax.ShapeDtypeStruct(buf.shape, buf.dtype)
        extra, aliases = [buf], {nr + nc: 0}
    res = pl.pallas_call(
        body, name=name, grid=(s // ts,), in_specs=in_specs, out_specs=out_specs, out_shape=out_shape,
        input_output_aliases=aliases, compiler_params=_params(("arbitrary",) if na else ("parallel",)),
    )(*[a for (a, _, _) in rows], *consts, *extra)
    return res


def rowwise_vjp(name, f, rows, consts, cts, grad_dtypes, ts=512, into=None):
    nr, nc, nt = len(rows), len(consts), len(cts)
    keep = [i for i, dt in enumerate(grad_dtypes) if dt is not None]
    k_into = 0 if into is None else into[2]

    def g(*vals):
        rv = [v.astype(F32) for v in vals[:nr]]
        ctv = tuple(v.astype(F32) for v in vals[nr:nr + nt])
        cv = list(vals[nr + nt:])
        _, vjp = jax.vjp(lambda *a: tuple(f(*a)), *rv, *cv)
        grads = vjp(ctv)
        row_grads = [grads[i] for i in keep]
        if k_into:
            row_grads = [jnp.concatenate(row_grads[:k_into], axis=1)] + row_grads[k_into:]
        return tuple(row_grads) + tuple(grads[nr:])

    outs = [(rows[i][2], grad_dtypes[i]) for i in keep]
    if k_into:
        assert len({dt for _, dt in outs[:k_into]}) == 1
        outs = [(sum(w for w, _ in outs[:k_into]), outs[0][1])] + outs[k_into:]
    return rowwise(name, g, list(rows) + list(cts), consts, outs, accs=[c.shape for c in consts], ts=ts,
                   into=None if into is None else into[:2])


def matmul(name, a, b, *, ta=False, tb=False, out_dtypes=(F32,), epilogue=None, epi=(), tm=None, tn=None, tk=4096, dep=None, b_cols=None, out_blocks=None,
           f32_tail=False):
    m, k = (a.shape[1], a.shape[0]) if ta else a.shape
    n = b.shape[0] if tb else b.shape[1]
    b_col0 = 0
    if b_cols is not None:
        assert not tb
        b_col0, n = b_cols
    assert (b.shape[1] if tb else b.shape[0]) == k
    tk = _tile(k, tk)
    if tk > 1024:
        tm, tn = _tile(m, tm or 1024), _tile(n, tn or 1024)
    else:
        tm, tn = _tile(m, tm or 2048), _tile(n, tn or (1024 if m >= 2048 else 2048))
    if out_blocks is not None:
        tn = n // out_blocks
    assert b_col0 % tn == 0
    jb = b_col0 // tn
    nk = k // tk
    nj = n // tn
    assert not f32_tail or (nk == 1 and epilogue is None)
    ne, no = len(epi), len(out_dtypes)
    dims = (((0 if ta else 1,), (1 if tb else 0,)), ((), ()))

    def body(*refs):
        a_ref, b_ref = refs[0], refs[1]
        epi_refs = refs[2:2 + ne]
        n_in = 2 + ne + (dep is not None)
        out_refs = refs[n_in:n_in + no]
        part = lax.dot_general(a_ref[...].astype(BF16), b_ref[...].astype(BF16), dims, preferred_element_type=F32)

        def finish(acc):
            res = (acc,) if epilogue is None else epilogue(acc, *[r[...] for r in epi_refs])
            for r, v in zip(out_refs, res):
                r[...] = v.astype(r.dtype)

        if nk == 1:
            finish(part)
            if f32_tail:
                @pl.when(pl.program_id(1) == nj - 1)
                def _():
                    refs[n_in + no][...] = part
        else:
            acc_ref = refs[-1]
            kk = pl.program_id(2)

            @pl.when(kk == 0)
            def _():
                acc_ref[...] = part

            @pl.when(kk > 0)
            def _():
                acc_ref[...] += part

            @pl.when(kk == nk - 1)
            def _():
                finish(acc_ref[...])

    a_spec = pl.BlockSpec((tk, tm), lambda i, j, kk: (kk, i)) if ta else pl.BlockSpec((tm, tk), lambda i, j, kk: (i, kk))
    b_spec = pl.BlockSpec((tn, tk), lambda i, j, kk: (j, kk)) if tb else pl.BlockSpec((tk, tn), lambda i, j, kk: (kk, j + jb))
    epi_specs = []
    for (arr, col0) in epi:
        assert col0 % tn == 0
        epi_specs.append(pl.BlockSpec((tm, tn), functools.partial(lambda i, j, kk, c0: (i, j + c0), c0=col0 // tn)))
    res = pl.pallas_call(
        body, name=name, grid=(m // tm, n // tn, nk),
        in_specs=[a_spec, b_spec] + epi_specs + ([] if dep is None else [pl.BlockSpec(dep.shape, lambda i, j, kk: (0, 0))]),
        out_specs=[pl.BlockSpec((tm, tn), lambda i, j, kk: (i, j)) if out_blocks is None else
                   pl.BlockSpec((None, tm, tn), lambda i, j, kk: (j, i, 0)) for _ in out_dtypes]
        + ([pl.BlockSpec((tm, tn), lambda i, j, kk: (i, 0))] if f32_tail else []),
        out_shape=[jax.ShapeDtypeStruct((m, n) if out_blocks is None else (out_blocks, m, tn), dt) for dt in out_dtypes]
        + ([jax.ShapeDtypeStruct((m, tn), F32)] if f32_tail else []),
        scratch_shapes=[pltpu.VMEM((tm, tn), F32)] if nk > 1 else [],
        compiler_params=_params(("parallel", "arbitrary" if f32_tail else "parallel", "arbitrary")),
    )(a, b, *[arr for (arr, _) in epi], *([] if dep is None else [dep]))
    return res[0] if len(res) == 1 else res


def _rms(x, g):
    return x * lax.rsqrt(jnp.mean(x * x, axis=-1, keepdims=True) + NORM_EPS) * g


def _ln(x, g, b):
    mu = jnp.mean(x, axis=-1, keepdims=True)
    xc = x - mu
    var = jnp.mean(xc * xc, axis=-1, keepdims=True)
    return xc * lax.rsqrt(var + NORM_EPS) * g + b


def _gelu(x):
    return 0.5 * x * (1.0 + jnp.tanh(math.sqrt(2.0 / math.pi) * (x + 0.044715 * (x * x * x))))


def _sigmoid(x):
    return 1.0 / (1.0 + jnp.exp(-x))


def _silu(x):
    return x * _sigmoid(x)


def _log_sigmoid(x):
    return jnp.minimum(x, 0.0) - jnp.log(1.0 + jnp.exp(-jnp.abs(x)))


def _f_pre(x, g, sc, sh):
    return (_rms(x, g) * (1.0 + sc) + sh,)


def _f_post(y, g, gt):
    return (gt * _rms(y, g),)


def _f_a1(u_raw, v_raw, g, b):
    return _gelu(u_raw), _ln(_gelu(v_raw), g, b)


def _f_glu(val, gate):
    return (val * _sigmoid(gate),)


def _f_lnsilu(zc, g, b):
    return (_silu(_ln(zc, g, b)),)


def _f_merge(g0, g1, g2, ya, yb, yc):
    return (_sigmoid(g0) * ya + _sigmoid(g1) * yb + _sigmoid(g2) * yc,)


def _lane_lt64(shape):
    return lax.broadcasted_iota(jnp.int32, shape, 1) < HEAD_DIM


def spatial_fwd(vln, u, w_bf, b_exp, rows_per_step=512):
    s = vln.shape[0]
    tr = min(rows_per_step, s)

    def body(v_ref, u_ref, w_ref, b_ref, sv_ref, ya_ref):
        lo = _lane_lt64((CHUNK, LANES))
        for ch in range(tr // CHUNK):
            r0 = ch * CHUNK
            for p in range(D_MIX // LANES):
                vp = v_ref[r0:r0 + CHUNK, p * LANES:(p + 1) * LANES]
                o0 = jnp.dot(w_ref[2 * p], vp, preferred_element_type=F32)
                o1 = jnp.dot(w_ref[2 * p + 1], vp, preferred_element_type=F32)
                sv = jnp.where(lo, o0, o1) + b_ref[:, p * LANES:(p + 1) * LANES]
                sv_ref[r0:r0 + CHUNK, p * LANES:(p + 1) * LANES] = sv
                ya_ref[r0:r0 + CHUNK, p * LANES:(p + 1) * LANES] = (
                    u_ref[r0:r0 + CHUNK, p * LANES:(p + 1) * LANES] * sv).astype(BF16)

    row = pl.BlockSpec((tr, D_MIX), lambda i: (i, 0))
    return pl.pallas_call(
        body, name="spatial_fwd", grid=(s // tr,),
        in_specs=[row, row, pl.BlockSpec(w_bf.shape, lambda i: (0, 0, 0)), pl.BlockSpec(b_exp.shape, lambda i: (0, 0))],
        out_specs=[row, row],
        out_shape=[jax.ShapeDtypeStruct((s, D_MIX), F32), jax.ShapeDtypeStruct((s, D_MIX), BF16)],
        compiler_params=_params(("parallel",)),
    )(vln, u, w_bf, b_exp)


def spatial_bwd(dya, u, sv, vln, wt_bf, rows_per_step=512):
    s = vln.shape[0]
    tr = min(rows_per_step, s)
    ng = wt_bf.shape[0]

    def body(dya_ref, u_ref, sv_ref, v_ref, wt_ref, du_ref, dv_ref, dw_ref, db_ref):
        @pl.when(pl.program_id(0) == 0)
        def _():
            dw_ref[...] = jnp.zeros(dw_ref.shape, F32)
            db_ref[...] = jnp.zeros(db_ref.shape, F32)

        lo = _lane_lt64((CHUNK, LANES))
        for ch in range(tr // CHUNK):
            r0 = ch * CHUNK
            for p in range(D_MIX // LANES):
                cs = slice(p * LANES, (p + 1) * LANES)
                dya_p = dya_ref[r0:r0 + CHUNK, cs].astype(F32)
                du_ref[r0:r0 + CHUNK, cs] = dya_p * sv_ref[r0:r0 + CHUNK, cs]
                dsv = dya_p * u_ref[r0:r0 + CHUNK, cs]
                db_ref[:, cs] += dsv
                dsv0 = jnp.where(lo, dsv, 0.0).astype(BF16)
                dsv1 = jnp.where(lo, 0.0, dsv).astype(BF16)
                vp = v_ref[r0:r0 + CHUNK, cs]
                d0 = jnp.dot(wt_ref[2 * p], dsv0, preferred_element_type=F32)
                d1 = jnp.dot(wt_ref[2 * p + 1], dsv1, preferred_element_type=F32)
                dv_ref[r0:r0 + CHUNK, cs] = d0 + d1
                nt = (((1,), (1,)), ((), ()))
                dw_ref[2 * p] += lax.dot_general(dsv0, vp, nt, preferred_element_type=F32)
                dw_ref[2 * p + 1] += lax.dot_general(dsv1, vp, nt, preferred_element_type=F32)

    row = pl.BlockSpec((tr, D_MIX), lambda i: (i, 0))
    return pl.pallas_call(
        body, name="spatial_bwd", grid=(s // tr,),
        in_specs=[row, row, row, row, pl.BlockSpec(wt_bf.shape, lambda i: (0, 0, 0))],
        out_specs=[row, row, pl.BlockSpec((ng, CHUNK, CHUNK), lambda i: (0, 0, 0)), pl.BlockSpec((CHUNK, D_MIX), lambda i: (0, 0))],
        out_shape=[jax.ShapeDtypeStruct((s, D_MIX), F32), jax.ShapeDtypeStruct((s, D_MIX), F32),
                   jax.ShapeDtypeStruct((ng, CHUNK, CHUNK), F32), jax.ShapeDtypeStruct((CHUNK, D_MIX), F32)],
        compiler_params=_params(("arbitrary",)),
    )(dya, u, sv, vln, wt_bf)


def _windows(ref, first, count, ts):
    for r in range(8):
        ks = [k for k in range(count) if (first + k) % 8 == r]
        if ks:
            base = first + ks[0]
            blk = ref[base:base + ks[-1] - ks[0] + ts, :]
            for k in ks:
                yield k, blk[k - ks[0]:k - ks[0] + ts, :]


def conv_fwd(proj, cb_val, cb_gate, w_pad, cb, ln_g, ln_b, ts=256):
    s = proj.shape[0]
    ts = min(ts, s)
    per = ts // CONV_HALO

    def body(val_ref, gate_ref, pval_ref, pgate_ref, w_ref, cb_ref, g_ref, b_ref, zc_ref, yb_ref, ext_ref):
        i = pl.program_id(0)
        zprev = pval_ref[...].astype(F32) * _sigmoid(pgate_ref[...].astype(F32))
        ext_ref[0:CONV_HALO, :] = jnp.where(i > 0, zprev, 0.0)
        ext_ref[CONV_HALO:, :] = val_ref[...].astype(F32) * _sigmoid(gate_ref[...].astype(F32))
        acc = jnp.zeros((ts, D_MIX), F32)
        for j, win in _windows(ext_ref, CONV_HALO - (CONV_WIDTH - 1), CONV_WIDTH, ts):
            acc = acc + w_ref[j:j + 1, :] * win
        zc = acc + cb_ref[...]
        zc_ref[...] = zc
        yb_ref[...] = _f_lnsilu(zc, g_ref[...], b_ref[...])[0].astype(BF16)

    def cur(c):
        return pl.BlockSpec((ts, D_MIX), functools.partial(lambda i, c: (i, c), c=c))

    def prev(c):
        return pl.BlockSpec((CONV_HALO, D_MIX), functools.partial(lambda i, c: (jnp.maximum(i * per - 1, 0), c), c=c))

    const = lambda a: pl.BlockSpec(a.shape, lambda i: (0, 0))
    out = pl.BlockSpec((ts, D_MIX), lambda i: (i, 0))
    return pl.pallas_call(
        body, name="conv_fwd", grid=(s // ts,),
        in_specs=[cur(cb_val), cur(cb_gate), prev(cb_val), prev(cb_gate), const(w_pad), const(cb), const(ln_g), const(ln_b)],
        out_specs=[out, out],
        out_shape=[jax.ShapeDtypeStruct((s, D_MIX), F32), jax.ShapeDtypeStruct((s, D_MIX), BF16)],
        scratch_shapes=[pltpu.VMEM((CONV_HALO + ts, D_MIX), F32)],
        compiler_params=_params(("parallel",)),
    )(proj, proj, proj, proj, w_pad, cb, ln_g, ln_b)


def conv_bwd(proj, cb_val, cb_gate, zc, dyb, w_pad, ln_g, ln_b, dproj, ts=256):
    assert cb_gate == cb_val + 1 and cb_val % 2 == 0
    s = proj.shape[0]
    ts = min(ts, s)
    per = ts // CONV_HALO
    n_tiles = s // ts
    n_halo = s // CONV_HALO

    def body(val_ref, gate_ref, pval_ref, pgate_ref, zc_ref, dyb_ref, nzc_ref, ndyb_ref, w_ref, g_ref, b_ref, dproj_in,
             dvg_ref, dw_ref, dcb_ref, dg_ref, db_ref, zext_ref, dext_ref):
        i = pl.program_id(0)

        @pl.when(i == 0)
        def _():
            for r in (dw_ref, dcb_ref, dg_ref, db_ref):
                r[...] = jnp.zeros(r.shape, F32)

        g, b = g_ref[...], b_ref[...]
        _, vjp = jax.vjp(lambda z, gg, bb: _f_lnsilu(z, gg, bb)[0], zc_ref[...], g, b)
        dzc, dg, db = vjp(dyb_ref[...].astype(F32))
        dg_ref[...] += dg
        db_ref[...] += db
        dcb_ref[...] += jnp.sum(dzc, axis=0, keepdims=True)
        _, vjp_n = jax.vjp(lambda z: _f_lnsilu(z, g, b)[0], nzc_ref[...])
        (dzc_next,) = vjp_n(ndyb_ref[...].astype(F32))
        dext_ref[0:ts, :] = dzc
        dext_ref[ts:, :] = jnp.where(i < n_tiles - 1, dzc_next, 0.0)
        val, gate = val_ref[...].astype(F32), gate_ref[...].astype(F32)
        zprev = pval_ref[...].astype(F32) * _sigmoid(pgate_ref[...].astype(F32))
        zext_ref[0:CONV_HALO, :] = jnp.where(i > 0, zprev, 0.0)
        zext_ref[CONV_HALO:, :] = val * _sigmoid(gate)
        dz = jnp.zeros((ts, D_MIX), F32)
        for shift, win in _windows(dext_ref, 0, CONV_WIDTH, ts):
            j = CONV_WIDTH - 1 - shift
            dz = dz + w_ref[j:j + 1, :] * win
        for j, win in _windows(zext_ref, CONV_HALO - (CONV_WIDTH - 1), CONV_WIDTH, ts):
            dw_ref[j:j + 1, :] += jnp.sum(dzc * win, axis=0, keepdims=True)
        _, vjp_glu = jax.vjp(lambda a, c: _f_glu(a, c)[0], val, gate)
        dval, dgate = vjp_glu(dz)
        dvg_ref[:, :D_MIX] = dval.astype(BF16)
        dvg_ref[:, D_MIX:] = dgate.astype(BF16)

    def cur(c):
        return pl.BlockSpec((ts, D_MIX), functools.partial(lambda i, c: (i, c), c=c))

    def prev(c):
        return pl.BlockSpec((CONV_HALO, D_MIX), functools.partial(lambda i, c: (jnp.maximum(i * per - 1, 0), c), c=c))

    nxt = pl.BlockSpec((CONV_HALO, D_MIX), lambda i: (jnp.minimum((i + 1) * per, n_halo - 1), 0))
    const = lambda a: pl.BlockSpec(a.shape, lambda i: (0, 0))
    out = pl.BlockSpec((ts, D_MIX), lambda i: (i, 0))
    vec = pl.BlockSpec((1, D_MIX), lambda i: (0, 0))
    return pl.pallas_call(
        body, name="conv_bwd", grid=(n_tiles,),
        in_specs=[cur(cb_val), cur(cb_gate), prev(cb_val), prev(cb_gate), out, out, nxt, nxt, const(w_pad), const(ln_g), const(ln_b),
                  pl.BlockSpec(memory_space=pl.ANY)],
        out_specs=[pl.BlockSpec((ts, 2 * D_MIX), lambda i: (i, cb_val // 2)), pl.BlockSpec((CONV_HALO, D_MIX), lambda i: (0, 0)), vec, vec, vec],
        out_shape=[jax.ShapeDtypeStruct(dproj.shape, dproj.dtype),
                   jax.ShapeDtypeStruct((CONV_HALO, D_MIX), F32)] + [jax.ShapeDtypeStruct((1, D_MIX), F32)] * 3,
        scratch_shapes=[pltpu.VMEM((CONV_HALO + ts, D_MIX), F32), pltpu.VMEM((ts + CONV_HALO, D_MIX), F32)],
        input_output_aliases={11: 0}, compiler_params=_params(("arbitrary",)),
    )(proj, proj, proj, proj, zc, dyb, zc, dyb, w_pad, ln_g, ln_b, dproj)


def forget_cumsum(proj, cb_f, bf_exp, t=256):
    s = proj.shape[0]
    t = min(t, s)

    def body(f_ref, bf_ref, out_ref, carry_ref):
        @pl.when(pl.program_id(0) == 0)
        def _():
            carry_ref[...] = jnp.zeros(carry_ref.shape, F32)

        lf = _log_sigmoid(f_ref[...] + bf_ref[...])
        tri = (lax.broadcasted_iota(jnp.int32, (t, t), 1) <= lax.broadcasted_iota(jnp.int32, (t, t), 0)).astype(F32)
        c = jnp.dot(tri, lf, precision=HIGHEST, preferred_element_type=F32) + carry_ref[...]
        out_ref[...] = c
        carry_ref[...] = c[t - 1:t, :]

    return pl.pallas_call(
        body, name="forget_cumsum", grid=(s // t,),
        in_specs=[pl.BlockSpec((t, D_MIX), functools.partial(lambda i, c: (i, c), c=cb_f)), pl.BlockSpec((1, D_MIX), lambda i: (0, 0))],
        out_specs=pl.BlockSpec((t, D_MIX), lambda i: (i, 0)),
        out_shape=jax.ShapeDtypeStruct((s, D_MIX), F32),
        scratch_shapes=[pltpu.VMEM((1, D_MIX), F32)],
        compiler_params=_params(("arbitrary",)),
    )(proj, bf_exp)


def forget_bwd(proj, cb_f, bf_exp, dcum, dproj, cb_out, t=256):
    s = proj.shape[0]
    t = min(t, s)
    n = s // t

    def body(f_ref, bf_ref, dc_ref, dproj_in, df_ref, dbf_ref, carry_ref):
        @pl.when(pl.program_id(0) == 0)
        def _():
            carry_ref[...] = jnp.zeros(carry_ref.shape, F32)
            dbf_ref[...] = jnp.zeros(dbf_ref.shape, F32)

        tri = (lax.broadcasted_iota(jnp.int32, (t, t), 1) >= lax.broadcasted_iota(jnp.int32, (t, t), 0)).astype(F32)
        r = jnp.dot(tri, dc_ref[...], precision=HIGHEST, preferred_element_type=F32) + carry_ref[...]
        carry_ref[...] = r[0:1, :]
        df = r * _sigmoid(-(f_ref[...] + bf_ref[...]))
        dbf_ref[...] += jnp.sum(df, axis=0, keepdims=True)
        live = lax.broadcasted_iota(jnp.int32, (t, D_MIX), 1) % HEAD_DIM == 0
        df_ref[...] = jnp.where(live, df, 0.0).astype(BF16)

    return pl.pallas_call(
        body, name="forget_bwd", grid=(n,),
        in_specs=[pl.BlockSpec((t, D_MIX), functools.partial(lambda i, c: (n - 1 - i, c), c=cb_f)), pl.BlockSpec((1, D_MIX), lambda i: (0, 0)),
                  pl.BlockSpec((t, D_MIX), lambda i: (n - 1 - i, 0)), pl.BlockSpec(memory_space=pl.ANY)],
        out_specs=[pl.BlockSpec((t, D_MIX), lambda i: (n - 1 - i, cb_out)), pl.BlockSpec((1, D_MIX), lambda i: (0, 0))],
        out_shape=[jax.ShapeDtypeStruct(dproj.shape, dproj.dtype), jax.ShapeDtypeStruct((1, D_MIX), F32)],
        scratch_shapes=[pltpu.VMEM((1, D_MIX), F32)],
        input_output_aliases={3: 0}, compiler_params=_params(("arbitrary",)),
    )(proj, bf_exp, dcum, dproj)


NT = (((1,), (1,)), ((), ()))
LOG2E = math.log2(math.e)
N_PAIR = D_MIX // LANES


def _split3(x):
    hi = x.astype(BF16).astype(F32)
    mid = (x - hi).astype(BF16).astype(F32)
    return hi, mid, x - hi - mid


def _triple(li, first, vals):
    out = jnp.where(li == first, vals[0], 0.0)
    for i in (1, 2):
        out = jnp.where(li == first + i, vals[i], out)
    return out


def _lane_ids(shape):
    lane = lax.broadcasted_iota(jnp.int32, shape, 1)
    return lane, lane % HEAD_DIM, lane < HEAD_DIM


def attn_prep(proj, cb_q, cum, ts=512):
    s = proj.shape[0]
    ts = min(ts, s)
    scale = LOG2E / math.sqrt(HEAD_DIM)

    def body(q_ref, k_ref, v_ref, c_ref, qe_ref, qo_ref, ke_ref, ko_ref, ve_ref, vo_ref):
        _, li, lo = _lane_ids((ts, LANES))
        one3 = lambda first: ((li >= first) & (li < first + 3)).astype(F32)
        for p in range(N_PAIR):
            ps = slice(p * LANES, (p + 1) * LANES)
            c3 = _split3(pltpu.roll(c_ref[:, ps] * LOG2E, HEAD_DIM, axis=1))
            eq = _triple(li, 0, c3) + one3(3)
            ek = one3(0) - _triple(li, 3, c3) + one3(6)
            ev = one3(0)
            for src, even, odd, extra, mul in ((q_ref, qe_ref, qo_ref, eq, scale), (k_ref, ke_ref, ko_ref, ek, 1.0), (v_ref, ve_ref, vo_ref, ev, 1.0)):
                x = src[:, ps].astype(F32) * mul
                even[:, ps] = jnp.where(lo, x, extra).astype(BF16)
                odd[:, ps] = jnp.where(lo, extra, x).astype(BF16)

    col = lambda c: pl.BlockSpec((ts, D_MIX), functools.partial(lambda i, c: (i, c), c=c))
    out = pl.BlockSpec((ts, D_MIX), lambda i: (i, 0))
    return pl.pallas_call(
        body, name="attn_prep", grid=(s // ts,),
        in_specs=[col(cb_q), col(cb_q + 1), col(cb_q + 2), pl.BlockSpec((ts, D_MIX), lambda i: (i, 0))],
        out_specs=[out] * 6, out_shape=[jax.ShapeDtypeStruct((s, D_MIX), BF16)] * 6,
        compiler_params=_params(("parallel",)),
    )(proj, proj, proj, cum)


def _pair_specs(s, t):
    return pl.BlockSpec((t, LANES), lambda p, i: (i, p)), pl.BlockSpec((s, LANES), lambda p, i: (0, p))


def attn_fwd(qe, qo, ke, ko, ve, vo, tq=512):
    s = qe.shape[0]
    tq = min(tq, s)

    def body(qe_ref, qo_ref, ke_ref, ko_ref, ve_ref, vo_ref, o_ref, qbe_ref, qbo_ref):
        qi = pl.program_id(1)
        qs, k_refs, v_refs = (qe_ref[...], qo_ref[...]), (ke_ref, ko_ref), (ve_ref, vo_ref)
        causal = lax.broadcasted_iota(jnp.int32, (tq, tq), 1) <= lax.broadcasted_iota(jnp.int32, (tq, tq), 0)

        def step(j, carry, diag):
            ks = pl.multiple_of(j * tq, tq)
            new = []
            for h in range(2):
                m, l, acc = carry[h]
                sc = lax.dot_general(qs[h], k_refs[h][pl.ds(ks, tq), :], NT, preferred_element_type=F32)
                if diag:
                    sc = jnp.where(causal, sc, -jnp.inf)
                m_new = jnp.maximum(m, jnp.max(sc, axis=1, keepdims=True))
                p = jnp.exp2(sc - m_new)
                alpha = jnp.exp2(m - m_new)
                l = alpha * l + jnp.sum(p, axis=1, keepdims=True)
                acc = alpha * acc + jnp.dot(p.astype(BF16), v_refs[h][pl.ds(ks, tq), :], preferred_element_type=F32)
                new.append((m_new, l, acc))
            return tuple(new)

        init = tuple((jnp.full((tq, 1), -jnp.inf, F32), jnp.zeros((tq, 1), F32), jnp.zeros((tq, LANES), F32)) for _ in range(2))
        carry = lax.fori_loop(0, qi, lambda j, c: step(j, c, False), init)
        (m0, l0, a0), (m1, l1, a1) = step(qi, carry, True)
        _, li, lo = _lane_ids((tq, LANES))
        o_ref[...] = jnp.where(lo, a0 / l0, a1 / l1).astype(o_ref.dtype)
        lse_lanes = (li >= 6) & (li < 9)
        for q, m, l, spare, out_ref in ((qs[0], m0, l0, ~lo, qbe_ref), (qs[1], m1, l1, lo, qbo_ref)):
            neg_lse = _triple(li, 6, _split3(-(m + jnp.log(l) * LOG2E)))
            out_ref[...] = jnp.where(spare & lse_lanes, neg_lse.astype(BF16), q)

    blk, full = _pair_specs(s, tq)
    return pl.pallas_call(
        body, name="attn_fwd", grid=(N_PAIR, s // tq),
        in_specs=[blk, blk, full, full, full, full],
        out_specs=[blk] * 3, out_shape=[jax.ShapeDtypeStruct((s, D_MIX), BF16)] * 3,
        compiler_params=_params(("parallel", "parallel")),
    )(qe, qo, ke, ko, ve, vo)


def attn_dsum(qbe, qbo, ke, ko, ve, vo, do, dep, tq=512):
    s = qbe.shape[0]
    tq = min(tq, s)

    def body(qe_ref, qo_ref, ke_ref, ko_ref, ve_ref, vo_ref, do_ref, dep_ref, dobe_ref, dobo_ref):
        qi = pl.program_id(1)
        _, li, lo = _lane_ids((tq, LANES))
        do_ = do_ref[...]
        qs, k_refs, v_refs = (qe_ref[...], qo_ref[...]), (ke_ref, ko_ref), (ve_ref, vo_ref)
        dos = (jnp.where(lo, do_, 0), jnp.where(lo, 0, do_))
        causal = lax.broadcasted_iota(jnp.int32, (tq, tq), 1) <= lax.broadcasted_iota(jnp.int32, (tq, tq), 0)

        def step(j, carry, diag):
            ks = pl.multiple_of(j * tq, tq)
            new = []
            for h in range(2):
                sc = lax.dot_general(qs[h], k_refs[h][pl.ds(ks, tq), :], NT, preferred_element_type=F32)
                if diag:
                    sc = jnp.where(causal, sc, -jnp.inf)
                pdp = jnp.exp2(sc) * lax.dot_general(dos[h], v_refs[h][pl.ds(ks, tq), :], NT, preferred_element_type=F32)
                new.append(carry[h] + jnp.sum(pdp, axis=1, keepdims=True))
            return tuple(new)

        init = (jnp.zeros((tq, 1), F32), jnp.zeros((tq, 1), F32))
        carry = lax.fori_loop(0, qi, lambda j, c: step(j, c, False), init)
        s0, s1 = step(qi, carry, True)
        dobe_ref[...] = jnp.where(lo, do_, _triple(li, 0, _split3(-s0)).astype(BF16))
        dobo_ref[...] = jnp.where(lo, _triple(li, 0, _split3(-s1)).astype(BF16), do_)

    blk, full = _pair_specs(s, tq)
    return pl.pallas_call(
        body, name="attn_dsum", grid=(N_PAIR, s // tq),
        in_specs=[blk, blk, full, full, full, full, blk, pl.BlockSpec(dep.shape, lambda p, i: (0, 0))],
        out_specs=[blk, blk], out_shape=[jax.ShapeDtypeStruct((s, D_MIX), BF16)] * 2,
        compiler_params=_params(("parallel", "parallel")),
    )(qbe, qbo, ke, ko, ve, vo, do, dep)


def attn_dkv(ke, ko, ve, vo, qbe, qbo, dobe, dobo, dproj, lane_block, tk=512):
    s = ke.shape[0]
    tk = min(tk, s)
    nq = s // tk
    tn = (((0,), (0,)), ((), ()))

    def body(ke_ref, ko_ref, ve_ref, vo_ref, qe_ref, qo_ref, de_ref, do_ref, dproj_in, dk_ref, dv_ref, dck_ref, dq_ref):
        kj = pl.program_id(1)
        lo = _lane_lt64((tk, LANES))
        ks_, vs_, q_refs, d_refs = (ke_ref[...], ko_ref[...]), (ve_ref[...], vo_ref[...]), (qe_ref, qo_ref), (de_ref, do_ref)
        causal = lax.broadcasted_iota(jnp.int32, (tk, tk), 0) <= lax.broadcasted_iota(jnp.int32, (tk, tk), 1)

        @pl.when(kj == 0)
        def _():
            dq_ref[...] = jnp.zeros(dq_ref.shape, F32)

        def step(i, carry, diag):
            qs = pl.multiple_of(i * tk, tk)
            new, dq_h = [], []
            for h in range(2):
                dk, dv, dck = carry[h]
                qblk = q_refs[h][pl.ds(qs, tk), :]
                dblk = d_refs[h][pl.ds(qs, tk), :]
                st = lax.dot_general(ks_[h], qblk, NT, preferred_element_type=F32)
                if diag:
                    st = jnp.where(causal, st, -jnp.inf)
                pt = jnp.exp2(st)
                dst = pt * lax.dot_general(vs_[h], dblk, NT, preferred_element_type=F32)
                dst_bf = dst.astype(BF16)
                dq_h.append(lax.dot_general(dst_bf, ks_[h], tn, preferred_element_type=F32))
                new.append((dk + jnp.dot(dst_bf, qblk, preferred_element_type=F32),
                            dv + jnp.dot(pt.astype(BF16), dblk, preferred_element_type=F32),
                            dck - jnp.sum(dst, axis=1, keepdims=True)))
            dq_ref[pl.ds(qs, tk), :] += jnp.where(lo, dq_h[0], dq_h[1])
            return tuple(new)

        init = tuple((jnp.zeros((tk, LANES), F32), jnp.zeros((tk, LANES), F32), jnp.zeros((tk, 1), F32)) for _ in range(2))
        carry = step(kj, init, True)
        (dk0, dv0, dc0), (dk1, dv1, dc1) = lax.fori_loop(kj + 1, nq, lambda i, c: step(i, c, False), carry)
        dk_ref[...] = (jnp.where(lo, dk0, dk1) * (1.0 / LOG2E)).astype(dk_ref.dtype)
        dv_ref[...] = jnp.where(lo, dv0, dv1).astype(dv_ref.dtype)
        dck_ref[...] = jnp.where(lo, dc0, dc1)

    blk, full = _pair_specs(s, tk)
    return pl.pallas_call(
        body, name="attn_dkv", grid=(N_PAIR, nq),
        in_specs=[blk, blk, blk, blk, full, full, full, full, pl.BlockSpec(memory_space=pl.ANY)],
        out_specs=[pl.BlockSpec((tk, LANES), lambda p, i: (i, lane_block + p)), blk, blk, full],
        out_shape=[jax.ShapeDtypeStruct(dproj.shape, dproj.dtype), jax.ShapeDtypeStruct((s, D_MIX), BF16), jax.ShapeDtypeStruct((s, D_MIX), F32),
                   jax.ShapeDtypeStruct((s, D_MIX), F32)],
        input_output_aliases={8: 0}, compiler_params=_params(("parallel", "arbitrary")),
    )(ke, ko, ve, vo, qbe, qbo, dobe, dobo, dproj)


def _pre_bwd(name, x, g, sc, sh, dh, dres):
    d = x.shape[1]

    def fn(xv, dhv, dresv, gv, scv, shv):
        _, vjp = jax.vjp(lambda *a: _f_pre(*a)[0], xv, gv, scv, shv)
        dx, dg, dsc, dsh = vjp(dhv.astype(F32))
        return dx + dresv, dg, dsc, dsh

    return rowwise(name, fn, [(x, 0, d), (dh, 0, d), (dres, 0, d)], [g, sc, sh], [(d, F32)], accs=[(1, d)] * 3)


def layer_fwd(x, mod, layer):
    s, d = x.shape
    m = D_MIX
    w = dict(layer["small"])
    sh1, sc1, gt1, sh2, sc2, gt2 = (mod[i:i + 1] for i in range(6))
    cb = 3 * d // m
    (h,) = rowwise("pre1", _f_pre, [(x, 0, d)], [w["mix_pre_g"], sc1, sh1], [(d, BF16)])
    w.update(layer["get_in"](h))
    proj, fproj = matmul("w_in", h, w["w_in"], out_dtypes=(BF16,), f32_tail=True, tn=2 * m)
    fcb = 1
    w.update(layer["get_abco"](proj))
    u, vln = rowwise("gmlp_in", _f_a1, [(proj, cb, m), (proj, cb + 1, m)], [w["gmlp_ln_g"], w["gmlp_ln_b"]], [(m, F32), (m, BF16)])
    sv, ya = spatial_fwd(vln, u, w["ws"], w["bs_exp"])
    y_a = matmul("w_a", ya, w["w_a_out"], out_dtypes=(BF16,))
    zc, yb = conv_fwd(proj, cb + 2, cb + 3, w["conv_w"], w["conv_b"], w["conv_ln_g"], w["conv_ln_b"])
    y_b = matmul("w_b", yb, w["w_b_out"], out_dtypes=(BF16,))
    cum = forget_cumsum(fproj, fcb, w["bf_exp"])
    kv_ops = attn_prep(proj, cb + 4, cum)
    o, qbe, qbo = attn_fwd(*kv_ops)
    att = (qbe, qbo) + tuple(kv_ops[2:])
    y_c = matmul("w_c", o, w["w_c_out"], out_dtypes=(BF16,))
    (merged,) = rowwise("merge", _f_merge, [(proj, 0, d), (proj, 1, d), (proj, 2, d), (y_a, 0, d), (y_b, 0, d), (y_c, 0, d)], [], [(d, BF16)])
    y = matmul("w_out", merged, w["w_out"], out_dtypes=(BF16,))
    w.update(layer["get_mlp"](y))

    def post_pre(xv, yv, gp, gt, g2, sc, sh):
        x1 = xv + _f_post(yv, gp, gt)[0]
        return x1, _f_pre(x1, g2, sc, sh)[0]

    x1, h2 = rowwise("post1", post_pre, [(x, 0, d), (y, 0, d)], [w["mix_post_g"], gt1, w["mlp_pre_g"], sc2, sh2], [(d, F32), (d, BF16)])
    r = matmul("w1", h2, w["mlp_w1"], out_dtypes=(BF16,), epilogue=lambda acc: (jnp.square(jnp.maximum(acc, 0.0)),))
    y2 = matmul("w2", r, w["mlp_w2"], out_dtypes=(BF16,))
    (x2,) = rowwise("post2", lambda xv, yv, g, gt: xv + _f_post(yv, g, gt)[0], [(x1, 0, d), (y2, 0, d)], [w["mlp_post_g"], gt2], [(d, F32)])
    saved = dict(w=w, x=x, h=h, proj=proj, fproj=fproj, fcb=fcb, u=u, vln=vln, sv=sv, ya=ya, y_a=y_a, zc=zc, yb=yb, y_b=y_b, att=att,
                 o=o, y_c=y_c, merged=merged, y=y, x1=x1, h2=h2, r=r, y2=y2)
    return x2, saved


def layer_bwd(dx2, mod, sv, emit, tok_in=None):
    x, proj, w = sv["x"], sv["proj"], sv["w"]
    s, d = x.shape
    m = D_MIX
    sh1, sc1, gt1, sh2, sc2, gt2 = (mod[i:i + 1] for i in range(6))
    cb = 3 * d // m
    g = {}
    if tok_in is not None:
        gt2 = gt2 + tok_in[0:1, 0:1]
    dy2, g["mlp_post_g"], dgt2 = rowwise_vjp("post2_b", _f_post, [(sv["y2"], 0, d)], [w["mlp_post_g"], gt2], [(dx2, 0, d)], [BF16])
    da = matmul("w2_dx", dy2, w["mlp_w2"], tb=True, out_dtypes=(BF16,),
                epilogue=lambda acc, r: (acc * (2.0 * jnp.sqrt(r.astype(F32))),), epi=[(sv["r"], 0)])
    big = {}
    big["mlp_w2"] = matmul("w2_dw", sv["r"], dy2, ta=True, out_dtypes=(BF16,))
    dh2 = matmul("w1_dx", da, w["mlp_w1"], tb=True, out_dtypes=(BF16,))
    big["mlp_w1"] = matmul("w1_dw", sv["h2"], da, ta=True, out_dtypes=(BF16,), out_blocks=N_DEV)
    tok = emit("mlp", big)
    dx1, g["mlp_pre_g"], dsc2, dsh2 = _pre_bwd("pre2_b", sv["x1"], w["mlp_pre_g"], sc2 + tok[0:1, 0:1], sh2, dh2, dx2)
    dy, g["mix_post_g"], dgt1 = rowwise_vjp("post1_b", _f_post, [(sv["y"], 0, d)], [w["mix_post_g"], gt1], [(dx1, 0, d)], [BF16])
    dmerged = matmul("w_out_dx", dy, w["w_out"], tb=True, out_dtypes=(BF16,))
    big = {}
    big["w_out"] = matmul("w_out_dw", sv["merged"], dy, ta=True, out_dtypes=(BF16,))
    assert (3 * d) % (2 * m) == 0
    dproj = lax.empty((s, 3 * d + 8 * m), BF16)
    dproj, dya_, dyb_, dyc_ = rowwise_vjp(
        "merge_b", _f_merge, [(proj, 0, d), (proj, 1, d), (proj, 2, d), (sv["y_a"], 0, d), (sv["y_b"], 0, d), (sv["y_c"], 0, d)], [],
        [(dmerged, 0, d)], [BF16] * 6, into=(dproj, 0, 3))
    dya_pre = matmul("w_a_dx", dya_, w["w_a_out"], tb=True, out_dtypes=(BF16,))
    big["w_a_out"] = matmul("w_a_dw", sv["ya"], dya_, ta=True, out_dtypes=(BF16,))
    dyb_pre = matmul("w_b_dx", dyb_, w["w_b_out"], tb=True, out_dtypes=(BF16,))
    big["w_b_out"] = matmul("w_b_dw", sv["yb"], dyb_, ta=True, out_dtypes=(BF16,))
    do = matmul("w_c_dx", dyc_, w["w_c_out"], tb=True, out_dtypes=(BF16,))
    big["w_c_out"] = matmul("w_c_dw", sv["o"], dyc_, ta=True, out_dtypes=(BF16,))
    tok = emit("abco", big)
    qbe, qbo, ke, ko, ve, vo = sv["att"]
    lane0 = (cb + 4) * (m // LANES)
    dobe, dobo = attn_dsum(qbe, qbo, ke, ko, ve, vo, do, tok)
    dproj, dv, dcum, dq = attn_dkv(ke, ko, ve, vo, qbe, qbo, dobe, dobo, dproj, lane0 + N_PAIR)
    (dproj,) = rowwise("dq_scale", lambda g: g * (1.0 / math.sqrt(HEAD_DIM)), [(dq, 0, m)], [], [(m, BF16)], into=(dproj, cb + 4))
    dproj = lax.dynamic_update_slice(dproj, dv, (0, (cb + 6) * m))
    dproj, dbf = forget_bwd(sv["fproj"], sv["fcb"], w["bf_exp"], dcum, dproj, cb + 7)
    g["fox_bf"] = dbf[0, ::HEAD_DIM]
    dproj, dwc, g["conv_b"], g["conv_ln_g"], g["conv_ln_b"] = conv_bwd(
        proj, cb + 2, cb + 3, sv["zc"], dyb_pre, w["conv_w"], w["conv_ln_g"], w["conv_ln_b"], dproj)
    g["conv_w"] = dwc[:CONV_WIDTH]
    du, dvln, dws, dbexp = spatial_bwd(dya_pre, sv["u"], sv["sv"], sv["vln"], w["ws_t"])
    g["gmlp_ws"] = dws * jnp.tril(jnp.ones((CHUNK, CHUNK), F32))
    g["gmlp_bs"] = dbexp.reshape(CHUNK, m // GROUP_DIM, GROUP_DIM).sum(-1).T
    dproj, g["gmlp_ln_g"], g["gmlp_ln_b"] = rowwise_vjp(
        "gmlp_in_b", _f_a1, [(proj, cb, m), (proj, cb + 1, m)], [w["gmlp_ln_g"], w["gmlp_ln_b"]], [(du, 0, m), (dvln, 0, m)], [BF16, BF16],
        into=(dproj, cb // 2, 2))
    tok = emit("in", {"w_in": matmul("w_in_dw", sv["h"], dproj, ta=True, out_dtypes=(BF16,))})
    dh = matmul("w_in_dx", dproj, w["w_in"], tb=True, dep=tok, out_dtypes=(BF16,))
    dx, g["mix_pre_g"], dsc1, dsh1 = _pre_bwd("pre1_b", x, w["mix_pre_g"], sc1, sh1, dh, dx1)
    dmod = jnp.concatenate([dsh1, dsc1, dgt1, dsh2, dsc2, dgt2], axis=0)
    return dx, dmod, g


def local_step(x, target, mods, layers):
    d = x.shape[1]
    saved = []
    for l in range(len(layers)):
        x, sv = layer_fwd(x, mods[l], layers[l])
        saved.append(sv)

    def loss_fn(xv, tv):
        err = xv - tv
        return err * (1.0 / d), jnp.sum(err * err, axis=0, keepdims=True)

    dx, sq = rowwise("loss", loss_fn, [(x, 0, d), (target, 0, d)], [], [(d, F32)], accs=[(1, d)])
    loss = (0.5 / d) * jnp.sum(sq)
    dmods, grads = [None] * len(layers), [None] * len(layers)
    tok = None
    for l in reversed(range(len(layers))):
        dx, dmods[l], grads[l] = layer_bwd(dx, mods[l], saved[l], layers[l]["emit"], tok)
        tok = layers[l]["emit_small"](dmods[l], grads[l])
    return loss, dx, dmods, grads


def exchange(name, arrs, scatter):
    n = len(arrs)

    def body(*refs):
        in_refs, out_refs = refs[:n], refs[n:2 * n]
        send_sems, recv_sems, local_sems = refs[2 * n:]
        x, y, c = lax.axis_index("x"), lax.axis_index("y"), lax.axis_index("c")
        me = 4 * x + 2 * y + c
        local = []
        for a in range(n):
            src = in_refs[a].at[me] if scatter else in_refs[a]
            cp = pltpu.make_async_copy(src, out_refs[a].at[me], local_sems.at[a])
            cp.start()
            local.append(cp)
        remote = []
        for k in range(1, N_DEV):
            px, py, pc = x ^ ((k >> 2) & 1), y ^ ((k >> 1) & 1), c ^ (k & 1)
            peer = 4 * px + 2 * py + pc
            for a in range(n):
                src = in_refs[a].at[peer] if scatter else in_refs[a]
                cp = pltpu.make_async_remote_copy(
                    src_ref=src, dst_ref=out_refs[a].at[me], send_sem=send_sems.at[a * (N_DEV - 1) + k - 1],
                    recv_sem=recv_sems.at[a * (N_DEV - 1) + k - 1], device_id=(px, py, pc), device_id_type=MESH)
                cp.start()
                remote.append(cp)
        for cp in remote:
            cp.wait()
        for cp in local:
            cp.wait()

    hbm = pl.BlockSpec(memory_space=pltpu.HBM)
    out_shape = [jax.ShapeDtypeStruct(a.shape if scatter else (N_DEV,) + a.shape, a.dtype) for a in arrs]
    return pl.pallas_call(
        body, name=name, in_specs=[hbm] * n, out_specs=[hbm] * n, out_shape=out_shape,
        scratch_shapes=[pltpu.SemaphoreType.DMA((n * (N_DEV - 1),)), pltpu.SemaphoreType.DMA((n * (N_DEV - 1),)),
                        pltpu.SemaphoreType.DMA((n,))],
    )(*arrs)


def _peers(x, y, c):
    out = []
    for k in range(1, N_DEV):
        px, py, pc = x ^ ((k >> 2) & 1), y ^ ((k >> 1) & 1), c ^ (k & 1)
        out.append((k - 1, (px, py, pc), 4 * px + 2 * py + pc))
    return out


def _exchange_copies(srcs, lands, send_sems, recv_sems, scatter):
    x, y, c = lax.axis_index("x"), lax.axis_index("y"), lax.axis_index("c")
    me = 4 * x + 2 * y + c
    copies = []
    for slot, pos, peer in _peers(x, y, c):
        for a, (src, land) in enumerate(zip(srcs, lands)):
            copies.append(pltpu.make_async_remote_copy(
                src_ref=src.at[peer] if scatter else src, dst_ref=land.at[me],
                send_sem=send_sems.at[a * (N_DEV - 1) + slot], recv_sem=recv_sems.at[a * (N_DEV - 1) + slot],
                device_id=pos, device_id_type=MESH))
    return me, copies


def exchange_start(name, arrs, scatter, after=None):
    n = len(arrs)
    lands = [lax.empty(a.shape if scatter else (N_DEV,) + a.shape, a.dtype) for a in arrs]
    n_in = 2 * n + (after is not None)

    def body(*refs):
        srcs, lands_ = refs[:n], refs[n:2 * n]
        send_sems, recv_sems = refs[n_in], refs[n_in + 1]
        token = refs[n_in + 2 + 2 * n]
        _, copies = _exchange_copies(srcs, lands_, send_sems, recv_sems, scatter)
        for cp in copies:
            cp.start()
        token[...] = jnp.zeros(token.shape, token.dtype)

    hbm = pl.BlockSpec(memory_space=pltpu.HBM)
    sem = pl.BlockSpec(memory_space=pltpu.SEMAPHORE)
    n_sem = n * (N_DEV - 1)
    res = pl.pallas_call(
        body, name=name,
        out_shape=(pltpu.SemaphoreType.DMA((n_sem,)), pltpu.SemaphoreType.DMA((n_sem,)),
                   *[pltpu.HBM(a.shape, a.dtype) for a in arrs], *[pltpu.HBM(l.shape, l.dtype) for l in lands],
                   jax.ShapeDtypeStruct((8, LANES), F32)),
        in_specs=[hbm] * (2 * n) + ([] if after is None else [pl.BlockSpec(memory_space=pl.ANY)]),
        out_specs=(sem, sem, *([hbm] * (2 * n)), pl.BlockSpec(memory_space=pltpu.VMEM)),
        input_output_aliases={i: 2 + i for i in range(2 * n)},
        compiler_params=pltpu.CompilerParams(has_side_effects=pltpu.SideEffectType.DATAFLOW_SIDE_EFFECTING),
    )(*[pltpu.with_memory_space_constraint(a, pltpu.HBM) for a in arrs],
      *[pltpu.with_memory_space_constraint(l, pltpu.HBM) for l in lands], *([] if after is None else [after]))
    return dict(n=n, scatter=scatter, send=res[0], recv=res[1], srcs=res[2:2 + n], lands=res[2 + n:2 + 2 * n], token=res[2 + 2 * n])


def exchange_wait(name, st, after):
    n, scatter = st["n"], st["scatter"]
    after = list(after) if isinstance(after, (list, tuple)) else [after]

    def body(*refs):
        srcs, lands_ = refs[:n], refs[n:2 * n]
        send_sems, recv_sems = refs[2 * n], refs[2 * n + 1]
        _, copies = _exchange_copies(srcs, lands_, send_sems, recv_sems, scatter)
        for cp in copies:
            cp.wait_send()
            cp.wait_recv()

    hbm = pl.BlockSpec(memory_space=pltpu.HBM)
    sem = pl.BlockSpec(memory_space=pltpu.SEMAPHORE)
    res = pl.pallas_call(
        body, name=name,
        out_shape=tuple(pltpu.HBM(a.shape, a.dtype) for a in (*st["srcs"], *st["lands"])),
        in_specs=[hbm] * (2 * n) + [sem, sem] + [pl.BlockSpec(memory_space=pl.ANY)] * len(after), out_specs=tuple([hbm] * (2 * n)),
        input_output_aliases={i: i for i in range(2 * n)},
        compiler_params=pltpu.CompilerParams(has_side_effects=pltpu.SideEffectType.DATAFLOW_SIDE_EFFECTING),
    )(*st["srcs"], *st["lands"], st["send"], st["recv"], *after)
    return list(res[:n]), list(res[n:])


def adamw_sum(name, parts, w, m, v, tr=256):
    nl = len(parts)
    k, r, c = parts[0].shape
    tr = _tile(r, tr, 16)
    c1 = 1.0 - ADAM_B1 ** ADAM_STEP
    c2 = 1.0 - ADAM_B2 ** ADAM_STEP

    def body(*refs):
        p_refs = refs[:nl]
        w_ref, m_ref, v_ref, g_ref, d_ref, nm_ref, nv_ref = refs[nl:]
        for l in range(nl):
            @pl.when(pl.program_id(0) == l)
            def _(p_ref=p_refs[l]):
                grad = p_ref[0].astype(F32)
                for j in range(1, k):
                    grad = grad + p_ref[j].astype(F32)
                new_m = ADAM_B1 * m_ref[...] + (1.0 - ADAM_B1) * grad
                new_v = ADAM_B2 * v_ref[...] + (1.0 - ADAM_B2) * (grad * grad)
                m_hat = new_m / c1
                v_hat = new_v / c2
                g_ref[...] = grad
                d_ref[...] = -ADAM_LR * (m_hat / (jnp.sqrt(v_hat) + ADAM_EPS) + ADAM_WD * w_ref[...])
                nm_ref[...] = new_m
                nv_ref[...] = new_v

    part = lambda l: pl.BlockSpec((k, tr, c), functools.partial(lambda ll, i, l: (0, jnp.where(ll == l, i, 0), 0), l=l))
    blk = pl.BlockSpec((None, tr, c), lambda ll, i: (ll, i, 0))
    return pl.pallas_call(
        body, name=name, grid=(nl, r // tr),
        in_specs=[part(l) for l in range(nl)] + [blk, blk, blk],
        out_specs=[blk] * 4, out_shape=[jax.ShapeDtypeStruct((nl, r, c), F32)] * 4,
        compiler_params=_params(("parallel", "parallel")),
    )(*parts, w, m, v)


def ada_fwd(c_all, ada_w):
    nl, d, n = ada_w.shape

    def body(c_ref, w_ref, o_ref):
        o_ref[...] = jnp.dot(_silu(c_ref[...]), w_ref[...], precision=HIGHEST, preferred_element_type=F32)

    return pl.pallas_call(
        body, name="ada_fwd", grid=(nl,),
        in_specs=[pl.BlockSpec((N_DEV, d), lambda l: (0, 0)), pl.BlockSpec((None, d, n), lambda l: (l, 0, 0))],
        out_specs=pl.BlockSpec((None, N_DEV, n), lambda l: (l, 0, 0)),
        out_shape=jax.ShapeDtypeStruct((nl, N_DEV, n), F32),
        compiler_params=_params(("parallel",)),
    )(c_all, ada_w)


def ada_bwd(c_all_t, dmod, td=256):
    d = c_all_t.shape[0]
    nl, _, n = dmod.shape
    td = _tile(d, td, 8)

    def body(c_ref, dm_ref, o_ref):
        ca = _silu(c_ref[...])
        acc = ca[:, 0:1] * dm_ref[0:1, :]
        for b in range(1, N_DEV):
            acc = acc + ca[:, b:b + 1] * dm_ref[b:b + 1, :]
        o_ref[...] = acc

    return pl.pallas_call(
        body, name="ada_bwd", grid=(nl, d // td),
        in_specs=[pl.BlockSpec((td, N_DEV), lambda l, i: (i, 0)), pl.BlockSpec((None, N_DEV, n), lambda l, i: (l, 0, 0))],
        out_specs=pl.BlockSpec((None, td, n), lambda l, i: (l, i, 0)),
        out_shape=jax.ShapeDtypeStruct((nl, d, n), F32),
        compiler_params=_params(("parallel", "parallel")),
    )(c_all_t, dmod)


ARG_NAMES = ["x", "c", "ada_w", "ada_b", "mix_pre_g", "mix_post_g", "mlp_pre_g", "mlp_post_g", "w_in", "gmlp_ln_g", "gmlp_ln_b",
             "gmlp_ws", "gmlp_bs", "w_a_out", "conv_w", "conv_b", "conv_ln_g", "conv_ln_b", "w_b_out", "fox_bf", "w_c_out",
             "w_out", "mlp_w1", "mlp_w2"]
WEIGHTS = ARG_NAMES[2:]
COL_SHARDED = ["w_in", "w_a_out", "w_b_out", "w_c_out", "mlp_w1"]
ROW_SHARDED = ["w_out", "mlp_w2"]
BIG = COL_SHARDED + ROW_SHARDED
GROUPS = {"in": ["w_in"], "abco": ["w_a_out", "w_b_out", "w_c_out", "w_out"], "mlp": ["mlp_w1", "mlp_w2"]}
SMALL = ["ada_b", "mix_pre_g", "mix_post_g", "mlp_pre_g", "mlp_post_g", "gmlp_ln_g", "gmlp_ln_b", "gmlp_ws", "gmlp_bs",
         "conv_b", "conv_ln_g", "conv_ln_b", "fox_bf"]
PACK_COLS = 512


def _ref_ranges(lo, hi, shard):
    out = []
    while lo < hi:
        j = lo // shard
        end = min(hi, (j + 1) * shard)
        out.append((j, lo - j * shard, end - j * shard))
        lo = end
    return out


def _w_in_to_kernel_layout(g, d):
    m = D_MIX
    nf = 7 * m
    shard = g.shape[2]
    cols = lambda lo, hi: [g[j, :, a:b] for j, a, b in _ref_ranges(lo, hi, shard)]
    forget = jnp.concatenate(cols(nf, nf + N_HEADS), axis=1)
    return jnp.concatenate(cols(nf + N_HEADS, nf + N_HEADS + 3 * d) + cols(0, nf) + [jnp.repeat(forget, HEAD_DIM, axis=1)], axis=1)


def _w_in_grad_blocks(gw, d):
    m = D_MIX
    nf = 7 * m
    n_ref = nf + N_HEADS + 3 * d
    shard = n_ref // N_DEV
    segs = [(0, nf, 3 * d, 1), (nf, nf + N_HEADS, 3 * d + nf, HEAD_DIM), (nf + N_HEADS, n_ref, 0, 1)]
    blocks = []
    for j in range(N_DEV):
        lo, hi = j * shard, (j + 1) * shard
        pieces = []
        for r0, r1, k0, stride in segs:
            a, b = max(lo, r0), min(hi, r1)
            if a < b:
                pieces.append(gw[:, k0 + (a - r0) * stride:k0 + (b - r0) * stride:stride])
        blocks.append(jnp.concatenate(pieces, axis=1) if len(pieces) > 1 else pieces[0])
    return jnp.stack(blocks)


def _pack(parts):
    flat = jnp.concatenate([p.reshape(-1).astype(F32) for p in parts])
    pad = (-flat.shape[0]) % (PACK_COLS * 8)
    return jnp.pad(flat, (0, pad)).reshape(-1, PACK_COLS)


def _unpack(packed, shapes):
    nl = packed.shape[0]
    flat, out, off = packed.reshape(nl, -1), [], 0
    for shp in shapes:
        n = math.prod(shp)
        out.append(flat[:, off:off + n].reshape((nl,) + tuple(shp)))
        off += n
    return out


def _layer_small(p, conv_full, l):
    wl = {}
    for k in ["mix_pre_g", "mix_post_g", "mlp_pre_g", "mlp_post_g", "gmlp_ln_g", "gmlp_ln_b", "conv_b", "conv_ln_g", "conv_ln_b"]:
        wl[k] = p[k][l][None, :]
    wm = p["gmlp_ws"][l] * jnp.tril(jnp.ones((CHUNK, CHUNK), F32))
    wl["ws"] = wm.astype(BF16)
    wl["ws_t"] = jnp.transpose(wm, (0, 2, 1)).astype(BF16)
    wl["bs_exp"] = jnp.repeat(p["gmlp_bs"][l].T, GROUP_DIM, axis=1)
    wl["bf_exp"] = jnp.repeat(p["fox_bf"][l], HEAD_DIM)[None, :]
    wl["conv_w"] = jnp.pad(conv_full[l], ((0, CONV_HALO - CONV_WIDTH), (0, 0)))
    return wl


def kernel(x, c, ada_w, ada_b, mix_pre_g, mix_post_g, mlp_pre_g, mlp_post_g, w_in, gmlp_ln_g, gmlp_ln_b, gmlp_ws, gmlp_bs, w_a_out, conv_w, conv_b, conv_ln_g, conv_ln_b, w_b_out, fox_bf, w_c_out, w_out, mlp_w1, mlp_w2, loss_target, m_ada_w, m_ada_b, m_mix_pre_g, m_mix_post_g, m_mlp_pre_g, m_mlp_post_g, m_w_in, m_gmlp_ln_g, m_gmlp_ln_b, m_gmlp_ws, m_gmlp_bs, m_w_a_out, m_conv_w, m_conv_b, m_conv_ln_g, m_conv_ln_b, m_w_b_out, m_fox_bf, m_w_c_out, m_w_out, m_mlp_w1, m_mlp_w2, v_ada_w, v_ada_b, v_mix_pre_g, v_mix_post_g, v_mlp_pre_g, v_mlp_post_g, v_w_in, v_gmlp_ln_g, v_gmlp_ln_b, v_gmlp_ws, v_gmlp_bs, v_w_a_out, v_conv_w, v_conv_b, v_conv_ln_g, v_conv_ln_b, v_w_b_out, v_fox_bf, v_w_c_out, v_w_out, v_mlp_w1, v_mlp_w2):
    args = (x, c, ada_w, ada_b, mix_pre_g, mix_post_g, mlp_pre_g, mlp_post_g, w_in, gmlp_ln_g, gmlp_ln_b, gmlp_ws, gmlp_bs, w_a_out,
            conv_w, conv_b, conv_ln_g, conv_ln_b, w_b_out, fox_bf, w_c_out, w_out, mlp_w1, mlp_w2)
    ms = (m_ada_w, m_ada_b, m_mix_pre_g, m_mix_post_g, m_mlp_pre_g, m_mlp_post_g, m_w_in, m_gmlp_ln_g, m_gmlp_ln_b, m_gmlp_ws, m_gmlp_bs,
          m_w_a_out, m_conv_w, m_conv_b, m_conv_ln_g, m_conv_ln_b, m_w_b_out, m_fox_bf, m_w_c_out, m_w_out, m_mlp_w1, m_mlp_w2)
    vs = (v_ada_w, v_ada_b, v_mix_pre_g, v_mix_post_g, v_mlp_pre_g, v_mlp_post_g, v_w_in, v_gmlp_ln_g, v_gmlp_ln_b, v_gmlp_ws, v_gmlp_bs,
          v_w_a_out, v_conv_w, v_conv_b, v_conv_ln_g, v_conv_ln_b, v_w_b_out, v_fox_bf, v_w_c_out, v_w_out, v_mlp_w1, v_mlp_w2)
    p = dict(zip(ARG_NAMES, args))
    mom = dict(zip(WEIGHTS, ms))
    var = dict(zip(WEIGHTS, vs))
    nl = ada_w.shape[0]
    s, d = x.shape[1], x.shape[2]
    me = 4 * lax.axis_index("x") + 2 * lax.axis_index("y") + lax.axis_index("c")

    c_all, conv_all = exchange("gather_c", [c, conv_w], scatter=False)
    c_all = c_all.reshape(N_DEV, d)
    n_ada = ada_w.shape[2]
    mod_parts = ada_fwd(c_all, ada_w)
    (mod_recv,) = exchange("scatter_mod", [jnp.transpose(mod_parts, (1, 0, 2))], scatter=True)
    conv_full = jnp.transpose(conv_all, (1, 2, 0, 3)).reshape(nl, CONV_WIDTH, D_MIX)

    def full_matrix(k, land, own):
        g = lax.dynamic_update_index_in_dim(land, own, me, 0)
        r, cc = own.shape
        if k == "w_in":
            return _w_in_to_kernel_layout(g, d)
        return jnp.transpose(g, (1, 0, 2)).reshape(r, N_DEV * cc) if k in COL_SHARDED else g.reshape(N_DEV * r, cc)

    started = [mod_recv]

    def fetch(l, tag):
        keys = GROUPS[tag]
        st = exchange_start(f"gather_{tag}{l}_start", [p[k][l].astype(BF16) for k in keys], scatter=False, after=started[-1])
        started.append(st["token"])

        def get(after):
            owns, lands = exchange_wait(f"gather_{tag}{l}_wait", st, after)
            return {k: full_matrix(k, land, o) for k, land, o in zip(keys, lands, owns)}

        return get

    getters = [{tag: fetch(l, tag) for tag in GROUPS} for l in range(nl)]
    mod = jnp.transpose(mod_recv, (1, 0, 2)).reshape(nl, N_DEV * n_ada) + ada_b + started[-1][0:1, 0:1]
    mods = [mod[l].reshape(6, d) for l in range(nl)]

    sent = {}

    def emitter(l):
        def emit(tag, grads_big):
            keys = GROUPS[tag]
            send = []
            for k in keys:
                gk = grads_big[k]
                if k == "w_in":
                    gk = _w_in_grad_blocks(gk, d)
                elif gk.ndim == 3:
                    pass
                elif k in COL_SHARDED:
                    r, cc = gk.shape[0], gk.shape[1] // N_DEV
                    gk = jnp.transpose(gk.reshape(r, N_DEV, cc), (1, 0, 2))
                else:
                    gk = gk.reshape(N_DEV, gk.shape[0] // N_DEV, gk.shape[1])
                send.append(gk.astype(BF16))
            sent[(l, tag)] = (keys, exchange_start(f"scatter_{tag}{l}_start", send, scatter=True))
            return sent[(l, tag)][1]["token"]

        return emit

    small_sent = {}

    def small_emitter(l):
        def emit_small(dmod_l, g):
            packed = _pack([dmod_l] + [g[k] for k in SMALL[1:]] + [g["conv_w"]])
            small_sent[l] = exchange_start(f"gather_small{l}_start", [packed], scatter=False)
            return small_sent[l]["token"]

        return emit_small

    layers = [dict(small=_layer_small(p, conv_full, l), emit_small=small_emitter(l), get_in=getters[l]["in"], get_abco=getters[l]["abco"], get_mlp=getters[l]["mlp"],
                   emit=emitter(l)) for l in range(nl)]
    loss_local, dx, dmods, grads = local_step(x[0], loss_target[0], mods, layers)
    loss = lax.psum(loss_local, ("x", "y", "c"))
    grad_x = dx[None]

    out = {k: [None] * 4 for k in WEIGHTS}

    def shard_update(name, parts, k):
        shp = p[k].shape
        flat = lambda a: a.reshape(nl, -1, shp[-1])
        res = adamw_sum(name, [pt.reshape((pt.shape[0],) + flat(p[k]).shape[1:]) for pt in parts], flat(p[k]), flat(mom[k]), flat(var[k]))
        out[k] = [a.reshape(shp) for a in res]

    parts = {}

    def collect(tag, after):
        for l in reversed(range(nl)):
            keys, st = sent[(l, tag)]
            sends, lands = exchange_wait(f"scatter_{tag}{l}_wait", st, after)
            for k, land, sd in zip(keys, lands, sends):
                own = lax.dynamic_index_in_dim(sd, me, 0, keepdims=False)
                parts[(k, l)] = lax.dynamic_update_index_in_dim(land, own, me, 0)

    behind_bwd = [dx] + [small_sent[l]["token"] for l in range(nl)]
    collect("mlp", behind_bwd)
    collect("abco", behind_bwd)
    for k in BIG[1:]:
        shard_update("adamw_" + k, [parts[(k, l)] for l in range(nl)], k)
    collect("in", [out[k][0] for k in BIG[1:]])
    shard_update("adamw_w_in", [parts[("w_in", l)] for l in range(nl)], "w_in")

    small_shapes = [p[k].shape[1:] for k in SMALL] + [(CONV_WIDTH, D_MIX)]
    small_all = []
    for l in range(nl):
        srcs, lands = exchange_wait(f"gather_small{l}_wait", small_sent[l], [out[k][0] for k in BIG])
        small_all.append(lax.dynamic_update_index_in_dim(lands[0], srcs[0], me, 0))
    zeros_conv = jnp.zeros(small_shapes[-1], F32)
    packs = [jnp.stack([_pack([src[k][l] for k in SMALL] + [zeros_conv]) for l in range(nl)]) for src in (p, mom, var)]
    small_out = [_unpack(o, small_shapes) for o in adamw_sum("adamw_small", small_all, *packs)]
    for i, k in enumerate(SMALL):
        for j in range(4):
            out[k][j] = small_out[j][i]

    n_conv = conv_w.shape[2]
    conv_grad = lax.dynamic_slice_in_dim(small_out[0][-1], me * n_conv, n_conv, axis=2)
    shard_update("adamw_conv_w", [conv_grad[l][None] for l in range(nl)], "conv_w")

    dmod_all = jnp.stack([small_all[l].reshape(N_DEV, -1)[:, :6 * d] for l in range(nl)])
    dmod_mine = lax.dynamic_slice_in_dim(dmod_all, me * n_ada, n_ada, axis=2)
    g_ada = ada_bwd(c_all.T, dmod_mine)
    shard_update("adamw_ada_w", [g_ada[l][None] for l in range(nl)], "ada_w")

    res = [loss, grad_x]
    for j in range(4):
        res += [out[k][j] for k in WEIGHTS]
    return tuple(res)
```

```python
import functools
import math

import jax
import jax.numpy as jnp
from jax import lax
from jax.experimental import pallas as pl
from jax.experimental.pallas import tpu as pltpu

F32 = jnp.float32
BF16 = jnp.bfloat16
MESH = pl.DeviceIdType.MESH
N_DEV = 8
NORM_EPS = 1e-6
D_MIX = 512
N_HEADS = 8
HEAD_DIM = 64
GROUP_DIM = 64
CHUNK = 128
CONV_WIDTH = 31
CONV_HALO = 32
LANES = 128
ADAM_LR, ADAM_B1, ADAM_B2, ADAM_EPS, ADAM_WD, ADAM_STEP = 0.001, 0.9, 0.999, 1e-08, 0.01, 10
VMEM_LIMIT = 56 * 1024 * 1024
HIGHEST = lax.Precision.HIGHEST


def _tile(dim, pref, mult=LANES):
    t = min(pref, dim)
    t -= t % mult
    while t >= mult:
        if dim % t == 0:
            return t
        t -= mult
    return dim


def _params(sem):
    return pltpu.CompilerParams(dimension_semantics=sem, vmem_limit_bytes=VMEM_LIMIT)


def rowwise(name, fn, rows, consts, outs, accs=(), ts=512, into=None):
    s = rows[0][0].shape[0]
    ts = min(ts, s)
    nr, nc, no, na = len(rows), len(consts), len(outs), len(accs)
    n_in = nr + nc + (into is not None)

    def body(*refs):
        vals = [r[...].astype(F32) for r in refs[:nr]] + [r[...] for r in refs[nr:nr + nc]]
        res = fn(*vals)
        if not isinstance(res, (tuple, list)):
            res = (res,)
        for r, v in zip(refs[n_in:n_in + no], res[:no]):
            r[...] = v.astype(r.dtype)
        if na:
            acc_refs = refs[n_in + no:]

            @pl.when(pl.program_id(0) == 0)
            def _():
                for r in acc_refs:
                    r[...] = jnp.zeros(r.shape, r.dtype)

            for r, v in zip(acc_refs, res[no:]):
                r[...] += v.astype(F32)

    in_specs = [pl.BlockSpec((ts, w), functools.partial(lambda i, cb: (i, cb), cb=cb)) for (_, cb, w) in rows]
    in_specs += [pl.BlockSpec(c.shape, lambda i: (0, 0)) for c in consts]
    out_specs = [pl.BlockSpec((ts, w), lambda i: (i, 0)) for (w, _) in outs]
    out_specs += [pl.BlockSpec(shp, lambda i: (0, 0)) for shp in accs]
    out_shape = [jax.ShapeDtypeStruct((s, w), dt) for (w, dt) in outs]
    out_shape += [jax.ShapeDtypeStruct(shp, F32) for shp in accs]
    extra, aliases = [], {}
    if into is not None:
        buf, cb_into = into
        assert buf.dtype == outs[0][1] and buf.shape[0] == s
        in_specs.append(pl.BlockSpec(memory_space=pl.ANY))
        out_specs[0] = pl.BlockSpec((ts, outs[0][0]), lambda i: (i, cb_into))
        out_shape[0] = jax.ShapeDtypeStruct(buf.shape, buf.dtype)
        extra, aliases = [buf], {nr + nc: 0}
    res = pl.pallas_call(
        body, name=name, grid=(s // ts,), in_specs=in_specs, out_specs=out_specs, out_shape=out_shape,
        input_output_aliases=aliases, compiler_params=_params(("arbitrary",) if na else ("parallel",)),
    )(*[a for (a, _, _) in rows], *consts, *extra)
    return res


def rowwise_vjp(name, f, rows, consts, cts, grad_dtypes, ts=512, into=None):
    nr, nc, nt = len(rows), len(consts), len(cts)
    keep = [i for i, dt in enumerate(grad_dtypes) if dt is not None]
    k_into = 0 if into is None else into[2]

    def g(*vals):
        rv = [v.astype(F32) for v in vals[:nr]]
        ctv = tuple(v.astype(F32) for v in vals[nr:nr + nt])
        cv = list(vals[nr + nt:])
        _, vjp = jax.vjp(lambda *a: tuple(f(*a)), *rv, *cv)
        grads = vjp(ctv)
        row_grads = [grads[i] for i in keep]
        if k_into:
            row_grads = [jnp.concatenate(row_grads[:k_into], axis=1)] + row_grads[k_into:]
        return tuple(row_grads) + tuple(grads[nr:])

    outs = [(rows[i][2], grad_dtypes[i]) for i in keep]
    if k_into:
        assert len({dt for _, dt in outs[:k_into]}) == 1
        outs = [(sum(w for w, _ in outs[:k_into]), outs[0][1])] + outs[k_into:]
    return rowwise(name, g, list(rows) + list(cts), consts, outs, accs=[c.shape for c in consts], ts=ts,
                   into=None if into is None else into[:2])


def matmul(name, a, b, *, ta=False, tb=False, out_dtypes=(F32,), epilogue=None, epi=(), tm=None, tn=None, tk=4096, dep=None, b_cols=None, out_blocks=None,
           f32_tail=False):
    m, k = (a.shape[1], a.shape[0]) if ta else a.shape
    n = b.shape[0] if tb else b.shape[1]
    b_col0 = 0
    if b_cols is not None:
        assert not tb
        b_col0, n = b_cols
    assert (b.shape[1] if tb else b.shape[0]) == k
    tk = _tile(k, tk)
    if tk > 1024:
        tm, tn = _tile(m, tm or 1024), _tile(n, tn or 1024)
    else:
        tm, tn = _tile(m, tm or 2048), _tile(n, tn or (1024 if m >= 2048 else 2048))
    if out_blocks is not None:
        tn = n // out_blocks
    assert b_col0 % tn == 0
    jb = b_col0 // tn
    nk = k // tk
    nj = n // tn
    assert not f32_tail or (nk == 1 and epilogue is None)
    ne, no = len(epi), len(out_dtypes)
    dims = (((0 if ta else 1,), (1 if tb else 0,)), ((), ()))

    def body(*refs):
        a_ref, b_ref = refs[0], refs[1]
        epi_refs = refs[2:2 + ne]
        n_in = 2 + ne + (dep is not None)
        out_refs = refs[n_in:n_in + no]
        part = lax.dot_general(a_ref[...].astype(BF16), b_ref[...].astype(BF16), dims, preferred_element_type=F32)

        def finish(acc):
            res = (acc,) if epilogue is None else epilogue(acc, *[r[...] for r in epi_refs])
            for r, v in zip(out_refs, res):
                r[...] = v.astype(r.dtype)

        if nk == 1:
            finish(part)
            if f32_tail:
                @pl.when(pl.program_id(1) == nj - 1)
                def _():
                    refs[n_in + no][...] = part
        else:
            acc_ref = refs[-1]
            kk = pl.program_id(2)

            @pl.when(kk == 0)
            def _():
                acc_ref[...] = part

            @pl.when(kk > 0)
            def _():
                acc_ref[...] += part

            @pl.when(kk == nk - 1)
            def _():
                finish(acc_ref[...])

    a_spec = pl.BlockSpec((tk, tm), lambda i, j, kk: (kk, i)) if ta else pl.BlockSpec((tm, tk), lambda i, j, kk: (i, kk))
    b_spec = pl.BlockSpec((tn, tk), lambda i, j, kk: (j, kk)) if tb else pl.BlockSpec((tk, tn), lambda i, j, kk: (kk, j + jb))
    epi_specs = []
    for (arr, col0) in epi:
        assert col0 % tn == 0
        epi_specs.append(pl.BlockSpec((tm, tn), functools.partial(lambda i, j, kk, c0: (i, j + c0), c0=col0 // tn)))
    res = pl.pallas_call(
        body, name=name, grid=(m // tm, n // tn, nk),
        in_specs=[a_spec, b_spec] + epi_specs + ([] if dep is None else [pl.BlockSpec(dep.shape, lambda i, j, kk: (0, 0))]),
        out_specs=[pl.BlockSpec((tm, tn), lambda i, j, kk: (i, j)) if out_blocks is None else
                   pl.BlockSpec((None, tm, tn), lambda i, j, kk: (j, i, 0)) for _ in out_dtypes]
        + ([pl.BlockSpec((tm, tn), lambda i, j, kk: (i, 0))] if f32_tail else []),
        out_shape=[jax.ShapeDtypeStruct((m, n) if out_blocks is None else (out_blocks, m, tn), dt) for dt in out_dtypes]
        + ([jax.ShapeDtypeStruct((m, tn), F32)] if f32_tail else []),
        scratch_shapes=[pltpu.VMEM((tm, tn), F32)] if nk > 1 else [],
        compiler_params=_params(("parallel", "arbitrary" if f32_tail else "parallel", "arbitrary")),
    )(a, b, *[arr for (arr, _) in epi], *([] if dep is None else [dep]))
    return res[0] if len(res) == 1 else res


def _rms(x, g):
    return x * lax.rsqrt(jnp.mean(x * x, axis=-1, keepdims=True) + NORM_EPS) * g


def _ln(x, g, b):
    mu = jnp.mean(x, axis=-1, keepdims=True)
    xc = x - mu
    var = jnp.mean(xc * xc, axis=-1, keepdims=True)
    return xc * lax.rsqrt(var + NORM_EPS) * g + b


def _gelu(x):
    return 0.5 * x * (1.0 + jnp.tanh(math.sqrt(2.0 / math.pi) * (x + 0.044715 * (x * x * x))))


def _sigmoid(x):
    return 1.0 / (1.0 + jnp.exp(-x))


def _silu(x):
    return x * _sigmoid(x)


def _log_sigmoid(x):
    return jnp.minimum(x, 0.0) - jnp.log(1.0 + jnp.exp(-jnp.abs(x)))


def _f_pre(x, g, sc, sh):
    return (_rms(x, g) * (1.0 + sc) + sh,)


def _f_post(y, g, gt):
    return (gt * _rms(y, g),)


def _f_a1(u_raw, v_raw, g, b):
    return _gelu(u_raw), _ln(_gelu(v_raw), g, b)


def _f_glu(val, gate):
    return (val * _sigmoid(gate),)


def _f_lnsilu(zc, g, b):
    return (_silu(_ln(zc, g, b)),)


def _f_merge(g0, g1, g2, ya, yb, yc):
    return (_sigmoid(g0) * ya + _sigmoid(g1) * yb + _sigmoid(g2) * yc,)


def _lane_lt64(shape):
    return lax.broadcasted_iota(jnp.int32, shape, 1) < HEAD_DIM


def spatial_fwd(vln, u, w_bf, b_exp, rows_per_step=512):
    s = vln.shape[0]
    tr = min(rows_per_step, s)

    def body(v_ref, u_ref, w_ref, b_ref, sv_ref, ya_ref):
        lo = _lane_lt64((CHUNK, LANES))
        for ch in range(tr // CHUNK):
            r0 = ch * CHUNK
            for p in range(D_MIX // LANES):
                vp = v_ref[r0:r0 + CHUNK, p * LANES:(p + 1) * LANES]
                o0 = jnp.dot(w_ref[2 * p], vp, preferred_element_type=F32)
                o1 = jnp.dot(w_ref[2 * p + 1], vp, preferred_element_type=F32)
                sv = jnp.where(lo, o0, o1) + b_ref[:, p * LANES:(p + 1) * LANES]
                sv_ref[r0:r0 + CHUNK, p * LANES:(p + 1) * LANES] = sv
                ya_ref[r0:r0 + CHUNK, p * LANES:(p + 1) * LANES] = (
                    u_ref[r0:r0 + CHUNK, p * LANES:(p + 1) * LANES] * sv).astype(BF16)

    row = pl.BlockSpec((tr, D_MIX), lambda i: (i, 0))
    return pl.pallas_call(
        body, name="spatial_fwd", grid=(s // tr,),
        in_specs=[row, row, pl.BlockSpec(w_bf.shape, lambda i: (0, 0, 0)), pl.BlockSpec(b_exp.shape, lambda i: (0, 0))],
        out_specs=[row, row],
        out_shape=[jax.ShapeDtypeStruct((s, D_MIX), F32), jax.ShapeDtypeStruct((s, D_MIX), BF16)],
        compiler_params=_params(("parallel",)),
    )(vln, u, w_bf, b_exp)


def spatial_bwd(dya, u, sv, vln, wt_bf, rows_per_step=512):
    s = vln.shape[0]
    tr = min(rows_per_step, s)
    ng = wt_bf.shape[0]

    def body(dya_ref, u_ref, sv_ref, v_ref, wt_ref, du_ref, dv_ref, dw_ref, db_ref):
        @pl.when(pl.program_id(0) == 0)
        def _():
            dw_ref[...] = jnp.zeros(dw_ref.shape, F32)
            db_ref[...] = jnp.zeros(db_ref.shape, F32)

        lo = _lane_lt64((CHUNK, LANES))
        for ch in range(tr // CHUNK):
            r0 = ch * CHUNK
            for p in range(D_MIX // LANES):
                cs = slice(p * LANES, (p + 1) * LANES)
                dya_p = dya_ref[r0:r0 + CHUNK, cs].astype(F32)
                du_ref[r0:r0 + CHUNK, cs] = dya_p * sv_ref[r0:r0 + CHUNK, cs]
                dsv = dya_p * u_ref[r0:r0 + CHUNK, cs]
                db_ref[:, cs] += dsv
                dsv0 = jnp.where(lo, dsv, 0.0).astype(BF16)
                dsv1 = jnp.where(lo, 0.0, dsv).astype(BF16)
                vp = v_ref[r0:r0 + CHUNK, cs]
                d0 = jnp.dot(wt_ref[2 * p], dsv0, preferred_element_type=F32)
                d1 = jnp.dot(wt_ref[2 * p + 1], dsv1, preferred_element_type=F32)
                dv_ref[r0:r0 + CHUNK, cs] = d0 + d1
                nt = (((1,), (1,)), ((), ()))
                dw_ref[2 * p] += lax.dot_general(dsv0, vp, nt, preferred_element_type=F32)
                dw_ref[2 * p + 1] += lax.dot_general(dsv1, vp, nt, preferred_element_type=F32)

    row = pl.BlockSpec((tr, D_MIX), lambda i: (i, 0))
    return pl.pallas_call(
        body, name="spatial_bwd", grid=(s // tr,),
        in_specs=[row, row, row, row, pl.BlockSpec(wt_bf.shape, lambda i: (0, 0, 0))],
        out_specs=[row, row, pl.BlockSpec((ng, CHUNK, CHUNK), lambda i: (0, 0, 0)), pl.BlockSpec((CHUNK, D_MIX), lambda i: (0, 0))],
        out_shape=[jax.ShapeDtypeStruct((s, D_MIX), F32), jax.ShapeDtypeStruct((s, D_MIX), F32),
                   jax.ShapeDtypeStruct((ng, CHUNK, CHUNK), F32), jax.ShapeDtypeStruct((CHUNK, D_MIX), F32)],
        compiler_params=_params(("arbitrary",)),
    )(dya, u, sv, vln, wt_bf)


def _windows(ref, first, count, ts):
    for r in range(8):
        ks = [k for k in range(count) if (first + k) % 8 == r]
        if ks:
            base = first + ks[0]
            blk = ref[base:base + ks[-1] - ks[0] + ts, :]
            for k in ks:
                yield k, blk[k - ks[0]:k - ks[0] + ts, :]


def conv_fwd(proj, cb_val, cb_gate, w_pad, cb, ln_g, ln_b, ts=256):
    s = proj.shape[0]
    ts = min(ts, s)
    per = ts // CONV_HALO

    def body(val_ref, gate_ref, pval_ref, pgate_ref, w_ref, cb_ref, g_ref, b_ref, zc_ref, yb_ref, ext_ref):
        i = pl.program_id(0)
        zprev = pval_ref[...].astype(F32) * _sigmoid(pgate_ref[...].astype(F32))
        ext_ref[0:CONV_HALO, :] = jnp.where(i > 0, zprev, 0.0)
        ext_ref[CONV_HALO:, :] = val_ref[...].astype(F32) * _sigmoid(gate_ref[...].astype(F32))
        acc = jnp.zeros((ts, D_MIX), F32)
        for j, win in _windows(ext_ref, CONV_HALO - (CONV_WIDTH - 1), CONV_WIDTH, ts):
            acc = acc + w_ref[j:j + 1, :] * win
        zc = acc + cb_ref[...]
        zc_ref[...] = zc
        yb_ref[...] = _f_lnsilu(zc, g_ref[...], b_ref[...])[0].astype(BF16)

    def cur(c):
        return pl.BlockSpec((ts, D_MIX), functools.partial(lambda i, c: (i, c), c=c))

    def prev(c):
        return pl.BlockSpec((CONV_HALO, D_MIX), functools.partial(lambda i, c: (jnp.maximum(i * per - 1, 0), c), c=c))

    const = lambda a: pl.BlockSpec(a.shape, lambda i: (0, 0))
    out = pl.BlockSpec((ts, D_MIX), lambda i: (i, 0))
    return pl.pallas_call(
        body, name="conv_fwd", grid=(s // ts,),
        in_specs=[cur(cb_val), cur(cb_gate), prev(cb_val), prev(cb_gate), const(w_pad), const(cb), const(ln_g), const(ln_b)],
        out_specs=[out, out],
        out_shape=[jax.ShapeDtypeStruct((s, D_MIX), F32), jax.ShapeDtypeStruct((s, D_MIX), BF16)],
        scratch_shapes=[pltpu.VMEM((CONV_HALO + ts, D_MIX), F32)],
        compiler_params=_params(("parallel",)),
    )(proj, proj, proj, proj, w_pad, cb, ln_g, ln_b)


def conv_bwd(proj, cb_val, cb_gate, zc, dyb, w_pad, ln_g, ln_b, dproj, ts=256):
    assert cb_gate == cb_val + 1 and cb_val % 2 == 0
    s = proj.shape[0]
    ts = min(ts, s)
    per = ts // CONV_HALO
    n_tiles = s // ts
    n_halo = s // CONV_HALO

    def body(val_ref, gate_ref, pval_ref, pgate_ref, zc_ref, dyb_ref, nzc_ref, ndyb_ref, w_ref, g_ref, b_ref, dproj_in,
             dvg_ref, dw_ref, dcb_ref, dg_ref, db_ref, zext_ref, dext_ref):
        i = pl.program_id(0)

        @pl.when(i == 0)
        def _():
            for r in (dw_ref, dcb_ref, dg_ref, db_ref):
                r[...] = jnp.zeros(r.shape, F32)

        g, b = g_ref[...], b_ref[...]
        _, vjp = jax.vjp(lambda z, gg, bb: _f_lnsilu(z, gg, bb)[0], zc_ref[...], g, b)
        dzc, dg, db = vjp(dyb_ref[...].astype(F32))
        dg_ref[...] += dg
        db_ref[...] += db
        dcb_ref[...] += jnp.sum(dzc, axis=0, keepdims=True)
        _, vjp_n = jax.vjp(lambda z: _f_lnsilu(z, g, b)[0], nzc_ref[...])
        (dzc_next,) = vjp_n(ndyb_ref[...].astype(F32))
        dext_ref[0:ts, :] = dzc
        dext_ref[ts:, :] = jnp.where(i < n_tiles - 1, dzc_next, 0.0)
        val, gate = val_ref[...].astype(F32), gate_ref[...].astype(F32)
        zprev = pval_ref[...].astype(F32) * _sigmoid(pgate_ref[...].astype(F32))
        zext_ref[0:CONV_HALO, :] = jnp.where(i > 0, zprev, 0.0)
        zext_ref[CONV_HALO:, :] = val * _sigmoid(gate)
        dz = jnp.zeros((ts, D_MIX), F32)
        for shift, win in _windows(dext_ref, 0, CONV_WIDTH, ts):
            j = CONV_WIDTH - 1 - shift
            dz = dz + w_ref[j:j + 1, :] * win
        for j, win in _windows(zext_ref, CONV_HALO - (CONV_WIDTH - 1), CONV_WIDTH, ts):
            dw_ref[j:j + 1, :] += jnp.sum(dzc * win, axis=0, keepdims=True)
        _, vjp_glu = jax.vjp(lambda a, c: _f_glu(a, c)[0], val, gate)
        dval, dgate = vjp_glu(dz)
        dvg_ref[:, :D_MIX] = dval.astype(BF16)
        dvg_ref[:, D_MIX:] = dgate.astype(BF16)

    def cur(c):
        return pl.BlockSpec((ts, D_MIX), functools.partial(lambda i, c: (i, c), c=c))

    def prev(c):
        return pl.BlockSpec((CONV_HALO, D_MIX), functools.partial(lambda i, c: (jnp.maximum(i * per - 1, 0), c), c=c))

    nxt = pl.BlockSpec((CONV_HALO, D_MIX), lambda i: (jnp.minimum((i + 1) * per, n_halo - 1), 0))
    const = lambda a: pl.BlockSpec(a.shape, lambda i: (0, 0))
    out = pl.BlockSpec((ts, D_MIX), lambda i: (i, 0))
    vec = pl.BlockSpec((1, D_MIX), lambda i: (0, 0))
    return pl.pallas_call(
        body, name="conv_bwd", grid=(n_tiles,),
        in_specs=[cur(cb_val), cur(cb_gate), prev(cb_val), prev(cb_gate), out, out, nxt, nxt, const(w_pad), const(ln_g), const(ln_b),
                  pl.BlockSpec(memory_space=pl.ANY)],
        out_specs=[pl.BlockSpec((ts, 2 * D_MIX), lambda i: (i, cb_val // 2)), pl.BlockSpec((CONV_HALO, D_MIX), lambda i: (0, 0)), vec, vec, vec],
        out_shape=[jax.ShapeDtypeStruct(dproj.shape, dproj.dtype),
                   jax.ShapeDtypeStruct((CONV_HALO, D_MIX), F32)] + [jax.ShapeDtypeStruct((1, D_MIX), F32)] * 3,
        scratch_shapes=[pltpu.VMEM((CONV_HALO + ts, D_MIX), F32), pltpu.VMEM((ts + CONV_HALO, D_MIX), F32)],
        input_output_aliases={11: 0}, compiler_params=_params(("arbitrary",)),
    )(proj, proj, proj, proj, zc, dyb, zc, dyb, w_pad, ln_g, ln_b, dproj)


def forget_cumsum(proj, cb_f, bf_exp, t=256):
    s = proj.shape[0]
    t = min(t, s)

    def body(f_ref, bf_ref, out_ref, carry_ref):
        @pl.when(pl.program_id(0) == 0)
        def _():
            carry_ref[...] = jnp.zeros(carry_ref.shape, F32)

        lf = _log_sigmoid(f_ref[...] + bf_ref[...])
        tri = (lax.broadcasted_iota(jnp.int32, (t, t), 1) <= lax.broadcasted_iota(jnp.int32, (t, t), 0)).astype(F32)
        c = jnp.dot(tri, lf, precision=HIGHEST, preferred_element_type=F32) + carry_ref[...]
        out_ref[...] = c
        carry_ref[...] = c[t - 1:t, :]

    return pl.pallas_call(
        body, name="forget_cumsum", grid=(s // t,),
        in_specs=[pl.BlockSpec((t, D_MIX), functools.partial(lambda i, c: (i, c), c=cb_f)), pl.BlockSpec((1, D_MIX), lambda i: (0, 0))],
        out_specs=pl.BlockSpec((t, D_MIX), lambda i: (i, 0)),
        out_shape=jax.ShapeDtypeStruct((s, D_MIX), F32),
        scratch_shapes=[pltpu.VMEM((1, D_MIX), F32)],
        compiler_params=_params(("arbitrary",)),
    )(proj, bf_exp)


def forget_bwd(proj, cb_f, bf_exp, dcum, dproj, cb_out, t=256):
    s = proj.shape[0]
    t = min(t, s)
    n = s // t

    def body(f_ref, bf_ref, dc_ref, dproj_in, df_ref, dbf_ref, carry_ref):
        @pl.when(pl.program_id(0) == 0)
        def _():
            carry_ref[...] = jnp.zeros(carry_ref.shape, F32)
            dbf_ref[...] = jnp.zeros(dbf_ref.shape, F32)

        tri = (lax.broadcasted_iota(jnp.int32, (t, t), 1) >= lax.broadcasted_iota(jnp.int32, (t, t), 0)).astype(F32)
        r = jnp.dot(tri, dc_ref[...], precision=HIGHEST, preferred_element_type=F32) + carry_ref[...]
        carry_ref[...] = r[0:1, :]
        df = r * _sigmoid(-(f_ref[...] + bf_ref[...]))
        dbf_ref[...] += jnp.sum(df, axis=0, keepdims=True)
        live = lax.broadcasted_iota(jnp.int32, (t, D_MIX), 1) % HEAD_DIM == 0
        df_ref[...] = jnp.where(live, df, 0.0).astype(BF16)

    return pl.pallas_call(
        body, name="forget_bwd", grid=(n,),
        in_specs=[pl.BlockSpec((t, D_MIX), functools.partial(lambda i, c: (n - 1 - i, c), c=cb_f)), pl.BlockSpec((1, D_MIX), lambda i: (0, 0)),
                  pl.BlockSpec((t, D_MIX), lambda i: (n - 1 - i, 0)), pl.BlockSpec(memory_space=pl.ANY)],
        out_specs=[pl.BlockSpec((t, D_MIX), lambda i: (n - 1 - i, cb_out)), pl.BlockSpec((1, D_MIX), lambda i: (0, 0))],
        out_shape=[jax.ShapeDtypeStruct(dproj.shape, dproj.dtype), jax.ShapeDtypeStruct((1, D_MIX), F32)],
        scratch_shapes=[pltpu.VMEM((1, D_MIX), F32)],
        input_output_aliases={3: 0}, compiler_params=_params(("arbitrary",)),
    )(proj, bf_exp, dcum, dproj)


NT = (((1,), (1,)), ((), ()))
LOG2E = math.log2(math.e)
N_PAIR = D_MIX // LANES


def _split3(x):
    hi = x.astype(BF16).astype(F32)
    mid = (x - hi).astype(BF16).astype(F32)
    return hi, mid, x - hi - mid


def _triple(li, first, vals):
    out = jnp.where(li == first, vals[0], 0.0)
    for i in (1, 2):
        out = jnp.where(li == first + i, vals[i], out)
    return out


def _lane_ids(shape):
    lane = lax.broadcasted_iota(jnp.int32, shape, 1)
    return lane, lane % HEAD_DIM, lane < HEAD_DIM


def attn_prep(proj, cb_q, cum, ts=512):
    s = proj.shape[0]
    ts = min(ts, s)
    scale = LOG2E / math.sqrt(HEAD_DIM)

    def body(q_ref, k_ref, v_ref, c_ref, qe_ref, qo_ref, ke_ref, ko_ref, ve_ref, vo_ref):
        _, li, lo = _lane_ids((ts, LANES))
        one3 = lambda first: ((li >= first) & (li < first + 3)).astype(F32)
        for p in range(N_PAIR):
            ps = slice(p * LANES, (p + 1) * LANES)
            c3 = _split3(pltpu.roll(c_ref[:, ps] * LOG2E, HEAD_DIM, axis=1))
            eq = _triple(li, 0, c3) + one3(3)
            ek = one3(0) - _triple(li, 3, c3) + one3(6)
            ev = one3(0)
            for src, even, odd, extra, mul in ((q_ref, qe_ref, qo_ref, eq, scale), (k_ref, ke_ref, ko_ref, ek, 1.0), (v_ref, ve_ref, vo_ref, ev, 1.0)):
                x = src[:, ps].astype(F32) * mul
                even[:, ps] = jnp.where(lo, x, extra).astype(BF16)
                odd[:, ps] = jnp.where(lo, extra, x).astype(BF16)

    col = lambda c: pl.BlockSpec((ts, D_MIX), functools.partial(lambda i, c: (i, c), c=c))
    out = pl.BlockSpec((ts, D_MIX), lambda i: (i, 0))
    return pl.pallas_call(
        body, name="attn_prep", grid=(s // ts,),
        in_specs=[col(cb_q), col(cb_q + 1), col(cb_q + 2), pl.BlockSpec((ts, D_MIX), lambda i: (i, 0))],
        out_specs=[out] * 6, out_shape=[jax.ShapeDtypeStruct((s, D_MIX), BF16)] * 6,
        compiler_params=_params(("parallel",)),
    )(proj, proj, proj, cum)


def _pair_specs(s, t):
    return pl.BlockSpec((t, LANES), lambda p, i: (i, p)), pl.BlockSpec((s, LANES), lambda p, i: (0, p))


def attn_fwd(qe, qo, ke, ko, ve, vo, tq=512):
    s = qe.shape[0]
    tq = min(tq, s)

    def body(qe_ref, qo_ref, ke_ref, ko_ref, ve_ref, vo_ref, o_ref, qbe_ref, qbo_ref):
        qi = pl.program_id(1)
        qs, k_refs, v_refs = (qe_ref[...], qo_ref[...]), (ke_ref, ko_ref), (ve_ref, vo_ref)
        causal = lax.broadcasted_iota(jnp.int32, (tq, tq), 1) <= lax.broadcasted_iota(jnp.int32, (tq, tq), 0)

        def step(j, carry, diag):
            ks = pl.multiple_of(j * tq, tq)
            new = []
            for h in range(2):
                m, l, acc = carry[h]
                sc = lax.dot_general(qs[h], k_refs[h][pl.ds(ks, tq), :], NT, preferred_element_type=F32)
                if diag:
                    sc = jnp.where(causal, sc, -jnp.inf)
                m_new = jnp.maximum(m, jnp.max(sc, axis=1, keepdims=True))
                p = jnp.exp2(sc - m_new)
                alpha = jnp.exp2(m - m_new)
                l = alpha * l + jnp.sum(p, axis=1, keepdims=True)
                acc = alpha * acc + jnp.dot(p.astype(BF16), v_refs[h][pl.ds(ks, tq), :], preferred_element_type=F32)
                new.append((m_new, l, acc))
            return tuple(new)

        init = tuple((jnp.full((tq, 1), -jnp.inf, F32), jnp.zeros((tq, 1), F32), jnp.zeros((tq, LANES), F32)) for _ in range(2))
        carry = lax.fori_loop(0, qi, lambda j, c: step(j, c, False), init)
        (m0, l0, a0), (m1, l1, a1) = step(qi, carry, True)
        _, li, lo = _lane_ids((tq, LANES))
        o_ref[...] = jnp.where(lo, a0 / l0, a1 / l1).astype(o_ref.dtype)
        lse_lanes = (li >= 6) & (li < 9)
        for q, m, l, spare, out_ref in ((qs[0], m0, l0, ~lo, qbe_ref), (qs[1], m1, l1, lo, qbo_ref)):
            neg_lse = _triple(li, 6, _split3(-(m + jnp.log(l) * LOG2E)))
            out_ref[...] = jnp.where(spare & lse_lanes, neg_lse.astype(BF16), q)

    blk, full = _pair_specs(s, tq)
    return pl.pallas_call(
        body, name="attn_fwd", grid=(N_PAIR, s // tq),
        in_specs=[blk, blk, full, full, full, full],
        out_specs=[blk] * 3, out_shape=[jax.ShapeDtypeStruct((s, D_MIX), BF16)] * 3,
        compiler_params=_params(("parallel", "parallel")),
    )(qe, qo, ke, ko, ve, vo)


def attn_dsum(qbe, qbo, ke, ko, ve, vo, do, dep, tq=512):
    s = qbe.shape[0]
    tq = min(tq, s)

    def body(qe_ref, qo_ref, ke_ref, ko_ref, ve_ref, vo_ref, do_ref, dep_ref, dobe_ref, dobo_ref):
        qi = pl.program_id(1)
        _, li, lo = _lane_ids((tq, LANES))
        do_ = do_ref[...]
        qs, k_refs, v_refs = (qe_ref[...], qo_ref[...]), (ke_ref, ko_ref), (ve_ref, vo_ref)
        dos = (jnp.where(lo, do_, 0), jnp.where(lo, 0, do_))
        causal = lax.broadcasted_iota(jnp.int32, (tq, tq), 1) <= lax.broadcasted_iota(jnp.int32, (tq, tq), 0)

        def step(j, carry, diag):
            ks = pl.multiple_of(j * tq, tq)
            new = []
            for h in range(2):
                sc = lax.dot_general(qs[h], k_refs[h][pl.ds(ks, tq), :], NT, preferred_element_type=F32)
                if diag:
                    sc = jnp.where(causal, sc, -jnp.inf)
                pdp = jnp.exp2(sc) * lax.dot_general(dos[h], v_refs[h][pl.ds(ks, tq), :], NT, preferred_element_type=F32)
                new.append(carry[h] + jnp.sum(pdp, axis=1, keepdims=True))
            return tuple(new)

        init = (jnp.zeros((tq, 1), F32), jnp.zeros((tq, 1), F32))
        carry = lax.fori_loop(0, qi, lambda j, c: step(j, c, False), init)
        s0, s1 = step(qi, carry, True)
        dobe_ref[...] = jnp.where(lo, do_, _triple(li, 0, _split3(-s0)).astype(BF16))
        dobo_ref[...] = jnp.where(lo, _triple(li, 0, _split3(-s1)).astype(BF16), do_)

    blk, full = _pair_specs(s, tq)
    return pl.pallas_call(
        body, name="attn_dsum", grid=(N_PAIR, s // tq),
        in_specs=[blk, blk, full, full, full, full, blk, pl.BlockSpec(dep.shape, lambda p, i: (0, 0))],
        out_specs=[blk, blk], out_shape=[jax.ShapeDtypeStruct((s, D_MIX), BF16)] * 2,
        compiler_params=_params(("parallel", "parallel")),
    )(qbe, qbo, ke, ko, ve, vo, do, dep)


def attn_dkv(ke, ko, ve, vo, qbe, qbo, dobe, dobo, dproj, lane_block, tk=512):
    s = ke.shape[0]
    tk = min(tk, s)
    nq = s // tk
    tn = (((0,), (0,)), ((), ()))

    def body(ke_ref, ko_ref, ve_ref, vo_ref, qe_ref, qo_ref, de_ref, do_ref, dproj_in, dk_ref, dv_ref, dck_ref, dq_ref):
        kj = pl.program_id(1)
        lo = _lane_lt64((tk, LANES))
        ks_, vs_, q_refs, d_refs = (ke_ref[...], ko_ref[...]), (ve_ref[...], vo_ref[...]), (qe_ref, qo_ref), (de_ref, do_ref)
        causal = lax.broadcasted_iota(jnp.int32, (tk, tk), 0) <= lax.broadcasted_iota(jnp.int32, (tk, tk), 1)

        @pl.when(kj == 0)
        def _():
            dq_ref[...] = jnp.zeros(dq_ref.shape, F32)

        def step(i, carry, diag):
            qs = pl.multiple_of(i * tk, tk)
            new, dq_h = [], []
            for h in range(2):
                dk, dv, dck = carry[h]
                qblk = q_refs[h][pl.ds(qs, tk), :]
                dblk = d_refs[h][pl.ds(qs, tk), :]
                st = lax.dot_general(ks_[h], qblk, NT, preferred_element_type=F32)
                if diag:
                    st = jnp.where(causal, st, -jnp.inf)
                pt = jnp.exp2(st)
                dst = pt * lax.dot_general(vs_[h], dblk, NT, preferred_element_type=F32)
                dst_bf = dst.astype(BF16)
                dq_h.append(lax.dot_general(dst_bf, ks_[h], tn, preferred_element_type=F32))
                new.append((dk + jnp.dot(dst_bf, qblk, preferred_element_type=F32),
                            dv + jnp.dot(pt.astype(BF16), dblk, preferred_element_type=F32),
                            dck - jnp.sum(dst, axis=1, keepdims=True)))
            dq_ref[pl.ds(qs, tk), :] += jnp.where(lo, dq_h[0], dq_h[1])
            return tuple(new)

        init = tuple((jnp.zeros((tk, LANES), F32), jnp.zeros((tk, LANES), F32), jnp.zeros((tk, 1), F32)) for _ in range(2))
        carry = step(kj, init, True)
        (dk0, dv0, dc0), (dk1, dv1, dc1) = lax.fori_loop(kj + 1, nq, lambda i, c: step(i, c, False), carry)
        dk_ref[...] = (jnp.where(lo, dk0, dk1) * (1.0 / LOG2E)).astype(dk_ref.dtype)
        dv_ref[...] = jnp.where(lo, dv0, dv1).astype(dv_ref.dtype)
        dck_ref[...] = jnp.where(lo, dc0, dc1)

    blk, full = _pair_specs(s, tk)
    return pl.pallas_call(
        body, name="attn_dkv", grid=(N_PAIR, nq),
        in_specs=[blk, blk, blk, blk, full, full, full, full, pl.BlockSpec(memory_space=pl.ANY)],
        out_specs=[pl.BlockSpec((tk, LANES), lambda p, i: (i, lane_block + p)), blk, blk, full],
        out_shape=[jax.ShapeDtypeStruct(dproj.shape, dproj.dtype), jax.ShapeDtypeStruct((s, D_MIX), BF16), jax.ShapeDtypeStruct((s, D_MIX), F32),
                   jax.ShapeDtypeStruct((s, D_MIX), F32)],
        input_output_aliases={8: 0}, compiler_params=_params(("parallel", "arbitrary")),
    )(ke, ko, ve, vo, qbe, qbo, dobe, dobo, dproj)


def _pre_bwd(name, x, g, sc, sh, dh, dres):
    d = x.shape[1]

    def fn(xv, dhv, dresv, gv, scv, shv):
        _, vjp = jax.vjp(lambda *a: _f_pre(*a)[0], xv, gv, scv, shv)
        dx, dg, dsc, dsh = vjp(dhv.astype(F32))
        return dx + dresv, dg, dsc, dsh

    return rowwise(name, fn, [(x, 0, d), (dh, 0, d), (dres, 0, d)], [g, sc, sh], [(d, F32)], accs=[(1, d)] * 3)


def layer_fwd(x, mod, layer):
    s, d = x.shape
    m = D_MIX
    w = dict(layer["small"])
    sh1, sc1, gt1, sh2, sc2, gt2 = (mod[i:i + 1] for i in range(6))
    cb = 3 * d // m
    (h,) = rowwise("pre1", _f_pre, [(x, 0, d)], [w["mix_pre_g"], sc1, sh1], [(d, BF16)])
    w.update(layer["get_in"](h))
    proj, fproj = matmul("w_in", h, w["w_in"], out_dtypes=(BF16,), f32_tail=True, tn=2 * m)
    fcb = 1
    w.update(layer["get_abco"](proj))
    u, vln = rowwise("gmlp_in", _f_a1, [(proj, cb, m), (proj, cb + 1, m)], [w["gmlp_ln_g"], w["gmlp_ln_b"]], [(m, F32), (m, BF16)])
    sv, ya = spatial_fwd(vln, u, w["ws"], w["bs_exp"])
    y_a = matmul("w_a", ya, w["w_a_out"], out_dtypes=(BF16,))
    zc, yb = conv_fwd(proj, cb + 2, cb + 3, w["conv_w"], w["conv_b"], w["conv_ln_g"], w["conv_ln_b"])
    y_b = matmul("w_b", yb, w["w_b_out"], out_dtypes=(BF16,))
    cum = forget_cumsum(fproj, fcb, w["bf_exp"])
    kv_ops = attn_prep(proj, cb + 4, cum)
    o, qbe, qbo = attn_fwd(*kv_ops)
    att = (qbe, qbo) + tuple(kv_ops[2:])
    y_c = matmul("w_c", o, w["w_c_out"], out_dtypes=(BF16,))
    (merged,) = rowwise("merge", _f_merge, [(proj, 0, d), (proj, 1, d), (proj, 2, d), (y_a, 0, d), (y_b, 0, d), (y_c, 0, d)], [], [(d, BF16)])
    y = matmul("w_out", merged, w["w_out"], out_dtypes=(BF16,))
    w.update(layer["get_mlp"](y))

    def post_pre(xv, yv, gp, gt, g2, sc, sh):
        x1 = xv + _f_post(yv, gp, gt)[0]
        return x1, _f_pre(x1, g2, sc, sh)[0]

    x1, h2 = rowwise("post1", post_pre, [(x, 0, d), (y, 0, d)], [w["mix_post_g"], gt1, w["mlp_pre_g"], sc2, sh2], [(d, F32), (d, BF16)])
    r = matmul("w1", h2, w["mlp_w1"], out_dtypes=(BF16,), epilogue=lambda acc: (jnp.square(jnp.maximum(acc, 0.0)),))
    y2 = matmul("w2", r, w["mlp_w2"], out_dtypes=(BF16,))
    (x2,) = rowwise("post2", lambda xv, yv, g, gt: xv + _f_post(yv, g, gt)[0], [(x1, 0, d), (y2, 0, d)], [w["mlp_post_g"], gt2], [(d, F32)])
    saved = dict(w=w, x=x, h=h, proj=proj, fproj=fproj, fcb=fcb, u=u, vln=vln, sv=sv, ya=ya, y_a=y_a, zc=zc, yb=yb, y_b=y_b, att=att,
                 o=o, y_c=y_c, merged=merged, y=y, x1=x1, h2=h2, r=r, y2=y2)
    return x2, saved


def layer_bwd(dx2, mod, sv, emit, tok_in=None):
    x, proj, w = sv["x"], sv["proj"], sv["w"]
    s, d = x.shape
    m = D_MIX
    sh1, sc1, gt1, sh2, sc2, gt2 = (mod[i:i + 1] for i in range(6))
    cb = 3 * d // m
    g = {}
    if tok_in is not None:
        gt2 = gt2 + tok_in[0:1, 0:1]
    dy2, g["mlp_post_g"], dgt2 = rowwise_vjp("post2_b", _f_post, [(sv["y2"], 0, d)], [w["mlp_post_g"], gt2], [(dx2, 0, d)], [BF16])
    da = matmul("w2_dx", dy2, w["mlp_w2"], tb=True, out_dtypes=(BF16,),
                epilogue=lambda acc, r: (acc * (2.0 * jnp.sqrt(r.astype(F32))),), epi=[(sv["r"], 0)])
    big = {}
    big["mlp_w2"] = matmul("w2_dw", sv["r"], dy2, ta=True, out_dtypes=(BF16,))
    dh2 = matmul("w1_dx", da, w["mlp_w1"], tb=True, out_dtypes=(BF16,))
    big["mlp_w1"] = matmul("w1_dw", sv["h2"], da, ta=True, out_dtypes=(BF16,), out_blocks=N_DEV)
    tok = emit("mlp", big)
    dx1, g["mlp_pre_g"], dsc2, dsh2 = _pre_bwd("pre2_b", sv["x1"], w["mlp_pre_g"], sc2 + tok[0:1, 0:1], sh2, dh2, dx2)
    dy, g["mix_post_g"], dgt1 = rowwise_vjp("post1_b", _f_post, [(sv["y"], 0, d)], [w["mix_post_g"], gt1], [(dx1, 0, d)], [BF16])
    dmerged = matmul("w_out_dx", dy, w["w_out"], tb=True, out_dtypes=(BF16,))
    big = {}
    big["w_out"] = matmul("w_out_dw", sv["merged"], dy, ta=True, out_dtypes=(BF16,))
    assert (3 * d) % (2 * m) == 0
    dproj = lax.empty((s, 3 * d + 8 * m), BF16)
    dproj, dya_, dyb_, dyc_ = rowwise_vjp(
        "merge_b", _f_merge, [(proj, 0, d), (proj, 1, d), (proj, 2, d), (sv["y_a"], 0, d), (sv["y_b"], 0, d), (sv["y_c"], 0, d)], [],
        [(dmerged, 0, d)], [BF16] * 6, into=(dproj, 0, 3))
    dya_pre = matmul("w_a_dx", dya_, w["w_a_out"], tb=True, out_dtypes=(BF16,))
    big["w_a_out"] = matmul("w_a_dw", sv["ya"], dya_, ta=True, out_dtypes=(BF16,))
    dyb_pre = matmul("w_b_dx", dyb_, w["w_b_out"], tb=True, out_dtypes=(BF16,))
    big["w_b_out"] = matmul("w_b_dw", sv["yb"], dyb_, ta=True, out_dtypes=(BF16,))
    do = matmul("w_c_dx", dyc_, w["w_c_out"], tb=True, out_dtypes=(BF16,))
    big["w_c_out"] = matmul("w_c_dw", sv["o"], dyc_, ta=True, out_dtypes=(BF16,))
    tok = emit("abco", big)
    qbe, qbo, ke, ko, ve, vo = sv["att"]
    lane0 = (cb + 4) * (m // LANES)
    dobe, dobo = attn_dsum(qbe, qbo, ke, ko, ve, vo, do, tok)
    dproj, dv, dcum, dq = attn_dkv(ke, ko, ve, vo, qbe, qbo, dobe, dobo, dproj, lane0 + N_PAIR)
    (dproj,) = rowwise("dq_scale", lambda g: g * (1.0 / math.sqrt(HEAD_DIM)), [(dq, 0, m)], [], [(m, BF16)], into=(dproj, cb + 4))
    dproj = lax.dynamic_update_slice(dproj, dv, (0, (cb + 6) * m))
    dproj, dbf = forget_bwd(sv["fproj"], sv["fcb"], w["bf_exp"], dcum, dproj, cb + 7)
    g["fox_bf"] = dbf[0, ::HEAD_DIM]
    dproj, dwc, g["conv_b"], g["conv_ln_g"], g["conv_ln_b"] = conv_bwd(
        proj, cb + 2, cb + 3, sv["zc"], dyb_pre, w["conv_w"], w["conv_ln_g"], w["conv_ln_b"], dproj)
    g["conv_w"] = dwc[:CONV_WIDTH]
    du, dvln, dws, dbexp = spatial_bwd(dya_pre, sv["u"], sv["sv"], sv["vln"], w["ws_t"])
    g["gmlp_ws"] = dws * jnp.tril(jnp.ones((CHUNK, CHUNK), F32))
    g["gmlp_bs"] = dbexp.reshape(CHUNK, m // GROUP_DIM, GROUP_DIM).sum(-1).T
    dproj, g["gmlp_ln_g"], g["gmlp_ln_b"] = rowwise_vjp(
        "gmlp_in_b", _f_a1, [(proj, cb, m), (proj, cb + 1, m)], [w["gmlp_ln_g"], w["gmlp_ln_b"]], [(du, 0, m), (dvln, 0, m)], [BF16, BF16],
        into=(dproj, cb // 2, 2))
    tok = emit("in", {"w_in": matmul("w_in_dw", sv["h"], dproj, ta=True, out_dtypes=(BF16,))})
    dh = matmul("w_in_dx", dproj, w["w_in"], tb=True, dep=tok, out_dtypes=(BF16,))
    dx, g["mix_pre_g"], dsc1, dsh1 = _pre_bwd("pre1_b", x, w["mix_pre_g"], sc1, sh1, dh, dx1)
    dmod = jnp.concatenate([dsh1, dsc1, dgt1, dsh2, dsc2, dgt2], axis=0)
    return dx, dmod, g


def local_step(x, target, mods, layers):
    d = x.shape[1]
    saved = []
    for l in range(len(layers)):
        x, sv = layer_fwd(x, mods[l], layers[l])
        saved.append(sv)

    def loss_fn(xv, tv):
        err = xv - tv
        return err * (1.0 / d), jnp.sum(err * err, axis=0, keepdims=True)

    dx, sq = rowwise("loss", loss_fn, [(x, 0, d), (target, 0, d)], [], [(d, F32)], accs=[(1, d)])
    loss = (0.5 / d) * jnp.sum(sq)
    dmods, grads = [None] * len(layers), [None] * len(layers)
    tok = None
    for l in reversed(range(len(layers))):
        dx, dmods[l], grads[l] = layer_bwd(dx, mods[l], saved[l], layers[l]["emit"], tok)
        tok = layers[l]["emit_small"](dmods[l], grads[l])
    return loss, dx, dmods, grads


def exchange(name, arrs, scatter):
    n = len(arrs)

    def body(*refs):
        in_refs, out_refs = refs[:n], refs[n:2 * n]
        send_sems, recv_sems, local_sems = refs[2 * n:]
        x, y, c = lax.axis_index("x"), lax.axis_index("y"), lax.axis_index("c")
        me = 4 * x + 2 * y + c
        local = []
        for a in range(n):
            src = in_refs[a].at[me] if scatter else in_refs[a]
            cp = pltpu.make_async_copy(src, out_refs[a].at[me], local_sems.at[a])
            cp.start()
            local.append(cp)
        remote = []
        for k in range(1, N_DEV):
            px, py, pc = x ^ ((k >> 2) & 1), y ^ ((k >> 1) & 1), c ^ (k & 1)
            peer = 4 * px + 2 * py + pc
            for a in range(n):
                src = in_refs[a].at[peer] if scatter else in_refs[a]
                cp = pltpu.make_async_remote_copy(
                    src_ref=src, dst_ref=out_refs[a].at[me], send_sem=send_sems.at[a * (N_DEV - 1) + k - 1],
                    recv_sem=recv_sems.at[a * (N_DEV - 1) + k - 1], device_id=(px, py, pc), device_id_type=MESH)
                cp.start()
                remote.append(cp)
        for cp in remote:
            cp.wait()
        for cp in local:
            cp.wait()

    hbm = pl.BlockSpec(memory_space=pltpu.HBM)
    out_shape = [jax.ShapeDtypeStruct(a.shape if scatter else (N_DEV,) + a.shape, a.dtype) for a in arrs]
    return pl.pallas_call(
        body, name=name, in_specs=[hbm] * n, out_specs=[hbm] * n, out_shape=out_shape,
        scratch_shapes=[pltpu.SemaphoreType.DMA((n * (N_DEV - 1),)), pltpu.SemaphoreType.DMA((n * (N_DEV - 1),)),
                        pltpu.SemaphoreType.DMA((n,))],
    )(*arrs)


def _peers(x, y, c):
    out = []
    for k in range(1, N_DEV):
        px, py, pc = x ^ ((k >> 2) & 1), y ^ ((k >> 1) & 1), c ^ (k & 1)
        out.append((k - 1, (px, py, pc), 4 * px + 2 * py + pc))
    return out


def _exchange_copies(srcs, lands, send_sems, recv_sems, scatter):
    x, y, c = lax.axis_index("x"), lax.axis_index("y"), lax.axis_index("c")
    me = 4 * x + 2 * y + c
    copies = []
    for slot, pos, peer in _peers(x, y, c):
        for a, (src, land) in enumerate(zip(srcs, lands)):
            copies.append(pltpu.make_async_remote_copy(
                src_ref=src.at[peer] if scatter else src, dst_ref=land.at[me],
                send_sem=send_sems.at[a * (N_DEV - 1) + slot], recv_sem=recv_sems.at[a * (N_DEV - 1) + slot],
                device_id=pos, device_id_type=MESH))
    return me, copies


def exchange_start(name, arrs, scatter, after=None):
    n = len(arrs)
    lands = [lax.empty(a.shape if scatter else (N_DEV,) + a.shape, a.dtype) for a in arrs]
    n_in = 2 * n + (after is not None)

    def body(*refs):
        srcs, lands_ = refs[:n], refs[n:2 * n]
        send_sems, recv_sems = refs[n_in], refs[n_in + 1]
        token = refs[n_in + 2 + 2 * n]
        _, copies = _exchange_copies(srcs, lands_, send_sems, recv_sems, scatter)
        for cp in copies:
            cp.start()
        token[...] = jnp.zeros(token.shape, token.dtype)

    hbm = pl.BlockSpec(memory_space=pltpu.HBM)
    sem = pl.BlockSpec(memory_space=pltpu.SEMAPHORE)
    n_sem = n * (N_DEV - 1)
    res = pl.pallas_call(
        body, name=name,
        out_shape=(pltpu.SemaphoreType.DMA((n_sem,)), pltpu.SemaphoreType.DMA((n_sem,)),
                   *[pltpu.HBM(a.shape, a.dtype) for a in arrs], *[pltpu.HBM(l.shape, l.dtype) for l in lands],
                   jax.ShapeDtypeStruct((8, LANES), F32)),
        in_specs=[hbm] * (2 * n) + ([] if after is None else [pl.BlockSpec(memory_space=pl.ANY)]),
        out_specs=(sem, sem, *([hbm] * (2 * n)), pl.BlockSpec(memory_space=pltpu.VMEM)),
        input_output_aliases={i: 2 + i for i in range(2 * n)},
        compiler_params=pltpu.CompilerParams(has_side_effects=pltpu.SideEffectType.DATAFLOW_SIDE_EFFECTING),
    )(*[pltpu.with_memory_space_constraint(a, pltpu.HBM) for a in arrs],
      *[pltpu.with_memory_space_constraint(l, pltpu.HBM) for l in lands], *([] if after is None else [after]))
    return dict(n=n, scatter=scatter, send=res[0], recv=res[1], srcs=res[2:2 + n], lands=res[2 + n:2 + 2 * n], token=res[2 + 2 * n])


def exchange_wait(name, st, after):
    n, scatter = st["n"], st["scatter"]
    after = list(after) if isinstance(after, (list, tuple)) else [after]

    def body(*refs):
        srcs, lands_ = refs[:n], refs[n:2 * n]
        send_sems, recv_sems = refs[2 * n], refs[2 * n + 1]
        _, copies = _exchange_copies(srcs, lands_, send_sems, recv_sems, scatter)
        for cp in copies:
            cp.wait_send()
            cp.wait_recv()

    hbm = pl.BlockSpec(memory_space=pltpu.HBM)
    sem = pl.BlockSpec(memory_space=pltpu.SEMAPHORE)
    res = pl.pallas_call(
        body, name=name,
        out_shape=tuple(pltpu.HBM(a.shape, a.dtype) for a in (*st["srcs"], *st["lands"])),
        in_specs=[hbm] * (2 * n) + [sem, sem] + [pl.BlockSpec(memory_space=pl.ANY)] * len(after), out_specs=tuple([hbm] * (2 * n)),
        input_output_aliases={i: i for i in range(2 * n)},
        compiler_params=pltpu.CompilerParams(has_side_effects=pltpu.SideEffectType.DATAFLOW_SIDE_EFFECTING),
    )(*st["srcs"], *st["lands"], st["send"], st["recv"], *after)
    return list(res[:n]), list(res[n:])


def adamw_sum(name, parts, w, m, v, tr=256):
    nl = len(parts)
    k, r, c = parts[0].shape
    tr = _tile(r, tr, 16)
    c1 = 1.0 - ADAM_B1 ** ADAM_STEP
    c2 = 1.0 - ADAM_B2 ** ADAM_STEP

    def body(*refs):
        p_refs = refs[:nl]
        w_ref, m_ref, v_ref, g_ref, d_ref, nm_ref, nv_ref = refs[nl:]
        for l in range(nl):
            @pl.when(pl.program_id(0) == l)
            def _(p_ref=p_refs[l]):
                grad = p_ref[0].astype(F32)
                for j in range(1, k):
                    grad = grad + p_ref[j].astype(F32)
                new_m = ADAM_B1 * m_ref[...] + (1.0 - ADAM_B1) * grad
                new_v = ADAM_B2 * v_ref[...] + (1.0 - ADAM_B2) * (grad * grad)
                m_hat = new_m / c1
                v_hat = new_v / c2
                g_ref[...] = grad
                d_ref[...] = -ADAM_LR * (m_hat / (jnp.sqrt(v_hat) + ADAM_EPS) + ADAM_WD * w_ref[...])
                nm_ref[...] = new_m
                nv_ref[...] = new_v

    part = lambda l: pl.BlockSpec((k, tr, c), functools.partial(lambda ll, i, l: (0, jnp.where(ll == l, i, 0), 0), l=l))
    blk = pl.BlockSpec((None, tr, c), lambda ll, i: (ll, i, 0))
    return pl.pallas_call(
        body, name=name, grid=(nl, r // tr),
        in_specs=[part(l) for l in range(nl)] + [blk, blk, blk],
        out_specs=[blk] * 4, out_shape=[jax.ShapeDtypeStruct((nl, r, c), F32)] * 4,
        compiler_params=_params(("parallel", "parallel")),
    )(*parts, w, m, v)


def ada_fwd(c_all, ada_w):
    nl, d, n = ada_w.shape

    def body(c_ref, w_ref, o_ref):
        o_ref[...] = jnp.dot(_silu(c_ref[...]), w_ref[...], precision=HIGHEST, preferred_element_type=F32)

    return pl.pallas_call(
        body, name="ada_fwd", grid=(nl,),
        in_specs=[pl.BlockSpec((N_DEV, d), lambda l: (0, 0)), pl.BlockSpec((None, d, n), lambda l: (l, 0, 0))],
        out_specs=pl.BlockSpec((None, N_DEV, n), lambda l: (l, 0, 0)),
        out_shape=jax.ShapeDtypeStruct((nl, N_DEV, n), F32),
        compiler_params=_params(("parallel",)),
    )(c_all, ada_w)


def ada_bwd(c_all_t, dmod, td=256):
    d = c_all_t.shape[0]
    nl, _, n = dmod.shape
    td = _tile(d, td, 8)

    def body(c_ref, dm_ref, o_ref):
        ca = _silu(c_ref[...])
        acc = ca[:, 0:1] * dm_ref[0:1, :]
        for b in range(1, N_DEV):
            acc = acc + ca[:, b:b + 1] * dm_ref[b:b + 1, :]
        o_ref[...] = acc

    return pl.pallas_call(
        body, name="ada_bwd", grid=(nl, d // td),
        in_specs=[pl.BlockSpec((td, N_DEV), lambda l, i: (i, 0)), pl.BlockSpec((None, N_DEV, n), lambda l, i: (l, 0, 0))],
        out_specs=pl.BlockSpec((None, td, n), lambda l, i: (l, i, 0)),
        out_shape=jax.ShapeDtypeStruct((nl, d, n), F32),
        compiler_params=_params(("parallel", "parallel")),
    )(c_all_t, dmod)


ARG_NAMES = ["x", "c", "ada_w", "ada_b", "mix_pre_g", "mix_post_g", "mlp_pre_g", "mlp_post_g", "w_in", "gmlp_ln_g", "gmlp_ln_b",
             "gmlp_ws", "gmlp_bs", "w_a_out", "conv_w", "conv_b", "conv_ln_g", "conv_ln_b", "w_b_out", "fox_bf", "w_c_out",
             "w_out", "mlp_w1", "mlp_w2"]
WEIGHTS = ARG_NAMES[2:]
COL_SHARDED = ["w_in", "w_a_out", "w_b_out", "w_c_out", "mlp_w1"]
ROW_SHARDED = ["w_out", "mlp_w2"]
BIG = COL_SHARDED + ROW_SHARDED
GROUPS = {"in": ["w_in"], "abco": ["w_a_out", "w_b_out", "w_c_out", "w_out"], "mlp": ["mlp_w1", "mlp_w2"]}
SMALL = ["ada_b", "mix_pre_g", "mix_post_g", "mlp_pre_g", "mlp_post_g", "gmlp_ln_g", "gmlp_ln_b", "gmlp_ws", "gmlp_bs",
         "conv_b", "conv_ln_g", "conv_ln_b", "fox_bf"]
PACK_COLS = 512


def _ref_ranges(lo, hi, shard):
    out = []
    while lo < hi:
        j = lo // shard
        end = min(hi, (j + 1) * shard)
        out.append((j, lo - j * shard, end - j * shard))
        lo = end
    return out


def _w_in_to_kernel_layout(g, d):
    m = D_MIX
    nf = 7 * m
    shard = g.shape[2]
    cols = lambda lo, hi: [g[j, :, a:b] for j, a, b in _ref_ranges(lo, hi, shard)]
    forget = jnp.concatenate(cols(nf, nf + N_HEADS), axis=1)
    return jnp.concatenate(cols(nf + N_HEADS, nf + N_HEADS + 3 * d) + cols(0, nf) + [jnp.repeat(forget, HEAD_DIM, axis=1)], axis=1)


def _w_in_grad_blocks(gw, d):
    m = D_MIX
    nf = 7 * m
    n_ref = nf + N_HEADS + 3 * d
    shard = n_ref // N_DEV
    segs = [(0, nf, 3 * d, 1), (nf, nf + N_HEADS, 3 * d + nf, HEAD_DIM), (nf + N_HEADS, n_ref, 0, 1)]
    blocks = []
    for j in range(N_DEV):
        lo, hi = j * shard, (j + 1) * shard
        pieces = []
        for r0, r1, k0, stride in segs:
            a, b = max(lo, r0), min(hi, r1)
            if a < b:
                pieces.append(gw[:, k0 + (a - r0) * stride:k0 + (b - r0) * stride:stride])
        blocks.append(jnp.concatenate(pieces, axis=1) if len(pieces) > 1 else pieces[0])
    return jnp.stack(blocks)


def _pack(parts):
    flat = jnp.concatenate([p.reshape(-1).astype(F32) for p in parts])
    pad = (-flat.shape[0]) % (PACK_COLS * 8)
    return jnp.pad(flat, (0, pad)).reshape(-1, PACK_COLS)


def _unpack(packed, shapes):
    nl = packed.shape[0]
    flat, out, off = packed.reshape(nl, -1), [], 0
    for shp in shapes:
        n = math.prod(shp)
        out.append(flat[:, off:off + n].reshape((nl,) + tuple(shp)))
        off += n
    return out


def _layer_small(p, conv_full, l):
    wl = {}
    for k in ["mix_pre_g", "mix_post_g", "mlp_pre_g", "mlp_post_g", "gmlp_ln_g", "gmlp_ln_b", "conv_b", "conv_ln_g", "conv_ln_b"]:
        wl[k] = p[k][l][None, :]
    wm = p["gmlp_ws"][l] * jnp.tril(jnp.ones((CHUNK, CHUNK), F32))
    wl["ws"] = wm.astype(BF16)
    wl["ws_t"] = jnp.transpose(wm, (0, 2, 1)).astype(BF16)
    wl["bs_exp"] = jnp.repeat(p["gmlp_bs"][l].T, GROUP_DIM, axis=1)
    wl["bf_exp"] = jnp.repeat(p["fox_bf"][l], HEAD_DIM)[None, :]
    wl["conv_w"] = jnp.pad(conv_full[l], ((0, CONV_HALO - CONV_WIDTH), (0, 0)))
    return wl


def kernel(x, c, ada_w, ada_b, mix_pre_g, mix_post_g, mlp_pre_g, mlp_post_g, w_in, gmlp_ln_g, gmlp_ln_b, gmlp_ws, gmlp_bs, w_a_out, conv_w, conv_b, conv_ln_g, conv_ln_b, w_b_out, fox_bf, w_c_out, w_out, mlp_w1, mlp_w2, loss_target, m_ada_w, m_ada_b, m_mix_pre_g, m_mix_post_g, m_mlp_pre_g, m_mlp_post_g, m_w_in, m_gmlp_ln_g, m_gmlp_ln_b, m_gmlp_ws, m_gmlp_bs, m_w_a_out, m_conv_w, m_conv_b, m_conv_ln_g, m_conv_ln_b, m_w_b_out, m_fox_bf, m_w_c_out, m_w_out, m_mlp_w1, m_mlp_w2, v_ada_w, v_ada_b, v_mix_pre_g, v_mix_post_g, v_mlp_pre_g, v_mlp_post_g, v_w_in, v_gmlp_ln_g, v_gmlp_ln_b, v_gmlp_ws, v_gmlp_bs, v_w_a_out, v_conv_w, v_conv_b, v_conv_ln_g, v_conv_ln_b, v_w_b_out, v_fox_bf, v_w_c_out, v_w_out, v_mlp_w1, v_mlp_w2):
    args = (x, c, ada_w, ada_b, mix_pre_g, mix_post_g, mlp_pre_g, mlp_post_g, w_in, gmlp_ln_g, gmlp_ln_b, gmlp_ws, gmlp_bs, w_a_out,
            conv_w, conv_b, conv_ln_g, conv_ln_b, w_b_out, fox_bf, w_c_out, w_out, mlp_w1, mlp_w2)
    ms = (m_ada_w, m_ada_b, m_mix_pre_g, m_mix_post_g, m_mlp_pre_g, m_mlp_post_g, m_w_in, m_gmlp_ln_g, m_gmlp_ln_b, m_gmlp_ws, m_gmlp_bs,
          m_w_a_out, m_conv_w, m_conv_b, m_conv_ln_g, m_conv_ln_b, m_w_b_out, m_fox_bf, m_w_c_out, m_w_out, m_mlp_w1, m_mlp_w2)
    vs = (v_ada_w, v_ada_b, v_mix_pre_g, v_mix_post_g, v_mlp_pre_g, v_mlp_post_g, v_w_in, v_gmlp_ln_g, v_gmlp_ln_b, v_gmlp_ws, v_gmlp_bs,
          v_w_a_out, v_conv_w, v_conv_b, v_conv_ln_g, v_conv_ln_b, v_w_b_out, v_fox_bf, v_w_c_out, v_w_out, v_mlp_w1, v_mlp_w2)
    p = dict(zip(ARG_NAMES, args))
    mom = dict(zip(WEIGHTS, ms))
    var = dict(zip(WEIGHTS, vs))
    nl = ada_w.shape[0]
    s, d = x.shape[1], x.shape[2]
    me = 4 * lax.axis_index("x") + 2 * lax.axis_index("y") + lax.axis_index("c")

    c_all, conv_all = exchange("gather_c", [c, conv_w], scatter=False)
    c_all = c_all.reshape(N_DEV, d)
    n_ada = ada_w.shape[2]
    mod_parts = ada_fwd(c_all, ada_w)
    (mod_recv,) = exchange("scatter_mod", [jnp.transpose(mod_parts, (1, 0, 2))], scatter=True)
    conv_full = jnp.transpose(conv_all, (1, 2, 0, 3)).reshape(nl, CONV_WIDTH, D_MIX)

    def full_matrix(k, land, own):
        g = lax.dynamic_update_index_in_dim(land, own, me, 0)
        r, cc = own.shape
        if k == "w_in":
            return _w_in_to_kernel_layout(g, d)
        return jnp.transpose(g, (1, 0, 2)).reshape(r, N_DEV * cc) if k in COL_SHARDED else g.reshape(N_DEV * r, cc)

    started = [mod_recv]

    def fetch(l, tag):
        keys = GROUPS[tag]
        st = exchange_start(f"gather_{tag}{l}_start", [p[k][l].astype(BF16) for k in keys], scatter=False, after=started[-1])
        started.append(st["token"])

        def get(after):
            owns, lands = exchange_wait(f"gather_{tag}{l}_wait", st, after)
            return {k: full_matrix(k, land, o) for k, land, o in zip(keys, lands, owns)}

        return get

    getters = [{tag: fetch(l, tag) for tag in GROUPS} for l in range(nl)]
    mod = jnp.transpose(mod_recv, (1, 0, 2)).reshape(nl, N_DEV * n_ada) + ada_b + started[-1][0:1, 0:1]
    mods = [mod[l].reshape(6, d) for l in range(nl)]

    sent = {}

    def emitter(l):
        def emit(tag, grads_big):
            keys = GROUPS[tag]
            send = []
            for k in keys:
                gk = grads_big[k]
                if k == "w_in":
                    gk = _w_in_grad_blocks(gk, d)
                elif gk.ndim == 3:
                    pass
                elif k in COL_SHARDED:
                    r, cc = gk.shape[0], gk.shape[1] // N_DEV
                    gk = jnp.transpose(gk.reshape(r, N_DEV, cc), (1, 0, 2))
                else:
                    gk = gk.reshape(N_DEV, gk.shape[0] // N_DEV, gk.shape[1])
                send.append(gk.astype(BF16))
            sent[(l, tag)] = (keys, exchange_start(f"scatter_{tag}{l}_start", send, scatter=True))
            return sent[(l, tag)][1]["token"]

        return emit

    small_sent = {}

    def small_emitter(l):
        def emit_small(dmod_l, g):
            packed = _pack([dmod_l] + [g[k] for k in SMALL[1:]] + [g["conv_w"]])
            small_sent[l] = exchange_start(f"gather_small{l}_start", [packed], scatter=False)
            return small_sent[l]["token"]

        return emit_small

    small_shapes = [p[k].shape[1:] for k in SMALL] + [(CONV_WIDTH, D_MIX)]
    zeros_conv = jnp.zeros(small_shapes[-1], F32)
    packs = [jnp.stack([_pack([src[k][l] for k in SMALL] + [zeros_conv]) for l in range(nl)]) for src in (p, mom, var)]
    smalls = [_layer_small(p, conv_full, l) for l in range(nl)]
    row_major = {"w_in": tuple(lax.reduce_precision(src["w_in"], 8, 23) for src in (p, mom, var))}
    early = packs + [a for sm in smalls for a in sm.values()] + list(row_major["w_in"])
    first_in = getters[0]["in"]
    getters[0]["in"] = lambda after: first_in([after] + early)
    layers = [dict(small=smalls[l], emit_small=small_emitter(l), get_in=getters[l]["in"], get_abco=getters[l]["abco"], get_mlp=getters[l]["mlp"],
                   emit=emitter(l)) for l in range(nl)]
    loss_local, dx, dmods, grads = local_step(x[0], loss_target[0], mods, layers)
    loss = lax.psum(loss_local, ("x", "y", "c"))
    grad_x = dx[None]

    out = {k: [None] * 4 for k in WEIGHTS}

    def shard_update(name, parts, k):
        shp = p[k].shape
        flat = lambda a: a.reshape(nl, -1, shp[-1])
        wk, mk, vk = row_major.get(k, (p[k], mom[k], var[k]))
        res = adamw_sum(name, [pt.reshape((pt.shape[0],) + flat(wk).shape[1:]) for pt in parts], flat(wk), flat(mk), flat(vk))
        out[k] = [a.reshape(shp) for a in res]

    parts = {}

    def collect(tag, after):
        for l in reversed(range(nl)):
            keys, st = sent[(l, tag)]
            sends, lands = exchange_wait(f"scatter_{tag}{l}_wait", st, after)
            for k, land, sd in zip(keys, lands, sends):
                own = lax.dynamic_index_in_dim(sd, me, 0, keepdims=False)
                parts[(k, l)] = lax.dynamic_update_index_in_dim(land, own, me, 0)

    behind_bwd = [dx] + [small_sent[l]["token"] for l in range(nl)]
    collect("mlp", behind_bwd)
    collect("abco", behind_bwd)
    for k in BIG[1:]:
        shard_update("adamw_" + k, [parts[(k, l)] for l in range(nl)], k)
    collect("in", [out[k][0] for k in BIG[1:]])
    shard_update("adamw_w_in", [parts[("w_in", l)] for l in range(nl)], "w_in")

    small_all = []
    for l in range(nl):
        srcs, lands = exchange_wait(f"gather_small{l}_wait", small_sent[l], [out[k][0] for k in BIG])
        small_all.append(lax.dynamic_update_index_in_dim(lands[0], srcs[0], me, 0))
    small_out = [_unpack(o, small_shapes) for o in adamw_sum("adamw_small", small_all, *packs)]
    for i, k in enumerate(SMALL):
        for j in range(4):
            out[k][j] = small_out[j][i]

    n_conv = conv_w.shape[2]
    conv_grad = lax.dynamic_slice_in_dim(small_out[0][-1], me * n_conv, n_conv, axis=2)
    shard_update("adamw_conv_w", [conv_grad[l][None] for l in range(nl)], "conv_w")

    dmod_all = jnp.stack([small_all[l].reshape(N_DEV, -1)[:, :6 * d] for l in range(nl)])
    dmod_mine = lax.dynamic_slice_in_dim(dmod_all, me * n_ada, n_ada, axis=2)
    g_ada = ada_bwd(c_all.T, dmod_mine)
    shard_update("adamw_ada_w", [g_ada[l][None] for l in range(nl)], "ada_w")

    res = [loss, grad_x]
    for j in range(4):
        res += [out[k][j] for k in WEIGHTS]
    return tuple(res)
```

```python
import functools
import math

import jax
import jax.numpy as jnp
from jax import lax
from jax.experimental import pallas as pl
from jax.experimental.pallas import tpu as pltpu

F32 = jnp.float32
BF16 = jnp.bfloat16
MESH = pl.DeviceIdType.MESH
N_DEV = 8
NORM_EPS = 1e-6
D_MIX = 512
N_HEADS = 8
HEAD_DIM = 64
GROUP_DIM = 64
CHUNK = 128
CONV_WIDTH = 31
CONV_HALO = 32
LANES = 128
ADAM_LR, ADAM_B1, ADAM_B2, ADAM_EPS, ADAM_WD, ADAM_STEP = 0.001, 0.9, 0.999, 1e-08, 0.01, 10
VMEM_LIMIT = 56 * 1024 * 1024
HIGHEST = lax.Precision.HIGHEST


def _tile(dim, pref, mult=LANES):
    t = min(pref, dim)
    t -= t % mult
    while t >= mult:
        if dim % t == 0:
            return t
        t -= mult
    return dim


def _params(sem):
    return pltpu.CompilerParams(dimension_semantics=sem, vmem_limit_bytes=VMEM_LIMIT)


def rowwise(name, fn, rows, consts, outs, accs=(), ts=512, into=None):
    s = rows[0][0].shape[0]
    ts = min(ts, s)
    nr, nc, no, na = len(rows), len(consts), len(outs), len(accs)
    n_in = nr + nc + (into is not None)

    def body(*refs):
        vals = [r[...].astype(F32) for r in refs[:nr]] + [r[...] for r in refs[nr:nr + nc]]
        res = fn(*vals)
        if not isinstance(res, (tuple, list)):
            res = (res,)
        for r, v in zip(refs[n_in:n_in + no], res[:no]):
            r[...] = v.astype(r.dtype)
        if na:
            acc_refs = refs[n_in + no:]

            @pl.when(pl.program_id(0) == 0)
            def _():
                for r in acc_refs:
                    r[...] = jnp.zeros(r.shape, r.dtype)

            for r, v in zip(acc_refs, res[no:]):
                r[...] += v.astype(F32)

    in_specs = [pl.BlockSpec((ts, w), functools.partial(lambda i, cb: (i, cb), cb=cb)) for (_, cb, w) in rows]
    in_specs += [pl.BlockSpec(c.shape, lambda i: (0, 0)) for c in consts]
    out_specs = [pl.BlockSpec((ts, w), lambda i: (i, 0)) for (w, _) in outs]
    out_specs += [pl.BlockSpec(shp, lambda i: (0, 0)) for shp in accs]
    out_shape = [jax.ShapeDtypeStruct((s, w), dt) for (w, dt) in outs]
    out_shape += [jax.ShapeDtypeStruct(shp, F32) for shp in accs]
    extra, aliases = [], {}
    if into is not None:
        buf, cb_into = into
        assert buf.dtype == outs[0][1] and buf.shape[0] == s
        in_specs.append(pl.BlockSpec(memory_space=pl.ANY))
        out_specs[0] = pl.BlockSpec((ts, outs[0][0]), lambda i: (i, cb_into))
        out_shape[0] = jax.ShapeDtypeStruct(buf.shape, buf.dtype)
        extra, aliases = [buf], {nr + nc: 0}
    res = pl.pallas_call(
        body, name=name, grid=(s // ts,), in_specs=in_specs, out_specs=out_specs, out_shape=out_shape,
        input_output_aliases=aliases, compiler_params=_params(("arbitrary",) if na else ("parallel",)),
    )(*[a for (a, _, _) in rows], *consts, *extra)
    return res


def rowwise_vjp(name, f, rows, consts, cts, grad_dtypes, ts=512, into=None):
    nr, nc, nt = len(rows), len(consts), len(cts)
    keep = [i for i, dt in enumerate(grad_dtypes) if dt is not None]
    k_into = 0 if into is None else into[2]

    def g(*vals):
        rv = [v.astype(F32) for v in vals[:nr]]
        ctv = tuple(v.astype(F32) for v in vals[nr:nr + nt])
        cv = list(vals[nr + nt:])
        _, vjp = jax.vjp(lambda *a: tuple(f(*a)), *rv, *cv)
        grads = vjp(ctv)
        row_grads = [grads[i] for i in keep]
        if k_into:
            row_grads = [jnp.concatenate(row_grads[:k_into], axis=1)] + row_grads[k_into:]
        return tuple(row_grads) + tuple(grads[nr:])

    outs = [(rows[i][2], grad_dtypes[i]) for i in keep]
    if k_into:
        assert len({dt for _, dt in outs[:k_into]}) == 1
        outs = [(sum(w for w, _ in outs[:k_into]), outs[0][1])] + outs[k_into:]
    return rowwise(name, g, list(rows) + list(cts), consts, outs, accs=[c.shape for c in consts], ts=ts,
                   into=None if into is None else into[:2])


def matmul(name, a, b, *, ta=False, tb=False, out_dtypes=(F32,), epilogue=None, epi=(), tm=None, tn=None, tk=4096, dep=None, b_cols=None, out_blocks=None,
           f32_tail=False):
    m, k = (a.shape[1], a.shape[0]) if ta else a.shape
    n = b.shape[0] if tb else b.shape[1]
    b_col0 = 0
    if b_cols is not None:
        assert not tb
        b_col0, n = b_cols
    assert (b.shape[1] if tb else b.shape[0]) == k
    tk = _tile(k, tk)
    if tk > 1024:
        tm, tn = _tile(m, tm or 1024), _tile(n, tn or 1024)
    else:
        tm, tn = _tile(m, tm or 2048), _tile(n, tn or (1024 if m >= 2048 else 2048))
    if out_blocks is not None:
        tn = n // out_blocks
    assert b_col0 % tn == 0
    jb = b_col0 // tn
    nk = k // tk
    nj = n // tn
    assert not f32_tail or (nk == 1 and epilogue is None)
    ne, no = len(epi), len(out_dtypes)
    dims = (((0 if ta else 1,), (1 if tb else 0,)), ((), ()))

    def body(*refs):
        a_ref, b_ref = refs[0], refs[1]
        epi_refs = refs[2:2 + ne]
        n_in = 2 + ne + (dep is not None)
        out_refs = refs[n_in:n_in + no]
        part = lax.dot_general(a_ref[...].astype(BF16), b_ref[...].astype(BF16), dims, preferred_element_type=F32)

        def finish(acc):
            res = (acc,) if epilogue is None else epilogue(acc, *[r[...] for r in epi_refs])
            for r, v in zip(out_refs, res):
                r[...] = v.astype(r.dtype)

        if nk == 1:
            finish(part)
            if f32_tail:
                @pl.when(pl.program_id(1) == nj - 1)
                def _():
                    refs[n_in + no][...] = part
        else:
            acc_ref = refs[-1]
            kk = pl.program_id(2)

            @pl.when(kk == 0)
            def _():
                acc_ref[...] = part

            @pl.when(kk > 0)
            def _():
                acc_ref[...] += part

            @pl.when(kk == nk - 1)
            def _():
                finish(acc_ref[...])

    a_spec = pl.BlockSpec((tk, tm), lambda i, j, kk: (kk, i)) if ta else pl.BlockSpec((tm, tk), lambda i, j, kk: (i, kk))
    b_spec = pl.BlockSpec((tn, tk), lambda i, j, kk: (j, kk)) if tb else pl.BlockSpec((tk, tn), lambda i, j, kk: (kk, j + jb))
    epi_specs = []
    for (arr, col0) in epi:
        assert col0 % tn == 0
        epi_specs.append(pl.BlockSpec((tm, tn), functools.partial(lambda i, j, kk, c0: (i, j + c0), c0=col0 // tn)))
    res = pl.pallas_call(
        body, name=name, grid=(m // tm, n // tn, nk),
        in_specs=[a_spec, b_spec] + epi_specs + ([] if dep is None else [pl.BlockSpec(dep.shape, lambda i, j, kk: (0, 0))]),
        out_specs=[pl.BlockSpec((tm, tn), lambda i, j, kk: (i, j)) if out_blocks is None else
                   pl.BlockSpec((None, tm, tn), lambda i, j, kk: (j, i, 0)) for _ in out_dtypes]
        + ([pl.BlockSpec((tm, tn), lambda i, j, kk: (i, 0))] if f32_tail else []),
        out_shape=[jax.ShapeDtypeStruct((m, n) if out_blocks is None else (out_blocks, m, tn), dt) for dt in out_dtypes]
        + ([jax.ShapeDtypeStruct((m, tn), F32)] if f32_tail else []),
        scratch_shapes=[pltpu.VMEM((tm, tn), F32)] if nk > 1 else [],
        compiler_params=_params(("parallel", "arbitrary" if f32_tail else "parallel", "arbitrary")),
    )(a, b, *[arr for (arr, _) in epi], *([] if dep is None else [dep]))
    return res[0] if len(res) == 1 else res


def _rms(x, g):
    return x * lax.rsqrt(jnp.mean(x * x, axis=-1, keepdims=True) + NORM_EPS) * g


def _ln(x, g, b):
    mu = jnp.mean(x, axis=-1, keepdims=True)
    xc = x - mu
    var = jnp.mean(xc * xc, axis=-1, keepdims=True)
    return xc * lax.rsqrt(var + NORM_EPS) * g + b


def _gelu(x):
    return 0.5 * x * (1.0 + jnp.tanh(math.sqrt(2.0 / math.pi) * (x + 0.044715 * (x * x * x))))


def _sigmoid(x):
    return 1.0 / (1.0 + jnp.exp(-x))


def _silu(x):
    return x * _sigmoid(x)


def _log_sigmoid(x):
    return jnp.minimum(x, 0.0) - jnp.log(1.0 + jnp.exp(-jnp.abs(x)))


def _f_pre(x, g, sc, sh):
    return (_rms(x, g) * (1.0 + sc) + sh,)


def _f_post(y, g, gt):
    return (gt * _rms(y, g),)


def _f_a1(u_raw, v_raw, g, b):
    return _gelu(u_raw), _ln(_gelu(v_raw), g, b)


def _f_glu(val, gate):
    return (val * _sigmoid(gate),)


def _f_lnsilu(zc, g, b):
    return (_silu(_ln(zc, g, b)),)


def _f_merge(g0, g1, g2, ya, yb, yc):
    return (_sigmoid(g0) * ya + _sigmoid(g1) * yb + _sigmoid(g2) * yc,)


def _lane_lt64(shape):
    return lax.broadcasted_iota(jnp.int32, shape, 1) < HEAD_DIM


def spatial_fwd(vln, u, w_bf, b_exp, rows_per_step=512):
    s = vln.shape[0]
    tr = min(rows_per_step, s)

    def body(v_ref, u_ref, w_ref, b_ref, sv_ref, ya_ref):
        lo = _lane_lt64((CHUNK, LANES))
        for ch in range(tr // CHUNK):
            r0 = ch * CHUNK
            for p in range(D_MIX // LANES):
                vp = v_ref[r0:r0 + CHUNK, p * LANES:(p + 1) * LANES]
                o0 = jnp.dot(w_ref[2 * p], vp, preferred_element_type=F32)
                o1 = jnp.dot(w_ref[2 * p + 1], vp, preferred_element_type=F32)
                sv = jnp.where(lo, o0, o1) + b_ref[:, p * LANES:(p + 1) * LANES]
                sv_ref[r0:r0 + CHUNK, p * LANES:(p + 1) * LANES] = sv
                ya_ref[r0:r0 + CHUNK, p * LANES:(p + 1) * LANES] = (
                    u_ref[r0:r0 + CHUNK, p * LANES:(p + 1) * LANES] * sv).astype(BF16)

    row = pl.BlockSpec((tr, D_MIX), lambda i: (i, 0))
    return pl.pallas_call(
        body, name="spatial_fwd", grid=(s // tr,),
        in_specs=[row, row, pl.BlockSpec(w_bf.shape, lambda i: (0, 0, 0)), pl.BlockSpec(b_exp.shape, lambda i: (0, 0))],
        out_specs=[row, row],
        out_shape=[jax.ShapeDtypeStruct((s, D_MIX), F32), jax.ShapeDtypeStruct((s, D_MIX), BF16)],
        compiler_params=_params(("parallel",)),
    )(vln, u, w_bf, b_exp)


def spatial_bwd(dya, u, sv, vln, wt_bf, rows_per_step=512):
    s = vln.shape[0]
    tr = min(rows_per_step, s)
    ng = wt_bf.shape[0]

    def body(dya_ref, u_ref, sv_ref, v_ref, wt_ref, du_ref, dv_ref, dw_ref, db_ref):
        @pl.when(pl.program_id(0) == 0)
        def _():
            dw_ref[...] = jnp.zeros(dw_ref.shape, F32)
            db_ref[...] = jnp.zeros(db_ref.shape, F32)

        lo = _lane_lt64((CHUNK, LANES))
        for ch in range(tr // CHUNK):
            r0 = ch * CHUNK
            for p in range(D_MIX // LANES):
                cs = slice(p * LANES, (p + 1) * LANES)
                dya_p = dya_ref[r0:r0 + CHUNK, cs].astype(F32)
                du_ref[r0:r0 + CHUNK, cs] = dya_p * sv_ref[r0:r0 + CHUNK, cs]
                dsv = dya_p * u_ref[r0:r0 + CHUNK, cs]
                db_ref[:, cs] += dsv
                dsv0 = jnp.where(lo, dsv, 0.0).astype(BF16)
                dsv1 = jnp.where(lo, 0.0, dsv).astype(BF16)
                vp = v_ref[r0:r0 + CHUNK, cs]
                d0 = jnp.dot(wt_ref[2 * p], dsv0, preferred_element_type=F32)
                d1 = jnp.dot(wt_ref[2 * p + 1], dsv1, preferred_element_type=F32)
                dv_ref[r0:r0 + CHUNK, cs] = d0 + d1
                nt = (((1,), (1,)), ((), ()))
                dw_ref[2 * p] += lax.dot_general(dsv0, vp, nt, preferred_element_type=F32)
                dw_ref[2 * p + 1] += lax.dot_general(dsv1, vp, nt, preferred_element_type=F32)

    row = pl.BlockSpec((tr, D_MIX), lambda i: (i, 0))
    return pl.pallas_call(
        body, name="spatial_bwd", grid=(s // tr,),
        in_specs=[row, row, row, row, pl.BlockSpec(wt_bf.shape, lambda i: (0, 0, 0))],
        out_specs=[row, row, pl.BlockSpec((ng, CHUNK, CHUNK), lambda i: (0, 0, 0)), pl.BlockSpec((CHUNK, D_MIX), lambda i: (0, 0))],
        out_shape=[jax.ShapeDtypeStruct((s, D_MIX), F32), jax.ShapeDtypeStruct((s, D_MIX), F32),
                   jax.ShapeDtypeStruct((ng, CHUNK, CHUNK), F32), jax.ShapeDtypeStruct((CHUNK, D_MIX), F32)],
        compiler_params=_params(("arbitrary",)),
    )(dya, u, sv, vln, wt_bf)


def _windows(ref, first, count, ts):
    for r in range(8):
        ks = [k for k in range(count) if (first + k) % 8 == r]
        if ks:
            base = first + ks[0]
            blk = ref[base:base + ks[-1] - ks[0] + ts, :]
            for k in ks:
                yield k, blk[k - ks[0]:k - ks[0] + ts, :]


def conv_fwd(proj, cb_val, cb_gate, w_pad, cb, ln_g, ln_b, ts=256):
    s = proj.shape[0]
    ts = min(ts, s)
    per = ts // CONV_HALO

    def body(val_ref, gate_ref, pval_ref, pgate_ref, w_ref, cb_ref, g_ref, b_ref, zc_ref, yb_ref, ext_ref):
        i = pl.program_id(0)
        zprev = pval_ref[...].astype(F32) * _sigmoid(pgate_ref[...].astype(F32))
        ext_ref[0:CONV_HALO, :] = jnp.where(i > 0, zprev, 0.0)
        ext_ref[CONV_HALO:, :] = val_ref[...].astype(F32) * _sigmoid(gate_ref[...].astype(F32))
        acc = jnp.zeros((ts, D_MIX), F32)
        for j, win in _windows(ext_ref, CONV_HALO - (CONV_WIDTH - 1), CONV_WIDTH, ts):
            acc = acc + w_ref[j:j + 1, :] * win
        zc = acc + cb_ref[...]
        zc_ref[...] = zc
        yb_ref[...] = _f_lnsilu(zc, g_ref[...], b_ref[...])[0].astype(BF16)

    def cur(c):
        return pl.BlockSpec((ts, D_MIX), functools.partial(lambda i, c: (i, c), c=c))

    def prev(c):
        return pl.BlockSpec((CONV_HALO, D_MIX), functools.partial(lambda i, c: (jnp.maximum(i * per - 1, 0), c), c=c))

    const = lambda a: pl.BlockSpec(a.shape, lambda i: (0, 0))
    out = pl.BlockSpec((ts, D_MIX), lambda i: (i, 0))
    return pl.pallas_call(
        body, name="conv_fwd", grid=(s // ts,),
        in_specs=[cur(cb_val), cur(cb_gate), prev(cb_val), prev(cb_gate), const(w_pad), const(cb), const(ln_g), const(ln_b)],
        out_specs=[out, out],
        out_shape=[jax.ShapeDtypeStruct((s, D_MIX), F32), jax.ShapeDtypeStruct((s, D_MIX), BF16)],
        scratch_shapes=[pltpu.VMEM((CONV_HALO + ts, D_MIX), F32)],
        compiler_params=_params(("parallel",)),
    )(proj, proj, proj, proj, w_pad, cb, ln_g, ln_b)


def conv_bwd(proj, cb_val, cb_gate, zc, dyb, w_pad, ln_g, ln_b, dproj, ts=256):
    assert cb_gate == cb_val + 1 and cb_val % 2 == 0
    s = proj.shape[0]
    ts = min(ts, s)
    per = ts // CONV_HALO
    n_tiles = s // ts
    n_halo = s // CONV_HALO

    def body(val_ref, gate_ref, pval_ref, pgate_ref, zc_ref, dyb_ref, nzc_ref, ndyb_ref, w_ref, g_ref, b_ref, dproj_in,
             dvg_ref, dw_ref, dcb_ref, dg_ref, db_ref, zext_ref, dext_ref):
        i = pl.program_id(0)

        @pl.when(i == 0)
        def _():
            for r in (dw_ref, dcb_ref, dg_ref, db_ref):
                r[...] = jnp.zeros(r.shape, F32)

        g, b = g_ref[...], b_ref[...]
        _, vjp = jax.vjp(lambda z, gg, bb: _f_lnsilu(z, gg, bb)[0], zc_ref[...], g, b)
        dzc, dg, db = vjp(dyb_ref[...].astype(F32))
        dg_ref[...] += dg
        db_ref[...] += db
        dcb_ref[...] += jnp.sum(dzc, axis=0, keepdims=True)
        _, vjp_n = jax.vjp(lambda z: _f_lnsilu(z, g, b)[0], nzc_ref[...])
        (dzc_next,) = vjp_n(ndyb_ref[...].astype(F32))
        dext_ref[0:ts, :] = dzc
        dext_ref[ts:, :] = jnp.where(i < n_tiles - 1, dzc_next, 0.0)
        val, gate = val_ref[...].astype(F32), gate_ref[...].astype(F32)
        zprev = pval_ref[...].astype(F32) * _sigmoid(pgate_ref[...].astype(F32))
        zext_ref[0:CONV_HALO, :] = jnp.where(i > 0, zprev, 0.0)
        zext_ref[CONV_HALO:, :] = val * _sigmoid(gate)
        dz = jnp.zeros((ts, D_MIX), F32)
        for shift, win in _windows(dext_ref, 0, CONV_WIDTH, ts):
            j = CONV_WIDTH - 1 - shift
            dz = dz + w_ref[j:j + 1, :] * win
        for j, win in _windows(zext_ref, CONV_HALO - (CONV_WIDTH - 1), CONV_WIDTH, ts):
            dw_ref[j:j + 1, :] += jnp.sum(dzc * win, axis=0, keepdims=True)
        _, vjp_glu = jax.vjp(lambda a, c: _f_glu(a, c)[0], val, gate)
        dval, dgate = vjp_glu(dz)
        dvg_ref[:, :D_MIX] = dval.astype(BF16)
        dvg_ref[:, D_MIX:] = dgate.astype(BF16)

    def cur(c):
        return pl.BlockSpec((ts, D_MIX), functools.partial(lambda i, c: (i, c), c=c))

    def prev(c):
        return pl.BlockSpec((CONV_HALO, D_MIX), functools.partial(lambda i, c: (jnp.maximum(i * per - 1, 0), c), c=c))

    nxt = pl.BlockSpec((CONV_HALO, D_MIX), lambda i: (jnp.minimum((i + 1) * per, n_halo - 1), 0))
    const = lambda a: pl.BlockSpec(a.shape, lambda i: (0, 0))
    out = pl.BlockSpec((ts, D_MIX), lambda i: (i, 0))
    vec = pl.BlockSpec((1, D_MIX), lambda i: (0, 0))
    return pl.pallas_call(
        body, name="conv_bwd", grid=(n_tiles,),
        in_specs=[cur(cb_val), cur(cb_gate), prev(cb_val), prev(cb_gate), out, out, nxt, nxt, const(w_pad), const(ln_g), const(ln_b),
                  pl.BlockSpec(memory_space=pl.ANY)],
        out_specs=[pl.BlockSpec((ts, 2 * D_MIX), lambda i: (i, cb_val // 2)), pl.BlockSpec((CONV_HALO, D_MIX), lambda i: (0, 0)), vec, vec, vec],
        out_shape=[jax.ShapeDtypeStruct(dproj.shape, dproj.dtype),
                   jax.ShapeDtypeStruct((CONV_HALO, D_MIX), F32)] + [jax.ShapeDtypeStruct((1, D_MIX), F32)] * 3,
        scratch_shapes=[pltpu.VMEM((CONV_HALO + ts, D_MIX), F32), pltpu.VMEM((ts + CONV_HALO, D_MIX), F32)],
        input_output_aliases={11: 0}, compiler_params=_params(("arbitrary",)),
    )(proj, proj, proj, proj, zc, dyb, zc, dyb, w_pad, ln_g, ln_b, dproj)


def forget_cumsum(proj, cb_f, bf_exp, t=256):
    s = proj.shape[0]
    t = min(t, s)

    def body(f_ref, bf_ref, out_ref, carry_ref):
        @pl.when(pl.program_id(0) == 0)
        def _():
            carry_ref[...] = jnp.zeros(carry_ref.shape, F32)

        lf = _log_sigmoid(f_ref[...] + bf_ref[...])
        tri = (lax.broadcasted_iota(jnp.int32, (t, t), 1) <= lax.broadcasted_iota(jnp.int32, (t, t), 0)).astype(F32)
        c = jnp.dot(tri, lf, precision=HIGHEST, preferred_element_type=F32) + carry_ref[...]
        out_ref[...] = c
        carry_ref[...] = c[t - 1:t, :]

    return pl.pallas_call(
        body, name="forget_cumsum", grid=(s // t,),
        in_specs=[pl.BlockSpec((t, D_MIX), functools.partial(lambda i, c: (i, c), c=cb_f)), pl.BlockSpec((1, D_MIX), lambda i: (0, 0))],
        out_specs=pl.BlockSpec((t, D_MIX), lambda i: (i, 0)),
        out_shape=jax.ShapeDtypeStruct((s, D_MIX), F32),
        scratch_shapes=[pltpu.VMEM((1, D_MIX), F32)],
        compiler_params=_params(("arbitrary",)),
    )(proj, bf_exp)


def forget_bwd(proj, cb_f, bf_exp, dcum, dproj, cb_out, t=256):
    s = proj.shape[0]
    t = min(t, s)
    n = s // t

    def body(f_ref, bf_ref, dc_ref, dproj_in, df_ref, dbf_ref, carry_ref):
        @pl.when(pl.program_id(0) == 0)
        def _():
            carry_ref[...] = jnp.zeros(carry_ref.shape, F32)
            dbf_ref[...] = jnp.zeros(dbf_ref.shape, F32)

        tri = (lax.broadcasted_iota(jnp.int32, (t, t), 1) >= lax.broadcasted_iota(jnp.int32, (t, t), 0)).astype(F32)
        r = jnp.dot(tri, dc_ref[...], precision=HIGHEST, preferred_element_type=F32) + carry_ref[...]
        carry_ref[...] = r[0:1, :]
        df = r * _sigmoid(-(f_ref[...] + bf_ref[...]))
        dbf_ref[...] += jnp.sum(df, axis=0, keepdims=True)
        live = lax.broadcasted_iota(jnp.int32, (t, D_MIX), 1) % HEAD_DIM == 0
        df_ref[...] = jnp.where(live, df, 0.0).astype(BF16)

    return pl.pallas_call(
        body, name="forget_bwd", grid=(n,),
        in_specs=[pl.BlockSpec((t, D_MIX), functools.partial(lambda i, c: (n - 1 - i, c), c=cb_f)), pl.BlockSpec((1, D_MIX), lambda i: (0, 0)),
                  pl.BlockSpec((t, D_MIX), lambda i: (n - 1 - i, 0)), pl.BlockSpec(memory_space=pl.ANY)],
        out_specs=[pl.BlockSpec((t, D_MIX), lambda i: (n - 1 - i, cb_out)), pl.BlockSpec((1, D_MIX), lambda i: (0, 0))],
        out_shape=[jax.ShapeDtypeStruct(dproj.shape, dproj.dtype), jax.ShapeDtypeStruct((1, D_MIX), F32)],
        scratch_shapes=[pltpu.VMEM((1, D_MIX), F32)],
        input_output_aliases={3: 0}, compiler_params=_params(("arbitrary",)),
    )(proj, bf_exp, dcum, dproj)


NT = (((1,), (1,)), ((), ()))
LOG2E = math.log2(math.e)
N_PAIR = D_MIX // LANES


def _split3(x):
    hi = x.astype(BF16).astype(F32)
    mid = (x - hi).astype(BF16).astype(F32)
    return hi, mid, x - hi - mid


def _triple(li, first, vals):
    out = jnp.where(li == first, vals[0], 0.0)
    for i in (1, 2):
        out = jnp.where(li == first + i, vals[i], out)
    return out


def _lane_ids(shape):
    lane = lax.broadcasted_iota(jnp.int32, shape, 1)
    return lane, lane % HEAD_DIM, lane < HEAD_DIM


def attn_prep(proj, cb_q, cum, ts=512):
    s = proj.shape[0]
    ts = min(ts, s)
    scale = LOG2E / math.sqrt(HEAD_DIM)

    def body(q_ref, k_ref, v_ref, c_ref, qe_ref, qo_ref, ke_ref, ko_ref, ve_ref, vo_ref):
        _, li, lo = _lane_ids((ts, LANES))
        one3 = lambda first: ((li >= first) & (li < first + 3)).astype(F32)
        for p in range(N_PAIR):
            ps = slice(p * LANES, (p + 1) * LANES)
            c3 = _split3(pltpu.roll(c_ref[:, ps] * LOG2E, HEAD_DIM, axis=1))
            eq = _triple(li, 0, c3) + one3(3)
            ek = one3(0) - _triple(li, 3, c3) + one3(6)
            ev = one3(0)
            for src, even, odd, extra, mul in ((q_ref, qe_ref, qo_ref, eq, scale), (k_ref, ke_ref, ko_ref, ek, 1.0), (v_ref, ve_ref, vo_ref, ev, 1.0)):
                x = src[:, ps].astype(F32) * mul
                even[:, ps] = jnp.where(lo, x, extra).astype(BF16)
                odd[:, ps] = jnp.where(lo, extra, x).astype(BF16)

    col = lambda c: pl.BlockSpec((ts, D_MIX), functools.partial(lambda i, c: (i, c), c=c))
    out = pl.BlockSpec((ts, D_MIX), lambda i: (i, 0))
    return pl.pallas_call(
        body, name="attn_prep", grid=(s // ts,),
        in_specs=[col(cb_q), col(cb_q + 1), col(cb_q + 2), pl.BlockSpec((ts, D_MIX), lambda i: (i, 0))],
        out_specs=[out] * 6, out_shape=[jax.ShapeDtypeStruct((s, D_MIX), BF16)] * 6,
        compiler_params=_params(("parallel",)),
    )(proj, proj, proj, cum)


def _pair_specs(s, t):
    return pl.BlockSpec((t, LANES), lambda p, i: (i, p)), pl.BlockSpec((s, LANES), lambda p, i: (0, p))


def attn_fwd(qe, qo, ke, ko, ve, vo, tq=1024):
    s = qe.shape[0]
    tq = min(tq, s)

    def body(qe_ref, qo_ref, ke_ref, ko_ref, ve_ref, vo_ref, o_ref, qbe_ref, qbo_ref):
        qi = pl.program_id(1)
        qs, k_refs, v_refs = (qe_ref[...], qo_ref[...]), (ke_ref, ko_ref), (ve_ref, vo_ref)
        causal = lax.broadcasted_iota(jnp.int32, (tq, tq), 1) <= lax.broadcasted_iota(jnp.int32, (tq, tq), 0)

        def step(j, carry, diag):
            ks = pl.multiple_of(j * tq, tq)
            new = []
            for h in range(2):
                m, l, acc = carry[h]
                sc = lax.dot_general(qs[h], k_refs[h][pl.ds(ks, tq), :], NT, preferred_element_type=F32)
                if diag:
                    sc = jnp.where(causal, sc, -jnp.inf)
                m_new = jnp.maximum(m, jnp.max(sc, axis=1, keepdims=True))
                p = jnp.exp2(sc - m_new)
                alpha = jnp.exp2(m - m_new)
                l = alpha * l + jnp.sum(p, axis=1, keepdims=True)
                acc = alpha * acc + jnp.dot(p.astype(BF16), v_refs[h][pl.ds(ks, tq), :], preferred_element_type=F32)
                new.append((m_new, l, acc))
            return tuple(new)

        init = tuple((jnp.full((tq, 1), -jnp.inf, F32), jnp.zeros((tq, 1), F32), jnp.zeros((tq, LANES), F32)) for _ in range(2))
        carry = lax.fori_loop(0, qi, lambda j, c: step(j, c, False), init)
        (m0, l0, a0), (m1, l1, a1) = step(qi, carry, True)
        _, li, lo = _lane_ids((tq, LANES))
        o_ref[...] = jnp.where(lo, a0 / l0, a1 / l1).astype(o_ref.dtype)
        lse_lanes = (li >= 6) & (li < 9)
        for q, m, l, spare, out_ref in ((qs[0], m0, l0, ~lo, qbe_ref), (qs[1], m1, l1, lo, qbo_ref)):
            neg_lse = _triple(li, 6, _split3(-(m + jnp.log(l) * LOG2E)))
            out_ref[...] = jnp.where(spare & lse_lanes, neg_lse.astype(BF16), q)

    blk, full = _pair_specs(s, tq)
    return pl.pallas_call(
        body, name="attn_fwd", grid=(N_PAIR, s // tq),
        in_specs=[blk, blk, full, full, full, full],
        out_specs=[blk] * 3, out_shape=[jax.ShapeDtypeStruct((s, D_MIX), BF16)] * 3,
        compiler_params=_params(("parallel", "parallel")),
    )(qe, qo, ke, ko, ve, vo)


def attn_dsum(qbe, qbo, ke, ko, ve, vo, do, dep, tq=1024):
    s = qbe.shape[0]
    tq = min(tq, s)

    def body(qe_ref, qo_ref, ke_ref, ko_ref, ve_ref, vo_ref, do_ref, dep_ref, dobe_ref, dobo_ref):
        qi = pl.program_id(1)
        _, li, lo = _lane_ids((tq, LANES))
        do_ = do_ref[...]
        qs, k_refs, v_refs = (qe_ref[...], qo_ref[...]), (ke_ref, ko_ref), (ve_ref, vo_ref)
        dos = (jnp.where(lo, do_, 0), jnp.where(lo, 0, do_))
        causal = lax.broadcasted_iota(jnp.int32, (tq, tq), 1) <= lax.broadcasted_iota(jnp.int32, (tq, tq), 0)

        def step(j, carry, diag):
            ks = pl.multiple_of(j * tq, tq)
            new = []
            for h in range(2):
                sc = lax.dot_general(qs[h], k_refs[h][pl.ds(ks, tq), :], NT, preferred_element_type=F32)
                if diag:
                    sc = jnp.where(causal, sc, -jnp.inf)
                pdp = jnp.exp2(sc) * lax.dot_general(dos[h], v_refs[h][pl.ds(ks, tq), :], NT, preferred_element_type=F32)
                new.append(carry[h] + jnp.sum(pdp, axis=1, keepdims=True))
            return tuple(new)

        init = (jnp.zeros((tq, 1), F32), jnp.zeros((tq, 1), F32))
        carry = lax.fori_loop(0, qi, lambda j, c: step(j, c, False), init)
        s0, s1 = step(qi, carry, True)
        dobe_ref[...] = jnp.where(lo, do_, _triple(li, 0, _split3(-s0)).astype(BF16))
        dobo_ref[...] = jnp.where(lo, _triple(li, 0, _split3(-s1)).astype(BF16), do_)

    blk, full = _pair_specs(s, tq)
    return pl.pallas_call(
        body, name="attn_dsum", grid=(N_PAIR, s // tq),
        in_specs=[blk, blk, full, full, full, full, blk, pl.BlockSpec(dep.shape, lambda p, i: (0, 0))],
        out_specs=[blk, blk], out_shape=[jax.ShapeDtypeStruct((s, D_MIX), BF16)] * 2,
        compiler_params=_params(("parallel", "parallel")),
    )(qbe, qbo, ke, ko, ve, vo, do, dep)


def attn_dkv(ke, ko, ve, vo, qbe, qbo, dobe, dobo, dproj, lane_block, tk=1024):
    s = ke.shape[0]
    tk = min(tk, s)
    nq = s // tk
    tn = (((0,), (0,)), ((), ()))

    def body(ke_ref, ko_ref, ve_ref, vo_ref, qe_ref, qo_ref, de_ref, do_ref, dproj_in, dk_ref, dv_ref, dck_ref, dq_ref):
        kj = pl.program_id(1)
        lo = _lane_lt64((tk, LANES))
        ks_, vs_, q_refs, d_refs = (ke_ref[...], ko_ref[...]), (ve_ref[...], vo_ref[...]), (qe_ref, qo_ref), (de_ref, do_ref)
        causal = lax.broadcasted_iota(jnp.int32, (tk, tk), 0) <= lax.broadcasted_iota(jnp.int32, (tk, tk), 1)

        @pl.when(kj == 0)
        def _():
            dq_ref[...] = jnp.zeros(dq_ref.shape, F32)

        def step(i, carry, diag):
            qs = pl.multiple_of(i * tk, tk)
            new, dq_h = [], []
            for h in range(2):
                dk, dv, dck = carry[h]
                qblk = q_refs[h][pl.ds(qs, tk), :]
                dblk = d_refs[h][pl.ds(qs, tk), :]
                st = lax.dot_general(ks_[h], qblk, NT, preferred_element_type=F32)
                if diag:
                    st = jnp.where(causal, st, -jnp.inf)
                pt = jnp.exp2(st)
                dst = pt * lax.dot_general(vs_[h], dblk, NT, preferred_element_type=F32)
                dst_bf = dst.astype(BF16)
                dq_h.append(lax.dot_general(dst_bf, ks_[h], tn, preferred_element_type=F32))
                new.append((dk + jnp.dot(dst_bf, qblk, preferred_element_type=F32),
                            dv + jnp.dot(pt.astype(BF16), dblk, preferred_element_type=F32),
                            dck - jnp.sum(dst, axis=1, keepdims=True)))
            dq_ref[pl.ds(qs, tk), :] += jnp.where(lo, dq_h[0], dq_h[1])
            return tuple(new)

        init = tuple((jnp.zeros((tk, LANES), F32), jnp.zeros((tk, LANES), F32), jnp.zeros((tk, 1), F32)) for _ in range(2))
        carry = step(kj, init, True)
        (dk0, dv0, dc0), (dk1, dv1, dc1) = lax.fori_loop(kj + 1, nq, lambda i, c: step(i, c, False), carry)
        dk_ref[...] = (jnp.where(lo, dk0, dk1) * (1.0 / LOG2E)).astype(dk_ref.dtype)
        dv_ref[...] = jnp.where(lo, dv0, dv1).astype(dv_ref.dtype)
        dck_ref[...] = jnp.where(lo, dc0, dc1)

    blk, full = _pair_specs(s, tk)
    return pl.pallas_call(
        body, name="attn_dkv", grid=(N_PAIR, nq),
        in_specs=[blk, blk, blk, blk, full, full, full, full, pl.BlockSpec(memory_space=pl.ANY)],
        out_specs=[pl.BlockSpec((tk, LANES), lambda p, i: (i, lane_block + p)), blk, blk, full],
        out_shape=[jax.ShapeDtypeStruct(dproj.shape, dproj.dtype), jax.ShapeDtypeStruct((s, D_MIX), BF16), jax.ShapeDtypeStruct((s, D_MIX), F32),
                   jax.ShapeDtypeStruct((s, D_MIX), F32)],
        input_output_aliases={8: 0}, compiler_params=_params(("parallel", "arbitrary")),
    )(ke, ko, ve, vo, qbe, qbo, dobe, dobo, dproj)


def _pre_bwd(name, x, g, sc, sh, dh, dres):
    d = x.shape[1]

    def fn(xv, dhv, dresv, gv, scv, shv):
        _, vjp = jax.vjp(lambda *a: _f_pre(*a)[0], xv, gv, scv, shv)
        dx, dg, dsc, dsh = vjp(dhv.astype(F32))
        return dx + dresv, dg, dsc, dsh

    return rowwise(name, fn, [(x, 0, d), (dh, 0, d), (dres, 0, d)], [g, sc, sh], [(d, F32)], accs=[(1, d)] * 3)


def layer_fwd(x, mod, layer):
    s, d = x.shape
    m = D_MIX
    w = dict(layer["small"])
    sh1, sc1, gt1, sh2, sc2, gt2 = (mod[i:i + 1] for i in range(6))
    cb = 3 * d // m
    (h,) = rowwise("pre1", _f_pre, [(x, 0, d)], [w["mix_pre_g"], sc1, sh1], [(d, BF16)])
    w.update(layer["get_in"](h))
    proj, fproj = matmul("w_in", h, w["w_in"], out_dtypes=(BF16,), f32_tail=True, tn=2 * m)
    fcb = 1
    w.update(layer["get_abco"](proj))
    u, vln = rowwise("gmlp_in", _f_a1, [(proj, cb, m), (proj, cb + 1, m)], [w["gmlp_ln_g"], w["gmlp_ln_b"]], [(m, F32), (m, BF16)])
    sv, ya = spatial_fwd(vln, u, w["ws"], w["bs_exp"])
    y_a = matmul("w_a", ya, w["w_a_out"], out_dtypes=(BF16,))
    zc, yb = conv_fwd(proj, cb + 2, cb + 3, w["conv_w"], w["conv_b"], w["conv_ln_g"], w["conv_ln_b"])
    y_b = matmul("w_b", yb, w["w_b_out"], out_dtypes=(BF16,))
    cum = forget_cumsum(fproj, fcb, w["bf_exp"])
    kv_ops = attn_prep(proj, cb + 4, cum)
    o, qbe, qbo = attn_fwd(*kv_ops)
    att = (qbe, qbo) + tuple(kv_ops[2:])
    y_c = matmul("w_c", o, w["w_c_out"], out_dtypes=(BF16,))
    (merged,) = rowwise("merge", _f_merge, [(proj, 0, d), (proj, 1, d), (proj, 2, d), (y_a, 0, d), (y_b, 0, d), (y_c, 0, d)], [], [(d, BF16)])
    y = matmul("w_out", merged, w["w_out"], out_dtypes=(BF16,))
    w.update(layer["get_mlp"](y))

    def post_pre(xv, yv, gp, gt, g2, sc, sh):
        x1 = xv + _f_post(yv, gp, gt)[0]
        return x1, _f_pre(x1, g2, sc, sh)[0]

    x1, h2 = rowwise("post1", post_pre, [(x, 0, d), (y, 0, d)], [w["mix_post_g"], gt1, w["mlp_pre_g"], sc2, sh2], [(d, F32), (d, BF16)])
    r = matmul("w1", h2, w["mlp_w1"], out_dtypes=(BF16,), epilogue=lambda acc: (jnp.square(jnp.maximum(acc, 0.0)),))
    y2 = matmul("w2", r, w["mlp_w2"], out_dtypes=(BF16,))
    (x2,) = rowwise("post2", lambda xv, yv, g, gt: xv + _f_post(yv, g, gt)[0], [(x1, 0, d), (y2, 0, d)], [w["mlp_post_g"], gt2], [(d, F32)])
    saved = dict(w=w, x=x, h=h, proj=proj, fproj=fproj, fcb=fcb, u=u, vln=vln, sv=sv, ya=ya, y_a=y_a, zc=zc, yb=yb, y_b=y_b, att=att,
                 o=o, y_c=y_c, merged=merged, y=y, x1=x1, h2=h2, r=r, y2=y2)
    return x2, saved


def layer_bwd(dx2, mod, sv, emit, tok_in=None):
    x, proj, w = sv["x"], sv["proj"], sv["w"]
    s, d = x.shape
    m = D_MIX
    sh1, sc1, gt1, sh2, sc2, gt2 = (mod[i:i + 1] for i in range(6))
    cb = 3 * d // m
    g = {}
    if tok_in is not None:
        gt2 = gt2 + tok_in[0:1, 0:1]
    dy2, g["mlp_post_g"], dgt2 = rowwise_vjp("post2_b", _f_post, [(sv["y2"], 0, d)], [w["mlp_post_g"], gt2], [(dx2, 0, d)], [BF16])
    da = matmul("w2_dx", dy2, w["mlp_w2"], tb=True, out_dtypes=(BF16,),
                epilogue=lambda acc, r: (acc * (2.0 * jnp.sqrt(r.astype(F32))),), epi=[(sv["r"], 0)])
    big = {}
    big["mlp_w2"] = matmul("w2_dw", sv["r"], dy2, ta=True, out_dtypes=(BF16,))
    dh2 = matmul("w1_dx", da, w["mlp_w1"], tb=True, out_dtypes=(BF16,))
    big["mlp_w1"] = matmul("w1_dw", sv["h2"], da, ta=True, out_dtypes=(BF16,), out_blocks=N_DEV)
    tok = emit("mlp", big)
    dx1, g["mlp_pre_g"], dsc2, dsh2 = _pre_bwd("pre2_b", sv["x1"], w["mlp_pre_g"], sc2 + tok[0:1, 0:1], sh2, dh2, dx2)
    dy, g["mix_post_g"], dgt1 = rowwise_vjp("post1_b", _f_post, [(sv["y"], 0, d)], [w["mix_post_g"], gt1], [(dx1, 0, d)], [BF16])
    dmerged = matmul("w_out_dx", dy, w["w_out"], tb=True, out_dtypes=(BF16,))
    big = {}
    big["w_out"] = matmul("w_out_dw", sv["merged"], dy, ta=True, out_dtypes=(BF16,))
    assert (3 * d) % (2 * m) == 0
    dproj = lax.empty((s, 3 * d + 8 * m), BF16)
    dproj, dya_, dyb_, dyc_ = rowwise_vjp(
        "merge_b", _f_merge, [(proj, 0, d), (proj, 1, d), (proj, 2, d), (sv["y_a"], 0, d), (sv["y_b"], 0, d), (sv["y_c"], 0, d)], [],
        [(dmerged, 0, d)], [BF16] * 6, into=(dproj, 0, 3))
    dya_pre = matmul("w_a_dx", dya_, w["w_a_out"], tb=True, out_dtypes=(BF16,))
    big["w_a_out"] = matmul("w_a_dw", sv["ya"], dya_, ta=True, out_dtypes=(BF16,))
    dyb_pre = matmul("w_b_dx", dyb_, w["w_b_out"], tb=True, out_dtypes=(BF16,))
    big["w_b_out"] = matmul("w_b_dw", sv["yb"], dyb_, ta=True, out_dtypes=(BF16,))
    do = matmul("w_c_dx", dyc_, w["w_c_out"], tb=True, out_dtypes=(BF16,))
    big["w_c_out"] = matmul("w_c_dw", sv["o"], dyc_, ta=True, out_dtypes=(BF16,))
    tok = emit("abco", big)
    qbe, qbo, ke, ko, ve, vo = sv["att"]
    lane0 = (cb + 4) * (m // LANES)
    dobe, dobo = attn_dsum(qbe, qbo, ke, ko, ve, vo, do, tok)
    dproj, dv, dcum, dq = attn_dkv(ke, ko, ve, vo, qbe, qbo, dobe, dobo, dproj, lane0 + N_PAIR)
    (dproj,) = rowwise("dq_scale", lambda g: g * (1.0 / math.sqrt(HEAD_DIM)), [(dq, 0, m)], [], [(m, BF16)], into=(dproj, cb + 4))
    dproj = lax.dynamic_update_slice(dproj, dv, (0, (cb + 6) * m))
    dproj, dbf = forget_bwd(sv["fproj"], sv["fcb"], w["bf_exp"], dcum, dproj, cb + 7)
    g["fox_bf"] = dbf[0, ::HEAD_DIM]
    dproj, dwc, g["conv_b"], g["conv_ln_g"], g["conv_ln_b"] = conv_bwd(
        proj, cb + 2, cb + 3, sv["zc"], dyb_pre, w["conv_w"], w["conv_ln_g"], w["conv_ln_b"], dproj)
    g["conv_w"] = dwc[:CONV_WIDTH]
    du, dvln, dws, dbexp = spatial_bwd(dya_pre, sv["u"], sv["sv"], sv["vln"], w["ws_t"])
    g["gmlp_ws"] = dws * jnp.tril(jnp.ones((CHUNK, CHUNK), F32))
    g["gmlp_bs"] = dbexp.reshape(CHUNK, m // GROUP_DIM, GROUP_DIM).sum(-1).T
    dproj, g["gmlp_ln_g"], g["gmlp_ln_b"] = rowwise_vjp(
        "gmlp_in_b", _f_a1, [(proj, cb, m), (proj, cb + 1, m)], [w["gmlp_ln_g"], w["gmlp_ln_b"]], [(du, 0, m), (dvln, 0, m)], [BF16, BF16],
        into=(dproj, cb // 2, 2))
    tok = emit("in", {"w_in": matmul("w_in_dw", sv["h"], dproj, ta=True, out_dtypes=(BF16,))})
    dh = matmul("w_in_dx", dproj, w["w_in"], tb=True, dep=tok, out_dtypes=(BF16,))
    dx, g["mix_pre_g"], dsc1, dsh1 = _pre_bwd("pre1_b", x, w["mix_pre_g"], sc1, sh1, dh, dx1)
    dmod = jnp.concatenate([dsh1, dsc1, dgt1, dsh2, dsc2, dgt2], axis=0)
    return dx, dmod, g


def local_step(x, target, mods, layers):
    d = x.shape[1]
    saved = []
    for l in range(len(layers)):
        x, sv = layer_fwd(x, mods[l], layers[l])
        saved.append(sv)

    def loss_fn(xv, tv):
        err = xv - tv
        return err * (1.0 / d), jnp.sum(err * err, axis=0, keepdims=True)

    dx, sq = rowwise("loss", loss_fn, [(x, 0, d), (target, 0, d)], [], [(d, F32)], accs=[(1, d)])
    loss = (0.5 / d) * jnp.sum(sq)
    dmods, grads = [None] * len(layers), [None] * len(layers)
    tok = None
    for l in reversed(range(len(layers))):
        dx, dmods[l], grads[l] = layer_bwd(dx, mods[l], saved[l], layers[l]["emit"], tok)
        tok = layers[l]["emit_small"](dmods[l], grads[l])
    return loss, dx, dmods, grads


def exchange(name, arrs, scatter):
    n = len(arrs)

    def body(*refs):
        in_refs, out_refs = refs[:n], refs[n:2 * n]
        send_sems, recv_sems, local_sems = refs[2 * n:]
        x, y, c = lax.axis_index("x"), lax.axis_index("y"), lax.axis_index("c")
        me = 4 * x + 2 * y + c
        local = []
        for a in range(n):
            src = in_refs[a].at[me] if scatter else in_refs[a]
            cp = pltpu.make_async_copy(src, out_refs[a].at[me], local_sems.at[a])
            cp.start()
            local.append(cp)
        remote = []
        for k in range(1, N_DEV):
            px, py, pc = x ^ ((k >> 2) & 1), y ^ ((k >> 1) & 1), c ^ (k & 1)
            peer = 4 * px + 2 * py + pc
            for a in range(n):
                src = in_refs[a].at[peer] if scatter else in_refs[a]
                cp = pltpu.make_async_remote_copy(
                    src_ref=src, dst_ref=out_refs[a].at[me], send_sem=send_sems.at[a * (N_DEV - 1) + k - 1],
                    recv_sem=recv_sems.at[a * (N_DEV - 1) + k - 1], device_id=(px, py, pc), device_id_type=MESH)
                cp.start()
                remote.append(cp)
        for cp in remote:
            cp.wait()
        for cp in local:
            cp.wait()

    hbm = pl.BlockSpec(memory_space=pltpu.HBM)
    out_shape = [jax.ShapeDtypeStruct(a.shape if scatter else (N_DEV,) + a.shape, a.dtype) for a in arrs]
    return pl.pallas_call(
        body, name=name, in_specs=[hbm] * n, out_specs=[hbm] * n, out_shape=out_shape,
        scratch_shapes=[pltpu.SemaphoreType.DMA((n * (N_DEV - 1),)), pltpu.SemaphoreType.DMA((n * (N_DEV - 1),)),
                        pltpu.SemaphoreType.DMA((n,))],
    )(*arrs)


def _peers(x, y, c):
    out = []
    for k in range(1, N_DEV):
        px, py, pc = x ^ ((k >> 2) & 1), y ^ ((k >> 1) & 1), c ^ (k & 1)
        out.append((k - 1, (px, py, pc), 4 * px + 2 * py + pc))
    return out


def _exchange_copies(srcs, lands, send_sems, recv_sems, scatter):
    x, y, c = lax.axis_index("x"), lax.axis_index("y"), lax.axis_index("c")
    me = 4 * x + 2 * y + c
    copies = []
    for slot, pos, peer in _peers(x, y, c):
        for a, (src, land) in enumerate(zip(srcs, lands)):
            copies.append(pltpu.make_async_remote_copy(
                src_ref=src.at[peer] if scatter else src, dst_ref=land.at[me],
                send_sem=send_sems.at[a * (N_DEV - 1) + slot], recv_sem=recv_sems.at[a * (N_DEV - 1) + slot],
                device_id=pos, device_id_type=MESH))
    return me, copies


def exchange_start(name, arrs, scatter, after=None):
    n = len(arrs)
    lands = [lax.empty(a.shape if scatter else (N_DEV,) + a.shape, a.dtype) for a in arrs]
    n_in = 2 * n + (after is not None)

    def body(*refs):
        srcs, lands_ = refs[:n], refs[n:2 * n]
        send_sems, recv_sems = refs[n_in], refs[n_in + 1]
        token = refs[n_in + 2 + 2 * n]
        _, copies = _exchange_copies(srcs, lands_, send_sems, recv_sems, scatter)
        for cp in copies:
            cp.start()
        token[...] = jnp.zeros(token.shape, token.dtype)

    hbm = pl.BlockSpec(memory_space=pltpu.HBM)
    sem = pl.BlockSpec(memory_space=pltpu.SEMAPHORE)
    n_sem = n * (N_DEV - 1)
    res = pl.pallas_call(
        body, name=name,
        out_shape=(pltpu.SemaphoreType.DMA((n_sem,)), pltpu.SemaphoreType.DMA((n_sem,)),
                   *[pltpu.HBM(a.shape, a.dtype) for a in arrs], *[pltpu.HBM(l.shape, l.dtype) for l in lands],
                   jax.ShapeDtypeStruct((8, LANES), F32)),
        in_specs=[hbm] * (2 * n) + ([] if after is None else [pl.BlockSpec(memory_space=pl.ANY)]),
        out_specs=(sem, sem, *([hbm] * (2 * n)), pl.BlockSpec(memory_space=pltpu.VMEM)),
        input_output_aliases={i: 2 + i for i in range(2 * n)},
        compiler_params=pltpu.CompilerParams(has_side_effects=pltpu.SideEffectType.DATAFLOW_SIDE_EFFECTING),
    )(*[pltpu.with_memory_space_constraint(a, pltpu.HBM) for a in arrs],
      *[pltpu.with_memory_space_constraint(l, pltpu.HBM) for l in lands], *([] if after is None else [after]))
    return dict(n=n, scatter=scatter, send=res[0], recv=res[1], srcs=res[2:2 + n], lands=res[2 + n:2 + 2 * n], token=res[2 + 2 * n])


def exchange_wait(name, st, after):
    n, scatter = st["n"], st["scatter"]
    after = list(after) if isinstance(after, (list, tuple)) else [after]

    def body(*refs):
        srcs, lands_ = refs[:n], refs[n:2 * n]
        send_sems, recv_sems = refs[2 * n], refs[2 * n + 1]
        _, copies = _exchange_copies(srcs, lands_, send_sems, recv_sems, scatter)
        for cp in copies:
            cp.wait_send()
            cp.wait_recv()

    hbm = pl.BlockSpec(memory_space=pltpu.HBM)
    sem = pl.BlockSpec(memory_space=pltpu.SEMAPHORE)
    res = pl.pallas_call(
        body, name=name,
        out_shape=tuple(pltpu.HBM(a.shape, a.dtype) for a in (*st["srcs"], *st["lands"])),
        in_specs=[hbm] * (2 * n) + [sem, sem] + [pl.BlockSpec(memory_space=pl.ANY)] * len(after), out_specs=tuple([hbm] * (2 * n)),
        input_output_aliases={i: i for i in range(2 * n)},
        compiler_params=pltpu.CompilerParams(has_side_effects=pltpu.SideEffectType.DATAFLOW_SIDE_EFFECTING),
    )(*st["srcs"], *st["lands"], st["send"], st["recv"], *after)
    return list(res[:n]), list(res[n:])


def adamw_sum(name, parts, w, m, v, tr=256):
    nl = len(parts)
    k, r, c = parts[0].shape
    tr = _tile(r, tr, 16)
    c1 = 1.0 - ADAM_B1 ** ADAM_STEP
    c2 = 1.0 - ADAM_B2 ** ADAM_STEP

    def body(*refs):
        p_refs = refs[:nl]
        w_ref, m_ref, v_ref, g_ref, d_ref, nm_ref, nv_ref = refs[nl:]
        for l in range(nl):
            @pl.when(pl.program_id(0) == l)
            def _(p_ref=p_refs[l]):
                grad = p_ref[0].astype(F32)
                for j in range(1, k):
                    grad = grad + p_ref[j].astype(F32)
                new_m = ADAM_B1 * m_ref[...] + (1.0 - ADAM_B1) * grad
                new_v = ADAM_B2 * v_ref[...] + (1.0 - ADAM_B2) * (grad * grad)
                m_hat = new_m / c1
                v_hat = new_v / c2
                g_ref[...] = grad
                d_ref[...] = -ADAM_LR * (m_hat / (jnp.sqrt(v_hat) + ADAM_EPS) + ADAM_WD * w_ref[...])
                nm_ref[...] = new_m
                nv_ref[...] = new_v

    part = lambda l: pl.BlockSpec((k, tr, c), functools.partial(lambda ll, i, l: (0, jnp.where(ll == l, i, 0), 0), l=l))
    blk = pl.BlockSpec((None, tr, c), lambda ll, i: (ll, i, 0))
    return pl.pallas_call(
        body, name=name, grid=(nl, r // tr),
        in_specs=[part(l) for l in range(nl)] + [blk, blk, blk],
        out_specs=[blk] * 4, out_shape=[jax.ShapeDtypeStruct((nl, r, c), F32)] * 4,
        compiler_params=_params(("parallel", "parallel")),
    )(*parts, w, m, v)


def ada_fwd(c_all, ada_w):
    nl, d, n = ada_w.shape

    def body(c_ref, w_ref, o_ref):
        o_ref[...] = jnp.dot(_silu(c_ref[...]), w_ref[...], precision=HIGHEST, preferred_element_type=F32)

    return pl.pallas_call(
        body, name="ada_fwd", grid=(nl,),
        in_specs=[pl.BlockSpec((N_DEV, d), lambda l: (0, 0)), pl.BlockSpec((None, d, n), lambda l: (l, 0, 0))],
        out_specs=pl.BlockSpec((None, N_DEV, n), lambda l: (l, 0, 0)),
        out_shape=jax.ShapeDtypeStruct((nl, N_DEV, n), F32),
        compiler_params=_params(("parallel",)),
    )(c_all, ada_w)


def ada_bwd(c_all_t, dmod, td=256):
    d = c_all_t.shape[0]
    nl, _, n = dmod.shape
    td = _tile(d, td, 8)

    def body(c_ref, dm_ref, o_ref):
        ca = _silu(c_ref[...])
        acc = ca[:, 0:1] * dm_ref[0:1, :]
        for b in range(1, N_DEV):
            acc = acc + ca[:, b:b + 1] * dm_ref[b:b + 1, :]
        o_ref[...] = acc

    return pl.pallas_call(
        body, name="ada_bwd", grid=(nl, d // td),
        in_specs=[pl.BlockSpec((td, N_DEV), lambda l, i: (i, 0)), pl.BlockSpec((None, N_DEV, n), lambda l, i: (l, 0, 0))],
        out_specs=pl.BlockSpec((None, td, n), lambda l, i: (l, i, 0)),
        out_shape=jax.ShapeDtypeStruct((nl, d, n), F32),
        compiler_params=_params(("parallel", "parallel")),
    )(c_all_t, dmod)


ARG_NAMES = ["x", "c", "ada_w", "ada_b", "mix_pre_g", "mix_post_g", "mlp_pre_g", "mlp_post_g", "w_in", "gmlp_ln_g", "gmlp_ln_b",
             "gmlp_ws", "gmlp_bs", "w_a_out", "conv_w", "conv_b", "conv_ln_g", "conv_ln_b", "w_b_out", "fox_bf", "w_c_out",
             "w_out", "mlp_w1", "mlp_w2"]
WEIGHTS = ARG_NAMES[2:]
COL_SHARDED = ["w_in", "w_a_out", "w_b_out", "w_c_out", "mlp_w1"]
ROW_SHARDED = ["w_out", "mlp_w2"]
BIG = COL_SHARDED + ROW_SHARDED
GROUPS = {"in": ["w_in"], "abco": ["w_a_out", "w_b_out", "w_c_out", "w_out"], "mlp": ["mlp_w1", "mlp_w2"]}
SMALL = ["ada_b", "mix_pre_g", "mix_post_g", "mlp_pre_g", "mlp_post_g", "gmlp_ln_g", "gmlp_ln_b", "gmlp_ws", "gmlp_bs",
         "conv_b", "conv_ln_g", "conv_ln_b", "fox_bf"]
PACK_COLS = 512


def _ref_ranges(lo, hi, shard):
    out = []
    while lo < hi:
        j = lo // shard
        end = min(hi, (j + 1) * shard)
        out.append((j, lo - j * shard, end - j * shard))
        lo = end
    return out


def _w_in_to_kernel_layout(g, d):
    m = D_MIX
    nf = 7 * m
    shard = g.shape[2]
    cols = lambda lo, hi: [g[j, :, a:b] for j, a, b in _ref_ranges(lo, hi, shard)]
    forget = jnp.concatenate(cols(nf, nf + N_HEADS), axis=1)
    return jnp.concatenate(cols(nf + N_HEADS, nf + N_HEADS + 3 * d) + cols(0, nf) + [jnp.repeat(forget, HEAD_DIM, axis=1)], axis=1)


def _w_in_grad_blocks(gw, d):
    m = D_MIX
    nf = 7 * m
    n_ref = nf + N_HEADS + 3 * d
    shard = n_ref // N_DEV
    segs = [(0, nf, 3 * d, 1), (nf, nf + N_HEADS, 3 * d + nf, HEAD_DIM), (nf + N_HEADS, n_ref, 0, 1)]
    blocks = []
    for j in range(N_DEV):
        lo, hi = j * shard, (j + 1) * shard
        pieces = []
        for r0, r1, k0, stride in segs:
            a, b = max(lo, r0), min(hi, r1)
            if a < b:
                pieces.append(gw[:, k0 + (a - r0) * stride:k0 + (b - r0) * stride:stride])
        blocks.append(jnp.concatenate(pieces, axis=1) if len(pieces) > 1 else pieces[0])
    return jnp.stack(blocks)


def _pack(parts):
    flat = jnp.concatenate([p.reshape(-1).astype(F32) for p in parts])
    pad = (-flat.shape[0]) % (PACK_COLS * 8)
    return jnp.pad(flat, (0, pad)).reshape(-1, PACK_COLS)


def _unpack(packed, shapes):
    nl = packed.shape[0]
    flat, out, off = packed.reshape(nl, -1), [], 0
    for shp in shapes:
        n = math.prod(shp)
        out.append(flat[:, off:off + n].reshape((nl,) + tuple(shp)))
        off += n
    return out


def _layer_small(p, conv_full, l):
    wl = {}
    for k in ["mix_pre_g", "mix_post_g", "mlp_pre_g", "mlp_post_g", "gmlp_ln_g", "gmlp_ln_b", "conv_b", "conv_ln_g", "conv_ln_b"]:
        wl[k] = p[k][l][None, :]
    wm = p["gmlp_ws"][l] * jnp.tril(jnp.ones((CHUNK, CHUNK), F32))
    wl["ws"] = wm.astype(BF16)
    wl["ws_t"] = jnp.transpose(wm, (0, 2, 1)).astype(BF16)
    wl["bs_exp"] = jnp.repeat(p["gmlp_bs"][l].T, GROUP_DIM, axis=1)
    wl["bf_exp"] = jnp.repeat(p["fox_bf"][l], HEAD_DIM)[None, :]
    wl["conv_w"] = jnp.pad(conv_full[l], ((0, CONV_HALO - CONV_WIDTH), (0, 0)))
    return wl


def kernel(x, c, ada_w, ada_b, mix_pre_g, mix_post_g, mlp_pre_g, mlp_post_g, w_in, gmlp_ln_g, gmlp_ln_b, gmlp_ws, gmlp_bs, w_a_out, conv_w, conv_b, conv_ln_g, conv_ln_b, w_b_out, fox_bf, w_c_out, w_out, mlp_w1, mlp_w2, loss_target, m_ada_w, m_ada_b, m_mix_pre_g, m_mix_post_g, m_mlp_pre_g, m_mlp_post_g, m_w_in, m_gmlp_ln_g, m_gmlp_ln_b, m_gmlp_ws, m_gmlp_bs, m_w_a_out, m_conv_w, m_conv_b, m_conv_ln_g, m_conv_ln_b, m_w_b_out, m_fox_bf, m_w_c_out, m_w_out, m_mlp_w1, m_mlp_w2, v_ada_w, v_ada_b, v_mix_pre_g, v_mix_post_g, v_mlp_pre_g, v_mlp_post_g, v_w_in, v_gmlp_ln_g, v_gmlp_ln_b, v_gmlp_ws, v_gmlp_bs, v_w_a_out, v_conv_w, v_conv_b, v_conv_ln_g, v_conv_ln_b, v_w_b_out, v_fox_bf, v_w_c_out, v_w_out, v_mlp_w1, v_mlp_w2):
    args = (x, c, ada_w, ada_b, mix_pre_g, mix_post_g, mlp_pre_g, mlp_post_g, w_in, gmlp_ln_g, gmlp_ln_b, gmlp_ws, gmlp_bs, w_a_out,
            conv_w, conv_b, conv_ln_g, conv_ln_b, w_b_out, fox_bf, w_c_out, w_out, mlp_w1, mlp_w2)
    ms = (m_ada_w, m_ada_b, m_mix_pre_g, m_mix_post_g, m_mlp_pre_g, m_mlp_post_g, m_w_in, m_gmlp_ln_g, m_gmlp_ln_b, m_gmlp_ws, m_gmlp_bs,
          m_w_a_out, m_conv_w, m_conv_b, m_conv_ln_g, m_conv_ln_b, m_w_b_out, m_fox_bf, m_w_c_out, m_w_out, m_mlp_w1, m_mlp_w2)
    vs = (v_ada_w, v_ada_b, v_mix_pre_g, v_mix_post_g, v_mlp_pre_g, v_mlp_post_g, v_w_in, v_gmlp_ln_g, v_gmlp_ln_b, v_gmlp_ws, v_gmlp_bs,
          v_w_a_out, v_conv_w, v_conv_b, v_conv_ln_g, v_conv_ln_b, v_w_b_out, v_fox_bf, v_w_c_out, v_w_out, v_mlp_w1, v_mlp_w2)
    p = dict(zip(ARG_NAMES, args))
    mom = dict(zip(WEIGHTS, ms))
    var = dict(zip(WEIGHTS, vs))
    nl = ada_w.shape[0]
    s, d = x.shape[1], x.shape[2]
    me = 4 * lax.axis_index("x") + 2 * lax.axis_index("y") + lax.axis_index("c")

    c_all, conv_all = exchange("gather_c", [c, conv_w], scatter=False)
    c_all = c_all.reshape(N_DEV, d)
    n_ada = ada_w.shape[2]
    mod_parts = ada_fwd(c_all, ada_w)
    (mod_recv,) = exchange("scatter_mod", [jnp.transpose(mod_parts, (1, 0, 2))], scatter=True)
    conv_full = jnp.transpose(conv_all, (1, 2, 0, 3)).reshape(nl, CONV_WIDTH, D_MIX)

    def full_matrix(k, land, own):
        g = lax.dynamic_update_index_in_dim(land, own, me, 0)
        r, cc = own.shape
        if k == "w_in":
            return _w_in_to_kernel_layout(g, d)
        return jnp.transpose(g, (1, 0, 2)).reshape(r, N_DEV * cc) if k in COL_SHARDED else g.reshape(N_DEV * r, cc)

    started = [mod_recv]

    def fetch(l, tag):
        keys = GROUPS[tag]
        st = exchange_start(f"gather_{tag}{l}_start", [p[k][l].astype(BF16) for k in keys], scatter=False, after=started[-1])
        started.append(st["token"])

        def get(after):
            owns, lands = exchange_wait(f"gather_{tag}{l}_wait", st, after)
            return {k: full_matrix(k, land, o) for k, land, o in zip(keys, lands, owns)}

        return get

    getters = [{tag: fetch(l, tag) for tag in GROUPS} for l in range(nl)]
    mod = jnp.transpose(mod_recv, (1, 0, 2)).reshape(nl, N_DEV * n_ada) + ada_b + started[-1][0:1, 0:1]
    mods = [mod[l].reshape(6, d) for l in range(nl)]

    sent = {}

    def emitter(l):
        def emit(tag, grads_big):
            keys = GROUPS[tag]
            send = []
            for k in keys:
                gk = grads_big[k]
                if k == "w_in":
                    gk = _w_in_grad_blocks(gk, d)
                elif gk.ndim == 3:
                    pass
                elif k in COL_SHARDED:
                    r, cc = gk.shape[0], gk.shape[1] // N_DEV
                    gk = jnp.transpose(gk.reshape(r, N_DEV, cc), (1, 0, 2))
                else:
                    gk = gk.reshape(N_DEV, gk.shape[0] // N_DEV, gk.shape[1])
                send.append(gk.astype(BF16))
            sent[(l, tag)] = (keys, exchange_start(f"scatter_{tag}{l}_start", send, scatter=True))
            return sent[(l, tag)][1]["token"]

        return emit

    small_sent = {}

    def small_emitter(l):
        def emit_small(dmod_l, g):
            packed = _pack([dmod_l] + [g[k] for k in SMALL[1:]] + [g["conv_w"]])
            small_sent[l] = exchange_start(f"gather_small{l}_start", [packed], scatter=False)
            return small_sent[l]["token"]

        return emit_small

    small_shapes = [p[k].shape[1:] for k in SMALL] + [(CONV_WIDTH, D_MIX)]
    zeros_conv = jnp.zeros(small_shapes[-1], F32)
    packs = [jnp.stack([_pack([src[k][l] for k in SMALL] + [zeros_conv]) for l in range(nl)]) for src in (p, mom, var)]
    smalls = [_layer_small(p, conv_full, l) for l in range(nl)]
    row_major = {"w_in": tuple(lax.reduce_precision(src["w_in"], 8, 23) for src in (p, mom, var))}
    early = packs + [a for sm in smalls for a in sm.values()] + list(row_major["w_in"])
    first_in = getters[0]["in"]
    getters[0]["in"] = lambda after: first_in([after] + early)
    layers = [dict(small=smalls[l], emit_small=small_emitter(l), get_in=getters[l]["in"], get_abco=getters[l]["abco"], get_mlp=getters[l]["mlp"],
                   emit=emitter(l)) for l in range(nl)]
    loss_local, dx, dmods, grads = local_step(x[0], loss_target[0], mods, layers)
    loss = lax.psum(loss_local, ("x", "y", "c"))
    grad_x = dx[None]

    out = {k: [None] * 4 for k in WEIGHTS}

    def shard_update(name, parts, k):
        shp = p[k].shape
        flat = lambda a: a.reshape(nl, -1, shp[-1])
        wk, mk, vk = row_major.get(k, (p[k], mom[k], var[k]))
        res = adamw_sum(name, [pt.reshape((pt.shape[0],) + flat(wk).shape[1:]) for pt in parts], flat(wk), flat(mk), flat(vk))
        out[k] = [a.reshape(shp) for a in res]

    parts = {}

    def collect(tag, after):
        for l in reversed(range(nl)):
            keys, st = sent[(l, tag)]
            sends, lands = exchange_wait(f"scatter_{tag}{l}_wait", st, after)
            for k, land, sd in zip(keys, lands, sends):
                own = lax.dynamic_index_in_dim(sd, me, 0, keepdims=False)
                parts[(k, l)] = lax.dynamic_update_index_in_dim(land, own, me, 0)

    behind_bwd = [dx] + [small_sent[l]["token"] for l in range(nl)]
    collect("mlp", behind_bwd)
    collect("abco", behind_bwd)
    for k in BIG[1:]:
        shard_update("adamw_" + k, [parts[(k, l)] for l in range(nl)], k)
    collect("in", [out[k][0] for k in BIG[1:]])
    shard_update("adamw_w_in", [parts[("w_in", l)] for l in range(nl)], "w_in")

    small_all = []
    for l in range(nl):
        srcs, lands = exchange_wait(f"gather_small{l}_wait", small_sent[l], [out[k][0] for k in BIG])
        small_all.append(lax.dynamic_update_index_in_dim(lands[0], srcs[0], me, 0))
    small_out = [_unpack(o, small_shapes) for o in adamw_sum("adamw_small", small_all, *packs)]
    for i, k in enumerate(SMALL):
        for j in range(4):
            out[k][j] = small_out[j][i]

    n_conv = conv_w.shape[2]
    conv_grad = lax.dynamic_slice_in_dim(small_out[0][-1], me * n_conv, n_conv, axis=2)
    shard_update("adamw_conv_w", [conv_grad[l][None] for l in range(nl)], "conv_w")

    dmod_all = jnp.stack([small_all[l].reshape(N_DEV, -1)[:, :6 * d] for l in range(nl)])
    dmod_mine = lax.dynamic_slice_in_dim(dmod_all, me * n_ada, n_ada, axis=2)
    g_ada = ada_bwd(c_all.T, dmod_mine)
    shard_update("adamw_ada_w", [g_ada[l][None] for l in range(nl)], "ada_w")

    res = [loss, grad_x]
    for j in range(4):
        res += [out[k][j] for k in WEIGHTS]
    return tuple(res)
```

```python
import functools
import math

import jax
import jax.numpy as jnp
from jax import lax
from jax.experimental import pallas as pl
from jax.experimental.pallas import tpu as pltpu

F32 = jnp.float32
BF16 = jnp.bfloat16
MESH = pl.DeviceIdType.MESH
N_DEV = 8
NORM_EPS = 1e-6
D_MIX = 512
N_HEADS = 8
HEAD_DIM = 64
GROUP_DIM = 64
CHUNK = 128
CONV_WIDTH = 31
CONV_HALO = 32
LANES = 128
ADAM_LR, ADAM_B1, ADAM_B2, ADAM_EPS, ADAM_WD, ADAM_STEP = 0.001, 0.9, 0.999, 1e-08, 0.01, 10
VMEM_LIMIT = 56 * 1024 * 1024
HIGHEST = lax.Precision.HIGHEST


def _tile(dim, pref, mult=LANES):
    t = min(pref, dim)
    t -= t % mult
    while t >= mult:
        if dim % t == 0:
            return t
        t -= mult
    return dim


def _params(sem):
    return pltpu.CompilerParams(dimension_semantics=sem, vmem_limit_bytes=VMEM_LIMIT)


def rowwise(name, fn, rows, consts, outs, accs=(), ts=512, into=None):
    s = rows[0][0].shape[0]
    ts = min(ts, s)
    nr, nc, no, na = len(rows), len(consts), len(outs), len(accs)
    n_in = nr + nc + (into is not None)

    def body(*refs):
        vals = [r[...].astype(F32) for r in refs[:nr]] + [r[...] for r in refs[nr:nr + nc]]
        res = fn(*vals)
        if not isinstance(res, (tuple, list)):
            res = (res,)
        for r, v in zip(refs[n_in:n_in + no], res[:no]):
            r[...] = v.astype(r.dtype)
        if na:
            acc_refs = refs[n_in + no:]

            @pl.when(pl.program_id(0) == 0)
            def _():
                for r in acc_refs:
                    r[...] = jnp.zeros(r.shape, r.dtype)

            for r, v in zip(acc_refs, res[no:]):
                r[...] += v.astype(F32)

    in_specs = [pl.BlockSpec((ts, w), functools.partial(lambda i, cb: (i, cb), cb=cb)) for (_, cb, w) in rows]
    in_specs += [pl.BlockSpec(c.shape, lambda i: (0, 0)) for c in consts]
    out_specs = [pl.BlockSpec((ts, w), lambda i: (i, 0)) for (w, _) in outs]
    out_specs += [pl.BlockSpec(shp, lambda i: (0, 0)) for shp in accs]
    out_shape = [jax.ShapeDtypeStruct((s, w), dt) for (w, dt) in outs]
    out_shape += [jax.ShapeDtypeStruct(shp, F32) for shp in accs]
    extra, aliases = [], {}
    if into is not None:
        buf, cb_into = into
        assert buf.dtype == outs[0][1] and buf.shape[0] == s
        in_specs.append(pl.BlockSpec(memory_space=pl.ANY))
        out_specs[0] = pl.BlockSpec((ts, outs[0][0]), lambda i: (i, cb_into))
        out_shape[0] = jax.ShapeDtypeStruct(buf.shape, buf.dtype)
        extra, aliases = [buf], {nr + nc: 0}
    res = pl.pallas_call(
        body, name=name, grid=(s // ts,), in_specs=in_specs, out_specs=out_specs, out_shape=out_shape,
        input_output_aliases=aliases, compiler_params=_params(("arbitrary",) if na else ("parallel",)),
    )(*[a for (a, _, _) in rows], *consts, *extra)
    return res


def rowwise_vjp(name, f, rows, consts, cts, grad_dtypes, ts=512, into=None):
    nr, nc, nt = len(rows), len(consts), len(cts)
    keep = [i for i, dt in enumerate(grad_dtypes) if dt is not None]
    k_into = 0 if into is None else into[2]

    def g(*vals):
        rv = [v.astype(F32) for v in vals[:nr]]
        ctv = tuple(v.astype(F32) for v in vals[nr:nr + nt])
        cv = list(vals[nr + nt:])
        _, vjp = jax.vjp(lambda *a: tuple(f(*a)), *rv, *cv)
        grads = vjp(ctv)
        row_grads = [grads[i] for i in keep]
        if k_into:
            row_grads = [jnp.concatenate(row_grads[:k_into], axis=1)] + row_grads[k_into:]
        return tuple(row_grads) + tuple(grads[nr:])

    outs = [(rows[i][2], grad_dtypes[i]) for i in keep]
    if k_into:
        assert len({dt for _, dt in outs[:k_into]}) == 1
        outs = [(sum(w for w, _ in outs[:k_into]), outs[0][1])] + outs[k_into:]
    return rowwise(name, g, list(rows) + list(cts), consts, outs, accs=[c.shape for c in consts], ts=ts,
                   into=None if into is None else into[:2])


def matmul(name, a, b, *, ta=False, tb=False, out_dtypes=(F32,), epilogue=None, epi=(), tm=None, tn=None, tk=4096, dep=None, b_cols=None, out_blocks=None,
           f32_tail=False):
    m, k = (a.shape[1], a.shape[0]) if ta else a.shape
    n = b.shape[0] if tb else b.shape[1]
    b_col0 = 0
    if b_cols is not None:
        assert not tb
        b_col0, n = b_cols
    assert (b.shape[1] if tb else b.shape[0]) == k
    tk = _tile(k, tk)
    if tk > 1024:
        tm, tn = _tile(m, tm or 1024), _tile(n, tn or 1024)
    else:
        tm, tn = _tile(m, tm or 2048), _tile(n, tn or (1024 if m >= 2048 else 2048))
    if out_blocks is not None:
        tn = n // out_blocks
    assert b_col0 % tn == 0
    jb = b_col0 // tn
    nk = k // tk
    nj = n // tn
    assert not f32_tail or (nk == 1 and epilogue is None)
    ne, no = len(epi), len(out_dtypes)
    dims = (((0 if ta else 1,), (1 if tb else 0,)), ((), ()))

    def body(*refs):
        a_ref, b_ref = refs[0], refs[1]
        epi_refs = refs[2:2 + ne]
        n_in = 2 + ne + (dep is not None)
        out_refs = refs[n_in:n_in + no]
        part = lax.dot_general(a_ref[...].astype(BF16), b_ref[...].astype(BF16), dims, preferred_element_type=F32)

        def finish(acc):
            res = (acc,) if epilogue is None else epilogue(acc, *[r[...] for r in epi_refs])
            for r, v in zip(out_refs, res):
                r[...] = v.astype(r.dtype)

        if nk == 1:
            finish(part)
            if f32_tail:
                @pl.when(pl.program_id(1) == nj - 1)
                def _():
                    refs[n_in + no][...] = part
        else:
            acc_ref = refs[-1]
            kk = pl.program_id(2)

            @pl.when(kk == 0)
            def _():
                acc_ref[...] = part

            @pl.when(kk > 0)
            def _():
                acc_ref[...] += part

            @pl.when(kk == nk - 1)
            def _():
                finish(acc_ref[...])

    a_spec = pl.BlockSpec((tk, tm), lambda i, j, kk: (kk, i)) if ta else pl.BlockSpec((tm, tk), lambda i, j, kk: (i, kk))
    b_spec = pl.BlockSpec((tn, tk), lambda i, j, kk: (j, kk)) if tb else pl.BlockSpec((tk, tn), lambda i, j, kk: (kk, j + jb))
    epi_specs = []
    for (arr, col0) in epi:
        assert col0 % tn == 0
        epi_specs.append(pl.BlockSpec((tm, tn), functools.partial(lambda i, j, kk, c0: (i, j + c0), c0=col0 // tn)))
    res = pl.pallas_call(
        body, name=name, grid=(m // tm, n // tn, nk),
        in_specs=[a_spec, b_spec] + epi_specs + ([] if dep is None else [pl.BlockSpec(dep.shape, lambda i, j, kk: (0, 0))]),
        out_specs=[pl.BlockSpec((tm, tn), lambda i, j, kk: (i, j)) if out_blocks is None else
                   pl.BlockSpec((None, tm, tn), lambda i, j, kk: (j, i, 0)) for _ in out_dtypes]
        + ([pl.BlockSpec((tm, tn), lambda i, j, kk: (i, 0))] if f32_tail else []),
        out_shape=[jax.ShapeDtypeStruct((m, n) if out_blocks is None else (out_blocks, m, tn), dt) for dt in out_dtypes]
        + ([jax.ShapeDtypeStruct((m, tn), F32)] if f32_tail else []),
        scratch_shapes=[pltpu.VMEM((tm, tn), F32)] if nk > 1 else [],
        compiler_params=_params(("parallel", "arbitrary" if f32_tail else "parallel", "arbitrary")),
    )(a, b, *[arr for (arr, _) in epi], *([] if dep is None else [dep]))
    return res[0] if len(res) == 1 else res


def _rms(x, g):
    return x * lax.rsqrt(jnp.mean(x * x, axis=-1, keepdims=True) + NORM_EPS) * g


def _ln(x, g, b):
    mu = jnp.mean(x, axis=-1, keepdims=True)
    xc = x - mu
    var = jnp.mean(xc * xc, axis=-1, keepdims=True)
    return xc * lax.rsqrt(var + NORM_EPS) * g + b


def _gelu(x):
    return 0.5 * x * (1.0 + jnp.tanh(math.sqrt(2.0 / math.pi) * (x + 0.044715 * (x * x * x))))


def _sigmoid(x):
    return 1.0 / (1.0 + jnp.exp(-x))


def _silu(x):
    return x * _sigmoid(x)


def _log_sigmoid(x):
    return jnp.minimum(x, 0.0) - jnp.log(1.0 + jnp.exp(-jnp.abs(x)))


def _f_pre(x, g, sc, sh):
    return (_rms(x, g) * (1.0 + sc) + sh,)


def _f_post(y, g, gt):
    return (gt * _rms(y, g),)


def _f_a1(u_raw, v_raw, g, b):
    return _gelu(u_raw), _ln(_gelu(v_raw), g, b)


def _f_glu(val, gate):
    return (val * _sigmoid(gate),)


def _f_lnsilu(zc, g, b):
    return (_silu(_ln(zc, g, b)),)


def _f_merge(g0, g1, g2, ya, yb, yc):
    return (_sigmoid(g0) * ya + _sigmoid(g1) * yb + _sigmoid(g2) * yc,)


def _lane_lt64(shape):
    return lax.broadcasted_iota(jnp.int32, shape, 1) < HEAD_DIM


def spatial_fwd(vln, u, w_bf, b_exp, rows_per_step=512):
    s = vln.shape[0]
    tr = min(rows_per_step, s)

    def body(v_ref, u_ref, w_ref, b_ref, sv_ref, ya_ref):
        lo = _lane_lt64((CHUNK, LANES))
        for ch in range(tr // CHUNK):
            r0 = ch * CHUNK
            for p in range(D_MIX // LANES):
                vp = v_ref[r0:r0 + CHUNK, p * LANES:(p + 1) * LANES]
                o0 = jnp.dot(w_ref[2 * p], vp, preferred_element_type=F32)
                o1 = jnp.dot(w_ref[2 * p + 1], vp, preferred_element_type=F32)
                sv = jnp.where(lo, o0, o1) + b_ref[:, p * LANES:(p + 1) * LANES]
                sv_ref[r0:r0 + CHUNK, p * LANES:(p + 1) * LANES] = sv
                ya_ref[r0:r0 + CHUNK, p * LANES:(p + 1) * LANES] = (
                    u_ref[r0:r0 + CHUNK, p * LANES:(p + 1) * LANES] * sv).astype(BF16)

    row = pl.BlockSpec((tr, D_MIX), lambda i: (i, 0))
    return pl.pallas_call(
        body, name="spatial_fwd", grid=(s // tr,),
        in_specs=[row, row, pl.BlockSpec(w_bf.shape, lambda i: (0, 0, 0)), pl.BlockSpec(b_exp.shape, lambda i: (0, 0))],
        out_specs=[row, row],
        out_shape=[jax.ShapeDtypeStruct((s, D_MIX), F32), jax.ShapeDtypeStruct((s, D_MIX), BF16)],
        compiler_params=_params(("parallel",)),
    )(vln, u, w_bf, b_exp)


def spatial_bwd(dya, u, sv, vln, wt_bf, rows_per_step=512):
    s = vln.shape[0]
    tr = min(rows_per_step, s)
    ng = wt_bf.shape[0]

    def body(dya_ref, u_ref, sv_ref, v_ref, wt_ref, du_ref, dv_ref, dw_ref, db_ref):
        @pl.when(pl.program_id(0) == 0)
        def _():
            dw_ref[...] = jnp.zeros(dw_ref.shape, F32)
            db_ref[...] = jnp.zeros(db_ref.shape, F32)

        lo = _lane_lt64((CHUNK, LANES))
        for ch in range(tr // CHUNK):
            r0 = ch * CHUNK
            for p in range(D_MIX // LANES):
                cs = slice(p * LANES, (p + 1) * LANES)
                dya_p = dya_ref[r0:r0 + CHUNK, cs].astype(F32)
                du_ref[r0:r0 + CHUNK, cs] = dya_p * sv_ref[r0:r0 + CHUNK, cs]
                dsv = dya_p * u_ref[r0:r0 + CHUNK, cs]
                db_ref[:, cs] += dsv
                dsv0 = jnp.where(lo, dsv, 0.0).astype(BF16)
                dsv1 = jnp.where(lo, 0.0, dsv).astype(BF16)
                vp = v_ref[r0:r0 + CHUNK, cs]
                d0 = jnp.dot(wt_ref[2 * p], dsv0, preferred_element_type=F32)
                d1 = jnp.dot(wt_ref[2 * p + 1], dsv1, preferred_element_type=F32)
                dv_ref[r0:r0 + CHUNK, cs] = d0 + d1
                nt = (((1,), (1,)), ((), ()))
                dw_ref[2 * p] += lax.dot_general(dsv0, vp, nt, preferred_element_type=F32)
                dw_ref[2 * p + 1] += lax.dot_general(dsv1, vp, nt, preferred_element_type=F32)

    row = pl.BlockSpec((tr, D_MIX), lambda i: (i, 0))
    return pl.pallas_call(
        body, name="spatial_bwd", grid=(s // tr,),
        in_specs=[row, row, row, row, pl.BlockSpec(wt_bf.shape, lambda i: (0, 0, 0))],
        out_specs=[row, row, pl.BlockSpec((ng, CHUNK, CHUNK), lambda i: (0, 0, 0)), pl.BlockSpec((CHUNK, D_MIX), lambda i: (0, 0))],
        out_shape=[jax.ShapeDtypeStruct((s, D_MIX), F32), jax.ShapeDtypeStruct((s, D_MIX), F32),
                   jax.ShapeDtypeStruct((ng, CHUNK, CHUNK), F32), jax.ShapeDtypeStruct((CHUNK, D_MIX), F32)],
        compiler_params=_params(("arbitrary",)),
    )(dya, u, sv, vln, wt_bf)


def _windows(ref, first, count, ts):
    for r in range(8):
        ks = [k for k in range(count) if (first + k) % 8 == r]
        if ks:
            base = first + ks[0]
            blk = ref[base:base + ks[-1] - ks[0] + ts, :]
            for k in ks:
                yield k, blk[k - ks[0]:k - ks[0] + ts, :]


def conv_fwd(proj, cb_val, cb_gate, w_pad, cb, ln_g, ln_b, ts=256):
    s = proj.shape[0]
    ts = min(ts, s)
    per = ts // CONV_HALO

    def body(val_ref, gate_ref, pval_ref, pgate_ref, w_ref, cb_ref, g_ref, b_ref, zc_ref, yb_ref, ext_ref):
        i = pl.program_id(0)
        zprev = pval_ref[...].astype(F32) * _sigmoid(pgate_ref[...].astype(F32))
        ext_ref[0:CONV_HALO, :] = jnp.where(i > 0, zprev, 0.0)
        ext_ref[CONV_HALO:, :] = val_ref[...].astype(F32) * _sigmoid(gate_ref[...].astype(F32))
        acc = jnp.zeros((ts, D_MIX), F32)
        for j, win in _windows(ext_ref, CONV_HALO - (CONV_WIDTH - 1), CONV_WIDTH, ts):
            acc = acc + w_ref[j:j + 1, :] * win
        zc = acc + cb_ref[...]
        zc_ref[...] = zc
        yb_ref[...] = _f_lnsilu(zc, g_ref[...], b_ref[...])[0].astype(BF16)

    def cur(c):
        return pl.BlockSpec((ts, D_MIX), functools.partial(lambda i, c: (i, c), c=c))

    def prev(c):
        return pl.BlockSpec((CONV_HALO, D_MIX), functools.partial(lambda i, c: (jnp.maximum(i * per - 1, 0), c), c=c))

    const = lambda a: pl.BlockSpec(a.shape, lambda i: (0, 0))
    out = pl.BlockSpec((ts, D_MIX), lambda i: (i, 0))
    return pl.pallas_call(
        body, name="conv_fwd", grid=(s // ts,),
        in_specs=[cur(cb_val), cur(cb_gate), prev(cb_val), prev(cb_gate), const(w_pad), const(cb), const(ln_g), const(ln_b)],
        out_specs=[out, out],
        out_shape=[jax.ShapeDtypeStruct((s, D_MIX), F32), jax.ShapeDtypeStruct((s, D_MIX), BF16)],
        scratch_shapes=[pltpu.VMEM((CONV_HALO + ts, D_MIX), F32)],
        compiler_params=_params(("parallel",)),
    )(proj, proj, proj, proj, w_pad, cb, ln_g, ln_b)


def conv_bwd(proj, cb_val, cb_gate, zc, dyb, w_pad, ln_g, ln_b, dproj, ts=256):
    assert cb_gate == cb_val + 1 and cb_val % 2 == 0
    s = proj.shape[0]
    ts = min(ts, s)
    per = ts // CONV_HALO
    n_tiles = s // ts
    n_halo = s // CONV_HALO

    def body(val_ref, gate_ref, pval_ref, pgate_ref, zc_ref, dyb_ref, nzc_ref, ndyb_ref, w_ref, g_ref, b_ref, dproj_in,
             dvg_ref, dw_ref, dcb_ref, dg_ref, db_ref, zext_ref, dext_ref):
        i = pl.program_id(0)

        @pl.when(i == 0)
        def _():
            for r in (dw_ref, dcb_ref, dg_ref, db_ref):
                r[...] = jnp.zeros(r.shape, F32)

        g, b = g_ref[...], b_ref[...]
        _, vjp = jax.vjp(lambda z, gg, bb: _f_lnsilu(z, gg, bb)[0], zc_ref[...], g, b)
        dzc, dg, db = vjp(dyb_ref[...].astype(F32))
        dg_ref[...] += dg
        db_ref[...] += db
        dcb_ref[...] += jnp.sum(dzc, axis=0, keepdims=True)
        _, vjp_n = jax.vjp(lambda z: _f_lnsilu(z, g, b)[0], nzc_ref[...])
        (dzc_next,) = vjp_n(ndyb_ref[...].astype(F32))
        dext_ref[0:ts, :] = dzc
        dext_ref[ts:, :] = jnp.where(i < n_tiles - 1, dzc_next, 0.0)
        val, gate = val_ref[...].astype(F32), gate_ref[...].astype(F32)
        zprev = pval_ref[...].astype(F32) * _sigmoid(pgate_ref[...].astype(F32))
        zext_ref[0:CONV_HALO, :] = jnp.where(i > 0, zprev, 0.0)
        zext_ref[CONV_HALO:, :] = val * _sigmoid(gate)
        dz = jnp.zeros((ts, D_MIX), F32)
        for shift, win in _windows(dext_ref, 0, CONV_WIDTH, ts):
            j = CONV_WIDTH - 1 - shift
            dz = dz + w_ref[j:j + 1, :] * win
        for j, win in _windows(zext_ref, CONV_HALO - (CONV_WIDTH - 1), CONV_WIDTH, ts):
            dw_ref[j:j + 1, :] += jnp.sum(dzc * win, axis=0, keepdims=True)
        _, vjp_glu = jax.vjp(lambda a, c: _f_glu(a, c)[0], val, gate)
        dval, dgate = vjp_glu(dz)
        dvg_ref[:, :D_MIX] = dval.astype(BF16)
        dvg_ref[:, D_MIX:] = dgate.astype(BF16)

    def cur(c):
        return pl.BlockSpec((ts, D_MIX), functools.partial(lambda i, c: (i, c), c=c))

    def prev(c):
        return pl.BlockSpec((CONV_HALO, D_MIX), functools.partial(lambda i, c: (jnp.maximum(i * per - 1, 0), c), c=c))

    nxt = pl.BlockSpec((CONV_HALO, D_MIX), lambda i: (jnp.minimum((i + 1) * per, n_halo - 1), 0))
    const = lambda a: pl.BlockSpec(a.shape, lambda i: (0, 0))
    out = pl.BlockSpec((ts, D_MIX), lambda i: (i, 0))
    vec = pl.BlockSpec((1, D_MIX), lambda i: (0, 0))
    return pl.pallas_call(
        body, name="conv_bwd", grid=(n_tiles,),
        in_specs=[cur(cb_val), cur(cb_gate), prev(cb_val), prev(cb_gate), out, out, nxt, nxt, const(w_pad), const(ln_g), const(ln_b),
                  pl.BlockSpec(memory_space=pl.ANY)],
        out_specs=[pl.BlockSpec((ts, 2 * D_MIX), lambda i: (i, cb_val // 2)), pl.BlockSpec((CONV_HALO, D_MIX), lambda i: (0, 0)), vec, vec, vec],
        out_shape=[jax.ShapeDtypeStruct(dproj.shape, dproj.dtype),
                   jax.ShapeDtypeStruct((CONV_HALO, D_MIX), F32)] + [jax.ShapeDtypeStruct((1, D_MIX), F32)] * 3,
        scratch_shapes=[pltpu.VMEM((CONV_HALO + ts, D_MIX), F32), pltpu.VMEM((ts + CONV_HALO, D_MIX), F32)],
        input_output_aliases={11: 0}, compiler_params=_params(("arbitrary",)),
    )(proj, proj, proj, proj, zc, dyb, zc, dyb, w_pad, ln_g, ln_b, dproj)


def forget_cumsum(proj, cb_f, bf_exp, t=256):
    s = proj.shape[0]
    t = min(t, s)

    def body(f_ref, bf_ref, out_ref, carry_ref):
        @pl.when(pl.program_id(0) == 0)
        def _():
            carry_ref[...] = jnp.zeros(carry_ref.shape, F32)

        lf = _log_sigmoid(f_ref[...] + bf_ref[...])
        tri = (lax.broadcasted_iota(jnp.int32, (t, t), 1) <= lax.broadcasted_iota(jnp.int32, (t, t), 0)).astype(F32)
        c = jnp.dot(tri, lf, precision=HIGHEST, preferred_element_type=F32) + carry_ref[...]
        out_ref[...] = c
        carry_ref[...] = c[t - 1:t, :]

    return pl.pallas_call(
        body, name="forget_cumsum", grid=(s // t,),
        in_specs=[pl.BlockSpec((t, D_MIX), functools.partial(lambda i, c: (i, c), c=cb_f)), pl.BlockSpec((1, D_MIX), lambda i: (0, 0))],
        out_specs=pl.BlockSpec((t, D_MIX), lambda i: (i, 0)),
        out_shape=jax.ShapeDtypeStruct((s, D_MIX), F32),
        scratch_shapes=[pltpu.VMEM((1, D_MIX), F32)],
        compiler_params=_params(("arbitrary",)),
    )(proj, bf_exp)


def forget_bwd(proj, cb_f, bf_exp, dcum, dproj, cb_out, t=256):
    s = proj.shape[0]
    t = min(t, s)
    n = s // t

    def body(f_ref, bf_ref, dc_ref, dproj_in, df_ref, dbf_ref, carry_ref):
        @pl.when(pl.program_id(0) == 0)
        def _():
            carry_ref[...] = jnp.zeros(carry_ref.shape, F32)
            dbf_ref[...] = jnp.zeros(dbf_ref.shape, F32)

        tri = (lax.broadcasted_iota(jnp.int32, (t, t), 1) >= lax.broadcasted_iota(jnp.int32, (t, t), 0)).astype(F32)
        r = jnp.dot(tri, dc_ref[...], precision=HIGHEST, preferred_element_type=F32) + carry_ref[...]
        carry_ref[...] = r[0:1, :]
        df = r * _sigmoid(-(f_ref[...] + bf_ref[...]))
        dbf_ref[...] += jnp.sum(df, axis=0, keepdims=True)
        live = lax.broadcasted_iota(jnp.int32, (t, D_MIX), 1) % HEAD_DIM == 0
        df_ref[...] = jnp.where(live, df, 0.0).astype(BF16)

    return pl.pallas_call(
        body, name="forget_bwd", grid=(n,),
        in_specs=[pl.BlockSpec((t, D_MIX), functools.partial(lambda i, c: (n - 1 - i, c), c=cb_f)), pl.BlockSpec((1, D_MIX), lambda i: (0, 0)),
                  pl.BlockSpec((t, D_MIX), lambda i: (n - 1 - i, 0)), pl.BlockSpec(memory_space=pl.ANY)],
        out_specs=[pl.BlockSpec((t, D_MIX), lambda i: (n - 1 - i, cb_out)), pl.BlockSpec((1, D_MIX), lambda i: (0, 0))],
        out_shape=[jax.ShapeDtypeStruct(dproj.shape, dproj.dtype), jax.ShapeDtypeStruct((1, D_MIX), F32)],
        scratch_shapes=[pltpu.VMEM((1, D_MIX), F32)],
        input_output_aliases={3: 0}, compiler_params=_params(("arbitrary",)),
    )(proj, bf_exp, dcum, dproj)


NT = (((1,), (1,)), ((), ()))
LOG2E = math.log2(math.e)
N_PAIR = D_MIX // LANES


def _split3(x):
    hi = x.astype(BF16).astype(F32)
    mid = (x - hi).astype(BF16).astype(F32)
    return hi, mid, x - hi - mid


def _triple(li, first, vals):
    out = jnp.where(li == first, vals[0], 0.0)
    for i in (1, 2):
        out = jnp.where(li == first + i, vals[i], out)
    return out


def _lane_ids(shape):
    lane = lax.broadcasted_iota(jnp.int32, shape, 1)
    return lane, lane % HEAD_DIM, lane < HEAD_DIM


def attn_prep(proj, cb_q, cum, ts=512):
    s = proj.shape[0]
    ts = min(ts, s)
    scale = LOG2E / math.sqrt(HEAD_DIM)

    def body(q_ref, k_ref, v_ref, c_ref, qe_ref, qo_ref, ke_ref, ko_ref, ve_ref, vo_ref):
        _, li, lo = _lane_ids((ts, LANES))
        one3 = lambda first: ((li >= first) & (li < first + 3)).astype(F32)
        for p in range(N_PAIR):
            ps = slice(p * LANES, (p + 1) * LANES)
            c3 = _split3(pltpu.roll(c_ref[:, ps] * LOG2E, HEAD_DIM, axis=1))
            eq = _triple(li, 0, c3) + one3(3)
            ek = one3(0) - _triple(li, 3, c3) + one3(6)
            ev = one3(0)
            for src, even, odd, extra, mul in ((q_ref, qe_ref, qo_ref, eq, scale), (k_ref, ke_ref, ko_ref, ek, 1.0), (v_ref, ve_ref, vo_ref, ev, 1.0)):
                x = src[:, ps].astype(F32) * mul
                even[:, ps] = jnp.where(lo, x, extra).astype(BF16)
                odd[:, ps] = jnp.where(lo, extra, x).astype(BF16)

    col = lambda c: pl.BlockSpec((ts, D_MIX), functools.partial(lambda i, c: (i, c), c=c))
    out = pl.BlockSpec((ts, D_MIX), lambda i: (i, 0))
    return pl.pallas_call(
        body, name="attn_prep", grid=(s // ts,),
        in_specs=[col(cb_q), col(cb_q + 1), col(cb_q + 2), pl.BlockSpec((ts, D_MIX), lambda i: (i, 0))],
        out_specs=[out] * 6, out_shape=[jax.ShapeDtypeStruct((s, D_MIX), BF16)] * 6,
        compiler_params=_params(("parallel",)),
    )(proj, proj, proj, cum)


def _pair_specs(s, t):
    return pl.BlockSpec((t, LANES), lambda p, i: (i, p)), pl.BlockSpec((s, LANES), lambda p, i: (0, p))


def attn_fwd(qe, qo, ke, ko, ve, vo, tq=1024):
    s = qe.shape[0]
    tq = min(tq, s)

    def body(qe_ref, qo_ref, ke_ref, ko_ref, ve_ref, vo_ref, o_ref, qbe_ref, qbo_ref):
        qi = pl.program_id(1)
        qs, k_refs, v_refs = (qe_ref[...], qo_ref[...]), (ke_ref, ko_ref), (ve_ref, vo_ref)
        causal = lax.broadcasted_iota(jnp.int32, (tq, tq), 1) <= lax.broadcasted_iota(jnp.int32, (tq, tq), 0)

        def step(j, carry, diag):
            ks = pl.multiple_of(j * tq, tq)
            new = []
            for h in range(2):
                m, l, acc = carry[h]
                sc = lax.dot_general(qs[h], k_refs[h][pl.ds(ks, tq), :], NT, preferred_element_type=F32)
                if diag:
                    sc = jnp.where(causal, sc, -jnp.inf)
                m_new = jnp.maximum(m, jnp.max(sc, axis=1, keepdims=True))
                p = jnp.exp2(sc - m_new)
                alpha = jnp.exp2(m - m_new)
                l = alpha * l + jnp.sum(p, axis=1, keepdims=True)
                acc = alpha * acc + jnp.dot(p.astype(BF16), v_refs[h][pl.ds(ks, tq), :], preferred_element_type=F32)
                new.append((m_new, l, acc))
            return tuple(new)

        init = tuple((jnp.full((tq, 1), -jnp.inf, F32), jnp.zeros((tq, 1), F32), jnp.zeros((tq, LANES), F32)) for _ in range(2))
        carry = lax.fori_loop(0, qi, lambda j, c: step(j, c, False), init)
        (m0, l0, a0), (m1, l1, a1) = step(qi, carry, True)
        _, li, lo = _lane_ids((tq, LANES))
        o_ref[...] = jnp.where(lo, a0 / l0, a1 / l1).astype(o_ref.dtype)
        lse_lanes = (li >= 6) & (li < 9)
        for q, m, l, spare, out_ref in ((qs[0], m0, l0, ~lo, qbe_ref), (qs[1], m1, l1, lo, qbo_ref)):
            neg_lse = _triple(li, 6, _split3(-(m + jnp.log(l) * LOG2E)))
            out_ref[...] = jnp.where(spare & lse_lanes, neg_lse.astype(BF16), q)

    blk, full = _pair_specs(s, tq)
    return pl.pallas_call(
        body, name="attn_fwd", grid=(N_PAIR, s // tq),
        in_specs=[blk, blk, full, full, full, full],
        out_specs=[blk] * 3, out_shape=[jax.ShapeDtypeStruct((s, D_MIX), BF16)] * 3,
        compiler_params=_params(("parallel", "parallel")),
    )(qe, qo, ke, ko, ve, vo)


def attn_dsum(qbe, qbo, ke, ko, ve, vo, do, dep, tq=1024):
    s = qbe.shape[0]
    tq = min(tq, s)

    def body(qe_ref, qo_ref, ke_ref, ko_ref, ve_ref, vo_ref, do_ref, dep_ref, dobe_ref, dobo_ref):
        qi = pl.program_id(1)
        _, li, lo = _lane_ids((tq, LANES))
        do_ = do_ref[...]
        qs, k_refs, v_refs = (qe_ref[...], qo_ref[...]), (ke_ref, ko_ref), (ve_ref, vo_ref)
        dos = (jnp.where(lo, do_, 0), jnp.where(lo, 0, do_))
        causal = lax.broadcasted_iota(jnp.int32, (tq, tq), 1) <= lax.broadcasted_iota(jnp.int32, (tq, tq), 0)

        def step(j, carry, diag):
            ks = pl.multiple_of(j * tq, tq)
            new = []
            for h in range(2):
                sc = lax.dot_general(qs[h], k_refs[h][pl.ds(ks, tq), :], NT, preferred_element_type=F32)
                if diag:
                    sc = jnp.where(causal, sc, -jnp.inf)
                pdp = jnp.exp2(sc) * lax.dot_general(dos[h], v_refs[h][pl.ds(ks, tq), :], NT, preferred_element_type=F32)
                new.append(carry[h] + jnp.sum(pdp, axis=1, keepdims=True))
            return tuple(new)

        init = (jnp.zeros((tq, 1), F32), jnp.zeros((tq, 1), F32))
        carry = lax.fori_loop(0, qi, lambda j, c: step(j, c, False), init)
        s0, s1 = step(qi, carry, True)
        dobe_ref[...] = jnp.where(lo, do_, _triple(li, 0, _split3(-s0)).astype(BF16))
        dobo_ref[...] = jnp.where(lo, _triple(li, 0, _split3(-s1)).astype(BF16), do_)

    blk, full = _pair_specs(s, tq)
    return pl.pallas_call(
        body, name="attn_dsum", grid=(N_PAIR, s // tq),
        in_specs=[blk, blk, full, full, full, full, blk, pl.BlockSpec(dep.shape, lambda p, i: (0, 0))],
        out_specs=[blk, blk], out_shape=[jax.ShapeDtypeStruct((s, D_MIX), BF16)] * 2,
        compiler_params=_params(("parallel", "parallel")),
    )(qbe, qbo, ke, ko, ve, vo, do, dep)


def attn_dkv(ke, ko, ve, vo, qbe, qbo, dobe, dobo, dproj, lane_block, tk=1024):
    s = ke.shape[0]
    tk = min(tk, s)
    nq = s // tk
    tn = (((0,), (0,)), ((), ()))

    def body(ke_ref, ko_ref, ve_ref, vo_ref, qe_ref, qo_ref, de_ref, do_ref, dproj_in, dk_ref, dv_ref, dck_ref, dq_ref):
        kj = pl.program_id(1)
        lo = _lane_lt64((tk, LANES))
        ks_, vs_, q_refs, d_refs = (ke_ref[...], ko_ref[...]), (ve_ref[...], vo_ref[...]), (qe_ref, qo_ref), (de_ref, do_ref)
        causal = lax.broadcasted_iota(jnp.int32, (tk, tk), 0) <= lax.broadcasted_iota(jnp.int32, (tk, tk), 1)

        @pl.when(kj == 0)
        def _():
            dq_ref[...] = jnp.zeros(dq_ref.shape, F32)

        def step(i, carry, diag):
            qs = pl.multiple_of(i * tk, tk)
            new, dq_h = [], []
            for h in range(2):
                dk, dv, dck = carry[h]
                qblk = q_refs[h][pl.ds(qs, tk), :]
                dblk = d_refs[h][pl.ds(qs, tk), :]
                st = lax.dot_general(ks_[h], qblk, NT, preferred_element_type=F32)
                if diag:
                    st = jnp.where(causal, st, -jnp.inf)
                pt = jnp.exp2(st)
                dst = pt * lax.dot_general(vs_[h], dblk, NT, preferred_element_type=F32)
                dst_bf = dst.astype(BF16)
                dq_h.append(lax.dot_general(dst_bf, ks_[h], tn, preferred_element_type=F32))
                new.append((dk + jnp.dot(dst_bf, qblk, preferred_element_type=F32),
                            dv + jnp.dot(pt.astype(BF16), dblk, preferred_element_type=F32),
                            dck - jnp.sum(dst, axis=1, keepdims=True)))
            dq_ref[pl.ds(qs, tk), :] += jnp.where(lo, dq_h[0], dq_h[1])
            return tuple(new)

        init = tuple((jnp.zeros((tk, LANES), F32), jnp.zeros((tk, LANES), F32), jnp.zeros((tk, 1), F32)) for _ in range(2))
        carry = step(kj, init, True)
        (dk0, dv0, dc0), (dk1, dv1, dc1) = lax.fori_loop(kj + 1, nq, lambda i, c: step(i, c, False), carry)
        dk_ref[...] = (jnp.where(lo, dk0, dk1) * (1.0 / LOG2E)).astype(dk_ref.dtype)
        dv_ref[...] = jnp.where(lo, dv0, dv1).astype(dv_ref.dtype)
        dck_ref[...] = jnp.where(lo, dc0, dc1)

    blk, full = _pair_specs(s, tk)
    return pl.pallas_call(
        body, name="attn_dkv", grid=(N_PAIR, nq),
        in_specs=[blk, blk, blk, blk, full, full, full, full, pl.BlockSpec(memory_space=pl.ANY)],
        out_specs=[pl.BlockSpec((tk, LANES), lambda p, i: (i, lane_block + p)), blk, blk, full],
        out_shape=[jax.ShapeDtypeStruct(dproj.shape, dproj.dtype), jax.ShapeDtypeStruct((s, D_MIX), BF16), jax.ShapeDtypeStruct((s, D_MIX), F32),
                   jax.ShapeDtypeStruct((s, D_MIX), F32)],
        input_output_aliases={8: 0}, compiler_params=_params(("parallel", "arbitrary")),
    )(ke, ko, ve, vo, qbe, qbo, dobe, dobo, dproj)


def _pre_bwd(name, x, g, sc, sh, dh, dres):
    d = x.shape[1]

    def fn(xv, dhv, dresv, gv, scv, shv):
        _, vjp = jax.vjp(lambda *a: _f_pre(*a)[0], xv, gv, scv, shv)
        dx, dg, dsc, dsh = vjp(dhv.astype(F32))
        return dx + dresv, dg, dsc, dsh

    return rowwise(name, fn, [(x, 0, d), (dh, 0, d), (dres, 0, d)], [g, sc, sh], [(d, F32)], accs=[(1, d)] * 3)


def mix_out(proj, y_a, y_b, y_c, w_out, x, consts, ts=512):
    s, d = x.shape
    ts = min(ts, s)

    def body(g0_ref, g1_ref, g2_ref, ya_ref, yb_ref, yc_ref, w_ref, x_ref, gp_ref, gt_ref, g2n_ref, sc_ref, sh_ref,
             m_ref, y_ref, x1_ref, h2_ref):
        f = lambda r: r[...].astype(F32)
        merged = _f_merge(f(g0_ref), f(g1_ref), f(g2_ref), f(ya_ref), f(yb_ref), f(yc_ref))[0].astype(BF16)
        m_ref[...] = merged
        y = jnp.dot(merged, w_ref[...], preferred_element_type=F32).astype(BF16)
        y_ref[...] = y
        x1 = x_ref[...] + _f_post(y.astype(F32), gp_ref[...], gt_ref[...])[0]
        x1_ref[...] = x1
        h2_ref[...] = _f_pre(x1, g2n_ref[...], sc_ref[...], sh_ref[...])[0].astype(BF16)

    col = lambda c: pl.BlockSpec((ts, d), functools.partial(lambda i, c: (i, c), c=c))
    row = pl.BlockSpec((ts, d), lambda i: (i, 0))
    const = lambda a: pl.BlockSpec(a.shape, lambda i: (0, 0))
    return pl.pallas_call(
        body, name="mix_out", grid=(s // ts,),
        in_specs=[col(0), col(1), col(2), row, row, row, const(w_out), row] + [const(a) for a in consts],
        out_specs=[row] * 4,
        out_shape=[jax.ShapeDtypeStruct((s, d), BF16), jax.ShapeDtypeStruct((s, d), BF16), jax.ShapeDtypeStruct((s, d), F32),
                   jax.ShapeDtypeStruct((s, d), BF16)],
        compiler_params=_params(("parallel",)),
    )(proj, proj, proj, y_a, y_b, y_c, w_out, x, *consts)


def layer_fwd(x, mod, layer):
    s, d = x.shape
    m = D_MIX
    w = dict(layer["small"])
    sh1, sc1, gt1, sh2, sc2, gt2 = (mod[i:i + 1] for i in range(6))
    cb = 3 * d // m
    (h,) = rowwise("pre1", _f_pre, [(x, 0, d)], [w["mix_pre_g"], sc1, sh1], [(d, BF16)])
    w.update(layer["get_in"](h))
    proj, fproj = matmul("w_in", h, w["w_in"], out_dtypes=(BF16,), f32_tail=True, tn=2 * m)
    fcb = 1
    w.update(layer["get_abco"](proj))
    u, vln = rowwise("gmlp_in", _f_a1, [(proj, cb, m), (proj, cb + 1, m)], [w["gmlp_ln_g"], w["gmlp_ln_b"]], [(m, F32), (m, BF16)])
    sv, ya = spatial_fwd(vln, u, w["ws"], w["bs_exp"])
    y_a = matmul("w_a", ya, w["w_a_out"], out_dtypes=(BF16,))
    zc, yb = conv_fwd(proj, cb + 2, cb + 3, w["conv_w"], w["conv_b"], w["conv_ln_g"], w["conv_ln_b"])
    y_b = matmul("w_b", yb, w["w_b_out"], out_dtypes=(BF16,))
    cum = forget_cumsum(fproj, fcb, w["bf_exp"])
    kv_ops = attn_prep(proj, cb + 4, cum)
    o, qbe, qbo = attn_fwd(*kv_ops)
    att = (qbe, qbo) + tuple(kv_ops[2:])
    y_c = matmul("w_c", o, w["w_c_out"], out_dtypes=(BF16,))
    merged, y, x1, h2 = mix_out(proj, y_a, y_b, y_c, w["w_out"], x, [w["mix_post_g"], gt1, w["mlp_pre_g"], sc2, sh2])
    w.update(layer["get_mlp"](y))
    r = matmul("w1", h2, w["mlp_w1"], out_dtypes=(BF16,), epilogue=lambda acc: (jnp.square(jnp.maximum(acc, 0.0)),))
    y2 = matmul("w2", r, w["mlp_w2"], out_dtypes=(BF16,))
    (x2,) = rowwise("post2", lambda xv, yv, g, gt: xv + _f_post(yv, g, gt)[0], [(x1, 0, d), (y2, 0, d)], [w["mlp_post_g"], gt2], [(d, F32)])
    saved = dict(w=w, x=x, h=h, proj=proj, fproj=fproj, fcb=fcb, u=u, vln=vln, sv=sv, ya=ya, y_a=y_a, zc=zc, yb=yb, y_b=y_b, att=att,
                 o=o, y_c=y_c, merged=merged, y=y, x1=x1, h2=h2, r=r, y2=y2)
    return x2, saved


def layer_bwd(dx2, mod, sv, emit, tok_in=None):
    x, proj, w = sv["x"], sv["proj"], sv["w"]
    s, d = x.shape
    m = D_MIX
    sh1, sc1, gt1, sh2, sc2, gt2 = (mod[i:i + 1] for i in range(6))
    cb = 3 * d // m
    g = {}
    if tok_in is not None:
        gt2 = gt2 + tok_in[0:1, 0:1]
    dy2, g["mlp_post_g"], dgt2 = rowwise_vjp("post2_b", _f_post, [(sv["y2"], 0, d)], [w["mlp_post_g"], gt2], [(dx2, 0, d)], [BF16])
    da = matmul("w2_dx", dy2, w["mlp_w2"], tb=True, out_dtypes=(BF16,),
                epilogue=lambda acc, r: (acc * (2.0 * jnp.sqrt(r.astype(F32))),), epi=[(sv["r"], 0)])
    big = {}
    big["mlp_w2"] = matmul("w2_dw", sv["r"], dy2, ta=True, out_dtypes=(BF16,))
    dh2 = matmul("w1_dx", da, w["mlp_w1"], tb=True, out_dtypes=(BF16,))
    big["mlp_w1"] = matmul("w1_dw", sv["h2"], da, ta=True, out_dtypes=(BF16,), out_blocks=N_DEV)
    tok = emit("mlp", big)
    dx1, g["mlp_pre_g"], dsc2, dsh2 = _pre_bwd("pre2_b", sv["x1"], w["mlp_pre_g"], sc2 + tok[0:1, 0:1], sh2, dh2, dx2)
    dy, g["mix_post_g"], dgt1 = rowwise_vjp("post1_b", _f_post, [(sv["y"], 0, d)], [w["mix_post_g"], gt1], [(dx1, 0, d)], [BF16])
    dmerged = matmul("w_out_dx", dy, w["w_out"], tb=True, out_dtypes=(BF16,))
    big = {}
    big["w_out"] = matmul("w_out_dw", sv["merged"], dy, ta=True, out_dtypes=(BF16,))
    assert (3 * d) % (2 * m) == 0
    dproj = lax.empty((s, 3 * d + 8 * m), BF16)
    dproj, dya_, dyb_, dyc_ = rowwise_vjp(
        "merge_b", _f_merge, [(proj, 0, d), (proj, 1, d), (proj, 2, d), (sv["y_a"], 0, d), (sv["y_b"], 0, d), (sv["y_c"], 0, d)], [],
        [(dmerged, 0, d)], [BF16] * 6, into=(dproj, 0, 3))
    dya_pre = matmul("w_a_dx", dya_, w["w_a_out"], tb=True, out_dtypes=(BF16,))
    big["w_a_out"] = matmul("w_a_dw", sv["ya"], dya_, ta=True, out_dtypes=(BF16,))
    dyb_pre = matmul("w_b_dx", dyb_, w["w_b_out"], tb=True, out_dtypes=(BF16,))
    big["w_b_out"] = matmul("w_b_dw", sv["yb"], dyb_, ta=True, out_dtypes=(BF16,))
    do = matmul("w_c_dx", dyc_, w["w_c_out"], tb=True, out_dtypes=(BF16,))
    big["w_c_out"] = matmul("w_c_dw", sv["o"], dyc_, ta=True, out_dtypes=(BF16,))
    tok = emit("abco", big)
    qbe, qbo, ke, ko, ve, vo = sv["att"]
    lane0 = (cb + 4) * (m // LANES)
    dobe, dobo = attn_dsum(qbe, qbo, ke, ko, ve, vo, do, tok)
    dproj, dv, dcum, dq = attn_dkv(ke, ko, ve, vo, qbe, qbo, dobe, dobo, dproj, lane0 + N_PAIR)
    (dproj,) = rowwise("dq_scale", lambda g: g * (1.0 / math.sqrt(HEAD_DIM)), [(dq, 0, m)], [], [(m, BF16)], into=(dproj, cb + 4))
    dproj = lax.dynamic_update_slice(dproj, dv, (0, (cb + 6) * m))
    dproj, dbf = forget_bwd(sv["fproj"], sv["fcb"], w["bf_exp"], dcum, dproj, cb + 7)
    g["fox_bf"] = dbf[0, ::HEAD_DIM]
    dproj, dwc, g["conv_b"], g["conv_ln_g"], g["conv_ln_b"] = conv_bwd(
        proj, cb + 2, cb + 3, sv["zc"], dyb_pre, w["conv_w"], w["conv_ln_g"], w["conv_ln_b"], dproj)
    g["conv_w"] = dwc[:CONV_WIDTH]
    du, dvln, dws, dbexp = spatial_bwd(dya_pre, sv["u"], sv["sv"], sv["vln"], w["ws_t"])
    g["gmlp_ws"] = dws * jnp.tril(jnp.ones((CHUNK, CHUNK), F32))
    g["gmlp_bs"] = dbexp.reshape(CHUNK, m // GROUP_DIM, GROUP_DIM).sum(-1).T
    dproj, g["gmlp_ln_g"], g["gmlp_ln_b"] = rowwise_vjp(
        "gmlp_in_b", _f_a1, [(proj, cb, m), (proj, cb + 1, m)], [w["gmlp_ln_g"], w["gmlp_ln_b"]], [(du, 0, m), (dvln, 0, m)], [BF16, BF16],
        into=(dproj, cb // 2, 2))
    tok = emit("in", {"w_in": matmul("w_in_dw", sv["h"], dproj, ta=True, out_dtypes=(BF16,))})
    dh = matmul("w_in_dx", dproj, w["w_in"], tb=True, dep=tok, out_dtypes=(BF16,))
    dx, g["mix_pre_g"], dsc1, dsh1 = _pre_bwd("pre1_b", x, w["mix_pre_g"], sc1, sh1, dh, dx1)
    dmod = jnp.concatenate([dsh1, dsc1, dgt1, dsh2, dsc2, dgt2], axis=0)
    return dx, dmod, g


def local_step(x, target, mods, layers):
    d = x.shape[1]
    saved = []
    for l in range(len(layers)):
        x, sv = layer_fwd(x, mods[l], layers[l])
        saved.append(sv)

    def loss_fn(xv, tv):
        err = xv - tv
        return err * (1.0 / d), jnp.sum(err * err, axis=0, keepdims=True)

    dx, sq = rowwise("loss", loss_fn, [(x, 0, d), (target, 0, d)], [], [(d, F32)], accs=[(1, d)])
    loss = (0.5 / d) * jnp.sum(sq)
    dmods, grads = [None] * len(layers), [None] * len(layers)
    tok = None
    for l in reversed(range(len(layers))):
        dx, dmods[l], grads[l] = layer_bwd(dx, mods[l], saved[l], layers[l]["emit"], tok)
        tok = layers[l]["emit_small"](dmods[l], grads[l])
    return loss, dx, dmods, grads


def exchange(name, arrs, scatter):
    n = len(arrs)

    def body(*refs):
        in_refs, out_refs = refs[:n], refs[n:2 * n]
        send_sems, recv_sems, local_sems = refs[2 * n:]
        x, y, c = lax.axis_index("x"), lax.axis_index("y"), lax.axis_index("c")
        me = 4 * x + 2 * y + c
        local = []
        for a in range(n):
            src = in_refs[a].at[me] if scatter else in_refs[a]
            cp = pltpu.make_async_copy(src, out_refs[a].at[me], local_sems.at[a])
            cp.start()
            local.append(cp)
        remote = []
        for k in range(1, N_DEV):
            px, py, pc = x ^ ((k >> 2) & 1), y ^ ((k >> 1) & 1), c ^ (k & 1)
            peer = 4 * px + 2 * py + pc
            for a in range(n):
                src = in_refs[a].at[peer] if scatter else in_refs[a]
                cp = pltpu.make_async_remote_copy(
                    src_ref=src, dst_ref=out_refs[a].at[me], send_sem=send_sems.at[a * (N_DEV - 1) + k - 1],
                    recv_sem=recv_sems.at[a * (N_DEV - 1) + k - 1], device_id=(px, py, pc), device_id_type=MESH)
                cp.start()
                remote.append(cp)
        for cp in remote:
            cp.wait()
        for cp in local:
            cp.wait()

    hbm = pl.BlockSpec(memory_space=pltpu.HBM)
    out_shape = [jax.ShapeDtypeStruct(a.shape if scatter else (N_DEV,) + a.shape, a.dtype) for a in arrs]
    return pl.pallas_call(
        body, name=name, in_specs=[hbm] * n, out_specs=[hbm] * n, out_shape=out_shape,
        scratch_shapes=[pltpu.SemaphoreType.DMA((n * (N_DEV - 1),)), pltpu.SemaphoreType.DMA((n * (N_DEV - 1),)),
                        pltpu.SemaphoreType.DMA((n,))],
    )(*arrs)


def _peers(x, y, c):
    out = []
    for k in range(1, N_DEV):
        px, py, pc = x ^ ((k >> 2) & 1), y ^ ((k >> 1) & 1), c ^ (k & 1)
        out.append((k - 1, (px, py, pc), 4 * px + 2 * py + pc))
    return out


def _exchange_copies(srcs, lands, send_sems, recv_sems, scatter):
    x, y, c = lax.axis_index("x"), lax.axis_index("y"), lax.axis_index("c")
    me = 4 * x + 2 * y + c
    copies = []
    for slot, pos, peer in _peers(x, y, c):
        for a, (src, land) in enumerate(zip(srcs, lands)):
            copies.append(pltpu.make_async_remote_copy(
                src_ref=src.at[peer] if scatter else src, dst_ref=land.at[me],
                send_sem=send_sems.at[a * (N_DEV - 1) + slot], recv_sem=recv_sems.at[a * (N_DEV - 1) + slot],
                device_id=pos, device_id_type=MESH))
    return me, copies


def exchange_start(name, arrs, scatter, after=None):
    n = len(arrs)
    lands = [lax.empty(a.shape if scatter else (N_DEV,) + a.shape, a.dtype) for a in arrs]
    n_in = 2 * n + (after is not None)

    def body(*refs):
        srcs, lands_ = refs[:n], refs[n:2 * n]
        send_sems, recv_sems = refs[n_in], refs[n_in + 1]
        token = refs[n_in + 2 + 2 * n]
        _, copies = _exchange_copies(srcs, lands_, send_sems, recv_sems, scatter)
        for cp in copies:
            cp.start()
        token[...] = jnp.zeros(token.shape, token.dtype)

    hbm = pl.BlockSpec(memory_space=pltpu.HBM)
    sem = pl.BlockSpec(memory_space=pltpu.SEMAPHORE)
    n_sem = n * (N_DEV - 1)
    res = pl.pallas_call(
        body, name=name,
        out_shape=(pltpu.SemaphoreType.DMA((n_sem,)), pltpu.SemaphoreType.DMA((n_sem,)),
                   *[pltpu.HBM(a.shape, a.dtype) for a in arrs], *[pltpu.HBM(l.shape, l.dtype) for l in lands],
                   jax.ShapeDtypeStruct((8, LANES), F32)),
        in_specs=[hbm] * (2 * n) + ([] if after is None else [pl.BlockSpec(memory_space=pl.ANY)]),
        out_specs=(sem, sem, *([hbm] * (2 * n)), pl.BlockSpec(memory_space=pltpu.VMEM)),
        input_output_aliases={i: 2 + i for i in range(2 * n)},
        compiler_params=pltpu.CompilerParams(has_side_effects=pltpu.SideEffectType.DATAFLOW_SIDE_EFFECTING),
    )(*[pltpu.with_memory_space_constraint(a, pltpu.HBM) for a in arrs],
      *[pltpu.with_memory_space_constraint(l, pltpu.HBM) for l in lands], *([] if after is None else [after]))
    return dict(n=n, scatter=scatter, send=res[0], recv=res[1], srcs=res[2:2 + n], lands=res[2 + n:2 + 2 * n], token=res[2 + 2 * n])


def exchange_wait(name, st, after):
    n, scatter = st["n"], st["scatter"]
    after = list(after) if isinstance(after, (list, tuple)) else [after]

    def body(*refs):
        srcs, lands_ = refs[:n], refs[n:2 * n]
        send_sems, recv_sems = refs[2 * n], refs[2 * n + 1]
        _, copies = _exchange_copies(srcs, lands_, send_sems, recv_sems, scatter)
        for cp in copies:
            cp.wait_send()
            cp.wait_recv()

    hbm = pl.BlockSpec(memory_space=pltpu.HBM)
    sem = pl.BlockSpec(memory_space=pltpu.SEMAPHORE)
    res = pl.pallas_call(
        body, name=name,
        out_shape=tuple(pltpu.HBM(a.shape, a.dtype) for a in (*st["srcs"], *st["lands"])),
        in_specs=[hbm] * (2 * n) + [sem, sem] + [pl.BlockSpec(memory_space=pl.ANY)] * len(after), out_specs=tuple([hbm] * (2 * n)),
        input_output_aliases={i: i for i in range(2 * n)},
        compiler_params=pltpu.CompilerParams(has_side_effects=pltpu.SideEffectType.DATAFLOW_SIDE_EFFECTING),
    )(*st["srcs"], *st["lands"], st["send"], st["recv"], *after)
    return list(res[:n]), list(res[n:])


def adamw_sum(name, parts, w, m, v, tr=256):
    nl = len(parts)
    k, r, c = parts[0].shape
    tr = _tile(r, tr, 16)
    c1 = 1.0 - ADAM_B1 ** ADAM_STEP
    c2 = 1.0 - ADAM_B2 ** ADAM_STEP

    def body(*refs):
        p_refs = refs[:nl]
        w_ref, m_ref, v_ref, g_ref, d_ref, nm_ref, nv_ref = refs[nl:]
        for l in range(nl):
            @pl.when(pl.program_id(0) == l)
            def _(p_ref=p_refs[l]):
                grad = p_ref[0].astype(F32)
                for j in range(1, k):
                    grad = grad + p_ref[j].astype(F32)
                new_m = ADAM_B1 * m_ref[...] + (1.0 - ADAM_B1) * grad
                new_v = ADAM_B2 * v_ref[...] + (1.0 - ADAM_B2) * (grad * grad)
                m_hat = new_m / c1
                v_hat = new_v / c2
                g_ref[...] = grad
                d_ref[...] = -ADAM_LR * (m_hat / (jnp.sqrt(v_hat) + ADAM_EPS) + ADAM_WD * w_ref[...])
                nm_ref[...] = new_m
                nv_ref[...] = new_v

    part = lambda l: pl.BlockSpec((k, tr, c), functools.partial(lambda ll, i, l: (0, jnp.where(ll == l, i, 0), 0), l=l))
    blk = pl.BlockSpec((None, tr, c), lambda ll, i: (ll, i, 0))
    return pl.pallas_call(
        body, name=name, grid=(nl, r // tr),
        in_specs=[part(l) for l in range(nl)] + [blk, blk, blk],
        out_specs=[blk] * 4, out_shape=[jax.ShapeDtypeStruct((nl, r, c), F32)] * 4,
        compiler_params=_params(("parallel", "parallel")),
    )(*parts, w, m, v)


def ada_fwd(c_all, ada_w):
    nl, d, n = ada_w.shape

    def body(c_ref, w_ref, o_ref):
        o_ref[...] = jnp.dot(_silu(c_ref[...]), w_ref[...], precision=HIGHEST, preferred_element_type=F32)

    return pl.pallas_call(
        body, name="ada_fwd", grid=(nl,),
        in_specs=[pl.BlockSpec((N_DEV, d), lambda l: (0, 0)), pl.BlockSpec((None, d, n), lambda l: (l, 0, 0))],
        out_specs=pl.BlockSpec((None, N_DEV, n), lambda l: (l, 0, 0)),
        out_shape=jax.ShapeDtypeStruct((nl, N_DEV, n), F32),
        compiler_params=_params(("parallel",)),
    )(c_all, ada_w)


def ada_bwd(c_all_t, dmod, td=256):
    d = c_all_t.shape[0]
    nl, _, n = dmod.shape
    td = _tile(d, td, 8)

    def body(c_ref, dm_ref, o_ref):
        ca = _silu(c_ref[...])
        acc = ca[:, 0:1] * dm_ref[0:1, :]
        for b in range(1, N_DEV):
            acc = acc + ca[:, b:b + 1] * dm_ref[b:b + 1, :]
        o_ref[...] = acc

    return pl.pallas_call(
        body, name="ada_bwd", grid=(nl, d // td),
        in_specs=[pl.BlockSpec((td, N_DEV), lambda l, i: (i, 0)), pl.BlockSpec((None, N_DEV, n), lambda l, i: (l, 0, 0))],
        out_specs=pl.BlockSpec((None, td, n), lambda l, i: (l, i, 0)),
        out_shape=jax.ShapeDtypeStruct((nl, d, n), F32),
        compiler_params=_params(("parallel", "parallel")),
    )(c_all_t, dmod)


ARG_NAMES = ["x", "c", "ada_w", "ada_b", "mix_pre_g", "mix_post_g", "mlp_pre_g", "mlp_post_g", "w_in", "gmlp_ln_g", "gmlp_ln_b",
             "gmlp_ws", "gmlp_bs", "w_a_out", "conv_w", "conv_b", "conv_ln_g", "conv_ln_b", "w_b_out", "fox_bf", "w_c_out",
             "w_out", "mlp_w1", "mlp_w2"]
WEIGHTS = ARG_NAMES[2:]
COL_SHARDED = ["w_in", "w_a_out", "w_b_out", "w_c_out", "mlp_w1"]
ROW_SHARDED = ["w_out", "mlp_w2"]
BIG = COL_SHARDED + ROW_SHARDED
GROUPS = {"in": ["w_in"], "abco": ["w_a_out", "w_b_out", "w_c_out", "w_out"], "mlp": ["mlp_w1", "mlp_w2"]}
SMALL = ["ada_b", "mix_pre_g", "mix_post_g", "mlp_pre_g", "mlp_post_g", "gmlp_ln_g", "gmlp_ln_b", "gmlp_ws", "gmlp_bs",
         "conv_b", "conv_ln_g", "conv_ln_b", "fox_bf"]
PACK_COLS = 512


def _ref_ranges(lo, hi, shard):
    out = []
    while lo < hi:
        j = lo // shard
        end = min(hi, (j + 1) * shard)
        out.append((j, lo - j * shard, end - j * shard))
        lo = end
    return out


def _w_in_to_kernel_layout(g, d):
    m = D_MIX
    nf = 7 * m
    shard = g.shape[2]
    cols = lambda lo, hi: [g[j, :, a:b] for j, a, b in _ref_ranges(lo, hi, shard)]
    forget = jnp.concatenate(cols(nf, nf + N_HEADS), axis=1)
    return jnp.concatenate(cols(nf + N_HEADS, nf + N_HEADS + 3 * d) + cols(0, nf) + [jnp.repeat(forget, HEAD_DIM, axis=1)], axis=1)


def _w_in_grad_blocks(gw, d):
    m = D_MIX
    nf = 7 * m
    n_ref = nf + N_HEADS + 3 * d
    shard = n_ref // N_DEV
    segs = [(0, nf, 3 * d, 1), (nf, nf + N_HEADS, 3 * d + nf, HEAD_DIM), (nf + N_HEADS, n_ref, 0, 1)]
    blocks = []
    for j in range(N_DEV):
        lo, hi = j * shard, (j + 1) * shard
        pieces = []
        for r0, r1, k0, stride in segs:
            a, b = max(lo, r0), min(hi, r1)
            if a < b:
                pieces.append(gw[:, k0 + (a - r0) * stride:k0 + (b - r0) * stride:stride])
        blocks.append(jnp.concatenate(pieces, axis=1) if len(pieces) > 1 else pieces[0])
    return jnp.stack(blocks)


def _pack(parts):
    flat = jnp.concatenate([p.reshape(-1).astype(F32) for p in parts])
    pad = (-flat.shape[0]) % (PACK_COLS * 8)
    return jnp.pad(flat, (0, pad)).reshape(-1, PACK_COLS)


def _unpack(packed, shapes):
    nl = packed.shape[0]
    flat, out, off = packed.reshape(nl, -1), [], 0
    for shp in shapes:
        n = math.prod(shp)
        out.append(flat[:, off:off + n].reshape((nl,) + tuple(shp)))
        off += n
    return out


def _layer_small(p, conv_full, l):
    wl = {}
    for k in ["mix_pre_g", "mix_post_g", "mlp_pre_g", "mlp_post_g", "gmlp_ln_g", "gmlp_ln_b", "conv_b", "conv_ln_g", "conv_ln_b"]:
        wl[k] = p[k][l][None, :]
    wm = p["gmlp_ws"][l] * jnp.tril(jnp.ones((CHUNK, CHUNK), F32))
    wl["ws"] = wm.astype(BF16)
    wl["ws_t"] = jnp.transpose(wm, (0, 2, 1)).astype(BF16)
    wl["bs_exp"] = jnp.repeat(p["gmlp_bs"][l].T, GROUP_DIM, axis=1)
    wl["bf_exp"] = jnp.repeat(p["fox_bf"][l], HEAD_DIM)[None, :]
    wl["conv_w"] = jnp.pad(conv_full[l], ((0, CONV_HALO - CONV_WIDTH), (0, 0)))
    return wl


def kernel(x, c, ada_w, ada_b, mix_pre_g, mix_post_g, mlp_pre_g, mlp_post_g, w_in, gmlp_ln_g, gmlp_ln_b, gmlp_ws, gmlp_bs, w_a_out, conv_w, conv_b, conv_ln_g, conv_ln_b, w_b_out, fox_bf, w_c_out, w_out, mlp_w1, mlp_w2, loss_target, m_ada_w, m_ada_b, m_mix_pre_g, m_mix_post_g, m_mlp_pre_g, m_mlp_post_g, m_w_in, m_gmlp_ln_g, m_gmlp_ln_b, m_gmlp_ws, m_gmlp_bs, m_w_a_out, m_conv_w, m_conv_b, m_conv_ln_g, m_conv_ln_b, m_w_b_out, m_fox_bf, m_w_c_out, m_w_out, m_mlp_w1, m_mlp_w2, v_ada_w, v_ada_b, v_mix_pre_g, v_mix_post_g, v_mlp_pre_g, v_mlp_post_g, v_w_in, v_gmlp_ln_g, v_gmlp_ln_b, v_gmlp_ws, v_gmlp_bs, v_w_a_out, v_conv_w, v_conv_b, v_conv_ln_g, v_conv_ln_b, v_w_b_out, v_fox_bf, v_w_c_out, v_w_out, v_mlp_w1, v_mlp_w2):
    args = (x, c, ada_w, ada_b, mix_pre_g, mix_post_g, mlp_pre_g, mlp_post_g, w_in, gmlp_ln_g, gmlp_ln_b, gmlp_ws, gmlp_bs, w_a_out,
            conv_w, conv_b, conv_ln_g, conv_ln_b, w_b_out, fox_bf, w_c_out, w_out, mlp_w1, mlp_w2)
    ms = (m_ada_w, m_ada_b, m_mix_pre_g, m_mix_post_g, m_mlp_pre_g, m_mlp_post_g, m_w_in, m_gmlp_ln_g, m_gmlp_ln_b, m_gmlp_ws, m_gmlp_bs,
          m_w_a_out, m_conv_w, m_conv_b, m_conv_ln_g, m_conv_ln_b, m_w_b_out, m_fox_bf, m_w_c_out, m_w_out, m_mlp_w1, m_mlp_w2)
    vs = (v_ada_w, v_ada_b, v_mix_pre_g, v_mix_post_g, v_mlp_pre_g, v_mlp_post_g, v_w_in, v_gmlp_ln_g, v_gmlp_ln_b, v_gmlp_ws, v_gmlp_bs,
          v_w_a_out, v_conv_w, v_conv_b, v_conv_ln_g, v_conv_ln_b, v_w_b_out, v_fox_bf, v_w_c_out, v_w_out, v_mlp_w1, v_mlp_w2)
    p = dict(zip(ARG_NAMES, args))
    mom = dict(zip(WEIGHTS, ms))
    var = dict(zip(WEIGHTS, vs))
    nl = ada_w.shape[0]
    s, d = x.shape[1], x.shape[2]
    me = 4 * lax.axis_index("x") + 2 * lax.axis_index("y") + lax.axis_index("c")

    c_all, conv_all = exchange("gather_c", [c, conv_w], scatter=False)
    c_all = c_all.reshape(N_DEV, d)
    n_ada = ada_w.shape[2]
    mod_parts = ada_fwd(c_all, ada_w)
    (mod_recv,) = exchange("scatter_mod", [jnp.transpose(mod_parts, (1, 0, 2))], scatter=True)
    conv_full = jnp.transpose(conv_all, (1, 2, 0, 3)).reshape(nl, CONV_WIDTH, D_MIX)

    def full_matrix(k, land, own):
        g = lax.dynamic_update_index_in_dim(land, own, me, 0)
        r, cc = own.shape
        if k == "w_in":
            return _w_in_to_kernel_layout(g, d)
        return jnp.transpose(g, (1, 0, 2)).reshape(r, N_DEV * cc) if k in COL_SHARDED else g.reshape(N_DEV * r, cc)

    started = [mod_recv]

    def fetch(l, tag):
        keys = GROUPS[tag]
        st = exchange_start(f"gather_{tag}{l}_start", [p[k][l].astype(BF16) for k in keys], scatter=False, after=started[-1])
        started.append(st["token"])

        def get(after):
            owns, lands = exchange_wait(f"gather_{tag}{l}_wait", st, after)
            return {k: full_matrix(k, land, o) for k, land, o in zip(keys, lands, owns)}

        return get

    getters = [{tag: fetch(l, tag) for tag in GROUPS} for l in range(nl)]
    mod = jnp.transpose(mod_recv, (1, 0, 2)).reshape(nl, N_DEV * n_ada) + ada_b + started[-1][0:1, 0:1]
    mods = [mod[l].reshape(6, d) for l in range(nl)]

    sent = {}

    def emitter(l):
        def emit(tag, grads_big):
            keys = GROUPS[tag]
            send = []
            for k in keys:
                gk = grads_big[k]
                if k == "w_in":
                    gk = _w_in_grad_blocks(gk, d)
                elif gk.ndim == 3:
                    pass
                elif k in COL_SHARDED:
                    r, cc = gk.shape[0], gk.shape[1] // N_DEV
                    gk = jnp.transpose(gk.reshape(r, N_DEV, cc), (1, 0, 2))
                else:
                    gk = gk.reshape(N_DEV, gk.shape[0] // N_DEV, gk.shape[1])
                send.append(gk.astype(BF16))
            sent[(l, tag)] = (keys, exchange_start(f"scatter_{tag}{l}_start", send, scatter=True))
            return sent[(l, tag)][1]["token"]

        return emit

    small_sent = {}

    def small_emitter(l):
        def emit_small(dmod_l, g):
            packed = _pack([dmod_l] + [g[k] for k in SMALL[1:]] + [g["conv_w"]])
            small_sent[l] = exchange_start(f"gather_small{l}_start", [packed], scatter=False)
            return small_sent[l]["token"]

        return emit_small

    small_shapes = [p[k].shape[1:] for k in SMALL] + [(CONV_WIDTH, D_MIX)]
    zeros_conv = jnp.zeros(small_shapes[-1], F32)
    packs = [jnp.stack([_pack([src[k][l] for k in SMALL] + [zeros_conv]) for l in range(nl)]) for src in (p, mom, var)]
    smalls = [_layer_small(p, conv_full, l) for l in range(nl)]
    row_major = {"w_in": tuple(lax.reduce_precision(src["w_in"], 8, 23) for src in (p, mom, var))}
    early = packs + [a for sm in smalls for a in sm.values()] + list(row_major["w_in"])
    first_in = getters[0]["in"]
    getters[0]["in"] = lambda after: first_in([after] + early)
    layers = [dict(small=smalls[l], emit_small=small_emitter(l), get_in=getters[l]["in"], get_abco=getters[l]["abco"], get_mlp=getters[l]["mlp"],
                   emit=emitter(l)) for l in range(nl)]
    loss_local, dx, dmods, grads = local_step(x[0], loss_target[0], mods, layers)
    loss = lax.psum(loss_local, ("x", "y", "c"))
    grad_x = dx[None]

    out = {k: [None] * 4 for k in WEIGHTS}

    def shard_update(name, parts, k):
        shp = p[k].shape
        flat = lambda a: a.reshape(nl, -1, shp[-1])
        wk, mk, vk = row_major.get(k, (p[k], mom[k], var[k]))
        res = adamw_sum(name, [pt.reshape((pt.shape[0],) + flat(wk).shape[1:]) for pt in parts], flat(wk), flat(mk), flat(vk))
        out[k] = [a.reshape(shp) for a in res]

    parts = {}

    def collect(tag, after):
        for l in reversed(range(nl)):
            keys, st = sent[(l, tag)]
            sends, lands = exchange_wait(f"scatter_{tag}{l}_wait", st, after)
            for k, land, sd in zip(keys, lands, sends):
                own = lax.dynamic_index_in_dim(sd, me, 0, keepdims=False)
                parts[(k, l)] = lax.dynamic_update_index_in_dim(land, own, me, 0)

    behind_bwd = [dx] + [small_sent[l]["token"] for l in range(nl)]
    collect("mlp", behind_bwd)
    collect("abco", behind_bwd)
    for k in BIG[1:]:
        shard_update("adamw_" + k, [parts[(k, l)] for l in range(nl)], k)
    collect("in", [out[k][0] for k in BIG[1:]])
    shard_update("adamw_w_in", [parts[("w_in", l)] for l in range(nl)], "w_in")

    small_all = []
    for l in range(nl):
        srcs, lands = exchange_wait(f"gather_small{l}_wait", small_sent[l], [out[k][0] for k in BIG])
        small_all.append(lax.dynamic_update_index_in_dim(lands[0], srcs[0], me, 0))
    small_out = [_unpack(o, small_shapes) for o in adamw_sum("adamw_small", small_all, *packs)]
    for i, k in enumerate(SMALL):
        for j in range(4):
            out[k][j] = small_out[j][i]

    n_conv = conv_w.shape[2]
    conv_grad = lax.dynamic_slice_in_dim(small_out[0][-1], me * n_conv, n_conv, axis=2)
    shard_update("adamw_conv_w", [conv_grad[l][None] for l in range(nl)], "conv_w")

    dmod_all = jnp.stack([small_all[l].reshape(N_DEV, -1)[:, :6 * d] for l in range(nl)])
    dmod_mine = lax.dynamic_slice_in_dim(dmod_all, me * n_ada, n_ada, axis=2)
    g_ada = ada_bwd(c_all.T, dmod_mine)
    shard_update("adamw_ada_w", [g_ada[l][None] for l in range(nl)], "ada_w")

    res = [loss, grad_x]
    for j in range(4):
        res += [out[k][j] for k in WEIGHTS]
    return tuple(res)
```

```python
import functools
import math

import jax
import jax.numpy as jnp
from jax import lax
from jax.experimental import pallas as pl
from jax.experimental.pallas import tpu as pltpu

F32 = jnp.float32
BF16 = jnp.bfloat16
MESH = pl.DeviceIdType.MESH
N_DEV = 8
NORM_EPS = 1e-6
D_MIX = 512
N_HEADS = 8
HEAD_DIM = 64
GROUP_DIM = 64
CHUNK = 128
CONV_WIDTH = 31
CONV_HALO = 32
LANES = 128
ADAM_LR, ADAM_B1, ADAM_B2, ADAM_EPS, ADAM_WD, ADAM_STEP = 0.001, 0.9, 0.999, 1e-08, 0.01, 10
VMEM_LIMIT = 56 * 1024 * 1024
HIGHEST = lax.Precision.HIGHEST


def _tile(dim, pref, mult=LANES):
    t = min(pref, dim)
    t -= t % mult
    while t >= mult:
        if dim % t == 0:
            return t
        t -= mult
    return dim


def _params(sem):
    return pltpu.CompilerParams(dimension_semantics=sem, vmem_limit_bytes=VMEM_LIMIT)


def rowwise(name, fn, rows, consts, outs, accs=(), ts=512, into=None):
    s = rows[0][0].shape[0]
    ts = min(ts, s)
    nr, nc, no, na = len(rows), len(consts), len(outs), len(accs)
    n_in = nr + nc + (into is not None)

    def body(*refs):
        vals = [r[...].astype(F32) for r in refs[:nr]] + [r[...] for r in refs[nr:nr + nc]]
        res = fn(*vals)
        if not isinstance(res, (tuple, list)):
            res = (res,)
        for r, v in zip(refs[n_in:n_in + no], res[:no]):
            r[...] = v.astype(r.dtype)
        if na:
            acc_refs = refs[n_in + no:]

            @pl.when(pl.program_id(0) == 0)
            def _():
                for r in acc_refs:
                    r[...] = jnp.zeros(r.shape, r.dtype)

            for r, v in zip(acc_refs, res[no:]):
                r[...] += v.astype(F32)

    in_specs = [pl.BlockSpec((ts, w), functools.partial(lambda i, cb: (i, cb), cb=cb)) for (_, cb, w) in rows]
    in_specs += [pl.BlockSpec(c.shape, lambda i: (0, 0)) for c in consts]
    out_specs = [pl.BlockSpec((ts, w), lambda i: (i, 0)) for (w, _) in outs]
    out_specs += [pl.BlockSpec(shp, lambda i: (0, 0)) for shp in accs]
    out_shape = [jax.ShapeDtypeStruct((s, w), dt) for (w, dt) in outs]
    out_shape += [jax.ShapeDtypeStruct(shp, F32) for shp in accs]
    extra, aliases = [], {}
    if into is not None:
        buf, cb_into = into
        assert buf.dtype == outs[0][1] and buf.shape[0] == s
        in_specs.append(pl.BlockSpec(memory_space=pl.ANY))
        out_specs[0] = pl.BlockSpec((ts, outs[0][0]), lambda i: (i, cb_into))
        out_shape[0] = jax.ShapeDtypeStruct(buf.shape, buf.dtype)
        extra, aliases = [buf], {nr + nc: 0}
    res = pl.pallas_call(
        body, name=name, grid=(s // ts,), in_specs=in_specs, out_specs=out_specs, out_shape=out_shape,
        input_output_aliases=aliases, compiler_params=_params(("arbitrary",) if na else ("parallel",)),
    )(*[a for (a, _, _) in rows], *consts, *extra)
    return res


def rowwise_vjp(name, f, rows, consts, cts, grad_dtypes, ts=512, into=None):
    nr, nc, nt = len(rows), len(consts), len(cts)
    keep = [i for i, dt in enumerate(grad_dtypes) if dt is not None]
    k_into = 0 if into is None else into[2]

    def g(*vals):
        rv = [v.astype(F32) for v in vals[:nr]]
        ctv = tuple(v.astype(F32) for v in vals[nr:nr + nt])
        cv = list(vals[nr + nt:])
        _, vjp = jax.vjp(lambda *a: tuple(f(*a)), *rv, *cv)
        grads = vjp(ctv)
        row_grads = [grads[i] for i in keep]
        if k_into:
            row_grads = [jnp.concatenate(row_grads[:k_into], axis=1)] + row_grads[k_into:]
        return tuple(row_grads) + tuple(grads[nr:])

    outs = [(rows[i][2], grad_dtypes[i]) for i in keep]
    if k_into:
        assert len({dt for _, dt in outs[:k_into]}) == 1
        outs = [(sum(w for w, _ in outs[:k_into]), outs[0][1])] + outs[k_into:]
    return rowwise(name, g, list(rows) + list(cts), consts, outs, accs=[c.shape for c in consts], ts=ts,
                   into=None if into is None else into[:2])


def matmul(name, a, b, *, ta=False, tb=False, out_dtypes=(F32,), epilogue=None, epi=(), tm=None, tn=None, tk=4096, dep=None, b_cols=None, out_blocks=None,
           f32_tail=False):
    m, k = (a.shape[1], a.shape[0]) if ta else a.shape
    n = b.shape[0] if tb else b.shape[1]
    b_col0 = 0
    if b_cols is not None:
        assert not tb
        b_col0, n = b_cols
    assert (b.shape[1] if tb else b.shape[0]) == k
    tk = _tile(k, tk)
    if tk > 1024:
        tm, tn = _tile(m, tm or 1024), _tile(n, tn or 1024)
    else:
        tm, tn = _tile(m, tm or 2048), _tile(n, tn or (1024 if m >= 2048 else 2048))
    if out_blocks is not None:
        tn = n // out_blocks
    assert b_col0 % tn == 0
    jb = b_col0 // tn
    nk = k // tk
    nj = n // tn
    assert not f32_tail or (nk == 1 and epilogue is None)
    ne, no = len(epi), len(out_dtypes)
    dims = (((0 if ta else 1,), (1 if tb else 0,)), ((), ()))

    def body(*refs):
        a_ref, b_ref = refs[0], refs[1]
        epi_refs = refs[2:2 + ne]
        n_in = 2 + ne + (dep is not None)
        out_refs = refs[n_in:n_in + no]
        part = lax.dot_general(a_ref[...].astype(BF16), b_ref[...].astype(BF16), dims, preferred_element_type=F32)

        def finish(acc):
            res = (acc,) if epilogue is None else epilogue(acc, *[r[...] for r in epi_refs])
            for r, v in zip(out_refs, res):
                r[...] = v.astype(r.dtype)

        if nk == 1:
            finish(part)
            if f32_tail:
                @pl.when(pl.program_id(1) == nj - 1)
                def _():
                    refs[n_in + no][...] = part
        else:
            acc_ref = refs[-1]
            kk = pl.program_id(2)

            @pl.when(kk == 0)
            def _():
                acc_ref[...] = part

            @pl.when(kk > 0)
            def _():
                acc_ref[...] += part

            @pl.when(kk == nk - 1)
            def _():
                finish(acc_ref[...])

    a_spec = pl.BlockSpec((tk, tm), lambda i, j, kk: (kk, i)) if ta else pl.BlockSpec((tm, tk), lambda i, j, kk: (i, kk))
    b_spec = pl.BlockSpec((tn, tk), lambda i, j, kk: (j, kk)) if tb else pl.BlockSpec((tk, tn), lambda i, j, kk: (kk, j + jb))
    epi_specs = []
    for (arr, col0) in epi:
        assert col0 % tn == 0
        epi_specs.append(pl.BlockSpec((tm, tn), functools.partial(lambda i, j, kk, c0: (i, j + c0), c0=col0 // tn)))
    res = pl.pallas_call(
        body, name=name, grid=(m // tm, n // tn, nk),
        in_specs=[a_spec, b_spec] + epi_specs + ([] if dep is None else [pl.BlockSpec(dep.shape, lambda i, j, kk: (0, 0))]),
        out_specs=[pl.BlockSpec((tm, tn), lambda i, j, kk: (i, j)) if out_blocks is None else
                   pl.BlockSpec((None, tm, tn), lambda i, j, kk: (j, i, 0)) for _ in out_dtypes]
        + ([pl.BlockSpec((tm, tn), lambda i, j, kk: (i, 0))] if f32_tail else []),
        out_shape=[jax.ShapeDtypeStruct((m, n) if out_blocks is None else (out_blocks, m, tn), dt) for dt in out_dtypes]
        + ([jax.ShapeDtypeStruct((m, tn), F32)] if f32_tail else []),
        scratch_shapes=[pltpu.VMEM((tm, tn), F32)] if nk > 1 else [],
        compiler_params=_params(("parallel", "arbitrary" if f32_tail else "parallel", "arbitrary")),
    )(a, b, *[arr for (arr, _) in epi], *([] if dep is None else [dep]))
    return res[0] if len(res) == 1 else res


def _rms(x, g):
    return x * lax.rsqrt(jnp.mean(x * x, axis=-1, keepdims=True) + NORM_EPS) * g


def _ln(x, g, b):
    mu = jnp.mean(x, axis=-1, keepdims=True)
    xc = x - mu
    var = jnp.mean(xc * xc, axis=-1, keepdims=True)
    return xc * lax.rsqrt(var + NORM_EPS) * g + b


def _gelu(x):
    return 0.5 * x * (1.0 + jnp.tanh(math.sqrt(2.0 / math.pi) * (x + 0.044715 * (x * x * x))))


def _sigmoid(x):
    return 1.0 / (1.0 + jnp.exp(-x))


def _silu(x):
    return x * _sigmoid(x)


def _log_sigmoid(x):
    return jnp.minimum(x, 0.0) - jnp.log(1.0 + jnp.exp(-jnp.abs(x)))


def _f_pre(x, g, sc, sh):
    return (_rms(x, g) * (1.0 + sc) + sh,)


def _f_post(y, g, gt):
    return (gt * _rms(y, g),)


def _f_a1(u_raw, v_raw, g, b):
    return _gelu(u_raw), _ln(_gelu(v_raw), g, b)


def _f_glu(val, gate):
    return (val * _sigmoid(gate),)


def _f_lnsilu(zc, g, b):
    return (_silu(_ln(zc, g, b)),)


def _f_merge(g0, g1, g2, ya, yb, yc):
    return (_sigmoid(g0) * ya + _sigmoid(g1) * yb + _sigmoid(g2) * yc,)


def _lane_lt64(shape):
    return lax.broadcasted_iota(jnp.int32, shape, 1) < HEAD_DIM


def spatial_fwd(vln, u, w_bf, b_exp, rows_per_step=512):
    s = vln.shape[0]
    tr = min(rows_per_step, s)

    def body(v_ref, u_ref, w_ref, b_ref, sv_ref, ya_ref):
        lo = _lane_lt64((CHUNK, LANES))
        for ch in range(tr // CHUNK):
            r0 = ch * CHUNK
            for p in range(D_MIX // LANES):
                vp = v_ref[r0:r0 + CHUNK, p * LANES:(p + 1) * LANES]
                o0 = jnp.dot(w_ref[2 * p], vp, preferred_element_type=F32)
                o1 = jnp.dot(w_ref[2 * p + 1], vp, preferred_element_type=F32)
                sv = jnp.where(lo, o0, o1) + b_ref[:, p * LANES:(p + 1) * LANES]
                sv_ref[r0:r0 + CHUNK, p * LANES:(p + 1) * LANES] = sv
                ya_ref[r0:r0 + CHUNK, p * LANES:(p + 1) * LANES] = (
                    u_ref[r0:r0 + CHUNK, p * LANES:(p + 1) * LANES] * sv).astype(BF16)

    row = pl.BlockSpec((tr, D_MIX), lambda i: (i, 0))
    return pl.pallas_call(
        body, name="spatial_fwd", grid=(s // tr,),
        in_specs=[row, row, pl.BlockSpec(w_bf.shape, lambda i: (0, 0, 0)), pl.BlockSpec(b_exp.shape, lambda i: (0, 0))],
        out_specs=[row, row],
        out_shape=[jax.ShapeDtypeStruct((s, D_MIX), F32), jax.ShapeDtypeStruct((s, D_MIX), BF16)],
        compiler_params=_params(("parallel",)),
    )(vln, u, w_bf, b_exp)


def spatial_bwd(dya, u, sv, vln, wt_bf, rows_per_step=512):
    s = vln.shape[0]
    tr = min(rows_per_step, s)
    ng = wt_bf.shape[0]

    def body(dya_ref, u_ref, sv_ref, v_ref, wt_ref, du_ref, dv_ref, dw_ref, db_ref):
        @pl.when(pl.program_id(0) == 0)
        def _():
            dw_ref[...] = jnp.zeros(dw_ref.shape, F32)
            db_ref[...] = jnp.zeros(db_ref.shape, F32)

        lo = _lane_lt64((CHUNK, LANES))
        for ch in range(tr // CHUNK):
            r0 = ch * CHUNK
            for p in range(D_MIX // LANES):
                cs = slice(p * LANES, (p + 1) * LANES)
                dya_p = dya_ref[r0:r0 + CHUNK, cs].astype(F32)
                du_ref[r0:r0 + CHUNK, cs] = dya_p * sv_ref[r0:r0 + CHUNK, cs]
                dsv = dya_p * u_ref[r0:r0 + CHUNK, cs]
                db_ref[:, cs] += dsv
                dsv0 = jnp.where(lo, dsv, 0.0).astype(BF16)
                dsv1 = jnp.where(lo, 0.0, dsv).astype(BF16)
                vp = v_ref[r0:r0 + CHUNK, cs]
                d0 = jnp.dot(wt_ref[2 * p], dsv0, preferred_element_type=F32)
                d1 = jnp.dot(wt_ref[2 * p + 1], dsv1, preferred_element_type=F32)
                dv_ref[r0:r0 + CHUNK, cs] = d0 + d1
                nt = (((1,), (1,)), ((), ()))
                dw_ref[2 * p] += lax.dot_general(dsv0, vp, nt, preferred_element_type=F32)
                dw_ref[2 * p + 1] += lax.dot_general(dsv1, vp, nt, preferred_element_type=F32)

    row = pl.BlockSpec((tr, D_MIX), lambda i: (i, 0))
    return pl.pallas_call(
        body, name="spatial_bwd", grid=(s // tr,),
        in_specs=[row, row, row, row, pl.BlockSpec(wt_bf.shape, lambda i: (0, 0, 0))],
        out_specs=[row, row, pl.BlockSpec((ng, CHUNK, CHUNK), lambda i: (0, 0, 0)), pl.BlockSpec((CHUNK, D_MIX), lambda i: (0, 0))],
        out_shape=[jax.ShapeDtypeStruct((s, D_MIX), F32), jax.ShapeDtypeStruct((s, D_MIX), F32),
                   jax.ShapeDtypeStruct((ng, CHUNK, CHUNK), F32), jax.ShapeDtypeStruct((CHUNK, D_MIX), F32)],
        compiler_params=_params(("arbitrary",)),
    )(dya, u, sv, vln, wt_bf)


def _windows(ref, first, count, ts):
    for r in range(8):
        ks = [k for k in range(count) if (first + k) % 8 == r]
        if ks:
            base = first + ks[0]
            blk = ref[base:base + ks[-1] - ks[0] + ts, :]
            for k in ks:
                yield k, blk[k - ks[0]:k - ks[0] + ts, :]


def conv_fwd(proj, cb_val, cb_gate, w_pad, cb, ln_g, ln_b, ts=256):
    s = proj.shape[0]
    ts = min(ts, s)
    per = ts // CONV_HALO

    def body(val_ref, gate_ref, pval_ref, pgate_ref, w_ref, cb_ref, g_ref, b_ref, zc_ref, yb_ref, ext_ref):
        i = pl.program_id(0)
        zprev = pval_ref[...].astype(F32) * _sigmoid(pgate_ref[...].astype(F32))
        ext_ref[0:CONV_HALO, :] = jnp.where(i > 0, zprev, 0.0)
        ext_ref[CONV_HALO:, :] = val_ref[...].astype(F32) * _sigmoid(gate_ref[...].astype(F32))
        acc = jnp.zeros((ts, D_MIX), F32)
        for j, win in _windows(ext_ref, CONV_HALO - (CONV_WIDTH - 1), CONV_WIDTH, ts):
            acc = acc + w_ref[j:j + 1, :] * win
        zc = acc + cb_ref[...]
        zc_ref[...] = zc
        yb_ref[...] = _f_lnsilu(zc, g_ref[...], b_ref[...])[0].astype(BF16)

    def cur(c):
        return pl.BlockSpec((ts, D_MIX), functools.partial(lambda i, c: (i, c), c=c))

    def prev(c):
        return pl.BlockSpec((CONV_HALO, D_MIX), functools.partial(lambda i, c: (jnp.maximum(i * per - 1, 0), c), c=c))

    const = lambda a: pl.BlockSpec(a.shape, lambda i: (0, 0))
    out = pl.BlockSpec((ts, D_MIX), lambda i: (i, 0))
    return pl.pallas_call(
        body, name="conv_fwd", grid=(s // ts,),
        in_specs=[cur(cb_val), cur(cb_gate), prev(cb_val), prev(cb_gate), const(w_pad), const(cb), const(ln_g), const(ln_b)],
        out_specs=[out, out],
        out_shape=[jax.ShapeDtypeStruct((s, D_MIX), F32), jax.ShapeDtypeStruct((s, D_MIX), BF16)],
        scratch_shapes=[pltpu.VMEM((CONV_HALO + ts, D_MIX), F32)],
        compiler_params=_params(("parallel",)),
    )(proj, proj, proj, proj, w_pad, cb, ln_g, ln_b)


def conv_bwd(proj, cb_val, cb_gate, zc, dyb, w_pad, ln_g, ln_b, dproj, ts=256):
    assert cb_gate == cb_val + 1 and cb_val % 2 == 0
    s = proj.shape[0]
    ts = min(ts, s)
    per = ts // CONV_HALO
    n_tiles = s // ts
    n_halo = s // CONV_HALO

    def body(val_ref, gate_ref, pval_ref, pgate_ref, zc_ref, dyb_ref, nzc_ref, ndyb_ref, w_ref, g_ref, b_ref, dproj_in,
             dvg_ref, dw_ref, dcb_ref, dg_ref, db_ref, zext_ref, dext_ref):
        i = pl.program_id(0)

        @pl.when(i == 0)
        def _():
            for r in (dw_ref, dcb_ref, dg_ref, db_ref):
                r[...] = jnp.zeros(r.shape, F32)

        g, b = g_ref[...], b_ref[...]
        _, vjp = jax.vjp(lambda z, gg, bb: _f_lnsilu(z, gg, bb)[0], zc_ref[...], g, b)
        dzc, dg, db = vjp(dyb_ref[...].astype(F32))
        dg_ref[...] += dg
        db_ref[...] += db
        dcb_ref[...] += jnp.sum(dzc, axis=0, keepdims=True)
        _, vjp_n = jax.vjp(lambda z: _f_lnsilu(z, g, b)[0], nzc_ref[...])
        (dzc_next,) = vjp_n(ndyb_ref[...].astype(F32))
        dext_ref[0:ts, :] = dzc
        dext_ref[ts:, :] = jnp.where(i < n_tiles - 1, dzc_next, 0.0)
        val, gate = val_ref[...].astype(F32), gate_ref[...].astype(F32)
        zprev = pval_ref[...].astype(F32) * _sigmoid(pgate_ref[...].astype(F32))
        zext_ref[0:CONV_HALO, :] = jnp.where(i > 0, zprev, 0.0)
        zext_ref[CONV_HALO:, :] = val * _sigmoid(gate)
        dz = jnp.zeros((ts, D_MIX), F32)
        for shift, win in _windows(dext_ref, 0, CONV_WIDTH, ts):
            j = CONV_WIDTH - 1 - shift
            dz = dz + w_ref[j:j + 1, :] * win
        for j, win in _windows(zext_ref, CONV_HALO - (CONV_WIDTH - 1), CONV_WIDTH, ts):
            dw_ref[j:j + 1, :] += jnp.sum(dzc * win, axis=0, keepdims=True)
        _, vjp_glu = jax.vjp(lambda a, c: _f_glu(a, c)[0], val, gate)
        dval, dgate = vjp_glu(dz)
        dvg_ref[:, :D_MIX] = dval.astype(BF16)
        dvg_ref[:, D_MIX:] = dgate.astype(BF16)

    def cur(c):
        return pl.BlockSpec((ts, D_MIX), functools.partial(lambda i, c: (i, c), c=c))

    def prev(c):
        return pl.BlockSpec((CONV_HALO, D_MIX), functools.partial(lambda i, c: (jnp.maximum(i * per - 1, 0), c), c=c))

    nxt = pl.BlockSpec((CONV_HALO, D_MIX), lambda i: (jnp.minimum((i + 1) * per, n_halo - 1), 0))
    const = lambda a: pl.BlockSpec(a.shape, lambda i: (0, 0))
    out = pl.BlockSpec((ts, D_MIX), lambda i: (i, 0))
    vec = pl.BlockSpec((1, D_MIX), lambda i: (0, 0))
    return pl.pallas_call(
        body, name="conv_bwd", grid=(n_tiles,),
        in_specs=[cur(cb_val), cur(cb_gate), prev(cb_val), prev(cb_gate), out, out, nxt, nxt, const(w_pad), const(ln_g), const(ln_b),
                  pl.BlockSpec(memory_space=pl.ANY)],
        out_specs=[pl.BlockSpec((ts, 2 * D_MIX), lambda i: (i, cb_val // 2)), pl.BlockSpec((CONV_HALO, D_MIX), lambda i: (0, 0)), vec, vec, vec],
        out_shape=[jax.ShapeDtypeStruct(dproj.shape, dproj.dtype),
                   jax.ShapeDtypeStruct((CONV_HALO, D_MIX), F32)] + [jax.ShapeDtypeStruct((1, D_MIX), F32)] * 3,
        scratch_shapes=[pltpu.VMEM((CONV_HALO + ts, D_MIX), F32), pltpu.VMEM((ts + CONV_HALO, D_MIX), F32)],
        input_output_aliases={11: 0}, compiler_params=_params(("arbitrary",)),
    )(proj, proj, proj, proj, zc, dyb, zc, dyb, w_pad, ln_g, ln_b, dproj)


def forget_cumsum(proj, cb_f, bf_exp, t=256):
    s = proj.shape[0]
    t = min(t, s)

    def body(f_ref, bf_ref, out_ref, carry_ref):
        @pl.when(pl.program_id(0) == 0)
        def _():
            carry_ref[...] = jnp.zeros(carry_ref.shape, F32)

        lf = _log_sigmoid(f_ref[...] + bf_ref[...])
        tri = (lax.broadcasted_iota(jnp.int32, (t, t), 1) <= lax.broadcasted_iota(jnp.int32, (t, t), 0)).astype(F32)
        c = jnp.dot(tri, lf, precision=HIGHEST, preferred_element_type=F32) + carry_ref[...]
        out_ref[...] = c
        carry_ref[...] = c[t - 1:t, :]

    return pl.pallas_call(
        body, name="forget_cumsum", grid=(s // t,),
        in_specs=[pl.BlockSpec((t, D_MIX), functools.partial(lambda i, c: (i, c), c=cb_f)), pl.BlockSpec((1, D_MIX), lambda i: (0, 0))],
        out_specs=pl.BlockSpec((t, D_MIX), lambda i: (i, 0)),
        out_shape=jax.ShapeDtypeStruct((s, D_MIX), F32),
        scratch_shapes=[pltpu.VMEM((1, D_MIX), F32)],
        compiler_params=_params(("arbitrary",)),
    )(proj, bf_exp)


def forget_bwd(proj, cb_f, bf_exp, dcum, dproj, cb_out, t=256):
    s = proj.shape[0]
    t = min(t, s)
    n = s // t

    def body(f_ref, bf_ref, dc_ref, dproj_in, df_ref, dbf_ref, carry_ref):
        @pl.when(pl.program_id(0) == 0)
        def _():
            carry_ref[...] = jnp.zeros(carry_ref.shape, F32)
            dbf_ref[...] = jnp.zeros(dbf_ref.shape, F32)

        tri = (lax.broadcasted_iota(jnp.int32, (t, t), 1) >= lax.broadcasted_iota(jnp.int32, (t, t), 0)).astype(F32)
        r = jnp.dot(tri, dc_ref[...], precision=HIGHEST, preferred_element_type=F32) + carry_ref[...]
        carry_ref[...] = r[0:1, :]
        df = r * _sigmoid(-(f_ref[...] + bf_ref[...]))
        dbf_ref[...] += jnp.sum(df, axis=0, keepdims=True)
        live = lax.broadcasted_iota(jnp.int32, (t, D_MIX), 1) % HEAD_DIM == 0
        df_ref[...] = jnp.where(live, df, 0.0).astype(BF16)

    return pl.pallas_call(
        body, name="forget_bwd", grid=(n,),
        in_specs=[pl.BlockSpec((t, D_MIX), functools.partial(lambda i, c: (n - 1 - i, c), c=cb_f)), pl.BlockSpec((1, D_MIX), lambda i: (0, 0)),
                  pl.BlockSpec((t, D_MIX), lambda i: (n - 1 - i, 0)), pl.BlockSpec(memory_space=pl.ANY)],
        out_specs=[pl.BlockSpec((t, D_MIX), lambda i: (n - 1 - i, cb_out)), pl.BlockSpec((1, D_MIX), lambda i: (0, 0))],
        out_shape=[jax.ShapeDtypeStruct(dproj.shape, dproj.dtype), jax.ShapeDtypeStruct((1, D_MIX), F32)],
        scratch_shapes=[pltpu.VMEM((1, D_MIX), F32)],
        input_output_aliases={3: 0}, compiler_params=_params(("arbitrary",)),
    )(proj, bf_exp, dcum, dproj)


NT = (((1,), (1,)), ((), ()))
LOG2E = math.log2(math.e)
N_PAIR = D_MIX // LANES


def _split3(x):
    hi = x.astype(BF16).astype(F32)
    mid = (x - hi).astype(BF16).astype(F32)
    return hi, mid, x - hi - mid


def _triple(li, first, vals):
    out = jnp.where(li == first, vals[0], 0.0)
    for i in (1, 2):
        out = jnp.where(li == first + i, vals[i], out)
    return out


def _lane_ids(shape):
    lane = lax.broadcasted_iota(jnp.int32, shape, 1)
    return lane, lane % HEAD_DIM, lane < HEAD_DIM


def attn_prep(proj, cb_q, cum, ts=512):
    s = proj.shape[0]
    ts = min(ts, s)
    scale = LOG2E / math.sqrt(HEAD_DIM)

    def body(q_ref, k_ref, v_ref, c_ref, qe_ref, qo_ref, ke_ref, ko_ref, ve_ref, vo_ref):
        _, li, lo = _lane_ids((ts, LANES))
        one3 = lambda first: ((li >= first) & (li < first + 3)).astype(F32)
        for p in range(N_PAIR):
            ps = slice(p * LANES, (p + 1) * LANES)
            c3 = _split3(pltpu.roll(c_ref[:, ps] * LOG2E, HEAD_DIM, axis=1))
            eq = _triple(li, 0, c3) + one3(3)
            ek = one3(0) - _triple(li, 3, c3) + one3(6)
            ev = one3(0)
            for src, even, odd, extra, mul in ((q_ref, qe_ref, qo_ref, eq, scale), (k_ref, ke_ref, ko_ref, ek, 1.0), (v_ref, ve_ref, vo_ref, ev, 1.0)):
                x = src[:, ps].astype(F32) * mul
                even[:, ps] = jnp.where(lo, x, extra).astype(BF16)
                odd[:, ps] = jnp.where(lo, extra, x).astype(BF16)

    col = lambda c: pl.BlockSpec((ts, D_MIX), functools.partial(lambda i, c: (i, c), c=c))
    out = pl.BlockSpec((ts, D_MIX), lambda i: (i, 0))
    return pl.pallas_call(
        body, name="attn_prep", grid=(s // ts,),
        in_specs=[col(cb_q), col(cb_q + 1), col(cb_q + 2), pl.BlockSpec((ts, D_MIX), lambda i: (i, 0))],
        out_specs=[out] * 6, out_shape=[jax.ShapeDtypeStruct((s, D_MIX), BF16)] * 6,
        compiler_params=_params(("parallel",)),
    )(proj, proj, proj, cum)


def _pair_specs(s, t):
    return pl.BlockSpec((t, LANES), lambda p, i: (i, p)), pl.BlockSpec((s, LANES), lambda p, i: (0, p))


def attn_fwd(qe, qo, ke, ko, ve, vo, tq=1024):
    s = qe.shape[0]
    tq = min(tq, s)

    def body(qe_ref, qo_ref, ke_ref, ko_ref, ve_ref, vo_ref, o_ref, qbe_ref, qbo_ref):
        qi = pl.program_id(1)
        qs, k_refs, v_refs = (qe_ref[...], qo_ref[...]), (ke_ref, ko_ref), (ve_ref, vo_ref)
        causal = lax.broadcasted_iota(jnp.int32, (tq, tq), 1) <= lax.broadcasted_iota(jnp.int32, (tq, tq), 0)

        def step(j, carry, diag):
            ks = pl.multiple_of(j * tq, tq)
            new = []
            for h in range(2):
                m, l, acc = carry[h]
                sc = lax.dot_general(qs[h], k_refs[h][pl.ds(ks, tq), :], NT, preferred_element_type=F32)
                if diag:
                    sc = jnp.where(causal, sc, -jnp.inf)
                m_new = jnp.maximum(m, jnp.max(sc, axis=1, keepdims=True))
                p = jnp.exp2(sc - m_new)
                alpha = jnp.exp2(m - m_new)
                l = alpha * l + jnp.sum(p, axis=1, keepdims=True)
                acc = alpha * acc + jnp.dot(p.astype(BF16), v_refs[h][pl.ds(ks, tq), :], preferred_element_type=F32)
                new.append((m_new, l, acc))
            return tuple(new)

        init = tuple((jnp.full((tq, 1), -jnp.inf, F32), jnp.zeros((tq, 1), F32), jnp.zeros((tq, LANES), F32)) for _ in range(2))
        carry = lax.fori_loop(0, qi, lambda j, c: step(j, c, False), init)
        (m0, l0, a0), (m1, l1, a1) = step(qi, carry, True)
        _, li, lo = _lane_ids((tq, LANES))
        o_ref[...] = jnp.where(lo, a0 / l0, a1 / l1).astype(o_ref.dtype)
        lse_lanes = (li >= 6) & (li < 9)
        for q, m, l, spare, out_ref in ((qs[0], m0, l0, ~lo, qbe_ref), (qs[1], m1, l1, lo, qbo_ref)):
            neg_lse = _triple(li, 6, _split3(-(m + jnp.log(l) * LOG2E)))
            out_ref[...] = jnp.where(spare & lse_lanes, neg_lse.astype(BF16), q)

    blk, full = _pair_specs(s, tq)
    return pl.pallas_call(
        body, name="attn_fwd", grid=(N_PAIR, s // tq),
        in_specs=[blk, blk, full, full, full, full],
        out_specs=[blk] * 3, out_shape=[jax.ShapeDtypeStruct((s, D_MIX), BF16)] * 3,
        compiler_params=_params(("parallel", "parallel")),
    )(qe, qo, ke, ko, ve, vo)


def attn_dsum(qbe, qbo, ke, ko, ve, vo, do, dep, tq=1024):
    s = qbe.shape[0]
    tq = min(tq, s)

    def body(qe_ref, qo_ref, ke_ref, ko_ref, ve_ref, vo_ref, do_ref, dep_ref, dobe_ref, dobo_ref):
        qi = pl.program_id(1)
        _, li, lo = _lane_ids((tq, LANES))
        do_ = do_ref[...]
        qs, k_refs, v_refs = (qe_ref[...], qo_ref[...]), (ke_ref, ko_ref), (ve_ref, vo_ref)
        dos = (jnp.where(lo, do_, 0), jnp.where(lo, 0, do_))
        causal = lax.broadcasted_iota(jnp.int32, (tq, tq), 1) <= lax.broadcasted_iota(jnp.int32, (tq, tq), 0)

        def step(j, carry, diag):
            ks = pl.multiple_of(j * tq, tq)
            new = []
            for h in range(2):
                sc = lax.dot_general(qs[h], k_refs[h][pl.ds(ks, tq), :], NT, preferred_element_type=F32)
                if diag:
                    sc = jnp.where(causal, sc, -jnp.inf)
                pdp = jnp.exp2(sc) * lax.dot_general(dos[h], v_refs[h][pl.ds(ks, tq), :], NT, preferred_element_type=F32)
                new.append(carry[h] + jnp.sum(pdp, axis=1, keepdims=True))
            return tuple(new)

        init = (jnp.zeros((tq, 1), F32), jnp.zeros((tq, 1), F32))
        carry = lax.fori_loop(0, qi, lambda j, c: step(j, c, False), init)
        s0, s1 = step(qi, carry, True)
        dobe_ref[...] = jnp.where(lo, do_, _triple(li, 0, _split3(-s0)).astype(BF16))
        dobo_ref[...] = jnp.where(lo, _triple(li, 0, _split3(-s1)).astype(BF16), do_)

    blk, full = _pair_specs(s, tq)
    return pl.pallas_call(
        body, name="attn_dsum", grid=(N_PAIR, s // tq),
        in_specs=[blk, blk, full, full, full, full, blk, pl.BlockSpec(dep.shape, lambda p, i: (0, 0))],
        out_specs=[blk, blk], out_shape=[jax.ShapeDtypeStruct((s, D_MIX), BF16)] * 2,
        compiler_params=_params(("parallel", "parallel")),
    )(qbe, qbo, ke, ko, ve, vo, do, dep)


def attn_dkv(ke, ko, ve, vo, qbe, qbo, dobe, dobo, dproj, lane_block, tk=1024):
    s = ke.shape[0]
    tk = min(tk, s)
    nq = s // tk
    tn = (((0,), (0,)), ((), ()))

    def body(ke_ref, ko_ref, ve_ref, vo_ref, qe_ref, qo_ref, de_ref, do_ref, dproj_in, dk_ref, dv_ref, dck_ref, dq_ref):
        kj = pl.program_id(1)
        lo = _lane_lt64((tk, LANES))
        ks_, vs_, q_refs, d_refs = (ke_ref[...], ko_ref[...]), (ve_ref[...], vo_ref[...]), (qe_ref, qo_ref), (de_ref, do_ref)
        causal = lax.broadcasted_iota(jnp.int32, (tk, tk), 0) <= lax.broadcasted_iota(jnp.int32, (tk, tk), 1)

        @pl.when(kj == 0)
        def _():
            dq_ref[...] = jnp.zeros(dq_ref.shape, F32)

        def step(i, carry, diag):
            qs = pl.multiple_of(i * tk, tk)
            new, dq_h = [], []
            for h in range(2):
                dk, dv, dck = carry[h]
                qblk = q_refs[h][pl.ds(qs, tk), :]
                dblk = d_refs[h][pl.ds(qs, tk), :]
                st = lax.dot_general(ks_[h], qblk, NT, preferred_element_type=F32)
                if diag:
                    st = jnp.where(causal, st, -jnp.inf)
                pt = jnp.exp2(st)
                dst = pt * lax.dot_general(vs_[h], dblk, NT, preferred_element_type=F32)
                dst_bf = dst.astype(BF16)
                dq_h.append(lax.dot_general(dst_bf, ks_[h], tn, preferred_element_type=F32))
                new.append((dk + jnp.dot(dst_bf, qblk, preferred_element_type=F32),
                            dv + jnp.dot(pt.astype(BF16), dblk, preferred_element_type=F32),
                            dck - jnp.sum(dst, axis=1, keepdims=True)))
            dq_ref[pl.ds(qs, tk), :] += jnp.where(lo, dq_h[0], dq_h[1])
            return tuple(new)

        init = tuple((jnp.zeros((tk, LANES), F32), jnp.zeros((tk, LANES), F32), jnp.zeros((tk, 1), F32)) for _ in range(2))
        carry = step(kj, init, True)
        (dk0, dv0, dc0), (dk1, dv1, dc1) = lax.fori_loop(kj + 1, nq, lambda i, c: step(i, c, False), carry)
        dk_ref[...] = (jnp.where(lo, dk0, dk1) * (1.0 / LOG2E)).astype(dk_ref.dtype)
        dv_ref[...] = jnp.where(lo, dv0, dv1).astype(dv_ref.dtype)
        dck_ref[...] = jnp.where(lo, dc0, dc1)

    blk, full = _pair_specs(s, tk)
    return pl.pallas_call(
        body, name="attn_dkv", grid=(N_PAIR, nq),
        in_specs=[blk, blk, blk, blk, full, full, full, full, pl.BlockSpec(memory_space=pl.ANY)],
        out_specs=[pl.BlockSpec((tk, LANES), lambda p, i: (i, lane_block + p)), blk, blk, full],
        out_shape=[jax.ShapeDtypeStruct(dproj.shape, dproj.dtype), jax.ShapeDtypeStruct((s, D_MIX), BF16), jax.ShapeDtypeStruct((s, D_MIX), F32),
                   jax.ShapeDtypeStruct((s, D_MIX), F32)],
        input_output_aliases={8: 0}, compiler_params=_params(("parallel", "arbitrary")),
    )(ke, ko, ve, vo, qbe, qbo, dobe, dobo, dproj)


def _pre_bwd(name, x, g, sc, sh, dh, dres):
    d = x.shape[1]

    def fn(xv, dhv, dresv, gv, scv, shv):
        _, vjp = jax.vjp(lambda *a: _f_pre(*a)[0], xv, gv, scv, shv)
        dx, dg, dsc, dsh = vjp(dhv.astype(F32))
        return dx + dresv, dg, dsc, dsh

    return rowwise(name, fn, [(x, 0, d), (dh, 0, d), (dres, 0, d)], [g, sc, sh], [(d, F32)], accs=[(1, d)] * 3)


def mix_out(proj, y_a, y_b, y_c, w_out, x, consts, ts=512):
    s, d = x.shape
    ts = min(ts, s)

    def body(g0_ref, g1_ref, g2_ref, ya_ref, yb_ref, yc_ref, w_ref, x_ref, gp_ref, gt_ref, g2n_ref, sc_ref, sh_ref,
             m_ref, y_ref, x1_ref, h2_ref):
        f = lambda r: r[...].astype(F32)
        merged = _f_merge(f(g0_ref), f(g1_ref), f(g2_ref), f(ya_ref), f(yb_ref), f(yc_ref))[0].astype(BF16)
        m_ref[...] = merged
        y = jnp.dot(merged, w_ref[...], preferred_element_type=F32).astype(BF16)
        y_ref[...] = y
        x1 = x_ref[...] + _f_post(y.astype(F32), gp_ref[...], gt_ref[...])[0]
        x1_ref[...] = x1
        h2_ref[...] = _f_pre(x1, g2n_ref[...], sc_ref[...], sh_ref[...])[0].astype(BF16)

    col = lambda c: pl.BlockSpec((ts, d), functools.partial(lambda i, c: (i, c), c=c))
    row = pl.BlockSpec((ts, d), lambda i: (i, 0))
    const = lambda a: pl.BlockSpec(a.shape, lambda i: (0, 0))
    return pl.pallas_call(
        body, name="mix_out", grid=(s // ts,),
        in_specs=[col(0), col(1), col(2), row, row, row, const(w_out), row] + [const(a) for a in consts],
        out_specs=[row] * 4,
        out_shape=[jax.ShapeDtypeStruct((s, d), BF16), jax.ShapeDtypeStruct((s, d), BF16), jax.ShapeDtypeStruct((s, d), F32),
                   jax.ShapeDtypeStruct((s, d), BF16)],
        compiler_params=_params(("parallel",)),
    )(proj, proj, proj, y_a, y_b, y_c, w_out, x, *consts)


def layer_fwd(x, mod, layer):
    s, d = x.shape
    m = D_MIX
    w = dict(layer["small"])
    sh1, sc1, gt1, sh2, sc2, gt2 = (mod[i:i + 1] for i in range(6))
    cb = 3 * d // m
    (h,) = rowwise("pre1", _f_pre, [(x, 0, d)], [w["mix_pre_g"], sc1, sh1], [(d, BF16)])
    w.update(layer["get_in"](h))
    proj, fproj = matmul("w_in", h, w["w_in"], out_dtypes=(BF16,), f32_tail=True, tn=2 * m)
    fcb = 1
    w.update(layer["get_abco"](proj))
    u, vln = rowwise("gmlp_in", _f_a1, [(proj, cb, m), (proj, cb + 1, m)], [w["gmlp_ln_g"], w["gmlp_ln_b"]], [(m, F32), (m, BF16)])
    sv, ya = spatial_fwd(vln, u, w["ws"], w["bs_exp"])
    y_a = matmul("w_a", ya, w["w_a_out"], out_dtypes=(BF16,))
    zc, yb = conv_fwd(proj, cb + 2, cb + 3, w["conv_w"], w["conv_b"], w["conv_ln_g"], w["conv_ln_b"])
    y_b = matmul("w_b", yb, w["w_b_out"], out_dtypes=(BF16,))
    cum = forget_cumsum(fproj, fcb, w["bf_exp"])
    kv_ops = attn_prep(proj, cb + 4, cum)
    o, qbe, qbo = attn_fwd(*kv_ops)
    att = (qbe, qbo) + tuple(kv_ops[2:])
    y_c = matmul("w_c", o, w["w_c_out"], out_dtypes=(BF16,))
    w.update(layer["get_mlp"](y_c))
    merged, y, x1, h2 = mix_out(proj, y_a, y_b, y_c, w["w_out"], x, [w["mix_post_g"], gt1, w["mlp_pre_g"], sc2, sh2])
    r = matmul("w1", h2, w["mlp_w1"], out_dtypes=(BF16,), epilogue=lambda acc: (jnp.square(jnp.maximum(acc, 0.0)),))
    y2 = matmul("w2", r, w["mlp_w2"], out_dtypes=(BF16,))
    (x2,) = rowwise("post2", lambda xv, yv, g, gt: xv + _f_post(yv, g, gt)[0], [(x1, 0, d), (y2, 0, d)], [w["mlp_post_g"], gt2], [(d, F32)])
    saved = dict(w=w, x=x, h=h, proj=proj, fproj=fproj, fcb=fcb, u=u, vln=vln, sv=sv, ya=ya, y_a=y_a, zc=zc, yb=yb, y_b=y_b, att=att,
                 o=o, y_c=y_c, merged=merged, y=y, x1=x1, h2=h2, r=r, y2=y2)
    return x2, saved


def layer_bwd(dx2, mod, sv, emit, tok_in=None):
    x, proj, w = sv["x"], sv["proj"], sv["w"]
    s, d = x.shape
    m = D_MIX
    sh1, sc1, gt1, sh2, sc2, gt2 = (mod[i:i + 1] for i in range(6))
    cb = 3 * d // m
    g = {}
    if tok_in is not None:
        gt2 = gt2 + tok_in[0:1, 0:1]
    dy2, g["mlp_post_g"], dgt2 = rowwise_vjp("post2_b", _f_post, [(sv["y2"], 0, d)], [w["mlp_post_g"], gt2], [(dx2, 0, d)], [BF16])
    da = matmul("w2_dx", dy2, w["mlp_w2"], tb=True, out_dtypes=(BF16,),
                epilogue=lambda acc, r: (acc * (2.0 * jnp.sqrt(r.astype(F32))),), epi=[(sv["r"], 0)])
    big = {}
    big["mlp_w2"] = matmul("w2_dw", sv["r"], dy2, ta=True, out_dtypes=(BF16,))
    dh2 = matmul("w1_dx", da, w["mlp_w1"], tb=True, out_dtypes=(BF16,))
    big["mlp_w1"] = matmul("w1_dw", sv["h2"], da, ta=True, out_dtypes=(BF16,), out_blocks=N_DEV)
    tok = emit("mlp", big)
    dx1, g["mlp_pre_g"], dsc2, dsh2 = _pre_bwd("pre2_b", sv["x1"], w["mlp_pre_g"], sc2 + tok[0:1, 0:1], sh2, dh2, dx2)
    dy, g["mix_post_g"], dgt1 = rowwise_vjp("post1_b", _f_post, [(sv["y"], 0, d)], [w["mix_post_g"], gt1], [(dx1, 0, d)], [BF16])
    dmerged = matmul("w_out_dx", dy, w["w_out"], tb=True, out_dtypes=(BF16,))
    big = {}
    big["w_out"] = matmul("w_out_dw", sv["merged"], dy, ta=True, out_dtypes=(BF16,))
    assert (3 * d) % (2 * m) == 0
    dproj = lax.empty((s, 3 * d + 8 * m), BF16)
    dproj, dya_, dyb_, dyc_ = rowwise_vjp(
        "merge_b", _f_merge, [(proj, 0, d), (proj, 1, d), (proj, 2, d), (sv["y_a"], 0, d), (sv["y_b"], 0, d), (sv["y_c"], 0, d)], [],
        [(dmerged, 0, d)], [BF16] * 6, into=(dproj, 0, 3))
    dya_pre = matmul("w_a_dx", dya_, w["w_a_out"], tb=True, out_dtypes=(BF16,))
    big["w_a_out"] = matmul("w_a_dw", sv["ya"], dya_, ta=True, out_dtypes=(BF16,))
    dyb_pre = matmul("w_b_dx", dyb_, w["w_b_out"], tb=True, out_dtypes=(BF16,))
    big["w_b_out"] = matmul("w_b_dw", sv["yb"], dyb_, ta=True, out_dtypes=(BF16,))
    do = matmul("w_c_dx", dyc_, w["w_c_out"], tb=True, out_dtypes=(BF16,))
    big["w_c_out"] = matmul("w_c_dw", sv["o"], dyc_, ta=True, out_dtypes=(BF16,))
    tok = emit("abco", big)
    qbe, qbo, ke, ko, ve, vo = sv["att"]
    lane0 = (cb + 4) * (m // LANES)
    dobe, dobo = attn_dsum(qbe, qbo, ke, ko, ve, vo, do, tok)
    dproj, dv, dcum, dq = attn_dkv(ke, ko, ve, vo, qbe, qbo, dobe, dobo, dproj, lane0 + N_PAIR)
    (dproj,) = rowwise("dq_scale", lambda g: g * (1.0 / math.sqrt(HEAD_DIM)), [(dq, 0, m)], [], [(m, BF16)], into=(dproj, cb + 4))
    dproj = lax.dynamic_update_slice(dproj, dv, (0, (cb + 6) * m))
    dproj, dbf = forget_bwd(sv["fproj"], sv["fcb"], w["bf_exp"], dcum, dproj, cb + 7)
    g["fox_bf"] = dbf[0, ::HEAD_DIM]
    dproj, dwc, g["conv_b"], g["conv_ln_g"], g["conv_ln_b"] = conv_bwd(
        proj, cb + 2, cb + 3, sv["zc"], dyb_pre, w["conv_w"], w["conv_ln_g"], w["conv_ln_b"], dproj)
    g["conv_w"] = dwc[:CONV_WIDTH]
    du, dvln, dws, dbexp = spatial_bwd(dya_pre, sv["u"], sv["sv"], sv["vln"], w["ws_t"])
    g["gmlp_ws"] = dws * jnp.tril(jnp.ones((CHUNK, CHUNK), F32))
    g["gmlp_bs"] = dbexp.reshape(CHUNK, m // GROUP_DIM, GROUP_DIM).sum(-1).T
    dproj, g["gmlp_ln_g"], g["gmlp_ln_b"] = rowwise_vjp(
        "gmlp_in_b", _f_a1, [(proj, cb, m), (proj, cb + 1, m)], [w["gmlp_ln_g"], w["gmlp_ln_b"]], [(du, 0, m), (dvln, 0, m)], [BF16, BF16],
        into=(dproj, cb // 2, 2))
    tok = emit("in", {"w_in": matmul("w_in_dw", sv["h"], dproj, ta=True, out_dtypes=(BF16,))})
    dh = matmul("w_in_dx", dproj, w["w_in"], tb=True, dep=tok, out_dtypes=(BF16,))
    dx, g["mix_pre_g"], dsc1, dsh1 = _pre_bwd("pre1_b", x, w["mix_pre_g"], sc1, sh1, dh, dx1)
    dmod = jnp.concatenate([dsh1, dsc1, dgt1, dsh2, dsc2, dgt2], axis=0)
    return dx, dmod, g


def local_step(x, target, mods, layers):
    d = x.shape[1]
    saved = []
    for l in range(len(layers)):
        x, sv = layer_fwd(x, mods[l], layers[l])
        saved.append(sv)

    def loss_fn(xv, tv):
        err = xv - tv
        return err * (1.0 / d), jnp.sum(err * err, axis=0, keepdims=True)

    dx, sq = rowwise("loss", loss_fn, [(x, 0, d), (target, 0, d)], [], [(d, F32)], accs=[(1, d)])
    loss = (0.5 / d) * jnp.sum(sq)
    dmods, grads = [None] * len(layers), [None] * len(layers)
    tok = None
    for l in reversed(range(len(layers))):
        dx, dmods[l], grads[l] = layer_bwd(dx, mods[l], saved[l], layers[l]["emit"], tok)
        tok = layers[l]["emit_small"](dmods[l], grads[l])
    return loss, dx, dmods, grads


def exchange(name, arrs, scatter):
    n = len(arrs)

    def body(*refs):
        in_refs, out_refs = refs[:n], refs[n:2 * n]
        send_sems, recv_sems, local_sems = refs[2 * n:]
        x, y, c = lax.axis_index("x"), lax.axis_index("y"), lax.axis_index("c")
        me = 4 * x + 2 * y + c
        local = []
        for a in range(n):
            src = in_refs[a].at[me] if scatter else in_refs[a]
            cp = pltpu.make_async_copy(src, out_refs[a].at[me], local_sems.at[a])
            cp.start()
            local.append(cp)
        remote = []
        for k in range(1, N_DEV):
            px, py, pc = x ^ ((k >> 2) & 1), y ^ ((k >> 1) & 1), c ^ (k & 1)
            peer = 4 * px + 2 * py + pc
            for a in range(n):
                src = in_refs[a].at[peer] if scatter else in_refs[a]
                cp = pltpu.make_async_remote_copy(
                    src_ref=src, dst_ref=out_refs[a].at[me], send_sem=send_sems.at[a * (N_DEV - 1) + k - 1],
                    recv_sem=recv_sems.at[a * (N_DEV - 1) + k - 1], device_id=(px, py, pc), device_id_type=MESH)
                cp.start()
                remote.append(cp)
        for cp in remote:
            cp.wait()
        for cp in local:
            cp.wait()

    hbm = pl.BlockSpec(memory_space=pltpu.HBM)
    out_shape = [jax.ShapeDtypeStruct(a.shape if scatter else (N_DEV,) + a.shape, a.dtype) for a in arrs]
    return pl.pallas_call(
        body, name=name, in_specs=[hbm] * n, out_specs=[hbm] * n, out_shape=out_shape,
        scratch_shapes=[pltpu.SemaphoreType.DMA((n * (N_DEV - 1),)), pltpu.SemaphoreType.DMA((n * (N_DEV - 1),)),
                        pltpu.SemaphoreType.DMA((n,))],
    )(*arrs)


def _peers(x, y, c):
    out = []
    for k in range(1, N_DEV):
        px, py, pc = x ^ ((k >> 2) & 1), y ^ ((k >> 1) & 1), c ^ (k & 1)
        out.append((k - 1, (px, py, pc), 4 * px + 2 * py + pc))
    return out


def _exchange_copies(srcs, lands, send_sems, recv_sems, scatter):
    x, y, c = lax.axis_index("x"), lax.axis_index("y"), lax.axis_index("c")
    me = 4 * x + 2 * y + c
    copies = []
    for slot, pos, peer in _peers(x, y, c):
        for a, (src, land) in enumerate(zip(srcs, lands)):
            copies.append(pltpu.make_async_remote_copy(
                src_ref=src.at[peer] if scatter else src, dst_ref=land.at[me],
                send_sem=send_sems.at[a * (N_DEV - 1) + slot], recv_sem=recv_sems.at[a * (N_DEV - 1) + slot],
                device_id=pos, device_id_type=MESH))
    return me, copies


def exchange_start(name, arrs, scatter, after=None):
    n = len(arrs)
    lands = [lax.empty(a.shape if scatter else (N_DEV,) + a.shape, a.dtype) for a in arrs]
    n_in = 2 * n + (after is not None)

    def body(*refs):
        srcs, lands_ = refs[:n], refs[n:2 * n]
        send_sems, recv_sems = refs[n_in], refs[n_in + 1]
        token = refs[n_in + 2 + 2 * n]
        _, copies = _exchange_copies(srcs, lands_, send_sems, recv_sems, scatter)
        for cp in copies:
            cp.start()
        token[...] = jnp.zeros(token.shape, token.dtype)

    hbm = pl.BlockSpec(memory_space=pltpu.HBM)
    sem = pl.BlockSpec(memory_space=pltpu.SEMAPHORE)
    n_sem = n * (N_DEV - 1)
    res = pl.pallas_call(
        body, name=name,
        out_shape=(pltpu.SemaphoreType.DMA((n_sem,)), pltpu.SemaphoreType.DMA((n_sem,)),
                   *[pltpu.HBM(a.shape, a.dtype) for a in arrs], *[pltpu.HBM(l.shape, l.dtype) for l in lands],
                   jax.ShapeDtypeStruct((8, LANES), F32)),
        in_specs=[hbm] * (2 * n) + ([] if after is None else [pl.BlockSpec(memory_space=pl.ANY)]),
        out_specs=(sem, sem, *([hbm] * (2 * n)), pl.BlockSpec(memory_space=pltpu.VMEM)),
        input_output_aliases={i: 2 + i for i in range(2 * n)},
        compiler_params=pltpu.CompilerParams(has_side_effects=pltpu.SideEffectType.DATAFLOW_SIDE_EFFECTING),
    )(*[pltpu.with_memory_space_constraint(a, pltpu.HBM) for a in arrs],
      *[pltpu.with_memory_space_constraint(l, pltpu.HBM) for l in lands], *([] if after is None else [after]))
    return dict(n=n, scatter=scatter, send=res[0], recv=res[1], srcs=res[2:2 + n], lands=res[2 + n:2 + 2 * n], token=res[2 + 2 * n])


def exchange_wait(name, st, after):
    n, scatter = st["n"], st["scatter"]
    after = list(after) if isinstance(after, (list, tuple)) else [after]

    def body(*refs):
        srcs, lands_ = refs[:n], refs[n:2 * n]
        send_sems, recv_sems = refs[2 * n], refs[2 * n + 1]
        _, copies = _exchange_copies(srcs, lands_, send_sems, recv_sems, scatter)
        for cp in copies:
            cp.wait_send()
            cp.wait_recv()

    hbm = pl.BlockSpec(memory_space=pltpu.HBM)
    sem = pl.BlockSpec(memory_space=pltpu.SEMAPHORE)
    res = pl.pallas_call(
        body, name=name,
        out_shape=tuple(pltpu.HBM(a.shape, a.dtype) for a in (*st["srcs"], *st["lands"])),
        in_specs=[hbm] * (2 * n) + [sem, sem] + [pl.BlockSpec(memory_space=pl.ANY)] * len(after), out_specs=tuple([hbm] * (2 * n)),
        input_output_aliases={i: i for i in range(2 * n)},
        compiler_params=pltpu.CompilerParams(has_side_effects=pltpu.SideEffectType.DATAFLOW_SIDE_EFFECTING),
    )(*st["srcs"], *st["lands"], st["send"], st["recv"], *after)
    return list(res[:n]), list(res[n:])


def adamw_sum(name, parts, w, m, v, tr=256):
    nl = len(parts)
    k, r, c = parts[0].shape
    tr = _tile(r, tr, 16)
    c1 = 1.0 - ADAM_B1 ** ADAM_STEP
    c2 = 1.0 - ADAM_B2 ** ADAM_STEP

    def body(*refs):
        p_refs = refs[:nl]
        w_ref, m_ref, v_ref, g_ref, d_ref, nm_ref, nv_ref = refs[nl:]
        for l in range(nl):
            @pl.when(pl.program_id(0) == l)
            def _(p_ref=p_refs[l]):
                grad = p_ref[0].astype(F32)
                for j in range(1, k):
                    grad = grad + p_ref[j].astype(F32)
                new_m = ADAM_B1 * m_ref[...] + (1.0 - ADAM_B1) * grad
                new_v = ADAM_B2 * v_ref[...] + (1.0 - ADAM_B2) * (grad * grad)
                m_hat = new_m / c1
                v_hat = new_v / c2
                g_ref[...] = grad
                d_ref[...] = -ADAM_LR * (m_hat / (jnp.sqrt(v_hat) + ADAM_EPS) + ADAM_WD * w_ref[...])
                nm_ref[...] = new_m
                nv_ref[...] = new_v

    part = lambda l: pl.BlockSpec((k, tr, c), functools.partial(lambda ll, i, l: (0, jnp.where(ll == l, i, 0), 0), l=l))
    blk = pl.BlockSpec((None, tr, c), lambda ll, i: (ll, i, 0))
    return pl.pallas_call(
        body, name=name, grid=(nl, r // tr),
        in_specs=[part(l) for l in range(nl)] + [blk, blk, blk],
        out_specs=[blk] * 4, out_shape=[jax.ShapeDtypeStruct((nl, r, c), F32)] * 4,
        compiler_params=_params(("parallel", "parallel")),
    )(*parts, w, m, v)


def ada_fwd(c_all, ada_w):
    nl, d, n = ada_w.shape

    def body(c_ref, w_ref, o_ref):
        o_ref[...] = jnp.dot(_silu(c_ref[...]), w_ref[...], precision=HIGHEST, preferred_element_type=F32)

    return pl.pallas_call(
        body, name="ada_fwd", grid=(nl,),
        in_specs=[pl.BlockSpec((N_DEV, d), lambda l: (0, 0)), pl.BlockSpec((None, d, n), lambda l: (l, 0, 0))],
        out_specs=pl.BlockSpec((None, N_DEV, n), lambda l: (l, 0, 0)),
        out_shape=jax.ShapeDtypeStruct((nl, N_DEV, n), F32),
        compiler_params=_params(("parallel",)),
    )(c_all, ada_w)


def ada_bwd(c_all_t, dmod, td=256):
    d = c_all_t.shape[0]
    nl, _, n = dmod.shape
    td = _tile(d, td, 8)

    def body(c_ref, dm_ref, o_ref):
        ca = _silu(c_ref[...])
        acc = ca[:, 0:1] * dm_ref[0:1, :]
        for b in range(1, N_DEV):
            acc = acc + ca[:, b:b + 1] * dm_ref[b:b + 1, :]
        o_ref[...] = acc

    return pl.pallas_call(
        body, name="ada_bwd", grid=(nl, d // td),
        in_specs=[pl.BlockSpec((td, N_DEV), lambda l, i: (i, 0)), pl.BlockSpec((None, N_DEV, n), lambda l, i: (l, 0, 0))],
        out_specs=pl.BlockSpec((None, td, n), lambda l, i: (l, i, 0)),
        out_shape=jax.ShapeDtypeStruct((nl, d, n), F32),
        compiler_params=_params(("parallel", "parallel")),
    )(c_all_t, dmod)


ARG_NAMES = ["x", "c", "ada_w", "ada_b", "mix_pre_g", "mix_post_g", "mlp_pre_g", "mlp_post_g", "w_in", "gmlp_ln_g", "gmlp_ln_b",
             "gmlp_ws", "gmlp_bs", "w_a_out", "conv_w", "conv_b", "conv_ln_g", "conv_ln_b", "w_b_out", "fox_bf", "w_c_out",
             "w_out", "mlp_w1", "mlp_w2"]
WEIGHTS = ARG_NAMES[2:]
COL_SHARDED = ["w_in", "w_a_out", "w_b_out", "w_c_out", "mlp_w1"]
ROW_SHARDED = ["w_out", "mlp_w2"]
BIG = COL_SHARDED + ROW_SHARDED
GROUPS = {"in": ["w_in"], "abco": ["w_a_out", "w_b_out", "w_c_out", "w_out"], "mlp": ["mlp_w1", "mlp_w2"]}
SMALL = ["ada_b", "mix_pre_g", "mix_post_g", "mlp_pre_g", "mlp_post_g", "gmlp_ln_g", "gmlp_ln_b", "gmlp_ws", "gmlp_bs",
         "conv_b", "conv_ln_g", "conv_ln_b", "fox_bf"]
PACK_COLS = 512


def _ref_ranges(lo, hi, shard):
    out = []
    while lo < hi:
        j = lo // shard
        end = min(hi, (j + 1) * shard)
        out.append((j, lo - j * shard, end - j * shard))
        lo = end
    return out


def _w_in_to_kernel_layout(g, d):
    m = D_MIX
    nf = 7 * m
    shard = g.shape[2]
    cols = lambda lo, hi: [g[j, :, a:b] for j, a, b in _ref_ranges(lo, hi, shard)]
    forget = jnp.concatenate(cols(nf, nf + N_HEADS), axis=1)
    return jnp.concatenate(cols(nf + N_HEADS, nf + N_HEADS + 3 * d) + cols(0, nf) + [jnp.repeat(forget, HEAD_DIM, axis=1)], axis=1)


def _w_in_grad_blocks(gw, d):
    m = D_MIX
    nf = 7 * m
    n_ref = nf + N_HEADS + 3 * d
    shard = n_ref // N_DEV
    segs = [(0, nf, 3 * d, 1), (nf, nf + N_HEADS, 3 * d + nf, HEAD_DIM), (nf + N_HEADS, n_ref, 0, 1)]
    blocks = []
    for j in range(N_DEV):
        lo, hi = j * shard, (j + 1) * shard
        pieces = []
        for r0, r1, k0, stride in segs:
            a, b = max(lo, r0), min(hi, r1)
            if a < b:
                pieces.append(gw[:, k0 + (a - r0) * stride:k0 + (b - r0) * stride:stride])
        blocks.append(jnp.concatenate(pieces, axis=1) if len(pieces) > 1 else pieces[0])
    return jnp.stack(blocks)


def _pack(parts):
    flat = jnp.concatenate([p.reshape(-1).astype(F32) for p in parts])
    pad = (-flat.shape[0]) % (PACK_COLS * 8)
    return jnp.pad(flat, (0, pad)).reshape(-1, PACK_COLS)


def _unpack(packed, shapes):
    nl = packed.shape[0]
    flat, out, off = packed.reshape(nl, -1), [], 0
    for shp in shapes:
        n = math.prod(shp)
        out.append(flat[:, off:off + n].reshape((nl,) + tuple(shp)))
        off += n
    return out


def _layer_small(p, conv_full, l):
    wl = {}
    for k in ["mix_pre_g", "mix_post_g", "mlp_pre_g", "mlp_post_g", "gmlp_ln_g", "gmlp_ln_b", "conv_b", "conv_ln_g", "conv_ln_b"]:
        wl[k] = p[k][l][None, :]
    wm = p["gmlp_ws"][l] * jnp.tril(jnp.ones((CHUNK, CHUNK), F32))
    wl["ws"] = wm.astype(BF16)
    wl["ws_t"] = jnp.transpose(wm, (0, 2, 1)).astype(BF16)
    wl["bs_exp"] = jnp.repeat(p["gmlp_bs"][l].T, GROUP_DIM, axis=1)
    wl["bf_exp"] = jnp.repeat(p["fox_bf"][l], HEAD_DIM)[None, :]
    wl["conv_w"] = jnp.pad(conv_full[l], ((0, CONV_HALO - CONV_WIDTH), (0, 0)))
    return wl


def kernel(x, c, ada_w, ada_b, mix_pre_g, mix_post_g, mlp_pre_g, mlp_post_g, w_in, gmlp_ln_g, gmlp_ln_b, gmlp_ws, gmlp_bs, w_a_out, conv_w, conv_b, conv_ln_g, conv_ln_b, w_b_out, fox_bf, w_c_out, w_out, mlp_w1, mlp_w2, loss_target, m_ada_w, m_ada_b, m_mix_pre_g, m_mix_post_g, m_mlp_pre_g, m_mlp_post_g, m_w_in, m_gmlp_ln_g, m_gmlp_ln_b, m_gmlp_ws, m_gmlp_bs, m_w_a_out, m_conv_w, m_conv_b, m_conv_ln_g, m_conv_ln_b, m_w_b_out, m_fox_bf, m_w_c_out, m_w_out, m_mlp_w1, m_mlp_w2, v_ada_w, v_ada_b, v_mix_pre_g, v_mix_post_g, v_mlp_pre_g, v_mlp_post_g, v_w_in, v_gmlp_ln_g, v_gmlp_ln_b, v_gmlp_ws, v_gmlp_bs, v_w_a_out, v_conv_w, v_conv_b, v_conv_ln_g, v_conv_ln_b, v_w_b_out, v_fox_bf, v_w_c_out, v_w_out, v_mlp_w1, v_mlp_w2):
    args = (x, c, ada_w, ada_b, mix_pre_g, mix_post_g, mlp_pre_g, mlp_post_g, w_in, gmlp_ln_g, gmlp_ln_b, gmlp_ws, gmlp_bs, w_a_out,
            conv_w, conv_b, conv_ln_g, conv_ln_b, w_b_out, fox_bf, w_c_out, w_out, mlp_w1, mlp_w2)
    ms = (m_ada_w, m_ada_b, m_mix_pre_g, m_mix_post_g, m_mlp_pre_g, m_mlp_post_g, m_w_in, m_gmlp_ln_g, m_gmlp_ln_b, m_gmlp_ws, m_gmlp_bs,
          m_w_a_out, m_conv_w, m_conv_b, m_conv_ln_g, m_conv_ln_b, m_w_b_out, m_fox_bf, m_w_c_out, m_w_out, m_mlp_w1, m_mlp_w2)
    vs = (v_ada_w, v_ada_b, v_mix_pre_g, v_mix_post_g, v_mlp_pre_g, v_mlp_post_g, v_w_in, v_gmlp_ln_g, v_gmlp_ln_b, v_gmlp_ws, v_gmlp_bs,
          v_w_a_out, v_conv_w, v_conv_b, v_conv_ln_g, v_conv_ln_b, v_w_b_out, v_fox_bf, v_w_c_out, v_w_out, v_mlp_w1, v_mlp_w2)
    p = dict(zip(ARG_NAMES, args))
    mom = dict(zip(WEIGHTS, ms))
    var = dict(zip(WEIGHTS, vs))
    nl = ada_w.shape[0]
    s, d = x.shape[1], x.shape[2]
    me = 4 * lax.axis_index("x") + 2 * lax.axis_index("y") + lax.axis_index("c")

    c_all, conv_all = exchange("gather_c", [c, conv_w], scatter=False)
    c_all = c_all.reshape(N_DEV, d)
    n_ada = ada_w.shape[2]
    mod_parts = ada_fwd(c_all, ada_w)
    (mod_recv,) = exchange("scatter_mod", [jnp.transpose(mod_parts, (1, 0, 2))], scatter=True)
    conv_full = jnp.transpose(conv_all, (1, 2, 0, 3)).reshape(nl, CONV_WIDTH, D_MIX)

    def full_matrix(k, land, own):
        g = lax.dynamic_update_index_in_dim(land, own, me, 0)
        r, cc = own.shape
        if k == "w_in":
            return _w_in_to_kernel_layout(g, d)
        return jnp.transpose(g, (1, 0, 2)).reshape(r, N_DEV * cc) if k in COL_SHARDED else g.reshape(N_DEV * r, cc)

    started = [mod_recv]

    def fetch(l, tag):
        keys = GROUPS[tag]
        st = exchange_start(f"gather_{tag}{l}_start", [p[k][l].astype(BF16) for k in keys], scatter=False, after=started[-1])
        started.append(st["token"])

        def get(after):
            owns, lands = exchange_wait(f"gather_{tag}{l}_wait", st, after)
            return {k: full_matrix(k, land, o) for k, land, o in zip(keys, lands, owns)}

        return get

    getters = [{tag: fetch(l, tag) for tag in GROUPS} for l in range(nl)]
    mod = jnp.transpose(mod_recv, (1, 0, 2)).reshape(nl, N_DEV * n_ada) + ada_b + started[-1][0:1, 0:1]
    mods = [mod[l].reshape(6, d) for l in range(nl)]

    sent = {}

    def emitter(l):
        def emit(tag, grads_big):
            keys = GROUPS[tag]
            send = []
            for k in keys:
                gk = grads_big[k]
                if k == "w_in":
                    gk = _w_in_grad_blocks(gk, d)
                elif gk.ndim == 3:
                    pass
                elif k in COL_SHARDED:
                    r, cc = gk.shape[0], gk.shape[1] // N_DEV
                    gk = jnp.transpose(gk.reshape(r, N_DEV, cc), (1, 0, 2))
                else:
                    gk = gk.reshape(N_DEV, gk.shape[0] // N_DEV, gk.shape[1])
                send.append(gk.astype(BF16))
            sent[(l, tag)] = (keys, exchange_start(f"scatter_{tag}{l}_start", send, scatter=True))
            return sent[(l, tag)][1]["token"]

        return emit

    small_sent = {}

    def small_emitter(l):
        def emit_small(dmod_l, g):
            packed = _pack([dmod_l] + [g[k] for k in SMALL[1:]] + [g["conv_w"]])
            small_sent[l] = exchange_start(f"gather_small{l}_start", [packed], scatter=False)
            return small_sent[l]["token"]

        return emit_small

    small_shapes = [p[k].shape[1:] for k in SMALL] + [(CONV_WIDTH, D_MIX)]
    zeros_conv = jnp.zeros(small_shapes[-1], F32)
    packs = [jnp.stack([_pack([src[k][l] for k in SMALL] + [zeros_conv]) for l in range(nl)]) for src in (p, mom, var)]
    smalls = [_layer_small(p, conv_full, l) for l in range(nl)]
    row_major = {"w_in": tuple(lax.reduce_precision(src["w_in"], 8, 23) for src in (p, mom, var))}
    early = packs + [a for sm in smalls for a in sm.values()] + list(row_major["w_in"])
    first_in = getters[0]["in"]
    getters[0]["in"] = lambda after: first_in([after] + early)
    layers = [dict(small=smalls[l], emit_small=small_emitter(l), get_in=getters[l]["in"], get_abco=getters[l]["abco"], get_mlp=getters[l]["mlp"],
                   emit=emitter(l)) for l in range(nl)]
    loss_local, dx, dmods, grads = local_step(x[0], loss_target[0], mods, layers)
    loss = lax.psum(loss_local, ("x", "y", "c"))
    grad_x = dx[None]

    out = {k: [None] * 4 for k in WEIGHTS}

    def shard_update(name, parts, k):
        shp = p[k].shape
        flat = lambda a: a.reshape(nl, -1, shp[-1])
        wk, mk, vk = row_major.get(k, (p[k], mom[k], var[k]))
        res = adamw_sum(name, [pt.reshape((pt.shape[0],) + flat(wk).shape[1:]) for pt in parts], flat(wk), flat(mk), flat(vk))
        out[k] = [a.reshape(shp) for a in res]

    parts = {}

    def collect(tag, after):
        for l in reversed(range(nl)):
            keys, st = sent[(l, tag)]
            sends, lands = exchange_wait(f"scatter_{tag}{l}_wait", st, after)
            for k, land, sd in zip(keys, lands, sends):
                own = lax.dynamic_index_in_dim(sd, me, 0, keepdims=False)
                parts[(k, l)] = lax.dynamic_update_index_in_dim(land, own, me, 0)

    behind_bwd = [dx] + [small_sent[l]["token"] for l in range(nl)]
    collect("mlp", behind_bwd)
    collect("abco", behind_bwd)
    for k in BIG[1:]:
        shard_update("adamw_" + k, [parts[(k, l)] for l in range(nl)], k)
    collect("in", [out[k][0] for k in BIG[1:]])
    shard_update("adamw_w_in", [parts[("w_in", l)] for l in range(nl)], "w_in")

    small_all = []
    for l in range(nl):
        srcs, lands = exchange_wait(f"gather_small{l}_wait", small_sent[l], [out[k][0] for k in BIG])
        small_all.append(lax.dynamic_update_index_in_dim(lands[0], srcs[0], me, 0))
    small_out = [_unpack(o, small_shapes) for o in adamw_sum("adamw_small", small_all, *packs)]
    for i, k in enumerate(SMALL):
        for j in range(4):
            out[k][j] = small_out[j][i]

    n_conv = conv_w.shape[2]
    conv_grad = lax.dynamic_slice_in_dim(small_out[0][-1], me * n_conv, n_conv, axis=2)
    shard_update("adamw_conv_w", [conv_grad[l][None] for l in range(nl)], "conv_w")

    dmod_all = jnp.stack([small_all[l].reshape(N_DEV, -1)[:, :6 * d] for l in range(nl)])
    dmod_mine = lax.dynamic_slice_in_dim(dmod_all, me * n_ada, n_ada, axis=2)
    g_ada = ada_bwd(c_all.T, dmod_mine)
    shard_update("adamw_ada_w", [g_ada[l][None] for l in range(nl)], "ada_w")

    res = [loss, grad_x]
    for j in range(4):
        res += [out[k][j] for k in WEIGHTS]
    return tuple(res)
```
